```python
import math
import jax, jax.numpy as jnp
from jax import lax
import numpy as np

D_MODEL = 1024
BATCH = 8
SEQ = 2048
DEPTH = 4

CONV_DIM = 512
CONV_WIDTH = 31
N_GROUPS = 3
HEADS_PER_GROUP = 8
HEAD_DIM = 64
N_HEADS = N_GROUPS * HEADS_PER_GROUP
ATTN_DIM = N_HEADS * HEAD_DIM
ATTN_OUT_DIM = HEADS_PER_GROUP * HEAD_DIM
WINDOWS = (128, 512, 2048)
DILATIONS = (1, 4, 16)
SUB_WINDOW = 128
BLOCK = 128
NUM_BUCKETS = 32
MAX_REL_DISTANCE = 2048
D_FF = 4 * D_MODEL
EPS = 1e-6
NEG_INF = -1e30
IN_COLS = 2 * CONV_DIM + 3 * ATTN_DIM + 2 * D_MODEL

kernel_name = "hybrid_conformer_conv_dilated_attn_gated"


def rms_norm(x, g):
    xf = x.astype(jnp.float32)
    y = xf * lax.rsqrt(jnp.mean(xf * xf, axis=-1, keepdims=True) + EPS)
    return (y * g.astype(jnp.float32)).astype(x.dtype)


def layer_norm(x, g, b):
    xf = x.astype(jnp.float32)
    mu = jnp.mean(xf, axis=-1, keepdims=True)
    xc = xf - mu
    y = xc * lax.rsqrt(jnp.mean(xc * xc, axis=-1, keepdims=True) + EPS)
    return (y * g.astype(jnp.float32) + b.astype(jnp.float32)).astype(x.dtype)


def t5_bucket(dist):
    max_exact = NUM_BUCKETS // 2
    nf = jnp.maximum(dist, 1).astype(jnp.float32)
    large = max_exact + (jnp.log(nf / max_exact) / math.log(MAX_REL_DISTANCE / max_exact)
                         * (NUM_BUCKETS - max_exact)).astype(jnp.int32)
    large = jnp.minimum(large, NUM_BUCKETS - 1)
    return jnp.where(dist < max_exact, dist, large)


def conformer_conv(u, dw_w, dw_b, ln_g, ln_b, w_pw):
    a, gt = jnp.split(u, 2, axis=-1)
    z = a * jax.nn.sigmoid(gt)
    z = lax.conv_general_dilated(
        z, dw_w[:, None, :].astype(z.dtype), window_strides=(1,),
        padding=((CONV_WIDTH - 1, 0),),
        dimension_numbers=('NWC', 'WIO', 'NWC'),
        feature_group_count=CONV_DIM) + dw_b
    z = jax.nn.silu(layer_norm(z, ln_g, ln_b))
    return z @ w_pw


def dilated_group(q, k, v, bias_g, d):
    B, S, H, Dh = q.shape
    L = S // d
    nb = -(-L // BLOCK)
    Lp = nb * BLOCK

    def to_blocks(t):
        t = t.reshape(B, L, d, H, Dh)
        t = jnp.pad(t, ((0, 0), (0, Lp - L), (0, 0), (0, 0), (0, 0)))
        return t.reshape(B, nb, BLOCK, d, H, Dh)

    def band_keys(t):
        prev = jnp.pad(t, ((0, 0), (1, 0), (0, 0), (0, 0), (0, 0), (0, 0)))[:, :-1]
        return jnp.concatenate([prev, t], axis=2)

    qb = to_blocks(q)
    kw = band_keys(to_blocks(k))
    vw = band_keys(to_blocks(v))
    s = jnp.einsum('bnqrhe,bnkrhe->bnrhqk', qb, kw)

    qi = jnp.arange(BLOCK)[:, None]
    kj = jnp.arange(2 * BLOCK)[None, :]
    off = qi + BLOCK - kj
    band = (off >= 0) & (off <= SUB_WINDOW)
    blk = jnp.arange(nb)[:, None, None]
    valid = band[None] & (blk * BLOCK + kj[None] - BLOCK >= 0)
    bucket = t5_bucket(jnp.clip(off, 0, SUB_WINDOW) * d)
    bias = jnp.transpose(bias_g.astype(jnp.float32)[bucket], (2, 0, 1))

    s = jnp.where(valid[None, :, None, None], s + bias, NEG_INF)
    lse = jax.nn.logsumexp(s, axis=-1)
    p = jnp.exp(s - lse[..., None])
    o = jnp.einsum('bnrhqk,bnkrhe->bnqrhe', p, vw)
    o = o.reshape(B, Lp, d, H, Dh)[:, :L].reshape(B, S, H, Dh)
    lse = jnp.transpose(lse, (0, 1, 4, 2, 3)).reshape(B, Lp, d, H)[:, :L].reshape(B, S, H)
    return o, lse


def dilated_attention(qkv, q_g, k_g, rel_bias):
    B, S, _ = qkv.shape
    q, k, v = jnp.split(qkv.astype(jnp.float32), 3, axis=-1)
    shp = (B, S, N_GROUPS, HEADS_PER_GROUP, HEAD_DIM)
    q = rms_norm(q.reshape(shp), q_g) * (HEAD_DIM ** -0.5)
    k = rms_norm(k.reshape(shp), k_g)
    v = v.reshape(shp)
    outs, lses = [], []
    for g in range(N_GROUPS):
        o, l = dilated_group(q[:, :, g], k[:, :, g], v[:, :, g],
                             rel_bias[:, g * HEADS_PER_GROUP:(g + 1) * HEADS_PER_GROUP], DILATIONS[g])
        outs.append(o)
        lses.append(l)
    w = jax.nn.softmax(jnp.stack(lses, axis=0), axis=0)
    o = jnp.sum(w[..., None] * jnp.stack(outs, axis=0), axis=0)
    return o.reshape(B, S, ATTN_OUT_DIM)


def _fwd_setup_inputs(seed: int = 0) -> dict:
    key = jax.random.key(seed)
    ks = jax.random.split(key, 20)
    f32 = jnp.float32

    def nrm(k, shape, scale):
        return jax.random.normal(k, shape, f32) * scale

    res_scale = (2 * DEPTH) ** -0.5
    return {
        "x": nrm(ks[0], (BATCH, SEQ, D_MODEL), 1.0),
        "rel_bias": nrm(ks[1], (NUM_BUCKETS, N_HEADS), 0.5),
        "norm1_g": 1.0 + nrm(ks[2], (DEPTH, D_MODEL), 0.02),
        "w_in": nrm(ks[3], (DEPTH, D_MODEL, IN_COLS), D_MODEL ** -0.5),
        "q_norm_g": 1.0 + nrm(ks[4], (DEPTH, HEAD_DIM), 0.02),
        "k_norm_g": 1.0 + nrm(ks[5], (DEPTH, HEAD_DIM), 0.02),
        "conv_dw_w": nrm(ks[6], (DEPTH, CONV_WIDTH, CONV_DIM), CONV_WIDTH ** -0.5),
        "conv_dw_b": nrm(ks[7], (DEPTH, CONV_DIM), 0.02),
        "conv_ln_g": 1.0 + nrm(ks[8], (DEPTH, CONV_DIM), 0.02),
        "conv_ln_b": nrm(ks[9], (DEPTH, CONV_DIM), 0.02),
        "w_conv_out": nrm(ks[10], (DEPTH, CONV_DIM, D_MODEL), CONV_DIM ** -0.5),
        "w_attn_out": nrm(ks[11], (DEPTH, ATTN_OUT_DIM, D_MODEL), ATTN_OUT_DIM ** -0.5),
        "w_out": nrm(ks[12], (DEPTH, D_MODEL, D_MODEL), D_MODEL ** -0.5 * res_scale),
        "norm2_g": 1.0 + nrm(ks[13], (DEPTH, D_MODEL), 0.02),
        "w_ff1": nrm(ks[14], (DEPTH, D_MODEL, D_FF), D_MODEL ** -0.5),
        "w_ff2": nrm(ks[15], (DEPTH, D_FF, D_MODEL), D_FF ** -0.5 * res_scale),
    }


def _fwd_reference(x, rel_bias, norm1_g, w_in, q_norm_g, k_norm_g, conv_dw_w, conv_dw_b,
              conv_ln_g, conv_ln_b, w_conv_out, w_attn_out, w_out, norm2_g, w_ff1, w_ff2):
    c_conv = 2 * CONV_DIM
    c_attn = c_conv + 3 * ATTN_DIM
    for l in range(DEPTH):
        h = rms_norm(x, norm1_g[l])
        u = h @ w_in[l]
        y_conv = conformer_conv(u[..., :c_conv], conv_dw_w[l], conv_dw_b[l],
                                conv_ln_g[l], conv_ln_b[l], w_conv_out[l])
        y_attn = dilated_attention(u[..., c_conv:c_attn], q_norm_g[l], k_norm_g[l],
                                   rel_bias).astype(x.dtype) @ w_attn_out[l]
        g_conv, g_attn = jnp.split(jax.nn.sigmoid(u[..., c_attn:]), 2, axis=-1)
        x = x + (g_conv * y_conv + g_attn * y_attn) @ w_out[l]
        h = rms_norm(x, norm2_g[l])
        x = x + jnp.square(jax.nn.relu(h @ w_ff1[l])) @ w_ff2[l]
    return x


import jax as _jax
import jax.numpy as _jnp

TWIN_FORMAT = 'train_step'
FWD_PARAMS = ['x', 'rel_bias', 'norm1_g', 'w_in', 'q_norm_g', 'k_norm_g', 'conv_dw_w', 'conv_dw_b', 'conv_ln_g', 'conv_ln_b', 'w_conv_out', 'w_attn_out', 'w_out', 'norm2_g', 'w_ff1', 'w_ff2']
TWIN_WEIGHTS = ['rel_bias', 'norm1_g', 'w_in', 'q_norm_g', 'k_norm_g', 'conv_dw_w', 'conv_dw_b', 'conv_ln_g', 'conv_ln_b', 'w_conv_out', 'w_attn_out', 'w_out', 'norm2_g', 'w_ff1', 'w_ff2']
TWIN_DIFF_INPUT = 'x'
TWIN_INPUTS = ['x', 'rel_bias', 'norm1_g', 'w_in', 'q_norm_g', 'k_norm_g', 'conv_dw_w', 'conv_dw_b', 'conv_ln_g', 'conv_ln_b', 'w_conv_out', 'w_attn_out', 'w_out', 'norm2_g', 'w_ff1', 'w_ff2', 'loss_target', 'm_rel_bias', 'm_norm1_g', 'm_w_in', 'm_q_norm_g', 'm_k_norm_g', 'm_conv_dw_w', 'm_conv_dw_b', 'm_conv_ln_g', 'm_conv_ln_b', 'm_w_conv_out', 'm_w_attn_out', 'm_w_out', 'm_norm2_g', 'm_w_ff1', 'm_w_ff2', 'v_rel_bias', 'v_norm1_g', 'v_w_in', 'v_q_norm_g', 'v_k_norm_g', 'v_conv_dw_w', 'v_conv_dw_b', 'v_conv_ln_g', 'v_conv_ln_b', 'v_w_conv_out', 'v_w_attn_out', 'v_w_out', 'v_norm2_g', 'v_w_ff1', 'v_w_ff2']
TWIN_OUTPUTS = ['loss', 'grad_x', 'grad_rel_bias', 'grad_norm1_g', 'grad_w_in', 'grad_q_norm_g', 'grad_k_norm_g', 'grad_conv_dw_w', 'grad_conv_dw_b', 'grad_conv_ln_g', 'grad_conv_ln_b', 'grad_w_conv_out', 'grad_w_attn_out', 'grad_w_out', 'grad_norm2_g', 'grad_w_ff1', 'grad_w_ff2', 'delta_rel_bias', 'delta_norm1_g', 'delta_w_in', 'delta_q_norm_g', 'delta_k_norm_g', 'delta_conv_dw_w', 'delta_conv_dw_b', 'delta_conv_ln_g', 'delta_conv_ln_b', 'delta_w_conv_out', 'delta_w_attn_out', 'delta_w_out', 'delta_norm2_g', 'delta_w_ff1', 'delta_w_ff2', 'new_m_rel_bias', 'new_m_norm1_g', 'new_m_w_in', 'new_m_q_norm_g', 'new_m_k_norm_g', 'new_m_conv_dw_w', 'new_m_conv_dw_b', 'new_m_conv_ln_g', 'new_m_conv_ln_b', 'new_m_w_conv_out', 'new_m_w_attn_out', 'new_m_w_out', 'new_m_norm2_g', 'new_m_w_ff1', 'new_m_w_ff2', 'new_v_rel_bias', 'new_v_norm1_g', 'new_v_w_in', 'new_v_q_norm_g', 'new_v_k_norm_g', 'new_v_conv_dw_w', 'new_v_conv_dw_b', 'new_v_conv_ln_g', 'new_v_conv_ln_b', 'new_v_w_conv_out', 'new_v_w_attn_out', 'new_v_w_out', 'new_v_norm2_g', 'new_v_w_ff1', 'new_v_w_ff2']
TWIN_LEAF_KINDS = {'loss': 'loss', 'grad_x': 'grad_x', 'grad_rel_bias': 'grad_w', 'grad_norm1_g': 'grad_w', 'grad_w_in': 'grad_w', 'grad_q_norm_g': 'grad_w', 'grad_k_norm_g': 'grad_w', 'grad_conv_dw_w': 'grad_w', 'grad_conv_dw_b': 'grad_w', 'grad_conv_ln_g': 'grad_w', 'grad_conv_ln_b': 'grad_w', 'grad_w_conv_out': 'grad_w', 'grad_w_attn_out': 'grad_w', 'grad_w_out': 'grad_w', 'grad_norm2_g': 'grad_w', 'grad_w_ff1': 'grad_w', 'grad_w_ff2': 'grad_w', 'delta_rel_bias': 'delta_w', 'delta_norm1_g': 'delta_w', 'delta_w_in': 'delta_w', 'delta_q_norm_g': 'delta_w', 'delta_k_norm_g': 'delta_w', 'delta_conv_dw_w': 'delta_w', 'delta_conv_dw_b': 'delta_w', 'delta_conv_ln_g': 'delta_w', 'delta_conv_ln_b': 'delta_w', 'delta_w_conv_out': 'delta_w', 'delta_w_attn_out': 'delta_w', 'delta_w_out': 'delta_w', 'delta_norm2_g': 'delta_w', 'delta_w_ff1': 'delta_w', 'delta_w_ff2': 'delta_w', 'new_m_rel_bias': 'new_m', 'new_m_norm1_g': 'new_m', 'new_m_w_in': 'new_m', 'new_m_q_norm_g': 'new_m', 'new_m_k_norm_g': 'new_m', 'new_m_conv_dw_w': 'new_m', 'new_m_conv_dw_b': 'new_m', 'new_m_conv_ln_g': 'new_m', 'new_m_conv_ln_b': 'new_m', 'new_m_w_conv_out': 'new_m', 'new_m_w_attn_out': 'new_m', 'new_m_w_out': 'new_m', 'new_m_norm2_g': 'new_m', 'new_m_w_ff1': 'new_m', 'new_m_w_ff2': 'new_m', 'new_v_rel_bias': 'new_v', 'new_v_norm1_g': 'new_v', 'new_v_w_in': 'new_v', 'new_v_q_norm_g': 'new_v', 'new_v_k_norm_g': 'new_v', 'new_v_conv_dw_w': 'new_v', 'new_v_conv_dw_b': 'new_v', 'new_v_conv_ln_g': 'new_v', 'new_v_conv_ln_b': 'new_v', 'new_v_w_conv_out': 'new_v', 'new_v_w_attn_out': 'new_v', 'new_v_w_out': 'new_v', 'new_v_norm2_g': 'new_v', 'new_v_w_ff1': 'new_v', 'new_v_w_ff2': 'new_v'}


def _forward(args):
    return _fwd_reference(*[args[k] for k in FWD_PARAMS])


def _output_shape():
    out = _jax.eval_shape(lambda: _forward(_fwd_setup_inputs(0)))
    return out.shape, out.dtype

N_MICROBATCH = 1
ADAM_LR = 0.001
ADAM_B1 = 0.9
ADAM_B2 = 0.999
ADAM_EPS = 1e-08
ADAM_WD = 0.01
ADAM_STEP = 10
PER_EXAMPLE_BATCH_AXIS = {'x': 0, 'loss_target': 0}
SHARED_INPUTS = []
_WEIGHT_DTYPES = {'rel_bias': _jnp.float32, 'norm1_g': _jnp.float32, 'w_in': _jnp.float32, 'q_norm_g': _jnp.float32, 'k_norm_g': _jnp.float32, 'conv_dw_w': _jnp.float32, 'conv_dw_b': _jnp.float32, 'conv_ln_g': _jnp.float32, 'conv_ln_b': _jnp.float32, 'w_conv_out': _jnp.float32, 'w_attn_out': _jnp.float32, 'w_out': _jnp.float32, 'norm2_g': _jnp.float32, 'w_ff1': _jnp.float32, 'w_ff2': _jnp.float32}
MOMENT_SCALE = {'rel_bias': 1.006275e-01, 'norm1_g': 1.748144e-01, 'w_in': 5.936396e-02, 'q_norm_g': 7.831186e-02, 'k_norm_g': 7.806411e-02, 'conv_dw_w': 1.673301e-01, 'conv_dw_b': 1.542262e+00, 'conv_ln_g': 8.048565e-01, 'conv_ln_b': 9.219990e-01, 'w_conv_out': 2.752724e-01, 'w_attn_out': 1.225843e-01, 'w_out': 8.296293e-01, 'norm2_g': 5.988943e+00, 'w_ff1': 2.441526e-01, 'w_ff2': 3.405430e+00}


def _to_microbatches(a, axis):
    t = _jnp.moveaxis(a, axis, 0)
    t = t.reshape((N_MICROBATCH, t.shape[0] // N_MICROBATCH) + t.shape[1:])
    return _jnp.moveaxis(t, 1, axis + 1)


def setup_inputs(seed: int = 0) -> dict:
    inp = _fwd_setup_inputs(seed)
    key = _jax.random.fold_in(_jax.random.key(seed), 7919)
    shape, _ = _output_shape()
    out = dict(inp)
    out["loss_target"] = _jax.random.normal(_jax.random.fold_in(key, 0), shape, _jnp.float32)
    for i, name in enumerate(TWIN_WEIGHTS):
        w = inp[name].astype(_jnp.float32)
        if MOMENT_SCALE is None:
            s = _jnp.sqrt(_jnp.mean(_jnp.square(w)) + 1e-30)
        else:
            s = MOMENT_SCALE[name]
        km, kv = _jax.random.split(_jax.random.fold_in(key, i + 1))
        out[name] = w
        out["m_" + name] = s * _jax.random.normal(km, w.shape, _jnp.float32)
        out["v_" + name] = (s * s) * _jax.random.uniform(kv, w.shape, _jnp.float32, 0.5, 1.5)
    if N_MICROBATCH > 1:
        for name, axis in PER_EXAMPLE_BATCH_AXIS.items():
            out[name] = _to_microbatches(out[name], axis)
    return {'x': out['x'], 'rel_bias': out['rel_bias'], 'norm1_g': out['norm1_g'], 'w_in': out['w_in'], 'q_norm_g': out['q_norm_g'], 'k_norm_g': out['k_norm_g'], 'conv_dw_w': out['conv_dw_w'], 'conv_dw_b': out['conv_dw_b'], 'conv_ln_g': out['conv_ln_g'], 'conv_ln_b': out['conv_ln_b'], 'w_conv_out': out['w_conv_out'], 'w_attn_out': out['w_attn_out'], 'w_out': out['w_out'], 'norm2_g': out['norm2_g'], 'w_ff1': out['w_ff1'], 'w_ff2': out['w_ff2'], 'loss_target': out['loss_target'], 'm_rel_bias': out['m_rel_bias'], 'm_norm1_g': out['m_norm1_g'], 'm_w_in': out['m_w_in'], 'm_q_norm_g': out['m_q_norm_g'], 'm_k_norm_g': out['m_k_norm_g'], 'm_conv_dw_w': out['m_conv_dw_w'], 'm_conv_dw_b': out['m_conv_dw_b'], 'm_conv_ln_g': out['m_conv_ln_g'], 'm_conv_ln_b': out['m_conv_ln_b'], 'm_w_conv_out': out['m_w_conv_out'], 'm_w_attn_out': out['m_w_attn_out'], 'm_w_out': out['m_w_out'], 'm_norm2_g': out['m_norm2_g'], 'm_w_ff1': out['m_w_ff1'], 'm_w_ff2': out['m_w_ff2'], 'v_rel_bias': out['v_rel_bias'], 'v_norm1_g': out['v_norm1_g'], 'v_w_in': out['v_w_in'], 'v_q_norm_g': out['v_q_norm_g'], 'v_k_norm_g': out['v_k_norm_g'], 'v_conv_dw_w': out['v_conv_dw_w'], 'v_conv_dw_b': out['v_conv_dw_b'], 'v_conv_ln_g': out['v_conv_ln_g'], 'v_conv_ln_b': out['v_conv_ln_b'], 'v_w_conv_out': out['v_w_conv_out'], 'v_w_attn_out': out['v_w_attn_out'], 'v_w_out': out['v_w_out'], 'v_norm2_g': out['v_norm2_g'], 'v_w_ff1': out['v_w_ff1'], 'v_w_ff2': out['v_w_ff2']}


def _loss(weights, diff, rest, loss_target):
    with _jax.named_scope("forward"):
        args = {**rest, TWIN_DIFF_INPUT: diff, **{k: w.astype(_WEIGHT_DTYPES[k]) for k, w in weights.items()}}
        y = _forward(args)
    with _jax.named_scope("loss_head"):
        err = _jnp.square(y.astype(_jnp.float32) - loss_target)
        return 0.5 * _jnp.sum(_jnp.mean(err, axis=-1)) if err.ndim else 0.5 * err


def _adamw(w, g, m, v):
    m = ADAM_B1 * m + (1.0 - ADAM_B1) * g
    v = ADAM_B2 * v + (1.0 - ADAM_B2) * _jnp.square(g)
    m_hat = m / (1.0 - ADAM_B1 ** ADAM_STEP)
    v_hat = v / (1.0 - ADAM_B2 ** ADAM_STEP)
    delta = -ADAM_LR * (m_hat / (_jnp.sqrt(v_hat) + ADAM_EPS) + ADAM_WD * w)
    return delta, m, v


def reference(x, rel_bias, norm1_g, w_in, q_norm_g, k_norm_g, conv_dw_w, conv_dw_b, conv_ln_g, conv_ln_b, w_conv_out, w_attn_out, w_out, norm2_g, w_ff1, w_ff2, loss_target, m_rel_bias, m_norm1_g, m_w_in, m_q_norm_g, m_k_norm_g, m_conv_dw_w, m_conv_dw_b, m_conv_ln_g, m_conv_ln_b, m_w_conv_out, m_w_attn_out, m_w_out, m_norm2_g, m_w_ff1, m_w_ff2, v_rel_bias, v_norm1_g, v_w_in, v_q_norm_g, v_k_norm_g, v_conv_dw_w, v_conv_dw_b, v_conv_ln_g, v_conv_ln_b, v_w_conv_out, v_w_attn_out, v_w_out, v_norm2_g, v_w_ff1, v_w_ff2):
    given = dict(x=x, rel_bias=rel_bias, norm1_g=norm1_g, w_in=w_in, q_norm_g=q_norm_g, k_norm_g=k_norm_g, conv_dw_w=conv_dw_w, conv_dw_b=conv_dw_b, conv_ln_g=conv_ln_g, conv_ln_b=conv_ln_b, w_conv_out=w_conv_out, w_attn_out=w_attn_out, w_out=w_out, norm2_g=norm2_g, w_ff1=w_ff1, w_ff2=w_ff2, loss_target=loss_target, m_rel_bias=m_rel_bias, m_norm1_g=m_norm1_g, m_w_in=m_w_in, m_q_norm_g=m_q_norm_g, m_k_norm_g=m_k_norm_g, m_conv_dw_w=m_conv_dw_w, m_conv_dw_b=m_conv_dw_b, m_conv_ln_g=m_conv_ln_g, m_conv_ln_b=m_conv_ln_b, m_w_conv_out=m_w_conv_out, m_w_attn_out=m_w_attn_out, m_w_out=m_w_out, m_norm2_g=m_norm2_g, m_w_ff1=m_w_ff1, m_w_ff2=m_w_ff2, v_rel_bias=v_rel_bias, v_norm1_g=v_norm1_g, v_w_in=v_w_in, v_q_norm_g=v_q_norm_g, v_k_norm_g=v_k_norm_g, v_conv_dw_w=v_conv_dw_w, v_conv_dw_b=v_conv_dw_b, v_conv_ln_g=v_conv_ln_g, v_conv_ln_b=v_conv_ln_b, v_w_conv_out=v_w_conv_out, v_w_attn_out=v_w_attn_out, v_w_out=v_w_out, v_norm2_g=v_norm2_g, v_w_ff1=v_w_ff1, v_w_ff2=v_w_ff2)
    weights = {n: given[n] for n in TWIN_WEIGHTS}
    shared = {n: given[n] for n in SHARED_INPUTS}
    per_example = {n: given[n] for n in ['x']}
    grad_fn = _jax.value_and_grad(_loss, argnums=(0, 1))

    def one_microbatch(ex, loss_target):
        ex = dict(ex)
        diff = ex.pop(TWIN_DIFF_INPUT)
        return grad_fn(weights, diff, {**shared, **ex}, loss_target)

    if N_MICROBATCH == 1:
        loss, (grad_w, grad_x) = one_microbatch(per_example, given["loss_target"])
    else:
        def body(carry, xs):
            loss_sum, grad_sum = carry
            l_k, (gw_k, gx_k) = one_microbatch(xs[0], xs[1])
            with _jax.named_scope("update"):
                return (loss_sum + l_k, _jax.tree.map(_jnp.add, grad_sum, gw_k)), gx_k

        init = (_jnp.zeros((), _jnp.float32), _jax.tree.map(_jnp.zeros_like, weights))
        (loss, grad_w), grad_x = _jax.lax.scan(body, init, (per_example, given["loss_target"]))
    with _jax.named_scope("update"):
        delta_w, new_m, new_v = {}, {}, {}
        for n in TWIN_WEIGHTS:
            delta_w[n], new_m[n], new_v[n] = _adamw(weights[n], grad_w[n], given["m_" + n], given["v_" + n])
    return (loss, grad_x, *[grad_w[n] for n in TWIN_WEIGHTS], *[delta_w[n] for n in TWIN_WEIGHTS],
            *[new_m[n] for n in TWIN_WEIGHTS], *[new_v[n] for n in TWIN_WEIGHTS])
```

```python
import functools
import math

import numpy as np
import jax
import jax.numpy as jnp
from jax import lax
from jax.experimental import pallas as pl
from jax.experimental.pallas import tpu as pltpu

F32 = jnp.float32
BF16 = jnp.bfloat16

HEAD_DIM = 64
N_GROUPS = 3
DILATIONS = (1, 4, 16)
SUB_WINDOW = 128
BLOCK = 128
CONV_WIDTH = 31
CONV_TAPS_PADDED = 32
NUM_BUCKETS = 32
MAX_REL_DISTANCE = 2048
EPS = 1e-6
NEG_INF = -1e30
LANES = 128

ADAM_LR = 0.001
ADAM_B1 = 0.9
ADAM_B2 = 0.999
ADAM_EPS = 1e-08
ADAM_WD = 0.01
ADAM_STEP = 10

N_DEV = 8
VMEM_LIMIT = 56 * 1024 * 1024
MESH = pl.DeviceIdType.MESH
HIGHEST = lax.Precision.HIGHEST


def _cparams(sem=None):
    return pltpu.CompilerParams(dimension_semantics=sem, vmem_limit_bytes=VMEM_LIMIT)


def _tile(n, target):
    if n <= target:
        return n
    t = (target // LANES) * LANES
    while t >= LANES:
        if n % t == 0:
            return t
        t -= LANES
    return n


def _sigmoid(v):
    return 1.0 / (1.0 + jnp.exp(-v))


def _mm(a, b, *, ta=False, tb=False, out_dtype=F32, epi=None, extra=None, name, tm=1024, tn=1024, tk=512):
    m = a.shape[1] if ta else a.shape[0]
    kdim = a.shape[0] if ta else a.shape[1]
    n = b.shape[0] if tb else b.shape[1]
    tm, tn, tk = _tile(m, tm), _tile(n, tn), _tile(kdim, tk)
    nk = kdim // tk
    a_spec = pl.BlockSpec((tk, tm), lambda i, j, k: (k, i)) if ta else pl.BlockSpec((tm, tk), lambda i, j, k: (i, k))
    b_spec = pl.BlockSpec((tn, tk), lambda i, j, k: (j, k)) if tb else pl.BlockSpec((tk, tn), lambda i, j, k: (k, j))
    o_spec = pl.BlockSpec((tm, tn), lambda i, j, k: (i, j))
    dims = (((0 if ta else 1,), (1 if tb else 0,)), ((), ()))
    n_in = 3 if extra is not None else 2
    n_out = 2 if epi == "relu2" else 1

    def body(*refs):
        a_ref, b_ref = refs[0], refs[1]
        e_ref = refs[2] if extra is not None else None
        outs = refs[n_in:n_in + n_out]
        acc_ref = refs[-1]
        k = pl.program_id(2)

        @pl.when(k == 0)
        def _():
            acc_ref[...] = jnp.zeros_like(acc_ref)

        acc_ref[...] += lax.dot_general(a_ref[...].astype(BF16), b_ref[...].astype(BF16), dims,
                                        preferred_element_type=F32)

        @pl.when(k == nk - 1)
        def _():
            acc = acc_ref[...]
            if epi is None:
                outs[0][...] = acc.astype(outs[0].dtype)
            elif epi == "res":
                outs[0][...] = (e_ref[...] + acc).astype(outs[0].dtype)
            elif epi == "relu2":
                outs[0][...] = acc
                r = jnp.maximum(acc, 0.0)
                outs[1][...] = (r * r).astype(BF16)
            elif epi == "drelu2":
                outs[0][...] = (acc * (2.0 * jnp.maximum(e_ref[...], 0.0))).astype(outs[0].dtype)

    in_specs = [a_spec, b_spec] + ([o_spec] if extra is not None else [])
    if epi == "relu2":
        out_shape = (jax.ShapeDtypeStruct((m, n), F32), jax.ShapeDtypeStruct((m, n), BF16))
        out_specs = (o_spec, o_spec)
    else:
        out_shape = jax.ShapeDtypeStruct((m, n), out_dtype)
        out_specs = o_spec
    args = (a, b) + ((extra,) if extra is not None else ())
    return pl.pallas_call(
        body, name=name, grid=(m // tm, n // tn, nk), in_specs=in_specs, out_specs=out_specs, out_shape=out_shape,
        scratch_shapes=[pltpu.VMEM((tm, tn), F32)],
        compiler_params=_cparams(("parallel", "parallel", "arbitrary")),
    )(*args)


ROWS = 256


def _rms_fwd(x, g, name):
    t, d = x.shape

    def body(x_ref, g_ref, h_ref):
        xv = x_ref[...]
        r = lax.rsqrt(jnp.mean(xv * xv, axis=-1, keepdims=True) + EPS)
        h_ref[...] = (xv * r * g_ref[...]).astype(BF16)

    return pl.pallas_call(
        body, name=name, grid=(t // ROWS,),
        in_specs=[pl.BlockSpec((ROWS, d), lambda i: (i, 0)), pl.BlockSpec((1, d), lambda i: (0, 0))],
        out_specs=pl.BlockSpec((ROWS, d), lambda i: (i, 0)),
        out_shape=jax.ShapeDtypeStruct((t, d), BF16), compiler_params=_cparams(("arbitrary",)),
    )(x, g.reshape(1, d))


def _rms_bwd(dh, x, g, dres, name):
    t, d = x.shape

    def body(dh_ref, x_ref, g_ref, dres_ref, dx_ref, dg_ref):
        @pl.when(pl.program_id(0) == 0)
        def _():
            dg_ref[...] = jnp.zeros_like(dg_ref)

        xv = x_ref[...]
        dhv = dh_ref[...]
        r = lax.rsqrt(jnp.mean(xv * xv, axis=-1, keepdims=True) + EPS)
        xh = xv * r
        dg_ref[...] += jnp.sum(dhv * xh, axis=0, keepdims=True)
        dxh = dhv * g_ref[...]
        dx_ref[...] = dres_ref[...] + r * (dxh - xh * jnp.mean(dxh * xh, axis=-1, keepdims=True))

    row = pl.BlockSpec((ROWS, d), lambda i: (i, 0))
    vec = pl.BlockSpec((1, d), lambda i: (0, 0))
    return pl.pallas_call(
        body, name=name, grid=(t // ROWS,), in_specs=[row, row, vec, row], out_specs=(row, vec),
        out_shape=(jax.ShapeDtypeStruct((t, d), F32), jax.ShapeDtypeStruct((1, d), F32)),
        compiler_params=_cparams(("arbitrary",)),
    )(dh, x, g.reshape(1, d), dres)


def _gate_fwd(u, yc, ya, gate_col, name):
    t, d = yc.shape
    td = math.gcd(_tile(d, 512), gate_col)
    nd = d // td
    c0 = gate_col // td

    def body(gc_ref, ga_ref, yc_ref, ya_ref, m_ref):
        m_ref[...] = (_sigmoid(gc_ref[...]) * yc_ref[...] + _sigmoid(ga_ref[...]) * ya_ref[...]).astype(BF16)

    blk = pl.BlockSpec((ROWS, td), lambda i, j: (i, j))
    return pl.pallas_call(
        body, name=name, grid=(t // ROWS, nd),
        in_specs=[pl.BlockSpec((ROWS, td), lambda i, j: (i, c0 + j)),
                  pl.BlockSpec((ROWS, td), lambda i, j: (i, c0 + nd + j)), blk, blk],
        out_specs=blk, out_shape=jax.ShapeDtypeStruct((t, d), BF16),
        compiler_params=_cparams(("arbitrary", "arbitrary")),
    )(u, u, yc, ya)


def _gate_bwd(dm, u, yc, ya, gate_col, name):
    t, d = yc.shape
    td = math.gcd(_tile(d, 512), gate_col)
    nd = d // td
    c0 = gate_col // td

    def body(dm_ref, gc_ref, ga_ref, yc_ref, ya_ref, dyc_ref, dya_ref, dugc_ref, duga_ref):
        dmv = dm_ref[...]
        gc = _sigmoid(gc_ref[...])
        ga = _sigmoid(ga_ref[...])
        dyc_ref[...] = (dmv * gc).astype(BF16)
        dya_ref[...] = (dmv * ga).astype(BF16)
        dugc_ref[...] = (dmv * yc_ref[...] * gc * (1.0 - gc)).astype(BF16)
        duga_ref[...] = (dmv * ya_ref[...] * ga * (1.0 - ga)).astype(BF16)

    blk = pl.BlockSpec((ROWS, td), lambda i, j: (i, j))
    o = jax.ShapeDtypeStruct((t, d), BF16)
    return pl.pallas_call(
        body, name=name, grid=(t // ROWS, nd),
        in_specs=[blk, pl.BlockSpec((ROWS, td), lambda i, j: (i, c0 + j)),
                  pl.BlockSpec((ROWS, td), lambda i, j: (i, c0 + nd + j)), blk, blk],
        out_specs=(blk, blk, blk, blk), out_shape=(o, o, o, o),
        compiler_params=_cparams(("arbitrary", "arbitrary")),
    )(dm, u, u, yc, ya)


def _loss_and_grad(y, target, name):
    t, d = y.shape
    n = t // ROWS

    def body(y_ref, t_ref, loss_ref, dy_ref, acc_ref):
        i = pl.program_id(0)

        @pl.when(i == 0)
        def _():
            acc_ref[...] = jnp.zeros_like(acc_ref)

        diff = y_ref[...] - t_ref[...]
        dy_ref[...] = diff * (1.0 / d)
        acc_ref[...] += jnp.sum(diff * diff, axis=0, keepdims=True)

        @pl.when(i == n - 1)
        def _():
            loss_ref[...] = jnp.sum(acc_ref[...], axis=-1, keepdims=True) * (0.5 / d)

    row = pl.BlockSpec((ROWS, d), lambda i: (i, 0))
    return pl.pallas_call(
        body, name=name, grid=(n,), in_specs=[row, row],
        out_specs=(pl.BlockSpec((1, 1), lambda i: (0, 0)), row),
        out_shape=(jax.ShapeDtypeStruct((1, 1), F32), jax.ShapeDtypeStruct((t, d), F32)),
        scratch_shapes=[pltpu.VMEM((1, d), F32)], compiler_params=_cparams(("arbitrary",)),
    )(y, target)


HALO = 32


def _conv_fwd(u, w, b, cdim, name):
    t = u.shape[0]
    ncb = cdim // LANES
    nt = t // BLOCK

    def body(a_ref, g_ref, w_ref, b_ref, zc_ref, zpad):
        zpad[0:HALO, :] = jnp.zeros((HALO, LANES), F32)
        zpad[HALO:HALO + t, :] = a_ref[...] * _sigmoid(g_ref[...])
        wv = w_ref[...]
        bv = b_ref[...]

        def tile(i, carry):
            r0 = pl.multiple_of(i * BLOCK, BLOCK)
            win = zpad[pl.ds(r0, BLOCK + HALO), :]
            acc = jnp.zeros((BLOCK, LANES), F32) + bv
            for j in range(CONV_WIDTH):
                s = CONV_WIDTH - 1 - j
                sh = win if s == 0 else pltpu.roll(win, s, 0)
                acc = acc + wv[j:j + 1, :] * sh[HALO:HALO + BLOCK, :]
            zc_ref[pl.ds(r0, BLOCK), :] = acc
            return carry

        lax.fori_loop(0, nt, tile, 0)

    col = lambda off: pl.BlockSpec((t, LANES), lambda c: (0, off + c))
    return pl.pallas_call(
        body, name=name, grid=(ncb,),
        in_specs=[col(0), col(ncb), pl.BlockSpec((CONV_TAPS_PADDED, LANES), lambda c: (0, c)),
                  pl.BlockSpec((1, LANES), lambda c: (0, c))],
        out_specs=pl.BlockSpec((t, LANES), lambda c: (0, c)),
        out_shape=jax.ShapeDtypeStruct((t, cdim), F32),
        scratch_shapes=[pltpu.VMEM((t + HALO, LANES), F32)], compiler_params=_cparams(("arbitrary",)),
    )(u, u, w, b.reshape(1, cdim))


def _conv_bwd(dzc, u, w, cdim, name):
    t = u.shape[0]
    ncb = cdim // LANES
    nt = t // BLOCK
    win_rows = BLOCK + HALO

    def body(dzc_ref, a_ref, g_ref, w_ref, da_ref, dg_ref, dw_ref, db_ref, zpad, dpad):
        av = a_ref[...]
        sg = _sigmoid(g_ref[...])
        zpad[0:HALO, :] = jnp.zeros((HALO, LANES), F32)
        zpad[HALO:HALO + t, :] = av * sg
        dpad[0:t, :] = dzc_ref[...]
        dpad[t:t + HALO, :] = jnp.zeros((HALO, LANES), F32)
        dw_ref[...] = jnp.zeros_like(dw_ref)
        db_ref[...] = jnp.sum(dzc_ref[...], axis=0, keepdims=True)
        wv = w_ref[...]

        def tile(i, carry):
            r0 = pl.multiple_of(i * BLOCK, BLOCK)
            zwin = zpad[pl.ds(r0, win_rows), :]
            dwin = dpad[pl.ds(r0, win_rows), :]
            dcur = dwin[0:BLOCK, :]
            dz = jnp.zeros((BLOCK, LANES), F32)
            for j in range(CONV_WIDTH):
                s = CONV_WIDTH - 1 - j
                zs = zwin if s == 0 else pltpu.roll(zwin, s, 0)
                dw_ref[j:j + 1, :] += jnp.sum(dcur * zs[HALO:HALO + BLOCK, :], axis=0, keepdims=True)
                ds = dwin if s == 0 else pltpu.roll(dwin, win_rows - s, 0)
                dz = dz + wv[j:j + 1, :] * ds[0:BLOCK, :]
            ac = a_ref[pl.ds(r0, BLOCK), :]
            sc = _sigmoid(g_ref[pl.ds(r0, BLOCK), :])
            da_ref[pl.ds(r0, BLOCK), :] = (dz * sc).astype(BF16)
            dg_ref[pl.ds(r0, BLOCK), :] = (dz * ac * sc * (1.0 - sc)).astype(BF16)
            return carry

        lax.fori_loop(0, nt, tile, 0)

    col = lambda off: pl.BlockSpec((t, LANES), lambda c: (0, off + c))
    wspec = pl.BlockSpec((CONV_TAPS_PADDED, LANES), lambda c: (0, c))
    o = jax.ShapeDtypeStruct((t, cdim), BF16)
    return pl.pallas_call(
        body, name=name, grid=(ncb,), in_specs=[col(0), col(0), col(ncb), wspec],
        out_specs=(col(0), col(0), wspec, pl.BlockSpec((1, LANES), lambda c: (0, c))),
        out_shape=(o, o, jax.ShapeDtypeStruct((CONV_TAPS_PADDED, cdim), F32), jax.ShapeDtypeStruct((1, cdim), F32)),
        scratch_shapes=[pltpu.VMEM((t + HALO, LANES), F32), pltpu.VMEM((t + HALO, LANES), F32)],
        compiler_params=_cparams(("arbitrary",)),
    )(dzc, u, u, w)


def _ln_swish_fwd(zc, g, b, name):
    t, c = zc.shape

    def body(z_ref, g_ref, b_ref, o_ref):
        z = z_ref[...]
        mu = jnp.mean(z, axis=-1, keepdims=True)
        zc_ = z - mu
        zn = zc_ * lax.rsqrt(jnp.mean(zc_ * zc_, axis=-1, keepdims=True) + EPS)
        y = zn * g_ref[...] + b_ref[...]
        o_ref[...] = (y * _sigmoid(y)).astype(BF16)

    row = pl.BlockSpec((ROWS, c), lambda i: (i, 0))
    vec = pl.BlockSpec((1, c), lambda i: (0, 0))
    return pl.pallas_call(
        body, name=name, grid=(t // ROWS,), in_specs=[row, vec, vec], out_specs=row,
        out_shape=jax.ShapeDtypeStruct((t, c), BF16), compiler_params=_cparams(("arbitrary",)),
    )(zc, g.reshape(1, c), b.reshape(1, c))


def _ln_swish_bwd(dzs, zc, g, b, name):
    t, c = zc.shape

    def body(d_ref, z_ref, g_ref, b_ref, dz_ref, dg_ref, db_ref):
        @pl.when(pl.program_id(0) == 0)
        def _():
            dg_ref[...] = jnp.zeros_like(dg_ref)
            db_ref[...] = jnp.zeros_like(db_ref)

        z = z_ref[...]
        mu = jnp.mean(z, axis=-1, keepdims=True)
        zc_ = z - mu
        rstd = lax.rsqrt(jnp.mean(zc_ * zc_, axis=-1, keepdims=True) + EPS)
        zn = zc_ * rstd
        y = zn * g_ref[...] + b_ref[...]
        sg = _sigmoid(y)
        dy = d_ref[...] * (sg * (1.0 + y * (1.0 - sg)))
        dg_ref[...] += jnp.sum(dy * zn, axis=0, keepdims=True)
        db_ref[...] += jnp.sum(dy, axis=0, keepdims=True)
        dzn = dy * g_ref[...]
        dz_ref[...] = rstd * (dzn - jnp.mean(dzn, axis=-1, keepdims=True)
                              - zn * jnp.mean(dzn * zn, axis=-1, keepdims=True))

    row = pl.BlockSpec((ROWS, c), lambda i: (i, 0))
    vec = pl.BlockSpec((1, c), lambda i: (0, 0))
    v = jax.ShapeDtypeStruct((1, c), F32)
    return pl.pallas_call(
        body, name=name, grid=(t // ROWS,), in_specs=[row, row, vec, vec], out_specs=(row, vec, vec),
        out_shape=(jax.ShapeDtypeStruct((t, c), F32), v, v), compiler_params=_cparams(("arbitrary",)),
    )(dzs, zc, g.reshape(1, c), b.reshape(1, c))


def _bucket_table():
    qi = np.arange(BLOCK)[:, None]
    kj = np.arange(2 * BLOCK)[None, :]
    off = qi + BLOCK - kj
    band = (off >= 0) & (off <= SUB_WINDOW)
    max_exact = NUM_BUCKETS // 2
    out = []
    for d in DILATIONS:
        dist = (np.clip(off, 0, SUB_WINDOW) * d).astype(np.int32)
        nf = np.maximum(dist, 1).astype(np.float32)
        large = max_exact + (np.log(nf / np.float32(max_exact)) / np.float32(math.log(MAX_REL_DISTANCE / max_exact))
                             * np.float32(NUM_BUCKETS - max_exact)).astype(np.int32)
        large = np.minimum(large, NUM_BUCKETS - 1)
        bucket = np.where(dist < max_exact, dist, large)
        out.append(np.where(band, bucket, -1))
    return np.stack(out).astype(np.int32)


def _bias_expand(rel_bias, buckets, hpg, name):
    nh = N_GROUPS * hpg

    def body(rb_ref, bk_ref, o_ref):
        h = pl.program_id(0)
        bk = bk_ref[0]
        acc = jnp.full((BLOCK, 2 * BLOCK), NEG_INF, F32)
        for bb in range(NUM_BUCKETS):
            acc = jnp.where(bk == bb, rb_ref[bb, h], acc)
        o_ref[0] = acc

    return pl.pallas_call(
        body, name=name, grid=(nh,),
        in_specs=[pl.BlockSpec(memory_space=pltpu.SMEM),
                  pl.BlockSpec((1, BLOCK, 2 * BLOCK), lambda h: (h // hpg, 0, 0))],
        out_specs=pl.BlockSpec((1, BLOCK, 2 * BLOCK), lambda h: (h, 0, 0)),
        out_shape=jax.ShapeDtypeStruct((nh, BLOCK, 2 * BLOCK), F32), compiler_params=_cparams(("arbitrary",)),
    )(rel_bias, buckets)


def _bias_reduce(ds_sum, buckets, hpg, name):
    nh = N_GROUPS * hpg

    def body(ds_ref, bk_ref, o_ref):
        bk = bk_ref[0]
        dsv = ds_ref[0]
        lane = lax.broadcasted_iota(jnp.int32, (1, LANES), 1)
        row = jnp.zeros((1, LANES), F32)
        for bb in range(NUM_BUCKETS):
            tot = jnp.sum(jnp.sum(jnp.where(bk == bb, dsv, 0.0), axis=-1, keepdims=True), axis=0, keepdims=True)
            row = jnp.where(lane == bb, tot, row)
        o_ref[0] = row

    return pl.pallas_call(
        body, name=name, grid=(nh,),
        in_specs=[pl.BlockSpec((1, BLOCK, 2 * BLOCK), lambda h: (h, 0, 0)),
                  pl.BlockSpec((1, BLOCK, 2 * BLOCK), lambda h: (h // hpg, 0, 0))],
        out_specs=pl.BlockSpec((1, 1, LANES), lambda h: (h, 0, 0)),
        out_shape=jax.ShapeDtypeStruct((nh, 1, LANES), F32), compiler_params=_cparams(("arbitrary",)),
    )(ds_sum, buckets)


def _chunk_rows(c, d, nb):
    r, n = c // nb, c % nb
    if d == 1:
        return pl.ds(c * BLOCK, BLOCK)
    return pl.ds(r + n * BLOCK * d, BLOCK, stride=d)


def _segment_ones():
    i = lax.broadcasted_iota(jnp.int32, (LANES, LANES), 0) // HEAD_DIM
    j = lax.broadcasted_iota(jnp.int32, (LANES, LANES), 1) // HEAD_DIM
    return (i == j).astype(F32)


def _head_mean(v, seg):
    return jnp.dot(v, seg, precision=HIGHEST, preferred_element_type=F32) * (1.0 / HEAD_DIM)


def _attn_fwd(u, qg2, kg2, bias, gi, cols, hpg, name):
    t = u.shape[0]
    d = DILATIONS[gi]
    nchunk = t // BLOCK
    nb = (t // d) // BLOCK
    hp = hpg // 2
    qc0, kc0, vc0 = [(c + gi * hpg * HEAD_DIM) // LANES for c in cols]
    contract_lanes = (((1,), (1,)), ((), ()))

    def body(q_ref, k_ref, v_ref, qg_ref, kg_ref, bias_ref, o_ref, lse_ref, qd, kd, vd, od, ld):
        seg = _segment_ones()
        lane = lax.broadcasted_iota(jnp.int32, (1, LANES), 1)
        qg = qg_ref[...] * (HEAD_DIM ** -0.5)
        kg = kg_ref[...]
        kd[0:BLOCK, :] = jnp.zeros((BLOCK, LANES), BF16)
        vd[0:BLOCK, :] = jnp.zeros((BLOCK, LANES), BF16)
        for c in range(nchunk):
            rows = _chunk_rows(c, d, nb)
            qv = q_ref[rows, :]
            kv = k_ref[rows, :]
            qd[c * BLOCK:(c + 1) * BLOCK, :] = (qv * lax.rsqrt(_head_mean(qv * qv, seg) + EPS) * qg).astype(BF16)
            kd[(c + 1) * BLOCK:(c + 2) * BLOCK, :] = (kv * lax.rsqrt(_head_mean(kv * kv, seg) + EPS) * kg).astype(BF16)
            vd[(c + 1) * BLOCK:(c + 2) * BLOCK, :] = v_ref[rows, :].astype(BF16)

        col = lax.broadcasted_iota(jnp.int32, (BLOCK, 2 * BLOCK), 1)

        def chunk(c, carry):
            r0 = pl.multiple_of(c * BLOCK, BLOCK)
            qc = qd[pl.ds(r0, BLOCK), :]
            kw = kd[pl.ds(r0, 2 * BLOCK), :]
            vw = vd[pl.ds(r0, 2 * BLOCK), :]
            kill = jnp.logical_and(col < BLOCK, (c % nb) == 0)
            o_acc = jnp.zeros((BLOCK, LANES), F32)
            l_acc = jnp.zeros((BLOCK, LANES), F32)
            for j in range(2):
                mj = jnp.logical_and(lane >= j * HEAD_DIM, lane < (j + 1) * HEAD_DIM)
                kj = jnp.where(mj, kw, jnp.zeros_like(kw))
                s = lax.dot_general(qc, kj, contract_lanes, preferred_element_type=F32) + bias_ref[j]
                s = jnp.where(kill, NEG_INF, s)
                mx = jnp.max(s, axis=-1, keepdims=True)
                p = jnp.exp(s - mx)
                l = jnp.sum(p, axis=-1, keepdims=True)
                oj = jnp.dot((p / l).astype(BF16), vw, preferred_element_type=F32)
                o_acc = jnp.where(mj, oj, o_acc)
                l_acc = jnp.where(mj, mx + jnp.log(l), l_acc)
            od[pl.ds(r0, BLOCK), :] = o_acc
            ld[pl.ds(r0, BLOCK), :] = l_acc
            return carry

        lax.fori_loop(0, nchunk, chunk, 0)
        for c in range(nchunk):
            rows = _chunk_rows(c, d, nb)
            o_ref[rows, :] = od[c * BLOCK:(c + 1) * BLOCK, :]
            lse_ref[rows, :] = ld[c * BLOCK:(c + 1) * BLOCK, :]

    ucol = lambda c0: pl.BlockSpec((t, LANES), lambda h: (0, c0 + h))
    vec = pl.BlockSpec((1, LANES), lambda h: (0, 0))
    oblk = pl.BlockSpec((t, LANES), lambda h: (0, h))
    osh = jax.ShapeDtypeStruct((t, hpg * HEAD_DIM), F32)
    return pl.pallas_call(
        body, name=name, grid=(hp,),
        in_specs=[ucol(qc0), ucol(kc0), ucol(vc0), vec, vec,
                  pl.BlockSpec((2, BLOCK, 2 * BLOCK), lambda h: (gi * hp + h, 0, 0))],
        out_specs=(oblk, oblk), out_shape=(osh, osh),
        scratch_shapes=[pltpu.VMEM((t, LANES), BF16), pltpu.VMEM((t + BLOCK, LANES), BF16),
                        pltpu.VMEM((t + BLOCK, LANES), BF16), pltpu.VMEM((t, LANES), F32), pltpu.VMEM((t, LANES), F32)],
        compiler_params=_cparams(("arbitrary",)),
    )(u, u, u, qg2, kg2, bias)


def _attn_bwd(u, do_g, dd_g, lse_g, qg2, kg2, bias, ds_in, gi, cols, hpg, name):
    t = u.shape[0]
    d = DILATIONS[gi]
    nchunk = t // BLOCK
    nb = (t // d) // BLOCK
    hp = hpg // 2
    qc0, kc0, vc0 = [(c + gi * hpg * HEAD_DIM) // LANES for c in cols]
    contract_lanes = (((1,), (1,)), ((), ()))
    contract_rows = (((0,), (0,)), ((), ()))
    qscale = HEAD_DIM ** -0.5

    def body(q_ref, k_ref, v_ref, do_ref, dd_ref, lse_ref, qg_ref, kg_ref, bias_ref, dsin_ref,
             dq_ref, dk_ref, dv_ref, dgq_ref, dgk_ref, dsout_ref,
             qd, kd, vd, dod, ddd, ld, dqd, dkd, dvd, dsacc):
        seg = _segment_ones()
        lane = lax.broadcasted_iota(jnp.int32, (1, LANES), 1)
        qg = qg_ref[...] * qscale
        kg = kg_ref[...]
        kd[0:BLOCK, :] = jnp.zeros((BLOCK, LANES), BF16)
        vd[0:BLOCK, :] = jnp.zeros((BLOCK, LANES), BF16)
        dkd[...] = jnp.zeros_like(dkd)
        dvd[...] = jnp.zeros_like(dvd)
        dsacc[...] = jnp.zeros_like(dsacc)
        for c in range(nchunk):
            rows = _chunk_rows(c, d, nb)
            qv = q_ref[rows, :]
            kv = k_ref[rows, :]
            qd[c * BLOCK:(c + 1) * BLOCK, :] = (qv * lax.rsqrt(_head_mean(qv * qv, seg) + EPS) * qg).astype(BF16)
            kd[(c + 1) * BLOCK:(c + 2) * BLOCK, :] = (kv * lax.rsqrt(_head_mean(kv * kv, seg) + EPS) * kg).astype(BF16)
            vd[(c + 1) * BLOCK:(c + 2) * BLOCK, :] = v_ref[rows, :].astype(BF16)
            dod[c * BLOCK:(c + 1) * BLOCK, :] = do_ref[rows, :].astype(BF16)
            ddd[c * BLOCK:(c + 1) * BLOCK, :] = dd_ref[rows, :]
            ld[c * BLOCK:(c + 1) * BLOCK, :] = lse_ref[rows, :]

        col = lax.broadcasted_iota(jnp.int32, (BLOCK, 2 * BLOCK), 1)

        def chunk(c, carry):
            r0 = pl.multiple_of(c * BLOCK, BLOCK)
            qc = qd[pl.ds(r0, BLOCK), :]
            kw = kd[pl.ds(r0, 2 * BLOCK), :]
            vw = vd[pl.ds(r0, 2 * BLOCK), :]
            doc = dod[pl.ds(r0, BLOCK), :]
            ddc = ddd[pl.ds(r0, BLOCK), :]
            lc = ld[pl.ds(r0, BLOCK), :]
            kill = jnp.logical_and(col < BLOCK, (c % nb) == 0)
            dq_acc = jnp.zeros((BLOCK, LANES), F32)
            dkw = jnp.zeros((2 * BLOCK, LANES), F32)
            dvw = jnp.zeros((2 * BLOCK, LANES), F32)
            for j in range(2):
                mj = jnp.logical_and(lane >= j * HEAD_DIM, lane < (j + 1) * HEAD_DIM)
                first = lane == j * HEAD_DIM
                kj = jnp.where(mj, kw, jnp.zeros_like(kw))
                vj = jnp.where(mj, vw, jnp.zeros_like(vw))
                qj = jnp.where(mj, qc, jnp.zeros_like(qc))
                doj = jnp.where(mj, doc, jnp.zeros_like(doc))
                s = lax.dot_general(qc, kj, contract_lanes, preferred_element_type=F32) + bias_ref[j]
                s = jnp.where(kill, NEG_INF, s)
                lse_j = jnp.sum(jnp.where(first, lc, 0.0), axis=-1, keepdims=True)
                dd_j = jnp.sum(jnp.where(first, ddc, 0.0), axis=-1, keepdims=True)
                p = jnp.exp(s - lse_j)
                dp = lax.dot_general(doc, vj, contract_lanes, preferred_element_type=F32)
                ds = p * (dp + dd_j)
                dsacc[j] += ds
                dsb = ds.astype(BF16)
                dq_acc = dq_acc + jnp.dot(dsb, kj, preferred_element_type=F32)
                dkw = dkw + lax.dot_general(dsb, qj, contract_rows, preferred_element_type=F32)
                dvw = dvw + lax.dot_general(p.astype(BF16), doj, contract_rows, preferred_element_type=F32)
            dqd[pl.ds(r0, BLOCK), :] = dq_acc
            dkd[pl.ds(r0, 2 * BLOCK), :] += dkw
            dvd[pl.ds(r0, 2 * BLOCK), :] += dvw
            return carry

        lax.fori_loop(0, nchunk, chunk, 0)
        dsout_ref[...] = dsin_ref[...] + dsacc[...]

        dgq = jnp.zeros((1, LANES), F32)
        dgk = jnp.zeros((1, LANES), F32)
        for c in range(nchunk):
            rows = _chunk_rows(c, d, nb)
            qv = q_ref[rows, :]
            rq = lax.rsqrt(_head_mean(qv * qv, seg) + EPS)
            qh = qv * rq
            dy = dqd[c * BLOCK:(c + 1) * BLOCK, :]
            dgq = dgq + jnp.sum(dy * qh, axis=0, keepdims=True) * qscale
            dxh = dy * qg
            ddd[rows, :] = rq * (dxh - qh * _head_mean(dxh * qh, seg))
            kv = k_ref[rows, :]
            rk = lax.rsqrt(_head_mean(kv * kv, seg) + EPS)
            kh = kv * rk
            dy = dkd[(c + 1) * BLOCK:(c + 2) * BLOCK, :]
            dgk = dgk + jnp.sum(dy * kh, axis=0, keepdims=True)
            dxh = dy * kg
            ld[rows, :] = rk * (dxh - kh * _head_mean(dxh * kh, seg))
        dq_ref[...] = ddd[...].astype(BF16)
        dk_ref[...] = ld[...].astype(BF16)
        for c in range(nchunk):
            ddd[_chunk_rows(c, d, nb), :] = dvd[(c + 1) * BLOCK:(c + 2) * BLOCK, :]
        dv_ref[...] = ddd[...].astype(BF16)
        dgq_ref[0] = dgq
        dgk_ref[0] = dgk

    ucol = lambda c0: pl.BlockSpec((t, LANES), lambda h: (0, c0 + h))
    vec = pl.BlockSpec((1, LANES), lambda h: (0, 0))
    oblk = pl.BlockSpec((t, LANES), lambda h: (0, h))
    bblk = pl.BlockSpec((2, BLOCK, 2 * BLOCK), lambda h: (gi * hp + h, 0, 0))
    gblk = pl.BlockSpec((1, 1, LANES), lambda h: (h, 0, 0))
    osh = jax.ShapeDtypeStruct((t, hpg * HEAD_DIM), BF16)
    gsh = jax.ShapeDtypeStruct((hp, 1, LANES), F32)
    return pl.pallas_call(
        body, name=name, grid=(hp,),
        in_specs=[ucol(qc0), ucol(kc0), ucol(vc0), oblk, oblk, oblk, vec, vec, bblk, bblk],
        out_specs=(oblk, oblk, oblk, gblk, gblk, bblk),
        out_shape=(osh, osh, osh, gsh, gsh, jax.ShapeDtypeStruct(ds_in.shape, F32)),
        input_output_aliases={9: 5},
        scratch_shapes=[pltpu.VMEM((t, LANES), BF16), pltpu.VMEM((t + BLOCK, LANES), BF16),
                        pltpu.VMEM((t + BLOCK, LANES), BF16), pltpu.VMEM((t, LANES), BF16),
                        pltpu.VMEM((t, LANES), F32), pltpu.VMEM((t, LANES), F32), pltpu.VMEM((t, LANES), F32),
                        pltpu.VMEM((t + BLOCK, LANES), F32), pltpu.VMEM((t + BLOCK, LANES), F32),
                        pltpu.VMEM((2, BLOCK, 2 * BLOCK), F32)],
        compiler_params=_cparams(("arbitrary",)),
    )(u, u, u, do_g, dd_g, lse_g, qg2, kg2, bias, ds_in)


def _group_weights(l0, l1, l2):
    mx = jnp.maximum(jnp.maximum(l0, l1), l2)
    e0, e1, e2 = jnp.exp(l0 - mx), jnp.exp(l1 - mx), jnp.exp(l2 - mx)
    inv = 1.0 / (e0 + e1 + e2)
    return e0 * inv, e1 * inv, e2 * inv


def _combine_fwd(os_, lses, name):
    t, ao = os_[0].shape

    def body(o0, o1, o2, l0, l1, l2, o_ref):
        w0, w1, w2 = _group_weights(l0[...], l1[...], l2[...])
        o_ref[...] = (w0 * o0[...] + w1 * o1[...] + w2 * o2[...]).astype(BF16)

    row = pl.BlockSpec((ROWS, ao), lambda i: (i, 0))
    return pl.pallas_call(
        body, name=name, grid=(t // ROWS,), in_specs=[row] * 6, out_specs=row,
        out_shape=jax.ShapeDtypeStruct((t, ao), BF16), compiler_params=_cparams(("arbitrary",)),
    )(*os_, *lses)


def _combine_bwd(do, os_, lses, name):
    t, ao = do.shape
    idx = np.arange(ao) // HEAD_DIM
    seg = jnp.asarray((idx[:, None] == idx[None, :]).astype(np.float32))

    def body(do_ref, o0, o1, o2, l0, l1, l2, seg_ref, g0, g1, g2, d0, d1, d2):
        w0, w1, w2 = _group_weights(l0[...], l1[...], l2[...])
        dov = do_ref[...]
        o = w0 * o0[...] + w1 * o1[...] + w2 * o2[...]
        sd = jnp.dot(dov * o, seg_ref[...], precision=HIGHEST, preferred_element_type=F32)
        for w, gref, dref in ((w0, g0, d0), (w1, g1, d1), (w2, g2, d2)):
            gref[...] = w * dov
            dref[...] = -(w * sd)

    row = pl.BlockSpec((ROWS, ao), lambda i: (i, 0))
    sh = jax.ShapeDtypeStruct((t, ao), F32)
    outs = pl.pallas_call(
        body, name=name, grid=(t // ROWS,), in_specs=[row] * 7 + [pl.BlockSpec((ao, ao), lambda i: (0, 0))],
        out_specs=(row,) * 6, out_shape=(sh,) * 6, compiler_params=_cparams(("arbitrary",)),
    )(do, *os_, *lses, seg)
    return outs[:3], outs[3:]


def _adamw(w, g, m, v, name):
    shape = w.shape
    cols = shape[-1]
    rows = int(np.prod(shape[:-1]))
    tr = rows if rows <= 512 else _tile_rows(rows)
    c1 = 1.0 - ADAM_B1 ** ADAM_STEP
    c2 = 1.0 - ADAM_B2 ** ADAM_STEP

    def body(w_ref, g_ref, m_ref, v_ref, d_ref, nm_ref, nv_ref):
        gv = g_ref[...]
        mn = ADAM_B1 * m_ref[...] + (1.0 - ADAM_B1) * gv
        vn = ADAM_B2 * v_ref[...] + (1.0 - ADAM_B2) * (gv * gv)
        nm_ref[...] = mn
        nv_ref[...] = vn
        d_ref[...] = -ADAM_LR * ((mn / c1) / (jnp.sqrt(vn / c2) + ADAM_EPS) + ADAM_WD * w_ref[...])

    blk = pl.BlockSpec((tr, cols), lambda i: (i, 0))
    sh = jax.ShapeDtypeStruct((rows, cols), F32)
    outs = pl.pallas_call(
        body, name=name, grid=(rows // tr,), in_specs=[blk] * 4, out_specs=(blk,) * 3, out_shape=(sh,) * 3,
        compiler_params=_cparams(("arbitrary",)),
    )(*[a.reshape(rows, cols) for a in (w, g, m, v)])
    return tuple(o.reshape(shape) for o in outs)


def _tile_rows(rows):
    for t in (512, 256, 128, 64, 32, 16, 8):
        if rows % t == 0:
            return t
    return rows


def _sum_slots(recv, name):
    _, rows, cols = recv.shape
    tr = rows if rows <= 512 else _tile_rows(rows)

    def body(r_ref, o_ref):
        acc = r_ref[0].astype(F32)
        for s in range(1, N_DEV):
            acc = acc + r_ref[s].astype(F32)
        o_ref[...] = acc

    return pl.pallas_call(
        body, name=name, grid=(rows // tr,), in_specs=[pl.BlockSpec((N_DEV, tr, cols), lambda i: (0, i, 0))],
        out_specs=pl.BlockSpec((tr, cols), lambda i: (i, 0)), out_shape=jax.ShapeDtypeStruct((rows, cols), F32),
        compiler_params=_cparams(("arbitrary",)),
    )(recv)


def _peer(k):
    x, y, c = lax.axis_index("x"), lax.axis_index("y"), lax.axis_index("c")
    return (1 - x if k & 4 else x, 1 - y if k & 2 else y, 1 - c if k & 1 else c)


def _dev_index(p):
    return 4 * p[0] + 2 * p[1] + p[2]


def _all_gather(shards, name):
    n = len(shards)
    chips = (4, 2, 6)

    def body(*refs):
        ins, outs = refs[:n], refs[n:2 * n]
        send_sems, recv_sems, local_sems = refs[2 * n:]
        me = _peer(0)
        sibling = _peer(1)

        def copy(i, k, block_dev, to, src=None):
            dst = outs[i].at[_dev_index(block_dev)]
            return pltpu.make_async_remote_copy(
                src_ref=dst if src is None else src, dst_ref=dst,
                send_sem=send_sems.at[i * 7 + k], recv_sem=recv_sems.at[i * 7 + k],
                device_id=to, device_id_type=MESH)

        mine = [pltpu.make_async_copy(ins[i], outs[i].at[_dev_index(me)], local_sems.at[i]) for i in range(n)]
        for cp in mine:
            cp.start()
        first = []
        for i in range(n):
            first.append(copy(i, 0, me, sibling, src=ins[i]))
            for j, k in enumerate(chips):
                first.append(copy(i, 1 + j, me, _peer(k), src=ins[i]))
        for cp in first:
            cp.start()
        passed = []
        for j, k in enumerate(chips):
            for i in range(n):
                copy(i, 1 + j, _peer(k), me).wait_recv()
                cp = copy(i, 4 + j, _peer(k), sibling)
                cp.start()
                passed.append(cp)
        for i in range(n):
            copy(i, 0, sibling, me).wait_recv()
            for j, k in enumerate(chips):
                copy(i, 4 + j, _peer(k | 1), me).wait_recv()
        for cp in first + passed:
            cp.wait_send()
        for cp in mine:
            cp.wait()

    anyspec = pl.BlockSpec(memory_space=pl.ANY)
    return pl.pallas_call(
        body, name=name, in_specs=[anyspec] * n, out_specs=tuple([anyspec] * n),
        out_shape=tuple(jax.ShapeDtypeStruct((N_DEV,) + s.shape, s.dtype) for s in shards),
        scratch_shapes=[pltpu.SemaphoreType.DMA((7 * n,)), pltpu.SemaphoreType.DMA((7 * n,)),
                        pltpu.SemaphoreType.DMA((n,))],
    )(*shards)


def _exchange(parts, name):
    n = len(parts)

    def body(*refs):
        ins, outs = refs[:n], refs[n:2 * n]
        send_sems, recv_sems, local_sems = refs[2 * n:]
        me = _dev_index(_peer(0))
        mine = [pltpu.make_async_copy(ins[i].at[me], outs[i].at[me], local_sems.at[i]) for i in range(n)]
        for cp in mine:
            cp.start()
        copies = []
        for k in range(1, N_DEV):
            to = _peer(k)
            for i in range(n):
                copies.append(pltpu.make_async_remote_copy(
                    src_ref=ins[i].at[_dev_index(to)], dst_ref=outs[i].at[me],
                    send_sem=send_sems.at[i * 7 + k - 1], recv_sem=recv_sems.at[i * 7 + k - 1],
                    device_id=to, device_id_type=MESH))
        for cp in copies:
            cp.start()
        for k in range(1, N_DEV):
            frm = _peer(k)
            for i in range(n):
                pltpu.make_async_remote_copy(
                    src_ref=ins[i].at[me], dst_ref=outs[i].at[_dev_index(frm)],
                    send_sem=send_sems.at[i * 7 + k - 1], recv_sem=recv_sems.at[i * 7 + k - 1],
                    device_id=frm, device_id_type=MESH).wait_recv()
        for cp in copies:
            cp.wait_send()
        for cp in mine:
            cp.wait()

    anyspec = pl.BlockSpec(memory_space=pl.ANY)
    return pl.pallas_call(
        body, name=name, in_specs=[anyspec] * n, out_specs=tuple([anyspec] * n),
        out_shape=tuple(jax.ShapeDtypeStruct(p.shape, p.dtype) for p in parts),
        scratch_shapes=[pltpu.SemaphoreType.DMA((7 * n,)), pltpu.SemaphoreType.DMA((7 * n,)),
                        pltpu.SemaphoreType.DMA((n,))],
    )(*parts)


def _all_reduce_small(v, name):
    rows = v.shape[0]

    def body(v_ref, o_ref, buf, send_sems, recv_sems):
        me = _dev_index(_peer(0))
        buf[me] = v_ref[...]
        copies = []
        for k in range(1, N_DEV):
            copies.append(pltpu.make_async_remote_copy(
                src_ref=v_ref, dst_ref=buf.at[me], send_sem=send_sems.at[k - 1], recv_sem=recv_sems.at[k - 1],
                device_id=_peer(k), device_id_type=MESH))
        for cp in copies:
            cp.start()
        for k in range(1, N_DEV):
            pltpu.make_async_remote_copy(
                src_ref=v_ref, dst_ref=buf.at[_dev_index(_peer(k))], send_sem=send_sems.at[k - 1],
                recv_sem=recv_sems.at[k - 1], device_id=_peer(k), device_id_type=MESH).wait_recv()
        for cp in copies:
            cp.wait_send()
        acc = buf[0]
        for s in range(1, N_DEV):
            acc = acc + buf[s]
        o_ref[...] = acc

    vm = pl.BlockSpec(memory_space=pltpu.VMEM)
    return pl.pallas_call(
        body, name=name, in_specs=[vm], out_specs=vm, out_shape=jax.ShapeDtypeStruct(v.shape, F32),
        scratch_shapes=[pltpu.VMEM((N_DEV, rows, LANES), F32), pltpu.SemaphoreType.DMA((7,)),
                        pltpu.SemaphoreType.DMA((7,))],
    )(v)


def _local_step(x, target, rel_bias, small, big):
    depth = small["norm1_g"].shape[0]
    t, dm = x.shape
    cdim = small["conv_ln_g"].shape[1]
    ao = big["w_attn_out"].shape[1]
    hpg = ao // HEAD_DIM
    hp = hpg // 2
    attn_dim = N_GROUPS * ao
    q_col = 2 * cdim
    cols = (q_col, q_col + attn_dim, q_col + 2 * attn_dim)
    gate_col = q_col + 3 * attn_dim

    buckets = jnp.asarray(_bucket_table())
    bias = _bias_expand(rel_bias, buckets, hpg, "bias_expand")
    saved = []
    for l in range(depth):
        qg2 = jnp.tile(small["q_norm_g"][l], 2).reshape(1, LANES)
        kg2 = jnp.tile(small["k_norm_g"][l], 2).reshape(1, LANES)
        h1 = _rms_fwd(x, small["norm1_g"][l], "rms1_fwd")
        u = _mm(h1, big["w_in"][l], name="mm_in")
        zc = _conv_fwd(u, big["conv_dw_w"][l], small["conv_dw_b"][l], cdim, "conv_fwd")
        zs = _ln_swish_fwd(zc, small["conv_ln_g"][l], small["conv_ln_b"][l], "ln_swish_fwd")
        yc = _mm(zs, big["w_conv_out"][l], name="mm_conv_out")
        os_, lses = [], []
        for gi in range(N_GROUPS):
            o_g, lse_g = _attn_fwd(u, qg2, kg2, bias, gi, cols, hpg, "attn_fwd_g%d" % gi)
            os_.append(o_g)
            lses.append(lse_g)
        o = _combine_fwd(os_, lses, "combine_fwd")
        ya = _mm(o, big["w_attn_out"][l], name="mm_attn_out")
        mg = _gate_fwd(u, yc, ya, gate_col, "gate_fwd")
        x1 = _mm(mg, big["w_out"][l], epi="res", extra=x, name="mm_out")
        h2 = _rms_fwd(x1, small["norm2_g"][l], "rms2_fwd")
        f, act = _mm(h2, big["w_ff1"][l], epi="relu2", name="mm_ff1")
        x2 = _mm(act, big["w_ff2"][l], epi="res", extra=x1, name="mm_ff2")
        saved.append(dict(x=x, h1=h1, u=u, zc=zc, zs=zs, yc=yc, os=os_, lses=lses, o=o, ya=ya, mg=mg, x1=x1, h2=h2,
                          f=f, act=act, qg2=qg2, kg2=kg2))
        x = x2

    loss, dx = _loss_and_grad(x, target, "loss")

    grads = {k: [None] * depth for k in ("norm1_g", "w_in", "q_norm_g", "k_norm_g", "conv_dw_w", "conv_dw_b",
                                         "conv_ln_g", "conv_ln_b", "w_conv_out", "w_attn_out", "w_out", "norm2_g",
                                         "w_ff1", "w_ff2")}
    ds_sum = jnp.zeros((N_GROUPS * hpg, BLOCK, 2 * BLOCK), F32)
    for l in reversed(range(depth)):
        s = saved[l]
        df = _mm(dx, big["w_ff2"][l], tb=True, epi="drelu2", extra=s["f"], out_dtype=BF16, name="mm_dff2")
        grads["w_ff2"][l] = _mm(s["act"], dx, ta=True, name="mm_gw_ff2")
        grads["w_ff1"][l] = _mm(s["h2"], df, ta=True, name="mm_gw_ff1")
        dh2 = _mm(df, big["w_ff1"][l], tb=True, name="mm_dff1")
        dx1, dg2 = _rms_bwd(dh2, s["x1"], small["norm2_g"][l], dx, "rms2_bwd")
        grads["norm2_g"][l] = dg2[0]
        dmg = _mm(dx1, big["w_out"][l], tb=True, name="mm_dout")
        grads["w_out"][l] = _mm(s["mg"], dx1, ta=True, name="mm_gw_out")
        dyc, dya, dugc, duga = _gate_bwd(dmg, s["u"], s["yc"], s["ya"], gate_col, "gate_bwd")
        dzs = _mm(dyc, big["w_conv_out"][l], tb=True, name="mm_dconv_out")
        grads["w_conv_out"][l] = _mm(s["zs"], dyc, ta=True, name="mm_gw_conv_out")
        do = _mm(dya, big["w_attn_out"][l], tb=True, name="mm_dattn_out")
        grads["w_attn_out"][l] = _mm(s["o"], dya, ta=True, name="mm_gw_attn_out")
        dzc, dlg, dlb = _ln_swish_bwd(dzs, s["zc"], small["conv_ln_g"][l], small["conv_ln_b"][l], "ln_swish_bwd")
        grads["conv_ln_g"][l] = dlg[0]
        grads["conv_ln_b"][l] = dlb[0]
        da, dgt, dcw, dcb = _conv_bwd(dzc, s["u"], big["conv_dw_w"][l], cdim, "conv_bwd")
        grads["conv_dw_w"][l] = dcw[:CONV_WIDTH]
        grads["conv_dw_b"][l] = dcb[0]
        do_gs, dd_gs = _combine_bwd(do, s["os"], s["lses"], "combine_bwd")
        dqs, dks, dvs = [], [], []
        dgq = jnp.zeros((HEAD_DIM,), F32)
        dgk = jnp.zeros((HEAD_DIM,), F32)
        for gi in range(N_GROUPS):
            dq, dk, dv, gq, gk, ds_sum = _attn_bwd(s["u"], do_gs[gi], dd_gs[gi], s["lses"][gi], s["qg2"], s["kg2"],
                                                   bias, ds_sum, gi, cols, hpg, "attn_bwd_g%d" % gi)
            dqs.append(dq)
            dks.append(dk)
            dvs.append(dv)
            dgq = dgq + jnp.sum(gq.reshape(hp * 2, HEAD_DIM), axis=0)
            dgk = dgk + jnp.sum(gk.reshape(hp * 2, HEAD_DIM), axis=0)
        grads["q_norm_g"][l] = dgq
        grads["k_norm_g"][l] = dgk
        du = jnp.concatenate([da, dgt] + dqs + dks + dvs + [dugc, duga], axis=1)
        grads["w_in"][l] = _mm(s["h1"], du, ta=True, name="mm_gw_in")
        dh1 = _mm(du, big["w_in"][l], tb=True, name="mm_din")
        dx, dg1 = _rms_bwd(dh1, s["x"], small["norm1_g"][l], dx1, "rms1_bwd")
        grads["norm1_g"][l] = dg1[0]

    grads = {k: jnp.stack(v) for k, v in grads.items()}
    db = _bias_reduce(ds_sum, buckets, hpg, "bias_reduce")
    grads["rel_bias"] = db[:, 0, :NUM_BUCKETS].T
    return loss, dx, grads


BIG = ("w_in", "conv_dw_w", "w_conv_out", "w_attn_out", "w_out", "w_ff1", "w_ff2")
COL_SHARDED = ("w_in", "conv_dw_w", "w_conv_out", "w_attn_out", "w_ff1")
SMALL = ("rel_bias", "norm1_g", "q_norm_g", "k_norm_g", "conv_dw_b", "conv_ln_g", "conv_ln_b", "norm2_g")
WEIGHTS = ("rel_bias", "norm1_g", "w_in", "q_norm_g", "k_norm_g", "conv_dw_w", "conv_dw_b", "conv_ln_g", "conv_ln_b",
           "w_conv_out", "w_attn_out", "w_out", "norm2_g", "w_ff1", "w_ff2")


def _to_whole(name, gathered):
    n, depth, a, b = gathered.shape
    if name in COL_SHARDED:
        return gathered.transpose(1, 2, 0, 3).reshape(depth, a, n * b)
    return gathered.transpose(1, 0, 2, 3).reshape(depth, n * a, b)


def _to_slots(name, whole):
    depth, a, b = whole.shape
    if name in COL_SHARDED:
        return whole.reshape(depth, a, N_DEV, b // N_DEV).transpose(2, 0, 1, 3)
    return whole.reshape(depth, N_DEV, a // N_DEV, b).transpose(1, 0, 2, 3)


def kernel(x, rel_bias, norm1_g, w_in, q_norm_g, k_norm_g, conv_dw_w, conv_dw_b, conv_ln_g, conv_ln_b, w_conv_out, w_attn_out, w_out, norm2_g, w_ff1, w_ff2, loss_target, m_rel_bias, m_norm1_g, m_w_in, m_q_norm_g, m_k_norm_g, m_conv_dw_w, m_conv_dw_b, m_conv_ln_g, m_conv_ln_b, m_w_conv_out, m_w_attn_out, m_w_out, m_norm2_g, m_w_ff1, m_w_ff2, v_rel_bias, v_norm1_g, v_w_in, v_q_norm_g, v_k_norm_g, v_conv_dw_w, v_conv_dw_b, v_conv_ln_g, v_conv_ln_b, v_w_conv_out, v_w_attn_out, v_w_out, v_norm2_g, v_w_ff1, v_w_ff2):
    w = dict(rel_bias=rel_bias, norm1_g=norm1_g, w_in=w_in, q_norm_g=q_norm_g, k_norm_g=k_norm_g, conv_dw_w=conv_dw_w,
             conv_dw_b=conv_dw_b, conv_ln_g=conv_ln_g, conv_ln_b=conv_ln_b, w_conv_out=w_conv_out,
             w_attn_out=w_attn_out, w_out=w_out, norm2_g=norm2_g, w_ff1=w_ff1, w_ff2=w_ff2)
    mom = dict(rel_bias=m_rel_bias, norm1_g=m_norm1_g, w_in=m_w_in, q_norm_g=m_q_norm_g, k_norm_g=m_k_norm_g,
               conv_dw_w=m_conv_dw_w, conv_dw_b=m_conv_dw_b, conv_ln_g=m_conv_ln_g, conv_ln_b=m_conv_ln_b,
               w_conv_out=m_w_conv_out, w_attn_out=m_w_attn_out, w_out=m_w_out, norm2_g=m_norm2_g, w_ff1=m_w_ff1,
               w_ff2=m_w_ff2)
    var = dict(rel_bias=v_rel_bias, norm1_g=v_norm1_g, w_in=v_w_in, q_norm_g=v_q_norm_g, k_norm_g=v_k_norm_g,
               conv_dw_w=v_conv_dw_w, conv_dw_b=v_conv_dw_b, conv_ln_g=v_conv_ln_g, conv_ln_b=v_conv_ln_b,
               w_conv_out=v_w_conv_out, w_attn_out=v_w_attn_out, w_out=v_w_out, norm2_g=v_norm2_g, w_ff1=v_w_ff1,
               w_ff2=v_w_ff2)

    shards = [w[k] if k == "conv_dw_w" else w[k].astype(BF16) for k in BIG]
    gathered = _all_gather(shards, "gather_weights")
    big = {k: _to_whole(k, g) for k, g in zip(BIG, gathered)}
    big["conv_dw_w"] = jnp.pad(big["conv_dw_w"], ((0, 0), (0, CONV_TAPS_PADDED - CONV_WIDTH), (0, 0)))
    small = {k: w[k] for k in SMALL}

    loss, grad_x, g = _local_step(x[0], loss_target[0], rel_bias, small, big)

    flat = jnp.concatenate([g[k].reshape(-1) for k in SMALL])
    nflat = flat.shape[0]
    rows = -(-nflat // (8 * LANES)) * 8
    packed = jnp.pad(flat, (0, rows * LANES - nflat)).reshape(rows, LANES)
    total = _all_reduce_small(packed, "reduce_small").reshape(-1)
    grad = {}
    off = 0
    for k in SMALL:
        size = int(np.prod(w[k].shape))
        grad[k] = total[off:off + size].reshape(w[k].shape)
        off += size

    parts = [_to_slots(k, g[k]).astype(BF16) for k in BIG]
    recv = _exchange(parts, "exchange_grads")
    for k, r in zip(BIG, recv):
        shp = w[k].shape
        grad[k] = _sum_slots(r.reshape(N_DEV, -1, shp[-1]), "sum_" + k).reshape(shp)

    loss = lax.psum(loss[0, 0], ("x", "y", "c"))
    outs = {k: _adamw(w[k], grad[k], mom[k], var[k], "adamw_" + k) for k in WEIGHTS}
    return (loss, grad_x[None], *[grad[k] for k in WEIGHTS], *[outs[k][0] for k in WEIGHTS],
            *[outs[k][1] for k in WEIGHTS], *[outs[k][2] for k in WEIGHTS])
```

```python
import functools
import math

import numpy as np
import jax
import jax.numpy as jnp
from jax import lax
from jax.experimental import pallas as pl
from jax.experimental.pallas import tpu as pltpu

F32 = jnp.float32
BF16 = jnp.bfloat16

HEAD_DIM = 64
N_GROUPS = 3
DILATIONS = (1, 4, 16)
SUB_WINDOW = 128
BLOCK = 128
CONV_WIDTH = 31
CONV_TAPS_PADDED = 32
NUM_BUCKETS = 32
MAX_REL_DISTANCE = 2048
EPS = 1e-6
NEG_INF = -1e30
LANES = 128

ADAM_LR = 0.001
ADAM_B1 = 0.9
ADAM_B2 = 0.999
ADAM_EPS = 1e-08
ADAM_WD = 0.01
ADAM_STEP = 10

N_DEV = 8
VMEM_LIMIT = 56 * 1024 * 1024
MESH = pl.DeviceIdType.MESH
HIGHEST = lax.Precision.HIGHEST


def _cparams(sem=None):
    return pltpu.CompilerParams(dimension_semantics=sem, vmem_limit_bytes=VMEM_LIMIT)


def _tile(n, target):
    if n <= target:
        return n
    t = (target // LANES) * LANES
    while t >= LANES:
        if n % t == 0:
            return t
        t -= LANES
    return n


def _sigmoid(v):
    return 1.0 / (1.0 + jnp.exp(-v))


def _mm(a, b, *, ta=False, tb=False, out_dtype=F32, epi=None, extra=None, after=None, name, tm=1024, tn=1024,
        tk=512):
    m = a.shape[1] if ta else a.shape[0]
    kdim = a.shape[0] if ta else a.shape[1]
    n = b.shape[0] if tb else b.shape[1]
    tm, tn, tk = _tile(m, tm), _tile(n, tn), _tile(kdim, tk)
    nk = kdim // tk
    a_spec = pl.BlockSpec((tk, tm), lambda i, j, k: (k, i)) if ta else pl.BlockSpec((tm, tk), lambda i, j, k: (i, k))
    b_spec = pl.BlockSpec((tn, tk), lambda i, j, k: (j, k)) if tb else pl.BlockSpec((tk, tn), lambda i, j, k: (k, j))
    o_spec = pl.BlockSpec((tm, tn), lambda i, j, k: (i, j))
    dims = (((0 if ta else 1,), (1 if tb else 0,)), ((), ()))
    n_in = 2 + (extra is not None) + (after is not None)
    n_out = 2 if epi == "relu2" else 1

    def body(*refs):
        a_ref, b_ref = refs[0], refs[1]
        e_ref = refs[2] if extra is not None else None
        outs = refs[n_in:n_in + n_out]
        acc_ref = refs[-1]
        k = pl.program_id(2)

        @pl.when(k == 0)
        def _():
            acc_ref[...] = jnp.zeros_like(acc_ref)

        acc_ref[...] += lax.dot_general(a_ref[...].astype(BF16), b_ref[...].astype(BF16), dims,
                                        preferred_element_type=F32)

        @pl.when(k == nk - 1)
        def _():
            acc = acc_ref[...]
            if epi is None:
                outs[0][...] = acc.astype(outs[0].dtype)
            elif epi == "res":
                outs[0][...] = (e_ref[...] + acc).astype(outs[0].dtype)
            elif epi == "relu2":
                outs[0][...] = acc
                r = jnp.maximum(acc, 0.0)
                outs[1][...] = (r * r).astype(BF16)
            elif epi == "drelu2":
                outs[0][...] = (acc * (2.0 * jnp.maximum(e_ref[...], 0.0))).astype(outs[0].dtype)

    in_specs = ([a_spec, b_spec] + ([o_spec] if extra is not None else [])
                + ([pl.BlockSpec(memory_space=pl.ANY)] if after is not None else []))
    if epi == "relu2":
        out_shape = (jax.ShapeDtypeStruct((m, n), F32), jax.ShapeDtypeStruct((m, n), BF16))
        out_specs = (o_spec, o_spec)
    else:
        out_shape = jax.ShapeDtypeStruct((m, n), out_dtype)
        out_specs = o_spec
    args = (a, b) + ((extra,) if extra is not None else ()) + ((after,) if after is not None else ())
    return pl.pallas_call(
        body, name=name, grid=(m // tm, n // tn, nk), in_specs=in_specs, out_specs=out_specs, out_shape=out_shape,
        scratch_shapes=[pltpu.VMEM((tm, tn), F32)],
        compiler_params=_cparams(("parallel", "parallel", "arbitrary")),
    )(*args)


ROWS = 256


def _rms_fwd(x, g, name):
    t, d = x.shape

    def body(x_ref, g_ref, h_ref):
        xv = x_ref[...]
        r = lax.rsqrt(jnp.mean(xv * xv, axis=-1, keepdims=True) + EPS)
        h_ref[...] = (xv * r * g_ref[...]).astype(BF16)

    return pl.pallas_call(
        body, name=name, grid=(t // ROWS,),
        in_specs=[pl.BlockSpec((ROWS, d), lambda i: (i, 0)), pl.BlockSpec((1, d), lambda i: (0, 0))],
        out_specs=pl.BlockSpec((ROWS, d), lambda i: (i, 0)),
        out_shape=jax.ShapeDtypeStruct((t, d), BF16), compiler_params=_cparams(("arbitrary",)),
    )(x, g.reshape(1, d))


def _rms_bwd(dh, x, g, dres, name):
    t, d = x.shape

    def body(dh_ref, x_ref, g_ref, dres_ref, dx_ref, dg_ref):
        @pl.when(pl.program_id(0) == 0)
        def _():
            dg_ref[...] = jnp.zeros_like(dg_ref)

        xv = x_ref[...]
        dhv = dh_ref[...]
        r = lax.rsqrt(jnp.mean(xv * xv, axis=-1, keepdims=True) + EPS)
        xh = xv * r
        dg_ref[...] += jnp.sum(dhv * xh, axis=0, keepdims=True)
        dxh = dhv * g_ref[...]
        dx_ref[...] = dres_ref[...] + r * (dxh - xh * jnp.mean(dxh * xh, axis=-1, keepdims=True))

    row = pl.BlockSpec((ROWS, d), lambda i: (i, 0))
    vec = pl.BlockSpec((1, d), lambda i: (0, 0))
    return pl.pallas_call(
        body, name=name, grid=(t // ROWS,), in_specs=[row, row, vec, row], out_specs=(row, vec),
        out_shape=(jax.ShapeDtypeStruct((t, d), F32), jax.ShapeDtypeStruct((1, d), F32)),
        compiler_params=_cparams(("arbitrary",)),
    )(dh, x, g.reshape(1, d), dres)


def _gate_fwd(u, yc, ya, gate_col, name):
    t, d = yc.shape
    td = math.gcd(_tile(d, 512), gate_col)
    nd = d // td
    c0 = gate_col // td

    def body(gc_ref, ga_ref, yc_ref, ya_ref, m_ref):
        m_ref[...] = (_sigmoid(gc_ref[...]) * yc_ref[...] + _sigmoid(ga_ref[...]) * ya_ref[...]).astype(BF16)

    blk = pl.BlockSpec((ROWS, td), lambda i, j: (i, j))
    return pl.pallas_call(
        body, name=name, grid=(t // ROWS, nd),
        in_specs=[pl.BlockSpec((ROWS, td), lambda i, j: (i, c0 + j)),
                  pl.BlockSpec((ROWS, td), lambda i, j: (i, c0 + nd + j)), blk, blk],
        out_specs=blk, out_shape=jax.ShapeDtypeStruct((t, d), BF16),
        compiler_params=_cparams(("arbitrary", "arbitrary")),
    )(u, u, yc, ya)


def _gate_bwd(dm, u, yc, ya, gate_col, name):
    t, d = yc.shape
    td = math.gcd(_tile(d, 512), gate_col)
    nd = d // td
    c0 = gate_col // td

    def body(dm_ref, gc_ref, ga_ref, yc_ref, ya_ref, dyc_ref, dya_ref, dugc_ref, duga_ref):
        dmv = dm_ref[...]
        gc = _sigmoid(gc_ref[...])
        ga = _sigmoid(ga_ref[...])
        dyc_ref[...] = (dmv * gc).astype(BF16)
        dya_ref[...] = (dmv * ga).astype(BF16)
        dugc_ref[...] = (dmv * yc_ref[...] * gc * (1.0 - gc)).astype(BF16)
        duga_ref[...] = (dmv * ya_ref[...] * ga * (1.0 - ga)).astype(BF16)

    blk = pl.BlockSpec((ROWS, td), lambda i, j: (i, j))
    o = jax.ShapeDtypeStruct((t, d), BF16)
    return pl.pallas_call(
        body, name=name, grid=(t // ROWS, nd),
        in_specs=[blk, pl.BlockSpec((ROWS, td), lambda i, j: (i, c0 + j)),
                  pl.BlockSpec((ROWS, td), lambda i, j: (i, c0 + nd + j)), blk, blk],
        out_specs=(blk, blk, blk, blk), out_shape=(o, o, o, o),
        compiler_params=_cparams(("arbitrary", "arbitrary")),
    )(dm, u, u, yc, ya)


def _loss_and_grad(y, target, name):
    t, d = y.shape
    n = t // ROWS

    def body(y_ref, t_ref, loss_ref, dy_ref, acc_ref):
        i = pl.program_id(0)

        @pl.when(i == 0)
        def _():
            acc_ref[...] = jnp.zeros_like(acc_ref)

        diff = y_ref[...] - t_ref[...]
        dy_ref[...] = diff * (1.0 / d)
        acc_ref[...] += jnp.sum(diff * diff, axis=0, keepdims=True)

        @pl.when(i == n - 1)
        def _():
            loss_ref[...] = jnp.sum(acc_ref[...], axis=-1, keepdims=True) * (0.5 / d)

    row = pl.BlockSpec((ROWS, d), lambda i: (i, 0))
    return pl.pallas_call(
        body, name=name, grid=(n,), in_specs=[row, row],
        out_specs=(pl.BlockSpec((1, 1), lambda i: (0, 0)), row),
        out_shape=(jax.ShapeDtypeStruct((1, 1), F32), jax.ShapeDtypeStruct((t, d), F32)),
        scratch_shapes=[pltpu.VMEM((1, d), F32)], compiler_params=_cparams(("arbitrary",)),
    )(y, target)


HALO = 32


def _conv_fwd(u, w, b, cdim, name):
    t = u.shape[0]
    ncb = cdim // LANES
    nt = t // BLOCK

    def body(a_ref, g_ref, w_ref, b_ref, zc_ref, zpad):
        zpad[0:HALO, :] = jnp.zeros((HALO, LANES), F32)
        zpad[HALO:HALO + t, :] = a_ref[...] * _sigmoid(g_ref[...])
        wv = w_ref[...]
        bv = b_ref[...]

        def tile(i, carry):
            r0 = pl.multiple_of(i * BLOCK, BLOCK)
            win = zpad[pl.ds(r0, BLOCK + HALO), :]
            acc = jnp.zeros((BLOCK, LANES), F32) + bv
            for j in range(CONV_WIDTH):
                s = CONV_WIDTH - 1 - j
                sh = win if s == 0 else pltpu.roll(win, s, 0)
                acc = acc + wv[j:j + 1, :] * sh[HALO:HALO + BLOCK, :]
            zc_ref[pl.ds(r0, BLOCK), :] = acc
            return carry

        lax.fori_loop(0, nt, tile, 0)

    col = lambda off: pl.BlockSpec((t, LANES), lambda c: (0, off + c))
    return pl.pallas_call(
        body, name=name, grid=(ncb,),
        in_specs=[col(0), col(ncb), pl.BlockSpec((CONV_TAPS_PADDED, LANES), lambda c: (0, c)),
                  pl.BlockSpec((1, LANES), lambda c: (0, c))],
        out_specs=pl.BlockSpec((t, LANES), lambda c: (0, c)),
        out_shape=jax.ShapeDtypeStruct((t, cdim), F32),
        scratch_shapes=[pltpu.VMEM((t + HALO, LANES), F32)], compiler_params=_cparams(("arbitrary",)),
    )(u, u, w, b.reshape(1, cdim))


def _conv_bwd(dzc, u, w, cdim, name):
    t = u.shape[0]
    ncb = cdim // LANES
    nt = t // BLOCK
    win_rows = BLOCK + HALO

    def body(dzc_ref, a_ref, g_ref, w_ref, da_ref, dg_ref, dw_ref, db_ref, zpad, dpad):
        av = a_ref[...]
        sg = _sigmoid(g_ref[...])
        zpad[0:HALO, :] = jnp.zeros((HALO, LANES), F32)
        zpad[HALO:HALO + t, :] = av * sg
        dpad[0:t, :] = dzc_ref[...]
        dpad[t:t + HALO, :] = jnp.zeros((HALO, LANES), F32)
        dw_ref[...] = jnp.zeros_like(dw_ref)
        db_ref[...] = jnp.sum(dzc_ref[...], axis=0, keepdims=True)
        wv = w_ref[...]

        def tile(i, carry):
            r0 = pl.multiple_of(i * BLOCK, BLOCK)
            zwin = zpad[pl.ds(r0, win_rows), :]
            dwin = dpad[pl.ds(r0, win_rows), :]
            dcur = dwin[0:BLOCK, :]
            dz = jnp.zeros((BLOCK, LANES), F32)
            for j in range(CONV_WIDTH):
                s = CONV_WIDTH - 1 - j
                zs = zwin if s == 0 else pltpu.roll(zwin, s, 0)
                dw_ref[j:j + 1, :] += jnp.sum(dcur * zs[HALO:HALO + BLOCK, :], axis=0, keepdims=True)
                ds = dwin if s == 0 else pltpu.roll(dwin, win_rows - s, 0)
                dz = dz + wv[j:j + 1, :] * ds[0:BLOCK, :]
            ac = a_ref[pl.ds(r0, BLOCK), :]
            sc = _sigmoid(g_ref[pl.ds(r0, BLOCK), :])
            da_ref[pl.ds(r0, BLOCK), :] = (dz * sc).astype(BF16)
            dg_ref[pl.ds(r0, BLOCK), :] = (dz * ac * sc * (1.0 - sc)).astype(BF16)
            return carry

        lax.fori_loop(0, nt, tile, 0)

    col = lambda off: pl.BlockSpec((t, LANES), lambda c: (0, off + c))
    wspec = pl.BlockSpec((CONV_TAPS_PADDED, LANES), lambda c: (0, c))
    o = jax.ShapeDtypeStruct((t, cdim), BF16)
    return pl.pallas_call(
        body, name=name, grid=(ncb,), in_specs=[col(0), col(0), col(ncb), wspec],
        out_specs=(col(0), col(0), wspec, pl.BlockSpec((1, LANES), lambda c: (0, c))),
        out_shape=(o, o, jax.ShapeDtypeStruct((CONV_TAPS_PADDED, cdim), F32), jax.ShapeDtypeStruct((1, cdim), F32)),
        scratch_shapes=[pltpu.VMEM((t + HALO, LANES), F32), pltpu.VMEM((t + HALO, LANES), F32)],
        compiler_params=_cparams(("arbitrary",)),
    )(dzc, u, u, w)


def _ln_swish_fwd(zc, g, b, name):
    t, c = zc.shape

    def body(z_ref, g_ref, b_ref, o_ref):
        z = z_ref[...]
        mu = jnp.mean(z, axis=-1, keepdims=True)
        zc_ = z - mu
        zn = zc_ * lax.rsqrt(jnp.mean(zc_ * zc_, axis=-1, keepdims=True) + EPS)
        y = zn * g_ref[...] + b_ref[...]
        o_ref[...] = (y * _sigmoid(y)).astype(BF16)

    row = pl.BlockSpec((ROWS, c), lambda i: (i, 0))
    vec = pl.BlockSpec((1, c), lambda i: (0, 0))
    return pl.pallas_call(
        body, name=name, grid=(t // ROWS,), in_specs=[row, vec, vec], out_specs=row,
        out_shape=jax.ShapeDtypeStruct((t, c), BF16), compiler_params=_cparams(("arbitrary",)),
    )(zc, g.reshape(1, c), b.reshape(1, c))


def _ln_swish_bwd(dzs, zc, g, b, name):
    t, c = zc.shape

    def body(d_ref, z_ref, g_ref, b_ref, dz_ref, dg_ref, db_ref):
        @pl.when(pl.program_id(0) == 0)
        def _():
            dg_ref[...] = jnp.zeros_like(dg_ref)
            db_ref[...] = jnp.zeros_like(db_ref)

        z = z_ref[...]
        mu = jnp.mean(z, axis=-1, keepdims=True)
        zc_ = z - mu
        rstd = lax.rsqrt(jnp.mean(zc_ * zc_, axis=-1, keepdims=True) + EPS)
        zn = zc_ * rstd
        y = zn * g_ref[...] + b_ref[...]
        sg = _sigmoid(y)
        dy = d_ref[...] * (sg * (1.0 + y * (1.0 - sg)))
        dg_ref[...] += jnp.sum(dy * zn, axis=0, keepdims=True)
        db_ref[...] += jnp.sum(dy, axis=0, keepdims=True)
        dzn = dy * g_ref[...]
        dz_ref[...] = rstd * (dzn - jnp.mean(dzn, axis=-1, keepdims=True)
                              - zn * jnp.mean(dzn * zn, axis=-1, keepdims=True))

    row = pl.BlockSpec((ROWS, c), lambda i: (i, 0))
    vec = pl.BlockSpec((1, c), lambda i: (0, 0))
    v = jax.ShapeDtypeStruct((1, c), F32)
    return pl.pallas_call(
        body, name=name, grid=(t // ROWS,), in_specs=[row, row, vec, vec], out_specs=(row, vec, vec),
        out_shape=(jax.ShapeDtypeStruct((t, c), F32), v, v), compiler_params=_cparams(("arbitrary",)),
    )(dzs, zc, g.reshape(1, c), b.reshape(1, c))


def _bucket_table():
    qi = np.arange(BLOCK)[:, None]
    kj = np.arange(2 * BLOCK)[None, :]
    off = qi + BLOCK - kj
    band = (off >= 0) & (off <= SUB_WINDOW)
    max_exact = NUM_BUCKETS // 2
    out = []
    for d in DILATIONS:
        dist = (np.clip(off, 0, SUB_WINDOW) * d).astype(np.int32)
        nf = np.maximum(dist, 1).astype(np.float32)
        large = max_exact + (np.log(nf / np.float32(max_exact)) / np.float32(math.log(MAX_REL_DISTANCE / max_exact))
                             * np.float32(NUM_BUCKETS - max_exact)).astype(np.int32)
        large = np.minimum(large, NUM_BUCKETS - 1)
        bucket = np.where(dist < max_exact, dist, large)
        out.append(np.where(band, bucket, -1))
    return np.stack(out).astype(np.int32)


def _bias_expand(rel_bias, buckets, hpg, name):
    nh = N_GROUPS * hpg

    def body(rb_ref, bk_ref, o_ref):
        h = pl.program_id(0)
        bk = bk_ref[0]
        acc = jnp.full((BLOCK, 2 * BLOCK), NEG_INF, F32)
        for bb in range(NUM_BUCKETS):
            acc = jnp.where(bk == bb, rb_ref[bb, h], acc)
        o_ref[0] = acc

    return pl.pallas_call(
        body, name=name, grid=(nh,),
        in_specs=[pl.BlockSpec(memory_space=pltpu.SMEM),
                  pl.BlockSpec((1, BLOCK, 2 * BLOCK), lambda h: (h // hpg, 0, 0))],
        out_specs=pl.BlockSpec((1, BLOCK, 2 * BLOCK), lambda h: (h, 0, 0)),
        out_shape=jax.ShapeDtypeStruct((nh, BLOCK, 2 * BLOCK), F32), compiler_params=_cparams(("arbitrary",)),
    )(rel_bias, buckets)


def _bias_reduce(ds_sum, buckets, hpg, name):
    nh = N_GROUPS * hpg

    def body(ds_ref, bk_ref, o_ref):
        bk = bk_ref[0]
        dsv = ds_ref[0]
        lane = lax.broadcasted_iota(jnp.int32, (1, LANES), 1)
        row = jnp.zeros((1, LANES), F32)
        for bb in range(NUM_BUCKETS):
            tot = jnp.sum(jnp.sum(jnp.where(bk == bb, dsv, 0.0), axis=-1, keepdims=True), axis=0, keepdims=True)
            row = jnp.where(lane == bb, tot, row)
        o_ref[0] = row

    return pl.pallas_call(
        body, name=name, grid=(nh,),
        in_specs=[pl.BlockSpec((1, BLOCK, 2 * BLOCK), lambda h: (h, 0, 0)),
                  pl.BlockSpec((1, BLOCK, 2 * BLOCK), lambda h: (h // hpg, 0, 0))],
        out_specs=pl.BlockSpec((1, 1, LANES), lambda h: (h, 0, 0)),
        out_shape=jax.ShapeDtypeStruct((nh, 1, LANES), F32), compiler_params=_cparams(("arbitrary",)),
    )(ds_sum, buckets)


def _chunk_rows(c, d, nb):
    r, n = c // nb, c % nb
    if d == 1:
        return pl.ds(c * BLOCK, BLOCK)
    return pl.ds(r + n * BLOCK * d, BLOCK, stride=d)


def _segment_ones():
    i = lax.broadcasted_iota(jnp.int32, (LANES, LANES), 0) // HEAD_DIM
    j = lax.broadcasted_iota(jnp.int32, (LANES, LANES), 1) // HEAD_DIM
    return (i == j).astype(F32)


def _head_mean(v, seg):
    return jnp.dot(v, seg, precision=HIGHEST, preferred_element_type=F32) * (1.0 / HEAD_DIM)


def _attn_fwd(u, qg2, kg2, bias, gi, cols, hpg, name):
    t = u.shape[0]
    d = DILATIONS[gi]
    nchunk = t // BLOCK
    nb = (t // d) // BLOCK
    hp = hpg // 2
    qc0, kc0, vc0 = [(c + gi * hpg * HEAD_DIM) // LANES for c in cols]
    contract_lanes = (((1,), (1,)), ((), ()))

    def body(q_ref, k_ref, v_ref, qg_ref, kg_ref, bias_ref, o_ref, lse_ref, qd, kd, vd, od, ld):
        seg = _segment_ones()
        lane = lax.broadcasted_iota(jnp.int32, (1, LANES), 1)
        qg = qg_ref[...] * (HEAD_DIM ** -0.5)
        kg = kg_ref[...]
        kd[0:BLOCK, :] = jnp.zeros((BLOCK, LANES), BF16)
        vd[0:BLOCK, :] = jnp.zeros((BLOCK, LANES), BF16)
        for c in range(nchunk):
            rows = _chunk_rows(c, d, nb)
            qv = q_ref[rows, :]
            kv = k_ref[rows, :]
            qd[c * BLOCK:(c + 1) * BLOCK, :] = (qv * lax.rsqrt(_head_mean(qv * qv, seg) + EPS) * qg).astype(BF16)
            kd[(c + 1) * BLOCK:(c + 2) * BLOCK, :] = (kv * lax.rsqrt(_head_mean(kv * kv, seg) + EPS) * kg).astype(BF16)
            vd[(c + 1) * BLOCK:(c + 2) * BLOCK, :] = v_ref[rows, :].astype(BF16)

        col = lax.broadcasted_iota(jnp.int32, (BLOCK, 2 * BLOCK), 1)

        def chunk(c, carry):
            r0 = pl.multiple_of(c * BLOCK, BLOCK)
            qc = qd[pl.ds(r0, BLOCK), :]
            kw = kd[pl.ds(r0, 2 * BLOCK), :]
            vw = vd[pl.ds(r0, 2 * BLOCK), :]
            kill = jnp.logical_and(col < BLOCK, (c % nb) == 0)
            o_acc = jnp.zeros((BLOCK, LANES), F32)
            l_acc = jnp.zeros((BLOCK, LANES), F32)
            for j in range(2):
                mj = jnp.logical_and(lane >= j * HEAD_DIM, lane < (j + 1) * HEAD_DIM)
                kj = jnp.where(mj, kw, jnp.zeros_like(kw))
                s = lax.dot_general(qc, kj, contract_lanes, preferred_element_type=F32) + bias_ref[j]
                s = jnp.where(kill, NEG_INF, s)
                mx = jnp.max(s, axis=-1, keepdims=True)
                p = jnp.exp(s - mx)
                l = jnp.sum(p, axis=-1, keepdims=True)
                oj = jnp.dot((p / l).astype(BF16), vw, preferred_element_type=F32)
                o_acc = jnp.where(mj, oj, o_acc)
                l_acc = jnp.where(mj, mx + jnp.log(l), l_acc)
            od[pl.ds(r0, BLOCK), :] = o_acc
            ld[pl.ds(r0, BLOCK), :] = l_acc
            return carry

        lax.fori_loop(0, nchunk, chunk, 0)
        for c in range(nchunk):
            rows = _chunk_rows(c, d, nb)
            o_ref[rows, :] = od[c * BLOCK:(c + 1) * BLOCK, :]
            lse_ref[rows, :] = ld[c * BLOCK:(c + 1) * BLOCK, :]

    ucol = lambda c0: pl.BlockSpec((t, LANES), lambda h: (0, c0 + h))
    vec = pl.BlockSpec((1, LANES), lambda h: (0, 0))
    oblk = pl.BlockSpec((t, LANES), lambda h: (0, h))
    osh = jax.ShapeDtypeStruct((t, hpg * HEAD_DIM), F32)
    return pl.pallas_call(
        body, name=name, grid=(hp,),
        in_specs=[ucol(qc0), ucol(kc0), ucol(vc0), vec, vec,
                  pl.BlockSpec((2, BLOCK, 2 * BLOCK), lambda h: (gi * hp + h, 0, 0))],
        out_specs=(oblk, oblk), out_shape=(osh, osh),
        scratch_shapes=[pltpu.VMEM((t, LANES), BF16), pltpu.VMEM((t + BLOCK, LANES), BF16),
                        pltpu.VMEM((t + BLOCK, LANES), BF16), pltpu.VMEM((t, LANES), F32), pltpu.VMEM((t, LANES), F32)],
        compiler_params=_cparams(("arbitrary",)),
    )(u, u, u, qg2, kg2, bias)


def _attn_bwd(u, do_g, dd_g, lse_g, qg2, kg2, bias, ds_in, gi, cols, hpg, name):
    t = u.shape[0]
    d = DILATIONS[gi]
    nchunk = t // BLOCK
    nb = (t // d) // BLOCK
    hp = hpg // 2
    qc0, kc0, vc0 = [(c + gi * hpg * HEAD_DIM) // LANES for c in cols]
    contract_lanes = (((1,), (1,)), ((), ()))
    contract_rows = (((0,), (0,)), ((), ()))
    qscale = HEAD_DIM ** -0.5

    def body(q_ref, k_ref, v_ref, do_ref, dd_ref, lse_ref, qg_ref, kg_ref, bias_ref, dsin_ref,
             dq_ref, dk_ref, dv_ref, dgq_ref, dgk_ref, dsout_ref,
             qd, kd, vd, dod, ddd, ld, dqd, dkd, dvd, dsacc):
        seg = _segment_ones()
        lane = lax.broadcasted_iota(jnp.int32, (1, LANES), 1)
        qg = qg_ref[...] * qscale
        kg = kg_ref[...]
        kd[0:BLOCK, :] = jnp.zeros((BLOCK, LANES), BF16)
        vd[0:BLOCK, :] = jnp.zeros((BLOCK, LANES), BF16)
        dkd[...] = jnp.zeros_like(dkd)
        dvd[...] = jnp.zeros_like(dvd)
        dsacc[...] = jnp.zeros_like(dsacc)
        for c in range(nchunk):
            rows = _chunk_rows(c, d, nb)
            qv = q_ref[rows, :]
            kv = k_ref[rows, :]
            qd[c * BLOCK:(c + 1) * BLOCK, :] = (qv * lax.rsqrt(_head_mean(qv * qv, seg) + EPS) * qg).astype(BF16)
            kd[(c + 1) * BLOCK:(c + 2) * BLOCK, :] = (kv * lax.rsqrt(_head_mean(kv * kv, seg) + EPS) * kg).astype(BF16)
            vd[(c + 1) * BLOCK:(c + 2) * BLOCK, :] = v_ref[rows, :].astype(BF16)
            dod[c * BLOCK:(c + 1) * BLOCK, :] = do_ref[rows, :].astype(BF16)
            ddd[c * BLOCK:(c + 1) * BLOCK, :] = dd_ref[rows, :]
            ld[c * BLOCK:(c + 1) * BLOCK, :] = lse_ref[rows, :]

        col = lax.broadcasted_iota(jnp.int32, (BLOCK, 2 * BLOCK), 1)

        def chunk(c, carry):
            r0 = pl.multiple_of(c * BLOCK, BLOCK)
            qc = qd[pl.ds(r0, BLOCK), :]
            kw = kd[pl.ds(r0, 2 * BLOCK), :]
            vw = vd[pl.ds(r0, 2 * BLOCK), :]
            doc = dod[pl.ds(r0, BLOCK), :]
            ddc = ddd[pl.ds(r0, BLOCK), :]
            lc = ld[pl.ds(r0, BLOCK), :]
            kill = jnp.logical_and(col < BLOCK, (c % nb) == 0)
            dq_acc = jnp.zeros((BLOCK, LANES), F32)
            dkw = jnp.zeros((2 * BLOCK, LANES), F32)
            dvw = jnp.zeros((2 * BLOCK, LANES), F32)
            for j in range(2):
                mj = jnp.logical_and(lane >= j * HEAD_DIM, lane < (j + 1) * HEAD_DIM)
                first = lane == j * HEAD_DIM
                kj = jnp.where(mj, kw, jnp.zeros_like(kw))
                vj = jnp.where(mj, vw, jnp.zeros_like(vw))
                qj = jnp.where(mj, qc, jnp.zeros_like(qc))
                doj = jnp.where(mj, doc, jnp.zeros_like(doc))
                s = lax.dot_general(qc, kj, contract_lanes, preferred_element_type=F32) + bias_ref[j]
                s = jnp.where(kill, NEG_INF, s)
                lse_j = jnp.sum(jnp.where(first, lc, 0.0), axis=-1, keepdims=True)
                dd_j = jnp.sum(jnp.where(first, ddc, 0.0), axis=-1, keepdims=True)
                p = jnp.exp(s - lse_j)
                dp = lax.dot_general(doc, vj, contract_lanes, preferred_element_type=F32)
                ds = p * (dp + dd_j)
                dsacc[j] += ds
                dsb = ds.astype(BF16)
                dq_acc = dq_acc + jnp.dot(dsb, kj, preferred_element_type=F32)
                dkw = dkw + lax.dot_general(dsb, qj, contract_rows, preferred_element_type=F32)
                dvw = dvw + lax.dot_general(p.astype(BF16), doj, contract_rows, preferred_element_type=F32)
            dqd[pl.ds(r0, BLOCK), :] = dq_acc
            dkd[pl.ds(r0, 2 * BLOCK), :] += dkw
            dvd[pl.ds(r0, 2 * BLOCK), :] += dvw
            return carry

        lax.fori_loop(0, nchunk, chunk, 0)
        dsout_ref[...] = dsin_ref[...] + dsacc[...]

        dgq = jnp.zeros((1, LANES), F32)
        dgk = jnp.zeros((1, LANES), F32)
        for c in range(nchunk):
            rows = _chunk_rows(c, d, nb)
            qv = q_ref[rows, :]
            rq = lax.rsqrt(_head_mean(qv * qv, seg) + EPS)
            qh = qv * rq
            dy = dqd[c * BLOCK:(c + 1) * BLOCK, :]
            dgq = dgq + jnp.sum(dy * qh, axis=0, keepdims=True) * qscale
            dxh = dy * qg
            ddd[rows, :] = rq * (dxh - qh * _head_mean(dxh * qh, seg))
            kv = k_ref[rows, :]
            rk = lax.rsqrt(_head_mean(kv * kv, seg) + EPS)
            kh = kv * rk
            dy = dkd[(c + 1) * BLOCK:(c + 2) * BLOCK, :]
            dgk = dgk + jnp.sum(dy * kh, axis=0, keepdims=True)
            dxh = dy * kg
            ld[rows, :] = rk * (dxh - kh * _head_mean(dxh * kh, seg))
        dq_ref[...] = ddd[...].astype(BF16)
        dk_ref[...] = ld[...].astype(BF16)
        for c in range(nchunk):
            ddd[_chunk_rows(c, d, nb), :] = dvd[(c + 1) * BLOCK:(c + 2) * BLOCK, :]
        dv_ref[...] = ddd[...].astype(BF16)
        dgq_ref[0] = dgq
        dgk_ref[0] = dgk

    ucol = lambda c0: pl.BlockSpec((t, LANES), lambda h: (0, c0 + h))
    vec = pl.BlockSpec((1, LANES), lambda h: (0, 0))
    oblk = pl.BlockSpec((t, LANES), lambda h: (0, h))
    bblk = pl.BlockSpec((2, BLOCK, 2 * BLOCK), lambda h: (gi * hp + h, 0, 0))
    gblk = pl.BlockSpec((1, 1, LANES), lambda h: (h, 0, 0))
    osh = jax.ShapeDtypeStruct((t, hpg * HEAD_DIM), BF16)
    gsh = jax.ShapeDtypeStruct((hp, 1, LANES), F32)
    return pl.pallas_call(
        body, name=name, grid=(hp,),
        in_specs=[ucol(qc0), ucol(kc0), ucol(vc0), oblk, oblk, oblk, vec, vec, bblk, bblk],
        out_specs=(oblk, oblk, oblk, gblk, gblk, bblk),
        out_shape=(osh, osh, osh, gsh, gsh, jax.ShapeDtypeStruct(ds_in.shape, F32)),
        input_output_aliases={9: 5},
        scratch_shapes=[pltpu.VMEM((t, LANES), BF16), pltpu.VMEM((t + BLOCK, LANES), BF16),
                        pltpu.VMEM((t + BLOCK, LANES), BF16), pltpu.VMEM((t, LANES), BF16),
                        pltpu.VMEM((t, LANES), F32), pltpu.VMEM((t, LANES), F32), pltpu.VMEM((t, LANES), F32),
                        pltpu.VMEM((t + BLOCK, LANES), F32), pltpu.VMEM((t + BLOCK, LANES), F32),
                        pltpu.VMEM((2, BLOCK, 2 * BLOCK), F32)],
        compiler_params=_cparams(("arbitrary",)),
    )(u, u, u, do_g, dd_g, lse_g, qg2, kg2, bias, ds_in)


def _group_weights(l0, l1, l2):
    mx = jnp.maximum(jnp.maximum(l0, l1), l2)
    e0, e1, e2 = jnp.exp(l0 - mx), jnp.exp(l1 - mx), jnp.exp(l2 - mx)
    inv = 1.0 / (e0 + e1 + e2)
    return e0 * inv, e1 * inv, e2 * inv


def _combine_fwd(os_, lses, name):
    t, ao = os_[0].shape

    def body(o0, o1, o2, l0, l1, l2, o_ref):
        w0, w1, w2 = _group_weights(l0[...], l1[...], l2[...])
        o_ref[...] = (w0 * o0[...] + w1 * o1[...] + w2 * o2[...]).astype(BF16)

    row = pl.BlockSpec((ROWS, ao), lambda i: (i, 0))
    return pl.pallas_call(
        body, name=name, grid=(t // ROWS,), in_specs=[row] * 6, out_specs=row,
        out_shape=jax.ShapeDtypeStruct((t, ao), BF16), compiler_params=_cparams(("arbitrary",)),
    )(*os_, *lses)


def _combine_bwd(do, os_, lses, name):
    t, ao = do.shape
    idx = np.arange(ao) // HEAD_DIM
    seg = jnp.asarray((idx[:, None] == idx[None, :]).astype(np.float32))

    def body(do_ref, o0, o1, o2, l0, l1, l2, seg_ref, g0, g1, g2, d0, d1, d2):
        w0, w1, w2 = _group_weights(l0[...], l1[...], l2[...])
        dov = do_ref[...]
        o = w0 * o0[...] + w1 * o1[...] + w2 * o2[...]
        sd = jnp.dot(dov * o, seg_ref[...], precision=HIGHEST, preferred_element_type=F32)
        for w, gref, dref in ((w0, g0, d0), (w1, g1, d1), (w2, g2, d2)):
            gref[...] = w * dov
            dref[...] = -(w * sd)

    row = pl.BlockSpec((ROWS, ao), lambda i: (i, 0))
    sh = jax.ShapeDtypeStruct((t, ao), F32)
    outs = pl.pallas_call(
        body, name=name, grid=(t // ROWS,), in_specs=[row] * 7 + [pl.BlockSpec((ao, ao), lambda i: (0, 0))],
        out_specs=(row,) * 6, out_shape=(sh,) * 6, compiler_params=_cparams(("arbitrary",)),
    )(do, *os_, *lses, seg)
    return outs[:3], outs[3:]


def _adamw(w, g, m, v, name):
    shape = w.shape
    cols = shape[-1]
    rows = int(np.prod(shape[:-1]))
    tr = rows if rows <= 512 else _tile_rows(rows)
    c1 = 1.0 - ADAM_B1 ** ADAM_STEP
    c2 = 1.0 - ADAM_B2 ** ADAM_STEP

    def body(w_ref, g_ref, m_ref, v_ref, d_ref, nm_ref, nv_ref):
        gv = g_ref[...]
        mn = ADAM_B1 * m_ref[...] + (1.0 - ADAM_B1) * gv
        vn = ADAM_B2 * v_ref[...] + (1.0 - ADAM_B2) * (gv * gv)
        nm_ref[...] = mn
        nv_ref[...] = vn
        d_ref[...] = -ADAM_LR * ((mn / c1) / (jnp.sqrt(vn / c2) + ADAM_EPS) + ADAM_WD * w_ref[...])

    blk = pl.BlockSpec((tr, cols), lambda i: (i, 0))
    sh = jax.ShapeDtypeStruct((rows, cols), F32)
    outs = pl.pallas_call(
        body, name=name, grid=(rows // tr,), in_specs=[blk] * 4, out_specs=(blk,) * 3, out_shape=(sh,) * 3,
        compiler_params=_cparams(("arbitrary",)),
    )(*[a.reshape(rows, cols) for a in (w, g, m, v)])
    return tuple(o.reshape(shape) for o in outs)


def _tile_rows(rows):
    for t in (512, 256, 128, 64, 32, 16, 8):
        if rows % t == 0:
            return t
    return rows


def _sum_slots(recv, name):
    _, rows, cols = recv.shape
    tr = rows if rows <= 512 else _tile_rows(rows)

    def body(r_ref, o_ref):
        acc = r_ref[0].astype(F32)
        for s in range(1, N_DEV):
            acc = acc + r_ref[s].astype(F32)
        o_ref[...] = acc

    return pl.pallas_call(
        body, name=name, grid=(rows // tr,), in_specs=[pl.BlockSpec((N_DEV, tr, cols), lambda i: (0, i, 0))],
        out_specs=pl.BlockSpec((tr, cols), lambda i: (i, 0)), out_shape=jax.ShapeDtypeStruct((rows, cols), F32),
        compiler_params=_cparams(("arbitrary",)),
    )(recv)


def _peer(k):
    x, y, c = lax.axis_index("x"), lax.axis_index("y"), lax.axis_index("c")
    return (1 - x if k & 4 else x, 1 - y if k & 2 else y, 1 - c if k & 1 else c)


def _dev_index(p):
    return 4 * p[0] + 2 * p[1] + p[2]


HBM_SPEC = pl.BlockSpec(memory_space=pltpu.HBM)
SEM_SPEC = pl.BlockSpec(memory_space=pltpu.SEMAPHORE)
ANY_SPEC = pl.BlockSpec(memory_space=pl.ANY)
CHIPS = (4, 2, 6)


def _remote(src, dst, send_sem, recv_sem, to):
    return pltpu.make_async_remote_copy(src_ref=src, dst_ref=dst, send_sem=send_sem, recv_sem=recv_sem,
                                        device_id=to, device_id_type=MESH)


def _hbm(a):
    return pltpu.with_memory_space_constraint(a, pltpu.HBM)


def _split_call(body, name, bufs, sems_in, sem_out_sizes, after):
    nb, ns, no = len(bufs), len(sems_in), len(sem_out_sizes)
    extra = [] if after is None else [after]

    def kern(*refs):
        pos = nb + ns + len(extra)
        body(refs[:nb], refs[nb:nb + ns], refs[pos:pos + no])
        token_ref = refs[pos + no + nb]
        token_ref[...] = jnp.zeros_like(token_ref)

    out_shape = (tuple(pltpu.SemaphoreType.DMA((s,)) for s in sem_out_sizes)
                 + tuple(pltpu.HBM(b.shape, b.dtype) for b in bufs) + (jax.ShapeDtypeStruct((8, LANES), F32),))
    res = pl.pallas_call(
        kern, name=name, out_shape=out_shape,
        in_specs=[HBM_SPEC] * nb + [SEM_SPEC] * ns + [ANY_SPEC] * len(extra),
        out_specs=(SEM_SPEC,) * no + (HBM_SPEC,) * nb + (pl.BlockSpec(memory_space=pltpu.VMEM),),
        input_output_aliases={i: no + i for i in range(nb)},
        compiler_params=pltpu.CompilerParams(has_side_effects=pltpu.SideEffectType.DATAFLOW_SIDE_EFFECTING),
    )(*bufs, *sems_in, *extra)
    return res[:no], res[no:no + nb], res[no + nb]


def _gather_start(shards, lands, after, name):
    n = len(shards)

    def body(bufs, _, sems):
        ins, lnd = bufs[:n], bufs[n:]
        d2d_s, d2d_r, ici_s, ici_r = sems
        me = _dev_index(_peer(0))
        for j, k in enumerate(CHIPS):
            for i in range(n):
                _remote(ins[i], lnd[i].at[me], ici_s.at[j], ici_r.at[j], _peer(k)).start()
        for i in range(n):
            _remote(ins[i], lnd[i].at[me], d2d_s.at[0], d2d_r.at[0], _peer(1)).start()

    return _split_call(body, name, [_hbm(a) for a in (*shards, *lands)], [], (1, 1, 3, 3), after)


def _gather_forward(n, bufs, ici_r, after, name):
    def body(refs, sems_in, sems):
        ins, lnd = refs[:n], refs[n:]
        (arrived,) = sems_in
        fwd_s, fwd_r = sems
        for j, k in enumerate(CHIPS):
            blk = _dev_index(_peer(k))
            for i in range(n):
                _remote(ins[i], lnd[i].at[blk], fwd_s.at[j], arrived.at[j], _peer(k)).wait_recv()
            for i in range(n):
                _remote(lnd[i].at[blk], lnd[i].at[blk], fwd_s.at[j], fwd_r.at[j], _peer(1)).start()

    return _split_call(body, name, bufs, [ici_r], (3, 3), after)


def _gather_finish(n, bufs, d2d_s, d2d_r, ici_s, fwd_s, fwd_r, after, name):
    def body(refs, sems_in, _):
        ins, lnd = refs[:n], refs[n:]
        d2d_send, d2d_recv, ici_send, fwd_send, fwd_recv = sems_in
        sib = _peer(1)
        for i in range(n):
            cp = _remote(ins[i], lnd[i].at[_dev_index(sib)], d2d_send.at[0], d2d_recv.at[0], sib)
            cp.wait_send()
            cp.wait_recv()
        for j, k in enumerate(CHIPS):
            passed = _dev_index(_peer(k))
            landed = _dev_index(_peer(k | 1))
            for i in range(n):
                _remote(ins[i], lnd[i].at[passed], ici_send.at[j], fwd_recv.at[j], _peer(k)).wait_send()
                cp = _remote(lnd[i].at[passed], lnd[i].at[landed], fwd_send.at[j], fwd_recv.at[j], sib)
                cp.wait_send()
                cp.wait_recv()

    _, out, token = _split_call(body, name, bufs, [d2d_s, d2d_r, ici_s, fwd_s, fwd_r], (), after)
    return out[n:], token


def _exchange_start(parts, lands, after, name):
    n = len(parts)

    def body(bufs, _, sems):
        src, lnd = bufs[:n], bufs[n:]
        send, recv = sems
        me = _dev_index(_peer(0))
        for k in (4, 5, 2, 3, 6, 7, 1):
            to = _peer(k)
            for i in range(n):
                _remote(src[i].at[_dev_index(to)], lnd[i].at[me], send.at[k - 1], recv.at[k - 1], to).start()

    return _split_call(body, name, [_hbm(a) for a in (*parts, *lands)], [], (7, 7), after)


def _exchange_finish(n, bufs, send, recv, after, name):
    def body(refs, sems_in, _):
        src, lnd = refs[:n], refs[n:]
        send_, recv_ = sems_in
        me = _dev_index(_peer(0))
        for k in range(1, N_DEV):
            frm = _peer(k)
            for i in range(n):
                cp = _remote(src[i].at[me], lnd[i].at[_dev_index(frm)], send_.at[k - 1], recv_.at[k - 1], frm)
                cp.wait_send()
                cp.wait_recv()

    _, out, token = _split_call(body, name, bufs, [send, recv], (), after)
    return out[n:], token


def _all_reduce_small(v, name):
    rows = v.shape[0]

    def body(v_ref, o_ref, buf, send_sems, recv_sems):
        me = _dev_index(_peer(0))
        buf[me] = v_ref[...]
        copies = []
        for k in range(1, N_DEV):
            copies.append(pltpu.make_async_remote_copy(
                src_ref=v_ref, dst_ref=buf.at[me], send_sem=send_sems.at[k - 1], recv_sem=recv_sems.at[k - 1],
                device_id=_peer(k), device_id_type=MESH))
        for cp in copies:
            cp.start()
        for k in range(1, N_DEV):
            pltpu.make_async_remote_copy(
                src_ref=v_ref, dst_ref=buf.at[_dev_index(_peer(k))], send_sem=send_sems.at[k - 1],
                recv_sem=recv_sems.at[k - 1], device_id=_peer(k), device_id_type=MESH).wait_recv()
        for cp in copies:
            cp.wait_send()
        acc = buf[0]
        for s in range(1, N_DEV):
            acc = acc + buf[s]
        o_ref[...] = acc

    vm = pl.BlockSpec(memory_space=pltpu.VMEM)
    return pl.pallas_call(
        body, name=name, in_specs=[vm], out_specs=vm, out_shape=jax.ShapeDtypeStruct(v.shape, F32),
        scratch_shapes=[pltpu.VMEM((N_DEV, rows, LANES), F32), pltpu.SemaphoreType.DMA((7,)),
                        pltpu.SemaphoreType.DMA((7,))],
    )(v)


def _columns(cdim, ao):
    q_col = 2 * cdim
    attn_dim = N_GROUPS * ao
    return (q_col, q_col + attn_dim, q_col + 2 * attn_dim), q_col + 3 * attn_dim


def _layer_fwd(x, sm, bg, bias):
    cdim = sm["conv_ln_g"].shape[0]
    ao = bg["w_attn_out"].shape[0]
    hpg = ao // HEAD_DIM
    cols, gate_col = _columns(cdim, ao)
    qg2 = jnp.tile(sm["q_norm_g"], 2).reshape(1, LANES)
    kg2 = jnp.tile(sm["k_norm_g"], 2).reshape(1, LANES)
    h1 = _rms_fwd(x, sm["norm1_g"], "rms1_fwd")
    u = _mm(h1, bg["w_in"], name="mm_in")
    zc = _conv_fwd(u, bg["conv_dw_w"], sm["conv_dw_b"], cdim, "conv_fwd")
    zs = _ln_swish_fwd(zc, sm["conv_ln_g"], sm["conv_ln_b"], "ln_swish_fwd")
    yc = _mm(zs, bg["w_conv_out"], name="mm_conv_out")
    os_, lses = [], []
    for gi in range(N_GROUPS):
        o_g, lse_g = _attn_fwd(u, qg2, kg2, bias, gi, cols, hpg, "attn_fwd_g%d" % gi)
        os_.append(o_g)
        lses.append(lse_g)
    o = _combine_fwd(os_, lses, "combine_fwd")
    ya = _mm(o, bg["w_attn_out"], name="mm_attn_out")
    mg = _gate_fwd(u, yc, ya, gate_col, "gate_fwd")
    x1 = _mm(mg, bg["w_out"], epi="res", extra=x, name="mm_out")
    h2 = _rms_fwd(x1, sm["norm2_g"], "rms2_fwd")
    f, act = _mm(h2, bg["w_ff1"], epi="relu2", name="mm_ff1")
    x2 = _mm(act, bg["w_ff2"], epi="res", extra=x1, name="mm_ff2")
    saved = dict(x=x, h1=h1, u=u, zc=zc, zs=zs, yc=yc, os=os_, lses=lses, o=o, ya=ya, mg=mg, x1=x1, h2=h2, f=f,
                 act=act, qg2=qg2, kg2=kg2)
    return x2, saved


def _layer_bwd(dx, s, sm, bg, bias, ds_sum, after):
    cdim = sm["conv_ln_g"].shape[0]
    ao = bg["w_attn_out"].shape[0]
    hpg = ao // HEAD_DIM
    hp = hpg // 2
    cols, gate_col = _columns(cdim, ao)
    g = {}
    df = _mm(dx, bg["w_ff2"], tb=True, epi="drelu2", extra=s["f"], out_dtype=BF16, after=after, name="mm_dff2")
    g["w_ff2"] = _mm(s["act"], dx, ta=True, name="mm_gw_ff2")
    g["w_ff1"] = _mm(s["h2"], df, ta=True, name="mm_gw_ff1")
    dh2 = _mm(df, bg["w_ff1"], tb=True, name="mm_dff1")
    dx1, dg2 = _rms_bwd(dh2, s["x1"], sm["norm2_g"], dx, "rms2_bwd")
    g["norm2_g"] = dg2[0]
    dmg = _mm(dx1, bg["w_out"], tb=True, name="mm_dout")
    g["w_out"] = _mm(s["mg"], dx1, ta=True, name="mm_gw_out")
    dyc, dya, dugc, duga = _gate_bwd(dmg, s["u"], s["yc"], s["ya"], gate_col, "gate_bwd")
    dzs = _mm(dyc, bg["w_conv_out"], tb=True, name="mm_dconv_out")
    g["w_conv_out"] = _mm(s["zs"], dyc, ta=True, name="mm_gw_conv_out")
    do = _mm(dya, bg["w_attn_out"], tb=True, name="mm_dattn_out")
    g["w_attn_out"] = _mm(s["o"], dya, ta=True, name="mm_gw_attn_out")
    dzc, dlg, dlb = _ln_swish_bwd(dzs, s["zc"], sm["conv_ln_g"], sm["conv_ln_b"], "ln_swish_bwd")
    g["conv_ln_g"] = dlg[0]
    g["conv_ln_b"] = dlb[0]
    da, dgt, dcw, dcb = _conv_bwd(dzc, s["u"], bg["conv_dw_w"], cdim, "conv_bwd")
    g["conv_dw_w"] = dcw[:CONV_WIDTH]
    g["conv_dw_b"] = dcb[0]
    do_gs, dd_gs = _combine_bwd(do, s["os"], s["lses"], "combine_bwd")
    dqs, dks, dvs = [], [], []
    dgq = jnp.zeros((HEAD_DIM,), F32)
    dgk = jnp.zeros((HEAD_DIM,), F32)
    for gi in range(N_GROUPS):
        dq, dk, dv, gq, gk, ds_sum = _attn_bwd(s["u"], do_gs[gi], dd_gs[gi], s["lses"][gi], s["qg2"], s["kg2"],
                                               bias, ds_sum, gi, cols, hpg, "attn_bwd_g%d" % gi)
        dqs.append(dq)
        dks.append(dk)
        dvs.append(dv)
        dgq = dgq + jnp.sum(gq.reshape(hp * 2, HEAD_DIM), axis=0)
        dgk = dgk + jnp.sum(gk.reshape(hp * 2, HEAD_DIM), axis=0)
    g["q_norm_g"] = dgq
    g["k_norm_g"] = dgk
    du = jnp.concatenate([da, dgt] + dqs + dks + dvs + [dugc, duga], axis=1)
    g["w_in"] = _mm(s["h1"], du, ta=True, name="mm_gw_in")
    dh1 = _mm(du, bg["w_in"], tb=True, name="mm_din")
    dx0, dg1 = _rms_bwd(dh1, s["x"], sm["norm1_g"], dx1, "rms1_bwd")
    g["norm1_g"] = dg1[0]
    return dx0, g, ds_sum


BIG = ("w_in", "conv_dw_w", "w_conv_out", "w_attn_out", "w_out", "w_ff1", "w_ff2")
COL_SHARDED = ("w_in", "conv_dw_w", "w_conv_out", "w_attn_out", "w_ff1")
SMALL = ("rel_bias", "norm1_g", "q_norm_g", "k_norm_g", "conv_dw_b", "conv_ln_g", "conv_ln_b", "norm2_g")
WEIGHTS = ("rel_bias", "norm1_g", "w_in", "q_norm_g", "k_norm_g", "conv_dw_w", "conv_dw_b", "conv_ln_g", "conv_ln_b",
           "w_conv_out", "w_attn_out", "w_out", "norm2_g", "w_ff1", "w_ff2")


def _to_whole(name, gathered):
    n, a, b = gathered.shape
    if name in COL_SHARDED:
        return gathered.transpose(1, 0, 2).reshape(a, n * b)
    return gathered.reshape(n * a, b)


def _to_slots(name, whole):
    a, b = whole.shape
    if name in COL_SHARDED:
        return whole.reshape(a, N_DEV, b // N_DEV).transpose(1, 0, 2)
    return whole.reshape(N_DEV, a // N_DEV, b)


def _own_slot(block, me):
    land = lax.empty((N_DEV,) + block.shape, block.dtype)
    return lax.dynamic_update_slice(land, block[None], (me,) + (0,) * block.ndim)


def kernel(x, rel_bias, norm1_g, w_in, q_norm_g, k_norm_g, conv_dw_w, conv_dw_b, conv_ln_g, conv_ln_b, w_conv_out, w_attn_out, w_out, norm2_g, w_ff1, w_ff2, loss_target, m_rel_bias, m_norm1_g, m_w_in, m_q_norm_g, m_k_norm_g, m_conv_dw_w, m_conv_dw_b, m_conv_ln_g, m_conv_ln_b, m_w_conv_out, m_w_attn_out, m_w_out, m_norm2_g, m_w_ff1, m_w_ff2, v_rel_bias, v_norm1_g, v_w_in, v_q_norm_g, v_k_norm_g, v_conv_dw_w, v_conv_dw_b, v_conv_ln_g, v_conv_ln_b, v_w_conv_out, v_w_attn_out, v_w_out, v_norm2_g, v_w_ff1, v_w_ff2):
    w = dict(rel_bias=rel_bias, norm1_g=norm1_g, w_in=w_in, q_norm_g=q_norm_g, k_norm_g=k_norm_g, conv_dw_w=conv_dw_w,
             conv_dw_b=conv_dw_b, conv_ln_g=conv_ln_g, conv_ln_b=conv_ln_b, w_conv_out=w_conv_out,
             w_attn_out=w_attn_out, w_out=w_out, norm2_g=norm2_g, w_ff1=w_ff1, w_ff2=w_ff2)
    mom = dict(rel_bias=m_rel_bias, norm1_g=m_norm1_g, w_in=m_w_in, q_norm_g=m_q_norm_g, k_norm_g=m_k_norm_g,
               conv_dw_w=m_conv_dw_w, conv_dw_b=m_conv_dw_b, conv_ln_g=m_conv_ln_g, conv_ln_b=m_conv_ln_b,
               w_conv_out=m_w_conv_out, w_attn_out=m_w_attn_out, w_out=m_w_out, norm2_g=m_norm2_g, w_ff1=m_w_ff1,
               w_ff2=m_w_ff2)
    var = dict(rel_bias=v_rel_bias, norm1_g=v_norm1_g, w_in=v_w_in, q_norm_g=v_q_norm_g, k_norm_g=v_k_norm_g,
               conv_dw_w=v_conv_dw_w, conv_dw_b=v_conv_dw_b, conv_ln_g=v_conv_ln_g, conv_ln_b=v_conv_ln_b,
               w_conv_out=v_w_conv_out, w_attn_out=v_w_attn_out, w_out=v_w_out, norm2_g=v_norm2_g, w_ff1=v_w_ff1,
               w_ff2=v_w_ff2)

    depth = norm1_g.shape[0]
    n = len(BIG)
    me = 4 * lax.axis_index("x") + 2 * lax.axis_index("y") + lax.axis_index("c")
    hpg = w_attn_out.shape[1] // HEAD_DIM
    buckets = jnp.asarray(_bucket_table())
    bias = _bias_expand(rel_bias, buckets, hpg, "bias_expand")

    def gather_start(l, after):
        shards = [w[k][l] if k == "conv_dw_w" else w[k][l].astype(BF16) for k in BIG]
        return _gather_start(shards, [_own_slot(s, me) for s in shards], after, "gather_start_l%d" % l)

    xs = x[0]
    saved, bigs, smalls = [], [], []
    started = gather_start(0, None)
    for l in range(depth):
        (d2d_s, d2d_r, ici_s, ici_r), bufs, _ = started
        (fwd_s, fwd_r), bufs, token = _gather_forward(n, bufs, ici_r, None if l == 0 else xs, "gather_forward_l%d" % l)
        if l + 1 < depth:
            started = gather_start(l + 1, token)
            token = started[2]
        gathered, _ = _gather_finish(n, bufs, d2d_s, d2d_r, ici_s, fwd_s, fwd_r, token, "gather_finish_l%d" % l)
        bg = {k: _to_whole(k, a) for k, a in zip(BIG, gathered)}
        bg["conv_dw_w"] = jnp.pad(bg["conv_dw_w"], ((0, CONV_TAPS_PADDED - CONV_WIDTH), (0, 0)))
        sm = {k: w[k][l] for k in SMALL if k != "rel_bias"}
        xs, sv = _layer_fwd(xs, sm, bg, bias)
        saved.append(sv)
        bigs.append(bg)
        smalls.append(sm)

    loss, dx = _loss_and_grad(xs, loss_target[0], "loss")

    ds_sum = jnp.zeros((N_GROUPS * hpg, BLOCK, 2 * BLOCK), F32)
    g = {k: [None] * depth for k in SMALL if k != "rel_bias"}
    grad_layers = {k: [None] * depth for k in BIG}

    def exchange_finish(pending, after):
        l, (send, recv), bufs = pending
        recvd, token = _exchange_finish(n, bufs, send, recv, after, "exchange_finish_l%d" % l)
        for k, r in zip(BIG, recvd):
            shp = w[k].shape[1:]
            grad_layers[k][l] = _sum_slots(r.reshape(N_DEV, -1, shp[-1]), "sum_" + k).reshape(shp)
        return token

    pending, token = None, None
    for l in reversed(range(depth)):
        dx, gl, ds_sum = _layer_bwd(dx, saved[l], smalls[l], bigs[l], bias, ds_sum, token)
        for k in g:
            g[k][l] = gl[k]
        token = exchange_finish(pending, dx) if pending is not None else None
        parts = [_to_slots(k, gl[k]).astype(BF16) for k in BIG]
        lands = [_own_slot(lax.dynamic_index_in_dim(p, me, 0, keepdims=False), me) for p in parts]
        sems, bufs, token = _exchange_start(parts, lands, token, "exchange_start_l%d" % l)
        pending = (l, sems, bufs)
    exchange_finish(pending, None)
    grad_x = dx

    g = {k: jnp.stack(v) for k, v in g.items()}
    db = _bias_reduce(ds_sum, buckets, hpg, "bias_reduce")
    g["rel_bias"] = db[:, 0, :NUM_BUCKETS].T

    flat = jnp.concatenate([g[k].reshape(-1) for k in SMALL])
    nflat = flat.shape[0]
    rows = -(-nflat // (8 * LANES)) * 8
    packed = jnp.pad(flat, (0, rows * LANES - nflat)).reshape(rows, LANES)
    total = _all_reduce_small(packed, "reduce_small").reshape(-1)
    grad = {}
    off = 0
    for k in SMALL:
        size = int(np.prod(w[k].shape))
        grad[k] = total[off:off + size].reshape(w[k].shape)
        off += size

    for k in BIG:
        grad[k] = jnp.stack(grad_layers[k])

    loss = lax.psum(loss[0, 0], ("x", "y", "c"))
    outs = {k: _adamw(w[k], grad[k], mom[k], var[k], "adamw_" + k) for k in WEIGHTS}
    return (loss, grad_x[None], *[grad[k] for k in WEIGHTS], *[outs[k][0] for k in WEIGHTS],
            *[outs[k][1] for k in WEIGHTS], *[outs[k][2] for k in WEIGHTS])
```

```python
import functools
import math

import numpy as np
import jax
import jax.numpy as jnp
from jax import lax
from jax.experimental import pallas as pl
from jax.experimental.pallas import tpu as pltpu

F32 = jnp.float32
BF16 = jnp.bfloat16

HEAD_DIM = 64
N_GROUPS = 3
DILATIONS = (1, 4, 16)
SUB_WINDOW = 128
BLOCK = 128
CONV_WIDTH = 31
CONV_TAPS_PADDED = 32
NUM_BUCKETS = 32
MAX_REL_DISTANCE = 2048
EPS = 1e-6
NEG_INF = -1e30
LANES = 128

ADAM_LR = 0.001
ADAM_B1 = 0.9
ADAM_B2 = 0.999
ADAM_EPS = 1e-08
ADAM_WD = 0.01
ADAM_STEP = 10

N_DEV = 8
VMEM_LIMIT = 56 * 1024 * 1024
MESH = pl.DeviceIdType.MESH


def _cparams(sem=None):
    return pltpu.CompilerParams(dimension_semantics=sem, vmem_limit_bytes=VMEM_LIMIT)


def _tile(n, target):
    if n <= target:
        return n
    t = (target // LANES) * LANES
    while t >= LANES:
        if n % t == 0:
            return t
        t -= LANES
    return n


def _sigmoid(v):
    return 1.0 / (1.0 + jnp.exp(-v))


MM_VMEM_BUDGET = 40 * 1024 * 1024


def _mm_tiles(m, n, kdim, a_bytes, b_bytes, io_bytes):
    tm, tn, tk = _tile(m, 1024), _tile(n, 1024), _tile(kdim, 2048)

    def need(tm, tn, tk):
        blocks = 2 * (tm * tk * a_bytes + tk * tn * b_bytes + tm * tn * io_bytes)
        casts = (tm * tk * 2 if a_bytes == 4 else 0) + (tk * tn * 2 if b_bytes == 4 else 0)
        return blocks + casts + 2 * tm * tn * 4

    while need(tm, tn, tk) > MM_VMEM_BUDGET:
        if tk >= 1024 and tk % 256 == 0:
            tk //= 2
        elif tn >= 512 and tn % 256 == 0:
            tn //= 2
        else:
            tm //= 2
    return tm, tn, tk


def _mm(a, b, *, ta=False, tb=False, out_dtype=F32, epi=None, extra=None, after=None, name):
    m = a.shape[1] if ta else a.shape[0]
    kdim = a.shape[0] if ta else a.shape[1]
    n = b.shape[0] if tb else b.shape[1]
    io_bytes = (6 if epi == "relu2" else jnp.dtype(out_dtype).itemsize) + (4 if extra is not None else 0)
    tm, tn, tk = _mm_tiles(m, n, kdim, a.dtype.itemsize, b.dtype.itemsize, io_bytes)
    nk = kdim // tk
    a_spec = pl.BlockSpec((tk, tm), lambda i, j, k: (k, i)) if ta else pl.BlockSpec((tm, tk), lambda i, j, k: (i, k))
    b_spec = pl.BlockSpec((tn, tk), lambda i, j, k: (j, k)) if tb else pl.BlockSpec((tk, tn), lambda i, j, k: (k, j))
    o_spec = pl.BlockSpec((tm, tn), lambda i, j, k: (i, j))
    dims = (((0 if ta else 1,), (1 if tb else 0,)), ((), ()))
    n_in = 2 + (extra is not None) + (after is not None)
    n_out = 2 if epi == "relu2" else 1

    def body(*refs):
        a_ref, b_ref = refs[0], refs[1]
        e_ref = refs[2] if extra is not None else None
        outs = refs[n_in:n_in + n_out]

        def product():
            return lax.dot_general(a_ref[...].astype(BF16), b_ref[...].astype(BF16), dims, preferred_element_type=F32)

        def finish(acc):
            if epi is None:
                outs[0][...] = acc.astype(outs[0].dtype)
            elif epi == "res":
                outs[0][...] = (e_ref[...] + acc).astype(outs[0].dtype)
            elif epi == "relu2":
                outs[0][...] = acc
                r = jnp.maximum(acc, 0.0)
                outs[1][...] = (r * r).astype(BF16)
            elif epi == "drelu2":
                outs[0][...] = (acc * (2.0 * jnp.maximum(e_ref[...], 0.0))).astype(outs[0].dtype)

        if nk == 1:
            finish(product())
            return
        acc_ref = refs[-1]
        k = pl.program_id(2)

        @pl.when(k == 0)
        def _():
            acc_ref[...] = product()

        @pl.when(jnp.logical_and(k > 0, k < nk - 1))
        def _():
            acc_ref[...] += product()

        @pl.when(k == nk - 1)
        def _():
            finish(acc_ref[...] + product())

    in_specs = ([a_spec, b_spec] + ([o_spec] if extra is not None else [])
                + ([pl.BlockSpec(memory_space=pl.ANY)] if after is not None else []))
    if epi == "relu2":
        out_shape = (jax.ShapeDtypeStruct((m, n), F32), jax.ShapeDtypeStruct((m, n), BF16))
        out_specs = (o_spec, o_spec)
    else:
        out_shape = jax.ShapeDtypeStruct((m, n), out_dtype)
        out_specs = o_spec
    args = (a, b) + ((extra,) if extra is not None else ()) + ((after,) if after is not None else ())
    return pl.pallas_call(
        body, name=name, grid=(m // tm, n // tn, nk), in_specs=in_specs, out_specs=out_specs, out_shape=out_shape,
        scratch_shapes=[pltpu.VMEM((tm, tn), F32)] if nk > 1 else [],
        compiler_params=_cparams(("parallel", "parallel", "arbitrary")),
    )(*args)


ROWS = 256


def _rms_fwd(x, g, name):
    t, d = x.shape

    def body(x_ref, g_ref, h_ref):
        xv = x_ref[...]
        r = lax.rsqrt(jnp.mean(xv * xv, axis=-1, keepdims=True) + EPS)
        h_ref[...] = (xv * r * g_ref[...]).astype(BF16)

    return pl.pallas_call(
        body, name=name, grid=(t // ROWS,),
        in_specs=[pl.BlockSpec((ROWS, d), lambda i: (i, 0)), pl.BlockSpec((1, d), lambda i: (0, 0))],
        out_specs=pl.BlockSpec((ROWS, d), lambda i: (i, 0)),
        out_shape=jax.ShapeDtypeStruct((t, d), BF16), compiler_params=_cparams(("arbitrary",)),
    )(x, g.reshape(1, d))


def _rms_bwd(dh, x, g, dres, name):
    t, d = x.shape

    def body(dh_ref, x_ref, g_ref, dres_ref, dx_ref, dg_ref):
        @pl.when(pl.program_id(0) == 0)
        def _():
            dg_ref[...] = jnp.zeros_like(dg_ref)

        xv = x_ref[...]
        dhv = dh_ref[...]
        r = lax.rsqrt(jnp.mean(xv * xv, axis=-1, keepdims=True) + EPS)
        xh = xv * r
        dg_ref[...] += jnp.sum(dhv * xh, axis=0, keepdims=True)
        dxh = dhv * g_ref[...]
        dx_ref[...] = dres_ref[...] + r * (dxh - xh * jnp.mean(dxh * xh, axis=-1, keepdims=True))

    row = pl.BlockSpec((ROWS, d), lambda i: (i, 0))
    vec = pl.BlockSpec((1, d), lambda i: (0, 0))
    return pl.pallas_call(
        body, name=name, grid=(t // ROWS,), in_specs=[row, row, vec, row], out_specs=(row, vec),
        out_shape=(jax.ShapeDtypeStruct((t, d), F32), jax.ShapeDtypeStruct((1, d), F32)),
        compiler_params=_cparams(("arbitrary",)),
    )(dh, x, g.reshape(1, d), dres)


def _gate_fwd(u, yc, ya, gate_col, name):
    t, d = yc.shape
    td = math.gcd(_tile(d, 512), gate_col)
    nd = d // td
    c0 = gate_col // td

    def body(gc_ref, ga_ref, yc_ref, ya_ref, m_ref):
        m_ref[...] = (_sigmoid(gc_ref[...]) * yc_ref[...] + _sigmoid(ga_ref[...]) * ya_ref[...]).astype(BF16)

    blk = pl.BlockSpec((ROWS, td), lambda i, j: (i, j))
    return pl.pallas_call(
        body, name=name, grid=(t // ROWS, nd),
        in_specs=[pl.BlockSpec((ROWS, td), lambda i, j: (i, c0 + j)),
                  pl.BlockSpec((ROWS, td), lambda i, j: (i, c0 + nd + j)), blk, blk],
        out_specs=blk, out_shape=jax.ShapeDtypeStruct((t, d), BF16),
        compiler_params=_cparams(("arbitrary", "arbitrary")),
    )(u, u, yc, ya)


def _gate_bwd(dm, u, yc, ya, gate_col, name):
    t, d = yc.shape
    td = math.gcd(_tile(d, 512), gate_col)
    nd = d // td
    c0 = gate_col // td

    def body(dm_ref, gc_ref, ga_ref, yc_ref, ya_ref, dyc_ref, dya_ref, dugc_ref, duga_ref):
        dmv = dm_ref[...]
        gc = _sigmoid(gc_ref[...])
        ga = _sigmoid(ga_ref[...])
        dyc_ref[...] = (dmv * gc).astype(BF16)
        dya_ref[...] = (dmv * ga).astype(BF16)
        dugc_ref[...] = (dmv * yc_ref[...] * gc * (1.0 - gc)).astype(BF16)
        duga_ref[...] = (dmv * ya_ref[...] * ga * (1.0 - ga)).astype(BF16)

    blk = pl.BlockSpec((ROWS, td), lambda i, j: (i, j))
    o = jax.ShapeDtypeStruct((t, d), BF16)
    return pl.pallas_call(
        body, name=name, grid=(t // ROWS, nd),
        in_specs=[blk, pl.BlockSpec((ROWS, td), lambda i, j: (i, c0 + j)),
                  pl.BlockSpec((ROWS, td), lambda i, j: (i, c0 + nd + j)), blk, blk],
        out_specs=(blk, blk, blk, blk), out_shape=(o, o, o, o),
        compiler_params=_cparams(("arbitrary", "arbitrary")),
    )(dm, u, u, yc, ya)


def _loss_and_grad(y, target, name):
    t, d = y.shape
    n = t // ROWS

    def body(y_ref, t_ref, loss_ref, dy_ref, acc_ref):
        i = pl.program_id(0)

        @pl.when(i == 0)
        def _():
            acc_ref[...] = jnp.zeros_like(acc_ref)

        diff = y_ref[...] - t_ref[...]
        dy_ref[...] = diff * (1.0 / d)
        acc_ref[...] += jnp.sum(diff * diff, axis=0, keepdims=True)

        @pl.when(i == n - 1)
        def _():
            loss_ref[...] = jnp.sum(acc_ref[...], axis=-1, keepdims=True) * (0.5 / d)

    row = pl.BlockSpec((ROWS, d), lambda i: (i, 0))
    return pl.pallas_call(
        body, name=name, grid=(n,), in_specs=[row, row],
        out_specs=(pl.BlockSpec((1, 1), lambda i: (0, 0)), row),
        out_shape=(jax.ShapeDtypeStruct((1, 1), F32), jax.ShapeDtypeStruct((t, d), F32)),
        scratch_shapes=[pltpu.VMEM((1, d), F32)], compiler_params=_cparams(("arbitrary",)),
    )(y, target)


HALO = 32


def _conv_fwd(u, w, b, cdim, name):
    t = u.shape[0]
    ncb = cdim // LANES
    nt = t // BLOCK

    def body(a_ref, g_ref, w_ref, b_ref, zc_ref, zpad):
        zpad[0:HALO, :] = jnp.zeros((HALO, LANES), F32)
        zpad[HALO:HALO + t, :] = a_ref[...] * _sigmoid(g_ref[...])
        wv = w_ref[...]
        bv = b_ref[...]

        def tile(i, carry):
            r0 = pl.multiple_of(i * BLOCK, BLOCK)
            win = zpad[pl.ds(r0, BLOCK + HALO), :]
            acc = jnp.zeros((BLOCK, LANES), F32) + bv
            for j in range(CONV_WIDTH):
                s = CONV_WIDTH - 1 - j
                sh = win if s == 0 else pltpu.roll(win, s, 0)
                acc = acc + wv[j:j + 1, :] * sh[HALO:HALO + BLOCK, :]
            zc_ref[pl.ds(r0, BLOCK), :] = acc
            return carry

        lax.fori_loop(0, nt, tile, 0)

    col = lambda off: pl.BlockSpec((t, LANES), lambda c: (0, off + c))
    return pl.pallas_call(
        body, name=name, grid=(ncb,),
        in_specs=[col(0), col(ncb), pl.BlockSpec((CONV_TAPS_PADDED, LANES), lambda c: (0, c)),
                  pl.BlockSpec((1, LANES), lambda c: (0, c))],
        out_specs=pl.BlockSpec((t, LANES), lambda c: (0, c)),
        out_shape=jax.ShapeDtypeStruct((t, cdim), F32),
        scratch_shapes=[pltpu.VMEM((t + HALO, LANES), F32)], compiler_params=_cparams(("arbitrary",)),
    )(u, u, w, b.reshape(1, cdim))


def _conv_bwd(dzc, u, w, cdim, name):
    t = u.shape[0]
    ncb = cdim // LANES
    nt = t // BLOCK
    win_rows = BLOCK + HALO

    def body(dzc_ref, a_ref, g_ref, w_ref, da_ref, dg_ref, dw_ref, db_ref, zpad, dpad):
        av = a_ref[...]
        sg = _sigmoid(g_ref[...])
        zpad[0:HALO, :] = jnp.zeros((HALO, LANES), F32)
        zpad[HALO:HALO + t, :] = av * sg
        dpad[0:t, :] = dzc_ref[...]
        dpad[t:t + HALO, :] = jnp.zeros((HALO, LANES), F32)
        dw_ref[...] = jnp.zeros_like(dw_ref)
        db_ref[...] = jnp.sum(dzc_ref[...], axis=0, keepdims=True)
        wv = w_ref[...]

        def tile(i, carry):
            r0 = pl.multiple_of(i * BLOCK, BLOCK)
            zwin = zpad[pl.ds(r0, win_rows), :]
            dwin = dpad[pl.ds(r0, win_rows), :]
            dcur = dwin[0:BLOCK, :]
            dz = jnp.zeros((BLOCK, LANES), F32)
            for j in range(CONV_WIDTH):
                s = CONV_WIDTH - 1 - j
                zs = zwin if s == 0 else pltpu.roll(zwin, s, 0)
                dw_ref[j:j + 1, :] += jnp.sum(dcur * zs[HALO:HALO + BLOCK, :], axis=0, keepdims=True)
                ds = dwin if s == 0 else pltpu.roll(dwin, win_rows - s, 0)
                dz = dz + wv[j:j + 1, :] * ds[0:BLOCK, :]
            ac = a_ref[pl.ds(r0, BLOCK), :]
            sc = _sigmoid(g_ref[pl.ds(r0, BLOCK), :])
            da_ref[pl.ds(r0, BLOCK), :] = (dz * sc).astype(BF16)
            dg_ref[pl.ds(r0, BLOCK), :] = (dz * ac * sc * (1.0 - sc)).astype(BF16)
            return carry

        lax.fori_loop(0, nt, tile, 0)

    col = lambda off: pl.BlockSpec((t, LANES), lambda c: (0, off + c))
    wspec = pl.BlockSpec((CONV_TAPS_PADDED, LANES), lambda c: (0, c))
    o = jax.ShapeDtypeStruct((t, cdim), BF16)
    return pl.pallas_call(
        body, name=name, grid=(ncb,), in_specs=[col(0), col(0), col(ncb), wspec],
        out_specs=(col(0), col(0), wspec, pl.BlockSpec((1, LANES), lambda c: (0, c))),
        out_shape=(o, o, jax.ShapeDtypeStruct((CONV_TAPS_PADDED, cdim), F32), jax.ShapeDtypeStruct((1, cdim), F32)),
        scratch_shapes=[pltpu.VMEM((t + HALO, LANES), F32), pltpu.VMEM((t + HALO, LANES), F32)],
        compiler_params=_cparams(("arbitrary",)),
    )(dzc, u, u, w)


def _ln_swish_fwd(zc, g, b, name):
    t, c = zc.shape

    def body(z_ref, g_ref, b_ref, o_ref):
        z = z_ref[...]
        mu = jnp.mean(z, axis=-1, keepdims=True)
        zc_ = z - mu
        zn = zc_ * lax.rsqrt(jnp.mean(zc_ * zc_, axis=-1, keepdims=True) + EPS)
        y = zn * g_ref[...] + b_ref[...]
        o_ref[...] = (y * _sigmoid(y)).astype(BF16)

    row = pl.BlockSpec((ROWS, c), lambda i: (i, 0))
    vec = pl.BlockSpec((1, c), lambda i: (0, 0))
    return pl.pallas_call(
        body, name=name, grid=(t // ROWS,), in_specs=[row, vec, vec], out_specs=row,
        out_shape=jax.ShapeDtypeStruct((t, c), BF16), compiler_params=_cparams(("arbitrary",)),
    )(zc, g.reshape(1, c), b.reshape(1, c))


def _ln_swish_bwd(dzs, zc, g, b, name):
    t, c = zc.shape

    def body(d_ref, z_ref, g_ref, b_ref, dz_ref, dg_ref, db_ref):
        @pl.when(pl.program_id(0) == 0)
        def _():
            dg_ref[...] = jnp.zeros_like(dg_ref)
            db_ref[...] = jnp.zeros_like(db_ref)

        z = z_ref[...]
        mu = jnp.mean(z, axis=-1, keepdims=True)
        zc_ = z - mu
        rstd = lax.rsqrt(jnp.mean(zc_ * zc_, axis=-1, keepdims=True) + EPS)
        zn = zc_ * rstd
        y = zn * g_ref[...] + b_ref[...]
        sg = _sigmoid(y)
        dy = d_ref[...] * (sg * (1.0 + y * (1.0 - sg)))
        dg_ref[...] += jnp.sum(dy * zn, axis=0, keepdims=True)
        db_ref[...] += jnp.sum(dy, axis=0, keepdims=True)
        dzn = dy * g_ref[...]
        dz_ref[...] = rstd * (dzn - jnp.mean(dzn, axis=-1, keepdims=True)
                              - zn * jnp.mean(dzn * zn, axis=-1, keepdims=True))

    row = pl.BlockSpec((ROWS, c), lambda i: (i, 0))
    vec = pl.BlockSpec((1, c), lambda i: (0, 0))
    v = jax.ShapeDtypeStruct((1, c), F32)
    return pl.pallas_call(
        body, name=name, grid=(t // ROWS,), in_specs=[row, row, vec, vec], out_specs=(row, vec, vec),
        out_shape=(jax.ShapeDtypeStruct((t, c), F32), v, v), compiler_params=_cparams(("arbitrary",)),
    )(dzs, zc, g.reshape(1, c), b.reshape(1, c))


def _bucket_table():
    qi = np.arange(BLOCK)[:, None]
    kj = np.arange(2 * BLOCK)[None, :]
    off = qi + BLOCK - kj
    band = (off >= 0) & (off <= SUB_WINDOW)
    max_exact = NUM_BUCKETS // 2
    out = []
    for d in DILATIONS:
        dist = (np.clip(off, 0, SUB_WINDOW) * d).astype(np.int32)
        nf = np.maximum(dist, 1).astype(np.float32)
        large = max_exact + (np.log(nf / np.float32(max_exact)) / np.float32(math.log(MAX_REL_DISTANCE / max_exact))
                             * np.float32(NUM_BUCKETS - max_exact)).astype(np.int32)
        large = np.minimum(large, NUM_BUCKETS - 1)
        bucket = np.where(dist < max_exact, dist, large)
        out.append(np.where(band, bucket, -1))
    return np.stack(out).astype(np.int32)


def _bias_expand(rel_bias, buckets, hpg, name):
    nh = N_GROUPS * hpg

    def body(rb_ref, bk_ref, o_ref):
        h = pl.program_id(0)
        bk = bk_ref[0]
        acc = jnp.full((BLOCK, 2 * BLOCK), NEG_INF, F32)
        for bb in range(NUM_BUCKETS):
            acc = jnp.where(bk == bb, rb_ref[bb, h], acc)
        o_ref[0] = acc

    return pl.pallas_call(
        body, name=name, grid=(nh,),
        in_specs=[pl.BlockSpec(memory_space=pltpu.SMEM),
                  pl.BlockSpec((1, BLOCK, 2 * BLOCK), lambda h: (h // hpg, 0, 0))],
        out_specs=pl.BlockSpec((1, BLOCK, 2 * BLOCK), lambda h: (h, 0, 0)),
        out_shape=jax.ShapeDtypeStruct((nh, BLOCK, 2 * BLOCK), F32), compiler_params=_cparams(("arbitrary",)),
    )(rel_bias, buckets)


def _bias_reduce(ds_sum, buckets, hpg, name):
    nh = N_GROUPS * hpg

    def body(ds_ref, bk_ref, o_ref):
        bk = bk_ref[0]
        dsv = ds_ref[0]
        lane = lax.broadcasted_iota(jnp.int32, (1, LANES), 1)
        row = jnp.zeros((1, LANES), F32)
        for bb in range(NUM_BUCKETS):
            tot = jnp.sum(jnp.sum(jnp.where(bk == bb, dsv, 0.0), axis=-1, keepdims=True), axis=0, keepdims=True)
            row = jnp.where(lane == bb, tot, row)
        o_ref[0] = row

    return pl.pallas_call(
        body, name=name, grid=(nh,),
        in_specs=[pl.BlockSpec((1, BLOCK, 2 * BLOCK), lambda h: (h, 0, 0)),
                  pl.BlockSpec((1, BLOCK, 2 * BLOCK), lambda h: (h // hpg, 0, 0))],
        out_specs=pl.BlockSpec((1, 1, LANES), lambda h: (h, 0, 0)),
        out_shape=jax.ShapeDtypeStruct((nh, 1, LANES), F32), compiler_params=_cparams(("arbitrary",)),
    )(ds_sum, buckets)


def _chunk_rows(c, d, nb):
    r, n = c // nb, c % nb
    if d == 1:
        return pl.ds(c * BLOCK, BLOCK)
    return pl.ds(r + n * BLOCK * d, BLOCK, stride=d)


def _segment_ones():
    i = lax.broadcasted_iota(jnp.int32, (LANES, LANES), 0) // HEAD_DIM
    j = lax.broadcasted_iota(jnp.int32, (LANES, LANES), 1) // HEAD_DIM
    return (i == j).astype(BF16)


def _segment_sum(v, seg):
    hi = v.astype(BF16)
    lo = (v - hi.astype(F32)).astype(BF16)
    return jnp.dot(hi, seg, preferred_element_type=F32) + jnp.dot(lo, seg, preferred_element_type=F32)


def _head_mean(v, seg):
    return _segment_sum(v, seg) * (1.0 / HEAD_DIM)


def _attn_fwd(u, qg2, kg2, bias, gi, cols, hpg, name):
    t = u.shape[0]
    d = DILATIONS[gi]
    nchunk = t // BLOCK
    nb = (t // d) // BLOCK
    hp = hpg // 2
    qc0, kc0, vc0 = [(c + gi * hpg * HEAD_DIM) // LANES for c in cols]
    contract_lanes = (((1,), (1,)), ((), ()))

    def body(q_ref, k_ref, v_ref, qg_ref, kg_ref, bias_ref, o_ref, lse_ref, qd, kd, vd, od, ld, sbuf):
        seg = _segment_ones()
        lane = lax.broadcasted_iota(jnp.int32, (1, LANES), 1)
        qg = qg_ref[...] * (HEAD_DIM ** -0.5)
        kg = kg_ref[...]
        kd[0:BLOCK, :] = jnp.zeros((BLOCK, LANES), BF16)
        vd[0:BLOCK, :] = jnp.zeros((BLOCK, LANES), BF16)
        for c in range(nchunk):
            rows = _chunk_rows(c, d, nb)
            qv = q_ref[rows, :]
            kv = k_ref[rows, :]
            qd[c * BLOCK:(c + 1) * BLOCK, :] = (qv * lax.rsqrt(_head_mean(qv * qv, seg) + EPS) * qg).astype(BF16)
            kd[(c + 1) * BLOCK:(c + 2) * BLOCK, :] = (kv * lax.rsqrt(_head_mean(kv * kv, seg) + EPS) * kg).astype(BF16)
            vd[(c + 1) * BLOCK:(c + 2) * BLOCK, :] = v_ref[rows, :].astype(BF16)

        col = lax.broadcasted_iota(jnp.int32, (BLOCK, 2 * BLOCK), 1)
        for j in range(2):
            mj = jnp.logical_and(lane >= j * HEAD_DIM, lane < (j + 1) * HEAD_DIM)
            for c in range(nchunk):
                kw = kd[c * BLOCK:(c + 2) * BLOCK, :]
                kj = jnp.where(mj, kw, jnp.zeros_like(kw))
                s = lax.dot_general(qd[c * BLOCK:(c + 1) * BLOCK, :], kj, contract_lanes,
                                    preferred_element_type=F32) + bias_ref[j]
                if c % nb == 0:
                    s = jnp.where(col < BLOCK, NEG_INF, s)
                sbuf[c] = s
            for c in range(nchunk):
                rows = slice(c * BLOCK, (c + 1) * BLOCK)
                s = sbuf[c]
                mx = jnp.max(s, axis=-1, keepdims=True)
                p = jnp.exp(s - mx).astype(BF16)
                vw = vd[c * BLOCK:(c + 2) * BLOCK, :]
                oj = jnp.dot(p, jnp.where(mj, vw, jnp.ones_like(vw)), preferred_element_type=F32)
                l = pltpu.roll(oj, HEAD_DIM, 1)
                on = oj / l
                ls = mx + jnp.log(l)
                if j == 0:
                    od[rows, :] = on
                    ld[rows, :] = ls
                else:
                    od[rows, :] = jnp.where(mj, on, od[rows, :])
                    ld[rows, :] = jnp.where(mj, ls, ld[rows, :])

        for c in range(nchunk):
            rows = _chunk_rows(c, d, nb)
            o_ref[rows, :] = od[c * BLOCK:(c + 1) * BLOCK, :]
            lse_ref[rows, :] = ld[c * BLOCK:(c + 1) * BLOCK, :]

    ucol = lambda c0: pl.BlockSpec((t, LANES), lambda h: (0, c0 + h))
    vec = pl.BlockSpec((1, LANES), lambda h: (0, 0))
    oblk = pl.BlockSpec((t, LANES), lambda h: (0, h))
    osh = jax.ShapeDtypeStruct((t, hpg * HEAD_DIM), F32)
    return pl.pallas_call(
        body, name=name, grid=(hp,),
        in_specs=[ucol(qc0), ucol(kc0), ucol(vc0), vec, vec,
                  pl.BlockSpec((2, BLOCK, 2 * BLOCK), lambda h: (gi * hp + h, 0, 0))],
        out_specs=(oblk, oblk), out_shape=(osh, osh),
        scratch_shapes=[pltpu.VMEM((t, LANES), BF16), pltpu.VMEM((t + BLOCK, LANES), BF16),
                        pltpu.VMEM((t + BLOCK, LANES), BF16), pltpu.VMEM((t, LANES), F32), pltpu.VMEM((t, LANES), F32),
                        pltpu.VMEM((nchunk, BLOCK, 2 * BLOCK), F32)],
        compiler_params=_cparams(("arbitrary",)),
    )(u, u, u, qg2, kg2, bias)


def _attn_bwd(u, do_g, dd_g, lse_g, qg2, kg2, bias, ds_in, gi, cols, hpg, name):
    t = u.shape[0]
    d = DILATIONS[gi]
    nchunk = t // BLOCK
    nb = (t // d) // BLOCK
    hp = hpg // 2
    qc0, kc0, vc0 = [(c + gi * hpg * HEAD_DIM) // LANES for c in cols]
    contract_lanes = (((1,), (1,)), ((), ()))
    contract_rows = (((0,), (0,)), ((), ()))
    qscale = HEAD_DIM ** -0.5

    def body(q_ref, k_ref, v_ref, do_ref, dd_ref, lse_ref, qg_ref, kg_ref, bias_ref, dsin_ref,
             dq_ref, dk_ref, dv_ref, dgq_ref, dgk_ref, dsout_ref,
             qd, kd, vd, dod, ddd, ld, dqd, dkd, dvd, dsacc, pbuf, dsbuf):
        seg = _segment_ones()
        lane = lax.broadcasted_iota(jnp.int32, (1, LANES), 1)
        qg = qg_ref[...] * qscale
        kg = kg_ref[...]
        kd[0:BLOCK, :] = jnp.zeros((BLOCK, LANES), BF16)
        vd[0:BLOCK, :] = jnp.zeros((BLOCK, LANES), BF16)
        dsacc[...] = jnp.zeros_like(dsacc)
        for c in range(nchunk):
            rows = _chunk_rows(c, d, nb)
            qv = q_ref[rows, :]
            kv = k_ref[rows, :]
            qd[c * BLOCK:(c + 1) * BLOCK, :] = (qv * lax.rsqrt(_head_mean(qv * qv, seg) + EPS) * qg).astype(BF16)
            kd[(c + 1) * BLOCK:(c + 2) * BLOCK, :] = (kv * lax.rsqrt(_head_mean(kv * kv, seg) + EPS) * kg).astype(BF16)
            vd[(c + 1) * BLOCK:(c + 2) * BLOCK, :] = v_ref[rows, :].astype(BF16)
            dod[c * BLOCK:(c + 1) * BLOCK, :] = do_ref[rows, :].astype(BF16)
            ddd[c * BLOCK:(c + 1) * BLOCK, :] = dd_ref[rows, :]
            ld[c * BLOCK:(c + 1) * BLOCK, :] = lse_ref[rows, :]

        col = lax.broadcasted_iota(jnp.int32, (BLOCK, 2 * BLOCK), 1)
        for j in range(2):
            mj = jnp.logical_and(lane >= j * HEAD_DIM, lane < (j + 1) * HEAD_DIM)
            first = lane == j * HEAD_DIM
            for c in range(nchunk):
                rows = slice(c * BLOCK, (c + 1) * BLOCK)
                kw = kd[c * BLOCK:(c + 2) * BLOCK, :]
                vw = vd[c * BLOCK:(c + 2) * BLOCK, :]
                kj = jnp.where(mj, kw, jnp.zeros_like(kw))
                vj = jnp.where(mj, vw, jnp.zeros_like(vw))
                s = lax.dot_general(qd[rows, :], kj, contract_lanes, preferred_element_type=F32) + bias_ref[j]
                if c % nb == 0:
                    s = jnp.where(col < BLOCK, NEG_INF, s)
                dp = lax.dot_general(dod[rows, :], vj, contract_lanes, preferred_element_type=F32)
                lse_j = jnp.sum(jnp.where(first, ld[rows, :], 0.0), axis=-1, keepdims=True)
                dd_j = jnp.sum(jnp.where(first, ddd[rows, :], 0.0), axis=-1, keepdims=True)
                p = jnp.exp(s - lse_j)
                ds = p * (dp + dd_j)
                dsacc[j] += ds
                pbuf[j, c] = p.astype(BF16)
                dsbuf[j, c] = ds.astype(BF16)
        for c in range(nchunk):
            rows = slice(c * BLOCK, (c + 1) * BLOCK)
            nxt = slice((c + 1) * BLOCK, (c + 2) * BLOCK)
            has_next = c + 1 < nchunk and (c + 1) % nb != 0
            kw = kd[c * BLOCK:(c + 2) * BLOCK, :]
            dq = jnp.zeros((BLOCK, LANES), F32)
            dk = jnp.zeros((BLOCK, LANES), F32)
            dv = jnp.zeros((BLOCK, LANES), F32)
            for j in range(2):
                mj = jnp.logical_and(lane >= j * HEAD_DIM, lane < (j + 1) * HEAD_DIM)
                zero = jnp.zeros((BLOCK, LANES), BF16)
                dq = dq + jnp.dot(dsbuf[j, c], jnp.where(mj, kw, jnp.zeros_like(kw)), preferred_element_type=F32)
                dk = dk + lax.dot_general(dsbuf[j, c, :, BLOCK:], jnp.where(mj, qd[rows, :], zero), contract_rows,
                                          preferred_element_type=F32)
                dv = dv + lax.dot_general(pbuf[j, c, :, BLOCK:], jnp.where(mj, dod[rows, :], zero), contract_rows,
                                          preferred_element_type=F32)
                if has_next:
                    dk = dk + lax.dot_general(dsbuf[j, c + 1, :, :BLOCK], jnp.where(mj, qd[nxt, :], zero),
                                              contract_rows, preferred_element_type=F32)
                    dv = dv + lax.dot_general(pbuf[j, c + 1, :, :BLOCK], jnp.where(mj, dod[nxt, :], zero),
                                              contract_rows, preferred_element_type=F32)
            dqd[rows, :] = dq
            dkd[rows, :] = dk
            dvd[rows, :] = dv

        dsout_ref[...] = dsin_ref[...] + dsacc[...]

        dgq = jnp.zeros((1, LANES), F32)
        dgk = jnp.zeros((1, LANES), F32)
        for c in range(nchunk):
            rows = _chunk_rows(c, d, nb)
            qv = q_ref[rows, :]
            rq = lax.rsqrt(_head_mean(qv * qv, seg) + EPS)
            qh = qv * rq
            dy = dqd[c * BLOCK:(c + 1) * BLOCK, :]
            dgq = dgq + jnp.sum(dy * qh, axis=0, keepdims=True) * qscale
            dxh = dy * qg
            ddd[rows, :] = rq * (dxh - qh * _head_mean(dxh * qh, seg))
            kv = k_ref[rows, :]
            rk = lax.rsqrt(_head_mean(kv * kv, seg) + EPS)
            kh = kv * rk
            dy = dkd[c * BLOCK:(c + 1) * BLOCK, :]
            dgk = dgk + jnp.sum(dy * kh, axis=0, keepdims=True)
            dxh = dy * kg
            ld[rows, :] = rk * (dxh - kh * _head_mean(dxh * kh, seg))
        dq_ref[...] = ddd[...].astype(BF16)
        dk_ref[...] = ld[...].astype(BF16)
        for c in range(nchunk):
            ddd[_chunk_rows(c, d, nb), :] = dvd[c * BLOCK:(c + 1) * BLOCK, :]
        dv_ref[...] = ddd[...].astype(BF16)
        dgq_ref[0] = dgq
        dgk_ref[0] = dgk

    ucol = lambda c0: pl.BlockSpec((t, LANES), lambda h: (0, c0 + h))
    vec = pl.BlockSpec((1, LANES), lambda h: (0, 0))
    oblk = pl.BlockSpec((t, LANES), lambda h: (0, h))
    bblk = pl.BlockSpec((2, BLOCK, 2 * BLOCK), lambda h: (gi * hp + h, 0, 0))
    gblk = pl.BlockSpec((1, 1, LANES), lambda h: (h, 0, 0))
    osh = jax.ShapeDtypeStruct((t, hpg * HEAD_DIM), BF16)
    gsh = jax.ShapeDtypeStruct((hp, 1, LANES), F32)
    return pl.pallas_call(
        body, name=name, grid=(hp,),
        in_specs=[ucol(qc0), ucol(kc0), ucol(vc0), oblk, oblk, oblk, vec, vec, bblk, bblk],
        out_specs=(oblk, oblk, oblk, gblk, gblk, bblk),
        out_shape=(osh, osh, osh, gsh, gsh, jax.ShapeDtypeStruct(ds_in.shape, F32)),
        input_output_aliases={9: 5},
        scratch_shapes=[pltpu.VMEM((t, LANES), BF16), pltpu.VMEM((t + BLOCK, LANES), BF16),
                        pltpu.VMEM((t + BLOCK, LANES), BF16), pltpu.VMEM((t, LANES), BF16),
                        pltpu.VMEM((t, LANES), F32), pltpu.VMEM((t, LANES), F32), pltpu.VMEM((t, LANES), F32),
                        pltpu.VMEM((t, LANES), F32), pltpu.VMEM((t, LANES), F32),
                        pltpu.VMEM((2, BLOCK, 2 * BLOCK), F32), pltpu.VMEM((2, nchunk, BLOCK, 2 * BLOCK), BF16),
                        pltpu.VMEM((2, nchunk, BLOCK, 2 * BLOCK), BF16)],
        compiler_params=_cparams(("arbitrary",)),
    )(u, u, u, do_g, dd_g, lse_g, qg2, kg2, bias, ds_in)


def _group_weights(l0, l1, l2):
    mx = jnp.maximum(jnp.maximum(l0, l1), l2)
    e0, e1, e2 = jnp.exp(l0 - mx), jnp.exp(l1 - mx), jnp.exp(l2 - mx)
    inv = 1.0 / (e0 + e1 + e2)
    return e0 * inv, e1 * inv, e2 * inv


def _combine_fwd(os_, lses, name):
    t, ao = os_[0].shape

    def body(o0, o1, o2, l0, l1, l2, o_ref):
        w0, w1, w2 = _group_weights(l0[...], l1[...], l2[...])
        o_ref[...] = (w0 * o0[...] + w1 * o1[...] + w2 * o2[...]).astype(BF16)

    row = pl.BlockSpec((ROWS, ao), lambda i: (i, 0))
    return pl.pallas_call(
        body, name=name, grid=(t // ROWS,), in_specs=[row] * 6, out_specs=row,
        out_shape=jax.ShapeDtypeStruct((t, ao), BF16), compiler_params=_cparams(("arbitrary",)),
    )(*os_, *lses)


def _combine_bwd(do, os_, lses, name):
    t, ao = do.shape
    idx = np.arange(ao) // HEAD_DIM
    seg = jnp.asarray((idx[:, None] == idx[None, :]).astype(np.float32), dtype=BF16)

    def body(do_ref, o0, o1, o2, l0, l1, l2, seg_ref, g0, g1, g2, d0, d1, d2):
        w0, w1, w2 = _group_weights(l0[...], l1[...], l2[...])
        dov = do_ref[...]
        o = w0 * o0[...] + w1 * o1[...] + w2 * o2[...]
        sd = _segment_sum(dov * o, seg_ref[...])
        for w, gref, dref in ((w0, g0, d0), (w1, g1, d1), (w2, g2, d2)):
            gref[...] = w * dov
            dref[...] = -(w * sd)

    row = pl.BlockSpec((ROWS, ao), lambda i: (i, 0))
    sh = jax.ShapeDtypeStruct((t, ao), F32)
    outs = pl.pallas_call(
        body, name=name, grid=(t // ROWS,), in_specs=[row] * 7 + [pl.BlockSpec((ao, ao), lambda i: (0, 0))],
        out_specs=(row,) * 6, out_shape=(sh,) * 6, compiler_params=_cparams(("arbitrary",)),
    )(do, *os_, *lses, seg)
    return outs[:3], outs[3:]


def _adamw(w, g, m, v, name):
    shape = w.shape
    cols = shape[-1]
    rows = int(np.prod(shape[:-1]))
    tr = rows if rows <= 512 else _tile_rows(rows)
    c1 = 1.0 - ADAM_B1 ** ADAM_STEP
    c2 = 1.0 - ADAM_B2 ** ADAM_STEP

    def body(w_ref, g_ref, m_ref, v_ref, d_ref, nm_ref, nv_ref):
        gv = g_ref[...]
        mn = ADAM_B1 * m_ref[...] + (1.0 - ADAM_B1) * gv
        vn = ADAM_B2 * v_ref[...] + (1.0 - ADAM_B2) * (gv * gv)
        nm_ref[...] = mn
        nv_ref[...] = vn
        d_ref[...] = -ADAM_LR * ((mn / c1) / (jnp.sqrt(vn / c2) + ADAM_EPS) + ADAM_WD * w_ref[...])

    blk = pl.BlockSpec((tr, cols), lambda i: (i, 0))
    sh = jax.ShapeDtypeStruct((rows, cols), F32)
    outs = pl.pallas_call(
        body, name=name, grid=(rows // tr,), in_specs=[blk] * 4, out_specs=(blk,) * 3, out_shape=(sh,) * 3,
        compiler_params=_cparams(("arbitrary",)),
    )(*[a.reshape(rows, cols) for a in (w, g, m, v)])
    return tuple(o.reshape(shape) for o in outs)


def _tile_rows(rows):
    for t in (512, 256, 128, 64, 32, 16, 8):
        if rows % t == 0:
            return t
    return rows


def _sum_slots(recv, name):
    _, rows, cols = recv.shape
    tr = rows if rows <= 512 else _tile_rows(rows)

    def body(r_ref, o_ref):
        acc = r_ref[0].astype(F32)
        for s in range(1, N_DEV):
            acc = acc + r_ref[s].astype(F32)
        o_ref[...] = acc

    return pl.pallas_call(
        body, name=name, grid=(rows // tr,), in_specs=[pl.BlockSpec((N_DEV, tr, cols), lambda i: (0, i, 0))],
        out_specs=pl.BlockSpec((tr, cols), lambda i: (i, 0)), out_shape=jax.ShapeDtypeStruct((rows, cols), F32),
        compiler_params=_cparams(("arbitrary",)),
    )(recv)


def _peer(k):
    x, y, c = lax.axis_index("x"), lax.axis_index("y"), lax.axis_index("c")
    return (1 - x if k & 4 else x, 1 - y if k & 2 else y, 1 - c if k & 1 else c)


def _dev_index(p):
    return 4 * p[0] + 2 * p[1] + p[2]


HBM_SPEC = pl.BlockSpec(memory_space=pltpu.HBM)
SEM_SPEC = pl.BlockSpec(memory_space=pltpu.SEMAPHORE)
ANY_SPEC = pl.BlockSpec(memory_space=pl.ANY)
CHIPS = (4, 2, 6)


def _remote(src, dst, send_sem, recv_sem, to):
    return pltpu.make_async_remote_copy(src_ref=src, dst_ref=dst, send_sem=send_sem, recv_sem=recv_sem,
                                        device_id=to, device_id_type=MESH)


def _hbm(a):
    return pltpu.with_memory_space_constraint(a, pltpu.HBM)


def _split_call(body, name, bufs, sems_in, sem_out_sizes, after):
    nb, ns, no = len(bufs), len(sems_in), len(sem_out_sizes)
    extra = [] if after is None else [after]

    def kern(*refs):
        pos = nb + ns + len(extra)
        body(refs[:nb], refs[nb:nb + ns], refs[pos:pos + no])
        token_ref = refs[pos + no + nb]
        token_ref[...] = jnp.zeros_like(token_ref)

    out_shape = (tuple(pltpu.SemaphoreType.DMA((s,)) for s in sem_out_sizes)
                 + tuple(pltpu.HBM(b.shape, b.dtype) for b in bufs) + (jax.ShapeDtypeStruct((8, LANES), F32),))
    res = pl.pallas_call(
        kern, name=name, out_shape=out_shape,
        in_specs=[HBM_SPEC] * nb + [SEM_SPEC] * ns + [ANY_SPEC] * len(extra),
        out_specs=(SEM_SPEC,) * no + (HBM_SPEC,) * nb + (pl.BlockSpec(memory_space=pltpu.VMEM),),
        input_output_aliases={i: no + i for i in range(nb)},
        compiler_params=pltpu.CompilerParams(has_side_effects=pltpu.SideEffectType.DATAFLOW_SIDE_EFFECTING),
    )(*bufs, *sems_in, *extra)
    return res[:no], res[no:no + nb], res[no + nb]


def _gather_start(shards, lands, after, name):
    n = len(shards)

    def body(bufs, _, sems):
        ins, lnd = bufs[:n], bufs[n:]
        d2d_s, d2d_r, ici_s, ici_r = sems
        me = _dev_index(_peer(0))
        for j, k in enumerate(CHIPS):
            for i in range(n):
                _remote(ins[i], lnd[i].at[me], ici_s.at[j], ici_r.at[j], _peer(k)).start()
        for i in range(n):
            _remote(ins[i], lnd[i].at[me], d2d_s.at[0], d2d_r.at[0], _peer(1)).start()

    return _split_call(body, name, [_hbm(a) for a in (*shards, *lands)], [], (1, 1, 3, 3), after)


def _gather_forward(n, bufs, ici_r, after, name):
    def body(refs, sems_in, sems):
        ins, lnd = refs[:n], refs[n:]
        (arrived,) = sems_in
        fwd_s, fwd_r = sems
        for j, k in enumerate(CHIPS):
            blk = _dev_index(_peer(k))
            for i in range(n):
                _remote(ins[i], lnd[i].at[blk], fwd_s.at[j], arrived.at[j], _peer(k)).wait_recv()
            for i in range(n):
                _remote(lnd[i].at[blk], lnd[i].at[blk], fwd_s.at[j], fwd_r.at[j], _peer(1)).start()

    return _split_call(body, name, bufs, [ici_r], (3, 3), after)


def _gather_finish(n, bufs, d2d_s, d2d_r, ici_s, fwd_s, fwd_r, after, name):
    def body(refs, sems_in, _):
        ins, lnd = refs[:n], refs[n:]
        d2d_send, d2d_recv, ici_send, fwd_send, fwd_recv = sems_in
        sib = _peer(1)
        for i in range(n):
            cp = _remote(ins[i], lnd[i].at[_dev_index(sib)], d2d_send.at[0], d2d_recv.at[0], sib)
            cp.wait_send()
            cp.wait_recv()
        for j, k in enumerate(CHIPS):
            passed = _dev_index(_peer(k))
            landed = _dev_index(_peer(k | 1))
            for i in range(n):
                _remote(ins[i], lnd[i].at[passed], ici_send.at[j], fwd_recv.at[j], _peer(k)).wait_send()
                cp = _remote(lnd[i].at[passed], lnd[i].at[landed], fwd_send.at[j], fwd_recv.at[j], sib)
                cp.wait_send()
                cp.wait_recv()

    _, out, token = _split_call(body, name, bufs, [d2d_s, d2d_r, ici_s, fwd_s, fwd_r], (), after)
    return out[n:], token


def _exchange_start(parts, lands, after, name):
    n = len(parts)

    def body(bufs, _, sems):
        src, lnd = bufs[:n], bufs[n:]
        send, recv = sems
        me = _dev_index(_peer(0))
        for k in (4, 5, 2, 3, 6, 7, 1):
            to = _peer(k)
            for i in range(n):
                _remote(src[i].at[_dev_index(to)], lnd[i].at[me], send.at[k - 1], recv.at[k - 1], to).start()

    return _split_call(body, name, [_hbm(a) for a in (*parts, *lands)], [], (7, 7), after)


def _exchange_finish(n, bufs, send, recv, after, name):
    def body(refs, sems_in, _):
        src, lnd = refs[:n], refs[n:]
        send_, recv_ = sems_in
        me = _dev_index(_peer(0))
        for k in range(1, N_DEV):
            frm = _peer(k)
            for i in range(n):
                cp = _remote(src[i].at[me], lnd[i].at[_dev_index(frm)], send_.at[k - 1], recv_.at[k - 1], frm)
                cp.wait_send()
                cp.wait_recv()

    _, out, token = _split_call(body, name, bufs, [send, recv], (), after)
    return out[n:], token


def _all_reduce_small(v, name):
    rows = v.shape[0]

    def body(v_ref, o_ref, buf, send_sems, recv_sems):
        me = _dev_index(_peer(0))
        buf[me] = v_ref[...]
        copies = []
        for k in range(1, N_DEV):
            copies.append(pltpu.make_async_remote_copy(
                src_ref=v_ref, dst_ref=buf.at[me], send_sem=send_sems.at[k - 1], recv_sem=recv_sems.at[k - 1],
                device_id=_peer(k), device_id_type=MESH))
        for cp in copies:
            cp.start()
        for k in range(1, N_DEV):
            pltpu.make_async_remote_copy(
                src_ref=v_ref, dst_ref=buf.at[_dev_index(_peer(k))], send_sem=send_sems.at[k - 1],
                recv_sem=recv_sems.at[k - 1], device_id=_peer(k), device_id_type=MESH).wait_recv()
        for cp in copies:
            cp.wait_send()
        acc = buf[0]
        for s in range(1, N_DEV):
            acc = acc + buf[s]
        o_ref[...] = acc

    vm = pl.BlockSpec(memory_space=pltpu.VMEM)
    return pl.pallas_call(
        body, name=name, in_specs=[vm], out_specs=vm, out_shape=jax.ShapeDtypeStruct(v.shape, F32),
        scratch_shapes=[pltpu.VMEM((N_DEV, rows, LANES), F32), pltpu.SemaphoreType.DMA((7,)),
                        pltpu.SemaphoreType.DMA((7,))],
    )(v)


def _columns(cdim, ao):
    q_col = 2 * cdim
    attn_dim = N_GROUPS * ao
    return (q_col, q_col + attn_dim, q_col + 2 * attn_dim), q_col + 3 * attn_dim


def _layer_fwd(x, sm, bg, bias):
    cdim = sm["conv_ln_g"].shape[0]
    ao = bg["w_attn_out"].shape[0]
    hpg = ao // HEAD_DIM
    cols, gate_col = _columns(cdim, ao)
    qg2 = jnp.tile(sm["q_norm_g"], 2).reshape(1, LANES)
    kg2 = jnp.tile(sm["k_norm_g"], 2).reshape(1, LANES)
    h1 = _rms_fwd(x, sm["norm1_g"], "rms1_fwd")
    u = _mm(h1, bg["w_in"], name="mm_in")
    zc = _conv_fwd(u, bg["conv_dw_w"], sm["conv_dw_b"], cdim, "conv_fwd")
    zs = _ln_swish_fwd(zc, sm["conv_ln_g"], sm["conv_ln_b"], "ln_swish_fwd")
    yc = _mm(zs, bg["w_conv_out"], name="mm_conv_out")
    os_, lses = [], []
    for gi in range(N_GROUPS):
        o_g, lse_g = _attn_fwd(u, qg2, kg2, bias, gi, cols, hpg, "attn_fwd_g%d" % gi)
        os_.append(o_g)
        lses.append(lse_g)
    o = _combine_fwd(os_, lses, "combine_fwd")
    ya = _mm(o, bg["w_attn_out"], name="mm_attn_out")
    mg = _gate_fwd(u, yc, ya, gate_col, "gate_fwd")
    x1 = _mm(mg, bg["w_out"], epi="res", extra=x, name="mm_out")
    h2 = _rms_fwd(x1, sm["norm2_g"], "rms2_fwd")
    f, act = _mm(h2, bg["w_ff1"], epi="relu2", name="mm_ff1")
    x2 = _mm(act, bg["w_ff2"], epi="res", extra=x1, name="mm_ff2")
    saved = dict(x=x, h1=h1, u=u, zc=zc, zs=zs, yc=yc, os=os_, lses=lses, o=o, ya=ya, mg=mg, x1=x1, h2=h2, f=f,
                 act=act, qg2=qg2, kg2=kg2)
    return x2, saved


def _layer_bwd(dx, s, sm, bg, bias, ds_sum, after):
    cdim = sm["conv_ln_g"].shape[0]
    ao = bg["w_attn_out"].shape[0]
    hpg = ao // HEAD_DIM
    hp = hpg // 2
    cols, gate_col = _columns(cdim, ao)
    g = {}
    df = _mm(dx, bg["w_ff2"], tb=True, epi="drelu2", extra=s["f"], out_dtype=BF16, after=after, name="mm_dff2")
    g["w_ff2"] = _mm(s["act"], dx, ta=True, name="mm_gw_ff2")
    g["w_ff1"] = _mm(s["h2"], df, ta=True, name="mm_gw_ff1")
    dh2 = _mm(df, bg["w_ff1"], tb=True, name="mm_dff1")
    dx1, dg2 = _rms_bwd(dh2, s["x1"], sm["norm2_g"], dx, "rms2_bwd")
    g["norm2_g"] = dg2[0]
    dmg = _mm(dx1, bg["w_out"], tb=True, name="mm_dout")
    g["w_out"] = _mm(s["mg"], dx1, ta=True, name="mm_gw_out")
    dyc, dya, dugc, duga = _gate_bwd(dmg, s["u"], s["yc"], s["ya"], gate_col, "gate_bwd")
    dzs = _mm(dyc, bg["w_conv_out"], tb=True, name="mm_dconv_out")
    g["w_conv_out"] = _mm(s["zs"], dyc, ta=True, name="mm_gw_conv_out")
    do = _mm(dya, bg["w_attn_out"], tb=True, name="mm_dattn_out")
    g["w_attn_out"] = _mm(s["o"], dya, ta=True, name="mm_gw_attn_out")
    dzc, dlg, dlb = _ln_swish_bwd(dzs, s["zc"], sm["conv_ln_g"], sm["conv_ln_b"], "ln_swish_bwd")
    g["conv_ln_g"] = dlg[0]
    g["conv_ln_b"] = dlb[0]
    da, dgt, dcw, dcb = _conv_bwd(dzc, s["u"], bg["conv_dw_w"], cdim, "conv_bwd")
    g["conv_dw_w"] = dcw[:CONV_WIDTH]
    g["conv_dw_b"] = dcb[0]
    do_gs, dd_gs = _combine_bwd(do, s["os"], s["lses"], "combine_bwd")
    dqs, dks, dvs = [], [], []
    dgq = jnp.zeros((HEAD_DIM,), F32)
    dgk = jnp.zeros((HEAD_DIM,), F32)
    for gi in range(N_GROUPS):
        dq, dk, dv, gq, gk, ds_sum = _attn_bwd(s["u"], do_gs[gi], dd_gs[gi], s["lses"][gi], s["qg2"], s["kg2"],
                                               bias, ds_sum, gi, cols, hpg, "attn_bwd_g%d" % gi)
        dqs.append(dq)
        dks.append(dk)
        dvs.append(dv)
        dgq = dgq + jnp.sum(gq.reshape(hp * 2, HEAD_DIM), axis=0)
        dgk = dgk + jnp.sum(gk.reshape(hp * 2, HEAD_DIM), axis=0)
    g["q_norm_g"] = dgq
    g["k_norm_g"] = dgk
    du = jnp.concatenate([da, dgt] + dqs + dks + dvs + [dugc, duga], axis=1)
    g["w_in"] = _mm(s["h1"], du, ta=True, name="mm_gw_in")
    dh1 = _mm(du, bg["w_in"], tb=True, name="mm_din")
    dx0, dg1 = _rms_bwd(dh1, s["x"], sm["norm1_g"], dx1, "rms1_bwd")
    g["norm1_g"] = dg1[0]
    return dx0, g, ds_sum


BIG = ("w_in", "conv_dw_w", "w_conv_out", "w_attn_out", "w_out", "w_ff1", "w_ff2")
COL_SHARDED = ("w_in", "conv_dw_w", "w_conv_out", "w_attn_out", "w_ff1")
SMALL = ("rel_bias", "norm1_g", "q_norm_g", "k_norm_g", "conv_dw_b", "conv_ln_g", "conv_ln_b", "norm2_g")
WEIGHTS = ("rel_bias", "norm1_g", "w_in", "q_norm_g", "k_norm_g", "conv_dw_w", "conv_dw_b", "conv_ln_g", "conv_ln_b",
           "w_conv_out", "w_attn_out", "w_out", "norm2_g", "w_ff1", "w_ff2")


def _to_whole(name, gathered):
    n, a, b = gathered.shape
    if name in COL_SHARDED:
        return gathered.transpose(1, 0, 2).reshape(a, n * b)
    return gathered.reshape(n * a, b)


def _to_slots(name, whole):
    a, b = whole.shape
    if name in COL_SHARDED:
        return whole.reshape(a, N_DEV, b // N_DEV).transpose(1, 0, 2)
    return whole.reshape(N_DEV, a // N_DEV, b)


def _own_slot(block, me):
    land = lax.empty((N_DEV,) + block.shape, block.dtype)
    return lax.dynamic_update_slice(land, block[None], (me,) + (0,) * block.ndim)


def kernel(x, rel_bias, norm1_g, w_in, q_norm_g, k_norm_g, conv_dw_w, conv_dw_b, conv_ln_g, conv_ln_b, w_conv_out, w_attn_out, w_out, norm2_g, w_ff1, w_ff2, loss_target, m_rel_bias, m_norm1_g, m_w_in, m_q_norm_g, m_k_norm_g, m_conv_dw_w, m_conv_dw_b, m_conv_ln_g, m_conv_ln_b, m_w_conv_out, m_w_attn_out, m_w_out, m_norm2_g, m_w_ff1, m_w_ff2, v_rel_bias, v_norm1_g, v_w_in, v_q_norm_g, v_k_norm_g, v_conv_dw_w, v_conv_dw_b, v_conv_ln_g, v_conv_ln_b, v_w_conv_out, v_w_attn_out, v_w_out, v_norm2_g, v_w_ff1, v_w_ff2):
    w = dict(rel_bias=rel_bias, norm1_g=norm1_g, w_in=w_in, q_norm_g=q_norm_g, k_norm_g=k_norm_g, conv_dw_w=conv_dw_w,
             conv_dw_b=conv_dw_b, conv_ln_g=conv_ln_g, conv_ln_b=conv_ln_b, w_conv_out=w_conv_out,
             w_attn_out=w_attn_out, w_out=w_out, norm2_g=norm2_g, w_ff1=w_ff1, w_ff2=w_ff2)
    mom = dict(rel_bias=m_rel_bias, norm1_g=m_norm1_g, w_in=m_w_in, q_norm_g=m_q_norm_g, k_norm_g=m_k_norm_g,
               conv_dw_w=m_conv_dw_w, conv_dw_b=m_conv_dw_b, conv_ln_g=m_conv_ln_g, conv_ln_b=m_conv_ln_b,
               w_conv_out=m_w_conv_out, w_attn_out=m_w_attn_out, w_out=m_w_out, norm2_g=m_norm2_g, w_ff1=m_w_ff1,
               w_ff2=m_w_ff2)
    var = dict(rel_bias=v_rel_bias, norm1_g=v_norm1_g, w_in=v_w_in, q_norm_g=v_q_norm_g, k_norm_g=v_k_norm_g,
               conv_dw_w=v_conv_dw_w, conv_dw_b=v_conv_dw_b, conv_ln_g=v_conv_ln_g, conv_ln_b=v_conv_ln_b,
               w_conv_out=v_w_conv_out, w_attn_out=v_w_attn_out, w_out=v_w_out, norm2_g=v_norm2_g, w_ff1=v_w_ff1,
               w_ff2=v_w_ff2)

    depth = norm1_g.shape[0]
    n = len(BIG)
    me = 4 * lax.axis_index("x") + 2 * lax.axis_index("y") + lax.axis_index("c")
    hpg = w_attn_out.shape[1] // HEAD_DIM
    buckets = jnp.asarray(_bucket_table())
    bias = _bias_expand(rel_bias, buckets, hpg, "bias_expand")

    def gather_start(l, after):
        shards = [w[k][l] if k == "conv_dw_w" else w[k][l].astype(BF16) for k in BIG]
        return _gather_start(shards, [_own_slot(s, me) for s in shards], after, "gather_start_l%d" % l)

    xs = x[0]
    saved, bigs, smalls = [], [], []
    started = gather_start(0, None)
    for l in range(depth):
        (d2d_s, d2d_r, ici_s, ici_r), bufs, _ = started
        (fwd_s, fwd_r), bufs, token = _gather_forward(n, bufs, ici_r, None if l == 0 else xs, "gather_forward_l%d" % l)
        if l + 1 < depth:
            started = gather_start(l + 1, token)
            token = started[2]
        gathered, _ = _gather_finish(n, bufs, d2d_s, d2d_r, ici_s, fwd_s, fwd_r, token, "gather_finish_l%d" % l)
        bg = {k: _to_whole(k, a) for k, a in zip(BIG, gathered)}
        bg["conv_dw_w"] = jnp.pad(bg["conv_dw_w"], ((0, CONV_TAPS_PADDED - CONV_WIDTH), (0, 0)))
        sm = {k: w[k][l] for k in SMALL if k != "rel_bias"}
        xs, sv = _layer_fwd(xs, sm, bg, bias)
        saved.append(sv)
        bigs.append(bg)
        smalls.append(sm)

    loss, dx = _loss_and_grad(xs, loss_target[0], "loss")

    ds_sum = jnp.zeros((N_GROUPS * hpg, BLOCK, 2 * BLOCK), F32)
    g = {k: [None] * depth for k in SMALL if k != "rel_bias"}
    grad_layers = {k: [None] * depth for k in BIG}

    def exchange_finish(pending, after):
        l, (send, recv), bufs = pending
        recvd, token = _exchange_finish(n, bufs, send, recv, after, "exchange_finish_l%d" % l)
        for k, r in zip(BIG, recvd):
            shp = w[k].shape[1:]
            grad_layers[k][l] = _sum_slots(r.reshape(N_DEV, -1, shp[-1]), "sum_" + k).reshape(shp)
        return token

    pending, token = None, None
    for l in reversed(range(depth)):
        dx, gl, ds_sum = _layer_bwd(dx, saved[l], smalls[l], bigs[l], bias, ds_sum, token)
        for k in g:
            g[k][l] = gl[k]
        token = exchange_finish(pending, dx) if pending is not None else None
        parts = [_to_slots(k, gl[k]).astype(BF16) for k in BIG]
        lands = [_own_slot(lax.dynamic_index_in_dim(p, me, 0, keepdims=False), me) for p in parts]
        sems, bufs, token = _exchange_start(parts, lands, token, "exchange_start_l%d" % l)
        pending = (l, sems, bufs)
    exchange_finish(pending, None)
    grad_x = dx

    g = {k: jnp.stack(v) for k, v in g.items()}
    db = _bias_reduce(ds_sum, buckets, hpg, "bias_reduce")
    g["rel_bias"] = db[:, 0, :NUM_BUCKETS].T

    flat = jnp.concatenate([g[k].reshape(-1) for k in SMALL])
    nflat = flat.shape[0]
    rows = -(-nflat // (8 * LANES)) * 8
    packed = jnp.pad(flat, (0, rows * LANES - nflat)).reshape(rows, LANES)
    total = _all_reduce_small(packed, "reduce_small").reshape(-1)
    grad = {}
    off = 0
    for k in SMALL:
        size = int(np.prod(w[k].shape))
        grad[k] = total[off:off + size].reshape(w[k].shape)
        off += size

    for k in BIG:
        grad[k] = jnp.stack(grad_layers[k])

    loss = lax.psum(loss[0, 0], ("x", "y", "c"))
    outs = {k: _adamw(w[k], grad[k], mom[k], var[k], "adamw_" + k) for k in WEIGHTS}
    return (loss, grad_x[None], *[grad[k] for k in WEIGHTS], *[outs[k][0] for k in WEIGHTS],
            *[outs[k][1] for k in WEIGHTS], *[outs[k][2] for k in WEIGHTS])
```

```python
import functools
import math

import numpy as np
import jax
import jax.numpy as jnp
from jax import lax
from jax.experimental import pallas as pl
from jax.experimental.pallas import tpu as pltpu

F32 = jnp.float32
BF16 = jnp.bfloat16

HEAD_DIM = 64
N_GROUPS = 3
DILATIONS = (1, 4, 16)
SUB_WINDOW = 128
BLOCK = 128
CONV_WIDTH = 31
CONV_TAPS_PADDED = 32
NUM_BUCKETS = 32
MAX_REL_DISTANCE = 2048
EPS = 1e-6
NEG_INF = -1e30
LANES = 128

ADAM_LR = 0.001
ADAM_B1 = 0.9
ADAM_B2 = 0.999
ADAM_EPS = 1e-08
ADAM_WD = 0.01
ADAM_STEP = 10

N_DEV = 8
VMEM_LIMIT = 56 * 1024 * 1024
MESH = pl.DeviceIdType.MESH


def _cparams(sem=None):
    return pltpu.CompilerParams(dimension_semantics=sem, vmem_limit_bytes=VMEM_LIMIT)


def _tile(n, target):
    if n <= target:
        return n
    t = (target // LANES) * LANES
    while t >= LANES:
        if n % t == 0:
            return t
        t -= LANES
    return n


def _sigmoid(v):
    return 1.0 / (1.0 + jnp.exp(-v))


MM_VMEM_BUDGET = 40 * 1024 * 1024


def _mm_tiles(m, n, kdim, a_bytes, b_bytes, io_bytes):
    tm, tn, tk = _tile(m, 1024), _tile(n, 1024), _tile(kdim, 2048)

    def need(tm, tn, tk):
        blocks = 2 * (tm * tk * a_bytes + tk * tn * b_bytes + tm * tn * io_bytes)
        casts = (tm * tk * 2 if a_bytes == 4 else 0) + (tk * tn * 2 if b_bytes == 4 else 0)
        return blocks + casts + 2 * tm * tn * 4

    while need(tm, tn, tk) > MM_VMEM_BUDGET:
        if tk >= 1024 and tk % 256 == 0:
            tk //= 2
        elif tn >= 512 and tn % 256 == 0:
            tn //= 2
        else:
            tm //= 2
    return tm, tn, tk


def _mm(a, b, *, ta=False, tb=False, out_dtype=F32, epi=None, extra=None, after=None, out_slots=False, name):
    m = a.shape[1] if ta else a.shape[0]
    kdim = a.shape[0] if ta else a.shape[1]
    n = b.shape[0] if tb else b.shape[1]
    io_bytes = (6 if epi == "relu2" else jnp.dtype(out_dtype).itemsize) + (4 if extra is not None else 0)
    tm, tn, tk = _mm_tiles(m, n // N_DEV if out_slots else n, kdim, a.dtype.itemsize, b.dtype.itemsize, io_bytes)
    if out_slots:
        assert epi is None and tn == n // N_DEV
    nk = kdim // tk
    a_spec = pl.BlockSpec((tk, tm), lambda i, j, k: (k, i)) if ta else pl.BlockSpec((tm, tk), lambda i, j, k: (i, k))
    b_spec = pl.BlockSpec((tn, tk), lambda i, j, k: (j, k)) if tb else pl.BlockSpec((tk, tn), lambda i, j, k: (k, j))
    o_spec = (pl.BlockSpec((None, tm, tn), lambda i, j, k: (j, i, 0)) if out_slots
              else pl.BlockSpec((tm, tn), lambda i, j, k: (i, j)))
    dims = (((0 if ta else 1,), (1 if tb else 0,)), ((), ()))
    n_in = 2 + (extra is not None) + (after is not None)
    n_out = 2 if epi == "relu2" else 1

    def body(*refs):
        a_ref, b_ref = refs[0], refs[1]
        e_ref = refs[2] if extra is not None else None
        outs = refs[n_in:n_in + n_out]

        def product():
            return lax.dot_general(a_ref[...].astype(BF16), b_ref[...].astype(BF16), dims, preferred_element_type=F32)

        def finish(acc):
            if epi is None:
                outs[0][...] = acc.astype(outs[0].dtype)
            elif epi == "res":
                outs[0][...] = (e_ref[...] + acc).astype(outs[0].dtype)
            elif epi == "relu2":
                outs[0][...] = acc
                r = jnp.maximum(acc, 0.0)
                outs[1][...] = (r * r).astype(BF16)
            elif epi == "drelu2":
                outs[0][...] = (acc * (2.0 * jnp.maximum(e_ref[...], 0.0))).astype(outs[0].dtype)

        if nk == 1:
            finish(product())
            return
        acc_ref = refs[-1]
        k = pl.program_id(2)

        @pl.when(k == 0)
        def _():
            acc_ref[...] = product()

        @pl.when(jnp.logical_and(k > 0, k < nk - 1))
        def _():
            acc_ref[...] += product()

        @pl.when(k == nk - 1)
        def _():
            finish(acc_ref[...] + product())

    in_specs = ([a_spec, b_spec] + ([o_spec] if extra is not None else [])
                + ([pl.BlockSpec(memory_space=pl.ANY)] if after is not None else []))
    if epi == "relu2":
        out_shape = (jax.ShapeDtypeStruct((m, n), F32), jax.ShapeDtypeStruct((m, n), BF16))
        out_specs = (o_spec, o_spec)
    else:
        out_shape = jax.ShapeDtypeStruct((N_DEV, m, tn) if out_slots else (m, n), out_dtype)
        out_specs = o_spec
    args = (a, b) + ((extra,) if extra is not None else ()) + ((after,) if after is not None else ())
    return pl.pallas_call(
        body, name=name, grid=(m // tm, n // tn, nk), in_specs=in_specs, out_specs=out_specs, out_shape=out_shape,
        scratch_shapes=[pltpu.VMEM((tm, tn), F32)] if nk > 1 else [],
        compiler_params=_cparams(("parallel", "parallel", "arbitrary")),
    )(*args)


ROWS = 256


def _rms_fwd(x, g, name):
    t, d = x.shape

    def body(x_ref, g_ref, h_ref):
        xv = x_ref[...]
        r = lax.rsqrt(jnp.mean(xv * xv, axis=-1, keepdims=True) + EPS)
        h_ref[...] = (xv * r * g_ref[...]).astype(BF16)

    return pl.pallas_call(
        body, name=name, grid=(t // ROWS,),
        in_specs=[pl.BlockSpec((ROWS, d), lambda i: (i, 0)), pl.BlockSpec((1, d), lambda i: (0, 0))],
        out_specs=pl.BlockSpec((ROWS, d), lambda i: (i, 0)),
        out_shape=jax.ShapeDtypeStruct((t, d), BF16), compiler_params=_cparams(("arbitrary",)),
    )(x, g.reshape(1, d))


def _rms_bwd(dh, x, g, dres, name):
    t, d = x.shape

    def body(dh_ref, x_ref, g_ref, dres_ref, dx_ref, dg_ref):
        @pl.when(pl.program_id(0) == 0)
        def _():
            dg_ref[...] = jnp.zeros_like(dg_ref)

        xv = x_ref[...]
        dhv = dh_ref[...]
        r = lax.rsqrt(jnp.mean(xv * xv, axis=-1, keepdims=True) + EPS)
        xh = xv * r
        dg_ref[...] += jnp.sum(dhv * xh, axis=0, keepdims=True)
        dxh = dhv * g_ref[...]
        dx_ref[...] = dres_ref[...] + r * (dxh - xh * jnp.mean(dxh * xh, axis=-1, keepdims=True))

    row = pl.BlockSpec((ROWS, d), lambda i: (i, 0))
    vec = pl.BlockSpec((1, d), lambda i: (0, 0))
    return pl.pallas_call(
        body, name=name, grid=(t // ROWS,), in_specs=[row, row, vec, row], out_specs=(row, vec),
        out_shape=(jax.ShapeDtypeStruct((t, d), F32), jax.ShapeDtypeStruct((1, d), F32)),
        compiler_params=_cparams(("arbitrary",)),
    )(dh, x, g.reshape(1, d), dres)


def _gate_fwd(u, yc, ya, gate_col, name):
    t, d = yc.shape
    td = math.gcd(_tile(d, 512), gate_col)
    nd = d // td
    c0 = gate_col // td

    def body(gc_ref, ga_ref, yc_ref, ya_ref, m_ref):
        m_ref[...] = (_sigmoid(gc_ref[...]) * yc_ref[...] + _sigmoid(ga_ref[...]) * ya_ref[...]).astype(BF16)

    blk = pl.BlockSpec((ROWS, td), lambda i, j: (i, j))
    return pl.pallas_call(
        body, name=name, grid=(t // ROWS, nd),
        in_specs=[pl.BlockSpec((ROWS, td), lambda i, j: (i, c0 + j)),
                  pl.BlockSpec((ROWS, td), lambda i, j: (i, c0 + nd + j)), blk, blk],
        out_specs=blk, out_shape=jax.ShapeDtypeStruct((t, d), BF16),
        compiler_params=_cparams(("arbitrary", "arbitrary")),
    )(u, u, yc, ya)


def _gate_bwd(dm, u, yc, ya, gate_col, name):
    t, d = yc.shape
    td = math.gcd(_tile(d, 512), gate_col)
    nd = d // td
    c0 = gate_col // td

    def body(dm_ref, gc_ref, ga_ref, yc_ref, ya_ref, dyc_ref, dya_ref, dugc_ref, duga_ref):
        dmv = dm_ref[...]
        gc = _sigmoid(gc_ref[...])
        ga = _sigmoid(ga_ref[...])
        dyc_ref[...] = (dmv * gc).astype(BF16)
        dya_ref[...] = (dmv * ga).astype(BF16)
        dugc_ref[...] = (dmv * yc_ref[...] * gc * (1.0 - gc)).astype(BF16)
        duga_ref[...] = (dmv * ya_ref[...] * ga * (1.0 - ga)).astype(BF16)

    blk = pl.BlockSpec((ROWS, td), lambda i, j: (i, j))
    o = jax.ShapeDtypeStruct((t, d), BF16)
    return pl.pallas_call(
        body, name=name, grid=(t // ROWS, nd),
        in_specs=[blk, pl.BlockSpec((ROWS, td), lambda i, j: (i, c0 + j)),
                  pl.BlockSpec((ROWS, td), lambda i, j: (i, c0 + nd + j)), blk, blk],
        out_specs=(blk, blk, blk, blk), out_shape=(o, o, o, o),
        compiler_params=_cparams(("arbitrary", "arbitrary")),
    )(dm, u, u, yc, ya)


def _loss_and_grad(y, target, name):
    t, d = y.shape
    n = t // ROWS

    def body(y_ref, t_ref, loss_ref, dy_ref, acc_ref):
        i = pl.program_id(0)

        @pl.when(i == 0)
        def _():
            acc_ref[...] = jnp.zeros_like(acc_ref)

        diff = y_ref[...] - t_ref[...]
        dy_ref[...] = diff * (1.0 / d)
        acc_ref[...] += jnp.sum(diff * diff, axis=0, keepdims=True)

        @pl.when(i == n - 1)
        def _():
            loss_ref[...] = jnp.sum(acc_ref[...], axis=-1, keepdims=True) * (0.5 / d)

    row = pl.BlockSpec((ROWS, d), lambda i: (i, 0))
    return pl.pallas_call(
        body, name=name, grid=(n,), in_specs=[row, row],
        out_specs=(pl.BlockSpec((1, 1), lambda i: (0, 0)), row),
        out_shape=(jax.ShapeDtypeStruct((1, 1), F32), jax.ShapeDtypeStruct((t, d), F32)),
        scratch_shapes=[pltpu.VMEM((1, d), F32)], compiler_params=_cparams(("arbitrary",)),
    )(y, target)


HALO = 32


def _conv_fwd(u, w, b, cdim, name):
    t = u.shape[0]
    ncb = cdim // LANES
    nt = t // BLOCK

    def body(a_ref, g_ref, w_ref, b_ref, zc_ref, zpad):
        zpad[0:HALO, :] = jnp.zeros((HALO, LANES), F32)
        zpad[HALO:HALO + t, :] = a_ref[...] * _sigmoid(g_ref[...])
        wv = w_ref[...]
        bv = b_ref[...]

        def tile(i, carry):
            r0 = pl.multiple_of(i * BLOCK, BLOCK)
            win = zpad[pl.ds(r0, BLOCK + HALO), :]
            acc = jnp.zeros((BLOCK, LANES), F32) + bv
            for j in range(CONV_WIDTH):
                s = CONV_WIDTH - 1 - j
                sh = win if s == 0 else pltpu.roll(win, s, 0)
                acc = acc + wv[j:j + 1, :] * sh[HALO:HALO + BLOCK, :]
            zc_ref[pl.ds(r0, BLOCK), :] = acc
            return carry

        lax.fori_loop(0, nt, tile, 0)

    col = lambda off: pl.BlockSpec((t, LANES), lambda c: (0, off + c))
    return pl.pallas_call(
        body, name=name, grid=(ncb,),
        in_specs=[col(0), col(ncb), pl.BlockSpec((CONV_TAPS_PADDED, LANES), lambda c: (0, c)),
                  pl.BlockSpec((1, LANES), lambda c: (0, c))],
        out_specs=pl.BlockSpec((t, LANES), lambda c: (0, c)),
        out_shape=jax.ShapeDtypeStruct((t, cdim), F32),
        scratch_shapes=[pltpu.VMEM((t + HALO, LANES), F32)], compiler_params=_cparams(("arbitrary",)),
    )(u, u, w, b.reshape(1, cdim))


def _conv_bwd(dzc, u, w, cdim, name):
    t = u.shape[0]
    ncb = cdim // LANES
    nt = t // BLOCK
    win_rows = BLOCK + HALO

    def body(dzc_ref, a_ref, g_ref, w_ref, da_ref, dg_ref, dw_ref, db_ref, zpad, dpad):
        av = a_ref[...]
        sg = _sigmoid(g_ref[...])
        zpad[0:HALO, :] = jnp.zeros((HALO, LANES), F32)
        zpad[HALO:HALO + t, :] = av * sg
        dpad[0:t, :] = dzc_ref[...]
        dpad[t:t + HALO, :] = jnp.zeros((HALO, LANES), F32)
        dw_ref[...] = jnp.zeros_like(dw_ref)
        db_ref[...] = jnp.sum(dzc_ref[...], axis=0, keepdims=True)
        wv = w_ref[...]

        def tile(i, carry):
            r0 = pl.multiple_of(i * BLOCK, BLOCK)
            zwin = zpad[pl.ds(r0, win_rows), :]
            dwin = dpad[pl.ds(r0, win_rows), :]
            dcur = dwin[0:BLOCK, :]
            dz = jnp.zeros((BLOCK, LANES), F32)
            for j in range(CONV_WIDTH):
                s = CONV_WIDTH - 1 - j
                zs = zwin if s == 0 else pltpu.roll(zwin, s, 0)
                dw_ref[j:j + 1, :] += jnp.sum(dcur * zs[HALO:HALO + BLOCK, :], axis=0, keepdims=True)
                ds = dwin if s == 0 else pltpu.roll(dwin, win_rows - s, 0)
                dz = dz + wv[j:j + 1, :] * ds[0:BLOCK, :]
            ac = a_ref[pl.ds(r0, BLOCK), :]
            sc = _sigmoid(g_ref[pl.ds(r0, BLOCK), :])
            da_ref[pl.ds(r0, BLOCK), :] = (dz * sc).astype(BF16)
            dg_ref[pl.ds(r0, BLOCK), :] = (dz * ac * sc * (1.0 - sc)).astype(BF16)
            return carry

        lax.fori_loop(0, nt, tile, 0)

    col = lambda off: pl.BlockSpec((t, LANES), lambda c: (0, off + c))
    wspec = pl.BlockSpec((CONV_TAPS_PADDED, LANES), lambda c: (0, c))
    o = jax.ShapeDtypeStruct((t, cdim), BF16)
    return pl.pallas_call(
        body, name=name, grid=(ncb,), in_specs=[col(0), col(0), col(ncb), wspec],
        out_specs=(col(0), col(0), wspec, pl.BlockSpec((1, LANES), lambda c: (0, c))),
        out_shape=(o, o, jax.ShapeDtypeStruct((CONV_TAPS_PADDED, cdim), F32), jax.ShapeDtypeStruct((1, cdim), F32)),
        scratch_shapes=[pltpu.VMEM((t + HALO, LANES), F32), pltpu.VMEM((t + HALO, LANES), F32)],
        compiler_params=_cparams(("arbitrary",)),
    )(dzc, u, u, w)


def _ln_swish_fwd(zc, g, b, name):
    t, c = zc.shape

    def body(z_ref, g_ref, b_ref, o_ref):
        z = z_ref[...]
        mu = jnp.mean(z, axis=-1, keepdims=True)
        zc_ = z - mu
        zn = zc_ * lax.rsqrt(jnp.mean(zc_ * zc_, axis=-1, keepdims=True) + EPS)
        y = zn * g_ref[...] + b_ref[...]
        o_ref[...] = (y * _sigmoid(y)).astype(BF16)

    row = pl.BlockSpec((ROWS, c), lambda i: (i, 0))
    vec = pl.BlockSpec((1, c), lambda i: (0, 0))
    return pl.pallas_call(
        body, name=name, grid=(t // ROWS,), in_specs=[row, vec, vec], out_specs=row,
        out_shape=jax.ShapeDtypeStruct((t, c), BF16), compiler_params=_cparams(("arbitrary",)),
    )(zc, g.reshape(1, c), b.reshape(1, c))


def _ln_swish_bwd(dzs, zc, g, b, name):
    t, c = zc.shape

    def body(d_ref, z_ref, g_ref, b_ref, dz_ref, dg_ref, db_ref):
        @pl.when(pl.program_id(0) == 0)
        def _():
            dg_ref[...] = jnp.zeros_like(dg_ref)
            db_ref[...] = jnp.zeros_like(db_ref)

        z = z_ref[...]
        mu = jnp.mean(z, axis=-1, keepdims=True)
        zc_ = z - mu
        rstd = lax.rsqrt(jnp.mean(zc_ * zc_, axis=-1, keepdims=True) + EPS)
        zn = zc_ * rstd
        y = zn * g_ref[...] + b_ref[...]
        sg = _sigmoid(y)
        dy = d_ref[...] * (sg * (1.0 + y * (1.0 - sg)))
        dg_ref[...] += jnp.sum(dy * zn, axis=0, keepdims=True)
        db_ref[...] += jnp.sum(dy, axis=0, keepdims=True)
        dzn = dy * g_ref[...]
        dz_ref[...] = rstd * (dzn - jnp.mean(dzn, axis=-1, keepdims=True)
                              - zn * jnp.mean(dzn * zn, axis=-1, keepdims=True))

    row = pl.BlockSpec((ROWS, c), lambda i: (i, 0))
    vec = pl.BlockSpec((1, c), lambda i: (0, 0))
    v = jax.ShapeDtypeStruct((1, c), F32)
    return pl.pallas_call(
        body, name=name, grid=(t // ROWS,), in_specs=[row, row, vec, vec], out_specs=(row, vec, vec),
        out_shape=(jax.ShapeDtypeStruct((t, c), F32), v, v), compiler_params=_cparams(("arbitrary",)),
    )(dzs, zc, g.reshape(1, c), b.reshape(1, c))


def _bucket_table():
    qi = np.arange(BLOCK)[:, None]
    kj = np.arange(2 * BLOCK)[None, :]
    off = qi + BLOCK - kj
    band = (off >= 0) & (off <= SUB_WINDOW)
    max_exact = NUM_BUCKETS // 2
    out = []
    for d in DILATIONS:
        dist = (np.clip(off, 0, SUB_WINDOW) * d).astype(np.int32)
        nf = np.maximum(dist, 1).astype(np.float32)
        large = max_exact + (np.log(nf / np.float32(max_exact)) / np.float32(math.log(MAX_REL_DISTANCE / max_exact))
                             * np.float32(NUM_BUCKETS - max_exact)).astype(np.int32)
        large = np.minimum(large, NUM_BUCKETS - 1)
        bucket = np.where(dist < max_exact, dist, large)
        out.append(np.where(band, bucket, -1))
    return np.stack(out).astype(np.int32)


def _bias_expand(rel_bias, buckets, hpg, name):
    nh = N_GROUPS * hpg

    def body(rb_ref, bk_ref, o_ref):
        h = pl.program_id(0)
        bk = bk_ref[0]
        acc = jnp.full((BLOCK, 2 * BLOCK), NEG_INF, F32)
        for bb in range(NUM_BUCKETS):
            acc = jnp.where(bk == bb, rb_ref[bb, h], acc)
        o_ref[0] = acc

    return pl.pallas_call(
        body, name=name, grid=(nh,),
        in_specs=[pl.BlockSpec(memory_space=pltpu.SMEM),
                  pl.BlockSpec((1, BLOCK, 2 * BLOCK), lambda h: (h // hpg, 0, 0))],
        out_specs=pl.BlockSpec((1, BLOCK, 2 * BLOCK), lambda h: (h, 0, 0)),
        out_shape=jax.ShapeDtypeStruct((nh, BLOCK, 2 * BLOCK), F32), compiler_params=_cparams(("arbitrary",)),
    )(rel_bias, buckets)


def _bias_reduce(ds_sum, buckets, hpg, name):
    nh = N_GROUPS * hpg

    def body(ds_ref, bk_ref, o_ref):
        bk = bk_ref[0]
        dsv = ds_ref[0]
        lane = lax.broadcasted_iota(jnp.int32, (1, LANES), 1)
        row = jnp.zeros((1, LANES), F32)
        for bb in range(NUM_BUCKETS):
            tot = jnp.sum(jnp.sum(jnp.where(bk == bb, dsv, 0.0), axis=-1, keepdims=True), axis=0, keepdims=True)
            row = jnp.where(lane == bb, tot, row)
        o_ref[0] = row

    return pl.pallas_call(
        body, name=name, grid=(nh,),
        in_specs=[pl.BlockSpec((1, BLOCK, 2 * BLOCK), lambda h: (h, 0, 0)),
                  pl.BlockSpec((1, BLOCK, 2 * BLOCK), lambda h: (h // hpg, 0, 0))],
        out_specs=pl.BlockSpec((1, 1, LANES), lambda h: (h, 0, 0)),
        out_shape=jax.ShapeDtypeStruct((nh, 1, LANES), F32), compiler_params=_cparams(("arbitrary",)),
    )(ds_sum, buckets)


def _chunk_rows(c, d, nb):
    r, n = c // nb, c % nb
    if d == 1:
        return pl.ds(c * BLOCK, BLOCK)
    return pl.ds(r + n * BLOCK * d, BLOCK, stride=d)


def _segment_ones():
    i = lax.broadcasted_iota(jnp.int32, (LANES, LANES), 0) // HEAD_DIM
    j = lax.broadcasted_iota(jnp.int32, (LANES, LANES), 1) // HEAD_DIM
    return (i == j).astype(BF16)


def _segment_sum(v, seg):
    hi = v.astype(BF16)
    lo = (v - hi.astype(F32)).astype(BF16)
    return jnp.dot(hi, seg, preferred_element_type=F32) + jnp.dot(lo, seg, preferred_element_type=F32)


def _head_mean(v, seg):
    return _segment_sum(v, seg) * (1.0 / HEAD_DIM)


def _attn_fwd(u, qg2, kg2, bias, gi, cols, hpg, name):
    t = u.shape[0]
    d = DILATIONS[gi]
    nchunk = t // BLOCK
    nb = (t // d) // BLOCK
    hp = hpg // 2
    qc0, kc0, vc0 = [(c + gi * hpg * HEAD_DIM) // LANES for c in cols]
    contract_lanes = (((1,), (1,)), ((), ()))

    def body(q_ref, k_ref, v_ref, qg_ref, kg_ref, bias_ref, o_ref, lse_ref, qd, kd, vd, od, ld, sbuf):
        seg = _segment_ones()
        lane = lax.broadcasted_iota(jnp.int32, (1, LANES), 1)
        qg = qg_ref[...] * (HEAD_DIM ** -0.5)
        kg = kg_ref[...]
        kd[0:BLOCK, :] = jnp.zeros((BLOCK, LANES), BF16)
        vd[0:BLOCK, :] = jnp.zeros((BLOCK, LANES), BF16)
        for c in range(nchunk):
            rows = _chunk_rows(c, d, nb)
            qv = q_ref[rows, :]
            kv = k_ref[rows, :]
            qd[c * BLOCK:(c + 1) * BLOCK, :] = (qv * lax.rsqrt(_head_mean(qv * qv, seg) + EPS) * qg).astype(BF16)
            kd[(c + 1) * BLOCK:(c + 2) * BLOCK, :] = (kv * lax.rsqrt(_head_mean(kv * kv, seg) + EPS) * kg).astype(BF16)
            vd[(c + 1) * BLOCK:(c + 2) * BLOCK, :] = v_ref[rows, :].astype(BF16)

        col = lax.broadcasted_iota(jnp.int32, (BLOCK, 2 * BLOCK), 1)
        for j in range(2):
            mj = jnp.logical_and(lane >= j * HEAD_DIM, lane < (j + 1) * HEAD_DIM)
            for c in range(nchunk):
                kw = kd[c * BLOCK:(c + 2) * BLOCK, :]
                kj = jnp.where(mj, kw, jnp.zeros_like(kw))
                s = lax.dot_general(qd[c * BLOCK:(c + 1) * BLOCK, :], kj, contract_lanes,
                                    preferred_element_type=F32) + bias_ref[j]
                if c % nb == 0:
                    s = jnp.where(col < BLOCK, NEG_INF, s)
                sbuf[c] = s
            for c in range(nchunk):
                rows = slice(c * BLOCK, (c + 1) * BLOCK)
                s = sbuf[c]
                mx = jnp.max(s, axis=-1, keepdims=True)
                p = jnp.exp(s - mx).astype(BF16)
                vw = vd[c * BLOCK:(c + 2) * BLOCK, :]
                oj = jnp.dot(p, jnp.where(mj, vw, jnp.ones_like(vw)), preferred_element_type=F32)
                l = pltpu.roll(oj, HEAD_DIM, 1)
                on = oj / l
                ls = mx + jnp.log(l)
                if j == 0:
                    od[rows, :] = on
                    ld[rows, :] = ls
                else:
                    od[rows, :] = jnp.where(mj, on, od[rows, :])
                    ld[rows, :] = jnp.where(mj, ls, ld[rows, :])

        for c in range(nchunk):
            rows = _chunk_rows(c, d, nb)
            o_ref[rows, :] = od[c * BLOCK:(c + 1) * BLOCK, :]
            lse_ref[rows, :] = ld[c * BLOCK:(c + 1) * BLOCK, :]

    ucol = lambda c0: pl.BlockSpec((t, LANES), lambda h: (0, c0 + h))
    vec = pl.BlockSpec((1, LANES), lambda h: (0, 0))
    oblk = pl.BlockSpec((t, LANES), lambda h: (0, h))
    osh = jax.ShapeDtypeStruct((t, hpg * HEAD_DIM), F32)
    return pl.pallas_call(
        body, name=name, grid=(hp,),
        in_specs=[ucol(qc0), ucol(kc0), ucol(vc0), vec, vec,
                  pl.BlockSpec((2, BLOCK, 2 * BLOCK), lambda h: (gi * hp + h, 0, 0))],
        out_specs=(oblk, oblk), out_shape=(osh, osh),
        scratch_shapes=[pltpu.VMEM((t, LANES), BF16), pltpu.VMEM((t + BLOCK, LANES), BF16),
                        pltpu.VMEM((t + BLOCK, LANES), BF16), pltpu.VMEM((t, LANES), F32), pltpu.VMEM((t, LANES), F32),
                        pltpu.VMEM((nchunk, BLOCK, 2 * BLOCK), F32)],
        compiler_params=_cparams(("arbitrary",)),
    )(u, u, u, qg2, kg2, bias)


def _attn_bwd(u, do_g, dd_g, lse_g, qg2, kg2, bias, ds_in, gi, cols, hpg, name):
    t = u.shape[0]
    d = DILATIONS[gi]
    nchunk = t // BLOCK
    nb = (t // d) // BLOCK
    hp = hpg // 2
    qc0, kc0, vc0 = [(c + gi * hpg * HEAD_DIM) // LANES for c in cols]
    contract_lanes = (((1,), (1,)), ((), ()))
    contract_rows = (((0,), (0,)), ((), ()))
    qscale = HEAD_DIM ** -0.5

    def body(q_ref, k_ref, v_ref, do_ref, dd_ref, lse_ref, qg_ref, kg_ref, bias_ref, dsin_ref,
             dq_ref, dk_ref, dv_ref, dgq_ref, dgk_ref, dsout_ref,
             qd, kd, vd, dod, ddd, ld, dqd, dkd, dvd, dsacc, pbuf, dsbuf):
        seg = _segment_ones()
        lane = lax.broadcasted_iota(jnp.int32, (1, LANES), 1)
        qg = qg_ref[...] * qscale
        kg = kg_ref[...]
        kd[0:BLOCK, :] = jnp.zeros((BLOCK, LANES), BF16)
        vd[0:BLOCK, :] = jnp.zeros((BLOCK, LANES), BF16)
        dsacc[...] = jnp.zeros_like(dsacc)
        for c in range(nchunk):
            rows = _chunk_rows(c, d, nb)
            qv = q_ref[rows, :]
            kv = k_ref[rows, :]
            qd[c * BLOCK:(c + 1) * BLOCK, :] = (qv * lax.rsqrt(_head_mean(qv * qv, seg) + EPS) * qg).astype(BF16)
            kd[(c + 1) * BLOCK:(c + 2) * BLOCK, :] = (kv * lax.rsqrt(_head_mean(kv * kv, seg) + EPS) * kg).astype(BF16)
            vd[(c + 1) * BLOCK:(c + 2) * BLOCK, :] = v_ref[rows, :].astype(BF16)
            dod[c * BLOCK:(c + 1) * BLOCK, :] = do_ref[rows, :].astype(BF16)
            ddd[c * BLOCK:(c + 1) * BLOCK, :] = dd_ref[rows, :]
            ld[c * BLOCK:(c + 1) * BLOCK, :] = lse_ref[rows, :]

        col = lax.broadcasted_iota(jnp.int32, (BLOCK, 2 * BLOCK), 1)
        for j in range(2):
            mj = jnp.logical_and(lane >= j * HEAD_DIM, lane < (j + 1) * HEAD_DIM)
            first = lane == j * HEAD_DIM
            for c in range(nchunk):
                rows = slice(c * BLOCK, (c + 1) * BLOCK)
                kw = kd[c * BLOCK:(c + 2) * BLOCK, :]
                vw = vd[c * BLOCK:(c + 2) * BLOCK, :]
                kj = jnp.where(mj, kw, jnp.zeros_like(kw))
                vj = jnp.where(mj, vw, jnp.zeros_like(vw))
                s = lax.dot_general(qd[rows, :], kj, contract_lanes, preferred_element_type=F32) + bias_ref[j]
                if c % nb == 0:
                    s = jnp.where(col < BLOCK, NEG_INF, s)
                dp = lax.dot_general(dod[rows, :], vj, contract_lanes, preferred_element_type=F32)
                lse_j = jnp.sum(jnp.where(first, ld[rows, :], 0.0), axis=-1, keepdims=True)
                dd_j = jnp.sum(jnp.where(first, ddd[rows, :], 0.0), axis=-1, keepdims=True)
                p = jnp.exp(s - lse_j)
                ds = p * (dp + dd_j)
                dsacc[j] += ds
                pbuf[j, c] = p.astype(BF16)
                dsbuf[j, c] = ds.astype(BF16)
        for c in range(nchunk):
            rows = slice(c * BLOCK, (c + 1) * BLOCK)
            nxt = slice((c + 1) * BLOCK, (c + 2) * BLOCK)
            has_next = c + 1 < nchunk and (c + 1) % nb != 0
            kw = kd[c * BLOCK:(c + 2) * BLOCK, :]
            dq = jnp.zeros((BLOCK, LANES), F32)
            dk = jnp.zeros((BLOCK, LANES), F32)
            dv = jnp.zeros((BLOCK, LANES), F32)
            for j in range(2):
                mj = jnp.logical_and(lane >= j * HEAD_DIM, lane < (j + 1) * HEAD_DIM)
                zero = jnp.zeros((BLOCK, LANES), BF16)
                dq = dq + jnp.dot(dsbuf[j, c], jnp.where(mj, kw, jnp.zeros_like(kw)), preferred_element_type=F32)
                dk = dk + lax.dot_general(dsbuf[j, c, :, BLOCK:], jnp.where(mj, qd[rows, :], zero), contract_rows,
                                          preferred_element_type=F32)
                dv = dv + lax.dot_general(pbuf[j, c, :, BLOCK:], jnp.where(mj, dod[rows, :], zero), contract_rows,
                                          preferred_element_type=F32)
                if has_next:
                    dk = dk + lax.dot_general(dsbuf[j, c + 1, :, :BLOCK], jnp.where(mj, qd[nxt, :], zero),
                                              contract_rows, preferred_element_type=F32)
                    dv = dv + lax.dot_general(pbuf[j, c + 1, :, :BLOCK], jnp.where(mj, dod[nxt, :], zero),
                                              contract_rows, preferred_element_type=F32)
            dqd[rows, :] = dq
            dkd[rows, :] = dk
            dvd[rows, :] = dv

        dsout_ref[...] = dsin_ref[...] + dsacc[...]

        dgq = jnp.zeros((1, LANES), F32)
        dgk = jnp.zeros((1, LANES), F32)
        for c in range(nchunk):
            rows = _chunk_rows(c, d, nb)
            qv = q_ref[rows, :]
            rq = lax.rsqrt(_head_mean(qv * qv, seg) + EPS)
            qh = qv * rq
            dy = dqd[c * BLOCK:(c + 1) * BLOCK, :]
            dgq = dgq + jnp.sum(dy * qh, axis=0, keepdims=True) * qscale
            dxh = dy * qg
            ddd[rows, :] = rq * (dxh - qh * _head_mean(dxh * qh, seg))
            kv = k_ref[rows, :]
            rk = lax.rsqrt(_head_mean(kv * kv, seg) + EPS)
            kh = kv * rk
            dy = dkd[c * BLOCK:(c + 1) * BLOCK, :]
            dgk = dgk + jnp.sum(dy * kh, axis=0, keepdims=True)
            dxh = dy * kg
            ld[rows, :] = rk * (dxh - kh * _head_mean(dxh * kh, seg))
        dq_ref[...] = ddd[...].astype(BF16)
        dk_ref[...] = ld[...].astype(BF16)
        for c in range(nchunk):
            ddd[_chunk_rows(c, d, nb), :] = dvd[c * BLOCK:(c + 1) * BLOCK, :]
        dv_ref[...] = ddd[...].astype(BF16)
        dgq_ref[0] = dgq
        dgk_ref[0] = dgk

    ucol = lambda c0: pl.BlockSpec((t, LANES), lambda h: (0, c0 + h))
    vec = pl.BlockSpec((1, LANES), lambda h: (0, 0))
    oblk = pl.BlockSpec((t, LANES), lambda h: (0, h))
    bblk = pl.BlockSpec((2, BLOCK, 2 * BLOCK), lambda h: (gi * hp + h, 0, 0))
    gblk = pl.BlockSpec((1, 1, LANES), lambda h: (h, 0, 0))
    osh = jax.ShapeDtypeStruct((t, hpg * HEAD_DIM), BF16)
    gsh = jax.ShapeDtypeStruct((hp, 1, LANES), F32)
    return pl.pallas_call(
        body, name=name, grid=(hp,),
        in_specs=[ucol(qc0), ucol(kc0), ucol(vc0), oblk, oblk, oblk, vec, vec, bblk, bblk],
        out_specs=(oblk, oblk, oblk, gblk, gblk, bblk),
        out_shape=(osh, osh, osh, gsh, gsh, jax.ShapeDtypeStruct(ds_in.shape, F32)),
        input_output_aliases={9: 5},
        scratch_shapes=[pltpu.VMEM((t, LANES), BF16), pltpu.VMEM((t + BLOCK, LANES), BF16),
                        pltpu.VMEM((t + BLOCK, LANES), BF16), pltpu.VMEM((t, LANES), BF16),
                        pltpu.VMEM((t, LANES), F32), pltpu.VMEM((t, LANES), F32), pltpu.VMEM((t, LANES), F32),
                        pltpu.VMEM((t, LANES), F32), pltpu.VMEM((t, LANES), F32),
                        pltpu.VMEM((2, BLOCK, 2 * BLOCK), F32), pltpu.VMEM((2, nchunk, BLOCK, 2 * BLOCK), BF16),
                        pltpu.VMEM((2, nchunk, BLOCK, 2 * BLOCK), BF16)],
        compiler_params=_cparams(("arbitrary",)),
    )(u, u, u, do_g, dd_g, lse_g, qg2, kg2, bias, ds_in)


def _group_weights(l0, l1, l2):
    mx = jnp.maximum(jnp.maximum(l0, l1), l2)
    e0, e1, e2 = jnp.exp(l0 - mx), jnp.exp(l1 - mx), jnp.exp(l2 - mx)
    inv = 1.0 / (e0 + e1 + e2)
    return e0 * inv, e1 * inv, e2 * inv


def _combine_fwd(os_, lses, name):
    t, ao = os_[0].shape

    def body(o0, o1, o2, l0, l1, l2, o_ref):
        w0, w1, w2 = _group_weights(l0[...], l1[...], l2[...])
        o_ref[...] = (w0 * o0[...] + w1 * o1[...] + w2 * o2[...]).astype(BF16)

    row = pl.BlockSpec((ROWS, ao), lambda i: (i, 0))
    return pl.pallas_call(
        body, name=name, grid=(t // ROWS,), in_specs=[row] * 6, out_specs=row,
        out_shape=jax.ShapeDtypeStruct((t, ao), BF16), compiler_params=_cparams(("arbitrary",)),
    )(*os_, *lses)


def _combine_bwd(do, os_, lses, name):
    t, ao = do.shape
    idx = np.arange(ao) // HEAD_DIM
    seg = jnp.asarray((idx[:, None] == idx[None, :]).astype(np.float32), dtype=BF16)

    def body(do_ref, o0, o1, o2, l0, l1, l2, seg_ref, g0, g1, g2, d0, d1, d2):
        w0, w1, w2 = _group_weights(l0[...], l1[...], l2[...])
        dov = do_ref[...]
        o = w0 * o0[...] + w1 * o1[...] + w2 * o2[...]
        sd = _segment_sum(dov * o, seg_ref[...])
        for w, gref, dref in ((w0, g0, d0), (w1, g1, d1), (w2, g2, d2)):
            gref[...] = w * dov
            dref[...] = -(w * sd)

    row = pl.BlockSpec((ROWS, ao), lambda i: (i, 0))
    sh = jax.ShapeDtypeStruct((t, ao), F32)
    outs = pl.pallas_call(
        body, name=name, grid=(t // ROWS,), in_specs=[row] * 7 + [pl.BlockSpec((ao, ao), lambda i: (0, 0))],
        out_specs=(row,) * 6, out_shape=(sh,) * 6, compiler_params=_cparams(("arbitrary",)),
    )(do, *os_, *lses, seg)
    return outs[:3], outs[3:]


def _adamw(w, g, m, v, name):
    shape = w.shape
    cols = shape[-1]
    rows = int(np.prod(shape[:-1]))
    tr = rows if rows <= 512 else _tile_rows(rows)
    c1 = 1.0 - ADAM_B1 ** ADAM_STEP
    c2 = 1.0 - ADAM_B2 ** ADAM_STEP

    def body(w_ref, g_ref, m_ref, v_ref, d_ref, nm_ref, nv_ref):
        gv = g_ref[...]
        mn = ADAM_B1 * m_ref[...] + (1.0 - ADAM_B1) * gv
        vn = ADAM_B2 * v_ref[...] + (1.0 - ADAM_B2) * (gv * gv)
        nm_ref[...] = mn
        nv_ref[...] = vn
        d_ref[...] = -ADAM_LR * ((mn / c1) / (jnp.sqrt(vn / c2) + ADAM_EPS) + ADAM_WD * w_ref[...])

    blk = pl.BlockSpec((tr, cols), lambda i: (i, 0))
    sh = jax.ShapeDtypeStruct((rows, cols), F32)
    outs = pl.pallas_call(
        body, name=name, grid=(rows // tr,), in_specs=[blk] * 4, out_specs=(blk,) * 3, out_shape=(sh,) * 3,
        compiler_params=_cparams(("arbitrary",)),
    )(*[a.reshape(rows, cols) for a in (w, g, m, v)])
    return tuple(o.reshape(shape) for o in outs)


def _tile_rows(rows):
    for t in (512, 256, 128, 64, 32, 16, 8):
        if rows % t == 0:
            return t
    return rows


def _sum_slots(recv, parts, me, name):
    _, rows, cols = recv.shape
    tr = rows if rows <= 512 else _tile_rows(rows)

    def body(me_ref, r_ref, own_ref, o_ref):
        acc = jnp.zeros(o_ref.shape, F32)
        for s in range(N_DEV):
            acc = acc + jnp.where(me_ref[0] == s, own_ref[...], r_ref[s]).astype(F32)
        o_ref[...] = acc

    return pl.pallas_call(
        body, name=name,
        grid_spec=pltpu.PrefetchScalarGridSpec(
            num_scalar_prefetch=1, grid=(rows // tr,),
            in_specs=[pl.BlockSpec((N_DEV, tr, cols), lambda i, me: (0, i, 0)),
                      pl.BlockSpec((None, tr, cols), lambda i, me: (me[0], i, 0))],
            out_specs=pl.BlockSpec((tr, cols), lambda i, me: (i, 0))),
        out_shape=jax.ShapeDtypeStruct((rows, cols), F32), compiler_params=_cparams(("arbitrary",)),
    )(me.reshape(1), recv, parts)


def _peer(k):
    x, y, c = lax.axis_index("x"), lax.axis_index("y"), lax.axis_index("c")
    return (1 - x if k & 4 else x, 1 - y if k & 2 else y, 1 - c if k & 1 else c)


def _dev_index(p):
    return 4 * p[0] + 2 * p[1] + p[2]


HBM_SPEC = pl.BlockSpec(memory_space=pltpu.HBM)
SEM_SPEC = pl.BlockSpec(memory_space=pltpu.SEMAPHORE)
ANY_SPEC = pl.BlockSpec(memory_space=pl.ANY)
CHIPS = (4, 2, 6)


def _remote(src, dst, send_sem, recv_sem, to):
    return pltpu.make_async_remote_copy(src_ref=src, dst_ref=dst, send_sem=send_sem, recv_sem=recv_sem,
                                        device_id=to, device_id_type=MESH)


def _hbm(a):
    return pltpu.with_memory_space_constraint(a, pltpu.HBM)


def _split_call(body, name, bufs, sems_in, sem_out_sizes, after):
    nb, ns, no = len(bufs), len(sems_in), len(sem_out_sizes)
    extra = [] if after is None else [after]

    def kern(*refs):
        pos = nb + ns + len(extra)
        body(refs[:nb], refs[nb:nb + ns], refs[pos:pos + no])
        token_ref = refs[pos + no + nb]
        token_ref[...] = jnp.zeros_like(token_ref)

    out_shape = (tuple(pltpu.SemaphoreType.DMA((s,)) for s in sem_out_sizes)
                 + tuple(pltpu.HBM(b.shape, b.dtype) for b in bufs) + (jax.ShapeDtypeStruct((8, LANES), F32),))
    res = pl.pallas_call(
        kern, name=name, out_shape=out_shape,
        in_specs=[HBM_SPEC] * nb + [SEM_SPEC] * ns + [ANY_SPEC] * len(extra),
        out_specs=(SEM_SPEC,) * no + (HBM_SPEC,) * nb + (pl.BlockSpec(memory_space=pltpu.VMEM),),
        input_output_aliases={i: no + i for i in range(nb)},
        compiler_params=pltpu.CompilerParams(has_side_effects=pltpu.SideEffectType.DATAFLOW_SIDE_EFFECTING),
    )(*bufs, *sems_in, *extra)
    return res[:no], res[no:no + nb], res[no + nb]


def _gather_start(shards, lands, after, name):
    n = len(shards)

    def body(bufs, _, sems):
        ins, lnd = bufs[:n], bufs[n:]
        d2d_s, d2d_r, ici_s, ici_r = sems
        me = _dev_index(_peer(0))
        for j, k in enumerate(CHIPS):
            for i in range(n):
                _remote(ins[i], lnd[i].at[me], ici_s.at[j], ici_r.at[j], _peer(k)).start()
        for i in range(n):
            _remote(ins[i], lnd[i].at[me], d2d_s.at[0], d2d_r.at[0], _peer(1)).start()

    return _split_call(body, name, [_hbm(a) for a in (*shards, *lands)], [], (1, 1, 3, 3), after)


def _gather_forward(n, bufs, ici_r, after, name):
    def body(refs, sems_in, sems):
        ins, lnd = refs[:n], refs[n:]
        (arrived,) = sems_in
        fwd_s, fwd_r = sems
        for j, k in enumerate(CHIPS):
            blk = _dev_index(_peer(k))
            for i in range(n):
                _remote(ins[i], lnd[i].at[blk], fwd_s.at[j], arrived.at[j], _peer(k)).wait_recv()
            for i in range(n):
                _remote(lnd[i].at[blk], lnd[i].at[blk], fwd_s.at[j], fwd_r.at[j], _peer(1)).start()

    return _split_call(body, name, bufs, [ici_r], (3, 3), after)


def _gather_finish(n, bufs, d2d_s, d2d_r, ici_s, fwd_s, fwd_r, after, name):
    def body(refs, sems_in, _):
        ins, lnd = refs[:n], refs[n:]
        d2d_send, d2d_recv, ici_send, fwd_send, fwd_recv = sems_in
        sib = _peer(1)
        for i in range(n):
            cp = _remote(ins[i], lnd[i].at[_dev_index(sib)], d2d_send.at[0], d2d_recv.at[0], sib)
            cp.wait_send()
            cp.wait_recv()
        for j, k in enumerate(CHIPS):
            passed = _dev_index(_peer(k))
            landed = _dev_index(_peer(k | 1))
            for i in range(n):
                _remote(ins[i], lnd[i].at[passed], ici_send.at[j], fwd_recv.at[j], _peer(k)).wait_send()
                cp = _remote(lnd[i].at[passed], lnd[i].at[landed], fwd_send.at[j], fwd_recv.at[j], sib)
                cp.wait_send()
                cp.wait_recv()

    _, out, token = _split_call(body, name, bufs, [d2d_s, d2d_r, ici_s, fwd_s, fwd_r], (), after)
    return out[n:], token


def _exchange_start(parts, lands, after, name):
    n = len(parts)

    def body(bufs, _, sems):
        src, lnd = bufs[:n], bufs[n:]
        send, recv = sems
        me = _dev_index(_peer(0))
        for k in (4, 5, 2, 3, 6, 7, 1):
            to = _peer(k)
            for i in range(n):
                _remote(src[i].at[_dev_index(to)], lnd[i].at[me], send.at[k - 1], recv.at[k - 1], to).start()

    return _split_call(body, name, [_hbm(a) for a in (*parts, *lands)], [], (7, 7), after)


def _exchange_finish(n, bufs, send, recv, after, name):
    def body(refs, sems_in, _):
        src, lnd = refs[:n], refs[n:]
        send_, recv_ = sems_in
        me = _dev_index(_peer(0))
        for k in range(1, N_DEV):
            frm = _peer(k)
            for i in range(n):
                cp = _remote(src[i].at[me], lnd[i].at[_dev_index(frm)], send_.at[k - 1], recv_.at[k - 1], frm)
                cp.wait_send()
                cp.wait_recv()

    _, out, token = _split_call(body, name, bufs, [send, recv], (), after)
    return out[:n], out[n:], token


def _all_reduce_small(v, name):
    rows = v.shape[0]

    def body(v_ref, o_ref, buf, send_sems, recv_sems):
        me = _dev_index(_peer(0))
        buf[me] = v_ref[...]
        copies = []
        for k in range(1, N_DEV):
            copies.append(pltpu.make_async_remote_copy(
                src_ref=v_ref, dst_ref=buf.at[me], send_sem=send_sems.at[k - 1], recv_sem=recv_sems.at[k - 1],
                device_id=_peer(k), device_id_type=MESH))
        for cp in copies:
            cp.start()
        for k in range(1, N_DEV):
            pltpu.make_async_remote_copy(
                src_ref=v_ref, dst_ref=buf.at[_dev_index(_peer(k))], send_sem=send_sems.at[k - 1],
                recv_sem=recv_sems.at[k - 1], device_id=_peer(k), device_id_type=MESH).wait_recv()
        for cp in copies:
            cp.wait_send()
        acc = buf[0]
        for s in range(1, N_DEV):
            acc = acc + buf[s]
        o_ref[...] = acc

    vm = pl.BlockSpec(memory_space=pltpu.VMEM)
    return pl.pallas_call(
        body, name=name, in_specs=[vm], out_specs=vm, out_shape=jax.ShapeDtypeStruct(v.shape, F32),
        scratch_shapes=[pltpu.VMEM((N_DEV, rows, LANES), F32), pltpu.SemaphoreType.DMA((7,)),
                        pltpu.SemaphoreType.DMA((7,))],
    )(v)


def _columns(cdim, ao):
    q_col = 2 * cdim
    attn_dim = N_GROUPS * ao
    return (q_col, q_col + attn_dim, q_col + 2 * attn_dim), q_col + 3 * attn_dim


def _layer_fwd(x, sm, bg, get_rest, bias, hpg):
    cdim = sm["conv_ln_g"].shape[0]
    ao = hpg * HEAD_DIM
    cols, gate_col = _columns(cdim, ao)
    qg2 = jnp.tile(sm["q_norm_g"], 2).reshape(1, LANES)
    kg2 = jnp.tile(sm["k_norm_g"], 2).reshape(1, LANES)
    h1 = _rms_fwd(x, sm["norm1_g"], "rms1_fwd")
    u = _mm(h1, bg["w_in"], name="mm_in")
    zc = _conv_fwd(u, bg["conv_dw_w"], sm["conv_dw_b"], cdim, "conv_fwd")
    zs = _ln_swish_fwd(zc, sm["conv_ln_g"], sm["conv_ln_b"], "ln_swish_fwd")
    os_, lses = [], []
    for gi in range(N_GROUPS):
        o_g, lse_g = _attn_fwd(u, qg2, kg2, bias, gi, cols, hpg, "attn_fwd_g%d" % gi)
        os_.append(o_g)
        lses.append(lse_g)
    o = _combine_fwd(os_, lses, "combine_fwd")
    bg = {**bg, **get_rest(o)}
    yc = _mm(zs, bg["w_conv_out"], name="mm_conv_out")
    ya = _mm(o, bg["w_attn_out"], name="mm_attn_out")
    mg = _gate_fwd(u, yc, ya, gate_col, "gate_fwd")
    x1 = _mm(mg, bg["w_out"], epi="res", extra=x, name="mm_out")
    h2 = _rms_fwd(x1, sm["norm2_g"], "rms2_fwd")
    f, act = _mm(h2, bg["w_ff1"], epi="relu2", name="mm_ff1")
    x2 = _mm(act, bg["w_ff2"], epi="res", extra=x1, name="mm_ff2")
    saved = dict(x=x, h1=h1, u=u, zc=zc, zs=zs, yc=yc, os=os_, lses=lses, o=o, ya=ya, mg=mg, x1=x1, h2=h2, f=f,
                 act=act, qg2=qg2, kg2=kg2)
    return x2, saved, bg


GRAD_GROUPS = (("w_ff2", "w_ff1"), ("w_out", "w_conv_out", "w_attn_out", "conv_dw_w"), ("w_in",))


def _layer_bwd(dx, s, sm, bg, bias, ds_sum, after, emit):
    cdim = sm["conv_ln_g"].shape[0]
    ao = bg["w_attn_out"].shape[0]
    hpg = ao // HEAD_DIM
    hp = hpg // 2
    cols, gate_col = _columns(cdim, ao)
    g = {}
    df = _mm(dx, bg["w_ff2"], tb=True, epi="drelu2", extra=s["f"], out_dtype=BF16, after=after, name="mm_dff2")
    g["w_ff2"] = _mm(s["act"], dx, ta=True, out_dtype=BF16, name="mm_gw_ff2")
    g["w_ff1"] = _mm(s["h2"], df, ta=True, out_dtype=BF16, out_slots=True, name="mm_gw_ff1")
    after = emit(GRAD_GROUPS[0], g)
    dh2 = _mm(df, bg["w_ff1"], tb=True, after=after, name="mm_dff1")
    dx1, dg2 = _rms_bwd(dh2, s["x1"], sm["norm2_g"], dx, "rms2_bwd")
    g["norm2_g"] = dg2[0]
    dmg = _mm(dx1, bg["w_out"], tb=True, name="mm_dout")
    g["w_out"] = _mm(s["mg"], dx1, ta=True, out_dtype=BF16, name="mm_gw_out")
    dyc, dya, dugc, duga = _gate_bwd(dmg, s["u"], s["yc"], s["ya"], gate_col, "gate_bwd")
    dzs = _mm(dyc, bg["w_conv_out"], tb=True, name="mm_dconv_out")
    g["w_conv_out"] = _mm(s["zs"], dyc, ta=True, out_dtype=BF16, name="mm_gw_conv_out")
    do = _mm(dya, bg["w_attn_out"], tb=True, name="mm_dattn_out")
    g["w_attn_out"] = _mm(s["o"], dya, ta=True, out_dtype=BF16, name="mm_gw_attn_out")
    dzc, dlg, dlb = _ln_swish_bwd(dzs, s["zc"], sm["conv_ln_g"], sm["conv_ln_b"], "ln_swish_bwd")
    g["conv_ln_g"] = dlg[0]
    g["conv_ln_b"] = dlb[0]
    da, dgt, dcw, dcb = _conv_bwd(dzc, s["u"], bg["conv_dw_w"], cdim, "conv_bwd")
    g["conv_dw_w"] = dcw[:CONV_WIDTH].astype(BF16)
    g["conv_dw_b"] = dcb[0]
    after = emit(GRAD_GROUPS[1], g)
    do_gs, dd_gs = _combine_bwd(do, s["os"], s["lses"], "combine_bwd")
    dqs, dks, dvs = [], [], []
    dgq = jnp.zeros((HEAD_DIM,), F32)
    dgk = jnp.zeros((HEAD_DIM,), F32)
    for gi in range(N_GROUPS):
        dq, dk, dv, gq, gk, ds_sum = _attn_bwd(s["u"], do_gs[gi], dd_gs[gi], s["lses"][gi], s["qg2"], s["kg2"],
                                               bias, ds_sum, gi, cols, hpg, "attn_bwd_g%d" % gi)
        dqs.append(dq)
        dks.append(dk)
        dvs.append(dv)
        dgq = dgq + jnp.sum(gq.reshape(hp * 2, HEAD_DIM), axis=0)
        dgk = dgk + jnp.sum(gk.reshape(hp * 2, HEAD_DIM), axis=0)
    g["q_norm_g"] = dgq
    g["k_norm_g"] = dgk
    du = jnp.concatenate([da, dgt] + dqs + dks + dvs + [dugc, duga], axis=1)
    g["w_in"] = _mm(s["h1"], du, ta=True, out_dtype=BF16, after=after, name="mm_gw_in")
    after = emit(GRAD_GROUPS[2], g)
    dh1 = _mm(du, bg["w_in"], tb=True, after=after, name="mm_din")
    dx0, dg1 = _rms_bwd(dh1, s["x"], sm["norm1_g"], dx1, "rms1_bwd")
    g["norm1_g"] = dg1[0]
    return dx0, g, ds_sum


BIG = ("w_in", "conv_dw_w", "w_conv_out", "w_attn_out", "w_out", "w_ff1", "w_ff2")
COL_SHARDED = ("w_in", "conv_dw_w", "w_conv_out", "w_attn_out", "w_ff1")
SMALL = ("rel_bias", "norm1_g", "q_norm_g", "k_norm_g", "conv_dw_b", "conv_ln_g", "conv_ln_b", "norm2_g")
WEIGHTS = ("rel_bias", "norm1_g", "w_in", "q_norm_g", "k_norm_g", "conv_dw_w", "conv_dw_b", "conv_ln_g", "conv_ln_b",
           "w_conv_out", "w_attn_out", "w_out", "norm2_g", "w_ff1", "w_ff2")


def _to_whole(name, gathered):
    n, a, b = gathered.shape
    if name in COL_SHARDED:
        return gathered.transpose(1, 0, 2).reshape(a, n * b)
    return gathered.reshape(n * a, b)


def _to_slots(name, whole):
    a, b = whole.shape
    if name in COL_SHARDED:
        return whole.reshape(a, N_DEV, b // N_DEV).transpose(1, 0, 2)
    return whole.reshape(N_DEV, a // N_DEV, b)


def _own_slot(block, me):
    land = lax.empty((N_DEV,) + block.shape, block.dtype)
    return lax.dynamic_update_slice(land, block[None], (me,) + (0,) * block.ndim)


def kernel(x, rel_bias, norm1_g, w_in, q_norm_g, k_norm_g, conv_dw_w, conv_dw_b, conv_ln_g, conv_ln_b, w_conv_out, w_attn_out, w_out, norm2_g, w_ff1, w_ff2, loss_target, m_rel_bias, m_norm1_g, m_w_in, m_q_norm_g, m_k_norm_g, m_conv_dw_w, m_conv_dw_b, m_conv_ln_g, m_conv_ln_b, m_w_conv_out, m_w_attn_out, m_w_out, m_norm2_g, m_w_ff1, m_w_ff2, v_rel_bias, v_norm1_g, v_w_in, v_q_norm_g, v_k_norm_g, v_conv_dw_w, v_conv_dw_b, v_conv_ln_g, v_conv_ln_b, v_w_conv_out, v_w_attn_out, v_w_out, v_norm2_g, v_w_ff1, v_w_ff2):
    w = dict(rel_bias=rel_bias, norm1_g=norm1_g, w_in=w_in, q_norm_g=q_norm_g, k_norm_g=k_norm_g, conv_dw_w=conv_dw_w,
             conv_dw_b=conv_dw_b, conv_ln_g=conv_ln_g, conv_ln_b=conv_ln_b, w_conv_out=w_conv_out,
             w_attn_out=w_attn_out, w_out=w_out, norm2_g=norm2_g, w_ff1=w_ff1, w_ff2=w_ff2)
    mom = dict(rel_bias=m_rel_bias, norm1_g=m_norm1_g, w_in=m_w_in, q_norm_g=m_q_norm_g, k_norm_g=m_k_norm_g,
               conv_dw_w=m_conv_dw_w, conv_dw_b=m_conv_dw_b, conv_ln_g=m_conv_ln_g, conv_ln_b=m_conv_ln_b,
               w_conv_out=m_w_conv_out, w_attn_out=m_w_attn_out, w_out=m_w_out, norm2_g=m_norm2_g, w_ff1=m_w_ff1,
               w_ff2=m_w_ff2)
    var = dict(rel_bias=v_rel_bias, norm1_g=v_norm1_g, w_in=v_w_in, q_norm_g=v_q_norm_g, k_norm_g=v_k_norm_g,
               conv_dw_w=v_conv_dw_w, conv_dw_b=v_conv_dw_b, conv_ln_g=v_conv_ln_g, conv_ln_b=v_conv_ln_b,
               w_conv_out=v_w_conv_out, w_attn_out=v_w_attn_out, w_out=v_w_out, norm2_g=v_norm2_g, w_ff1=v_w_ff1,
               w_ff2=v_w_ff2)

    depth = norm1_g.shape[0]
    me = 4 * lax.axis_index("x") + 2 * lax.axis_index("y") + lax.axis_index("c")
    hpg = w_attn_out.shape[1] // HEAD_DIM
    buckets = jnp.asarray(_bucket_table())
    bias = _bias_expand(rel_bias, buckets, hpg, "bias_expand")

    first_names = ("w_in", "conv_dw_w")
    rest_names = tuple(k for k in BIG if k not in first_names)

    def chain_start(l, names, after):
        shards = [w[k][l] if k == "conv_dw_w" else w[k][l].astype(BF16) for k in names]
        sems, bufs, token = _gather_start(shards, [_own_slot(s, me) for s in shards], after,
                                          "gather_start_%s_l%d" % (names[0], l))
        return dict(l=l, names=names, sems=sems, bufs=bufs, token=token)

    def chain_forward(ch, after):
        fwd, bufs, token = _gather_forward(len(ch["names"]), ch["bufs"], ch["sems"][3], after,
                                           "gather_forward_%s_l%d" % (ch["names"][0], ch["l"]))
        ch.update(fwd=fwd, bufs=bufs)
        return token

    def chain_finish(ch, after):
        d2d_s, d2d_r, ici_s, _ = ch["sems"]
        gathered, _ = _gather_finish(len(ch["names"]), ch["bufs"], d2d_s, d2d_r, ici_s, ch["fwd"][0], ch["fwd"][1],
                                     after, "gather_finish_%s_l%d" % (ch["names"][0], ch["l"]))
        out = {k: _to_whole(k, a) for k, a in zip(ch["names"], gathered)}
        if "conv_dw_w" in out:
            out["conv_dw_w"] = jnp.pad(out["conv_dw_w"], ((0, CONV_TAPS_PADDED - CONV_WIDTH), (0, 0)))
        return out

    xs = x[0]
    saved, bigs, smalls = [], [], []
    chains = {}
    for l in range(depth):
        sm = {k: w[k][l] for k in SMALL if k != "rel_bias"}
        if l == 0:
            first = chain_start(0, first_names, None)
            token = chain_forward(first, None)
            rest = chain_start(0, rest_names, token)
            bg = chain_finish(first, rest["token"])

            def get_rest(o, rest=rest):
                token = chain_forward(rest, o)
                if depth > 1:
                    chains[1] = chain_start(1, BIG, token)
                    token = chains[1]["token"]
                return chain_finish(rest, token)
        else:
            token = chain_forward(chains[l], xs)
            if l + 1 < depth:
                chains[l + 1] = chain_start(l + 1, BIG, token)
                token = chains[l + 1]["token"]
            whole = chain_finish(chains[l], token)
            bg = {k: whole[k] for k in first_names}

            def get_rest(o, whole=whole):
                return {k: whole[k] for k in rest_names}
        xs, sv, bg = _layer_fwd(xs, sm, bg, get_rest, bias, hpg)
        saved.append(sv)
        bigs.append(bg)
        smalls.append(sm)

    loss, dx = _loss_and_grad(xs, loss_target[0], "loss")

    ds_sum = jnp.zeros((N_GROUPS * hpg, BLOCK, 2 * BLOCK), F32)
    g = {k: [None] * depth for k in SMALL if k != "rel_bias"}
    grad_layers = {k: [None] * depth for k in BIG}
    pending = []

    def finish_oldest(after):
        names, l, (send, recv), bufs = pending.pop(0)
        parts, recvd, token = _exchange_finish(len(names), bufs, send, recv, after,
                                               "exchange_finish_%s_l%d" % (names[0], l))
        for k, r, p in zip(names, recvd, parts):
            shp = w[k].shape[1:]
            three = (N_DEV, -1, shp[-1])
            grad_layers[k][l] = _sum_slots(r.reshape(three), p.reshape(three), me, "sum_" + k).reshape(shp)
        return token

    def make_emit(l):
        def emit(names, gl):
            parts = [gl[k] if k == "w_ff1" else _to_slots(k, gl[k]) for k in names]
            token = finish_oldest(parts[0]) if len(pending) >= len(GRAD_GROUPS) else None
            lands = [lax.empty(p.shape, p.dtype) for p in parts]
            sems, bufs, token = _exchange_start(parts, lands, token, "exchange_start_%s_l%d" % (names[0], l))
            pending.append((names, l, sems, bufs))
            return token
        return emit

    token = None
    for l in reversed(range(depth)):
        dx, gl, ds_sum = _layer_bwd(dx, saved[l], smalls[l], bigs[l], bias, ds_sum, token, make_emit(l))
        for k in g:
            g[k][l] = gl[k]
        token = None
    while pending:
        finish_oldest(None)
    grad_x = dx

    g = {k: jnp.stack(v) for k, v in g.items()}
    db = _bias_reduce(ds_sum, buckets, hpg, "bias_reduce")
    g["rel_bias"] = db[:, 0, :NUM_BUCKETS].T

    flat = jnp.concatenate([g[k].reshape(-1) for k in SMALL])
    nflat = flat.shape[0]
    rows = -(-nflat // (8 * LANES)) * 8
    packed = jnp.pad(flat, (0, rows * LANES - nflat)).reshape(rows, LANES)
    total = _all_reduce_small(packed, "reduce_small").reshape(-1)
    grad = {}
    off = 0
    for k in SMALL:
        size = int(np.prod(w[k].shape))
        grad[k] = total[off:off + size].reshape(w[k].shape)
        off += size

    for k in BIG:
        grad[k] = jnp.stack(grad_layers[k])

    loss = lax.psum(loss[0, 0], ("x", "y", "c"))
    outs = {k: _adamw(w[k], grad[k], mom[k], var[k], "adamw_" + k) for k in WEIGHTS}
    return (loss, grad_x[None], *[grad[k] for k in WEIGHTS], *[outs[k][0] for k in WEIGHTS],
            *[outs[k][1] for k in WEIGHTS], *[outs[k][2] for k in WEIGHTS])
```

```python
import functools
import math

import numpy as np
import jax
import jax.numpy as jnp
from jax import lax
from jax.experimental import pallas as pl
from jax.experimental.pallas import tpu as pltpu

F32 = jnp.float32
BF16 = jnp.bfloat16

HEAD_DIM = 64
N_GROUPS = 3
DILATIONS = (1, 4, 16)
SUB_WINDOW = 128
BLOCK = 128
CONV_WIDTH = 31
CONV_TAPS_PADDED = 32
NUM_BUCKETS = 32
MAX_REL_DISTANCE = 2048
EPS = 1e-6
NEG_INF = -1e30
LANES = 128

ADAM_LR = 0.001
ADAM_B1 = 0.9
ADAM_B2 = 0.999
ADAM_EPS = 1e-08
ADAM_WD = 0.01
ADAM_STEP = 10

N_DEV = 8
VMEM_LIMIT = 56 * 1024 * 1024
MESH = pl.DeviceIdType.MESH


def _cparams(sem=None):
    return pltpu.CompilerParams(dimension_semantics=sem, vmem_limit_bytes=VMEM_LIMIT)


def _tile(n, target):
    if n <= target:
        return n
    t = (target // LANES) * LANES
    while t >= LANES:
        if n % t == 0:
            return t
        t -= LANES
    return n


def _sigmoid(v):
    return 1.0 / (1.0 + jnp.exp(-v))


MM_VMEM_BUDGET = 40 * 1024 * 1024


def _mm_tiles(m, n, kdim, a_bytes, b_bytes, io_bytes):
    tm, tn, tk = _tile(m, 1024), _tile(n, 1024), _tile(kdim, 2048)

    def need(tm, tn, tk):
        blocks = 2 * (tm * tk * a_bytes + tk * tn * b_bytes + tm * tn * io_bytes)
        casts = (tm * tk * 2 if a_bytes == 4 else 0) + (tk * tn * 2 if b_bytes == 4 else 0)
        return blocks + casts + 2 * tm * tn * 4

    while need(tm, tn, tk) > MM_VMEM_BUDGET:
        if tk >= 1024 and tk % 256 == 0:
            tk //= 2
        elif tn >= 512 and tn % 256 == 0:
            tn //= 2
        else:
            tm //= 2
    return tm, tn, tk


def _mm(a, b, *, ta=False, tb=False, out_dtype=F32, epi=None, extra=None, after=None, out_slots=False, name):
    m = a.shape[1] if ta else a.shape[0]
    kdim = a.shape[0] if ta else a.shape[1]
    n = b.shape[0] if tb else b.shape[1]
    io_bytes = (6 if epi == "relu2" else jnp.dtype(out_dtype).itemsize) + (4 if extra is not None else 0)
    tm, tn, tk = _mm_tiles(m, n // N_DEV if out_slots else n, kdim, a.dtype.itemsize, b.dtype.itemsize, io_bytes)
    if out_slots:
        assert epi is None and tn == n // N_DEV
    nk = kdim // tk
    a_spec = pl.BlockSpec((tk, tm), lambda i, j, k: (k, i)) if ta else pl.BlockSpec((tm, tk), lambda i, j, k: (i, k))
    b_spec = pl.BlockSpec((tn, tk), lambda i, j, k: (j, k)) if tb else pl.BlockSpec((tk, tn), lambda i, j, k: (k, j))
    o_spec = (pl.BlockSpec((None, tm, tn), lambda i, j, k: (j, i, 0)) if out_slots
              else pl.BlockSpec((tm, tn), lambda i, j, k: (i, j)))
    dims = (((0 if ta else 1,), (1 if tb else 0,)), ((), ()))
    n_in = 2 + (extra is not None) + (after is not None)
    n_out = 2 if epi == "relu2" else 1

    def body(*refs):
        a_ref, b_ref = refs[0], refs[1]
        e_ref = refs[2] if extra is not None else None
        outs = refs[n_in:n_in + n_out]

        def product():
            return lax.dot_general(a_ref[...].astype(BF16), b_ref[...].astype(BF16), dims, preferred_element_type=F32)

        def finish(acc):
            if epi is None:
                outs[0][...] = acc.astype(outs[0].dtype)
            elif epi == "res":
                outs[0][...] = (e_ref[...] + acc).astype(outs[0].dtype)
            elif epi == "relu2":
                outs[0][...] = acc
                r = jnp.maximum(acc, 0.0)
                outs[1][...] = (r * r).astype(BF16)
            elif epi == "drelu2":
                outs[0][...] = (acc * (2.0 * jnp.maximum(e_ref[...], 0.0))).astype(outs[0].dtype)

        if nk == 1:
            finish(product())
            return
        acc_ref = refs[-1]
        k = pl.program_id(2)

        @pl.when(k == 0)
        def _():
            acc_ref[...] = product()

        @pl.when(jnp.logical_and(k > 0, k < nk - 1))
        def _():
            acc_ref[...] += product()

        @pl.when(k == nk - 1)
        def _():
            finish(acc_ref[...] + product())

    in_specs = ([a_spec, b_spec] + ([o_spec] if extra is not None else [])
                + ([pl.BlockSpec(memory_space=pl.ANY)] if after is not None else []))
    if epi == "relu2":
        out_shape = (jax.ShapeDtypeStruct((m, n), F32), jax.ShapeDtypeStruct((m, n), BF16))
        out_specs = (o_spec, o_spec)
    else:
        out_shape = jax.ShapeDtypeStruct((N_DEV, m, tn) if out_slots else (m, n), out_dtype)
        out_specs = o_spec
    args = (a, b) + ((extra,) if extra is not None else ()) + ((after,) if after is not None else ())
    return pl.pallas_call(
        body, name=name, grid=(m // tm, n // tn, nk), in_specs=in_specs, out_specs=out_specs, out_shape=out_shape,
        scratch_shapes=[pltpu.VMEM((tm, tn), F32)] if nk > 1 else [],
        compiler_params=_cparams(("parallel", "parallel", "arbitrary")),
    )(*args)


SHIFT = HEAD_DIM


def _pair_blocks(e, o):
    wp = e.shape[-1]
    return e[:, :wp - LANES], e[:, wp - LANES:] + o[:, :LANES], o[:, LANES:]


def _mm_in_pairs(a, wg, name):
    t, kdim = a.shape
    wp = wg.shape[-1]
    ws = wp - SHIFT
    tm = _tile(t, 1024)

    def body(a_ref, e_ref, o_ref, u_ref):
        av = a_ref[...]
        lo, mid, hi = _pair_blocks(e_ref[...], o_ref[...])
        u_ref[:, :wp - LANES] = jnp.dot(av, lo, preferred_element_type=F32)
        u_ref[:, wp - LANES:wp] = jnp.dot(av, mid, preferred_element_type=F32)
        u_ref[:, wp:] = jnp.dot(av, hi, preferred_element_type=F32)

    return pl.pallas_call(
        body, name=name, grid=(N_DEV // 2, t // tm),
        in_specs=[pl.BlockSpec((tm, kdim), lambda p, i: (i, 0)),
                  pl.BlockSpec((None, kdim, wp), lambda p, i: (2 * p, 0, 0)),
                  pl.BlockSpec((None, kdim, wp), lambda p, i: (2 * p + 1, 0, 0))],
        out_specs=pl.BlockSpec((tm, 2 * ws), lambda p, i: (i, p)),
        out_shape=jax.ShapeDtypeStruct((t, N_DEV * ws), F32), compiler_params=_cparams(("arbitrary", "arbitrary")),
    )(a, wg, wg)


def _mm_din_pairs(du, wg, after, name):
    t = du.shape[0]
    _, kdim, wp = wg.shape
    ws = wp - SHIFT
    tm = _tile(t, 1024)
    npair = N_DEV // 2
    lanes = (((1,), (1,)), ((), ()))
    extra = [] if after is None else [after]

    def body(d_ref, e_ref, o_ref, *rest):
        out_ref, acc_ref = rest[-2], rest[-1]
        p = pl.program_id(1)
        lo, mid, hi = _pair_blocks(e_ref[...], o_ref[...])
        part = (lax.dot_general(d_ref[:, :wp - LANES], lo, lanes, preferred_element_type=F32)
                + lax.dot_general(d_ref[:, wp - LANES:wp], mid, lanes, preferred_element_type=F32)
                + lax.dot_general(d_ref[:, wp:], hi, lanes, preferred_element_type=F32))

        @pl.when(p == 0)
        def _():
            acc_ref[...] = part

        @pl.when(jnp.logical_and(p > 0, p < npair - 1))
        def _():
            acc_ref[...] += part

        @pl.when(p == npair - 1)
        def _():
            out_ref[...] = acc_ref[...] + part

    return pl.pallas_call(
        body, name=name, grid=(t // tm, npair),
        in_specs=[pl.BlockSpec((tm, 2 * ws), lambda i, p: (i, p)),
                  pl.BlockSpec((None, kdim, wp), lambda i, p: (2 * p, 0, 0)),
                  pl.BlockSpec((None, kdim, wp), lambda i, p: (2 * p + 1, 0, 0))]
        + [pl.BlockSpec(memory_space=pl.ANY)] * len(extra),
        out_specs=pl.BlockSpec((tm, kdim), lambda i, p: (i, 0)),
        out_shape=jax.ShapeDtypeStruct((t, kdim), F32), scratch_shapes=[pltpu.VMEM((tm, kdim), F32)],
        compiler_params=_cparams(("arbitrary", "arbitrary")),
    )(du, wg, wg, *extra)


def _mm_gw_in_pairs(h, du, after, name):
    t, kdim = h.shape
    ws = du.shape[1] // N_DEV
    wp = ws + SHIFT
    tm = _tile(kdim, 512)
    rows = (((0,), (0,)), ((), ()))
    extra = [] if after is None else [after]

    def body(h_ref, d_ref, *rest):
        g_ref = rest[-1]
        g = lax.dot_general(h_ref[...], d_ref[...], rows, preferred_element_type=F32)
        g_ref[0] = g[:, :wp].astype(BF16)
        g_ref[1] = g[:, wp - LANES:].astype(BF16)

    return pl.pallas_call(
        body, name=name, grid=(N_DEV // 2, kdim // tm),
        in_specs=[pl.BlockSpec((t, tm), lambda p, i: (0, i)), pl.BlockSpec((t, 2 * ws), lambda p, i: (0, p))]
        + [pl.BlockSpec(memory_space=pl.ANY)] * len(extra),
        out_specs=pl.BlockSpec((2, tm, wp), lambda p, i: (p, i, 0)),
        out_shape=jax.ShapeDtypeStruct((N_DEV, kdim, wp), BF16), compiler_params=_cparams(("arbitrary", "arbitrary")),
    )(h, du, *extra)


ROW_BLOCK_BUDGET = 24 * 1024 * 1024


def _rows(t, row_bytes):
    rows = t
    while rows > 8 and (2 * rows * row_bytes > ROW_BLOCK_BUDGET or t % rows):
        rows //= 2
    return rows


def _rms_fwd(x, g, name):
    t, d = x.shape
    ROWS = _rows(t, 6 * d)

    def body(x_ref, g_ref, h_ref):
        xv = x_ref[...]
        r = lax.rsqrt(jnp.mean(xv * xv, axis=-1, keepdims=True) + EPS)
        h_ref[...] = (xv * r * g_ref[...]).astype(BF16)

    return pl.pallas_call(
        body, name=name, grid=(t // ROWS,),
        in_specs=[pl.BlockSpec((ROWS, d), lambda i: (i, 0)), pl.BlockSpec((1, d), lambda i: (0, 0))],
        out_specs=pl.BlockSpec((ROWS, d), lambda i: (i, 0)),
        out_shape=jax.ShapeDtypeStruct((t, d), BF16), compiler_params=_cparams(("arbitrary",)),
    )(x, g.reshape(1, d))


def _rms_bwd(dh, x, g, dres, name):
    t, d = x.shape
    ROWS = _rows(t, 16 * d)

    def body(dh_ref, x_ref, g_ref, dres_ref, dx_ref, dg_ref):
        @pl.when(pl.program_id(0) == 0)
        def _():
            dg_ref[...] = jnp.zeros_like(dg_ref)

        xv = x_ref[...]
        dhv = dh_ref[...]
        r = lax.rsqrt(jnp.mean(xv * xv, axis=-1, keepdims=True) + EPS)
        xh = xv * r
        dg_ref[...] += jnp.sum(dhv * xh, axis=0, keepdims=True)
        dxh = dhv * g_ref[...]
        dx_ref[...] = dres_ref[...] + r * (dxh - xh * jnp.mean(dxh * xh, axis=-1, keepdims=True))

    row = pl.BlockSpec((ROWS, d), lambda i: (i, 0))
    vec = pl.BlockSpec((1, d), lambda i: (0, 0))
    return pl.pallas_call(
        body, name=name, grid=(t // ROWS,), in_specs=[row, row, vec, row], out_specs=(row, vec),
        out_shape=(jax.ShapeDtypeStruct((t, d), F32), jax.ShapeDtypeStruct((1, d), F32)),
        compiler_params=_cparams(("arbitrary",)),
    )(dh, x, g.reshape(1, d), dres)


def _gate_fwd(u, yc, ya, gate_col, name):
    t, d = yc.shape
    td = math.gcd(_tile(d, 512), gate_col)
    nd = d // td
    c0 = gate_col // td
    ROWS = _rows(t, 18 * td)

    def body(gc_ref, ga_ref, yc_ref, ya_ref, m_ref):
        m_ref[...] = (_sigmoid(gc_ref[...]) * yc_ref[...] + _sigmoid(ga_ref[...]) * ya_ref[...]).astype(BF16)

    blk = pl.BlockSpec((ROWS, td), lambda i, j: (i, j))
    return pl.pallas_call(
        body, name=name, grid=(t // ROWS, nd),
        in_specs=[pl.BlockSpec((ROWS, td), lambda i, j: (i, c0 + j)),
                  pl.BlockSpec((ROWS, td), lambda i, j: (i, c0 + nd + j)), blk, blk],
        out_specs=blk, out_shape=jax.ShapeDtypeStruct((t, d), BF16),
        compiler_params=_cparams(("arbitrary", "arbitrary")),
    )(u, u, yc, ya)


def _gate_bwd(dm, u, yc, ya, gate_col, name):
    t, d = yc.shape
    td = math.gcd(_tile(d, 512), gate_col)
    nd = d // td
    c0 = gate_col // td
    ROWS = _rows(t, 28 * td)

    def body(dm_ref, gc_ref, ga_ref, yc_ref, ya_ref, dyc_ref, dya_ref, dugc_ref, duga_ref):
        dmv = dm_ref[...]
        gc = _sigmoid(gc_ref[...])
        ga = _sigmoid(ga_ref[...])
        dyc_ref[...] = (dmv * gc).astype(BF16)
        dya_ref[...] = (dmv * ga).astype(BF16)
        dugc_ref[...] = (dmv * yc_ref[...] * gc * (1.0 - gc)).astype(BF16)
        duga_ref[...] = (dmv * ya_ref[...] * ga * (1.0 - ga)).astype(BF16)

    blk = pl.BlockSpec((ROWS, td), lambda i, j: (i, j))
    o = jax.ShapeDtypeStruct((t, d), BF16)
    return pl.pallas_call(
        body, name=name, grid=(t // ROWS, nd),
        in_specs=[blk, pl.BlockSpec((ROWS, td), lambda i, j: (i, c0 + j)),
                  pl.BlockSpec((ROWS, td), lambda i, j: (i, c0 + nd + j)), blk, blk],
        out_specs=(blk, blk, blk, blk), out_shape=(o, o, o, o),
        compiler_params=_cparams(("arbitrary", "arbitrary")),
    )(dm, u, u, yc, ya)


def _loss_and_grad(y, target, name):
    t, d = y.shape
    ROWS = _rows(t, 12 * d)
    n = t // ROWS

    def body(y_ref, t_ref, loss_ref, dy_ref, acc_ref):
        i = pl.program_id(0)

        @pl.when(i == 0)
        def _():
            acc_ref[...] = jnp.zeros_like(acc_ref)

        diff = y_ref[...] - t_ref[...]
        dy_ref[...] = diff * (1.0 / d)
        acc_ref[...] += jnp.sum(diff * diff, axis=0, keepdims=True)

        @pl.when(i == n - 1)
        def _():
            loss_ref[...] = jnp.sum(acc_ref[...], axis=-1, keepdims=True) * (0.5 / d)

    row = pl.BlockSpec((ROWS, d), lambda i: (i, 0))
    return pl.pallas_call(
        body, name=name, grid=(n,), in_specs=[row, row],
        out_specs=(pl.BlockSpec((1, 1), lambda i: (0, 0)), row),
        out_shape=(jax.ShapeDtypeStruct((1, 1), F32), jax.ShapeDtypeStruct((t, d), F32)),
        scratch_shapes=[pltpu.VMEM((1, d), F32)], compiler_params=_cparams(("arbitrary",)),
    )(y, target)


HALO = 32


def _conv_fwd(u, w, b, cdim, name):
    t = u.shape[0]
    ncb = cdim // LANES
    nt = t // BLOCK

    def body(a_ref, g_ref, w_ref, b_ref, zc_ref, zpad):
        zpad[0:HALO, :] = jnp.zeros((HALO, LANES), F32)
        zpad[HALO:HALO + t, :] = a_ref[...] * _sigmoid(g_ref[...])
        wv = w_ref[...]
        bv = b_ref[...]

        def tile(i, carry):
            r0 = pl.multiple_of(i * BLOCK, BLOCK)
            win = zpad[pl.ds(r0, BLOCK + HALO), :]
            acc = jnp.zeros((BLOCK, LANES), F32) + bv
            for j in range(CONV_WIDTH):
                s = CONV_WIDTH - 1 - j
                sh = win if s == 0 else pltpu.roll(win, s, 0)
                acc = acc + wv[j:j + 1, :] * sh[HALO:HALO + BLOCK, :]
            zc_ref[pl.ds(r0, BLOCK), :] = acc
            return carry

        lax.fori_loop(0, nt, tile, 0)

    col = lambda off: pl.BlockSpec((t, LANES), lambda c: (0, off + c))
    return pl.pallas_call(
        body, name=name, grid=(ncb,),
        in_specs=[col(0), col(ncb), pl.BlockSpec((CONV_TAPS_PADDED, LANES), lambda c: (0, c)),
                  pl.BlockSpec((1, LANES), lambda c: (0, c))],
        out_specs=pl.BlockSpec((t, LANES), lambda c: (0, c)),
        out_shape=jax.ShapeDtypeStruct((t, cdim), F32),
        scratch_shapes=[pltpu.VMEM((t + HALO, LANES), F32)], compiler_params=_cparams(("arbitrary",)),
    )(u, u, w, b.reshape(1, cdim))


def _conv_bwd(dzc, u, w, cdim, name):
    t = u.shape[0]
    ncb = cdim // LANES
    nt = t // BLOCK
    win_rows = BLOCK + HALO

    def body(dzc_ref, a_ref, g_ref, w_ref, da_ref, dg_ref, dw_ref, db_ref, zpad, dpad):
        av = a_ref[...]
        sg = _sigmoid(g_ref[...])
        zpad[0:HALO, :] = jnp.zeros((HALO, LANES), F32)
        zpad[HALO:HALO + t, :] = av * sg
        dpad[0:t, :] = dzc_ref[...]
        dpad[t:t + HALO, :] = jnp.zeros((HALO, LANES), F32)
        dw_ref[...] = jnp.zeros_like(dw_ref)
        db_ref[...] = jnp.sum(dzc_ref[...], axis=0, keepdims=True)
        wv = w_ref[...]

        def tile(i, carry):
            r0 = pl.multiple_of(i * BLOCK, BLOCK)
            zwin = zpad[pl.ds(r0, win_rows), :]
            dwin = dpad[pl.ds(r0, win_rows), :]
            dcur = dwin[0:BLOCK, :]
            dz = jnp.zeros((BLOCK, LANES), F32)
            for j in range(CONV_WIDTH):
                s = CONV_WIDTH - 1 - j
                zs = zwin if s == 0 else pltpu.roll(zwin, s, 0)
                dw_ref[j:j + 1, :] += jnp.sum(dcur * zs[HALO:HALO + BLOCK, :], axis=0, keepdims=True)
                ds = dwin if s == 0 else pltpu.roll(dwin, win_rows - s, 0)
                dz = dz + wv[j:j + 1, :] * ds[0:BLOCK, :]
            ac = a_ref[pl.ds(r0, BLOCK), :]
            sc = _sigmoid(g_ref[pl.ds(r0, BLOCK), :])
            da_ref[pl.ds(r0, BLOCK), :] = (dz * sc).astype(BF16)
            dg_ref[pl.ds(r0, BLOCK), :] = (dz * ac * sc * (1.0 - sc)).astype(BF16)
            return carry

        lax.fori_loop(0, nt, tile, 0)

    col = lambda off: pl.BlockSpec((t, LANES), lambda c: (0, off + c))
    wspec = pl.BlockSpec((CONV_TAPS_PADDED, LANES), lambda c: (0, c))
    o = jax.ShapeDtypeStruct((t, cdim), BF16)
    return pl.pallas_call(
        body, name=name, grid=(ncb,), in_specs=[col(0), col(0), col(ncb), wspec],
        out_specs=(col(0), col(0), wspec, pl.BlockSpec((1, LANES), lambda c: (0, c))),
        out_shape=(o, o, jax.ShapeDtypeStruct((CONV_TAPS_PADDED, cdim), F32), jax.ShapeDtypeStruct((1, cdim), F32)),
        scratch_shapes=[pltpu.VMEM((t + HALO, LANES), F32), pltpu.VMEM((t + HALO, LANES), F32)],
        compiler_params=_cparams(("arbitrary",)),
    )(dzc, u, u, w)


def _ln_swish_fwd(zc, g, b, name):
    t, c = zc.shape
    ROWS = _rows(t, 6 * c)

    def body(z_ref, g_ref, b_ref, o_ref):
        z = z_ref[...]
        mu = jnp.mean(z, axis=-1, keepdims=True)
        zc_ = z - mu
        zn = zc_ * lax.rsqrt(jnp.mean(zc_ * zc_, axis=-1, keepdims=True) + EPS)
        y = zn * g_ref[...] + b_ref[...]
        o_ref[...] = (y * _sigmoid(y)).astype(BF16)

    row = pl.BlockSpec((ROWS, c), lambda i: (i, 0))
    vec = pl.BlockSpec((1, c), lambda i: (0, 0))
    return pl.pallas_call(
        body, name=name, grid=(t // ROWS,), in_specs=[row, vec, vec], out_specs=row,
        out_shape=jax.ShapeDtypeStruct((t, c), BF16), compiler_params=_cparams(("arbitrary",)),
    )(zc, g.reshape(1, c), b.reshape(1, c))


def _ln_swish_bwd(dzs, zc, g, b, name):
    t, c = zc.shape
    ROWS = _rows(t, 12 * c)

    def body(d_ref, z_ref, g_ref, b_ref, dz_ref, dg_ref, db_ref):
        @pl.when(pl.program_id(0) == 0)
        def _():
            dg_ref[...] = jnp.zeros_like(dg_ref)
            db_ref[...] = jnp.zeros_like(db_ref)

        z = z_ref[...]
        mu = jnp.mean(z, axis=-1, keepdims=True)
        zc_ = z - mu
        rstd = lax.rsqrt(jnp.mean(zc_ * zc_, axis=-1, keepdims=True) + EPS)
        zn = zc_ * rstd
        y = zn * g_ref[...] + b_ref[...]
        sg = _sigmoid(y)
        dy = d_ref[...] * (sg * (1.0 + y * (1.0 - sg)))
        dg_ref[...] += jnp.sum(dy * zn, axis=0, keepdims=True)
        db_ref[...] += jnp.sum(dy, axis=0, keepdims=True)
        dzn = dy * g_ref[...]
        dz_ref[...] = rstd * (dzn - jnp.mean(dzn, axis=-1, keepdims=True)
                              - zn * jnp.mean(dzn * zn, axis=-1, keepdims=True))

    row = pl.BlockSpec((ROWS, c), lambda i: (i, 0))
    vec = pl.BlockSpec((1, c), lambda i: (0, 0))
    v = jax.ShapeDtypeStruct((1, c), F32)
    return pl.pallas_call(
        body, name=name, grid=(t // ROWS,), in_specs=[row, row, vec, vec], out_specs=(row, vec, vec),
        out_shape=(jax.ShapeDtypeStruct((t, c), F32), v, v), compiler_params=_cparams(("arbitrary",)),
    )(dzs, zc, g.reshape(1, c), b.reshape(1, c))


def _bucket_table():
    qi = np.arange(BLOCK)[:, None]
    kj = np.arange(2 * BLOCK)[None, :]
    off = qi + BLOCK - kj
    band = (off >= 0) & (off <= SUB_WINDOW)
    max_exact = NUM_BUCKETS // 2
    out = []
    for d in DILATIONS:
        dist = (np.clip(off, 0, SUB_WINDOW) * d).astype(np.int32)
        nf = np.maximum(dist, 1).astype(np.float32)
        large = max_exact + (np.log(nf / np.float32(max_exact)) / np.float32(math.log(MAX_REL_DISTANCE / max_exact))
                             * np.float32(NUM_BUCKETS - max_exact)).astype(np.int32)
        large = np.minimum(large, NUM_BUCKETS - 1)
        bucket = np.where(dist < max_exact, dist, large)
        out.append(np.where(band, bucket, -1))
    return np.stack(out).astype(np.int32)


def _bias_expand(rel_bias, buckets, hpg, name):
    nh = N_GROUPS * hpg

    def body(rb_ref, bk_ref, o_ref):
        h = pl.program_id(0)
        bk = bk_ref[0]
        acc = jnp.full((BLOCK, 2 * BLOCK), NEG_INF, F32)
        for bb in range(NUM_BUCKETS):
            acc = jnp.where(bk == bb, rb_ref[bb, h], acc)
        o_ref[0] = acc

    return pl.pallas_call(
        body, name=name, grid=(nh,),
        in_specs=[pl.BlockSpec(memory_space=pltpu.SMEM),
                  pl.BlockSpec((1, BLOCK, 2 * BLOCK), lambda h: (h // hpg, 0, 0))],
        out_specs=pl.BlockSpec((1, BLOCK, 2 * BLOCK), lambda h: (h, 0, 0)),
        out_shape=jax.ShapeDtypeStruct((nh, BLOCK, 2 * BLOCK), F32), compiler_params=_cparams(("arbitrary",)),
    )(rel_bias, buckets)


def _bias_reduce(ds_sum, buckets, hpg, name):
    nh = N_GROUPS * hpg

    def body(ds_ref, bk_ref, o_ref):
        bk = bk_ref[0]
        dsv = ds_ref[0]
        lane = lax.broadcasted_iota(jnp.int32, (1, LANES), 1)
        row = jnp.zeros((1, LANES), F32)
        for bb in range(NUM_BUCKETS):
            tot = jnp.sum(jnp.sum(jnp.where(bk == bb, dsv, 0.0), axis=-1, keepdims=True), axis=0, keepdims=True)
            row = jnp.where(lane == bb, tot, row)
        o_ref[0] = row

    return pl.pallas_call(
        body, name=name, grid=(nh,),
        in_specs=[pl.BlockSpec((1, BLOCK, 2 * BLOCK), lambda h: (h, 0, 0)),
                  pl.BlockSpec((1, BLOCK, 2 * BLOCK), lambda h: (h // hpg, 0, 0))],
        out_specs=pl.BlockSpec((1, 1, LANES), lambda h: (h, 0, 0)),
        out_shape=jax.ShapeDtypeStruct((nh, 1, LANES), F32), compiler_params=_cparams(("arbitrary",)),
    )(ds_sum, buckets)


def _chunk_rows(c, d, nb):
    r, n = c // nb, c % nb
    if d == 1:
        return pl.ds(c * BLOCK, BLOCK)
    return pl.ds(r + n * BLOCK * d, BLOCK, stride=d)


def _segment_ones():
    i = lax.broadcasted_iota(jnp.int32, (LANES, LANES), 0) // HEAD_DIM
    j = lax.broadcasted_iota(jnp.int32, (LANES, LANES), 1) // HEAD_DIM
    return (i == j).astype(BF16)


def _segment_sum(v, seg):
    hi = v.astype(BF16)
    lo = (v - hi.astype(F32)).astype(BF16)
    return jnp.dot(hi, seg, preferred_element_type=F32) + jnp.dot(lo, seg, preferred_element_type=F32)


def _head_mean(v, seg):
    return _segment_sum(v, seg) * (1.0 / HEAD_DIM)


def _attn_fwd(u, qg2, kg2, bias, gi, cols, hpg, name):
    t = u.shape[0]
    d = DILATIONS[gi]
    nchunk = t // BLOCK
    nb = (t // d) // BLOCK
    hp = hpg // 2
    qc0, kc0, vc0 = [(c + gi * hpg * HEAD_DIM) // LANES for c in cols]
    contract_lanes = (((1,), (1,)), ((), ()))

    def body(q_ref, k_ref, v_ref, qg_ref, kg_ref, bias_ref, o_ref, lse_ref, qd, kd, vd, od, ld, sbuf):
        seg = _segment_ones()
        lane = lax.broadcasted_iota(jnp.int32, (1, LANES), 1)
        qg = qg_ref[...] * (HEAD_DIM ** -0.5)
        kg = kg_ref[...]
        kd[0:BLOCK, :] = jnp.zeros((BLOCK, LANES), BF16)
        vd[0:BLOCK, :] = jnp.zeros((BLOCK, LANES), BF16)
        for c in range(nchunk):
            rows = _chunk_rows(c, d, nb)
            qv = q_ref[rows, :]
            kv = k_ref[rows, :]
            qd[c * BLOCK:(c + 1) * BLOCK, :] = (qv * lax.rsqrt(_head_mean(qv * qv, seg) + EPS) * qg).astype(BF16)
            kd[(c + 1) * BLOCK:(c + 2) * BLOCK, :] = (kv * lax.rsqrt(_head_mean(kv * kv, seg) + EPS) * kg).astype(BF16)
            vd[(c + 1) * BLOCK:(c + 2) * BLOCK, :] = v_ref[rows, :].astype(BF16)

        col = lax.broadcasted_iota(jnp.int32, (BLOCK, 2 * BLOCK), 1)
        for j in range(2):
            mj = jnp.logical_and(lane >= j * HEAD_DIM, lane < (j + 1) * HEAD_DIM)
            for c in range(nchunk):
                kw = kd[c * BLOCK:(c + 2) * BLOCK, :]
                kj = jnp.where(mj, kw, jnp.zeros_like(kw))
                s = lax.dot_general(qd[c * BLOCK:(c + 1) * BLOCK, :], kj, contract_lanes,
                                    preferred_element_type=F32) + bias_ref[j]
                if c % nb == 0:
                    s = jnp.where(col < BLOCK, NEG_INF, s)
                sbuf[c] = s
            for c in range(nchunk):
                rows = slice(c * BLOCK, (c + 1) * BLOCK)
                s = sbuf[c]
                mx = jnp.max(s, axis=-1, keepdims=True)
                p = jnp.exp(s - mx).astype(BF16)
                vw = vd[c * BLOCK:(c + 2) * BLOCK, :]
                oj = jnp.dot(p, jnp.where(mj, vw, jnp.ones_like(vw)), preferred_element_type=F32)
                l = pltpu.roll(oj, HEAD_DIM, 1)
                on = oj / l
                ls = mx + jnp.log(l)
                if j == 0:
                    od[rows, :] = on
                    ld[rows, :] = ls
                else:
                    od[rows, :] = jnp.where(mj, on, od[rows, :])
                    ld[rows, :] = jnp.where(mj, ls, ld[rows, :])

        for c in range(nchunk):
            rows = _chunk_rows(c, d, nb)
            o_ref[rows, :] = od[c * BLOCK:(c + 1) * BLOCK, :]
            lse_ref[rows, :] = ld[c * BLOCK:(c + 1) * BLOCK, :]

    ucol = lambda c0: pl.BlockSpec((t, LANES), lambda h: (0, c0 + h))
    vec = pl.BlockSpec((1, LANES), lambda h: (0, 0))
    oblk = pl.BlockSpec((t, LANES), lambda h: (0, h))
    osh = jax.ShapeDtypeStruct((t, hpg * HEAD_DIM), F32)
    return pl.pallas_call(
        body, name=name, grid=(hp,),
        in_specs=[ucol(qc0), ucol(kc0), ucol(vc0), vec, vec,
                  pl.BlockSpec((2, BLOCK, 2 * BLOCK), lambda h: (gi * hp + h, 0, 0))],
        out_specs=(oblk, oblk), out_shape=(osh, osh),
        scratch_shapes=[pltpu.VMEM((t, LANES), BF16), pltpu.VMEM((t + BLOCK, LANES), BF16),
                        pltpu.VMEM((t + BLOCK, LANES), BF16), pltpu.VMEM((t, LANES), F32), pltpu.VMEM((t, LANES), F32),
                        pltpu.VMEM((nchunk, BLOCK, 2 * BLOCK), F32)],
        compiler_params=_cparams(("arbitrary",)),
    )(u, u, u, qg2, kg2, bias)


def _attn_bwd(u, do_g, dd_g, lse_g, qg2, kg2, bias, ds_in, gi, cols, hpg, name):
    t = u.shape[0]
    d = DILATIONS[gi]
    nchunk = t // BLOCK
    nb = (t // d) // BLOCK
    hp = hpg // 2
    qc0, kc0, vc0 = [(c + gi * hpg * HEAD_DIM) // LANES for c in cols]
    contract_lanes = (((1,), (1,)), ((), ()))
    contract_rows = (((0,), (0,)), ((), ()))
    qscale = HEAD_DIM ** -0.5

    def body(q_ref, k_ref, v_ref, do_ref, dd_ref, lse_ref, qg_ref, kg_ref, bias_ref, dsin_ref,
             dq_ref, dk_ref, dv_ref, dgq_ref, dgk_ref, dsout_ref,
             qd, kd, vd, dod, ddd, ld, dqd, dkd, dvd, dsacc, pbuf, dsbuf):
        seg = _segment_ones()
        lane = lax.broadcasted_iota(jnp.int32, (1, LANES), 1)
        qg = qg_ref[...] * qscale
        kg = kg_ref[...]
        kd[0:BLOCK, :] = jnp.zeros((BLOCK, LANES), BF16)
        vd[0:BLOCK, :] = jnp.zeros((BLOCK, LANES), BF16)
        dsacc[...] = jnp.zeros_like(dsacc)
        for c in range(nchunk):
            rows = _chunk_rows(c, d, nb)
            qv = q_ref[rows, :]
            kv = k_ref[rows, :]
            qd[c * BLOCK:(c + 1) * BLOCK, :] = (qv * lax.rsqrt(_head_mean(qv * qv, seg) + EPS) * qg).astype(BF16)
            kd[(c + 1) * BLOCK:(c + 2) * BLOCK, :] = (kv * lax.rsqrt(_head_mean(kv * kv, seg) + EPS) * kg).astype(BF16)
            vd[(c + 1) * BLOCK:(c + 2) * BLOCK, :] = v_ref[rows, :].astype(BF16)
            dod[c * BLOCK:(c + 1) * BLOCK, :] = do_ref[rows, :].astype(BF16)
            ddd[c * BLOCK:(c + 1) * BLOCK, :] = dd_ref[rows, :]
            ld[c * BLOCK:(c + 1) * BLOCK, :] = lse_ref[rows, :]

        col = lax.broadcasted_iota(jnp.int32, (BLOCK, 2 * BLOCK), 1)
        for j in range(2):
            mj = jnp.logical_and(lane >= j * HEAD_DIM, lane < (j + 1) * HEAD_DIM)
            first = lane == j * HEAD_DIM
            for c in range(nchunk):
                rows = slice(c * BLOCK, (c + 1) * BLOCK)
                kw = kd[c * BLOCK:(c + 2) * BLOCK, :]
                vw = vd[c * BLOCK:(c + 2) * BLOCK, :]
                kj = jnp.where(mj, kw, jnp.zeros_like(kw))
                vj = jnp.where(mj, vw, jnp.zeros_like(vw))
                s = lax.dot_general(qd[rows, :], kj, contract_lanes, preferred_element_type=F32) + bias_ref[j]
                if c % nb == 0:
                    s = jnp.where(col < BLOCK, NEG_INF, s)
                dp = lax.dot_general(dod[rows, :], vj, contract_lanes, preferred_element_type=F32)
                lse_j = jnp.sum(jnp.where(first, ld[rows, :], 0.0), axis=-1, keepdims=True)
                dd_j = jnp.sum(jnp.where(first, ddd[rows, :], 0.0), axis=-1, keepdims=True)
                p = jnp.exp(s - lse_j)
                ds = p * (dp + dd_j)
                dsacc[j] += ds
                pbuf[j, c] = p.astype(BF16)
                dsbuf[j, c] = ds.astype(BF16)
        for c in range(nchunk):
            rows = slice(c * BLOCK, (c + 1) * BLOCK)
            nxt = slice((c + 1) * BLOCK, (c + 2) * BLOCK)
            has_next = c + 1 < nchunk and (c + 1) % nb != 0
            kw = kd[c * BLOCK:(c + 2) * BLOCK, :]
            dq = jnp.zeros((BLOCK, LANES), F32)
            dk = jnp.zeros((BLOCK, LANES), F32)
            dv = jnp.zeros((BLOCK, LANES), F32)
            for j in range(2):
                mj = jnp.logical_and(lane >= j * HEAD_DIM, lane < (j + 1) * HEAD_DIM)
                zero = jnp.zeros((BLOCK, LANES), BF16)
                dq = dq + jnp.dot(dsbuf[j, c], jnp.where(mj, kw, jnp.zeros_like(kw)), preferred_element_type=F32)
                dk = dk + lax.dot_general(dsbuf[j, c, :, BLOCK:], jnp.where(mj, qd[rows, :], zero), contract_rows,
                                          preferred_element_type=F32)
                dv = dv + lax.dot_general(pbuf[j, c, :, BLOCK:], jnp.where(mj, dod[rows, :], zero), contract_rows,
                                          preferred_element_type=F32)
                if has_next:
                    dk = dk + lax.dot_general(dsbuf[j, c + 1, :, :BLOCK], jnp.where(mj, qd[nxt, :], zero),
                                              contract_rows, preferred_element_type=F32)
                    dv = dv + lax.dot_general(pbuf[j, c + 1, :, :BLOCK], jnp.where(mj, dod[nxt, :], zero),
                                              contract_rows, preferred_element_type=F32)
            dqd[rows, :] = dq
            dkd[rows, :] = dk
            dvd[rows, :] = dv

        dsout_ref[...] = dsin_ref[...] + dsacc[...]

        dgq = jnp.zeros((1, LANES), F32)
        dgk = jnp.zeros((1, LANES), F32)
        for c in range(nchunk):
            rows = _chunk_rows(c, d, nb)
            qv = q_ref[rows, :]
            rq = lax.rsqrt(_head_mean(qv * qv, seg) + EPS)
            qh = qv * rq
            dy = dqd[c * BLOCK:(c + 1) * BLOCK, :]
            dgq = dgq + jnp.sum(dy * qh, axis=0, keepdims=True) * qscale
            dxh = dy * qg
            ddd[rows, :] = rq * (dxh - qh * _head_mean(dxh * qh, seg))
            kv = k_ref[rows, :]
            rk = lax.rsqrt(_head_mean(kv * kv, seg) + EPS)
            kh = kv * rk
            dy = dkd[c * BLOCK:(c + 1) * BLOCK, :]
            dgk = dgk + jnp.sum(dy * kh, axis=0, keepdims=True)
            dxh = dy * kg
            ld[rows, :] = rk * (dxh - kh * _head_mean(dxh * kh, seg))
        dq_ref[...] = ddd[...].astype(BF16)
        dk_ref[...] = ld[...].astype(BF16)
        for c in range(nchunk):
            ddd[_chunk_rows(c, d, nb), :] = dvd[c * BLOCK:(c + 1) * BLOCK, :]
        dv_ref[...] = ddd[...].astype(BF16)
        dgq_ref[0] = dgq
        dgk_ref[0] = dgk

    ucol = lambda c0: pl.BlockSpec((t, LANES), lambda h: (0, c0 + h))
    vec = pl.BlockSpec((1, LANES), lambda h: (0, 0))
    oblk = pl.BlockSpec((t, LANES), lambda h: (0, h))
    bblk = pl.BlockSpec((2, BLOCK, 2 * BLOCK), lambda h: (gi * hp + h, 0, 0))
    gblk = pl.BlockSpec((1, 1, LANES), lambda h: (h, 0, 0))
    osh = jax.ShapeDtypeStruct((t, hpg * HEAD_DIM), BF16)
    gsh = jax.ShapeDtypeStruct((hp, 1, LANES), F32)
    return pl.pallas_call(
        body, name=name, grid=(hp,),
        in_specs=[ucol(qc0), ucol(kc0), ucol(vc0), oblk, oblk, oblk, vec, vec, bblk, bblk],
        out_specs=(oblk, oblk, oblk, gblk, gblk, bblk),
        out_shape=(osh, osh, osh, gsh, gsh, jax.ShapeDtypeStruct(ds_in.shape, F32)),
        input_output_aliases={9: 5},
        scratch_shapes=[pltpu.VMEM((t, LANES), BF16), pltpu.VMEM((t + BLOCK, LANES), BF16),
                        pltpu.VMEM((t + BLOCK, LANES), BF16), pltpu.VMEM((t, LANES), BF16),
                        pltpu.VMEM((t, LANES), F32), pltpu.VMEM((t, LANES), F32), pltpu.VMEM((t, LANES), F32),
                        pltpu.VMEM((t, LANES), F32), pltpu.VMEM((t, LANES), F32),
                        pltpu.VMEM((2, BLOCK, 2 * BLOCK), F32), pltpu.VMEM((2, nchunk, BLOCK, 2 * BLOCK), BF16),
                        pltpu.VMEM((2, nchunk, BLOCK, 2 * BLOCK), BF16)],
        compiler_params=_cparams(("arbitrary",)),
    )(u, u, u, do_g, dd_g, lse_g, qg2, kg2, bias, ds_in)


def _group_weights(l0, l1, l2):
    mx = jnp.maximum(jnp.maximum(l0, l1), l2)
    e0, e1, e2 = jnp.exp(l0 - mx), jnp.exp(l1 - mx), jnp.exp(l2 - mx)
    inv = 1.0 / (e0 + e1 + e2)
    return e0 * inv, e1 * inv, e2 * inv


def _combine_fwd(os_, lses, name):
    t, ao = os_[0].shape
    ROWS = _rows(t, 26 * ao)

    def body(o0, o1, o2, l0, l1, l2, o_ref):
        w0, w1, w2 = _group_weights(l0[...], l1[...], l2[...])
        o_ref[...] = (w0 * o0[...] + w1 * o1[...] + w2 * o2[...]).astype(BF16)

    row = pl.BlockSpec((ROWS, ao), lambda i: (i, 0))
    return pl.pallas_call(
        body, name=name, grid=(t // ROWS,), in_specs=[row] * 6, out_specs=row,
        out_shape=jax.ShapeDtypeStruct((t, ao), BF16), compiler_params=_cparams(("arbitrary",)),
    )(*os_, *lses)


def _combine_bwd(do, os_, lses, name):
    t, ao = do.shape
    idx = np.arange(ao) // HEAD_DIM
    seg = jnp.asarray((idx[:, None] == idx[None, :]).astype(np.float32), dtype=BF16)
    ROWS = _rows(t, 52 * ao)

    def body(do_ref, o0, o1, o2, l0, l1, l2, seg_ref, g0, g1, g2, d0, d1, d2):
        w0, w1, w2 = _group_weights(l0[...], l1[...], l2[...])
        dov = do_ref[...]
        o = w0 * o0[...] + w1 * o1[...] + w2 * o2[...]
        sd = _segment_sum(dov * o, seg_ref[...])
        for w, gref, dref in ((w0, g0, d0), (w1, g1, d1), (w2, g2, d2)):
            gref[...] = w * dov
            dref[...] = -(w * sd)

    row = pl.BlockSpec((ROWS, ao), lambda i: (i, 0))
    sh = jax.ShapeDtypeStruct((t, ao), F32)
    outs = pl.pallas_call(
        body, name=name, grid=(t // ROWS,), in_specs=[row] * 7 + [pl.BlockSpec((ao, ao), lambda i: (0, 0))],
        out_specs=(row,) * 6, out_shape=(sh,) * 6, compiler_params=_cparams(("arbitrary",)),
    )(do, *os_, *lses, seg)
    return outs[:3], outs[3:]


def _adamw(w, g, m, v, name):
    shape = w.shape
    cols = shape[-1]
    rows = int(np.prod(shape[:-1]))
    tr = rows if rows <= 512 else _tile_rows(rows)
    c1 = 1.0 - ADAM_B1 ** ADAM_STEP
    c2 = 1.0 - ADAM_B2 ** ADAM_STEP

    def body(w_ref, g_ref, m_ref, v_ref, d_ref, nm_ref, nv_ref):
        gv = g_ref[...]
        mn = ADAM_B1 * m_ref[...] + (1.0 - ADAM_B1) * gv
        vn = ADAM_B2 * v_ref[...] + (1.0 - ADAM_B2) * (gv * gv)
        nm_ref[...] = mn
        nv_ref[...] = vn
        d_ref[...] = -ADAM_LR * ((mn / c1) / (jnp.sqrt(vn / c2) + ADAM_EPS) + ADAM_WD * w_ref[...])

    blk = pl.BlockSpec((tr, cols), lambda i: (i, 0))
    sh = jax.ShapeDtypeStruct((rows, cols), F32)
    outs = pl.pallas_call(
        body, name=name, grid=(rows // tr,), in_specs=[blk] * 4, out_specs=(blk,) * 3, out_shape=(sh,) * 3,
        compiler_params=_cparams(("arbitrary",)),
    )(*[a.reshape(rows, cols) for a in (w, g, m, v)])
    return tuple(o.reshape(shape) for o in outs)


def _tile_rows(rows):
    for t in (512, 256, 128, 64, 32, 16, 8):
        if rows % t == 0:
            return t
    return rows


def _sum_slots(recv, parts, me, name):
    _, rows, cols = recv.shape
    tr = rows if rows <= 512 else _tile_rows(rows)

    def body(me_ref, r_ref, own_ref, o_ref):
        acc = jnp.zeros(o_ref.shape, F32)
        for s in range(N_DEV):
            acc = acc + jnp.where(me_ref[0] == s, own_ref[...], r_ref[s]).astype(F32)
        o_ref[...] = acc

    return pl.pallas_call(
        body, name=name,
        grid_spec=pltpu.PrefetchScalarGridSpec(
            num_scalar_prefetch=1, grid=(rows // tr,),
            in_specs=[pl.BlockSpec((N_DEV, tr, cols), lambda i, me: (0, i, 0)),
                      pl.BlockSpec((None, tr, cols), lambda i, me: (me[0], i, 0))],
            out_specs=pl.BlockSpec((tr, cols), lambda i, me: (i, 0))),
        out_shape=jax.ShapeDtypeStruct((rows, cols), F32), compiler_params=_cparams(("arbitrary",)),
    )(me.reshape(1), recv, parts)


def _peer(k):
    x, y, c = lax.axis_index("x"), lax.axis_index("y"), lax.axis_index("c")
    return (1 - x if k & 4 else x, 1 - y if k & 2 else y, 1 - c if k & 1 else c)


def _dev_index(p):
    return 4 * p[0] + 2 * p[1] + p[2]


HBM_SPEC = pl.BlockSpec(memory_space=pltpu.HBM)
SEM_SPEC = pl.BlockSpec(memory_space=pltpu.SEMAPHORE)
ANY_SPEC = pl.BlockSpec(memory_space=pl.ANY)
CHIPS = (4, 2, 6)


def _remote(src, dst, send_sem, recv_sem, to):
    return pltpu.make_async_remote_copy(src_ref=src, dst_ref=dst, send_sem=send_sem, recv_sem=recv_sem,
                                        device_id=to, device_id_type=MESH)


def _hbm(a):
    return pltpu.with_memory_space_constraint(a, pltpu.HBM)


def _split_call(body, name, bufs, sems_in, sem_out_sizes, after):
    nb, ns, no = len(bufs), len(sems_in), len(sem_out_sizes)
    extra = [] if after is None else [after]

    def kern(*refs):
        pos = nb + ns + len(extra)
        body(refs[:nb], refs[nb:nb + ns], refs[pos:pos + no])
        token_ref = refs[pos + no + nb]
        token_ref[...] = jnp.zeros_like(token_ref)

    out_shape = (tuple(pltpu.SemaphoreType.DMA((s,)) for s in sem_out_sizes)
                 + tuple(pltpu.HBM(b.shape, b.dtype) for b in bufs) + (jax.ShapeDtypeStruct((8, LANES), F32),))
    res = pl.pallas_call(
        kern, name=name, out_shape=out_shape,
        in_specs=[HBM_SPEC] * nb + [SEM_SPEC] * ns + [ANY_SPEC] * len(extra),
        out_specs=(SEM_SPEC,) * no + (HBM_SPEC,) * nb + (pl.BlockSpec(memory_space=pltpu.VMEM),),
        input_output_aliases={i: no + i for i in range(nb)},
        compiler_params=pltpu.CompilerParams(has_side_effects=pltpu.SideEffectType.DATAFLOW_SIDE_EFFECTING),
    )(*bufs, *sems_in, *extra)
    return res[:no], res[no:no + nb], res[no + nb]


def _gather_start(shards, lands, after, name):
    n = len(shards)

    def body(bufs, _, sems):
        ins, lnd = bufs[:n], bufs[n:]
        d2d_s, d2d_r, ici_s, ici_r = sems
        me = _dev_index(_peer(0))
        for j, k in enumerate(CHIPS):
            for i in range(n):
                _remote(ins[i], lnd[i].at[me], ici_s.at[j], ici_r.at[j], _peer(k)).start()
        for i in range(n):
            _remote(ins[i], lnd[i].at[me], d2d_s.at[0], d2d_r.at[0], _peer(1)).start()

    return _split_call(body, name, [_hbm(a) for a in (*shards, *lands)], [], (1, 1, 3, 3), after)


def _gather_forward(n, bufs, ici_r, after, name):
    def body(refs, sems_in, sems):
        ins, lnd = refs[:n], refs[n:]
        (arrived,) = sems_in
        fwd_s, fwd_r = sems
        for j, k in enumerate(CHIPS):
            blk = _dev_index(_peer(k))
            for i in range(n):
                _remote(ins[i], lnd[i].at[blk], fwd_s.at[j], arrived.at[j], _peer(k)).wait_recv()
            for i in range(n):
                _remote(lnd[i].at[blk], lnd[i].at[blk], fwd_s.at[j], fwd_r.at[j], _peer(1)).start()

    return _split_call(body, name, bufs, [ici_r], (3, 3), after)


def _gather_finish(n, bufs, d2d_s, d2d_r, ici_s, fwd_s, fwd_r, after, name):
    def body(refs, sems_in, _):
        ins, lnd = refs[:n], refs[n:]
        d2d_send, d2d_recv, ici_send, fwd_send, fwd_recv = sems_in
        sib = _peer(1)
        for i in range(n):
            cp = _remote(ins[i], lnd[i].at[_dev_index(sib)], d2d_send.at[0], d2d_recv.at[0], sib)
            cp.wait_send()
            cp.wait_recv()
        for j, k in enumerate(CHIPS):
            passed = _dev_index(_peer(k))
            landed = _dev_index(_peer(k | 1))
            for i in range(n):
                _remote(ins[i], lnd[i].at[passed], ici_send.at[j], fwd_recv.at[j], _peer(k)).wait_send()
                cp = _remote(lnd[i].at[passed], lnd[i].at[landed], fwd_send.at[j], fwd_recv.at[j], sib)
                cp.wait_send()
                cp.wait_recv()

    _, out, token = _split_call(body, name, bufs, [d2d_s, d2d_r, ici_s, fwd_s, fwd_r], (), after)
    return out[n:], token


def _exchange_start(parts, lands, after, name):
    n = len(parts)

    def body(bufs, _, sems):
        src, lnd = bufs[:n], bufs[n:]
        send, recv = sems
        me = _dev_index(_peer(0))
        for k in (4, 5, 2, 3, 6, 7, 1):
            to = _peer(k)
            for i in range(n):
                _remote(src[i].at[_dev_index(to)], lnd[i].at[me], send.at[k - 1], recv.at[k - 1], to).start()

    return _split_call(body, name, [_hbm(a) for a in (*parts, *lands)], [], (7, 7), after)


def _exchange_finish(n, bufs, send, recv, after, name):
    def body(refs, sems_in, _):
        src, lnd = refs[:n], refs[n:]
        send_, recv_ = sems_in
        me = _dev_index(_peer(0))
        for k in range(1, N_DEV):
            frm = _peer(k)
            for i in range(n):
                cp = _remote(src[i].at[me], lnd[i].at[_dev_index(frm)], send_.at[k - 1], recv_.at[k - 1], frm)
                cp.wait_send()
                cp.wait_recv()

    _, out, token = _split_call(body, name, bufs, [send, recv], (), after)
    return out[:n], out[n:], token


def _all_reduce_small(v, name):
    rows = v.shape[0]

    def body(v_ref, o_ref, buf, send_sems, recv_sems):
        me = _dev_index(_peer(0))
        buf[me] = v_ref[...]
        copies = []
        for k in range(1, N_DEV):
            copies.append(pltpu.make_async_remote_copy(
                src_ref=v_ref, dst_ref=buf.at[me], send_sem=send_sems.at[k - 1], recv_sem=recv_sems.at[k - 1],
                device_id=_peer(k), device_id_type=MESH))
        for cp in copies:
            cp.start()
        for k in range(1, N_DEV):
            pltpu.make_async_remote_copy(
                src_ref=v_ref, dst_ref=buf.at[_dev_index(_peer(k))], send_sem=send_sems.at[k - 1],
                recv_sem=recv_sems.at[k - 1], device_id=_peer(k), device_id_type=MESH).wait_recv()
        for cp in copies:
            cp.wait_send()
        acc = buf[0]
        for s in range(1, N_DEV):
            acc = acc + buf[s]
        o_ref[...] = acc

    vm = pl.BlockSpec(memory_space=pltpu.VMEM)
    return pl.pallas_call(
        body, name=name, in_specs=[vm], out_specs=vm, out_shape=jax.ShapeDtypeStruct(v.shape, F32),
        scratch_shapes=[pltpu.VMEM((N_DEV, rows, LANES), F32), pltpu.SemaphoreType.DMA((7,)),
                        pltpu.SemaphoreType.DMA((7,))],
    )(v)


def _columns(cdim, ao):
    q_col = 2 * cdim
    attn_dim = N_GROUPS * ao
    return (q_col, q_col + attn_dim, q_col + 2 * attn_dim), q_col + 3 * attn_dim


def _layer_fwd(x, sm, bg, get_rest, bias, hpg):
    cdim = sm["conv_ln_g"].shape[0]
    ao = hpg * HEAD_DIM
    cols, gate_col = _columns(cdim, ao)
    qg2 = jnp.tile(sm["q_norm_g"], 2).reshape(1, LANES)
    kg2 = jnp.tile(sm["k_norm_g"], 2).reshape(1, LANES)
    h1 = _rms_fwd(x, sm["norm1_g"], "rms1_fwd")
    u = _mm_in_pairs(h1, bg["w_in"], "mm_in")
    zc = _conv_fwd(u, bg["conv_dw_w"], sm["conv_dw_b"], cdim, "conv_fwd")
    zs = _ln_swish_fwd(zc, sm["conv_ln_g"], sm["conv_ln_b"], "ln_swish_fwd")
    os_, lses = [], []
    for gi in range(N_GROUPS):
        o_g, lse_g = _attn_fwd(u, qg2, kg2, bias, gi, cols, hpg, "attn_fwd_g%d" % gi)
        os_.append(o_g)
        lses.append(lse_g)
    o = _combine_fwd(os_, lses, "combine_fwd")
    bg = {**bg, **get_rest(o)}
    yc = _mm(zs, bg["w_conv_out"], name="mm_conv_out")
    ya = _mm(o, bg["w_attn_out"], name="mm_attn_out")
    mg = _gate_fwd(u, yc, ya, gate_col, "gate_fwd")
    x1 = _mm(mg, bg["w_out"], epi="res", extra=x, name="mm_out")
    h2 = _rms_fwd(x1, sm["norm2_g"], "rms2_fwd")
    f, act = _mm(h2, bg["w_ff1"], epi="relu2", name="mm_ff1")
    x2 = _mm(act, bg["w_ff2"], epi="res", extra=x1, name="mm_ff2")
    saved = dict(x=x, h1=h1, u=u, zc=zc, zs=zs, yc=yc, os=os_, lses=lses, o=o, ya=ya, mg=mg, x1=x1, h2=h2, f=f,
                 act=act, qg2=qg2, kg2=kg2)
    return x2, saved, bg


GRAD_GROUPS = (("w_ff2", "w_ff1"), ("w_out", "w_conv_out", "w_attn_out", "conv_dw_w"), ("w_in",))


def _layer_bwd(dx, s, sm, bg, bias, ds_sum, after, emit):
    cdim = sm["conv_ln_g"].shape[0]
    ao = bg["w_attn_out"].shape[0]
    hpg = ao // HEAD_DIM
    hp = hpg // 2
    cols, gate_col = _columns(cdim, ao)
    g = {}
    df = _mm(dx, bg["w_ff2"], tb=True, epi="drelu2", extra=s["f"], out_dtype=BF16, after=after, name="mm_dff2")
    g["w_ff2"] = _mm(s["act"], dx, ta=True, out_dtype=BF16, name="mm_gw_ff2")
    g["w_ff1"] = _mm(s["h2"], df, ta=True, out_dtype=BF16, out_slots=True, name="mm_gw_ff1")
    after = emit(GRAD_GROUPS[0], g)
    dh2 = _mm(df, bg["w_ff1"], tb=True, after=after, name="mm_dff1")
    dx1, dg2 = _rms_bwd(dh2, s["x1"], sm["norm2_g"], dx, "rms2_bwd")
    g["norm2_g"] = dg2[0]
    dmg = _mm(dx1, bg["w_out"], tb=True, name="mm_dout")
    g["w_out"] = _mm(s["mg"], dx1, ta=True, out_dtype=BF16, name="mm_gw_out")
    dyc, dya, dugc, duga = _gate_bwd(dmg, s["u"], s["yc"], s["ya"], gate_col, "gate_bwd")
    dzs = _mm(dyc, bg["w_conv_out"], tb=True, name="mm_dconv_out")
    g["w_conv_out"] = _mm(s["zs"], dyc, ta=True, out_dtype=BF16, name="mm_gw_conv_out")
    do = _mm(dya, bg["w_attn_out"], tb=True, name="mm_dattn_out")
    g["w_attn_out"] = _mm(s["o"], dya, ta=True, out_dtype=BF16, name="mm_gw_attn_out")
    dzc, dlg, dlb = _ln_swish_bwd(dzs, s["zc"], sm["conv_ln_g"], sm["conv_ln_b"], "ln_swish_bwd")
    g["conv_ln_g"] = dlg[0]
    g["conv_ln_b"] = dlb[0]
    da, dgt, dcw, dcb = _conv_bwd(dzc, s["u"], bg["conv_dw_w"], cdim, "conv_bwd")
    g["conv_dw_w"] = dcw[:CONV_WIDTH].astype(BF16)
    g["conv_dw_b"] = dcb[0]
    after = emit(GRAD_GROUPS[1], g)
    do_gs, dd_gs = _combine_bwd(do, s["os"], s["lses"], "combine_bwd")
    dqs, dks, dvs = [], [], []
    dgq = jnp.zeros((HEAD_DIM,), F32)
    dgk = jnp.zeros((HEAD_DIM,), F32)
    for gi in range(N_GROUPS):
        dq, dk, dv, gq, gk, ds_sum = _attn_bwd(s["u"], do_gs[gi], dd_gs[gi], s["lses"][gi], s["qg2"], s["kg2"],
                                               bias, ds_sum, gi, cols, hpg, "attn_bwd_g%d" % gi)
        dqs.append(dq)
        dks.append(dk)
        dvs.append(dv)
        dgq = dgq + jnp.sum(gq.reshape(hp * 2, HEAD_DIM), axis=0)
        dgk = dgk + jnp.sum(gk.reshape(hp * 2, HEAD_DIM), axis=0)
    g["q_norm_g"] = dgq
    g["k_norm_g"] = dgk
    du = jnp.concatenate([da, dgt] + dqs + dks + dvs + [dugc, duga], axis=1)
    g["w_in"] = _mm_gw_in_pairs(s["h1"], du, after, "mm_gw_in")
    after = emit(GRAD_GROUPS[2], g)
    dh1 = _mm_din_pairs(du, bg["w_in"], after, "mm_din")
    dx0, dg1 = _rms_bwd(dh1, s["x"], sm["norm1_g"], dx1, "rms1_bwd")
    g["norm1_g"] = dg1[0]
    return dx0, g, ds_sum


BIG = ("w_in", "conv_dw_w", "w_conv_out", "w_attn_out", "w_out", "w_ff1", "w_ff2")
COL_SHARDED = ("w_in", "conv_dw_w", "w_conv_out", "w_attn_out", "w_ff1")
SMALL = ("rel_bias", "norm1_g", "q_norm_g", "k_norm_g", "conv_dw_b", "conv_ln_g", "conv_ln_b", "norm2_g")
WEIGHTS = ("rel_bias", "norm1_g", "w_in", "q_norm_g", "k_norm_g", "conv_dw_w", "conv_dw_b", "conv_ln_g", "conv_ln_b",
           "w_conv_out", "w_attn_out", "w_out", "norm2_g", "w_ff1", "w_ff2")


def _to_whole(name, gathered):
    n, a, b = gathered.shape
    if name in COL_SHARDED:
        return gathered.transpose(1, 0, 2).reshape(a, n * b)
    return gathered.reshape(n * a, b)


def _to_slots(name, whole):
    a, b = whole.shape
    if name in COL_SHARDED:
        return whole.reshape(a, N_DEV, b // N_DEV).transpose(1, 0, 2)
    return whole.reshape(N_DEV, a // N_DEV, b)


def _own_slot(block, me):
    land = lax.empty((N_DEV,) + block.shape, block.dtype)
    return lax.dynamic_update_slice(land, block[None], (me,) + (0,) * block.ndim)


def kernel(x, rel_bias, norm1_g, w_in, q_norm_g, k_norm_g, conv_dw_w, conv_dw_b, conv_ln_g, conv_ln_b, w_conv_out, w_attn_out, w_out, norm2_g, w_ff1, w_ff2, loss_target, m_rel_bias, m_norm1_g, m_w_in, m_q_norm_g, m_k_norm_g, m_conv_dw_w, m_conv_dw_b, m_conv_ln_g, m_conv_ln_b, m_w_conv_out, m_w_attn_out, m_w_out, m_norm2_g, m_w_ff1, m_w_ff2, v_rel_bias, v_norm1_g, v_w_in, v_q_norm_g, v_k_norm_g, v_conv_dw_w, v_conv_dw_b, v_conv_ln_g, v_conv_ln_b, v_w_conv_out, v_w_attn_out, v_w_out, v_norm2_g, v_w_ff1, v_w_ff2):
    w = dict(rel_bias=rel_bias, norm1_g=norm1_g, w_in=w_in, q_norm_g=q_norm_g, k_norm_g=k_norm_g, conv_dw_w=conv_dw_w,
             conv_dw_b=conv_dw_b, conv_ln_g=conv_ln_g, conv_ln_b=conv_ln_b, w_conv_out=w_conv_out,
             w_attn_out=w_attn_out, w_out=w_out, norm2_g=norm2_g, w_ff1=w_ff1, w_ff2=w_ff2)
    mom = dict(rel_bias=m_rel_bias, norm1_g=m_norm1_g, w_in=m_w_in, q_norm_g=m_q_norm_g, k_norm_g=m_k_norm_g,
               conv_dw_w=m_conv_dw_w, conv_dw_b=m_conv_dw_b, conv_ln_g=m_conv_ln_g, conv_ln_b=m_conv_ln_b,
               w_conv_out=m_w_conv_out, w_attn_out=m_w_attn_out, w_out=m_w_out, norm2_g=m_norm2_g, w_ff1=m_w_ff1,
               w_ff2=m_w_ff2)
    var = dict(rel_bias=v_rel_bias, norm1_g=v_norm1_g, w_in=v_w_in, q_norm_g=v_q_norm_g, k_norm_g=v_k_norm_g,
               conv_dw_w=v_conv_dw_w, conv_dw_b=v_conv_dw_b, conv_ln_g=v_conv_ln_g, conv_ln_b=v_conv_ln_b,
               w_conv_out=v_w_conv_out, w_attn_out=v_w_attn_out, w_out=v_w_out, norm2_g=v_norm2_g, w_ff1=v_w_ff1,
               w_ff2=v_w_ff2)

    depth = norm1_g.shape[0]
    me = 4 * lax.axis_index("x") + 2 * lax.axis_index("y") + lax.axis_index("c")
    odd_core = lax.axis_index("c") == 1
    hpg = w_attn_out.shape[1] // HEAD_DIM
    buckets = jnp.asarray(_bucket_table())
    bias = _bias_expand(rel_bias, buckets, hpg, "bias_expand")

    first_names = ("w_in", "conv_dw_w")
    rest_names = tuple(k for k in BIG if k not in first_names)

    def chain_start(l, names, after):
        shards = [w[k][l] if k == "conv_dw_w" else w[k][l].astype(BF16) for k in names]
        if "w_in" in names:
            i = names.index("w_in")
            shards[i] = jnp.where(odd_core, jnp.pad(shards[i], ((0, 0), (SHIFT, 0))),
                                  jnp.pad(shards[i], ((0, 0), (0, SHIFT))))
        sems, bufs, token = _gather_start(shards, [_own_slot(s, me) for s in shards], after,
                                          "gather_start_%s_l%d" % (names[0], l))
        return dict(l=l, names=names, sems=sems, bufs=bufs, token=token)

    def chain_forward(ch, after):
        fwd, bufs, token = _gather_forward(len(ch["names"]), ch["bufs"], ch["sems"][3], after,
                                           "gather_forward_%s_l%d" % (ch["names"][0], ch["l"]))
        ch.update(fwd=fwd, bufs=bufs)
        return token

    def chain_finish(ch, after):
        d2d_s, d2d_r, ici_s, _ = ch["sems"]
        gathered, _ = _gather_finish(len(ch["names"]), ch["bufs"], d2d_s, d2d_r, ici_s, ch["fwd"][0], ch["fwd"][1],
                                     after, "gather_finish_%s_l%d" % (ch["names"][0], ch["l"]))
        out = {k: a if k == "w_in" else _to_whole(k, a) for k, a in zip(ch["names"], gathered)}
        if "conv_dw_w" in out:
            out["conv_dw_w"] = jnp.pad(out["conv_dw_w"], ((0, CONV_TAPS_PADDED - CONV_WIDTH), (0, 0)))
        return out

    xs = x[0]
    saved, bigs, smalls = [], [], []
    chains = {}
    for l in range(depth):
        sm = {k: w[k][l] for k in SMALL if k != "rel_bias"}
        if l == 0:
            first = chain_start(0, first_names, None)
            token = chain_forward(first, None)
            rest = chain_start(0, rest_names, token)
            bg = chain_finish(first, rest["token"])

            def get_rest(o, rest=rest):
                token = chain_forward(rest, o)
                if depth > 1:
                    chains[1] = chain_start(1, BIG, token)
                    token = chains[1]["token"]
                return chain_finish(rest, token)
        else:
            token = chain_forward(chains[l], xs)
            if l + 1 < depth:
                chains[l + 1] = chain_start(l + 1, BIG, token)
                token = chains[l + 1]["token"]
            whole = chain_finish(chains[l], token)
            bg = {k: whole[k] for k in first_names}

            def get_rest(o, whole=whole):
                return {k: whole[k] for k in rest_names}
        xs, sv, bg = _layer_fwd(xs, sm, bg, get_rest, bias, hpg)
        saved.append(sv)
        bigs.append(bg)
        smalls.append(sm)

    loss, dx = _loss_and_grad(xs, loss_target[0], "loss")

    ds_sum = jnp.zeros((N_GROUPS * hpg, BLOCK, 2 * BLOCK), F32)
    g = {k: [None] * depth for k in SMALL if k != "rel_bias"}
    grad_layers = {k: [None] * depth for k in BIG}
    pending = []

    def finish_oldest(after):
        names, l, (send, recv), bufs = pending.pop(0)
        parts, recvd, token = _exchange_finish(len(names), bufs, send, recv, after,
                                               "exchange_finish_%s_l%d" % (names[0], l))
        for k, r, p in zip(names, recvd, parts):
            shp = w[k].shape[1:]
            three = (N_DEV, -1, r.shape[-1])
            total = _sum_slots(r.reshape(three), p.reshape(three), me, "sum_" + k)
            if k == "w_in":
                total = jnp.where(odd_core, total[:, SHIFT:], total[:, :shp[-1]])
            grad_layers[k][l] = total.reshape(shp)
        return token

    def make_emit(l):
        def emit(names, gl):
            parts = [gl[k] if k in ("w_ff1", "w_in") else _to_slots(k, gl[k]) for k in names]
            token = finish_oldest(parts[0]) if len(pending) >= len(GRAD_GROUPS) else None
            lands = [lax.empty(p.shape, p.dtype) for p in parts]
            sems, bufs, token = _exchange_start(parts, lands, token, "exchange_start_%s_l%d" % (names[0], l))
            pending.append((names, l, sems, bufs))
            return token
        return emit

    token = None
    for l in reversed(range(depth)):
        dx, gl, ds_sum = _layer_bwd(dx, saved[l], smalls[l], bigs[l], bias, ds_sum, token, make_emit(l))
        for k in g:
            g[k][l] = gl[k]
        token = None
    while pending:
        finish_oldest(None)
    grad_x = dx

    g = {k: jnp.stack(v) for k, v in g.items()}
    db = _bias_reduce(ds_sum, buckets, hpg, "bias_reduce")
    g["rel_bias"] = db[:, 0, :NUM_BUCKETS].T

    flat = jnp.concatenate([g[k].reshape(-1) for k in SMALL])
    nflat = flat.shape[0]
    rows = -(-nflat // (8 * LANES)) * 8
    packed = jnp.pad(flat, (0, rows * LANES - nflat)).reshape(rows, LANES)
    total = _all_reduce_small(packed, "reduce_small").reshape(-1)
    grad = {}
    off = 0
    for k in SMALL:
        size = int(np.prod(w[k].shape))
        grad[k] = total[off:off + size].reshape(w[k].shape)
        off += size

    for k in BIG:
        grad[k] = jnp.stack(grad_layers[k])

    loss = lax.psum(loss[0, 0], ("x", "y", "c"))
    outs = {k: _adamw(w[k], grad[k], mom[k], var[k], "adamw_" + k) for k in WEIGHTS}
    return (loss, grad_x[None], *[grad[k] for k in WEIGHTS], *[outs[k][0] for k in WEIGHTS],
            *[outs[k][1] for k in WEIGHTS], *[outs[k][2] for k in WEIGHTS])
```

```python
import functools
import math

import numpy as np
import jax
import jax.numpy as jnp
from jax import lax
from jax.experimental import pallas as pl
from jax.experimental.pallas import tpu as pltpu

F32 = jnp.float32
BF16 = jnp.bfloat16

HEAD_DIM = 64
N_GROUPS = 3
DILATIONS = (1, 4, 16)
SUB_WINDOW = 128
BLOCK = 128
CONV_WIDTH = 31
CONV_TAPS_PADDED = 32
NUM_BUCKETS = 32
MAX_REL_DISTANCE = 2048
EPS = 1e-6
NEG_INF = -1e30
LANES = 128

ADAM_LR = 0.001
ADAM_B1 = 0.9
ADAM_B2 = 0.999
ADAM_EPS = 1e-08
ADAM_WD = 0.01
ADAM_STEP = 10

N_DEV = 8
VMEM_LIMIT = 56 * 1024 * 1024
MESH = pl.DeviceIdType.MESH


def _cparams(sem=None):
    return pltpu.CompilerParams(dimension_semantics=sem, vmem_limit_bytes=VMEM_LIMIT)


def _tile(n, target):
    if n <= target:
        return n
    t = (target // LANES) * LANES
    while t >= LANES:
        if n % t == 0:
            return t
        t -= LANES
    return n


def _sigmoid(v):
    return 1.0 / (1.0 + jnp.exp(-v))


MM_VMEM_BUDGET = 40 * 1024 * 1024


def _mm_tiles(m, n, kdim, a_bytes, b_bytes, io_bytes):
    tm, tn, tk = _tile(m, 1024), _tile(n, 1024), _tile(kdim, 2048)

    def need(tm, tn, tk):
        blocks = 2 * (tm * tk * a_bytes + tk * tn * b_bytes + tm * tn * io_bytes)
        casts = (tm * tk * 2 if a_bytes == 4 else 0) + (tk * tn * 2 if b_bytes == 4 else 0)
        return blocks + casts + 2 * tm * tn * 4

    while need(tm, tn, tk) > MM_VMEM_BUDGET:
        if tk >= 1024 and tk % 256 == 0:
            tk //= 2
        elif tn >= 512 and tn % 256 == 0:
            tn //= 2
        else:
            tm //= 2
    return tm, tn, tk


def _mm(a, b, *, ta=False, tb=False, out_dtype=F32, epi=None, extra=None, after=None, out_slots=False,
        a_relu2=False, name):
    m = a.shape[1] if ta else a.shape[0]
    kdim = a.shape[0] if ta else a.shape[1]
    n = b.shape[0] if tb else b.shape[1]
    io_bytes = jnp.dtype(out_dtype).itemsize + (extra.dtype.itemsize if extra is not None else 0)
    tm, tn, tk = _mm_tiles(m, n // N_DEV if out_slots else n, kdim, a.dtype.itemsize, b.dtype.itemsize, io_bytes)
    if out_slots:
        assert epi is None and tn == n // N_DEV
    nk = kdim // tk
    a_spec = pl.BlockSpec((tk, tm), lambda i, j, k: (k, i)) if ta else pl.BlockSpec((tm, tk), lambda i, j, k: (i, k))
    b_spec = pl.BlockSpec((tn, tk), lambda i, j, k: (j, k)) if tb else pl.BlockSpec((tk, tn), lambda i, j, k: (k, j))
    o_spec = (pl.BlockSpec((None, tm, tn), lambda i, j, k: (j, i, 0)) if out_slots
              else pl.BlockSpec((tm, tn), lambda i, j, k: (i, j)))
    dims = (((0 if ta else 1,), (1 if tb else 0,)), ((), ()))
    n_in = 2 + (extra is not None) + (after is not None)
    n_out = 1

    def body(*refs):
        a_ref, b_ref = refs[0], refs[1]
        e_ref = refs[2] if extra is not None else None
        outs = refs[n_in:n_in + n_out]

        def product():
            av = a_ref[...]
            if a_relu2:
                r = jnp.maximum(av.astype(F32), 0.0)
                av = r * r
            return lax.dot_general(av.astype(BF16), b_ref[...].astype(BF16), dims, preferred_element_type=F32)

        def finish(acc):
            if epi is None:
                outs[0][...] = acc.astype(outs[0].dtype)
            elif epi == "res":
                outs[0][...] = (e_ref[...] + acc).astype(outs[0].dtype)
            elif epi == "drelu2":
                outs[0][...] = (acc * (2.0 * jnp.maximum(e_ref[...].astype(F32), 0.0))).astype(outs[0].dtype)

        if nk == 1:
            finish(product())
            return
        acc_ref = refs[-1]
        k = pl.program_id(2)

        @pl.when(k == 0)
        def _():
            acc_ref[...] = product()

        @pl.when(jnp.logical_and(k > 0, k < nk - 1))
        def _():
            acc_ref[...] += product()

        @pl.when(k == nk - 1)
        def _():
            finish(acc_ref[...] + product())

    in_specs = ([a_spec, b_spec] + ([o_spec] if extra is not None else [])
                + ([pl.BlockSpec(memory_space=pl.ANY)] if after is not None else []))
    out_shape = jax.ShapeDtypeStruct((N_DEV, m, tn) if out_slots else (m, n), out_dtype)
    out_specs = o_spec
    args = (a, b) + ((extra,) if extra is not None else ()) + ((after,) if after is not None else ())
    return pl.pallas_call(
        body, name=name, grid=(m // tm, n // tn, nk), in_specs=in_specs, out_specs=out_specs, out_shape=out_shape,
        scratch_shapes=[pltpu.VMEM((tm, tn), F32)] if nk > 1 else [],
        compiler_params=_cparams(("parallel", "parallel", "arbitrary")),
    )(*args)


SHIFT = HEAD_DIM


def _pair_blocks(e, o):
    wp = e.shape[-1]
    return e[:, :wp - LANES], e[:, wp - LANES:] + o[:, :LANES], o[:, LANES:]


def _mm_in_pairs(a, wg, name):
    t, kdim = a.shape
    wp = wg.shape[-1]
    ws = wp - SHIFT
    tm = _tile(t, 1024)

    def body(a_ref, e_ref, o_ref, u_ref):
        av = a_ref[...]
        lo, mid, hi = _pair_blocks(e_ref[...], o_ref[...])
        u_ref[:, :wp - LANES] = jnp.dot(av, lo, preferred_element_type=F32).astype(BF16)
        u_ref[:, wp - LANES:wp] = jnp.dot(av, mid, preferred_element_type=F32).astype(BF16)
        u_ref[:, wp:] = jnp.dot(av, hi, preferred_element_type=F32).astype(BF16)

    return pl.pallas_call(
        body, name=name, grid=(N_DEV // 2, t // tm),
        in_specs=[pl.BlockSpec((tm, kdim), lambda p, i: (i, 0)),
                  pl.BlockSpec((None, kdim, wp), lambda p, i: (2 * p, 0, 0)),
                  pl.BlockSpec((None, kdim, wp), lambda p, i: (2 * p + 1, 0, 0))],
        out_specs=pl.BlockSpec((tm, 2 * ws), lambda p, i: (i, p)),
        out_shape=jax.ShapeDtypeStruct((t, N_DEV * ws), BF16), compiler_params=_cparams(("arbitrary", "arbitrary")),
    )(a, wg, wg)


def _mm_din_pairs(du, wg, after, name):
    t = du.shape[0]
    _, kdim, wp = wg.shape
    ws = wp - SHIFT
    tm = _tile(t, 1024)
    npair = N_DEV // 2
    lanes = (((1,), (1,)), ((), ()))
    extra = [] if after is None else [after]

    def body(d_ref, e_ref, o_ref, *rest):
        out_ref, acc_ref = rest[-2], rest[-1]
        p = pl.program_id(1)
        lo, mid, hi = _pair_blocks(e_ref[...], o_ref[...])
        part = (lax.dot_general(d_ref[:, :wp - LANES], lo, lanes, preferred_element_type=F32)
                + lax.dot_general(d_ref[:, wp - LANES:wp], mid, lanes, preferred_element_type=F32)
                + lax.dot_general(d_ref[:, wp:], hi, lanes, preferred_element_type=F32))

        @pl.when(p == 0)
        def _():
            acc_ref[...] = part

        @pl.when(jnp.logical_and(p > 0, p < npair - 1))
        def _():
            acc_ref[...] += part

        @pl.when(p == npair - 1)
        def _():
            out_ref[...] = acc_ref[...] + part

    return pl.pallas_call(
        body, name=name, grid=(t // tm, npair),
        in_specs=[pl.BlockSpec((tm, 2 * ws), lambda i, p: (i, p)),
                  pl.BlockSpec((None, kdim, wp), lambda i, p: (2 * p, 0, 0)),
                  pl.BlockSpec((None, kdim, wp), lambda i, p: (2 * p + 1, 0, 0))]
        + [pl.BlockSpec(memory_space=pl.ANY)] * len(extra),
        out_specs=pl.BlockSpec((tm, kdim), lambda i, p: (i, 0)),
        out_shape=jax.ShapeDtypeStruct((t, kdim), F32), scratch_shapes=[pltpu.VMEM((tm, kdim), F32)],
        compiler_params=_cparams(("arbitrary", "arbitrary")),
    )(du, wg, wg, *extra)


def _mm_gw_in_pairs(h, du, after, name):
    t, kdim = h.shape
    ws = du.shape[1] // N_DEV
    wp = ws + SHIFT
    tm = _tile(kdim, 512)
    rows = (((0,), (0,)), ((), ()))
    extra = [] if after is None else [after]

    def body(h_ref, d_ref, *rest):
        g_ref = rest[-1]
        g = lax.dot_general(h_ref[...], d_ref[...], rows, preferred_element_type=F32)
        g_ref[0] = g[:, :wp].astype(BF16)
        g_ref[1] = g[:, wp - LANES:].astype(BF16)

    return pl.pallas_call(
        body, name=name, grid=(N_DEV // 2, kdim // tm),
        in_specs=[pl.BlockSpec((t, tm), lambda p, i: (0, i)), pl.BlockSpec((t, 2 * ws), lambda p, i: (0, p))]
        + [pl.BlockSpec(memory_space=pl.ANY)] * len(extra),
        out_specs=pl.BlockSpec((2, tm, wp), lambda p, i: (p, i, 0)),
        out_shape=jax.ShapeDtypeStruct((N_DEV, kdim, wp), BF16), compiler_params=_cparams(("arbitrary", "arbitrary")),
    )(h, du, *extra)


ROW_BLOCK_BUDGET = 24 * 1024 * 1024


def _rows(t, row_bytes):
    rows = t
    while rows > 8 and (2 * rows * row_bytes > ROW_BLOCK_BUDGET or t % rows):
        rows //= 2
    return rows


def _rms_fwd(x, g, name):
    t, d = x.shape
    ROWS = _rows(t, 6 * d)

    def body(x_ref, g_ref, h_ref):
        xv = x_ref[...]
        r = lax.rsqrt(jnp.mean(xv * xv, axis=-1, keepdims=True) + EPS)
        h_ref[...] = (xv * r * g_ref[...]).astype(BF16)

    return pl.pallas_call(
        body, name=name, grid=(t // ROWS,),
        in_specs=[pl.BlockSpec((ROWS, d), lambda i: (i, 0)), pl.BlockSpec((1, d), lambda i: (0, 0))],
        out_specs=pl.BlockSpec((ROWS, d), lambda i: (i, 0)),
        out_shape=jax.ShapeDtypeStruct((t, d), BF16), compiler_params=_cparams(("arbitrary",)),
    )(x, g.reshape(1, d))


def _rms_bwd(dh, x, g, dres, name):
    t, d = x.shape
    ROWS = _rows(t, 16 * d)

    def body(dh_ref, x_ref, g_ref, dres_ref, dx_ref, dg_ref):
        @pl.when(pl.program_id(0) == 0)
        def _():
            dg_ref[...] = jnp.zeros_like(dg_ref)

        xv = x_ref[...]
        dhv = dh_ref[...]
        r = lax.rsqrt(jnp.mean(xv * xv, axis=-1, keepdims=True) + EPS)
        xh = xv * r
        dg_ref[...] += jnp.sum(dhv * xh, axis=0, keepdims=True)
        dxh = dhv * g_ref[...]
        dx_ref[...] = dres_ref[...] + r * (dxh - xh * jnp.mean(dxh * xh, axis=-1, keepdims=True))

    row = pl.BlockSpec((ROWS, d), lambda i: (i, 0))
    vec = pl.BlockSpec((1, d), lambda i: (0, 0))
    return pl.pallas_call(
        body, name=name, grid=(t // ROWS,), in_specs=[row, row, vec, row], out_specs=(row, vec),
        out_shape=(jax.ShapeDtypeStruct((t, d), F32), jax.ShapeDtypeStruct((1, d), F32)),
        compiler_params=_cparams(("arbitrary",)),
    )(dh, x, g.reshape(1, d), dres)


def _gate_fwd(u, yc, ya, gate_col, name):
    t, d = yc.shape
    td = math.gcd(_tile(d, 512), gate_col)
    nd = d // td
    c0 = gate_col // td
    ROWS = _rows(t, 14 * td)

    def body(gc_ref, ga_ref, yc_ref, ya_ref, m_ref):
        gc = _sigmoid(gc_ref[...].astype(F32))
        ga = _sigmoid(ga_ref[...].astype(F32))
        m_ref[...] = (gc * yc_ref[...] + ga * ya_ref[...]).astype(BF16)

    blk = pl.BlockSpec((ROWS, td), lambda i, j: (i, j))
    return pl.pallas_call(
        body, name=name, grid=(t // ROWS, nd),
        in_specs=[pl.BlockSpec((ROWS, td), lambda i, j: (i, c0 + j)),
                  pl.BlockSpec((ROWS, td), lambda i, j: (i, c0 + nd + j)), blk, blk],
        out_specs=blk, out_shape=jax.ShapeDtypeStruct((t, d), BF16),
        compiler_params=_cparams(("arbitrary", "arbitrary")),
    )(u, u, yc, ya)


def _gate_bwd(dm, u, yc, ya, gate_col, name):
    t, d = yc.shape
    td = math.gcd(_tile(d, 512), gate_col)
    nd = d // td
    c0 = gate_col // td
    ROWS = _rows(t, 24 * td)

    def body(dm_ref, gc_ref, ga_ref, yc_ref, ya_ref, dyc_ref, dya_ref, dugc_ref, duga_ref):
        dmv = dm_ref[...]
        gc = _sigmoid(gc_ref[...].astype(F32))
        ga = _sigmoid(ga_ref[...].astype(F32))
        dyc_ref[...] = (dmv * gc).astype(BF16)
        dya_ref[...] = (dmv * ga).astype(BF16)
        dugc_ref[...] = (dmv * yc_ref[...] * gc * (1.0 - gc)).astype(BF16)
        duga_ref[...] = (dmv * ya_ref[...] * ga * (1.0 - ga)).astype(BF16)

    blk = pl.BlockSpec((ROWS, td), lambda i, j: (i, j))
    o = jax.ShapeDtypeStruct((t, d), BF16)
    return pl.pallas_call(
        body, name=name, grid=(t // ROWS, nd),
        in_specs=[blk, pl.BlockSpec((ROWS, td), lambda i, j: (i, c0 + j)),
                  pl.BlockSpec((ROWS, td), lambda i, j: (i, c0 + nd + j)), blk, blk],
        out_specs=(blk, blk, blk, blk), out_shape=(o, o, o, o),
        compiler_params=_cparams(("arbitrary", "arbitrary")),
    )(dm, u, u, yc, ya)


def _loss_and_grad(y, target, name):
    t, d = y.shape
    ROWS = _rows(t, 12 * d)
    n = t // ROWS

    def body(y_ref, t_ref, loss_ref, dy_ref, acc_ref):
        i = pl.program_id(0)

        @pl.when(i == 0)
        def _():
            acc_ref[...] = jnp.zeros_like(acc_ref)

        diff = y_ref[...] - t_ref[...]
        dy_ref[...] = diff * (1.0 / d)
        acc_ref[...] += jnp.sum(diff * diff, axis=0, keepdims=True)

        @pl.when(i == n - 1)
        def _():
            loss_ref[...] = jnp.sum(acc_ref[...], axis=-1, keepdims=True) * (0.5 / d)

    row = pl.BlockSpec((ROWS, d), lambda i: (i, 0))
    return pl.pallas_call(
        body, name=name, grid=(n,), in_specs=[row, row],
        out_specs=(pl.BlockSpec((1, 1), lambda i: (0, 0)), row),
        out_shape=(jax.ShapeDtypeStruct((1, 1), F32), jax.ShapeDtypeStruct((t, d), F32)),
        scratch_shapes=[pltpu.VMEM((1, d), F32)], compiler_params=_cparams(("arbitrary",)),
    )(y, target)


HALO = 32


def _conv_fwd(u, w, b, cdim, name):
    t = u.shape[0]
    ncb = cdim // LANES
    nt = t // BLOCK

    def body(a_ref, g_ref, w_ref, b_ref, zc_ref, zpad):
        zpad[0:HALO, :] = jnp.zeros((HALO, LANES), F32)
        zpad[HALO:HALO + t, :] = a_ref[...].astype(F32) * _sigmoid(g_ref[...].astype(F32))
        wv = w_ref[...]
        bv = b_ref[...]

        def tile(i, carry):
            r0 = pl.multiple_of(i * BLOCK, BLOCK)
            win = zpad[pl.ds(r0, BLOCK + HALO), :]
            acc = jnp.zeros((BLOCK, LANES), F32) + bv
            for j in range(CONV_WIDTH):
                s = CONV_WIDTH - 1 - j
                sh = win if s == 0 else pltpu.roll(win, s, 0)
                acc = acc + wv[j:j + 1, :] * sh[HALO:HALO + BLOCK, :]
            zc_ref[pl.ds(r0, BLOCK), :] = acc
            return carry

        lax.fori_loop(0, nt, tile, 0)

    col = lambda off: pl.BlockSpec((t, LANES), lambda c: (0, off + c))
    return pl.pallas_call(
        body, name=name, grid=(ncb,),
        in_specs=[col(0), col(ncb), pl.BlockSpec((CONV_TAPS_PADDED, LANES), lambda c: (0, c)),
                  pl.BlockSpec((1, LANES), lambda c: (0, c))],
        out_specs=pl.BlockSpec((t, LANES), lambda c: (0, c)),
        out_shape=jax.ShapeDtypeStruct((t, cdim), F32),
        scratch_shapes=[pltpu.VMEM((t + HALO, LANES), F32)], compiler_params=_cparams(("arbitrary",)),
    )(u, u, w, b.reshape(1, cdim))


def _conv_bwd(dzc, u, w, cdim, name):
    t = u.shape[0]
    ncb = cdim // LANES
    nt = t // BLOCK
    win_rows = BLOCK + HALO

    def body(dzc_ref, a_ref, g_ref, w_ref, da_ref, dg_ref, dw_ref, db_ref, zpad, dpad):
        av = a_ref[...].astype(F32)
        sg = _sigmoid(g_ref[...].astype(F32))
        zpad[0:HALO, :] = jnp.zeros((HALO, LANES), F32)
        zpad[HALO:HALO + t, :] = av * sg
        dpad[0:t, :] = dzc_ref[...]
        dpad[t:t + HALO, :] = jnp.zeros((HALO, LANES), F32)
        dw_ref[...] = jnp.zeros_like(dw_ref)
        db_ref[...] = jnp.sum(dzc_ref[...], axis=0, keepdims=True)
        wv = w_ref[...]

        def tile(i, carry):
            r0 = pl.multiple_of(i * BLOCK, BLOCK)
            zwin = zpad[pl.ds(r0, win_rows), :]
            dwin = dpad[pl.ds(r0, win_rows), :]
            dcur = dwin[0:BLOCK, :]
            dz = jnp.zeros((BLOCK, LANES), F32)
            for j in range(CONV_WIDTH):
                s = CONV_WIDTH - 1 - j
                zs = zwin if s == 0 else pltpu.roll(zwin, s, 0)
                dw_ref[j:j + 1, :] += jnp.sum(dcur * zs[HALO:HALO + BLOCK, :], axis=0, keepdims=True)
                ds = dwin if s == 0 else pltpu.roll(dwin, win_rows - s, 0)
                dz = dz + wv[j:j + 1, :] * ds[0:BLOCK, :]
            ac = a_ref[pl.ds(r0, BLOCK), :].astype(F32)
            sc = _sigmoid(g_ref[pl.ds(r0, BLOCK), :].astype(F32))
            da_ref[pl.ds(r0, BLOCK), :] = (dz * sc).astype(BF16)
            dg_ref[pl.ds(r0, BLOCK), :] = (dz * ac * sc * (1.0 - sc)).astype(BF16)
            return carry

        lax.fori_loop(0, nt, tile, 0)

    col = lambda off: pl.BlockSpec((t, LANES), lambda c: (0, off + c))
    wspec = pl.BlockSpec((CONV_TAPS_PADDED, LANES), lambda c: (0, c))
    o = jax.ShapeDtypeStruct((t, cdim), BF16)
    return pl.pallas_call(
        body, name=name, grid=(ncb,), in_specs=[col(0), col(0), col(ncb), wspec],
        out_specs=(col(0), col(0), wspec, pl.BlockSpec((1, LANES), lambda c: (0, c))),
        out_shape=(o, o, jax.ShapeDtypeStruct((CONV_TAPS_PADDED, cdim), F32), jax.ShapeDtypeStruct((1, cdim), F32)),
        scratch_shapes=[pltpu.VMEM((t + HALO, LANES), F32), pltpu.VMEM((t + HALO, LANES), F32)],
        compiler_params=_cparams(("arbitrary",)),
    )(dzc, u, u, w)


def _ln_swish_fwd(zc, g, b, name):
    t, c = zc.shape
    ROWS = _rows(t, 6 * c)

    def body(z_ref, g_ref, b_ref, o_ref):
        z = z_ref[...]
        mu = jnp.mean(z, axis=-1, keepdims=True)
        zc_ = z - mu
        zn = zc_ * lax.rsqrt(jnp.mean(zc_ * zc_, axis=-1, keepdims=True) + EPS)
        y = zn * g_ref[...] + b_ref[...]
        o_ref[...] = (y * _sigmoid(y)).astype(BF16)

    row = pl.BlockSpec((ROWS, c), lambda i: (i, 0))
    vec = pl.BlockSpec((1, c), lambda i: (0, 0))
    return pl.pallas_call(
        body, name=name, grid=(t // ROWS,), in_specs=[row, vec, vec], out_specs=row,
        out_shape=jax.ShapeDtypeStruct((t, c), BF16), compiler_params=_cparams(("arbitrary",)),
    )(zc, g.reshape(1, c), b.reshape(1, c))


def _ln_swish_bwd(dzs, zc, g, b, name):
    t, c = zc.shape
    ROWS = _rows(t, 12 * c)

    def body(d_ref, z_ref, g_ref, b_ref, dz_ref, dg_ref, db_ref):
        @pl.when(pl.program_id(0) == 0)
        def _():
            dg_ref[...] = jnp.zeros_like(dg_ref)
            db_ref[...] = jnp.zeros_like(db_ref)

        z = z_ref[...]
        mu = jnp.mean(z, axis=-1, keepdims=True)
        zc_ = z - mu
        rstd = lax.rsqrt(jnp.mean(zc_ * zc_, axis=-1, keepdims=True) + EPS)
        zn = zc_ * rstd
        y = zn * g_ref[...] + b_ref[...]
        sg = _sigmoid(y)
        dy = d_ref[...] * (sg * (1.0 + y * (1.0 - sg)))
        dg_ref[...] += jnp.sum(dy * zn, axis=0, keepdims=True)
        db_ref[...] += jnp.sum(dy, axis=0, keepdims=True)
        dzn = dy * g_ref[...]
        dz_ref[...] = rstd * (dzn - jnp.mean(dzn, axis=-1, keepdims=True)
                              - zn * jnp.mean(dzn * zn, axis=-1, keepdims=True))

    row = pl.BlockSpec((ROWS, c), lambda i: (i, 0))
    vec = pl.BlockSpec((1, c), lambda i: (0, 0))
    v = jax.ShapeDtypeStruct((1, c), F32)
    return pl.pallas_call(
        body, name=name, grid=(t // ROWS,), in_specs=[row, row, vec, vec], out_specs=(row, vec, vec),
        out_shape=(jax.ShapeDtypeStruct((t, c), F32), v, v), compiler_params=_cparams(("arbitrary",)),
    )(dzs, zc, g.reshape(1, c), b.reshape(1, c))


def _bucket_table():
    qi = np.arange(BLOCK)[:, None]
    kj = np.arange(2 * BLOCK)[None, :]
    off = qi + BLOCK - kj
    band = (off >= 0) & (off <= SUB_WINDOW)
    max_exact = NUM_BUCKETS // 2
    out = []
    for d in DILATIONS:
        dist = (np.clip(off, 0, SUB_WINDOW) * d).astype(np.int32)
        nf = np.maximum(dist, 1).astype(np.float32)
        large = max_exact + (np.log(nf / np.float32(max_exact)) / np.float32(math.log(MAX_REL_DISTANCE / max_exact))
                             * np.float32(NUM_BUCKETS - max_exact)).astype(np.int32)
        large = np.minimum(large, NUM_BUCKETS - 1)
        bucket = np.where(dist < max_exact, dist, large)
        out.append(np.where(band, bucket, -1))
    return np.stack(out).astype(np.int32)


def _bias_expand(rel_bias, buckets, hpg, name):
    nh = N_GROUPS * hpg

    def body(rb_ref, bk_ref, o_ref):
        h = pl.program_id(0)
        bk = bk_ref[0]
        acc = jnp.full((BLOCK, 2 * BLOCK), NEG_INF, F32)
        for bb in range(NUM_BUCKETS):
            acc = jnp.where(bk == bb, rb_ref[bb, h], acc)
        o_ref[0] = acc

    return pl.pallas_call(
        body, name=name, grid=(nh,),
        in_specs=[pl.BlockSpec(memory_space=pltpu.SMEM),
                  pl.BlockSpec((1, BLOCK, 2 * BLOCK), lambda h: (h // hpg, 0, 0))],
        out_specs=pl.BlockSpec((1, BLOCK, 2 * BLOCK), lambda h: (h, 0, 0)),
        out_shape=jax.ShapeDtypeStruct((nh, BLOCK, 2 * BLOCK), F32), compiler_params=_cparams(("arbitrary",)),
    )(rel_bias, buckets)


def _bias_reduce(ds_sum, buckets, hpg, name):
    nh = N_GROUPS * hpg

    def body(ds_ref, bk_ref, o_ref):
        bk = bk_ref[0]
        dsv = ds_ref[0]
        lane = lax.broadcasted_iota(jnp.int32, (1, LANES), 1)
        row = jnp.zeros((1, LANES), F32)
        for bb in range(NUM_BUCKETS):
            tot = jnp.sum(jnp.sum(jnp.where(bk == bb, dsv, 0.0), axis=-1, keepdims=True), axis=0, keepdims=True)
            row = jnp.where(lane == bb, tot, row)
        o_ref[0] = row

    return pl.pallas_call(
        body, name=name, grid=(nh,),
        in_specs=[pl.BlockSpec((1, BLOCK, 2 * BLOCK), lambda h: (h, 0, 0)),
                  pl.BlockSpec((1, BLOCK, 2 * BLOCK), lambda h: (h // hpg, 0, 0))],
        out_specs=pl.BlockSpec((1, 1, LANES), lambda h: (h, 0, 0)),
        out_shape=jax.ShapeDtypeStruct((nh, 1, LANES), F32), compiler_params=_cparams(("arbitrary",)),
    )(ds_sum, buckets)


def _chunk_rows(c, d, nb):
    r, n = c // nb, c % nb
    if d == 1:
        return pl.ds(c * BLOCK, BLOCK)
    return pl.ds(r + n * BLOCK * d, BLOCK, stride=d)


def _segment_ones():
    i = lax.broadcasted_iota(jnp.int32, (LANES, LANES), 0) // HEAD_DIM
    j = lax.broadcasted_iota(jnp.int32, (LANES, LANES), 1) // HEAD_DIM
    return (i == j).astype(BF16)


def _segment_sum(v, seg):
    hi = v.astype(BF16)
    lo = (v - hi.astype(F32)).astype(BF16)
    return jnp.dot(hi, seg, preferred_element_type=F32) + jnp.dot(lo, seg, preferred_element_type=F32)


def _head_mean(v, seg):
    return _segment_sum(v, seg) * (1.0 / HEAD_DIM)


def _attn_fwd(u, qg2, kg2, bias, gi, cols, hpg, name):
    t = u.shape[0]
    d = DILATIONS[gi]
    nchunk = t // BLOCK
    nb = (t // d) // BLOCK
    hp = hpg // 2
    qc0, kc0, vc0 = [(c + gi * hpg * HEAD_DIM) // LANES for c in cols]
    contract_lanes = (((1,), (1,)), ((), ()))

    def body(q_ref, k_ref, v_ref, qg_ref, kg_ref, bias_ref, o_ref, lse_ref, qd, kd, vd, od, ld, sbuf):
        seg = _segment_ones()
        lane = lax.broadcasted_iota(jnp.int32, (1, LANES), 1)
        qg = qg_ref[...] * (HEAD_DIM ** -0.5)
        kg = kg_ref[...]
        kd[0:BLOCK, :] = jnp.zeros((BLOCK, LANES), BF16)
        vd[0:BLOCK, :] = jnp.zeros((BLOCK, LANES), BF16)
        od[...] = q_ref[...].astype(F32)
        ld[...] = k_ref[...].astype(F32)
        for c in range(nchunk):
            rows = _chunk_rows(c, d, nb)
            qv = od[rows, :]
            kv = ld[rows, :]
            qd[c * BLOCK:(c + 1) * BLOCK, :] = (qv * lax.rsqrt(_head_mean(qv * qv, seg) + EPS) * qg).astype(BF16)
            kd[(c + 1) * BLOCK:(c + 2) * BLOCK, :] = (kv * lax.rsqrt(_head_mean(kv * kv, seg) + EPS) * kg).astype(BF16)
        od[...] = v_ref[...].astype(F32)
        for c in range(nchunk):
            vd[(c + 1) * BLOCK:(c + 2) * BLOCK, :] = od[_chunk_rows(c, d, nb), :].astype(BF16)

        col = lax.broadcasted_iota(jnp.int32, (BLOCK, 2 * BLOCK), 1)
        for j in range(2):
            mj = jnp.logical_and(lane >= j * HEAD_DIM, lane < (j + 1) * HEAD_DIM)
            for c in range(nchunk):
                kw = kd[c * BLOCK:(c + 2) * BLOCK, :]
                kj = jnp.where(mj, kw, jnp.zeros_like(kw))
                s = lax.dot_general(qd[c * BLOCK:(c + 1) * BLOCK, :], kj, contract_lanes,
                                    preferred_element_type=F32) + bias_ref[j]
                if c % nb == 0:
                    s = jnp.where(col < BLOCK, NEG_INF, s)
                sbuf[c] = s
            for c in range(nchunk):
                rows = slice(c * BLOCK, (c + 1) * BLOCK)
                s = sbuf[c]
                mx = jnp.max(s, axis=-1, keepdims=True)
                p = jnp.exp(s - mx).astype(BF16)
                vw = vd[c * BLOCK:(c + 2) * BLOCK, :]
                oj = jnp.dot(p, jnp.where(mj, vw, jnp.ones_like(vw)), preferred_element_type=F32)
                l = pltpu.roll(oj, HEAD_DIM, 1)
                on = oj / l
                ls = mx + jnp.log(l)
                if j == 0:
                    od[rows, :] = on
                    ld[rows, :] = ls
                else:
                    od[rows, :] = jnp.where(mj, on, od[rows, :])
                    ld[rows, :] = jnp.where(mj, ls, ld[rows, :])

        for c in range(nchunk):
            rows = _chunk_rows(c, d, nb)
            o_ref[rows, :] = od[c * BLOCK:(c + 1) * BLOCK, :]
            lse_ref[rows, :] = ld[c * BLOCK:(c + 1) * BLOCK, :]

    ucol = lambda c0: pl.BlockSpec((t, LANES), lambda h: (0, c0 + h))
    vec = pl.BlockSpec((1, LANES), lambda h: (0, 0))
    oblk = pl.BlockSpec((t, LANES), lambda h: (0, h))
    osh = jax.ShapeDtypeStruct((t, hpg * HEAD_DIM), F32)
    return pl.pallas_call(
        body, name=name, grid=(hp,),
        in_specs=[ucol(qc0), ucol(kc0), ucol(vc0), vec, vec,
                  pl.BlockSpec((2, BLOCK, 2 * BLOCK), lambda h: (gi * hp + h, 0, 0))],
        out_specs=(oblk, oblk), out_shape=(osh, osh),
        scratch_shapes=[pltpu.VMEM((t, LANES), BF16), pltpu.VMEM((t + BLOCK, LANES), BF16),
                        pltpu.VMEM((t + BLOCK, LANES), BF16), pltpu.VMEM((t, LANES), F32), pltpu.VMEM((t, LANES), F32),
                        pltpu.VMEM((nchunk, BLOCK, 2 * BLOCK), F32)],
        compiler_params=_cparams(("arbitrary",)),
    )(u, u, u, qg2, kg2, bias)


def _attn_bwd(u, do_g, dd_g, lse_g, qg2, kg2, bias, ds_in, gi, cols, hpg, name):
    t = u.shape[0]
    d = DILATIONS[gi]
    nchunk = t // BLOCK
    nb = (t // d) // BLOCK
    hp = hpg // 2
    qc0, kc0, vc0 = [(c + gi * hpg * HEAD_DIM) // LANES for c in cols]
    contract_lanes = (((1,), (1,)), ((), ()))
    contract_rows = (((0,), (0,)), ((), ()))
    qscale = HEAD_DIM ** -0.5

    def body(q_ref, k_ref, v_ref, do_ref, dd_ref, lse_ref, qg_ref, kg_ref, bias_ref, dsin_ref,
             dq_ref, dk_ref, dv_ref, dgq_ref, dgk_ref, dsout_ref,
             qd, kd, vd, dod, ddd, ld, dqd, dkd, dvd, dsacc, pbuf, dsbuf, qs, ks):
        seg = _segment_ones()
        lane = lax.broadcasted_iota(jnp.int32, (1, LANES), 1)
        qg = qg_ref[...] * qscale
        kg = kg_ref[...]
        kd[0:BLOCK, :] = jnp.zeros((BLOCK, LANES), BF16)
        vd[0:BLOCK, :] = jnp.zeros((BLOCK, LANES), BF16)
        dsacc[...] = jnp.zeros_like(dsacc)
        qs[...] = q_ref[...].astype(F32)
        ks[...] = k_ref[...].astype(F32)
        dqd[...] = v_ref[...].astype(F32)
        for c in range(nchunk):
            rows = _chunk_rows(c, d, nb)
            qv = qs[rows, :]
            kv = ks[rows, :]
            qd[c * BLOCK:(c + 1) * BLOCK, :] = (qv * lax.rsqrt(_head_mean(qv * qv, seg) + EPS) * qg).astype(BF16)
            kd[(c + 1) * BLOCK:(c + 2) * BLOCK, :] = (kv * lax.rsqrt(_head_mean(kv * kv, seg) + EPS) * kg).astype(BF16)
            vd[(c + 1) * BLOCK:(c + 2) * BLOCK, :] = dqd[rows, :].astype(BF16)
            dod[c * BLOCK:(c + 1) * BLOCK, :] = do_ref[rows, :].astype(BF16)
            ddd[c * BLOCK:(c + 1) * BLOCK, :] = dd_ref[rows, :]
            ld[c * BLOCK:(c + 1) * BLOCK, :] = lse_ref[rows, :]

        col = lax.broadcasted_iota(jnp.int32, (BLOCK, 2 * BLOCK), 1)
        for j in range(2):
            mj = jnp.logical_and(lane >= j * HEAD_DIM, lane < (j + 1) * HEAD_DIM)
            first = lane == j * HEAD_DIM
            for c in range(nchunk):
                rows = slice(c * BLOCK, (c + 1) * BLOCK)
                kw = kd[c * BLOCK:(c + 2) * BLOCK, :]
                vw = vd[c * BLOCK:(c + 2) * BLOCK, :]
                kj = jnp.where(mj, kw, jnp.zeros_like(kw))
                vj = jnp.where(mj, vw, jnp.zeros_like(vw))
                s = lax.dot_general(qd[rows, :], kj, contract_lanes, preferred_element_type=F32) + bias_ref[j]
                if c % nb == 0:
                    s = jnp.where(col < BLOCK, NEG_INF, s)
                dp = lax.dot_general(dod[rows, :], vj, contract_lanes, preferred_element_type=F32)
                lse_j = jnp.sum(jnp.where(first, ld[rows, :], 0.0), axis=-1, keepdims=True)
                dd_j = jnp.sum(jnp.where(first, ddd[rows, :], 0.0), axis=-1, keepdims=True)
                p = jnp.exp(s - lse_j)
                ds = p * (dp + dd_j)
                dsacc[j] += ds
                pbuf[j, c] = p.astype(BF16)
                dsbuf[j, c] = ds.astype(BF16)
        for c in range(nchunk):
            rows = slice(c * BLOCK, (c + 1) * BLOCK)
            nxt = slice((c + 1) * BLOCK, (c + 2) * BLOCK)
            has_next = c + 1 < nchunk and (c + 1) % nb != 0
            kw = kd[c * BLOCK:(c + 2) * BLOCK, :]
            dq = jnp.zeros((BLOCK, LANES), F32)
            dk = jnp.zeros((BLOCK, LANES), F32)
            dv = jnp.zeros((BLOCK, LANES), F32)
            for j in range(2):
                mj = jnp.logical_and(lane >= j * HEAD_DIM, lane < (j + 1) * HEAD_DIM)
                zero = jnp.zeros((BLOCK, LANES), BF16)
                dq = dq + jnp.dot(dsbuf[j, c], jnp.where(mj, kw, jnp.zeros_like(kw)), preferred_element_type=F32)
                dk = dk + lax.dot_general(dsbuf[j, c, :, BLOCK:], jnp.where(mj, qd[rows, :], zero), contract_rows,
                                          preferred_element_type=F32)
                dv = dv + lax.dot_general(pbuf[j, c, :, BLOCK:], jnp.where(mj, dod[rows, :], zero), contract_rows,
                                          preferred_element_type=F32)
                if has_next:
                    dk = dk + lax.dot_general(dsbuf[j, c + 1, :, :BLOCK], jnp.where(mj, qd[nxt, :], zero),
                                              contract_rows, preferred_element_type=F32)
                    dv = dv + lax.dot_general(pbuf[j, c + 1, :, :BLOCK], jnp.where(mj, dod[nxt, :], zero),
                                              contract_rows, preferred_element_type=F32)
            dqd[rows, :] = dq
            dkd[rows, :] = dk
            dvd[rows, :] = dv

        dsout_ref[...] = dsin_ref[...] + dsacc[...]

        dgq = jnp.zeros((1, LANES), F32)
        dgk = jnp.zeros((1, LANES), F32)
        for c in range(nchunk):
            rows = _chunk_rows(c, d, nb)
            qv = qs[rows, :]
            rq = lax.rsqrt(_head_mean(qv * qv, seg) + EPS)
            qh = qv * rq
            dy = dqd[c * BLOCK:(c + 1) * BLOCK, :]
            dgq = dgq + jnp.sum(dy * qh, axis=0, keepdims=True) * qscale
            dxh = dy * qg
            ddd[rows, :] = rq * (dxh - qh * _head_mean(dxh * qh, seg))
            kv = ks[rows, :]
            rk = lax.rsqrt(_head_mean(kv * kv, seg) + EPS)
            kh = kv * rk
            dy = dkd[c * BLOCK:(c + 1) * BLOCK, :]
            dgk = dgk + jnp.sum(dy * kh, axis=0, keepdims=True)
            dxh = dy * kg
            ld[rows, :] = rk * (dxh - kh * _head_mean(dxh * kh, seg))
        dq_ref[...] = ddd[...].astype(BF16)
        dk_ref[...] = ld[...].astype(BF16)
        for c in range(nchunk):
            ddd[_chunk_rows(c, d, nb), :] = dvd[c * BLOCK:(c + 1) * BLOCK, :]
        dv_ref[...] = ddd[...].astype(BF16)
        dgq_ref[0] = dgq
        dgk_ref[0] = dgk

    ucol = lambda c0: pl.BlockSpec((t, LANES), lambda h: (0, c0 + h))
    vec = pl.BlockSpec((1, LANES), lambda h: (0, 0))
    oblk = pl.BlockSpec((t, LANES), lambda h: (0, h))
    bblk = pl.BlockSpec((2, BLOCK, 2 * BLOCK), lambda h: (gi * hp + h, 0, 0))
    gblk = pl.BlockSpec((1, 1, LANES), lambda h: (h, 0, 0))
    osh = jax.ShapeDtypeStruct((t, hpg * HEAD_DIM), BF16)
    gsh = jax.ShapeDtypeStruct((hp, 1, LANES), F32)
    return pl.pallas_call(
        body, name=name, grid=(hp,),
        in_specs=[ucol(qc0), ucol(kc0), ucol(vc0), oblk, oblk, oblk, vec, vec, bblk, bblk],
        out_specs=(oblk, oblk, oblk, gblk, gblk, bblk),
        out_shape=(osh, osh, osh, gsh, gsh, jax.ShapeDtypeStruct(ds_in.shape, F32)),
        input_output_aliases={9: 5},
        scratch_shapes=[pltpu.VMEM((t, LANES), BF16), pltpu.VMEM((t + BLOCK, LANES), BF16),
                        pltpu.VMEM((t + BLOCK, LANES), BF16), pltpu.VMEM((t, LANES), BF16),
                        pltpu.VMEM((t, LANES), F32), pltpu.VMEM((t, LANES), F32), pltpu.VMEM((t, LANES), F32),
                        pltpu.VMEM((t, LANES), F32), pltpu.VMEM((t, LANES), F32),
                        pltpu.VMEM((2, BLOCK, 2 * BLOCK), F32), pltpu.VMEM((2, nchunk, BLOCK, 2 * BLOCK), BF16),
                        pltpu.VMEM((2, nchunk, BLOCK, 2 * BLOCK), BF16), pltpu.VMEM((t, LANES), F32),
                        pltpu.VMEM((t, LANES), F32)],
        compiler_params=_cparams(("arbitrary",)),
    )(u, u, u, do_g, dd_g, lse_g, qg2, kg2, bias, ds_in)


def _group_weights(l0, l1, l2):
    mx = jnp.maximum(jnp.maximum(l0, l1), l2)
    e0, e1, e2 = jnp.exp(l0 - mx), jnp.exp(l1 - mx), jnp.exp(l2 - mx)
    inv = 1.0 / (e0 + e1 + e2)
    return e0 * inv, e1 * inv, e2 * inv


def _combine_fwd(os_, lses, name):
    t, ao = os_[0].shape
    ROWS = _rows(t, 26 * ao)

    def body(o0, o1, o2, l0, l1, l2, o_ref):
        w0, w1, w2 = _group_weights(l0[...], l1[...], l2[...])
        o_ref[...] = (w0 * o0[...] + w1 * o1[...] + w2 * o2[...]).astype(BF16)

    row = pl.BlockSpec((ROWS, ao), lambda i: (i, 0))
    return pl.pallas_call(
        body, name=name, grid=(t // ROWS,), in_specs=[row] * 6, out_specs=row,
        out_shape=jax.ShapeDtypeStruct((t, ao), BF16), compiler_params=_cparams(("arbitrary",)),
    )(*os_, *lses)


def _combine_bwd(do, os_, lses, name):
    t, ao = do.shape
    idx = np.arange(ao) // HEAD_DIM
    seg = jnp.asarray((idx[:, None] == idx[None, :]).astype(np.float32), dtype=BF16)
    ROWS = _rows(t, 52 * ao)

    def body(do_ref, o0, o1, o2, l0, l1, l2, seg_ref, g0, g1, g2, d0, d1, d2):
        w0, w1, w2 = _group_weights(l0[...], l1[...], l2[...])
        dov = do_ref[...]
        o = w0 * o0[...] + w1 * o1[...] + w2 * o2[...]
        sd = _segment_sum(dov * o, seg_ref[...])
        for w, gref, dref in ((w0, g0, d0), (w1, g1, d1), (w2, g2, d2)):
            gref[...] = w * dov
            dref[...] = -(w * sd)

    row = pl.BlockSpec((ROWS, ao), lambda i: (i, 0))
    sh = jax.ShapeDtypeStruct((t, ao), F32)
    outs = pl.pallas_call(
        body, name=name, grid=(t // ROWS,), in_specs=[row] * 7 + [pl.BlockSpec((ao, ao), lambda i: (0, 0))],
        out_specs=(row,) * 6, out_shape=(sh,) * 6, compiler_params=_cparams(("arbitrary",)),
    )(do, *os_, *lses, seg)
    return outs[:3], outs[3:]


def _adamw(w, g, m, v, name):
    shape = w.shape
    cols = shape[-1]
    rows = int(np.prod(shape[:-1]))
    tr = rows if rows <= 512 else _tile_rows(rows)
    c1 = 1.0 - ADAM_B1 ** ADAM_STEP
    c2 = 1.0 - ADAM_B2 ** ADAM_STEP

    def body(w_ref, g_ref, m_ref, v_ref, d_ref, nm_ref, nv_ref):
        gv = g_ref[...]
        mn = ADAM_B1 * m_ref[...] + (1.0 - ADAM_B1) * gv
        vn = ADAM_B2 * v_ref[...] + (1.0 - ADAM_B2) * (gv * gv)
        nm_ref[...] = mn
        nv_ref[...] = vn
        d_ref[...] = -ADAM_LR * ((mn / c1) / (jnp.sqrt(vn / c2) + ADAM_EPS) + ADAM_WD * w_ref[...])

    blk = pl.BlockSpec((tr, cols), lambda i: (i, 0))
    sh = jax.ShapeDtypeStruct((rows, cols), F32)
    outs = pl.pallas_call(
        body, name=name, grid=(rows // tr,), in_specs=[blk] * 4, out_specs=(blk,) * 3, out_shape=(sh,) * 3,
        compiler_params=_cparams(("arbitrary",)),
    )(*[a.reshape(rows, cols) for a in (w, g, m, v)])
    return tuple(o.reshape(shape) for o in outs)


def _tile_rows(rows):
    for t in (512, 256, 128, 64, 32, 16, 8):
        if rows % t == 0:
            return t
    return rows


def _sum_slots(recv, parts, me, name):
    _, rows, cols = recv.shape
    tr = rows if rows <= 512 else _tile_rows(rows)

    def body(me_ref, r_ref, own_ref, o_ref):
        acc = jnp.zeros(o_ref.shape, F32)
        for s in range(N_DEV):
            acc = acc + jnp.where(me_ref[0] == s, own_ref[...], r_ref[s]).astype(F32)
        o_ref[...] = acc

    return pl.pallas_call(
        body, name=name,
        grid_spec=pltpu.PrefetchScalarGridSpec(
            num_scalar_prefetch=1, grid=(rows // tr,),
            in_specs=[pl.BlockSpec((N_DEV, tr, cols), lambda i, me: (0, i, 0)),
                      pl.BlockSpec((None, tr, cols), lambda i, me: (me[0], i, 0))],
            out_specs=pl.BlockSpec((tr, cols), lambda i, me: (i, 0))),
        out_shape=jax.ShapeDtypeStruct((rows, cols), F32), compiler_params=_cparams(("arbitrary",)),
    )(me.reshape(1), recv, parts)


def _peer(k):
    x, y, c = lax.axis_index("x"), lax.axis_index("y"), lax.axis_index("c")
    return (1 - x if k & 4 else x, 1 - y if k & 2 else y, 1 - c if k & 1 else c)


def _dev_index(p):
    return 4 * p[0] + 2 * p[1] + p[2]


HBM_SPEC = pl.BlockSpec(memory_space=pltpu.HBM)
SEM_SPEC = pl.BlockSpec(memory_space=pltpu.SEMAPHORE)
ANY_SPEC = pl.BlockSpec(memory_space=pl.ANY)
CHIPS = (4, 2, 6)


def _remote(src, dst, send_sem, recv_sem, to):
    return pltpu.make_async_remote_copy(src_ref=src, dst_ref=dst, send_sem=send_sem, recv_sem=recv_sem,
                                        device_id=to, device_id_type=MESH)


def _hbm(a):
    return pltpu.with_memory_space_constraint(a, pltpu.HBM)


def _split_call(body, name, bufs, sems_in, sem_out_sizes, after):
    nb, ns, no = len(bufs), len(sems_in), len(sem_out_sizes)
    extra = [] if after is None else [after]

    def kern(*refs):
        pos = nb + ns + len(extra)
        body(refs[:nb], refs[nb:nb + ns], refs[pos:pos + no])
        token_ref = refs[pos + no + nb]
        token_ref[...] = jnp.zeros_like(token_ref)

    out_shape = (tuple(pltpu.SemaphoreType.DMA((s,)) for s in sem_out_sizes)
                 + tuple(pltpu.HBM(b.shape, b.dtype) for b in bufs) + (jax.ShapeDtypeStruct((8, LANES), F32),))
    res = pl.pallas_call(
        kern, name=name, out_shape=out_shape,
        in_specs=[HBM_SPEC] * nb + [SEM_SPEC] * ns + [ANY_SPEC] * len(extra),
        out_specs=(SEM_SPEC,) * no + (HBM_SPEC,) * nb + (pl.BlockSpec(memory_space=pltpu.VMEM),),
        input_output_aliases={i: no + i for i in range(nb)},
        compiler_params=pltpu.CompilerParams(has_side_effects=pltpu.SideEffectType.DATAFLOW_SIDE_EFFECTING),
    )(*bufs, *sems_in, *extra)
    return res[:no], res[no:no + nb], res[no + nb]


def _gather_start(shards, lands, after, name):
    n = len(shards)

    def body(bufs, _, sems):
        ins, lnd = bufs[:n], bufs[n:]
        d2d_s, d2d_r, ici_s, ici_r = sems
        me = _dev_index(_peer(0))
        for j, k in enumerate(CHIPS):
            for i in range(n):
                _remote(ins[i], lnd[i].at[me], ici_s.at[j], ici_r.at[j], _peer(k)).start()
        for i in range(n):
            _remote(ins[i], lnd[i].at[me], d2d_s.at[0], d2d_r.at[0], _peer(1)).start()

    return _split_call(body, name, [_hbm(a) for a in (*shards, *lands)], [], (1, 1, 3, 3), after)


def _gather_forward(n, bufs, ici_r, after, name):
    def body(refs, sems_in, sems):
        ins, lnd = refs[:n], refs[n:]
        (arrived,) = sems_in
        fwd_s, fwd_r = sems
        for j, k in enumerate(CHIPS):
            blk = _dev_index(_peer(k))
            for i in range(n):
                _remote(ins[i], lnd[i].at[blk], fwd_s.at[j], arrived.at[j], _peer(k)).wait_recv()
            for i in range(n):
                _remote(lnd[i].at[blk], lnd[i].at[blk], fwd_s.at[j], fwd_r.at[j], _peer(1)).start()

    return _split_call(body, name, bufs, [ici_r], (3, 3), after)


def _gather_finish(n, bufs, d2d_s, d2d_r, ici_s, fwd_s, fwd_r, after, name):
    def body(refs, sems_in, _):
        ins, lnd = refs[:n], refs[n:]
        d2d_send, d2d_recv, ici_send, fwd_send, fwd_recv = sems_in
        sib = _peer(1)
        for i in range(n):
            cp = _remote(ins[i], lnd[i].at[_dev_index(sib)], d2d_send.at[0], d2d_recv.at[0], sib)
            cp.wait_send()
            cp.wait_recv()
        for j, k in enumerate(CHIPS):
            passed = _dev_index(_peer(k))
            landed = _dev_index(_peer(k | 1))
            for i in range(n):
                _remote(ins[i], lnd[i].at[passed], ici_send.at[j], fwd_recv.at[j], _peer(k)).wait_send()
                cp = _remote(lnd[i].at[passed], lnd[i].at[landed], fwd_send.at[j], fwd_recv.at[j], sib)
                cp.wait_send()
                cp.wait_recv()

    _, out, token = _split_call(body, name, bufs, [d2d_s, d2d_r, ici_s, fwd_s, fwd_r], (), after)
    return out[n:], token


def _exchange_start(parts, lands, after, name):
    n = len(parts)

    def body(bufs, _, sems):
        src, lnd = bufs[:n], bufs[n:]
        send, recv = sems
        me = _dev_index(_peer(0))
        for k in (4, 5, 2, 3, 6, 7, 1):
            to = _peer(k)
            for i in range(n):
                _remote(src[i].at[_dev_index(to)], lnd[i].at[me], send.at[k - 1], recv.at[k - 1], to).start()

    return _split_call(body, name, [_hbm(a) for a in (*parts, *lands)], [], (7, 7), after)


def _exchange_finish(n, bufs, send, recv, after, name):
    def body(refs, sems_in, _):
        src, lnd = refs[:n], refs[n:]
        send_, recv_ = sems_in
        me = _dev_index(_peer(0))
        for k in range(1, N_DEV):
            frm = _peer(k)
            for i in range(n):
                cp = _remote(src[i].at[me], lnd[i].at[_dev_index(frm)], send_.at[k - 1], recv_.at[k - 1], frm)
                cp.wait_send()
                cp.wait_recv()

    _, out, token = _split_call(body, name, bufs, [send, recv], (), after)
    return out[:n], out[n:], token


def _all_reduce_small(v, name):
    rows = v.shape[0]

    def body(v_ref, o_ref, buf, send_sems, recv_sems):
        me = _dev_index(_peer(0))
        buf[me] = v_ref[...]
        copies = []
        for k in range(1, N_DEV):
            copies.append(pltpu.make_async_remote_copy(
                src_ref=v_ref, dst_ref=buf.at[me], send_sem=send_sems.at[k - 1], recv_sem=recv_sems.at[k - 1],
                device_id=_peer(k), device_id_type=MESH))
        for cp in copies:
            cp.start()
        for k in range(1, N_DEV):
            pltpu.make_async_remote_copy(
                src_ref=v_ref, dst_ref=buf.at[_dev_index(_peer(k))], send_sem=send_sems.at[k - 1],
                recv_sem=recv_sems.at[k - 1], device_id=_peer(k), device_id_type=MESH).wait_recv()
        for cp in copies:
            cp.wait_send()
        acc = buf[0]
        for s in range(1, N_DEV):
            acc = acc + buf[s]
        o_ref[...] = acc

    vm = pl.BlockSpec(memory_space=pltpu.VMEM)
    return pl.pallas_call(
        body, name=name, in_specs=[vm], out_specs=vm, out_shape=jax.ShapeDtypeStruct(v.shape, F32),
        scratch_shapes=[pltpu.VMEM((N_DEV, rows, LANES), F32), pltpu.SemaphoreType.DMA((7,)),
                        pltpu.SemaphoreType.DMA((7,))],
    )(v)


def _columns(cdim, ao):
    q_col = 2 * cdim
    attn_dim = N_GROUPS * ao
    return (q_col, q_col + attn_dim, q_col + 2 * attn_dim), q_col + 3 * attn_dim


def _layer_fwd(x, sm, bg, get_rest, bias, hpg):
    cdim = sm["conv_ln_g"].shape[0]
    ao = hpg * HEAD_DIM
    cols, gate_col = _columns(cdim, ao)
    qg2 = jnp.tile(sm["q_norm_g"], 2).reshape(1, LANES)
    kg2 = jnp.tile(sm["k_norm_g"], 2).reshape(1, LANES)
    h1 = _rms_fwd(x, sm["norm1_g"], "rms1_fwd")
    u = _mm_in_pairs(h1, bg["w_in"], "mm_in")
    zc = _conv_fwd(u, bg["conv_dw_w"], sm["conv_dw_b"], cdim, "conv_fwd")
    zs = _ln_swish_fwd(zc, sm["conv_ln_g"], sm["conv_ln_b"], "ln_swish_fwd")
    os_, lses = [], []
    for gi in range(N_GROUPS):
        o_g, lse_g = _attn_fwd(u, qg2, kg2, bias, gi, cols, hpg, "attn_fwd_g%d" % gi)
        os_.append(o_g)
        lses.append(lse_g)
    o = _combine_fwd(os_, lses, "combine_fwd")
    bg = {**bg, **get_rest(o)}
    yc = _mm(zs, bg["w_conv_out"], name="mm_conv_out")
    ya = _mm(o, bg["w_attn_out"], name="mm_attn_out")
    mg = _gate_fwd(u, yc, ya, gate_col, "gate_fwd")
    x1 = _mm(mg, bg["w_out"], epi="res", extra=x, name="mm_out")
    h2 = _rms_fwd(x1, sm["norm2_g"], "rms2_fwd")
    f = _mm(h2, bg["w_ff1"], out_dtype=BF16, name="mm_ff1")
    x2 = _mm(f, bg["w_ff2"], a_relu2=True, epi="res", extra=x1, name="mm_ff2")
    saved = dict(x=x, h1=h1, u=u, zc=zc, zs=zs, yc=yc, os=os_, lses=lses, o=o, ya=ya, mg=mg, x1=x1, h2=h2, f=f,
                 qg2=qg2, kg2=kg2)
    return x2, saved, bg


GRAD_GROUPS = (("w_ff2", "w_ff1"), ("w_out", "w_conv_out", "w_attn_out", "conv_dw_w"), ("w_in",))


def _layer_bwd(dx, s, sm, bg, bias, ds_sum, after, emit):
    cdim = sm["conv_ln_g"].shape[0]
    ao = bg["w_attn_out"].shape[0]
    hpg = ao // HEAD_DIM
    hp = hpg // 2
    cols, gate_col = _columns(cdim, ao)
    g = {}
    df = _mm(dx, bg["w_ff2"], tb=True, epi="drelu2", extra=s["f"], out_dtype=BF16, after=after, name="mm_dff2")
    g["w_ff2"] = _mm(s["f"], dx, ta=True, a_relu2=True, out_dtype=BF16, name="mm_gw_ff2")
    g["w_ff1"] = _mm(s["h2"], df, ta=True, out_dtype=BF16, out_slots=True, name="mm_gw_ff1")
    after = emit(GRAD_GROUPS[0], g)
    dh2 = _mm(df, bg["w_ff1"], tb=True, after=after, name="mm_dff1")
    dx1, dg2 = _rms_bwd(dh2, s["x1"], sm["norm2_g"], dx, "rms2_bwd")
    g["norm2_g"] = dg2[0]
    dmg = _mm(dx1, bg["w_out"], tb=True, name="mm_dout")
    g["w_out"] = _mm(s["mg"], dx1, ta=True, out_dtype=BF16, name="mm_gw_out")
    dyc, dya, dugc, duga = _gate_bwd(dmg, s["u"], s["yc"], s["ya"], gate_col, "gate_bwd")
    dzs = _mm(dyc, bg["w_conv_out"], tb=True, name="mm_dconv_out")
    g["w_conv_out"] = _mm(s["zs"], dyc, ta=True, out_dtype=BF16, name="mm_gw_conv_out")
    do = _mm(dya, bg["w_attn_out"], tb=True, name="mm_dattn_out")
    g["w_attn_out"] = _mm(s["o"], dya, ta=True, out_dtype=BF16, name="mm_gw_attn_out")
    dzc, dlg, dlb = _ln_swish_bwd(dzs, s["zc"], sm["conv_ln_g"], sm["conv_ln_b"], "ln_swish_bwd")
    g["conv_ln_g"] = dlg[0]
    g["conv_ln_b"] = dlb[0]
    da, dgt, dcw, dcb = _conv_bwd(dzc, s["u"], bg["conv_dw_w"], cdim, "conv_bwd")
    g["conv_dw_w"] = dcw[:CONV_WIDTH].astype(BF16)
    g["conv_dw_b"] = dcb[0]
    after = emit(GRAD_GROUPS[1], g)
    do_gs, dd_gs = _combine_bwd(do, s["os"], s["lses"], "combine_bwd")
    dqs, dks, dvs = [], [], []
    dgq = jnp.zeros((HEAD_DIM,), F32)
    dgk = jnp.zeros((HEAD_DIM,), F32)
    for gi in range(N_GROUPS):
        dq, dk, dv, gq, gk, ds_sum = _attn_bwd(s["u"], do_gs[gi], dd_gs[gi], s["lses"][gi], s["qg2"], s["kg2"],
                                               bias, ds_sum, gi, cols, hpg, "attn_bwd_g%d" % gi)
        dqs.append(dq)
        dks.append(dk)
        dvs.append(dv)
        dgq = dgq + jnp.sum(gq.reshape(hp * 2, HEAD_DIM), axis=0)
        dgk = dgk + jnp.sum(gk.reshape(hp * 2, HEAD_DIM), axis=0)
    g["q_norm_g"] = dgq
    g["k_norm_g"] = dgk
    du = jnp.concatenate([da, dgt] + dqs + dks + dvs + [dugc, duga], axis=1)
    g["w_in"] = _mm_gw_in_pairs(s["h1"], du, after, "mm_gw_in")
    after = emit(GRAD_GROUPS[2], g)
    dh1 = _mm_din_pairs(du, bg["w_in"], after, "mm_din")
    dx0, dg1 = _rms_bwd(dh1, s["x"], sm["norm1_g"], dx1, "rms1_bwd")
    g["norm1_g"] = dg1[0]
    return dx0, g, ds_sum


BIG = ("w_in", "conv_dw_w", "w_conv_out", "w_attn_out", "w_out", "w_ff1", "w_ff2")
COL_SHARDED = ("w_in", "conv_dw_w", "w_conv_out", "w_attn_out", "w_ff1")
SMALL = ("rel_bias", "norm1_g", "q_norm_g", "k_norm_g", "conv_dw_b", "conv_ln_g", "conv_ln_b", "norm2_g")
WEIGHTS = ("rel_bias", "norm1_g", "w_in", "q_norm_g", "k_norm_g", "conv_dw_w", "conv_dw_b", "conv_ln_g", "conv_ln_b",
           "w_conv_out", "w_attn_out", "w_out", "norm2_g", "w_ff1", "w_ff2")


def _to_whole(name, gathered):
    n, a, b = gathered.shape
    if name in COL_SHARDED:
        return gathered.transpose(1, 0, 2).reshape(a, n * b)
    return gathered.reshape(n * a, b)


def _to_slots(name, whole):
    a, b = whole.shape
    if name in COL_SHARDED:
        return whole.reshape(a, N_DEV, b // N_DEV).transpose(1, 0, 2)
    return whole.reshape(N_DEV, a // N_DEV, b)


def _own_slot(block, me):
    land = lax.empty((N_DEV,) + block.shape, block.dtype)
    return lax.dynamic_update_slice(land, block[None], (me,) + (0,) * block.ndim)


def kernel(x, rel_bias, norm1_g, w_in, q_norm_g, k_norm_g, conv_dw_w, conv_dw_b, conv_ln_g, conv_ln_b, w_conv_out, w_attn_out, w_out, norm2_g, w_ff1, w_ff2, loss_target, m_rel_bias, m_norm1_g, m_w_in, m_q_norm_g, m_k_norm_g, m_conv_dw_w, m_conv_dw_b, m_conv_ln_g, m_conv_ln_b, m_w_conv_out, m_w_attn_out, m_w_out, m_norm2_g, m_w_ff1, m_w_ff2, v_rel_bias, v_norm1_g, v_w_in, v_q_norm_g, v_k_norm_g, v_conv_dw_w, v_conv_dw_b, v_conv_ln_g, v_conv_ln_b, v_w_conv_out, v_w_attn_out, v_w_out, v_norm2_g, v_w_ff1, v_w_ff2):
    w = dict(rel_bias=rel_bias, norm1_g=norm1_g, w_in=w_in, q_norm_g=q_norm_g, k_norm_g=k_norm_g, conv_dw_w=conv_dw_w,
             conv_dw_b=conv_dw_b, conv_ln_g=conv_ln_g, conv_ln_b=conv_ln_b, w_conv_out=w_conv_out,
             w_attn_out=w_attn_out, w_out=w_out, norm2_g=norm2_g, w_ff1=w_ff1, w_ff2=w_ff2)
    mom = dict(rel_bias=m_rel_bias, norm1_g=m_norm1_g, w_in=m_w_in, q_norm_g=m_q_norm_g, k_norm_g=m_k_norm_g,
               conv_dw_w=m_conv_dw_w, conv_dw_b=m_conv_dw_b, conv_ln_g=m_conv_ln_g, conv_ln_b=m_conv_ln_b,
               w_conv_out=m_w_conv_out, w_attn_out=m_w_attn_out, w_out=m_w_out, norm2_g=m_norm2_g, w_ff1=m_w_ff1,
               w_ff2=m_w_ff2)
    var = dict(rel_bias=v_rel_bias, norm1_g=v_norm1_g, w_in=v_w_in, q_norm_g=v_q_norm_g, k_norm_g=v_k_norm_g,
               conv_dw_w=v_conv_dw_w, conv_dw_b=v_conv_dw_b, conv_ln_g=v_conv_ln_g, conv_ln_b=v_conv_ln_b,
               w_conv_out=v_w_conv_out, w_attn_out=v_w_attn_out, w_out=v_w_out, norm2_g=v_norm2_g, w_ff1=v_w_ff1,
               w_ff2=v_w_ff2)

    depth = norm1_g.shape[0]
    me = 4 * lax.axis_index("x") + 2 * lax.axis_index("y") + lax.axis_index("c")
    odd_core = lax.axis_index("c") == 1
    hpg = w_attn_out.shape[1] // HEAD_DIM
    buckets = jnp.asarray(_bucket_table())
    bias = _bias_expand(rel_bias, buckets, hpg, "bias_expand")

    first_names = ("w_in", "conv_dw_w")
    rest_names = tuple(k for k in BIG if k not in first_names)

    def chain_start(l, names, after):
        shards = [w[k][l] if k == "conv_dw_w" else w[k][l].astype(BF16) for k in names]
        if "w_in" in names:
            i = names.index("w_in")
            shards[i] = jnp.where(odd_core, jnp.pad(shards[i], ((0, 0), (SHIFT, 0))),
                                  jnp.pad(shards[i], ((0, 0), (0, SHIFT))))
        sems, bufs, token = _gather_start(shards, [_own_slot(s, me) for s in shards], after,
                                          "gather_start_%s_l%d" % (names[0], l))
        return dict(l=l, names=names, sems=sems, bufs=bufs, token=token)

    def chain_forward(ch, after):
        fwd, bufs, token = _gather_forward(len(ch["names"]), ch["bufs"], ch["sems"][3], after,
                                           "gather_forward_%s_l%d" % (ch["names"][0], ch["l"]))
        ch.update(fwd=fwd, bufs=bufs)
        return token

    def chain_finish(ch, after):
        d2d_s, d2d_r, ici_s, _ = ch["sems"]
        gathered, _ = _gather_finish(len(ch["names"]), ch["bufs"], d2d_s, d2d_r, ici_s, ch["fwd"][0], ch["fwd"][1],
                                     after, "gather_finish_%s_l%d" % (ch["names"][0], ch["l"]))
        out = {k: a if k == "w_in" else _to_whole(k, a) for k, a in zip(ch["names"], gathered)}
        if "conv_dw_w" in out:
            out["conv_dw_w"] = jnp.pad(out["conv_dw_w"], ((0, CONV_TAPS_PADDED - CONV_WIDTH), (0, 0)))
        return out

    xs = x[0]
    saved, bigs, smalls = [], [], []
    chains = {}
    for l in range(depth):
        sm = {k: w[k][l] for k in SMALL if k != "rel_bias"}
        if l == 0:
            first = chain_start(0, first_names, None)
            token = chain_forward(first, None)
            rest = chain_start(0, rest_names, token)
            bg = chain_finish(first, rest["token"])

            def get_rest(o, rest=rest):
                token = chain_forward(rest, o)
                if depth > 1:
                    chains[1] = chain_start(1, BIG, token)
                    token = chains[1]["token"]
                return chain_finish(rest, token)
        else:
            token = chain_forward(chains[l], xs)
            if l + 1 < depth:
                chains[l + 1] = chain_start(l + 1, BIG, token)
                token = chains[l + 1]["token"]
            whole = chain_finish(chains[l], token)
            bg = {k: whole[k] for k in first_names}

            def get_rest(o, whole=whole):
                return {k: whole[k] for k in rest_names}
        xs, sv, bg = _layer_fwd(xs, sm, bg, get_rest, bias, hpg)
        saved.append(sv)
        bigs.append(bg)
        smalls.append(sm)

    loss, dx = _loss_and_grad(xs, loss_target[0], "loss")

    ds_sum = jnp.zeros((N_GROUPS * hpg, BLOCK, 2 * BLOCK), F32)
    g = {k: [None] * depth for k in SMALL if k != "rel_bias"}
    grad_layers = {k: [None] * depth for k in BIG}
    pending = []

    def finish_oldest(after):
        names, l, (send, recv), bufs = pending.pop(0)
        parts, recvd, token = _exchange_finish(len(names), bufs, send, recv, after,
                                               "exchange_finish_%s_l%d" % (names[0], l))
        for k, r, p in zip(names, recvd, parts):
            shp = w[k].shape[1:]
            three = (N_DEV, -1, r.shape[-1])
            total = _sum_slots(r.reshape(three), p.reshape(three), me, "sum_" + k)
            if k == "w_in":
                total = jnp.where(odd_core, total[:, SHIFT:], total[:, :shp[-1]])
            grad_layers[k][l] = total.reshape(shp)
        return token

    def make_emit(l):
        def emit(names, gl):
            parts = [gl[k] if k in ("w_ff1", "w_in") else _to_slots(k, gl[k]) for k in names]
            token = finish_oldest(parts[0]) if len(pending) >= len(GRAD_GROUPS) else None
            lands = [lax.empty(p.shape, p.dtype) for p in parts]
            sems, bufs, token = _exchange_start(parts, lands, token, "exchange_start_%s_l%d" % (names[0], l))
            pending.append((names, l, sems, bufs))
            return token
        return emit

    token = None
    for l in reversed(range(depth)):
        dx, gl, ds_sum = _layer_bwd(dx, saved[l], smalls[l], bigs[l], bias, ds_sum, token, make_emit(l))
        for k in g:
            g[k][l] = gl[k]
        token = None
    while pending:
        finish_oldest(None)
    grad_x = dx

    g = {k: jnp.stack(v) for k, v in g.items()}
    db = _bias_reduce(ds_sum, buckets, hpg, "bias_reduce")
    g["rel_bias"] = db[:, 0, :NUM_BUCKETS].T

    flat = jnp.concatenate([g[k].reshape(-1) for k in SMALL])
    nflat = flat.shape[0]
    rows = -(-nflat // (8 * LANES)) * 8
    packed = jnp.pad(flat, (0, rows * LANES - nflat)).reshape(rows, LANES)
    total = _all_reduce_small(packed, "reduce_small").reshape(-1)
    grad = {}
    off = 0
    for k in SMALL:
        size = int(np.prod(w[k].shape))
        grad[k] = total[off:off + size].reshape(w[k].shape)
        off += size

    for k in BIG:
        grad[k] = jnp.stack(grad_layers[k])

    loss = lax.psum(loss[0, 0], ("x", "y", "c"))
    outs = {k: _adamw(w[k], grad[k], mom[k], var[k], "adamw_" + k) for k in WEIGHTS}
    return (loss, grad_x[None], *[grad[k] for k in WEIGHTS], *[outs[k][0] for k in WEIGHTS],
            *[outs[k][1] for k in WEIGHTS], *[outs[k][2] for k in WEIGHTS])
```

```python
import functools
import math

import numpy as np
import jax
import jax.numpy as jnp
from jax import lax
from jax.experimental import pallas as pl
from jax.experimental.pallas import tpu as pltpu

F32 = jnp.float32
BF16 = jnp.bfloat16

HEAD_DIM = 64
N_GROUPS = 3
DILATIONS = (1, 4, 16)
SUB_WINDOW = 128
BLOCK = 128
CONV_WIDTH = 31
CONV_TAPS_PADDED = 32
NUM_BUCKETS = 32
MAX_REL_DISTANCE = 2048
EPS = 1e-6
NEG_INF = -1e30
LANES = 128

ADAM_LR = 0.001
ADAM_B1 = 0.9
ADAM_B2 = 0.999
ADAM_EPS = 1e-08
ADAM_WD = 0.01
ADAM_STEP = 10

N_DEV = 8
VMEM_LIMIT = 56 * 1024 * 1024
MESH = pl.DeviceIdType.MESH


def _cparams(sem=None):
    return pltpu.CompilerParams(dimension_semantics=sem, vmem_limit_bytes=VMEM_LIMIT)


def _tile(n, target):
    if n <= target:
        return n
    t = (target // LANES) * LANES
    while t >= LANES:
        if n % t == 0:
            return t
        t -= LANES
    return n


def _sigmoid(v):
    return 1.0 / (1.0 + jnp.exp(-v))


MM_VMEM_BUDGET = 40 * 1024 * 1024


def _rms_apply(x, g):
    return x * lax.rsqrt(jnp.mean(x * x, axis=-1, keepdims=True) + EPS) * g


def _rms_grad(dh, x, g):
    r = lax.rsqrt(jnp.mean(x * x, axis=-1, keepdims=True) + EPS)
    xh = x * r
    dxh = dh * g
    dx = r * (dxh - xh * jnp.mean(dxh * xh, axis=-1, keepdims=True))
    return dx, jnp.sum(dh * xh, axis=0, keepdims=True)


def _mm_tiles(m, n, kdim, a_bytes, b_bytes, io_bytes, whole_rows=False, temps=2):
    def need(tm, tn, tk):
        blocks = 2 * (tm * tk * a_bytes + tk * tn * b_bytes + tm * tn * io_bytes)
        casts = (tm * tk * 2 if a_bytes == 4 else 0) + (tk * tn * 2 if b_bytes == 4 else 0)
        return blocks + casts + temps * tm * tn * 4

    tn = n if whole_rows else _tile(n, 1024)
    while True:
        fits = [(tm * tk, tm, tk) for tm in {_tile(m, c) for c in (1024, 512, 256, 128)}
                for tk in {_tile(kdim, c) for c in (2048, 1024, 512, 256)} if need(tm, tn, tk) <= MM_VMEM_BUDGET]
        if fits:
            _, tm, tk = max(fits)
            return tm, tn, tk
        assert not whole_rows and tn % 256 == 0, "no block size fits the VMEM budget"
        tn //= 2


def _mm(a, b, *, ta=False, tb=False, out_dtype=F32, epi=None, extra=(), gain=None, after=None, out_slots=False,
        a_relu2=False, name):
    extra = tuple(extra) if isinstance(extra, (tuple, list)) else (extra,)
    m = a.shape[1] if ta else a.shape[0]
    kdim = a.shape[0] if ta else a.shape[1]
    n = b.shape[0] if tb else b.shape[1]
    norm = epi in ("res_rms", "rms_bwd")
    io_bytes = (jnp.dtype(out_dtype).itemsize + sum(e.dtype.itemsize for e in extra) + (2 if epi == "res_rms" else 0))
    tm, tn, tk = _mm_tiles(m, n // N_DEV if out_slots else n, kdim, a.dtype.itemsize, b.dtype.itemsize, io_bytes,
                           whole_rows=norm, temps=6 if norm else 2)
    if out_slots:
        assert epi is None and tn == n // N_DEV
    nk = kdim // tk
    a_spec = pl.BlockSpec((tk, tm), lambda i, j, k: (k, i)) if ta else pl.BlockSpec((tm, tk), lambda i, j, k: (i, k))
    b_spec = pl.BlockSpec((tn, tk), lambda i, j, k: (j, k)) if tb else pl.BlockSpec((tk, tn), lambda i, j, k: (k, j))
    o_spec = (pl.BlockSpec((None, tm, tn), lambda i, j, k: (j, i, 0)) if out_slots
              else pl.BlockSpec((tm, tn), lambda i, j, k: (i, j)))
    v_spec = pl.BlockSpec((1, tn), lambda i, j, k: (0, j))
    dims = (((0 if ta else 1,), (1 if tb else 0,)), ((), ()))
    n_extra = len(extra)
    n_in = 2 + n_extra + (gain is not None) + (after is not None)
    n_out = 2 if norm else 1

    def body(*refs):
        a_ref, b_ref = refs[0], refs[1]
        e_refs = refs[2:2 + n_extra]
        g_ref = refs[2 + n_extra] if gain is not None else None
        outs = refs[n_in:n_in + n_out]

        def product():
            av = a_ref[...]
            if a_relu2:
                r = jnp.maximum(av.astype(F32), 0.0)
                av = r * r
            return lax.dot_general(av.astype(BF16), b_ref[...].astype(BF16), dims, preferred_element_type=F32)

        def finish(acc):
            if epi is None:
                outs[0][...] = acc.astype(outs[0].dtype)
            elif epi == "res":
                outs[0][...] = (e_refs[0][...] + acc).astype(outs[0].dtype)
            elif epi == "drelu2":
                outs[0][...] = (acc * (2.0 * jnp.maximum(e_refs[0][...].astype(F32), 0.0))).astype(outs[0].dtype)
            elif epi == "res_rms":
                x1 = e_refs[0][...] + acc
                outs[0][...] = x1
                outs[1][...] = _rms_apply(x1, g_ref[...]).astype(BF16)
            elif epi == "rms_bwd":
                dx, dg = _rms_grad(acc, e_refs[0][...], g_ref[...])
                outs[0][...] = e_refs[1][...] + dx
                i = pl.program_id(0)

                @pl.when(i == 0)
                def _():
                    outs[1][...] = dg

                @pl.when(i > 0)
                def _():
                    outs[1][...] += dg

        if nk == 1:
            finish(product())
            return
        acc_ref = refs[-1]
        k = pl.program_id(2)

        @pl.when(k == 0)
        def _():
            acc_ref[...] = product()

        @pl.when(jnp.logical_and(k > 0, k < nk - 1))
        def _():
            acc_ref[...] += product()

        @pl.when(k == nk - 1)
        def _():
            finish(acc_ref[...] + product())

    in_specs = ([a_spec, b_spec] + [o_spec] * n_extra + ([v_spec] if gain is not None else [])
                + ([pl.BlockSpec(memory_space=pl.ANY)] if after is not None else []))
    if epi == "res_rms":
        out_shape = (jax.ShapeDtypeStruct((m, n), F32), jax.ShapeDtypeStruct((m, n), BF16))
        out_specs = (o_spec, o_spec)
    elif epi == "rms_bwd":
        out_shape = (jax.ShapeDtypeStruct((m, n), F32), jax.ShapeDtypeStruct((1, n), F32))
        out_specs = (o_spec, v_spec)
    else:
        out_shape = jax.ShapeDtypeStruct((N_DEV, m, tn) if out_slots else (m, n), out_dtype)
        out_specs = o_spec
    args = (a, b) + extra + ((gain.reshape(1, n),) if gain is not None else ()) + ((after,) if after is not None else ())
    return pl.pallas_call(
        body, name=name, grid=(m // tm, n // tn, nk), in_specs=in_specs, out_specs=out_specs, out_shape=out_shape,
        scratch_shapes=[pltpu.VMEM((tm, tn), F32)] if nk > 1 else [],
        compiler_params=_cparams(("arbitrary", "arbitrary", "arbitrary")),
    )(*args)


SHIFT = HEAD_DIM


def _pair_blocks(e, o):
    wp = e.shape[-1]
    return e[:, :wp - LANES], e[:, wp - LANES:] + o[:, :LANES], o[:, LANES:]


def _mm_in_pairs(a, wg, name):
    t, kdim = a.shape
    wp = wg.shape[-1]
    ws = wp - SHIFT
    tm = _tile(t, 1024)

    def body(a_ref, e_ref, o_ref, u_ref):
        av = a_ref[...]
        lo, mid, hi = _pair_blocks(e_ref[...], o_ref[...])
        u_ref[:, :wp - LANES] = jnp.dot(av, lo, preferred_element_type=F32).astype(BF16)
        u_ref[:, wp - LANES:wp] = jnp.dot(av, mid, preferred_element_type=F32).astype(BF16)
        u_ref[:, wp:] = jnp.dot(av, hi, preferred_element_type=F32).astype(BF16)

    return pl.pallas_call(
        body, name=name, grid=(N_DEV // 2, t // tm),
        in_specs=[pl.BlockSpec((tm, kdim), lambda p, i: (i, 0)),
                  pl.BlockSpec((None, kdim, wp), lambda p, i: (2 * p, 0, 0)),
                  pl.BlockSpec((None, kdim, wp), lambda p, i: (2 * p + 1, 0, 0))],
        out_specs=pl.BlockSpec((tm, 2 * ws), lambda p, i: (i, p)),
        out_shape=jax.ShapeDtypeStruct((t, N_DEV * ws), BF16), compiler_params=_cparams(("arbitrary", "arbitrary")),
    )(a, wg, wg)


def _mm_din_pairs(du, wg, x, gain, dres, after, name):
    t = du.shape[0]
    _, kdim, wp = wg.shape
    ws = wp - SHIFT
    tm = _tile(t, 512)
    npair = N_DEV // 2
    lanes = (((1,), (1,)), ((), ()))
    extra = [] if after is None else [after]

    def body(d_ref, e_ref, o_ref, x_ref, g_ref, r_ref, *rest):
        dx_ref, dg_ref, acc_ref = rest[-3], rest[-2], rest[-1]
        i, p = pl.program_id(0), pl.program_id(1)
        lo, mid, hi = _pair_blocks(e_ref[...], o_ref[...])
        part = (lax.dot_general(d_ref[:, :wp - LANES], lo, lanes, preferred_element_type=F32)
                + lax.dot_general(d_ref[:, wp - LANES:wp], mid, lanes, preferred_element_type=F32)
                + lax.dot_general(d_ref[:, wp:], hi, lanes, preferred_element_type=F32))

        @pl.when(p == 0)
        def _():
            acc_ref[...] = part

        @pl.when(jnp.logical_and(p > 0, p < npair - 1))
        def _():
            acc_ref[...] += part

        @pl.when(p == npair - 1)
        def _():
            dx, dg = _rms_grad(acc_ref[...] + part, x_ref[...], g_ref[...])
            dx_ref[...] = r_ref[...] + dx

            @pl.when(i == 0)
            def _():
                dg_ref[...] = dg

            @pl.when(i > 0)
            def _():
                dg_ref[...] += dg

    row = pl.BlockSpec((tm, kdim), lambda i, p: (i, 0))
    vec = pl.BlockSpec((1, kdim), lambda i, p: (0, 0))
    return pl.pallas_call(
        body, name=name, grid=(t // tm, npair),
        in_specs=[pl.BlockSpec((tm, 2 * ws), lambda i, p: (i, p)),
                  pl.BlockSpec((None, kdim, wp), lambda i, p: (2 * p, 0, 0)),
                  pl.BlockSpec((None, kdim, wp), lambda i, p: (2 * p + 1, 0, 0)), row, vec, row]
        + [pl.BlockSpec(memory_space=pl.ANY)] * len(extra),
        out_specs=(row, vec),
        out_shape=(jax.ShapeDtypeStruct((t, kdim), F32), jax.ShapeDtypeStruct((1, kdim), F32)),
        scratch_shapes=[pltpu.VMEM((tm, kdim), F32)], compiler_params=_cparams(("arbitrary", "arbitrary")),
    )(du, wg, wg, x, gain.reshape(1, kdim), dres, *extra)


def _mm_gw_in_pairs(h, du, after, name):
    t, kdim = h.shape
    ws = du.shape[1] // N_DEV
    wp = ws + SHIFT
    tm = _tile(kdim, 512)
    rows = (((0,), (0,)), ((), ()))
    extra = [] if after is None else [after]

    def body(h_ref, d_ref, *rest):
        g_ref = rest[-1]
        g = lax.dot_general(h_ref[...], d_ref[...], rows, preferred_element_type=F32)
        g_ref[0] = g[:, :wp].astype(BF16)
        g_ref[1] = g[:, wp - LANES:].astype(BF16)

    return pl.pallas_call(
        body, name=name, grid=(N_DEV // 2, kdim // tm),
        in_specs=[pl.BlockSpec((t, tm), lambda p, i: (0, i)), pl.BlockSpec((t, 2 * ws), lambda p, i: (0, p))]
        + [pl.BlockSpec(memory_space=pl.ANY)] * len(extra),
        out_specs=pl.BlockSpec((2, tm, wp), lambda p, i: (p, i, 0)),
        out_shape=jax.ShapeDtypeStruct((N_DEV, kdim, wp), BF16), compiler_params=_cparams(("arbitrary", "arbitrary")),
    )(h, du, *extra)


ROW_BLOCK_BUDGET = 24 * 1024 * 1024


def _rows(t, row_bytes):
    rows = t
    while rows > 8 and (2 * rows * row_bytes > ROW_BLOCK_BUDGET or t % rows):
        rows //= 2
    return rows


def _rms_fwd(x, g, name):
    t, d = x.shape
    ROWS = _rows(t, 6 * d)

    def body(x_ref, g_ref, h_ref):
        h_ref[...] = _rms_apply(x_ref[...], g_ref[...]).astype(BF16)

    return pl.pallas_call(
        body, name=name, grid=(t // ROWS,),
        in_specs=[pl.BlockSpec((ROWS, d), lambda i: (i, 0)), pl.BlockSpec((1, d), lambda i: (0, 0))],
        out_specs=pl.BlockSpec((ROWS, d), lambda i: (i, 0)),
        out_shape=jax.ShapeDtypeStruct((t, d), BF16), compiler_params=_cparams(("arbitrary",)),
    )(x, g.reshape(1, d))


def _gate_fwd(u, yc, ya, gate_col, name):
    t, d = yc.shape
    td = math.gcd(_tile(d, 512), gate_col)
    nd = d // td
    c0 = gate_col // td
    ROWS = _rows(t, 14 * td)

    def body(gc_ref, ga_ref, yc_ref, ya_ref, m_ref):
        gc = _sigmoid(gc_ref[...].astype(F32))
        ga = _sigmoid(ga_ref[...].astype(F32))
        m_ref[...] = (gc * yc_ref[...] + ga * ya_ref[...]).astype(BF16)

    blk = pl.BlockSpec((ROWS, td), lambda i, j: (i, j))
    return pl.pallas_call(
        body, name=name, grid=(t // ROWS, nd),
        in_specs=[pl.BlockSpec((ROWS, td), lambda i, j: (i, c0 + j)),
                  pl.BlockSpec((ROWS, td), lambda i, j: (i, c0 + nd + j)), blk, blk],
        out_specs=blk, out_shape=jax.ShapeDtypeStruct((t, d), BF16),
        compiler_params=_cparams(("arbitrary", "arbitrary")),
    )(u, u, yc, ya)


def _gate_bwd(dm, u, yc, ya, gate_col, name):
    t, d = yc.shape
    td = math.gcd(_tile(d, 512), gate_col)
    nd = d // td
    c0 = gate_col // td
    ROWS = _rows(t, 24 * td)

    def body(dm_ref, gc_ref, ga_ref, yc_ref, ya_ref, dyc_ref, dya_ref, dugc_ref, duga_ref):
        dmv = dm_ref[...]
        gc = _sigmoid(gc_ref[...].astype(F32))
        ga = _sigmoid(ga_ref[...].astype(F32))
        dyc_ref[...] = (dmv * gc).astype(BF16)
        dya_ref[...] = (dmv * ga).astype(BF16)
        dugc_ref[...] = (dmv * yc_ref[...] * gc * (1.0 - gc)).astype(BF16)
        duga_ref[...] = (dmv * ya_ref[...] * ga * (1.0 - ga)).astype(BF16)

    blk = pl.BlockSpec((ROWS, td), lambda i, j: (i, j))
    o = jax.ShapeDtypeStruct((t, d), BF16)
    return pl.pallas_call(
        body, name=name, grid=(t // ROWS, nd),
        in_specs=[blk, pl.BlockSpec((ROWS, td), lambda i, j: (i, c0 + j)),
                  pl.BlockSpec((ROWS, td), lambda i, j: (i, c0 + nd + j)), blk, blk],
        out_specs=(blk, blk, blk, blk), out_shape=(o, o, o, o),
        compiler_params=_cparams(("arbitrary", "arbitrary")),
    )(dm, u, u, yc, ya)


def _loss_and_grad(y, target, name):
    t, d = y.shape
    ROWS = _rows(t, 12 * d)
    n = t // ROWS

    def body(y_ref, t_ref, loss_ref, dy_ref, acc_ref):
        i = pl.program_id(0)

        @pl.when(i == 0)
        def _():
            acc_ref[...] = jnp.zeros_like(acc_ref)

        diff = y_ref[...] - t_ref[...]
        dy_ref[...] = diff * (1.0 / d)
        acc_ref[...] += jnp.sum(diff * diff, axis=0, keepdims=True)

        @pl.when(i == n - 1)
        def _():
            loss_ref[...] = jnp.sum(acc_ref[...], axis=-1, keepdims=True) * (0.5 / d)

    row = pl.BlockSpec((ROWS, d), lambda i: (i, 0))
    return pl.pallas_call(
        body, name=name, grid=(n,), in_specs=[row, row],
        out_specs=(pl.BlockSpec((1, 1), lambda i: (0, 0)), row),
        out_shape=(jax.ShapeDtypeStruct((1, 1), F32), jax.ShapeDtypeStruct((t, d), F32)),
        scratch_shapes=[pltpu.VMEM((1, d), F32)], compiler_params=_cparams(("arbitrary",)),
    )(y, target)


HALO = 32


def _conv_fwd(u, w, b, cdim, name):
    t = u.shape[0]
    ncb = cdim // LANES
    nt = t // BLOCK

    def body(a_ref, g_ref, w_ref, b_ref, zc_ref, zpad):
        zpad[0:HALO, :] = jnp.zeros((HALO, LANES), F32)
        zpad[HALO:HALO + t, :] = a_ref[...].astype(F32) * _sigmoid(g_ref[...].astype(F32))
        wv = w_ref[...]
        bv = b_ref[...]

        def tile(i, carry):
            r0 = pl.multiple_of(i * BLOCK, BLOCK)
            win = zpad[pl.ds(r0, BLOCK + HALO), :]
            acc = jnp.zeros((BLOCK, LANES), F32) + bv
            for j in range(CONV_WIDTH):
                s = CONV_WIDTH - 1 - j
                sh = win if s == 0 else pltpu.roll(win, s, 0)
                acc = acc + wv[j:j + 1, :] * sh[HALO:HALO + BLOCK, :]
            zc_ref[pl.ds(r0, BLOCK), :] = acc
            return carry

        lax.fori_loop(0, nt, tile, 0)

    col = lambda off: pl.BlockSpec((t, LANES), lambda c: (0, off + c))
    return pl.pallas_call(
        body, name=name, grid=(ncb,),
        in_specs=[col(0), col(ncb), pl.BlockSpec((CONV_TAPS_PADDED, LANES), lambda c: (0, c)),
                  pl.BlockSpec((1, LANES), lambda c: (0, c))],
        out_specs=pl.BlockSpec((t, LANES), lambda c: (0, c)),
        out_shape=jax.ShapeDtypeStruct((t, cdim), F32),
        scratch_shapes=[pltpu.VMEM((t + HALO, LANES), F32)], compiler_params=_cparams(("arbitrary",)),
    )(u, u, w, b.reshape(1, cdim))


def _conv_bwd(dzc, u, w, cdim, name):
    t = u.shape[0]
    ncb = cdim // LANES
    nt = t // BLOCK
    win_rows = BLOCK + HALO

    def body(dzc_ref, a_ref, g_ref, w_ref, da_ref, dg_ref, dw_ref, db_ref, zpad, dpad):
        av = a_ref[...].astype(F32)
        sg = _sigmoid(g_ref[...].astype(F32))
        zpad[0:HALO, :] = jnp.zeros((HALO, LANES), F32)
        zpad[HALO:HALO + t, :] = av * sg
        dpad[0:t, :] = dzc_ref[...]
        dpad[t:t + HALO, :] = jnp.zeros((HALO, LANES), F32)
        dw_ref[...] = jnp.zeros_like(dw_ref)
        db_ref[...] = jnp.sum(dzc_ref[...], axis=0, keepdims=True)
        wv = w_ref[...]

        def tile(i, carry):
            r0 = pl.multiple_of(i * BLOCK, BLOCK)
            zwin = zpad[pl.ds(r0, win_rows), :]
            dwin = dpad[pl.ds(r0, win_rows), :]
            dcur = dwin[0:BLOCK, :]
            dz = jnp.zeros((BLOCK, LANES), F32)
            for j in range(CONV_WIDTH):
                s = CONV_WIDTH - 1 - j
                zs = zwin if s == 0 else pltpu.roll(zwin, s, 0)
                dw_ref[j:j + 1, :] += jnp.sum(dcur * zs[HALO:HALO + BLOCK, :], axis=0, keepdims=True)
                ds = dwin if s == 0 else pltpu.roll(dwin, win_rows - s, 0)
                dz = dz + wv[j:j + 1, :] * ds[0:BLOCK, :]
            ac = a_ref[pl.ds(r0, BLOCK), :].astype(F32)
            sc = _sigmoid(g_ref[pl.ds(r0, BLOCK), :].astype(F32))
            da_ref[pl.ds(r0, BLOCK), :] = (dz * sc).astype(BF16)
            dg_ref[pl.ds(r0, BLOCK), :] = (dz * ac * sc * (1.0 - sc)).astype(BF16)
            return carry

        lax.fori_loop(0, nt, tile, 0)

    col = lambda off: pl.BlockSpec((t, LANES), lambda c: (0, off + c))
    wspec = pl.BlockSpec((CONV_TAPS_PADDED, LANES), lambda c: (0, c))
    o = jax.ShapeDtypeStruct((t, cdim), BF16)
    return pl.pallas_call(
        body, name=name, grid=(ncb,), in_specs=[col(0), col(0), col(ncb), wspec],
        out_specs=(col(0), col(0), wspec, pl.BlockSpec((1, LANES), lambda c: (0, c))),
        out_shape=(o, o, jax.ShapeDtypeStruct((CONV_TAPS_PADDED, cdim), F32), jax.ShapeDtypeStruct((1, cdim), F32)),
        scratch_shapes=[pltpu.VMEM((t + HALO, LANES), F32), pltpu.VMEM((t + HALO, LANES), F32)],
        compiler_params=_cparams(("arbitrary",)),
    )(dzc, u, u, w)


def _ln_swish_fwd(zc, g, b, name):
    t, c = zc.shape
    ROWS = _rows(t, 6 * c)

    def body(z_ref, g_ref, b_ref, o_ref):
        z = z_ref[...]
        mu = jnp.mean(z, axis=-1, keepdims=True)
        zc_ = z - mu
        zn = zc_ * lax.rsqrt(jnp.mean(zc_ * zc_, axis=-1, keepdims=True) + EPS)
        y = zn * g_ref[...] + b_ref[...]
        o_ref[...] = (y * _sigmoid(y)).astype(BF16)

    row = pl.BlockSpec((ROWS, c), lambda i: (i, 0))
    vec = pl.BlockSpec((1, c), lambda i: (0, 0))
    return pl.pallas_call(
        body, name=name, grid=(t // ROWS,), in_specs=[row, vec, vec], out_specs=row,
        out_shape=jax.ShapeDtypeStruct((t, c), BF16), compiler_params=_cparams(("arbitrary",)),
    )(zc, g.reshape(1, c), b.reshape(1, c))


def _ln_swish_bwd(dzs, zc, g, b, name):
    t, c = zc.shape
    ROWS = _rows(t, 12 * c)

    def body(d_ref, z_ref, g_ref, b_ref, dz_ref, dg_ref, db_ref):
        @pl.when(pl.program_id(0) == 0)
        def _():
            dg_ref[...] = jnp.zeros_like(dg_ref)
            db_ref[...] = jnp.zeros_like(db_ref)

        z = z_ref[...]
        mu = jnp.mean(z, axis=-1, keepdims=True)
        zc_ = z - mu
        rstd = lax.rsqrt(jnp.mean(zc_ * zc_, axis=-1, keepdims=True) + EPS)
        zn = zc_ * rstd
        y = zn * g_ref[...] + b_ref[...]
        sg = _sigmoid(y)
        dy = d_ref[...] * (sg * (1.0 + y * (1.0 - sg)))
        dg_ref[...] += jnp.sum(dy * zn, axis=0, keepdims=True)
        db_ref[...] += jnp.sum(dy, axis=0, keepdims=True)
        dzn = dy * g_ref[...]
        dz_ref[...] = rstd * (dzn - jnp.mean(dzn, axis=-1, keepdims=True)
                              - zn * jnp.mean(dzn * zn, axis=-1, keepdims=True))

    row = pl.BlockSpec((ROWS, c), lambda i: (i, 0))
    vec = pl.BlockSpec((1, c), lambda i: (0, 0))
    v = jax.ShapeDtypeStruct((1, c), F32)
    return pl.pallas_call(
        body, name=name, grid=(t // ROWS,), in_specs=[row, row, vec, vec], out_specs=(row, vec, vec),
        out_shape=(jax.ShapeDtypeStruct((t, c), F32), v, v), compiler_params=_cparams(("arbitrary",)),
    )(dzs, zc, g.reshape(1, c), b.reshape(1, c))


def _bucket_table():
    qi = np.arange(BLOCK)[:, None]
    kj = np.arange(2 * BLOCK)[None, :]
    off = qi + BLOCK - kj
    band = (off >= 0) & (off <= SUB_WINDOW)
    max_exact = NUM_BUCKETS // 2
    out = []
    for d in DILATIONS:
        dist = (np.clip(off, 0, SUB_WINDOW) * d).astype(np.int32)
        nf = np.maximum(dist, 1).astype(np.float32)
        large = max_exact + (np.log(nf / np.float32(max_exact)) / np.float32(math.log(MAX_REL_DISTANCE / max_exact))
                             * np.float32(NUM_BUCKETS - max_exact)).astype(np.int32)
        large = np.minimum(large, NUM_BUCKETS - 1)
        bucket = np.where(dist < max_exact, dist, large)
        out.append(np.where(band, bucket, -1))
    return np.stack(out).astype(np.int32)


def _bias_expand(rel_bias, buckets, hpg, name):
    nh = N_GROUPS * hpg

    def body(rb_ref, bk_ref, o_ref):
        h = pl.program_id(0)
        bk = bk_ref[0]
        acc = jnp.full((BLOCK, 2 * BLOCK), NEG_INF, F32)
        for bb in range(NUM_BUCKETS):
            acc = jnp.where(bk == bb, rb_ref[bb, h], acc)
        o_ref[0] = acc

    return pl.pallas_call(
        body, name=name, grid=(nh,),
        in_specs=[pl.BlockSpec(memory_space=pltpu.SMEM),
                  pl.BlockSpec((1, BLOCK, 2 * BLOCK), lambda h: (h // hpg, 0, 0))],
        out_specs=pl.BlockSpec((1, BLOCK, 2 * BLOCK), lambda h: (h, 0, 0)),
        out_shape=jax.ShapeDtypeStruct((nh, BLOCK, 2 * BLOCK), F32), compiler_params=_cparams(("arbitrary",)),
    )(rel_bias, buckets)


def _bias_reduce(ds_sum, buckets, hpg, name):
    nh = N_GROUPS * hpg

    def body(ds_ref, bk_ref, o_ref):
        bk = bk_ref[0]
        dsv = ds_ref[0]
        lane = lax.broadcasted_iota(jnp.int32, (1, LANES), 1)
        row = jnp.zeros((1, LANES), F32)
        for bb in range(NUM_BUCKETS):
            tot = jnp.sum(jnp.sum(jnp.where(bk == bb, dsv, 0.0), axis=-1, keepdims=True), axis=0, keepdims=True)
            row = jnp.where(lane == bb, tot, row)
        o_ref[0] = row

    return pl.pallas_call(
        body, name=name, grid=(nh,),
        in_specs=[pl.BlockSpec((1, BLOCK, 2 * BLOCK), lambda h: (h, 0, 0)),
                  pl.BlockSpec((1, BLOCK, 2 * BLOCK), lambda h: (h // hpg, 0, 0))],
        out_specs=pl.BlockSpec((1, 1, LANES), lambda h: (h, 0, 0)),
        out_shape=jax.ShapeDtypeStruct((nh, 1, LANES), F32), compiler_params=_cparams(("arbitrary",)),
    )(ds_sum, buckets)


def _chunk_rows(c, d, nb):
    r, n = c // nb, c % nb
    if d == 1:
        return pl.ds(c * BLOCK, BLOCK)
    return pl.ds(r + n * BLOCK * d, BLOCK, stride=d)


def _segment_ones():
    i = lax.broadcasted_iota(jnp.int32, (LANES, LANES), 0) // HEAD_DIM
    j = lax.broadcasted_iota(jnp.int32, (LANES, LANES), 1) // HEAD_DIM
    return (i == j).astype(BF16)


def _segment_sum(v, seg):
    hi = v.astype(BF16)
    lo = (v - hi.astype(F32)).astype(BF16)
    return jnp.dot(hi, seg, preferred_element_type=F32) + jnp.dot(lo, seg, preferred_element_type=F32)


def _head_mean(v, seg):
    return _segment_sum(v, seg) * (1.0 / HEAD_DIM)


def _attn_fwd(u, qg2, kg2, bias, gi, cols, hpg, name):
    t = u.shape[0]
    d = DILATIONS[gi]
    nchunk = t // BLOCK
    nb = (t // d) // BLOCK
    hp = hpg // 2
    qc0, kc0, vc0 = [(c + gi * hpg * HEAD_DIM) // LANES for c in cols]
    contract_lanes = (((1,), (1,)), ((), ()))

    def body(q_ref, k_ref, v_ref, qg_ref, kg_ref, bias_ref, o_ref, lse_ref, qd, kd, vd, od, ld, sbuf):
        seg = _segment_ones()
        lane = lax.broadcasted_iota(jnp.int32, (1, LANES), 1)
        qg = qg_ref[...] * (HEAD_DIM ** -0.5)
        kg = kg_ref[...]
        kd[0:BLOCK, :] = jnp.zeros((BLOCK, LANES), BF16)
        vd[0:BLOCK, :] = jnp.zeros((BLOCK, LANES), BF16)
        od[...] = q_ref[...].astype(F32)
        ld[...] = k_ref[...].astype(F32)
        for c in range(nchunk):
            rows = _chunk_rows(c, d, nb)
            qv = od[rows, :]
            kv = ld[rows, :]
            qd[c * BLOCK:(c + 1) * BLOCK, :] = (qv * lax.rsqrt(_head_mean(qv * qv, seg) + EPS) * qg).astype(BF16)
            kd[(c + 1) * BLOCK:(c + 2) * BLOCK, :] = (kv * lax.rsqrt(_head_mean(kv * kv, seg) + EPS) * kg).astype(BF16)
        od[...] = v_ref[...].astype(F32)
        for c in range(nchunk):
            vd[(c + 1) * BLOCK:(c + 2) * BLOCK, :] = od[_chunk_rows(c, d, nb), :].astype(BF16)

        col = lax.broadcasted_iota(jnp.int32, (BLOCK, 2 * BLOCK), 1)
        for j in range(2):
            mj = jnp.logical_and(lane >= j * HEAD_DIM, lane < (j + 1) * HEAD_DIM)
            for c in range(nchunk):
                kw = kd[c * BLOCK:(c + 2) * BLOCK, :]
                kj = jnp.where(mj, kw, jnp.zeros_like(kw))
                s = lax.dot_general(qd[c * BLOCK:(c + 1) * BLOCK, :], kj, contract_lanes,
                                    preferred_element_type=F32) + bias_ref[j]
                if c % nb == 0:
                    s = jnp.where(col < BLOCK, NEG_INF, s)
                sbuf[c] = s
            for c in range(nchunk):
                rows = slice(c * BLOCK, (c + 1) * BLOCK)
                s = sbuf[c]
                mx = jnp.max(s, axis=-1, keepdims=True)
                p = jnp.exp(s - mx).astype(BF16)
                vw = vd[c * BLOCK:(c + 2) * BLOCK, :]
                oj = jnp.dot(p, jnp.where(mj, vw, jnp.ones_like(vw)), preferred_element_type=F32)
                l = pltpu.roll(oj, HEAD_DIM, 1)
                on = oj / l
                ls = mx + jnp.log(l)
                if j == 0:
                    od[rows, :] = on
                    ld[rows, :] = ls
                else:
                    od[rows, :] = jnp.where(mj, on, od[rows, :])
                    ld[rows, :] = jnp.where(mj, ls, ld[rows, :])

        for c in range(nchunk):
            rows = _chunk_rows(c, d, nb)
            o_ref[rows, :] = od[c * BLOCK:(c + 1) * BLOCK, :]
            lse_ref[rows, :] = ld[c * BLOCK:(c + 1) * BLOCK, :]

    ucol = lambda c0: pl.BlockSpec((t, LANES), lambda h: (0, c0 + h))
    vec = pl.BlockSpec((1, LANES), lambda h: (0, 0))
    oblk = pl.BlockSpec((t, LANES), lambda h: (0, h))
    osh = jax.ShapeDtypeStruct((t, hpg * HEAD_DIM), F32)
    return pl.pallas_call(
        body, name=name, grid=(hp,),
        in_specs=[ucol(qc0), ucol(kc0), ucol(vc0), vec, vec,
                  pl.BlockSpec((2, BLOCK, 2 * BLOCK), lambda h: (gi * hp + h, 0, 0))],
        out_specs=(oblk, oblk), out_shape=(osh, osh),
        scratch_shapes=[pltpu.VMEM((t, LANES), BF16), pltpu.VMEM((t + BLOCK, LANES), BF16),
                        pltpu.VMEM((t + BLOCK, LANES), BF16), pltpu.VMEM((t, LANES), F32), pltpu.VMEM((t, LANES), F32),
                        pltpu.VMEM((nchunk, BLOCK, 2 * BLOCK), F32)],
        compiler_params=_cparams(("arbitrary",)),
    )(u, u, u, qg2, kg2, bias)


def _attn_bwd(u, do_g, dd_g, lse_g, qg2, kg2, bias, ds_in, gi, cols, hpg, name):
    t = u.shape[0]
    d = DILATIONS[gi]
    nchunk = t // BLOCK
    nb = (t // d) // BLOCK
    hp = hpg // 2
    qc0, kc0, vc0 = [(c + gi * hpg * HEAD_DIM) // LANES for c in cols]
    contract_lanes = (((1,), (1,)), ((), ()))
    contract_rows = (((0,), (0,)), ((), ()))
    qscale = HEAD_DIM ** -0.5

    def body(q_ref, k_ref, v_ref, do_ref, dd_ref, lse_ref, qg_ref, kg_ref, bias_ref, dsin_ref,
             dq_ref, dk_ref, dv_ref, dgq_ref, dgk_ref, dsout_ref,
             qd, kd, vd, dod, ddd, ld, dqd, dkd, dvd, dsacc, pbuf, dsbuf, qs, ks):
        seg = _segment_ones()
        lane = lax.broadcasted_iota(jnp.int32, (1, LANES), 1)
        qg = qg_ref[...] * qscale
        kg = kg_ref[...]
        kd[0:BLOCK, :] = jnp.zeros((BLOCK, LANES), BF16)
        vd[0:BLOCK, :] = jnp.zeros((BLOCK, LANES), BF16)
        dsacc[...] = jnp.zeros_like(dsacc)
        qs[...] = q_ref[...].astype(F32)
        ks[...] = k_ref[...].astype(F32)
        dqd[...] = v_ref[...].astype(F32)
        for c in range(nchunk):
            rows = _chunk_rows(c, d, nb)
            qv = qs[rows, :]
            kv = ks[rows, :]
            qd[c * BLOCK:(c + 1) * BLOCK, :] = (qv * lax.rsqrt(_head_mean(qv * qv, seg) + EPS) * qg).astype(BF16)
            kd[(c + 1) * BLOCK:(c + 2) * BLOCK, :] = (kv * lax.rsqrt(_head_mean(kv * kv, seg) + EPS) * kg).astype(BF16)
            vd[(c + 1) * BLOCK:(c + 2) * BLOCK, :] = dqd[rows, :].astype(BF16)
            dod[c * BLOCK:(c + 1) * BLOCK, :] = do_ref[rows, :].astype(BF16)
            ddd[c * BLOCK:(c + 1) * BLOCK, :] = dd_ref[rows, :]
            ld[c * BLOCK:(c + 1) * BLOCK, :] = lse_ref[rows, :]

        col = lax.broadcasted_iota(jnp.int32, (BLOCK, 2 * BLOCK), 1)
        for j in range(2):
            mj = jnp.logical_and(lane >= j * HEAD_DIM, lane < (j + 1) * HEAD_DIM)
            first = lane == j * HEAD_DIM
            for c in range(nchunk):
                rows = slice(c * BLOCK, (c + 1) * BLOCK)
                kw = kd[c * BLOCK:(c + 2) * BLOCK, :]
                vw = vd[c * BLOCK:(c + 2) * BLOCK, :]
                kj = jnp.where(mj, kw, jnp.zeros_like(kw))
                vj = jnp.where(mj, vw, jnp.zeros_like(vw))
                s = lax.dot_general(qd[rows, :], kj, contract_lanes, preferred_element_type=F32) + bias_ref[j]
                if c % nb == 0:
                    s = jnp.where(col < BLOCK, NEG_INF, s)
                dp = lax.dot_general(dod[rows, :], vj, contract_lanes, preferred_element_type=F32)
                lse_j = jnp.sum(jnp.where(first, ld[rows, :], 0.0), axis=-1, keepdims=True)
                dd_j = jnp.sum(jnp.where(first, ddd[rows, :], 0.0), axis=-1, keepdims=True)
                p = jnp.exp(s - lse_j)
                ds = p * (dp + dd_j)
                dsacc[j] += ds
                pbuf[j, c] = p.astype(BF16)
                dsbuf[j, c] = ds.astype(BF16)
        for c in range(nchunk):
            rows = slice(c * BLOCK, (c + 1) * BLOCK)
            has_next = c + 1 < nchunk and (c + 1) % nb != 0
            kw = kd[c * BLOCK:(c + 2) * BLOCK, :]
            dq = jnp.zeros((BLOCK, LANES), F32)
            dk = jnp.zeros((BLOCK, LANES), F32)
            dv = jnp.zeros((BLOCK, LANES), F32)
            both = slice(c * BLOCK, (c + 2) * BLOCK) if has_next else rows
            for j in range(2):
                mj = jnp.logical_and(lane >= j * HEAD_DIM, lane < (j + 1) * HEAD_DIM)
                dq = dq + jnp.dot(dsbuf[j, c], jnp.where(mj, kw, jnp.zeros_like(kw)), preferred_element_type=F32)
                dsk = dsbuf[j, c, :, BLOCK:]
                pk = pbuf[j, c, :, BLOCK:]
                if has_next:
                    dsk = jnp.concatenate([dsk, dsbuf[j, c + 1, :, :BLOCK]], axis=0)
                    pk = jnp.concatenate([pk, pbuf[j, c + 1, :, :BLOCK]], axis=0)
                qq = qd[both, :]
                dd = dod[both, :]
                dk = dk + lax.dot_general(dsk, jnp.where(mj, qq, jnp.zeros_like(qq)), contract_rows,
                                          preferred_element_type=F32)
                dv = dv + lax.dot_general(pk, jnp.where(mj, dd, jnp.zeros_like(dd)), contract_rows,
                                          preferred_element_type=F32)
            dqd[rows, :] = dq
            dkd[rows, :] = dk
            dvd[rows, :] = dv

        dsout_ref[...] = dsin_ref[...] + dsacc[...]

        dgq = jnp.zeros((1, LANES), F32)
        dgk = jnp.zeros((1, LANES), F32)
        for c in range(nchunk):
            rows = _chunk_rows(c, d, nb)
            qv = qs[rows, :]
            rq = lax.rsqrt(_head_mean(qv * qv, seg) + EPS)
            qh = qv * rq
            dy = dqd[c * BLOCK:(c + 1) * BLOCK, :]
            dgq = dgq + jnp.sum(dy * qh, axis=0, keepdims=True) * qscale
            dxh = dy * qg
            ddd[rows, :] = rq * (dxh - qh * _head_mean(dxh * qh, seg))
            kv = ks[rows, :]
            rk = lax.rsqrt(_head_mean(kv * kv, seg) + EPS)
            kh = kv * rk
            dy = dkd[c * BLOCK:(c + 1) * BLOCK, :]
            dgk = dgk + jnp.sum(dy * kh, axis=0, keepdims=True)
            dxh = dy * kg
            ld[rows, :] = rk * (dxh - kh * _head_mean(dxh * kh, seg))
        dq_ref[...] = ddd[...].astype(BF16)
        dk_ref[...] = ld[...].astype(BF16)
        for c in range(nchunk):
            ddd[_chunk_rows(c, d, nb), :] = dvd[c * BLOCK:(c + 1) * BLOCK, :]
        dv_ref[...] = ddd[...].astype(BF16)
        dgq_ref[0] = dgq
        dgk_ref[0] = dgk

    ucol = lambda c0: pl.BlockSpec((t, LANES), lambda h: (0, c0 + h))
    vec = pl.BlockSpec((1, LANES), lambda h: (0, 0))
    oblk = pl.BlockSpec((t, LANES), lambda h: (0, h))
    bblk = pl.BlockSpec((2, BLOCK, 2 * BLOCK), lambda h: (gi * hp + h, 0, 0))
    gblk = pl.BlockSpec((1, 1, LANES), lambda h: (h, 0, 0))
    osh = jax.ShapeDtypeStruct((t, hpg * HEAD_DIM), BF16)
    gsh = jax.ShapeDtypeStruct((hp, 1, LANES), F32)
    return pl.pallas_call(
        body, name=name, grid=(hp,),
        in_specs=[ucol(qc0), ucol(kc0), ucol(vc0), oblk, oblk, oblk, vec, vec, bblk, bblk],
        out_specs=(oblk, oblk, oblk, gblk, gblk, bblk),
        out_shape=(osh, osh, osh, gsh, gsh, jax.ShapeDtypeStruct(ds_in.shape, F32)),
        input_output_aliases={9: 5},
        scratch_shapes=[pltpu.VMEM((t, LANES), BF16), pltpu.VMEM((t + BLOCK, LANES), BF16),
                        pltpu.VMEM((t + BLOCK, LANES), BF16), pltpu.VMEM((t, LANES), BF16),
                        pltpu.VMEM((t, LANES), F32), pltpu.VMEM((t, LANES), F32), pltpu.VMEM((t, LANES), F32),
                        pltpu.VMEM((t, LANES), F32), pltpu.VMEM((t, LANES), F32),
                        pltpu.VMEM((2, BLOCK, 2 * BLOCK), F32), pltpu.VMEM((2, nchunk, BLOCK, 2 * BLOCK), BF16),
                        pltpu.VMEM((2, nchunk, BLOCK, 2 * BLOCK), BF16), pltpu.VMEM((t, LANES), F32),
                        pltpu.VMEM((t, LANES), F32)],
        compiler_params=_cparams(("arbitrary",)),
    )(u, u, u, do_g, dd_g, lse_g, qg2, kg2, bias, ds_in)


def _group_weights(l0, l1, l2):
    mx = jnp.maximum(jnp.maximum(l0, l1), l2)
    e0, e1, e2 = jnp.exp(l0 - mx), jnp.exp(l1 - mx), jnp.exp(l2 - mx)
    inv = 1.0 / (e0 + e1 + e2)
    return e0 * inv, e1 * inv, e2 * inv


def _combine_fwd(os_, lses, name):
    t, ao = os_[0].shape
    ROWS = _rows(t, 26 * ao)

    def body(o0, o1, o2, l0, l1, l2, o_ref):
        w0, w1, w2 = _group_weights(l0[...], l1[...], l2[...])
        o_ref[...] = (w0 * o0[...] + w1 * o1[...] + w2 * o2[...]).astype(BF16)

    row = pl.BlockSpec((ROWS, ao), lambda i: (i, 0))
    return pl.pallas_call(
        body, name=name, grid=(t // ROWS,), in_specs=[row] * 6, out_specs=row,
        out_shape=jax.ShapeDtypeStruct((t, ao), BF16), compiler_params=_cparams(("arbitrary",)),
    )(*os_, *lses)


def _combine_bwd(do, os_, lses, name):
    t, ao = do.shape
    idx = np.arange(ao) // HEAD_DIM
    seg = jnp.asarray((idx[:, None] == idx[None, :]).astype(np.float32), dtype=BF16)
    ROWS = _rows(t, 52 * ao)

    def body(do_ref, o0, o1, o2, l0, l1, l2, seg_ref, g0, g1, g2, d0, d1, d2):
        w0, w1, w2 = _group_weights(l0[...], l1[...], l2[...])
        dov = do_ref[...]
        o = w0 * o0[...] + w1 * o1[...] + w2 * o2[...]
        sd = _segment_sum(dov * o, seg_ref[...])
        for w, gref, dref in ((w0, g0, d0), (w1, g1, d1), (w2, g2, d2)):
            gref[...] = w * dov
            dref[...] = -(w * sd)

    row = pl.BlockSpec((ROWS, ao), lambda i: (i, 0))
    sh = jax.ShapeDtypeStruct((t, ao), F32)
    outs = pl.pallas_call(
        body, name=name, grid=(t // ROWS,), in_specs=[row] * 7 + [pl.BlockSpec((ao, ao), lambda i: (0, 0))],
        out_specs=(row,) * 6, out_shape=(sh,) * 6, compiler_params=_cparams(("arbitrary",)),
    )(do, *os_, *lses, seg)
    return outs[:3], outs[3:]


def _adamw(w, g, m, v, name):
    shape = w.shape
    cols = shape[-1]
    rows = int(np.prod(shape[:-1]))
    tr = rows if rows <= 512 else _tile_rows(rows)
    c1 = 1.0 - ADAM_B1 ** ADAM_STEP
    c2 = 1.0 - ADAM_B2 ** ADAM_STEP

    def body(w_ref, g_ref, m_ref, v_ref, d_ref, nm_ref, nv_ref):
        gv = g_ref[...]
        mn = ADAM_B1 * m_ref[...] + (1.0 - ADAM_B1) * gv
        vn = ADAM_B2 * v_ref[...] + (1.0 - ADAM_B2) * (gv * gv)
        nm_ref[...] = mn
        nv_ref[...] = vn
        d_ref[...] = -ADAM_LR * ((mn / c1) / (jnp.sqrt(vn / c2) + ADAM_EPS) + ADAM_WD * w_ref[...])

    blk = pl.BlockSpec((tr, cols), lambda i: (i, 0))
    sh = jax.ShapeDtypeStruct((rows, cols), F32)
    outs = pl.pallas_call(
        body, name=name, grid=(rows // tr,), in_specs=[blk] * 4, out_specs=(blk,) * 3, out_shape=(sh,) * 3,
        compiler_params=_cparams(("arbitrary",)),
    )(*[a.reshape(rows, cols) for a in (w, g, m, v)])
    return tuple(o.reshape(shape) for o in outs)


def _tile_rows(rows):
    for t in (512, 256, 128, 64, 32, 16, 8):
        if rows % t == 0:
            return t
    return rows


def _sum_slots(recv, parts, me, name):
    _, rows, cols = recv.shape
    tr = rows if rows <= 512 else _tile_rows(rows)

    def body(me_ref, r_ref, own_ref, o_ref):
        acc = jnp.zeros(o_ref.shape, F32)
        for s in range(N_DEV):
            acc = acc + jnp.where(me_ref[0] == s, own_ref[...], r_ref[s]).astype(F32)
        o_ref[...] = acc

    return pl.pallas_call(
        body, name=name,
        grid_spec=pltpu.PrefetchScalarGridSpec(
            num_scalar_prefetch=1, grid=(rows // tr,),
            in_specs=[pl.BlockSpec((N_DEV, tr, cols), lambda i, me: (0, i, 0)),
                      pl.BlockSpec((None, tr, cols), lambda i, me: (me[0], i, 0))],
            out_specs=pl.BlockSpec((tr, cols), lambda i, me: (i, 0))),
        out_shape=jax.ShapeDtypeStruct((rows, cols), F32), compiler_params=_cparams(("arbitrary",)),
    )(me.reshape(1), recv, parts)


def _peer(k):
    x, y, c = lax.axis_index("x"), lax.axis_index("y"), lax.axis_index("c")
    return (1 - x if k & 4 else x, 1 - y if k & 2 else y, 1 - c if k & 1 else c)


def _dev_index(p):
    return 4 * p[0] + 2 * p[1] + p[2]


HBM_SPEC = pl.BlockSpec(memory_space=pltpu.HBM)
SEM_SPEC = pl.BlockSpec(memory_space=pltpu.SEMAPHORE)
ANY_SPEC = pl.BlockSpec(memory_space=pl.ANY)
CHIPS = (4, 2, 6)


def _remote(src, dst, send_sem, recv_sem, to):
    return pltpu.make_async_remote_copy(src_ref=src, dst_ref=dst, send_sem=send_sem, recv_sem=recv_sem,
                                        device_id=to, device_id_type=MESH)


def _hbm(a):
    return pltpu.with_memory_space_constraint(a, pltpu.HBM)


def _split_call(body, name, bufs, sems_in, sem_out_sizes, after):
    nb, ns, no = len(bufs), len(sems_in), len(sem_out_sizes)
    extra = [] if after is None else [after]

    def kern(*refs):
        pos = nb + ns + len(extra)
        body(refs[:nb], refs[nb:nb + ns], refs[pos:pos + no])
        token_ref = refs[pos + no + nb]
        token_ref[...] = jnp.zeros_like(token_ref)

    out_shape = (tuple(pltpu.SemaphoreType.DMA((s,)) for s in sem_out_sizes)
                 + tuple(pltpu.HBM(b.shape, b.dtype) for b in bufs) + (jax.ShapeDtypeStruct((8, LANES), F32),))
    res = pl.pallas_call(
        kern, name=name, out_shape=out_shape,
        in_specs=[HBM_SPEC] * nb + [SEM_SPEC] * ns + [ANY_SPEC] * len(extra),
        out_specs=(SEM_SPEC,) * no + (HBM_SPEC,) * nb + (pl.BlockSpec(memory_space=pltpu.VMEM),),
        input_output_aliases={i: no + i for i in range(nb)},
        compiler_params=pltpu.CompilerParams(has_side_effects=pltpu.SideEffectType.DATAFLOW_SIDE_EFFECTING),
    )(*bufs, *sems_in, *extra)
    return res[:no], res[no:no + nb], res[no + nb]


def _gather_start(shards, lands, after, name):
    n = len(shards)

    def body(bufs, _, sems):
        ins, lnd = bufs[:n], bufs[n:]
        d2d_s, d2d_r, ici_s, ici_r = sems
        me = _dev_index(_peer(0))
        for j, k in enumerate(CHIPS):
            for i in range(n):
                _remote(ins[i], lnd[i].at[me], ici_s.at[j], ici_r.at[j], _peer(k)).start()
        for i in range(n):
            _remote(ins[i], lnd[i].at[me], d2d_s.at[0], d2d_r.at[0], _peer(1)).start()

    return _split_call(body, name, [_hbm(a) for a in (*shards, *lands)], [], (1, 1, 3, 3), after)


def _gather_forward(n, bufs, ici_r, after, name):
    def body(refs, sems_in, sems):
        ins, lnd = refs[:n], refs[n:]
        (arrived,) = sems_in
        fwd_s, fwd_r = sems
        for j, k in enumerate(CHIPS):
            blk = _dev_index(_peer(k))
            for i in range(n):
                _remote(ins[i], lnd[i].at[blk], fwd_s.at[j], arrived.at[j], _peer(k)).wait_recv()
            for i in range(n):
                _remote(lnd[i].at[blk], lnd[i].at[blk], fwd_s.at[j], fwd_r.at[j], _peer(1)).start()

    return _split_call(body, name, bufs, [ici_r], (3, 3), after)


def _gather_finish(n, bufs, d2d_s, d2d_r, ici_s, fwd_s, fwd_r, after, name):
    def body(refs, sems_in, _):
        ins, lnd = refs[:n], refs[n:]
        d2d_send, d2d_recv, ici_send, fwd_send, fwd_recv = sems_in
        sib = _peer(1)
        for i in range(n):
            cp = _remote(ins[i], lnd[i].at[_dev_index(sib)], d2d_send.at[0], d2d_recv.at[0], sib)
            cp.wait_send()
            cp.wait_recv()
        for j, k in enumerate(CHIPS):
            passed = _dev_index(_peer(k))
            landed = _dev_index(_peer(k | 1))
            for i in range(n):
                _remote(ins[i], lnd[i].at[passed], ici_send.at[j], fwd_recv.at[j], _peer(k)).wait_send()
                cp = _remote(lnd[i].at[passed], lnd[i].at[landed], fwd_send.at[j], fwd_recv.at[j], sib)
                cp.wait_send()
                cp.wait_recv()

    _, out, token = _split_call(body, name, bufs, [d2d_s, d2d_r, ici_s, fwd_s, fwd_r], (), after)
    return out[n:], token


def _exchange_start(parts, lands, after, name):
    n = len(parts)

    def body(bufs, _, sems):
        src, lnd = bufs[:n], bufs[n:]
        send, recv = sems
        me = _dev_index(_peer(0))
        for k in (4, 5, 2, 3, 6, 7, 1):
            to = _peer(k)
            for i in range(n):
                _remote(src[i].at[_dev_index(to)], lnd[i].at[me], send.at[k - 1], recv.at[k - 1], to).start()

    return _split_call(body, name, [_hbm(a) for a in (*parts, *lands)], [], (7, 7), after)


def _exchange_finish(n, bufs, send, recv, after, name):
    def body(refs, sems_in, _):
        src, lnd = refs[:n], refs[n:]
        send_, recv_ = sems_in
        me = _dev_index(_peer(0))
        for k in range(1, N_DEV):
            frm = _peer(k)
            for i in range(n):
                cp = _remote(src[i].at[me], lnd[i].at[_dev_index(frm)], send_.at[k - 1], recv_.at[k - 1], frm)
                cp.wait_send()
                cp.wait_recv()

    _, out, token = _split_call(body, name, bufs, [send, recv], (), after)
    return out[:n], out[n:], token


def _all_reduce_small(v, name):
    rows = v.shape[0]

    def body(v_ref, o_ref, buf, send_sems, recv_sems):
        me = _dev_index(_peer(0))
        buf[me] = v_ref[...]
        copies = []
        for k in range(1, N_DEV):
            copies.append(pltpu.make_async_remote_copy(
                src_ref=v_ref, dst_ref=buf.at[me], send_sem=send_sems.at[k - 1], recv_sem=recv_sems.at[k - 1],
                device_id=_peer(k), device_id_type=MESH))
        for cp in copies:
            cp.start()
        for k in range(1, N_DEV):
            pltpu.make_async_remote_copy(
                src_ref=v_ref, dst_ref=buf.at[_dev_index(_peer(k))], send_sem=send_sems.at[k - 1],
                recv_sem=recv_sems.at[k - 1], device_id=_peer(k), device_id_type=MESH).wait_recv()
        for cp in copies:
            cp.wait_send()
        acc = buf[0]
        for s in range(1, N_DEV):
            acc = acc + buf[s]
        o_ref[...] = acc

    vm = pl.BlockSpec(memory_space=pltpu.VMEM)
    return pl.pallas_call(
        body, name=name, in_specs=[vm], out_specs=vm, out_shape=jax.ShapeDtypeStruct(v.shape, F32),
        scratch_shapes=[pltpu.VMEM((N_DEV, rows, LANES), F32), pltpu.SemaphoreType.DMA((7,)),
                        pltpu.SemaphoreType.DMA((7,))],
    )(v)


def _columns(cdim, ao):
    q_col = 2 * cdim
    attn_dim = N_GROUPS * ao
    return (q_col, q_col + attn_dim, q_col + 2 * attn_dim), q_col + 3 * attn_dim


def _layer_fwd(x, h1, sm, bg, get_rest, bias, hpg, next_gain):
    cdim = sm["conv_ln_g"].shape[0]
    ao = hpg * HEAD_DIM
    cols, gate_col = _columns(cdim, ao)
    qg2 = jnp.tile(sm["q_norm_g"], 2).reshape(1, LANES)
    kg2 = jnp.tile(sm["k_norm_g"], 2).reshape(1, LANES)
    u = _mm_in_pairs(h1, bg["w_in"], "mm_in")
    zc = _conv_fwd(u, bg["conv_dw_w"], sm["conv_dw_b"], cdim, "conv_fwd")
    zs = _ln_swish_fwd(zc, sm["conv_ln_g"], sm["conv_ln_b"], "ln_swish_fwd")
    os_, lses = [], []
    for gi in range(N_GROUPS):
        o_g, lse_g = _attn_fwd(u, qg2, kg2, bias, gi, cols, hpg, "attn_fwd_g%d" % gi)
        os_.append(o_g)
        lses.append(lse_g)
    o = _combine_fwd(os_, lses, "combine_fwd")
    bg = {**bg, **get_rest(o)}
    yc = _mm(zs, bg["w_conv_out"], name="mm_conv_out")
    ya = _mm(o, bg["w_attn_out"], name="mm_attn_out")
    mg = _gate_fwd(u, yc, ya, gate_col, "gate_fwd")
    x1, h2 = _mm(mg, bg["w_out"], epi="res_rms", extra=x, gain=sm["norm2_g"], name="mm_out")
    f = _mm(h2, bg["w_ff1"], out_dtype=BF16, name="mm_ff1")
    if next_gain is None:
        x2, h_next = _mm(f, bg["w_ff2"], a_relu2=True, epi="res", extra=x1, name="mm_ff2"), None
    else:
        x2, h_next = _mm(f, bg["w_ff2"], a_relu2=True, epi="res_rms", extra=x1, gain=next_gain, name="mm_ff2")
    saved = dict(x=x, h1=h1, u=u, zc=zc, zs=zs, yc=yc, os=os_, lses=lses, o=o, ya=ya, mg=mg, x1=x1, h2=h2, f=f,
                 qg2=qg2, kg2=kg2)
    return x2, h_next, saved, bg


GRAD_GROUPS = (("w_ff2", "w_ff1"), ("w_out", "w_conv_out", "w_attn_out", "conv_dw_w"), ("w_in",))


def _layer_bwd(dx, s, sm, bg, bias, ds_sum, after, emit):
    cdim = sm["conv_ln_g"].shape[0]
    ao = bg["w_attn_out"].shape[0]
    hpg = ao // HEAD_DIM
    hp = hpg // 2
    cols, gate_col = _columns(cdim, ao)
    g = {}
    df = _mm(dx, bg["w_ff2"], tb=True, epi="drelu2", extra=s["f"], out_dtype=BF16, after=after, name="mm_dff2")
    g["w_ff2"] = _mm(s["f"], dx, ta=True, a_relu2=True, out_dtype=BF16, name="mm_gw_ff2")
    g["w_ff1"] = _mm(s["h2"], df, ta=True, out_dtype=BF16, out_slots=True, name="mm_gw_ff1")
    after = emit(GRAD_GROUPS[0], g)
    dx1, dg2 = _mm(df, bg["w_ff1"], tb=True, epi="rms_bwd", extra=(s["x1"], dx), gain=sm["norm2_g"], after=after,
                   name="mm_dff1")
    g["norm2_g"] = dg2[0]
    dmg = _mm(dx1, bg["w_out"], tb=True, name="mm_dout")
    g["w_out"] = _mm(s["mg"], dx1, ta=True, out_dtype=BF16, name="mm_gw_out")
    dyc, dya, dugc, duga = _gate_bwd(dmg, s["u"], s["yc"], s["ya"], gate_col, "gate_bwd")
    dzs = _mm(dyc, bg["w_conv_out"], tb=True, name="mm_dconv_out")
    g["w_conv_out"] = _mm(s["zs"], dyc, ta=True, out_dtype=BF16, name="mm_gw_conv_out")
    do = _mm(dya, bg["w_attn_out"], tb=True, name="mm_dattn_out")
    g["w_attn_out"] = _mm(s["o"], dya, ta=True, out_dtype=BF16, name="mm_gw_attn_out")
    dzc, dlg, dlb = _ln_swish_bwd(dzs, s["zc"], sm["conv_ln_g"], sm["conv_ln_b"], "ln_swish_bwd")
    g["conv_ln_g"] = dlg[0]
    g["conv_ln_b"] = dlb[0]
    da, dgt, dcw, dcb = _conv_bwd(dzc, s["u"], bg["conv_dw_w"], cdim, "conv_bwd")
    g["conv_dw_w"] = dcw[:CONV_WIDTH].astype(BF16)
    g["conv_dw_b"] = dcb[0]
    after = emit(GRAD_GROUPS[1], g)
    do_gs, dd_gs = _combine_bwd(do, s["os"], s["lses"], "combine_bwd")
    dqs, dks, dvs = [], [], []
    dgq = jnp.zeros((HEAD_DIM,), F32)
    dgk = jnp.zeros((HEAD_DIM,), F32)
    for gi in range(N_GROUPS):
        dq, dk, dv, gq, gk, ds_sum = _attn_bwd(s["u"], do_gs[gi], dd_gs[gi], s["lses"][gi], s["qg2"], s["kg2"],
                                               bias, ds_sum, gi, cols, hpg, "attn_bwd_g%d" % gi)
        dqs.append(dq)
        dks.append(dk)
        dvs.append(dv)
        dgq = dgq + jnp.sum(gq.reshape(hp * 2, HEAD_DIM), axis=0)
        dgk = dgk + jnp.sum(gk.reshape(hp * 2, HEAD_DIM), axis=0)
    g["q_norm_g"] = dgq
    g["k_norm_g"] = dgk
    du = jnp.concatenate([da, dgt] + dqs + dks + dvs + [dugc, duga], axis=1)
    g["w_in"] = _mm_gw_in_pairs(s["h1"], du, after, "mm_gw_in")
    after = emit(GRAD_GROUPS[2], g)
    dx0, dg1 = _mm_din_pairs(du, bg["w_in"], s["x"], sm["norm1_g"], dx1, after, "mm_din")
    g["norm1_g"] = dg1[0]
    return dx0, g, ds_sum


BIG = ("w_in", "conv_dw_w", "w_conv_out", "w_attn_out", "w_out", "w_ff1", "w_ff2")
COL_SHARDED = ("w_in", "conv_dw_w", "w_conv_out", "w_attn_out", "w_ff1")
SMALL = ("rel_bias", "norm1_g", "q_norm_g", "k_norm_g", "conv_dw_b", "conv_ln_g", "conv_ln_b", "norm2_g")
WEIGHTS = ("rel_bias", "norm1_g", "w_in", "q_norm_g", "k_norm_g", "conv_dw_w", "conv_dw_b", "conv_ln_g", "conv_ln_b",
           "w_conv_out", "w_attn_out", "w_out", "norm2_g", "w_ff1", "w_ff2")


def _to_whole(name, gathered):
    n, a, b = gathered.shape
    if name in COL_SHARDED:
        return gathered.transpose(1, 0, 2).reshape(a, n * b)
    return gathered.reshape(n * a, b)


def _to_slots(name, whole):
    a, b = whole.shape
    if name in COL_SHARDED:
        return whole.reshape(a, N_DEV, b // N_DEV).transpose(1, 0, 2)
    return whole.reshape(N_DEV, a // N_DEV, b)


def _own_slot(block, me):
    land = lax.empty((N_DEV,) + block.shape, block.dtype)
    return lax.dynamic_update_slice(land, block[None], (me,) + (0,) * block.ndim)


def kernel(x, rel_bias, norm1_g, w_in, q_norm_g, k_norm_g, conv_dw_w, conv_dw_b, conv_ln_g, conv_ln_b, w_conv_out, w_attn_out, w_out, norm2_g, w_ff1, w_ff2, loss_target, m_rel_bias, m_norm1_g, m_w_in, m_q_norm_g, m_k_norm_g, m_conv_dw_w, m_conv_dw_b, m_conv_ln_g, m_conv_ln_b, m_w_conv_out, m_w_attn_out, m_w_out, m_norm2_g, m_w_ff1, m_w_ff2, v_rel_bias, v_norm1_g, v_w_in, v_q_norm_g, v_k_norm_g, v_conv_dw_w, v_conv_dw_b, v_conv_ln_g, v_conv_ln_b, v_w_conv_out, v_w_attn_out, v_w_out, v_norm2_g, v_w_ff1, v_w_ff2):
    w = dict(rel_bias=rel_bias, norm1_g=norm1_g, w_in=w_in, q_norm_g=q_norm_g, k_norm_g=k_norm_g, conv_dw_w=conv_dw_w,
             conv_dw_b=conv_dw_b, conv_ln_g=conv_ln_g, conv_ln_b=conv_ln_b, w_conv_out=w_conv_out,
             w_attn_out=w_attn_out, w_out=w_out, norm2_g=norm2_g, w_ff1=w_ff1, w_ff2=w_ff2)
    mom = dict(rel_bias=m_rel_bias, norm1_g=m_norm1_g, w_in=m_w_in, q_norm_g=m_q_norm_g, k_norm_g=m_k_norm_g,
               conv_dw_w=m_conv_dw_w, conv_dw_b=m_conv_dw_b, conv_ln_g=m_conv_ln_g, conv_ln_b=m_conv_ln_b,
               w_conv_out=m_w_conv_out, w_attn_out=m_w_attn_out, w_out=m_w_out, norm2_g=m_norm2_g, w_ff1=m_w_ff1,
               w_ff2=m_w_ff2)
    var = dict(rel_bias=v_rel_bias, norm1_g=v_norm1_g, w_in=v_w_in, q_norm_g=v_q_norm_g, k_norm_g=v_k_norm_g,
               conv_dw_w=v_conv_dw_w, conv_dw_b=v_conv_dw_b, conv_ln_g=v_conv_ln_g, conv_ln_b=v_conv_ln_b,
               w_conv_out=v_w_conv_out, w_attn_out=v_w_attn_out, w_out=v_w_out, norm2_g=v_norm2_g, w_ff1=v_w_ff1,
               w_ff2=v_w_ff2)

    depth = norm1_g.shape[0]
    me = 4 * lax.axis_index("x") + 2 * lax.axis_index("y") + lax.axis_index("c")
    odd_core = lax.axis_index("c") == 1
    hpg = w_attn_out.shape[1] // HEAD_DIM
    buckets = jnp.asarray(_bucket_table())
    bias = _bias_expand(rel_bias, buckets, hpg, "bias_expand")

    first_names = ("w_in", "conv_dw_w")
    rest_names = tuple(k for k in BIG if k not in first_names)

    def chain_start(l, names, after):
        shards = [w[k][l] if k == "conv_dw_w" else w[k][l].astype(BF16) for k in names]
        if "w_in" in names:
            i = names.index("w_in")
            shards[i] = jnp.where(odd_core, jnp.pad(shards[i], ((0, 0), (SHIFT, 0))),
                                  jnp.pad(shards[i], ((0, 0), (0, SHIFT))))
        sems, bufs, token = _gather_start(shards, [_own_slot(s, me) for s in shards], after,
                                          "gather_start_%s_l%d" % (names[0], l))
        return dict(l=l, names=names, sems=sems, bufs=bufs, token=token)

    def chain_forward(ch, after):
        fwd, bufs, token = _gather_forward(len(ch["names"]), ch["bufs"], ch["sems"][3], after,
                                           "gather_forward_%s_l%d" % (ch["names"][0], ch["l"]))
        ch.update(fwd=fwd, bufs=bufs)
        return token

    def chain_finish(ch, after):
        d2d_s, d2d_r, ici_s, _ = ch["sems"]
        gathered, _ = _gather_finish(len(ch["names"]), ch["bufs"], d2d_s, d2d_r, ici_s, ch["fwd"][0], ch["fwd"][1],
                                     after, "gather_finish_%s_l%d" % (ch["names"][0], ch["l"]))
        out = {k: a if k == "w_in" else _to_whole(k, a) for k, a in zip(ch["names"], gathered)}
        if "conv_dw_w" in out:
            out["conv_dw_w"] = jnp.pad(out["conv_dw_w"], ((0, CONV_TAPS_PADDED - CONV_WIDTH), (0, 0)))
        return out

    xs = x[0]
    h1 = _rms_fwd(xs, norm1_g[0], "rms1_fwd")
    saved, bigs, smalls = [], [], []
    chains = {}
    for l in range(depth):
        sm = {k: w[k][l] for k in SMALL if k != "rel_bias"}
        if l == 0:
            first = chain_start(0, first_names, None)
            token = chain_forward(first, None)
            rest = chain_start(0, rest_names, token)
            bg = chain_finish(first, rest["token"])

            def get_rest(o, rest=rest):
                token = chain_forward(rest, o)
                if depth > 1:
                    chains[1] = (chain_start(1, first_names, token),)
                    chains[1] += (chain_start(1, rest_names, chains[1][0]["token"]),)
                    token = chains[1][1]["token"]
                return chain_finish(rest, token)
        elif l == 1:
            first, rest = chains[1]
            token = chain_forward(first, xs)
            if depth > 2:
                chains[2] = chain_start(2, BIG, token)
                token = chains[2]["token"]
            bg = chain_finish(first, token)

            def get_rest(o, rest=rest):
                return chain_finish(rest, chain_forward(rest, o))
        else:
            token = chain_forward(chains[l], xs)
            if l + 1 < depth:
                chains[l + 1] = chain_start(l + 1, BIG, token)
                token = chains[l + 1]["token"]
            whole = chain_finish(chains[l], token)
            bg = {k: whole[k] for k in first_names}

            def get_rest(o, whole=whole):
                return {k: whole[k] for k in rest_names}
        xs, h1, sv, bg = _layer_fwd(xs, h1, sm, bg, get_rest, bias, hpg, norm1_g[l + 1] if l + 1 < depth else None)
        saved.append(sv)
        bigs.append(bg)
        smalls.append(sm)

    loss, dx = _loss_and_grad(xs, loss_target[0], "loss")

    ds_sum = jnp.zeros((N_GROUPS * hpg, BLOCK, 2 * BLOCK), F32)
    g = {k: [None] * depth for k in SMALL if k != "rel_bias"}
    grad_layers = {k: [None] * depth for k in BIG}
    pending = []

    def finish_oldest(after):
        names, l, (send, recv), bufs = pending.pop(0)
        parts, recvd, token = _exchange_finish(len(names), bufs, send, recv, after,
                                               "exchange_finish_%s_l%d" % (names[0], l))
        for k, r, p in zip(names, recvd, parts):
            shp = w[k].shape[1:]
            three = (N_DEV, -1, r.shape[-1])
            total = _sum_slots(r.reshape(three), p.reshape(three), me, "sum_" + k)
            if k == "w_in":
                total = jnp.where(odd_core, total[:, SHIFT:], total[:, :shp[-1]])
            grad_layers[k][l] = total.reshape(shp)
        return token

    def make_emit(l):
        def emit(names, gl):
            parts = [gl[k] if k in ("w_ff1", "w_in") else _to_slots(k, gl[k]) for k in names]
            token = finish_oldest(parts[0]) if len(pending) >= len(GRAD_GROUPS) else None
            lands = [lax.empty(p.shape, p.dtype) for p in parts]
            sems, bufs, token = _exchange_start(parts, lands, token, "exchange_start_%s_l%d" % (names[0], l))
            pending.append((names, l, sems, bufs))
            return token
        return emit

    token = None
    for l in reversed(range(depth)):
        dx, gl, ds_sum = _layer_bwd(dx, saved[l], smalls[l], bigs[l], bias, ds_sum, token, make_emit(l))
        for k in g:
            g[k][l] = gl[k]
        token = None
    while pending:
        finish_oldest(None)
    grad_x = dx

    g = {k: jnp.stack(v) for k, v in g.items()}
    db = _bias_reduce(ds_sum, buckets, hpg, "bias_reduce")
    g["rel_bias"] = db[:, 0, :NUM_BUCKETS].T

    flat = jnp.concatenate([g[k].reshape(-1) for k in SMALL])
    nflat = flat.shape[0]
    rows = -(-nflat // (8 * LANES)) * 8
    packed = jnp.pad(flat, (0, rows * LANES - nflat)).reshape(rows, LANES)
    total = _all_reduce_small(packed, "reduce_small").reshape(-1)
    grad = {}
    off = 0
    for k in SMALL:
        size = int(np.prod(w[k].shape))
        grad[k] = total[off:off + size].reshape(w[k].shape)
        off += size

    for k in BIG:
        grad[k] = jnp.stack(grad_layers[k])

    loss = lax.psum(loss[0, 0], ("x", "y", "c"))
    outs = {k: _adamw(w[k], grad[k], mom[k], var[k], "adamw_" + k) for k in WEIGHTS}
    return (loss, grad_x[None], *[grad[k] for k in WEIGHTS], *[outs[k][0] for k in WEIGHTS],
            *[outs[k][1] for k in WEIGHTS], *[outs[k][2] for k in WEIGHTS])
```

```python
import functools
import math

import numpy as np
import jax
import jax.numpy as jnp
from jax import lax
from jax.experimental import pallas as pl
from jax.experimental.pallas import tpu as pltpu

F32 = jnp.float32
BF16 = jnp.bfloat16

HEAD_DIM = 64
N_GROUPS = 3
DILATIONS = (1, 4, 16)
SUB_WINDOW = 128
BLOCK = 128
CONV_WIDTH = 31
CONV_TAPS_PADDED = 32
NUM_BUCKETS = 32
MAX_REL_DISTANCE = 2048
EPS = 1e-6
NEG_INF = -1e30
LANES = 128
SUBLANES = 8

ADAM_LR = 0.001
ADAM_B1 = 0.9
ADAM_B2 = 0.999
ADAM_EPS = 1e-08
ADAM_WD = 0.01
ADAM_STEP = 10

N_DEV = 8
VMEM_LIMIT = 56 * 1024 * 1024
MESH = pl.DeviceIdType.MESH


def _cparams(sem=None):
    return pltpu.CompilerParams(dimension_semantics=sem, vmem_limit_bytes=VMEM_LIMIT)


def _tile(n, target):
    if n <= target:
        return n
    t = (target // LANES) * LANES
    while t >= LANES:
        if n % t == 0:
            return t
        t -= LANES
    return n


def _sigmoid(v):
    return 1.0 / (1.0 + jnp.exp(-v))


MM_VMEM_BUDGET = 40 * 1024 * 1024


def _rms_apply(x, g):
    return x * lax.rsqrt(jnp.mean(x * x, axis=-1, keepdims=True) + EPS) * g


def _rms_grad(dh, x, g):
    r = lax.rsqrt(jnp.mean(x * x, axis=-1, keepdims=True) + EPS)
    xh = x * r
    dxh = dh * g
    dx = r * (dxh - xh * jnp.mean(dxh * xh, axis=-1, keepdims=True))
    return dx, jnp.sum(dh * xh, axis=0, keepdims=True)


def _mm_tiles(m, n, kdim, a_bytes, b_bytes, io_bytes, whole_rows=False, temps=2):
    def need(tm, tn, tk):
        blocks = 2 * (tm * tk * a_bytes + tk * tn * b_bytes + tm * tn * io_bytes)
        casts = (tm * tk * 2 if a_bytes == 4 else 0) + (tk * tn * 2 if b_bytes == 4 else 0)
        return blocks + casts + temps * tm * tn * 4

    tn = n if whole_rows else _tile(n, 1024)
    while True:
        fits = [(tm * tk, tm, tk) for tm in {_tile(m, c) for c in (1024, 512, 256, 128)}
                for tk in {_tile(kdim, c) for c in (2048, 1024, 512, 256)} if need(tm, tn, tk) <= MM_VMEM_BUDGET]
        if fits:
            _, tm, tk = max(fits)
            return tm, tn, tk
        assert not whole_rows and tn % 256 == 0, "no block size fits the VMEM budget"
        tn //= 2


def _mm(a, b, *, ta=False, tb=False, out_dtype=F32, epi=None, extra=(), gain=None, after=None, out_slots=False,
        a_relu2=False, name):
    extra = tuple(extra) if isinstance(extra, (tuple, list)) else (extra,)
    m = a.shape[1] if ta else a.shape[0]
    kdim = a.shape[0] if ta else a.shape[1]
    n = b.shape[0] if tb else b.shape[1]
    norm = epi in ("res_rms", "rms_bwd")
    io_bytes = (jnp.dtype(out_dtype).itemsize + sum(e.dtype.itemsize for e in extra) + (2 if epi == "res_rms" else 0))
    tm, tn, tk = _mm_tiles(m, n // N_DEV if out_slots else n, kdim, a.dtype.itemsize, b.dtype.itemsize, io_bytes,
                           whole_rows=norm, temps=6 if norm else 2)
    if out_slots:
        assert epi is None and tn == n // N_DEV
    nk = kdim // tk
    a_spec = pl.BlockSpec((tk, tm), lambda i, j, k: (k, i)) if ta else pl.BlockSpec((tm, tk), lambda i, j, k: (i, k))
    b_spec = pl.BlockSpec((tn, tk), lambda i, j, k: (j, k)) if tb else pl.BlockSpec((tk, tn), lambda i, j, k: (k, j))
    o_spec = (pl.BlockSpec((None, tm, tn), lambda i, j, k: (j, i, 0)) if out_slots
              else pl.BlockSpec((tm, tn), lambda i, j, k: (i, j)))
    v_spec = pl.BlockSpec((1, tn), lambda i, j, k: (0, j))
    dims = (((0 if ta else 1,), (1 if tb else 0,)), ((), ()))
    n_extra = len(extra)
    n_in = 2 + n_extra + (gain is not None) + (after is not None)
    n_out = 2 if norm else 1

    def body(*refs):
        a_ref, b_ref = refs[0], refs[1]
        e_refs = refs[2:2 + n_extra]
        g_ref = refs[2 + n_extra] if gain is not None else None
        outs = refs[n_in:n_in + n_out]

        def product():
            av = a_ref[...]
            if a_relu2:
                r = jnp.maximum(av.astype(F32), 0.0)
                av = r * r
            return lax.dot_general(av.astype(BF16), b_ref[...].astype(BF16), dims, preferred_element_type=F32)

        def finish(acc):
            if epi is None:
                outs[0][...] = acc.astype(outs[0].dtype)
            elif epi == "res":
                outs[0][...] = (e_refs[0][...] + acc).astype(outs[0].dtype)
            elif epi == "drelu2":
                outs[0][...] = (acc * (2.0 * jnp.maximum(e_refs[0][...].astype(F32), 0.0))).astype(outs[0].dtype)
            elif epi == "res_rms":
                x1 = e_refs[0][...] + acc
                outs[0][...] = x1
                outs[1][...] = _rms_apply(x1, g_ref[...]).astype(BF16)
            elif epi == "rms_bwd":
                dx, dg = _rms_grad(acc, e_refs[0][...], g_ref[...])
                outs[0][...] = e_refs[1][...] + dx
                i = pl.program_id(0)

                @pl.when(i == 0)
                def _():
                    outs[1][...] = dg

                @pl.when(i > 0)
                def _():
                    outs[1][...] += dg

        if nk == 1:
            finish(product())
            return
        acc_ref = refs[-1]
        k = pl.program_id(2)

        @pl.when(k == 0)
        def _():
            acc_ref[...] = product()

        @pl.when(jnp.logical_and(k > 0, k < nk - 1))
        def _():
            acc_ref[...] += product()

        @pl.when(k == nk - 1)
        def _():
            finish(acc_ref[...] + product())

    in_specs = ([a_spec, b_spec] + [o_spec] * n_extra + ([v_spec] if gain is not None else [])
                + ([pl.BlockSpec(memory_space=pl.ANY)] if after is not None else []))
    if epi == "res_rms":
        out_shape = (jax.ShapeDtypeStruct((m, n), F32), jax.ShapeDtypeStruct((m, n), BF16))
        out_specs = (o_spec, o_spec)
    elif epi == "rms_bwd":
        out_shape = (jax.ShapeDtypeStruct((m, n), F32), jax.ShapeDtypeStruct((1, n), F32))
        out_specs = (o_spec, v_spec)
    else:
        out_shape = jax.ShapeDtypeStruct((N_DEV, m, tn) if out_slots else (m, n), out_dtype)
        out_specs = o_spec
    args = (a, b) + extra + ((gain.reshape(1, n),) if gain is not None else ()) + ((after,) if after is not None else ())
    return pl.pallas_call(
        body, name=name, grid=(m // tm, n // tn, nk), in_specs=in_specs, out_specs=out_specs, out_shape=out_shape,
        scratch_shapes=[pltpu.VMEM((tm, tn), F32)] if nk > 1 else [],
        compiler_params=_cparams(("arbitrary", "arbitrary", "arbitrary")),
    )(*args)


SHIFT = HEAD_DIM


def _pair_blocks(e, o):
    wp = e.shape[-1]
    return e[:, :wp - LANES], e[:, wp - LANES:] + o[:, :LANES], o[:, LANES:]


def _mm_in_pairs(a, wg, name):
    t, kdim = a.shape
    wp = wg.shape[-1]
    ws = wp - SHIFT
    tm = _tile(t, 1024)

    def body(a_ref, e_ref, o_ref, u_ref):
        av = a_ref[...]
        lo, mid, hi = _pair_blocks(e_ref[...], o_ref[...])
        u_ref[:, :wp - LANES] = jnp.dot(av, lo, preferred_element_type=F32).astype(BF16)
        u_ref[:, wp - LANES:wp] = jnp.dot(av, mid, preferred_element_type=F32).astype(BF16)
        u_ref[:, wp:] = jnp.dot(av, hi, preferred_element_type=F32).astype(BF16)

    return pl.pallas_call(
        body, name=name, grid=(N_DEV // 2, t // tm),
        in_specs=[pl.BlockSpec((tm, kdim), lambda p, i: (i, 0)),
                  pl.BlockSpec((None, kdim, wp), lambda p, i: (2 * p, 0, 0)),
                  pl.BlockSpec((None, kdim, wp), lambda p, i: (2 * p + 1, 0, 0))],
        out_specs=pl.BlockSpec((tm, 2 * ws), lambda p, i: (i, p)),
        out_shape=jax.ShapeDtypeStruct((t, N_DEV * ws), BF16), compiler_params=_cparams(("arbitrary", "arbitrary")),
    )(a, wg, wg)


def _mm_din_pairs(du, wg, x, gain, dres, after, name):
    t = du.shape[0]
    _, kdim, wp = wg.shape
    ws = wp - SHIFT
    tm = _tile(t, 512)
    npair = N_DEV // 2
    lanes = (((1,), (1,)), ((), ()))
    extra = [] if after is None else [after]

    def body(d_ref, e_ref, o_ref, x_ref, g_ref, r_ref, *rest):
        dx_ref, dg_ref, acc_ref = rest[-3], rest[-2], rest[-1]
        i, p = pl.program_id(0), pl.program_id(1)
        lo, mid, hi = _pair_blocks(e_ref[...], o_ref[...])
        part = (lax.dot_general(d_ref[:, :wp - LANES], lo, lanes, preferred_element_type=F32)
                + lax.dot_general(d_ref[:, wp - LANES:wp], mid, lanes, preferred_element_type=F32)
                + lax.dot_general(d_ref[:, wp:], hi, lanes, preferred_element_type=F32))

        @pl.when(p == 0)
        def _():
            acc_ref[...] = part

        @pl.when(jnp.logical_and(p > 0, p < npair - 1))
        def _():
            acc_ref[...] += part

        @pl.when(p == npair - 1)
        def _():
            dx, dg = _rms_grad(acc_ref[...] + part, x_ref[...], g_ref[...])
            dx_ref[...] = r_ref[...] + dx

            @pl.when(i == 0)
            def _():
                dg_ref[...] = dg

            @pl.when(i > 0)
            def _():
                dg_ref[...] += dg

    row = pl.BlockSpec((tm, kdim), lambda i, p: (i, 0))
    vec = pl.BlockSpec((1, kdim), lambda i, p: (0, 0))
    return pl.pallas_call(
        body, name=name, grid=(t // tm, npair),
        in_specs=[pl.BlockSpec((tm, 2 * ws), lambda i, p: (i, p)),
                  pl.BlockSpec((None, kdim, wp), lambda i, p: (2 * p, 0, 0)),
                  pl.BlockSpec((None, kdim, wp), lambda i, p: (2 * p + 1, 0, 0)), row, vec, row]
        + [pl.BlockSpec(memory_space=pl.ANY)] * len(extra),
        out_specs=(row, vec),
        out_shape=(jax.ShapeDtypeStruct((t, kdim), F32), jax.ShapeDtypeStruct((1, kdim), F32)),
        scratch_shapes=[pltpu.VMEM((tm, kdim), F32)], compiler_params=_cparams(("arbitrary", "arbitrary")),
    )(du, wg, wg, x, gain.reshape(1, kdim), dres, *extra)


def _mm_gw_in_pairs(h, du, after, name):
    t, kdim = h.shape
    ws = du.shape[1] // N_DEV
    wp = ws + SHIFT
    tm = _tile(kdim, 512)
    rows = (((0,), (0,)), ((), ()))
    extra = [] if after is None else [after]

    def body(h_ref, d_ref, *rest):
        g_ref = rest[-1]
        g = lax.dot_general(h_ref[...], d_ref[...], rows, preferred_element_type=F32)
        g_ref[0] = g[:, :wp].astype(BF16)
        g_ref[1] = g[:, wp - LANES:].astype(BF16)

    return pl.pallas_call(
        body, name=name, grid=(N_DEV // 2, kdim // tm),
        in_specs=[pl.BlockSpec((t, tm), lambda p, i: (0, i)), pl.BlockSpec((t, 2 * ws), lambda p, i: (0, p))]
        + [pl.BlockSpec(memory_space=pl.ANY)] * len(extra),
        out_specs=pl.BlockSpec((2, tm, wp), lambda p, i: (p, i, 0)),
        out_shape=jax.ShapeDtypeStruct((N_DEV, kdim, wp), BF16), compiler_params=_cparams(("arbitrary", "arbitrary")),
    )(h, du, *extra)


ROW_BLOCK_BUDGET = 24 * 1024 * 1024


def _rows(t, row_bytes):
    rows = t
    while rows > 8 and (2 * rows * row_bytes > ROW_BLOCK_BUDGET or t % rows):
        rows //= 2
    return rows


def _rms_fwd(x, g, name):
    t, d = x.shape
    ROWS = _rows(t, 6 * d)

    def body(x_ref, g_ref, h_ref):
        h_ref[...] = _rms_apply(x_ref[...], g_ref[...]).astype(BF16)

    return pl.pallas_call(
        body, name=name, grid=(t // ROWS,),
        in_specs=[pl.BlockSpec((ROWS, d), lambda i: (i, 0)), pl.BlockSpec((1, d), lambda i: (0, 0))],
        out_specs=pl.BlockSpec((ROWS, d), lambda i: (i, 0)),
        out_shape=jax.ShapeDtypeStruct((t, d), BF16), compiler_params=_cparams(("arbitrary",)),
    )(x, g.reshape(1, d))


def _gate_fwd(u, yc, ya, gate_col, name):
    t, d = yc.shape
    td = math.gcd(_tile(d, 512), gate_col)
    nd = d // td
    c0 = gate_col // td
    ROWS = _rows(t, 14 * td)

    def body(gc_ref, ga_ref, yc_ref, ya_ref, m_ref):
        gc = _sigmoid(gc_ref[...].astype(F32))
        ga = _sigmoid(ga_ref[...].astype(F32))
        m_ref[...] = (gc * yc_ref[...] + ga * ya_ref[...]).astype(BF16)

    blk = pl.BlockSpec((ROWS, td), lambda i, j: (i, j))
    return pl.pallas_call(
        body, name=name, grid=(t // ROWS, nd),
        in_specs=[pl.BlockSpec((ROWS, td), lambda i, j: (i, c0 + j)),
                  pl.BlockSpec((ROWS, td), lambda i, j: (i, c0 + nd + j)), blk, blk],
        out_specs=blk, out_shape=jax.ShapeDtypeStruct((t, d), BF16),
        compiler_params=_cparams(("arbitrary", "arbitrary")),
    )(u, u, yc, ya)


def _gate_bwd(dm, u, yc, ya, gate_col, name):
    t, d = yc.shape
    td = math.gcd(_tile(d, 512), gate_col)
    nd = d // td
    c0 = gate_col // td
    ROWS = _rows(t, 24 * td)

    def body(dm_ref, gc_ref, ga_ref, yc_ref, ya_ref, dyc_ref, dya_ref, dugc_ref, duga_ref):
        dmv = dm_ref[...]
        gc = _sigmoid(gc_ref[...].astype(F32))
        ga = _sigmoid(ga_ref[...].astype(F32))
        dyc_ref[...] = (dmv * gc).astype(BF16)
        dya_ref[...] = (dmv * ga).astype(BF16)
        dugc_ref[...] = (dmv * yc_ref[...] * gc * (1.0 - gc)).astype(BF16)
        duga_ref[...] = (dmv * ya_ref[...] * ga * (1.0 - ga)).astype(BF16)

    blk = pl.BlockSpec((ROWS, td), lambda i, j: (i, j))
    o = jax.ShapeDtypeStruct((t, d), BF16)
    return pl.pallas_call(
        body, name=name, grid=(t // ROWS, nd),
        in_specs=[blk, pl.BlockSpec((ROWS, td), lambda i, j: (i, c0 + j)),
                  pl.BlockSpec((ROWS, td), lambda i, j: (i, c0 + nd + j)), blk, blk],
        out_specs=(blk, blk, blk, blk), out_shape=(o, o, o, o),
        compiler_params=_cparams(("arbitrary", "arbitrary")),
    )(dm, u, u, yc, ya)


def _loss_and_grad(y, target, name):
    t, d = y.shape
    ROWS = _rows(t, 12 * d)
    n = t // ROWS

    def body(y_ref, t_ref, loss_ref, dy_ref, acc_ref):
        i = pl.program_id(0)

        @pl.when(i == 0)
        def _():
            acc_ref[...] = jnp.zeros_like(acc_ref)

        diff = y_ref[...] - t_ref[...]
        dy_ref[...] = diff * (1.0 / d)
        acc_ref[...] += jnp.sum(diff * diff, axis=0, keepdims=True)

        @pl.when(i == n - 1)
        def _():
            loss_ref[...] = jnp.sum(acc_ref[...], axis=-1, keepdims=True) * (0.5 / d)

    row = pl.BlockSpec((ROWS, d), lambda i: (i, 0))
    return pl.pallas_call(
        body, name=name, grid=(n,), in_specs=[row, row],
        out_specs=(pl.BlockSpec((1, 1), lambda i: (0, 0)), row),
        out_shape=(jax.ShapeDtypeStruct((1, 1), F32), jax.ShapeDtypeStruct((t, d), F32)),
        scratch_shapes=[pltpu.VMEM((1, d), F32)], compiler_params=_cparams(("arbitrary",)),
    )(y, target)


HALO = 32


def _conv_fwd(u, w, b, cdim, name):
    t = u.shape[0]
    ncb = cdim // LANES
    nt = t // BLOCK

    def body(a_ref, g_ref, w_ref, b_ref, zc_ref, zpad):
        zpad[0:HALO, :] = jnp.zeros((HALO, LANES), F32)
        zpad[HALO:HALO + t, :] = a_ref[...].astype(F32) * _sigmoid(g_ref[...].astype(F32))
        wv = w_ref[...]
        bv = b_ref[...]

        def tile(i, carry):
            r0 = pl.multiple_of(i * BLOCK, BLOCK)
            win = zpad[pl.ds(r0, BLOCK + HALO), :]
            acc = jnp.zeros((BLOCK, LANES), F32) + bv
            for b in range(SUBLANES):
                sh = win if b == 0 else pltpu.roll(win, b, 0)
                for a in range(HALO // SUBLANES):
                    j = CONV_WIDTH - 1 - (SUBLANES * a + b)
                    if j >= 0:
                        lo = HALO - SUBLANES * a
                        acc = acc + wv[j:j + 1, :] * sh[lo:lo + BLOCK, :]
            zc_ref[pl.ds(r0, BLOCK), :] = acc
            return carry

        lax.fori_loop(0, nt, tile, 0)

    col = lambda off: pl.BlockSpec((t, LANES), lambda c: (0, off + c))
    return pl.pallas_call(
        body, name=name, grid=(ncb,),
        in_specs=[col(0), col(ncb), pl.BlockSpec((CONV_TAPS_PADDED, LANES), lambda c: (0, c)),
                  pl.BlockSpec((1, LANES), lambda c: (0, c))],
        out_specs=pl.BlockSpec((t, LANES), lambda c: (0, c)),
        out_shape=jax.ShapeDtypeStruct((t, cdim), F32),
        scratch_shapes=[pltpu.VMEM((t + HALO, LANES), F32)], compiler_params=_cparams(("arbitrary",)),
    )(u, u, w, b.reshape(1, cdim))


def _conv_bwd(dzc, u, w, cdim, name):
    t = u.shape[0]
    ncb = cdim // LANES
    nt = t // BLOCK
    win_rows = BLOCK + HALO

    def body(dzc_ref, a_ref, g_ref, w_ref, da_ref, dg_ref, dw_ref, db_ref, zpad, dpad):
        av = a_ref[...].astype(F32)
        sg = _sigmoid(g_ref[...].astype(F32))
        zpad[0:HALO, :] = jnp.zeros((HALO, LANES), F32)
        zpad[HALO:HALO + t, :] = av * sg
        dpad[0:t, :] = dzc_ref[...]
        dpad[t:t + HALO, :] = jnp.zeros((HALO, LANES), F32)
        dw_ref[...] = jnp.zeros_like(dw_ref)
        db_ref[...] = jnp.sum(dzc_ref[...], axis=0, keepdims=True)
        wv = w_ref[...]

        def tile(i, carry):
            r0 = pl.multiple_of(i * BLOCK, BLOCK)
            zwin = zpad[pl.ds(r0, win_rows), :]
            dwin = dpad[pl.ds(r0, win_rows), :]
            dcur = dwin[0:BLOCK, :]
            dz = jnp.zeros((BLOCK, LANES), F32)
            for b in range(SUBLANES):
                zs = zwin if b == 0 else pltpu.roll(zwin, b, 0)
                ds = dwin if b == 0 else pltpu.roll(dwin, win_rows - b, 0)
                for a in range(HALO // SUBLANES):
                    j = CONV_WIDTH - 1 - (SUBLANES * a + b)
                    if j >= 0:
                        lo = HALO - SUBLANES * a
                        dw_ref[j:j + 1, :] += jnp.sum(dcur * zs[lo:lo + BLOCK, :], axis=0, keepdims=True)
                        dz = dz + wv[j:j + 1, :] * ds[SUBLANES * a:SUBLANES * a + BLOCK, :]
            ac = a_ref[pl.ds(r0, BLOCK), :].astype(F32)
            sc = _sigmoid(g_ref[pl.ds(r0, BLOCK), :].astype(F32))
            da_ref[pl.ds(r0, BLOCK), :] = (dz * sc).astype(BF16)
            dg_ref[pl.ds(r0, BLOCK), :] = (dz * ac * sc * (1.0 - sc)).astype(BF16)
            return carry

        lax.fori_loop(0, nt, tile, 0)

    col = lambda off: pl.BlockSpec((t, LANES), lambda c: (0, off + c))
    wspec = pl.BlockSpec((CONV_TAPS_PADDED, LANES), lambda c: (0, c))
    o = jax.ShapeDtypeStruct((t, cdim), BF16)
    return pl.pallas_call(
        body, name=name, grid=(ncb,), in_specs=[col(0), col(0), col(ncb), wspec],
        out_specs=(col(0), col(0), wspec, pl.BlockSpec((1, LANES), lambda c: (0, c))),
        out_shape=(o, o, jax.ShapeDtypeStruct((CONV_TAPS_PADDED, cdim), F32), jax.ShapeDtypeStruct((1, cdim), F32)),
        scratch_shapes=[pltpu.VMEM((t + HALO, LANES), F32), pltpu.VMEM((t + HALO, LANES), F32)],
        compiler_params=_cparams(("arbitrary",)),
    )(dzc, u, u, w)


def _ln_swish_fwd(zc, g, b, name):
    t, c = zc.shape
    ROWS = _rows(t, 6 * c)

    def body(z_ref, g_ref, b_ref, o_ref):
        z = z_ref[...]
        mu = jnp.mean(z, axis=-1, keepdims=True)
        zc_ = z - mu
        zn = zc_ * lax.rsqrt(jnp.mean(zc_ * zc_, axis=-1, keepdims=True) + EPS)
        y = zn * g_ref[...] + b_ref[...]
        o_ref[...] = (y * _sigmoid(y)).astype(BF16)

    row = pl.BlockSpec((ROWS, c), lambda i: (i, 0))
    vec = pl.BlockSpec((1, c), lambda i: (0, 0))
    return pl.pallas_call(
        body, name=name, grid=(t // ROWS,), in_specs=[row, vec, vec], out_specs=row,
        out_shape=jax.ShapeDtypeStruct((t, c), BF16), compiler_params=_cparams(("arbitrary",)),
    )(zc, g.reshape(1, c), b.reshape(1, c))


def _ln_swish_bwd(dzs, zc, g, b, name):
    t, c = zc.shape
    ROWS = _rows(t, 12 * c)

    def body(d_ref, z_ref, g_ref, b_ref, dz_ref, dg_ref, db_ref):
        @pl.when(pl.program_id(0) == 0)
        def _():
            dg_ref[...] = jnp.zeros_like(dg_ref)
            db_ref[...] = jnp.zeros_like(db_ref)

        z = z_ref[...]
        mu = jnp.mean(z, axis=-1, keepdims=True)
        zc_ = z - mu
        rstd = lax.rsqrt(jnp.mean(zc_ * zc_, axis=-1, keepdims=True) + EPS)
        zn = zc_ * rstd
        y = zn * g_ref[...] + b_ref[...]
        sg = _sigmoid(y)
        dy = d_ref[...] * (sg * (1.0 + y * (1.0 - sg)))
        dg_ref[...] += jnp.sum(dy * zn, axis=0, keepdims=True)
        db_ref[...] += jnp.sum(dy, axis=0, keepdims=True)
        dzn = dy * g_ref[...]
        dz_ref[...] = rstd * (dzn - jnp.mean(dzn, axis=-1, keepdims=True)
                              - zn * jnp.mean(dzn * zn, axis=-1, keepdims=True))

    row = pl.BlockSpec((ROWS, c), lambda i: (i, 0))
    vec = pl.BlockSpec((1, c), lambda i: (0, 0))
    v = jax.ShapeDtypeStruct((1, c), F32)
    return pl.pallas_call(
        body, name=name, grid=(t // ROWS,), in_specs=[row, row, vec, vec], out_specs=(row, vec, vec),
        out_shape=(jax.ShapeDtypeStruct((t, c), F32), v, v), compiler_params=_cparams(("arbitrary",)),
    )(dzs, zc, g.reshape(1, c), b.reshape(1, c))


def _bucket_table():
    qi = np.arange(BLOCK)[:, None]
    kj = np.arange(2 * BLOCK)[None, :]
    off = qi + BLOCK - kj
    band = (off >= 0) & (off <= SUB_WINDOW)
    max_exact = NUM_BUCKETS // 2
    out = []
    for d in DILATIONS:
        dist = (np.clip(off, 0, SUB_WINDOW) * d).astype(np.int32)
        nf = np.maximum(dist, 1).astype(np.float32)
        large = max_exact + (np.log(nf / np.float32(max_exact)) / np.float32(math.log(MAX_REL_DISTANCE / max_exact))
                             * np.float32(NUM_BUCKETS - max_exact)).astype(np.int32)
        large = np.minimum(large, NUM_BUCKETS - 1)
        bucket = np.where(dist < max_exact, dist, large)
        out.append(np.where(band, bucket, -1))
    return np.stack(out).astype(np.int32)


def _bias_expand(rel_bias, buckets, hpg, name):
    nh = N_GROUPS * hpg

    def body(rb_ref, bk_ref, o_ref):
        h = pl.program_id(0)
        bk = bk_ref[0]
        acc = jnp.full((BLOCK, 2 * BLOCK), NEG_INF, F32)
        for bb in range(NUM_BUCKETS):
            acc = jnp.where(bk == bb, rb_ref[bb, h], acc)
        o_ref[0] = acc

    return pl.pallas_call(
        body, name=name, grid=(nh,),
        in_specs=[pl.BlockSpec(memory_space=pltpu.SMEM),
                  pl.BlockSpec((1, BLOCK, 2 * BLOCK), lambda h: (h // hpg, 0, 0))],
        out_specs=pl.BlockSpec((1, BLOCK, 2 * BLOCK), lambda h: (h, 0, 0)),
        out_shape=jax.ShapeDtypeStruct((nh, BLOCK, 2 * BLOCK), F32), compiler_params=_cparams(("arbitrary",)),
    )(rel_bias, buckets)


def _bias_reduce(ds_sum, buckets, hpg, name):
    nh = N_GROUPS * hpg

    def body(ds_ref, bk_ref, o_ref):
        bk = bk_ref[0]
        dsv = ds_ref[0]
        lane = lax.broadcasted_iota(jnp.int32, (1, LANES), 1)
        row = jnp.zeros((1, LANES), F32)
        for bb in range(NUM_BUCKETS):
            tot = jnp.sum(jnp.sum(jnp.where(bk == bb, dsv, 0.0), axis=-1, keepdims=True), axis=0, keepdims=True)
            row = jnp.where(lane == bb, tot, row)
        o_ref[0] = row

    return pl.pallas_call(
        body, name=name, grid=(nh,),
        in_specs=[pl.BlockSpec((1, BLOCK, 2 * BLOCK), lambda h: (h, 0, 0)),
                  pl.BlockSpec((1, BLOCK, 2 * BLOCK), lambda h: (h // hpg, 0, 0))],
        out_specs=pl.BlockSpec((1, 1, LANES), lambda h: (h, 0, 0)),
        out_shape=jax.ShapeDtypeStruct((nh, 1, LANES), F32), compiler_params=_cparams(("arbitrary",)),
    )(ds_sum, buckets)


def _chunk_rows(c, d, nb):
    r, n = c // nb, c % nb
    if d == 1:
        return pl.ds(c * BLOCK, BLOCK)
    return pl.ds(r + n * BLOCK * d, BLOCK, stride=d)


def _segment_ones():
    i = lax.broadcasted_iota(jnp.int32, (LANES, LANES), 0) // HEAD_DIM
    j = lax.broadcasted_iota(jnp.int32, (LANES, LANES), 1) // HEAD_DIM
    return (i == j).astype(BF16)


def _segment_sum(v, seg):
    hi = v.astype(BF16)
    lo = (v - hi.astype(F32)).astype(BF16)
    return jnp.dot(hi, seg, preferred_element_type=F32) + jnp.dot(lo, seg, preferred_element_type=F32)


def _head_mean(v, seg):
    return _segment_sum(v, seg) * (1.0 / HEAD_DIM)


def _attn_fwd(u, qg2, kg2, bias, gi, cols, hpg, name):
    t = u.shape[0]
    d = DILATIONS[gi]
    nchunk = t // BLOCK
    nb = (t // d) // BLOCK
    hp = hpg // 2
    qc0, kc0, vc0 = [(c + gi * hpg * HEAD_DIM) // LANES for c in cols]
    contract_lanes = (((1,), (1,)), ((), ()))

    def body(q_ref, k_ref, v_ref, qg_ref, kg_ref, bias_ref, o_ref, lse_ref, qd, kd, vd, od, ld, sbuf):
        seg = _segment_ones()
        lane = lax.broadcasted_iota(jnp.int32, (1, LANES), 1)
        qg = qg_ref[...] * (HEAD_DIM ** -0.5)
        kg = kg_ref[...]
        kd[0:BLOCK, :] = jnp.zeros((BLOCK, LANES), BF16)
        vd[0:BLOCK, :] = jnp.zeros((BLOCK, LANES), BF16)
        od[...] = q_ref[...].astype(F32)
        ld[...] = k_ref[...].astype(F32)
        for c in range(nchunk):
            rows = _chunk_rows(c, d, nb)
            qv = od[rows, :]
            kv = ld[rows, :]
            qd[c * BLOCK:(c + 1) * BLOCK, :] = (qv * lax.rsqrt(_head_mean(qv * qv, seg) + EPS) * qg).astype(BF16)
            kd[(c + 1) * BLOCK:(c + 2) * BLOCK, :] = (kv * lax.rsqrt(_head_mean(kv * kv, seg) + EPS) * kg).astype(BF16)
        od[...] = v_ref[...].astype(F32)
        for c in range(nchunk):
            vd[(c + 1) * BLOCK:(c + 2) * BLOCK, :] = od[_chunk_rows(c, d, nb), :].astype(BF16)

        col = lax.broadcasted_iota(jnp.int32, (BLOCK, 2 * BLOCK), 1)
        for j in range(2):
            mj = jnp.logical_and(lane >= j * HEAD_DIM, lane < (j + 1) * HEAD_DIM)
            for c in range(nchunk):
                kw = kd[c * BLOCK:(c + 2) * BLOCK, :]
                kj = jnp.where(mj, kw, jnp.zeros_like(kw))
                s = lax.dot_general(qd[c * BLOCK:(c + 1) * BLOCK, :], kj, contract_lanes,
                                    preferred_element_type=F32) + bias_ref[j]
                if c % nb == 0:
                    s = jnp.where(col < BLOCK, NEG_INF, s)
                sbuf[c] = s
            for c in range(nchunk):
                rows = slice(c * BLOCK, (c + 1) * BLOCK)
                s = sbuf[c]
                mx = jnp.max(s, axis=-1, keepdims=True)
                p = jnp.exp(s - mx).astype(BF16)
                vw = vd[c * BLOCK:(c + 2) * BLOCK, :]
                oj = jnp.dot(p, jnp.where(mj, vw, jnp.ones_like(vw)), preferred_element_type=F32)
                l = pltpu.roll(oj, HEAD_DIM, 1)
                on = oj / l
                ls = mx + jnp.log(l)
                if j == 0:
                    od[rows, :] = on
                    ld[rows, :] = ls
                else:
                    od[rows, :] = jnp.where(mj, on, od[rows, :])
                    ld[rows, :] = jnp.where(mj, ls, ld[rows, :])

        for c in range(nchunk):
            rows = _chunk_rows(c, d, nb)
            o_ref[rows, :] = od[c * BLOCK:(c + 1) * BLOCK, :]
            lse_ref[rows, :] = ld[c * BLOCK:(c + 1) * BLOCK, :]

    ucol = lambda c0: pl.BlockSpec((t, LANES), lambda h: (0, c0 + h))
    vec = pl.BlockSpec((1, LANES), lambda h: (0, 0))
    oblk = pl.BlockSpec((t, LANES), lambda h: (0, h))
    osh = jax.ShapeDtypeStruct((t, hpg * HEAD_DIM), F32)
    return pl.pallas_call(
        body, name=name, grid=(hp,),
        in_specs=[ucol(qc0), ucol(kc0), ucol(vc0), vec, vec,
                  pl.BlockSpec((2, BLOCK, 2 * BLOCK), lambda h: (gi * hp + h, 0, 0))],
        out_specs=(oblk, oblk), out_shape=(osh, osh),
        scratch_shapes=[pltpu.VMEM((t, LANES), BF16), pltpu.VMEM((t + BLOCK, LANES), BF16),
                        pltpu.VMEM((t + BLOCK, LANES), BF16), pltpu.VMEM((t, LANES), F32), pltpu.VMEM((t, LANES), F32),
                        pltpu.VMEM((nchunk, BLOCK, 2 * BLOCK), F32)],
        compiler_params=_cparams(("arbitrary",)),
    )(u, u, u, qg2, kg2, bias)


def _attn_bwd(u, do_g, dd_g, lse_g, qg2, kg2, bias, ds_in, dqkv_in, gi, cols, hpg, name):
    t = u.shape[0]
    d = DILATIONS[gi]
    nchunk = t // BLOCK
    nb = (t // d) // BLOCK
    hp = hpg // 2
    qc0, kc0, vc0 = [(c + gi * hpg * HEAD_DIM) // LANES for c in cols]
    contract_lanes = (((1,), (1,)), ((), ()))
    contract_rows = (((0,), (0,)), ((), ()))
    qscale = HEAD_DIM ** -0.5

    def body(q_ref, k_ref, v_ref, do_ref, dd_ref, lse_ref, qg_ref, kg_ref, bias_ref, dsin_ref, _dq_in, _dk_in, _dv_in,
             dq_ref, dk_ref, dv_ref, dgq_ref, dgk_ref, dsout_ref,
             qd, kd, vd, dod, ddd, ld, dqd, dkd, dvd, dsacc, pbuf, dsbuf, qs, ks):
        seg = _segment_ones()
        lane = lax.broadcasted_iota(jnp.int32, (1, LANES), 1)
        qg = qg_ref[...] * qscale
        kg = kg_ref[...]
        kd[0:BLOCK, :] = jnp.zeros((BLOCK, LANES), BF16)
        vd[0:BLOCK, :] = jnp.zeros((BLOCK, LANES), BF16)
        dsacc[...] = jnp.zeros_like(dsacc)
        qs[...] = q_ref[...].astype(F32)
        ks[...] = k_ref[...].astype(F32)
        dqd[...] = v_ref[...].astype(F32)
        for c in range(nchunk):
            rows = _chunk_rows(c, d, nb)
            qv = qs[rows, :]
            kv = ks[rows, :]
            qd[c * BLOCK:(c + 1) * BLOCK, :] = (qv * lax.rsqrt(_head_mean(qv * qv, seg) + EPS) * qg).astype(BF16)
            kd[(c + 1) * BLOCK:(c + 2) * BLOCK, :] = (kv * lax.rsqrt(_head_mean(kv * kv, seg) + EPS) * kg).astype(BF16)
            vd[(c + 1) * BLOCK:(c + 2) * BLOCK, :] = dqd[rows, :].astype(BF16)
            dod[c * BLOCK:(c + 1) * BLOCK, :] = do_ref[rows, :].astype(BF16)
            ddd[c * BLOCK:(c + 1) * BLOCK, :] = dd_ref[rows, :]
            ld[c * BLOCK:(c + 1) * BLOCK, :] = lse_ref[rows, :]

        col = lax.broadcasted_iota(jnp.int32, (BLOCK, 2 * BLOCK), 1)
        for j in range(2):
            mj = jnp.logical_and(lane >= j * HEAD_DIM, lane < (j + 1) * HEAD_DIM)
            first = lane == j * HEAD_DIM
            for c in range(nchunk):
                rows = slice(c * BLOCK, (c + 1) * BLOCK)
                kw = kd[c * BLOCK:(c + 2) * BLOCK, :]
                vw = vd[c * BLOCK:(c + 2) * BLOCK, :]
                kj = jnp.where(mj, kw, jnp.zeros_like(kw))
                vj = jnp.where(mj, vw, jnp.zeros_like(vw))
                s = lax.dot_general(qd[rows, :], kj, contract_lanes, preferred_element_type=F32) + bias_ref[j]
                if c % nb == 0:
                    s = jnp.where(col < BLOCK, NEG_INF, s)
                dp = lax.dot_general(dod[rows, :], vj, contract_lanes, preferred_element_type=F32)
                lse_j = jnp.sum(jnp.where(first, ld[rows, :], 0.0), axis=-1, keepdims=True)
                dd_j = jnp.sum(jnp.where(first, ddd[rows, :], 0.0), axis=-1, keepdims=True)
                p = jnp.exp(s - lse_j)
                ds = p * (dp + dd_j)
                dsacc[j] += ds
                pbuf[j, c] = p.astype(BF16)
                dsbuf[j, c] = ds.astype(BF16)
        for c in range(nchunk):
            rows = slice(c * BLOCK, (c + 1) * BLOCK)
            has_next = c + 1 < nchunk and (c + 1) % nb != 0
            kw = kd[c * BLOCK:(c + 2) * BLOCK, :]
            dq = jnp.zeros((BLOCK, LANES), F32)
            dk = jnp.zeros((BLOCK, LANES), F32)
            dv = jnp.zeros((BLOCK, LANES), F32)
            both = slice(c * BLOCK, (c + 2) * BLOCK) if has_next else rows
            for j in range(2):
                mj = jnp.logical_and(lane >= j * HEAD_DIM, lane < (j + 1) * HEAD_DIM)
                dq = dq + jnp.dot(dsbuf[j, c], jnp.where(mj, kw, jnp.zeros_like(kw)), preferred_element_type=F32)
                dsk = dsbuf[j, c, :, BLOCK:]
                pk = pbuf[j, c, :, BLOCK:]
                if has_next:
                    dsk = jnp.concatenate([dsk, dsbuf[j, c + 1, :, :BLOCK]], axis=0)
                    pk = jnp.concatenate([pk, pbuf[j, c + 1, :, :BLOCK]], axis=0)
                qq = qd[both, :]
                dd = dod[both, :]
                dk = dk + lax.dot_general(dsk, jnp.where(mj, qq, jnp.zeros_like(qq)), contract_rows,
                                          preferred_element_type=F32)
                dv = dv + lax.dot_general(pk, jnp.where(mj, dd, jnp.zeros_like(dd)), contract_rows,
                                          preferred_element_type=F32)
            dqd[rows, :] = dq
            dkd[rows, :] = dk
            dvd[rows, :] = dv

        dsout_ref[...] = dsin_ref[...] + dsacc[...]

        dgq = jnp.zeros((1, LANES), F32)
        dgk = jnp.zeros((1, LANES), F32)
        for c in range(nchunk):
            rows = _chunk_rows(c, d, nb)
            qv = qs[rows, :]
            rq = lax.rsqrt(_head_mean(qv * qv, seg) + EPS)
            qh = qv * rq
            dy = dqd[c * BLOCK:(c + 1) * BLOCK, :]
            dgq = dgq + jnp.sum(dy * qh, axis=0, keepdims=True) * qscale
            dxh = dy * qg
            ddd[rows, :] = rq * (dxh - qh * _head_mean(dxh * qh, seg))
            kv = ks[rows, :]
            rk = lax.rsqrt(_head_mean(kv * kv, seg) + EPS)
            kh = kv * rk
            dy = dkd[c * BLOCK:(c + 1) * BLOCK, :]
            dgk = dgk + jnp.sum(dy * kh, axis=0, keepdims=True)
            dxh = dy * kg
            ld[rows, :] = rk * (dxh - kh * _head_mean(dxh * kh, seg))
        dq_ref[...] = ddd[...].astype(BF16)
        dk_ref[...] = ld[...].astype(BF16)
        for c in range(nchunk):
            ddd[_chunk_rows(c, d, nb), :] = dvd[c * BLOCK:(c + 1) * BLOCK, :]
        dv_ref[...] = ddd[...].astype(BF16)
        dgq_ref[0] = dgq
        dgk_ref[0] = dgk

    ucol = lambda c0: pl.BlockSpec((t, LANES), lambda h: (0, c0 + h))
    vec = pl.BlockSpec((1, LANES), lambda h: (0, 0))
    oblk = pl.BlockSpec((t, LANES), lambda h: (0, h))
    bblk = pl.BlockSpec((2, BLOCK, 2 * BLOCK), lambda h: (gi * hp + h, 0, 0))
    gblk = pl.BlockSpec((1, 1, LANES), lambda h: (h, 0, 0))
    osh = jax.ShapeDtypeStruct((t, N_GROUPS * hpg * HEAD_DIM), BF16)
    gsh = jax.ShapeDtypeStruct((hp, 1, LANES), F32)
    gcol = pl.BlockSpec((t, LANES), lambda h: (0, gi * hp + h))
    hbm = pl.BlockSpec(memory_space=pl.ANY)
    return pl.pallas_call(
        body, name=name, grid=(hp,),
        in_specs=[ucol(qc0), ucol(kc0), ucol(vc0), oblk, oblk, oblk, vec, vec, bblk, bblk, hbm, hbm, hbm],
        out_specs=(gcol, gcol, gcol, gblk, gblk, bblk),
        out_shape=(osh, osh, osh, gsh, gsh, jax.ShapeDtypeStruct(ds_in.shape, F32)),
        input_output_aliases={9: 5, 10: 0, 11: 1, 12: 2},
        scratch_shapes=[pltpu.VMEM((t, LANES), BF16), pltpu.VMEM((t + BLOCK, LANES), BF16),
                        pltpu.VMEM((t + BLOCK, LANES), BF16), pltpu.VMEM((t, LANES), BF16),
                        pltpu.VMEM((t, LANES), F32), pltpu.VMEM((t, LANES), F32), pltpu.VMEM((t, LANES), F32),
                        pltpu.VMEM((t, LANES), F32), pltpu.VMEM((t, LANES), F32),
                        pltpu.VMEM((2, BLOCK, 2 * BLOCK), F32), pltpu.VMEM((2, nchunk, BLOCK, 2 * BLOCK), BF16),
                        pltpu.VMEM((2, nchunk, BLOCK, 2 * BLOCK), BF16), pltpu.VMEM((t, LANES), F32),
                        pltpu.VMEM((t, LANES), F32)],
        compiler_params=_cparams(("arbitrary",)),
    )(u, u, u, do_g, dd_g, lse_g, qg2, kg2, bias, ds_in, *dqkv_in)


def _group_weights(l0, l1, l2):
    mx = jnp.maximum(jnp.maximum(l0, l1), l2)
    e0, e1, e2 = jnp.exp(l0 - mx), jnp.exp(l1 - mx), jnp.exp(l2 - mx)
    inv = 1.0 / (e0 + e1 + e2)
    return e0 * inv, e1 * inv, e2 * inv


def _combine_fwd(os_, lses, name):
    t, ao = os_[0].shape
    ROWS = _rows(t, 26 * ao)

    def body(o0, o1, o2, l0, l1, l2, o_ref):
        w0, w1, w2 = _group_weights(l0[...], l1[...], l2[...])
        o_ref[...] = (w0 * o0[...] + w1 * o1[...] + w2 * o2[...]).astype(BF16)

    row = pl.BlockSpec((ROWS, ao), lambda i: (i, 0))
    return pl.pallas_call(
        body, name=name, grid=(t // ROWS,), in_specs=[row] * 6, out_specs=row,
        out_shape=jax.ShapeDtypeStruct((t, ao), BF16), compiler_params=_cparams(("arbitrary",)),
    )(*os_, *lses)


def _combine_bwd(do, os_, lses, name):
    t, ao = do.shape
    idx = np.arange(ao) // HEAD_DIM
    seg = jnp.asarray((idx[:, None] == idx[None, :]).astype(np.float32), dtype=BF16)
    ROWS = _rows(t, 52 * ao)

    def body(do_ref, o0, o1, o2, l0, l1, l2, seg_ref, g0, g1, g2, d0, d1, d2):
        w0, w1, w2 = _group_weights(l0[...], l1[...], l2[...])
        dov = do_ref[...]
        o = w0 * o0[...] + w1 * o1[...] + w2 * o2[...]
        sd = _segment_sum(dov * o, seg_ref[...])
        for w, gref, dref in ((w0, g0, d0), (w1, g1, d1), (w2, g2, d2)):
            gref[...] = w * dov
            dref[...] = -(w * sd)

    row = pl.BlockSpec((ROWS, ao), lambda i: (i, 0))
    sh = jax.ShapeDtypeStruct((t, ao), F32)
    outs = pl.pallas_call(
        body, name=name, grid=(t // ROWS,), in_specs=[row] * 7 + [pl.BlockSpec((ao, ao), lambda i: (0, 0))],
        out_specs=(row,) * 6, out_shape=(sh,) * 6, compiler_params=_cparams(("arbitrary",)),
    )(do, *os_, *lses, seg)
    return outs[:3], outs[3:]


def _adamw(w, g, m, v, name):
    shape = w.shape
    cols = shape[-1]
    rows = int(np.prod(shape[:-1]))
    tr = rows if rows <= 512 else _tile_rows(rows)
    c1 = 1.0 - ADAM_B1 ** ADAM_STEP
    c2 = 1.0 - ADAM_B2 ** ADAM_STEP

    def body(w_ref, g_ref, m_ref, v_ref, d_ref, nm_ref, nv_ref):
        gv = g_ref[...]
        mn = ADAM_B1 * m_ref[...] + (1.0 - ADAM_B1) * gv
        vn = ADAM_B2 * v_ref[...] + (1.0 - ADAM_B2) * (gv * gv)
        nm_ref[...] = mn
        nv_ref[...] = vn
        d_ref[...] = -ADAM_LR * ((mn / c1) / (jnp.sqrt(vn / c2) + ADAM_EPS) + ADAM_WD * w_ref[...])

    blk = pl.BlockSpec((tr, cols), lambda i: (i, 0))
    sh = jax.ShapeDtypeStruct((rows, cols), F32)
    outs = pl.pallas_call(
        body, name=name, grid=(rows // tr,), in_specs=[blk] * 4, out_specs=(blk,) * 3, out_shape=(sh,) * 3,
        compiler_params=_cparams(("arbitrary",)),
    )(*[a.reshape(rows, cols) for a in (w, g, m, v)])
    return tuple(o.reshape(shape) for o in outs)


def _tile_rows(rows):
    for t in (512, 256, 128, 64, 32, 16, 8):
        if rows % t == 0:
            return t
    return rows


def _sum_slots(recv, parts, me, name):
    _, rows, cols = recv.shape
    tr = rows if rows <= 512 else _tile_rows(rows)

    def body(me_ref, r_ref, own_ref, o_ref):
        acc = jnp.zeros(o_ref.shape, F32)
        for s in range(N_DEV):
            acc = acc + jnp.where(me_ref[0] == s, own_ref[...], r_ref[s]).astype(F32)
        o_ref[...] = acc

    return pl.pallas_call(
        body, name=name,
        grid_spec=pltpu.PrefetchScalarGridSpec(
            num_scalar_prefetch=1, grid=(rows // tr,),
            in_specs=[pl.BlockSpec((N_DEV, tr, cols), lambda i, me: (0, i, 0)),
                      pl.BlockSpec((None, tr, cols), lambda i, me: (me[0], i, 0))],
            out_specs=pl.BlockSpec((tr, cols), lambda i, me: (i, 0))),
        out_shape=jax.ShapeDtypeStruct((rows, cols), F32), compiler_params=_cparams(("arbitrary",)),
    )(me.reshape(1), recv, parts)


def _peer(k):
    x, y, c = lax.axis_index("x"), lax.axis_index("y"), lax.axis_index("c")
    return (1 - x if k & 4 else x, 1 - y if k & 2 else y, 1 - c if k & 1 else c)


def _dev_index(p):
    return 4 * p[0] + 2 * p[1] + p[2]


HBM_SPEC = pl.BlockSpec(memory_space=pltpu.HBM)
SEM_SPEC = pl.BlockSpec(memory_space=pltpu.SEMAPHORE)
ANY_SPEC = pl.BlockSpec(memory_space=pl.ANY)
CHIPS = (4, 2, 6)


def _remote(src, dst, send_sem, recv_sem, to):
    return pltpu.make_async_remote_copy(src_ref=src, dst_ref=dst, send_sem=send_sem, recv_sem=recv_sem,
                                        device_id=to, device_id_type=MESH)


def _hbm(a):
    return pltpu.with_memory_space_constraint(a, pltpu.HBM)


def _split_call(body, name, bufs, sems_in, sem_out_sizes, after):
    nb, ns, no = len(bufs), len(sems_in), len(sem_out_sizes)
    extra = [] if after is None else [after]

    def kern(*refs):
        pos = nb + ns + len(extra)
        body(refs[:nb], refs[nb:nb + ns], refs[pos:pos + no])
        token_ref = refs[pos + no + nb]
        token_ref[...] = jnp.zeros_like(token_ref)

    out_shape = (tuple(pltpu.SemaphoreType.DMA((s,)) for s in sem_out_sizes)
                 + tuple(pltpu.HBM(b.shape, b.dtype) for b in bufs) + (jax.ShapeDtypeStruct((8, LANES), F32),))
    res = pl.pallas_call(
        kern, name=name, out_shape=out_shape,
        in_specs=[HBM_SPEC] * nb + [SEM_SPEC] * ns + [ANY_SPEC] * len(extra),
        out_specs=(SEM_SPEC,) * no + (HBM_SPEC,) * nb + (pl.BlockSpec(memory_space=pltpu.VMEM),),
        input_output_aliases={i: no + i for i in range(nb)},
        compiler_params=pltpu.CompilerParams(has_side_effects=pltpu.SideEffectType.DATAFLOW_SIDE_EFFECTING),
    )(*bufs, *sems_in, *extra)
    return res[:no], res[no:no + nb], res[no + nb]


def _gather_start(shards, lands, after, name):
    n = len(shards)

    def body(bufs, _, sems):
        ins, lnd = bufs[:n], bufs[n:]
        d2d_s, d2d_r, ici_s, ici_r = sems
        me = _dev_index(_peer(0))
        for j, k in enumerate(CHIPS):
            for i in range(n):
                _remote(ins[i], lnd[i].at[me], ici_s.at[j], ici_r.at[j], _peer(k)).start()
        for i in range(n):
            _remote(ins[i], lnd[i].at[me], d2d_s.at[0], d2d_r.at[0], _peer(1)).start()

    return _split_call(body, name, [_hbm(a) for a in (*shards, *lands)], [], (1, 1, 3, 3), after)


def _gather_forward(n, bufs, ici_r, after, name):
    def body(refs, sems_in, sems):
        ins, lnd = refs[:n], refs[n:]
        (arrived,) = sems_in
        fwd_s, fwd_r = sems
        for j, k in enumerate(CHIPS):
            blk = _dev_index(_peer(k))
            for i in range(n):
                _remote(ins[i], lnd[i].at[blk], fwd_s.at[j], arrived.at[j], _peer(k)).wait_recv()
            for i in range(n):
                _remote(lnd[i].at[blk], lnd[i].at[blk], fwd_s.at[j], fwd_r.at[j], _peer(1)).start()

    return _split_call(body, name, bufs, [ici_r], (3, 3), after)


def _gather_finish(n, bufs, d2d_s, d2d_r, ici_s, fwd_s, fwd_r, after, name):
    def body(refs, sems_in, _):
        ins, lnd = refs[:n], refs[n:]
        d2d_send, d2d_recv, ici_send, fwd_send, fwd_recv = sems_in
        sib = _peer(1)
        for i in range(n):
            cp = _remote(ins[i], lnd[i].at[_dev_index(sib)], d2d_send.at[0], d2d_recv.at[0], sib)
            cp.wait_send()
            cp.wait_recv()
        for j, k in enumerate(CHIPS):
            passed = _dev_index(_peer(k))
            landed = _dev_index(_peer(k | 1))
            for i in range(n):
                _remote(ins[i], lnd[i].at[passed], ici_send.at[j], fwd_recv.at[j], _peer(k)).wait_send()
                cp = _remote(lnd[i].at[passed], lnd[i].at[landed], fwd_send.at[j], fwd_recv.at[j], sib)
                cp.wait_send()
                cp.wait_recv()

    _, out, token = _split_call(body, name, bufs, [d2d_s, d2d_r, ici_s, fwd_s, fwd_r], (), after)
    return out[n:], token


def _exchange_start(parts, lands, after, name):
    n = len(parts)

    def body(bufs, _, sems):
        src, lnd = bufs[:n], bufs[n:]
        send, recv = sems
        me = _dev_index(_peer(0))
        for k in (4, 5, 2, 3, 6, 7, 1):
            to = _peer(k)
            for i in range(n):
                _remote(src[i].at[_dev_index(to)], lnd[i].at[me], send.at[k - 1], recv.at[k - 1], to).start()

    return _split_call(body, name, [_hbm(a) for a in (*parts, *lands)], [], (7, 7), after)


def _exchange_finish(n, bufs, send, recv, after, name):
    def body(refs, sems_in, _):
        src, lnd = refs[:n], refs[n:]
        send_, recv_ = sems_in
        me = _dev_index(_peer(0))
        for k in range(1, N_DEV):
            frm = _peer(k)
            for i in range(n):
                cp = _remote(src[i].at[me], lnd[i].at[_dev_index(frm)], send_.at[k - 1], recv_.at[k - 1], frm)
                cp.wait_send()
                cp.wait_recv()

    _, out, token = _split_call(body, name, bufs, [send, recv], (), after)
    return out[:n], out[n:], token


def _all_reduce_small(v, name):
    rows = v.shape[0]

    def body(v_ref, o_ref, buf, send_sems, recv_sems):
        me = _dev_index(_peer(0))
        buf[me] = v_ref[...]
        copies = []
        for k in range(1, N_DEV):
            copies.append(pltpu.make_async_remote_copy(
                src_ref=v_ref, dst_ref=buf.at[me], send_sem=send_sems.at[k - 1], recv_sem=recv_sems.at[k - 1],
                device_id=_peer(k), device_id_type=MESH))
        for cp in copies:
            cp.start()
        for k in range(1, N_DEV):
            pltpu.make_async_remote_copy(
                src_ref=v_ref, dst_ref=buf.at[_dev_index(_peer(k))], send_sem=send_sems.at[k - 1],
                recv_sem=recv_sems.at[k - 1], device_id=_peer(k), device_id_type=MESH).wait_recv()
        for cp in copies:
            cp.wait_send()
        acc = buf[0]
        for s in range(1, N_DEV):
            acc = acc + buf[s]
        o_ref[...] = acc

    vm = pl.BlockSpec(memory_space=pltpu.VMEM)
    return pl.pallas_call(
        body, name=name, in_specs=[vm], out_specs=vm, out_shape=jax.ShapeDtypeStruct(v.shape, F32),
        scratch_shapes=[pltpu.VMEM((N_DEV, rows, LANES), F32), pltpu.SemaphoreType.DMA((7,)),
                        pltpu.SemaphoreType.DMA((7,))],
    )(v)


def _columns(cdim, ao):
    q_col = 2 * cdim
    attn_dim = N_GROUPS * ao
    return (q_col, q_col + attn_dim, q_col + 2 * attn_dim), q_col + 3 * attn_dim


def _layer_fwd(x, h1, sm, bg, get_rest, bias, hpg, next_gain):
    cdim = sm["conv_ln_g"].shape[0]
    ao = hpg * HEAD_DIM
    cols, gate_col = _columns(cdim, ao)
    qg2 = jnp.tile(sm["q_norm_g"], 2).reshape(1, LANES)
    kg2 = jnp.tile(sm["k_norm_g"], 2).reshape(1, LANES)
    u = _mm_in_pairs(h1, bg["w_in"], "mm_in")
    zc = _conv_fwd(u, bg["conv_dw_w"], sm["conv_dw_b"], cdim, "conv_fwd")
    zs = _ln_swish_fwd(zc, sm["conv_ln_g"], sm["conv_ln_b"], "ln_swish_fwd")
    os_, lses = [], []
    for gi in range(N_GROUPS):
        o_g, lse_g = _attn_fwd(u, qg2, kg2, bias, gi, cols, hpg, "attn_fwd_g%d" % gi)
        os_.append(o_g)
        lses.append(lse_g)
    o = _combine_fwd(os_, lses, "combine_fwd")
    bg = {**bg, **get_rest(o)}
    yc = _mm(zs, bg["w_conv_out"], name="mm_conv_out")
    ya = _mm(o, bg["w_attn_out"], name="mm_attn_out")
    mg = _gate_fwd(u, yc, ya, gate_col, "gate_fwd")
    x1, h2 = _mm(mg, bg["w_out"], epi="res_rms", extra=x, gain=sm["norm2_g"], name="mm_out")
    f = _mm(h2, bg["w_ff1"], out_dtype=BF16, name="mm_ff1")
    if next_gain is None:
        x2, h_next = _mm(f, bg["w_ff2"], a_relu2=True, epi="res", extra=x1, name="mm_ff2"), None
    else:
        x2, h_next = _mm(f, bg["w_ff2"], a_relu2=True, epi="res_rms", extra=x1, gain=next_gain, name="mm_ff2")
    saved = dict(x=x, h1=h1, u=u, zc=zc, zs=zs, yc=yc, os=os_, lses=lses, o=o, ya=ya, mg=mg, x1=x1, h2=h2, f=f,
                 qg2=qg2, kg2=kg2)
    return x2, h_next, saved, bg


GRAD_GROUPS = (("w_ff2", "w_ff1"), ("w_out", "w_conv_out", "w_attn_out", "conv_dw_w"), ("w_in",))


def _layer_bwd(dx, s, sm, bg, bias, ds_sum, after, emit):
    cdim = sm["conv_ln_g"].shape[0]
    ao = bg["w_attn_out"].shape[0]
    hpg = ao // HEAD_DIM
    hp = hpg // 2
    cols, gate_col = _columns(cdim, ao)
    g = {}
    df = _mm(dx, bg["w_ff2"], tb=True, epi="drelu2", extra=s["f"], out_dtype=BF16, after=after, name="mm_dff2")
    g["w_ff2"] = _mm(s["f"], dx, ta=True, a_relu2=True, out_dtype=BF16, name="mm_gw_ff2")
    g["w_ff1"] = _mm(s["h2"], df, ta=True, out_dtype=BF16, out_slots=True, name="mm_gw_ff1")
    after = emit(GRAD_GROUPS[0], g)
    dx1, dg2 = _mm(df, bg["w_ff1"], tb=True, epi="rms_bwd", extra=(s["x1"], dx), gain=sm["norm2_g"], after=after,
                   name="mm_dff1")
    g["norm2_g"] = dg2[0]
    dmg = _mm(dx1, bg["w_out"], tb=True, name="mm_dout")
    g["w_out"] = _mm(s["mg"], dx1, ta=True, out_dtype=BF16, name="mm_gw_out")
    dyc, dya, dugc, duga = _gate_bwd(dmg, s["u"], s["yc"], s["ya"], gate_col, "gate_bwd")
    dzs = _mm(dyc, bg["w_conv_out"], tb=True, name="mm_dconv_out")
    g["w_conv_out"] = _mm(s["zs"], dyc, ta=True, out_dtype=BF16, name="mm_gw_conv_out")
    do = _mm(dya, bg["w_attn_out"], tb=True, name="mm_dattn_out")
    g["w_attn_out"] = _mm(s["o"], dya, ta=True, out_dtype=BF16, name="mm_gw_attn_out")
    dzc, dlg, dlb = _ln_swish_bwd(dzs, s["zc"], sm["conv_ln_g"], sm["conv_ln_b"], "ln_swish_bwd")
    g["conv_ln_g"] = dlg[0]
    g["conv_ln_b"] = dlb[0]
    da, dgt, dcw, dcb = _conv_bwd(dzc, s["u"], bg["conv_dw_w"], cdim, "conv_bwd")
    g["conv_dw_w"] = dcw[:CONV_WIDTH].astype(BF16)
    g["conv_dw_b"] = dcb[0]
    after = emit(GRAD_GROUPS[1], g)
    do_gs, dd_gs = _combine_bwd(do, s["os"], s["lses"], "combine_bwd")
    dqkv = [lax.empty((dx.shape[0], N_GROUPS * ao), BF16) for _ in range(3)]
    dgq = jnp.zeros((HEAD_DIM,), F32)
    dgk = jnp.zeros((HEAD_DIM,), F32)
    for gi in range(N_GROUPS):
        *dqkv, gq, gk, ds_sum = _attn_bwd(s["u"], do_gs[gi], dd_gs[gi], s["lses"][gi], s["qg2"], s["kg2"], bias,
                                          ds_sum, dqkv, gi, cols, hpg, "attn_bwd_g%d" % gi)
        dgq = dgq + jnp.sum(gq.reshape(hp * 2, HEAD_DIM), axis=0)
        dgk = dgk + jnp.sum(gk.reshape(hp * 2, HEAD_DIM), axis=0)
    g["q_norm_g"] = dgq
    g["k_norm_g"] = dgk
    du = jnp.concatenate([da, dgt] + dqkv + [dugc, duga], axis=1)
    g["w_in"] = _mm_gw_in_pairs(s["h1"], du, after, "mm_gw_in")
    after = emit(GRAD_GROUPS[2], g)
    dx0, dg1 = _mm_din_pairs(du, bg["w_in"], s["x"], sm["norm1_g"], dx1, after, "mm_din")
    g["norm1_g"] = dg1[0]
    return dx0, g, ds_sum


BIG = ("w_in", "conv_dw_w", "w_conv_out", "w_attn_out", "w_out", "w_ff1", "w_ff2")
COL_SHARDED = ("w_in", "conv_dw_w", "w_conv_out", "w_attn_out", "w_ff1")
SMALL = ("rel_bias", "norm1_g", "q_norm_g", "k_norm_g", "conv_dw_b", "conv_ln_g", "conv_ln_b", "norm2_g")
WEIGHTS = ("rel_bias", "norm1_g", "w_in", "q_norm_g", "k_norm_g", "conv_dw_w", "conv_dw_b", "conv_ln_g", "conv_ln_b",
           "w_conv_out", "w_attn_out", "w_out", "norm2_g", "w_ff1", "w_ff2")


def _to_whole(name, gathered):
    n, a, b = gathered.shape
    if name in COL_SHARDED:
        return gathered.transpose(1, 0, 2).reshape(a, n * b)
    return gathered.reshape(n * a, b)


def _to_slots(name, whole):
    a, b = whole.shape
    if name in COL_SHARDED:
        return whole.reshape(a, N_DEV, b // N_DEV).transpose(1, 0, 2)
    return whole.reshape(N_DEV, a // N_DEV, b)


def _own_slot(block, me):
    land = lax.empty((N_DEV,) + block.shape, block.dtype)
    return lax.dynamic_update_slice(land, block[None], (me,) + (0,) * block.ndim)


def kernel(x, rel_bias, norm1_g, w_in, q_norm_g, k_norm_g, conv_dw_w, conv_dw_b, conv_ln_g, conv_ln_b, w_conv_out, w_attn_out, w_out, norm2_g, w_ff1, w_ff2, loss_target, m_rel_bias, m_norm1_g, m_w_in, m_q_norm_g, m_k_norm_g, m_conv_dw_w, m_conv_dw_b, m_conv_ln_g, m_conv_ln_b, m_w_conv_out, m_w_attn_out, m_w_out, m_norm2_g, m_w_ff1, m_w_ff2, v_rel_bias, v_norm1_g, v_w_in, v_q_norm_g, v_k_norm_g, v_conv_dw_w, v_conv_dw_b, v_conv_ln_g, v_conv_ln_b, v_w_conv_out, v_w_attn_out, v_w_out, v_norm2_g, v_w_ff1, v_w_ff2):
    w = dict(rel_bias=rel_bias, norm1_g=norm1_g, w_in=w_in, q_norm_g=q_norm_g, k_norm_g=k_norm_g, conv_dw_w=conv_dw_w,
             conv_dw_b=conv_dw_b, conv_ln_g=conv_ln_g, conv_ln_b=conv_ln_b, w_conv_out=w_conv_out,
             w_attn_out=w_attn_out, w_out=w_out, norm2_g=norm2_g, w_ff1=w_ff1, w_ff2=w_ff2)
    mom = dict(rel_bias=m_rel_bias, norm1_g=m_norm1_g, w_in=m_w_in, q_norm_g=m_q_norm_g, k_norm_g=m_k_norm_g,
               conv_dw_w=m_conv_dw_w, conv_dw_b=m_conv_dw_b, conv_ln_g=m_conv_ln_g, conv_ln_b=m_conv_ln_b,
               w_conv_out=m_w_conv_out, w_attn_out=m_w_attn_out, w_out=m_w_out, norm2_g=m_norm2_g, w_ff1=m_w_ff1,
               w_ff2=m_w_ff2)
    var = dict(rel_bias=v_rel_bias, norm1_g=v_norm1_g, w_in=v_w_in, q_norm_g=v_q_norm_g, k_norm_g=v_k_norm_g,
               conv_dw_w=v_conv_dw_w, conv_dw_b=v_conv_dw_b, conv_ln_g=v_conv_ln_g, conv_ln_b=v_conv_ln_b,
               w_conv_out=v_w_conv_out, w_attn_out=v_w_attn_out, w_out=v_w_out, norm2_g=v_norm2_g, w_ff1=v_w_ff1,
               w_ff2=v_w_ff2)

    depth = norm1_g.shape[0]
    me = 4 * lax.axis_index("x") + 2 * lax.axis_index("y") + lax.axis_index("c")
    odd_core = lax.axis_index("c") == 1
    hpg = w_attn_out.shape[1] // HEAD_DIM
    buckets = jnp.asarray(_bucket_table())
    bias = _bias_expand(rel_bias, buckets, hpg, "bias_expand")

    first_names = ("w_in", "conv_dw_w")
    rest_names = tuple(k for k in BIG if k not in first_names)

    def chain_start(l, names, after):
        shards = [w[k][l] if k == "conv_dw_w" else w[k][l].astype(BF16) for k in names]
        if "w_in" in names:
            i = names.index("w_in")
            shards[i] = jnp.where(odd_core, jnp.pad(shards[i], ((0, 0), (SHIFT, 0))),
                                  jnp.pad(shards[i], ((0, 0), (0, SHIFT))))
        sems, bufs, token = _gather_start(shards, [_own_slot(s, me) for s in shards], after,
                                          "gather_start_%s_l%d" % (names[0], l))
        return dict(l=l, names=names, sems=sems, bufs=bufs, token=token)

    def chain_forward(ch, after):
        fwd, bufs, token = _gather_forward(len(ch["names"]), ch["bufs"], ch["sems"][3], after,
                                           "gather_forward_%s_l%d" % (ch["names"][0], ch["l"]))
        ch.update(fwd=fwd, bufs=bufs)
        return token

    def chain_finish(ch, after):
        d2d_s, d2d_r, ici_s, _ = ch["sems"]
        gathered, _ = _gather_finish(len(ch["names"]), ch["bufs"], d2d_s, d2d_r, ici_s, ch["fwd"][0], ch["fwd"][1],
                                     after, "gather_finish_%s_l%d" % (ch["names"][0], ch["l"]))
        out = {k: a if k == "w_in" else _to_whole(k, a) for k, a in zip(ch["names"], gathered)}
        if "conv_dw_w" in out:
            out["conv_dw_w"] = jnp.pad(out["conv_dw_w"], ((0, CONV_TAPS_PADDED - CONV_WIDTH), (0, 0)))
        return out

    xs = x[0]
    h1 = _rms_fwd(xs, norm1_g[0], "rms1_fwd")
    saved, bigs, smalls = [], [], []
    chains = {}
    for l in range(depth):
        sm = {k: w[k][l] for k in SMALL if k != "rel_bias"}
        if l == 0:
            first = chain_start(0, first_names, None)
            token = chain_forward(first, None)
            rest = chain_start(0, rest_names, token)
            bg = chain_finish(first, rest["token"])

            def get_rest(o, rest=rest):
                token = chain_forward(rest, o)
                if depth > 1:
                    chains[1] = (chain_start(1, first_names, token),)
                    chains[1] += (chain_start(1, rest_names, chains[1][0]["token"]),)
                    token = chains[1][1]["token"]
                return chain_finish(rest, token)
        elif l == 1:
            first, rest = chains[1]
            token = chain_forward(first, xs)
            if depth > 2:
                chains[2] = chain_start(2, BIG, token)
                token = chains[2]["token"]
            bg = chain_finish(first, token)

            def get_rest(o, rest=rest):
                return chain_finish(rest, chain_forward(rest, o))
        else:
            token = chain_forward(chains[l], xs)
            if l + 1 < depth:
                chains[l + 1] = chain_start(l + 1, BIG, token)
                token = chains[l + 1]["token"]
            whole = chain_finish(chains[l], token)
            bg = {k: whole[k] for k in first_names}

            def get_rest(o, whole=whole):
                return {k: whole[k] for k in rest_names}
        xs, h1, sv, bg = _layer_fwd(xs, h1, sm, bg, get_rest, bias, hpg, norm1_g[l + 1] if l + 1 < depth else None)
        saved.append(sv)
        bigs.append(bg)
        smalls.append(sm)

    loss, dx = _loss_and_grad(xs, loss_target[0], "loss")

    ds_sum = jnp.zeros((N_GROUPS * hpg, BLOCK, 2 * BLOCK), F32)
    g = {k: [None] * depth for k in SMALL if k != "rel_bias"}
    grad_layers = {k: [None] * depth for k in BIG}
    pending = []

    def finish_oldest(after):
        names, l, (send, recv), bufs = pending.pop(0)
        parts, recvd, token = _exchange_finish(len(names), bufs, send, recv, after,
                                               "exchange_finish_%s_l%d" % (names[0], l))
        for k, r, p in zip(names, recvd, parts):
            shp = w[k].shape[1:]
            three = (N_DEV, -1, r.shape[-1])
            total = _sum_slots(r.reshape(three), p.reshape(three), me, "sum_" + k)
            if k == "w_in":
                total = jnp.where(odd_core, total[:, SHIFT:], total[:, :shp[-1]])
            grad_layers[k][l] = total.reshape(shp)
        return token

    def make_emit(l):
        def emit(names, gl):
            parts = [gl[k] if k in ("w_ff1", "w_in") else _to_slots(k, gl[k]) for k in names]
            token = finish_oldest(parts[0]) if len(pending) >= len(GRAD_GROUPS) else None
            lands = [lax.empty(p.shape, p.dtype) for p in parts]
            sems, bufs, token = _exchange_start(parts, lands, token, "exchange_start_%s_l%d" % (names[0], l))
            pending.append((names, l, sems, bufs))
            return token
        return emit

    token = None
    for l in reversed(range(depth)):
        dx, gl, ds_sum = _layer_bwd(dx, saved[l], smalls[l], bigs[l], bias, ds_sum, token, make_emit(l))
        for k in g:
            g[k][l] = gl[k]
        token = None
    while pending:
        finish_oldest(None)
    grad_x = dx

    g = {k: jnp.stack(v) for k, v in g.items()}
    db = _bias_reduce(ds_sum, buckets, hpg, "bias_reduce")
    g["rel_bias"] = db[:, 0, :NUM_BUCKETS].T

    flat = jnp.concatenate([g[k].reshape(-1) for k in SMALL])
    nflat = flat.shape[0]
    rows = -(-nflat // (8 * LANES)) * 8
    packed = jnp.pad(flat, (0, rows * LANES - nflat)).reshape(rows, LANES)
    total = _all_reduce_small(packed, "reduce_small").reshape(-1)
    grad = {}
    off = 0
    for k in SMALL:
        size = int(np.prod(w[k].shape))
        grad[k] = total[off:off + size].reshape(w[k].shape)
        off += size

    for k in BIG:
        grad[k] = jnp.stack(grad_layers[k])

    loss = lax.psum(loss[0, 0], ("x", "y", "c"))
    outs = {k: _adamw(w[k], grad[k], mom[k], var[k], "adamw_" + k) for k in WEIGHTS}
    return (loss, grad_x[None], *[grad[k] for k in WEIGHTS], *[outs[k][0] for k in WEIGHTS],
            *[outs[k][1] for k in WEIGHTS], *[outs[k][2] for k in WEIGHTS])
```

```python
import functools
import math

import numpy as np
import jax
import jax.numpy as jnp
from jax import lax
from jax.experimental import pallas as pl
from jax.experimental.pallas import tpu as pltpu

F32 = jnp.float32
BF16 = jnp.bfloat16

HEAD_DIM = 64
N_GROUPS = 3
DILATIONS = (1, 4, 16)
SUB_WINDOW = 128
BLOCK = 128
CONV_WIDTH = 31
CONV_TAPS_PADDED = 32
NUM_BUCKETS = 32
MAX_REL_DISTANCE = 2048
EPS = 1e-6
NEG_INF = -1e30
LANES = 128
SUBLANES = 8

ADAM_LR = 0.001
ADAM_B1 = 0.9
ADAM_B2 = 0.999
ADAM_EPS = 1e-08
ADAM_WD = 0.01
ADAM_STEP = 10

N_DEV = 8
VMEM_LIMIT = 56 * 1024 * 1024
MESH = pl.DeviceIdType.MESH


def _cparams(sem=None):
    return pltpu.CompilerParams(dimension_semantics=sem, vmem_limit_bytes=VMEM_LIMIT)


def _tile(n, target):
    if n <= target:
        return n
    t = (target // LANES) * LANES
    while t >= LANES:
        if n % t == 0:
            return t
        t -= LANES
    return n


def _sigmoid(v):
    return 1.0 / (1.0 + jnp.exp(-v))


MM_VMEM_BUDGET = 40 * 1024 * 1024


def _rms_apply(x, g):
    return x * lax.rsqrt(jnp.mean(x * x, axis=-1, keepdims=True) + EPS) * g


def _rms_grad(dh, x, g):
    r = lax.rsqrt(jnp.mean(x * x, axis=-1, keepdims=True) + EPS)
    xh = x * r
    dxh = dh * g
    dx = r * (dxh - xh * jnp.mean(dxh * xh, axis=-1, keepdims=True))
    return dx, jnp.sum(dh * xh, axis=0, keepdims=True)


def _mm_tiles(m, n, kdim, a_bytes, b_bytes, io_bytes, whole_rows=False, temps=2):
    def need(tm, tn, tk):
        blocks = 2 * (tm * tk * a_bytes + tk * tn * b_bytes + tm * tn * io_bytes)
        casts = (tm * tk * 2 if a_bytes == 4 else 0) + (tk * tn * 2 if b_bytes == 4 else 0)
        return blocks + casts + temps * tm * tn * 4

    tn = n if whole_rows else _tile(n, 1024)
    while True:
        fits = [(tm * tk, tm, tk) for tm in {_tile(m, c) for c in (1024, 512, 256, 128)}
                for tk in {_tile(kdim, c) for c in (2048, 1024, 512, 256)} if need(tm, tn, tk) <= MM_VMEM_BUDGET]
        if fits:
            _, tm, tk = max(fits)
            return tm, tn, tk
        assert not whole_rows and tn % 256 == 0, "no block size fits the VMEM budget"
        tn //= 2


def _mm(a, b, *, ta=False, tb=False, out_dtype=F32, epi=None, extra=(), gain=None, after=None, out_slots=False,
        a_relu2=False, name):
    extra = tuple(extra) if isinstance(extra, (tuple, list)) else (extra,)
    m = a.shape[1] if ta else a.shape[0]
    kdim = a.shape[0] if ta else a.shape[1]
    n = b.shape[0] if tb else b.shape[1]
    norm = epi in ("res_rms", "rms_bwd")
    io_bytes = (jnp.dtype(out_dtype).itemsize + sum(e.dtype.itemsize for e in extra) + (2 if epi == "res_rms" else 0))
    tm, tn, tk = _mm_tiles(m, n // N_DEV if out_slots else n, kdim, a.dtype.itemsize, b.dtype.itemsize, io_bytes,
                           whole_rows=norm, temps=6 if norm else 2)
    if out_slots:
        assert epi is None and tn == n // N_DEV
    nk = kdim // tk
    a_spec = pl.BlockSpec((tk, tm), lambda i, j, k: (k, i)) if ta else pl.BlockSpec((tm, tk), lambda i, j, k: (i, k))
    b_spec = pl.BlockSpec((tn, tk), lambda i, j, k: (j, k)) if tb else pl.BlockSpec((tk, tn), lambda i, j, k: (k, j))
    o_spec = (pl.BlockSpec((None, tm, tn), lambda i, j, k: (j, i, 0)) if out_slots
              else pl.BlockSpec((tm, tn), lambda i, j, k: (i, j)))
    v_spec = pl.BlockSpec((1, tn), lambda i, j, k: (0, j))
    dims = (((0 if ta else 1,), (1 if tb else 0,)), ((), ()))
    n_extra = len(extra)
    n_in = 2 + n_extra + (gain is not None) + (after is not None)
    n_out = 2 if norm else 1

    def body(*refs):
        a_ref, b_ref = refs[0], refs[1]
        e_refs = refs[2:2 + n_extra]
        g_ref = refs[2 + n_extra] if gain is not None else None
        outs = refs[n_in:n_in + n_out]

        def product():
            av = a_ref[...]
            if a_relu2:
                r = jnp.maximum(av.astype(F32), 0.0)
                av = r * r
            return lax.dot_general(av.astype(BF16), b_ref[...].astype(BF16), dims, preferred_element_type=F32)

        def finish(acc):
            if epi is None:
                outs[0][...] = acc.astype(outs[0].dtype)
            elif epi == "res":
                outs[0][...] = (e_refs[0][...] + acc).astype(outs[0].dtype)
            elif epi == "drelu2":
                outs[0][...] = (acc * (2.0 * jnp.maximum(e_refs[0][...].astype(F32), 0.0))).astype(outs[0].dtype)
            elif epi == "res_rms":
                x1 = e_refs[0][...] + acc
                outs[0][...] = x1
                outs[1][...] = _rms_apply(x1, g_ref[...]).astype(BF16)
            elif epi == "rms_bwd":
                dx, dg = _rms_grad(acc, e_refs[0][...], g_ref[...])
                outs[0][...] = e_refs[1][...] + dx
                i = pl.program_id(0)

                @pl.when(i == 0)
                def _():
                    outs[1][...] = dg

                @pl.when(i > 0)
                def _():
                    outs[1][...] += dg

        if nk == 1:
            finish(product())
            return
        acc_ref = refs[-1]
        k = pl.program_id(2)

        @pl.when(k == 0)
        def _():
            acc_ref[...] = product()

        @pl.when(jnp.logical_and(k > 0, k < nk - 1))
        def _():
            acc_ref[...] += product()

        @pl.when(k == nk - 1)
        def _():
            finish(acc_ref[...] + product())

    in_specs = ([a_spec, b_spec] + [o_spec] * n_extra + ([v_spec] if gain is not None else [])
                + ([pl.BlockSpec(memory_space=pl.ANY)] if after is not None else []))
    if epi == "res_rms":
        out_shape = (jax.ShapeDtypeStruct((m, n), F32), jax.ShapeDtypeStruct((m, n), BF16))
        out_specs = (o_spec, o_spec)
    elif epi == "rms_bwd":
        out_shape = (jax.ShapeDtypeStruct((m, n), F32), jax.ShapeDtypeStruct((1, n), F32))
        out_specs = (o_spec, v_spec)
    else:
        out_shape = jax.ShapeDtypeStruct((N_DEV, m, tn) if out_slots else (m, n), out_dtype)
        out_specs = o_spec
    args = (a, b) + extra + ((gain.reshape(1, n),) if gain is not None else ()) + ((after,) if after is not None else ())
    return pl.pallas_call(
        body, name=name, grid=(m // tm, n // tn, nk), in_specs=in_specs, out_specs=out_specs, out_shape=out_shape,
        scratch_shapes=[pltpu.VMEM((tm, tn), F32)] if nk > 1 else [],
        compiler_params=_cparams(("arbitrary", "arbitrary", "arbitrary")),
    )(*args)


SHIFT = HEAD_DIM


def _pair_blocks(e, o):
    wp = e.shape[-1]
    return e[:, :wp - LANES], e[:, wp - LANES:] + o[:, :LANES], o[:, LANES:]


def _mm_in_pairs(a, wg, name):
    t, kdim = a.shape
    wp = wg.shape[-1]
    ws = wp - SHIFT
    tm = _tile(t, 1024)

    def body(a_ref, e_ref, o_ref, u_ref):
        av = a_ref[...]
        lo, mid, hi = _pair_blocks(e_ref[...], o_ref[...])
        u_ref[:, :wp - LANES] = jnp.dot(av, lo, preferred_element_type=F32).astype(BF16)
        u_ref[:, wp - LANES:wp] = jnp.dot(av, mid, preferred_element_type=F32).astype(BF16)
        u_ref[:, wp:] = jnp.dot(av, hi, preferred_element_type=F32).astype(BF16)

    return pl.pallas_call(
        body, name=name, grid=(N_DEV // 2, t // tm),
        in_specs=[pl.BlockSpec((tm, kdim), lambda p, i: (i, 0)),
                  pl.BlockSpec((None, kdim, wp), lambda p, i: (2 * p, 0, 0)),
                  pl.BlockSpec((None, kdim, wp), lambda p, i: (2 * p + 1, 0, 0))],
        out_specs=pl.BlockSpec((tm, 2 * ws), lambda p, i: (i, p)),
        out_shape=jax.ShapeDtypeStruct((t, N_DEV * ws), BF16), compiler_params=_cparams(("arbitrary", "arbitrary")),
    )(a, wg, wg)


def _mm_din_pairs(du, wg, x, gain, dres, after, name):
    t = du.shape[0]
    _, kdim, wp = wg.shape
    ws = wp - SHIFT
    tm = _tile(t, 512)
    npair = N_DEV // 2
    lanes = (((1,), (1,)), ((), ()))
    extra = [] if after is None else [after]

    def body(d_ref, e_ref, o_ref, x_ref, g_ref, r_ref, *rest):
        dx_ref, dg_ref, acc_ref = rest[-3], rest[-2], rest[-1]
        i, p = pl.program_id(0), pl.program_id(1)
        lo, mid, hi = _pair_blocks(e_ref[...], o_ref[...])
        part = (lax.dot_general(d_ref[:, :wp - LANES], lo, lanes, preferred_element_type=F32)
                + lax.dot_general(d_ref[:, wp - LANES:wp], mid, lanes, preferred_element_type=F32)
                + lax.dot_general(d_ref[:, wp:], hi, lanes, preferred_element_type=F32))

        @pl.when(p == 0)
        def _():
            acc_ref[...] = part

        @pl.when(jnp.logical_and(p > 0, p < npair - 1))
        def _():
            acc_ref[...] += part

        @pl.when(p == npair - 1)
        def _():
            dx, dg = _rms_grad(acc_ref[...] + part, x_ref[...], g_ref[...])
            dx_ref[...] = r_ref[...] + dx

            @pl.when(i == 0)
            def _():
                dg_ref[...] = dg

            @pl.when(i > 0)
            def _():
                dg_ref[...] += dg

    row = pl.BlockSpec((tm, kdim), lambda i, p: (i, 0))
    vec = pl.BlockSpec((1, kdim), lambda i, p: (0, 0))
    return pl.pallas_call(
        body, name=name, grid=(t // tm, npair),
        in_specs=[pl.BlockSpec((tm, 2 * ws), lambda i, p: (i, p)),
                  pl.BlockSpec((None, kdim, wp), lambda i, p: (2 * p, 0, 0)),
                  pl.BlockSpec((None, kdim, wp), lambda i, p: (2 * p + 1, 0, 0)), row, vec, row]
        + [pl.BlockSpec(memory_space=pl.ANY)] * len(extra),
        out_specs=(row, vec),
        out_shape=(jax.ShapeDtypeStruct((t, kdim), F32), jax.ShapeDtypeStruct((1, kdim), F32)),
        scratch_shapes=[pltpu.VMEM((tm, kdim), F32)], compiler_params=_cparams(("arbitrary", "arbitrary")),
    )(du, wg, wg, x, gain.reshape(1, kdim), dres, *extra)


def _mm_gw_in_pairs(h, du, after, name):
    t, kdim = h.shape
    ws = du.shape[1] // N_DEV
    wp = ws + SHIFT
    tm = _tile(kdim, 512)
    rows = (((0,), (0,)), ((), ()))
    extra = [] if after is None else [after]

    def body(h_ref, d_ref, *rest):
        g_ref = rest[-1]
        g = lax.dot_general(h_ref[...], d_ref[...], rows, preferred_element_type=F32)
        g_ref[0] = g[:, :wp].astype(BF16)
        g_ref[1] = g[:, wp - LANES:].astype(BF16)

    return pl.pallas_call(
        body, name=name, grid=(N_DEV // 2, kdim // tm),
        in_specs=[pl.BlockSpec((t, tm), lambda p, i: (0, i)), pl.BlockSpec((t, 2 * ws), lambda p, i: (0, p))]
        + [pl.BlockSpec(memory_space=pl.ANY)] * len(extra),
        out_specs=pl.BlockSpec((2, tm, wp), lambda p, i: (p, i, 0)),
        out_shape=jax.ShapeDtypeStruct((N_DEV, kdim, wp), BF16), compiler_params=_cparams(("arbitrary", "arbitrary")),
    )(h, du, *extra)


ROW_BLOCK_BUDGET = 24 * 1024 * 1024


def _rows(t, row_bytes):
    rows = t
    while rows > 8 and (2 * rows * row_bytes > ROW_BLOCK_BUDGET or t % rows):
        rows //= 2
    return rows


def _rms_fwd(x, g, name):
    t, d = x.shape
    ROWS = _rows(t, 6 * d)

    def body(x_ref, g_ref, h_ref):
        h_ref[...] = _rms_apply(x_ref[...], g_ref[...]).astype(BF16)

    return pl.pallas_call(
        body, name=name, grid=(t // ROWS,),
        in_specs=[pl.BlockSpec((ROWS, d), lambda i: (i, 0)), pl.BlockSpec((1, d), lambda i: (0, 0))],
        out_specs=pl.BlockSpec((ROWS, d), lambda i: (i, 0)),
        out_shape=jax.ShapeDtypeStruct((t, d), BF16), compiler_params=_cparams(("arbitrary",)),
    )(x, g.reshape(1, d))


def _gate_fwd(u, yc, ya, gate_col, name):
    t, d = yc.shape
    td = math.gcd(_tile(d, 512), gate_col)
    nd = d // td
    c0 = gate_col // td
    ROWS = _rows(t, 14 * td)

    def body(gc_ref, ga_ref, yc_ref, ya_ref, m_ref):
        gc = _sigmoid(gc_ref[...].astype(F32))
        ga = _sigmoid(ga_ref[...].astype(F32))
        m_ref[...] = (gc * yc_ref[...] + ga * ya_ref[...]).astype(BF16)

    blk = pl.BlockSpec((ROWS, td), lambda i, j: (i, j))
    return pl.pallas_call(
        body, name=name, grid=(t // ROWS, nd),
        in_specs=[pl.BlockSpec((ROWS, td), lambda i, j: (i, c0 + j)),
                  pl.BlockSpec((ROWS, td), lambda i, j: (i, c0 + nd + j)), blk, blk],
        out_specs=blk, out_shape=jax.ShapeDtypeStruct((t, d), BF16),
        compiler_params=_cparams(("arbitrary", "arbitrary")),
    )(u, u, yc, ya)


def _gate_bwd(dm, u, yc, ya, gate_col, name):
    t, d = yc.shape
    td = math.gcd(_tile(d, 512), gate_col)
    nd = d // td
    c0 = gate_col // td
    ROWS = _rows(t, 24 * td)

    def body(dm_ref, gc_ref, ga_ref, yc_ref, ya_ref, dyc_ref, dya_ref, dugc_ref, duga_ref):
        dmv = dm_ref[...]
        gc = _sigmoid(gc_ref[...].astype(F32))
        ga = _sigmoid(ga_ref[...].astype(F32))
        dyc_ref[...] = (dmv * gc).astype(BF16)
        dya_ref[...] = (dmv * ga).astype(BF16)
        dugc_ref[...] = (dmv * yc_ref[...] * gc * (1.0 - gc)).astype(BF16)
        duga_ref[...] = (dmv * ya_ref[...] * ga * (1.0 - ga)).astype(BF16)

    blk = pl.BlockSpec((ROWS, td), lambda i, j: (i, j))
    o = jax.ShapeDtypeStruct((t, d), BF16)
    return pl.pallas_call(
        body, name=name, grid=(t // ROWS, nd),
        in_specs=[blk, pl.BlockSpec((ROWS, td), lambda i, j: (i, c0 + j)),
                  pl.BlockSpec((ROWS, td), lambda i, j: (i, c0 + nd + j)), blk, blk],
        out_specs=(blk, blk, blk, blk), out_shape=(o, o, o, o),
        compiler_params=_cparams(("arbitrary", "arbitrary")),
    )(dm, u, u, yc, ya)


def _loss_and_grad(y, target, name):
    t, d = y.shape
    ROWS = _rows(t, 12 * d)
    n = t // ROWS

    def body(y_ref, t_ref, loss_ref, dy_ref, acc_ref):
        i = pl.program_id(0)

        @pl.when(i == 0)
        def _():
            acc_ref[...] = jnp.zeros_like(acc_ref)

        diff = y_ref[...] - t_ref[...]
        dy_ref[...] = diff * (1.0 / d)
        acc_ref[...] += jnp.sum(diff * diff, axis=0, keepdims=True)

        @pl.when(i == n - 1)
        def _():
            loss_ref[...] = jnp.sum(acc_ref[...], axis=-1, keepdims=True) * (0.5 / d)

    row = pl.BlockSpec((ROWS, d), lambda i: (i, 0))
    return pl.pallas_call(
        body, name=name, grid=(n,), in_specs=[row, row],
        out_specs=(pl.BlockSpec((1, 1), lambda i: (0, 0)), row),
        out_shape=(jax.ShapeDtypeStruct((1, 1), F32), jax.ShapeDtypeStruct((t, d), F32)),
        scratch_shapes=[pltpu.VMEM((1, d), F32)], compiler_params=_cparams(("arbitrary",)),
    )(y, target)


HALO = 32


def _conv_fwd(u, w, b, cdim, name):
    t = u.shape[0]
    ncb = cdim // LANES
    nt = t // BLOCK

    def body(a_ref, g_ref, w_ref, b_ref, zc_ref, zpad):
        zpad[0:HALO, :] = jnp.zeros((HALO, LANES), F32)
        zpad[HALO:HALO + t, :] = a_ref[...].astype(F32) * _sigmoid(g_ref[...].astype(F32))
        wv = w_ref[...]
        bv = b_ref[...]

        def tile(i, carry):
            r0 = pl.multiple_of(i * BLOCK, BLOCK)
            win = zpad[pl.ds(r0, BLOCK + HALO), :]
            acc = jnp.zeros((BLOCK, LANES), F32) + bv
            for b in range(SUBLANES):
                sh = win if b == 0 else pltpu.roll(win, b, 0)
                for a in range(HALO // SUBLANES):
                    j = CONV_WIDTH - 1 - (SUBLANES * a + b)
                    if j >= 0:
                        lo = HALO - SUBLANES * a
                        acc = acc + wv[j:j + 1, :] * sh[lo:lo + BLOCK, :]
            zc_ref[pl.ds(r0, BLOCK), :] = acc
            return carry

        lax.fori_loop(0, nt, tile, 0)

    col = lambda off: pl.BlockSpec((t, LANES), lambda c: (0, off + c))
    return pl.pallas_call(
        body, name=name, grid=(ncb,),
        in_specs=[col(0), col(ncb), pl.BlockSpec((CONV_TAPS_PADDED, LANES), lambda c: (0, c)),
                  pl.BlockSpec((1, LANES), lambda c: (0, c))],
        out_specs=pl.BlockSpec((t, LANES), lambda c: (0, c)),
        out_shape=jax.ShapeDtypeStruct((t, cdim), F32),
        scratch_shapes=[pltpu.VMEM((t + HALO, LANES), F32)], compiler_params=_cparams(("arbitrary",)),
    )(u, u, w, b.reshape(1, cdim))


def _conv_bwd(dzc, u, w, cdim, name):
    t = u.shape[0]
    ncb = cdim // LANES
    nt = t // BLOCK
    win_rows = BLOCK + HALO

    def body(dzc_ref, a_ref, g_ref, w_ref, da_ref, dg_ref, dw_ref, db_ref, zpad, dpad):
        av = a_ref[...].astype(F32)
        sg = _sigmoid(g_ref[...].astype(F32))
        zpad[0:HALO, :] = jnp.zeros((HALO, LANES), F32)
        zpad[HALO:HALO + t, :] = av * sg
        dpad[0:t, :] = dzc_ref[...]
        dpad[t:t + HALO, :] = jnp.zeros((HALO, LANES), F32)
        dw_ref[...] = jnp.zeros_like(dw_ref)
        db_ref[...] = jnp.sum(dzc_ref[...], axis=0, keepdims=True)
        wv = w_ref[...]

        def tile(i, carry):
            r0 = pl.multiple_of(i * BLOCK, BLOCK)
            zwin = zpad[pl.ds(r0, win_rows), :]
            dwin = dpad[pl.ds(r0, win_rows), :]
            dcur = dwin[0:BLOCK, :]
            dz = jnp.zeros((BLOCK, LANES), F32)
            for b in range(SUBLANES):
                zs = zwin if b == 0 else pltpu.roll(zwin, b, 0)
                ds = dwin if b == 0 else pltpu.roll(dwin, win_rows - b, 0)
                for a in range(HALO // SUBLANES):
                    j = CONV_WIDTH - 1 - (SUBLANES * a + b)
                    if j >= 0:
                        lo = HALO - SUBLANES * a
                        dw_ref[j:j + 1, :] += jnp.sum(dcur * zs[lo:lo + BLOCK, :], axis=0, keepdims=True)
                        dz = dz + wv[j:j + 1, :] * ds[SUBLANES * a:SUBLANES * a + BLOCK, :]
            ac = a_ref[pl.ds(r0, BLOCK), :].astype(F32)
            sc = _sigmoid(g_ref[pl.ds(r0, BLOCK), :].astype(F32))
            da_ref[pl.ds(r0, BLOCK), :] = (dz * sc).astype(BF16)
            dg_ref[pl.ds(r0, BLOCK), :] = (dz * ac * sc * (1.0 - sc)).astype(BF16)
            return carry

        lax.fori_loop(0, nt, tile, 0)

    col = lambda off: pl.BlockSpec((t, LANES), lambda c: (0, off + c))
    wspec = pl.BlockSpec((CONV_TAPS_PADDED, LANES), lambda c: (0, c))
    o = jax.ShapeDtypeStruct((t, cdim), BF16)
    return pl.pallas_call(
        body, name=name, grid=(ncb,), in_specs=[col(0), col(0), col(ncb), wspec],
        out_specs=(col(0), col(0), wspec, pl.BlockSpec((1, LANES), lambda c: (0, c))),
        out_shape=(o, o, jax.ShapeDtypeStruct((CONV_TAPS_PADDED, cdim), F32), jax.ShapeDtypeStruct((1, cdim), F32)),
        scratch_shapes=[pltpu.VMEM((t + HALO, LANES), F32), pltpu.VMEM((t + HALO, LANES), F32)],
        compiler_params=_cparams(("arbitrary",)),
    )(dzc, u, u, w)


def _ln_swish_fwd(zc, g, b, name):
    t, c = zc.shape
    ROWS = _rows(t, 6 * c)

    def body(z_ref, g_ref, b_ref, o_ref):
        z = z_ref[...]
        mu = jnp.mean(z, axis=-1, keepdims=True)
        zc_ = z - mu
        zn = zc_ * lax.rsqrt(jnp.mean(zc_ * zc_, axis=-1, keepdims=True) + EPS)
        y = zn * g_ref[...] + b_ref[...]
        o_ref[...] = (y * _sigmoid(y)).astype(BF16)

    row = pl.BlockSpec((ROWS, c), lambda i: (i, 0))
    vec = pl.BlockSpec((1, c), lambda i: (0, 0))
    return pl.pallas_call(
        body, name=name, grid=(t // ROWS,), in_specs=[row, vec, vec], out_specs=row,
        out_shape=jax.ShapeDtypeStruct((t, c), BF16), compiler_params=_cparams(("arbitrary",)),
    )(zc, g.reshape(1, c), b.reshape(1, c))


def _ln_swish_bwd(dzs, zc, g, b, name):
    t, c = zc.shape
    ROWS = _rows(t, 12 * c)

    def body(d_ref, z_ref, g_ref, b_ref, dz_ref, dg_ref, db_ref):
        @pl.when(pl.program_id(0) == 0)
        def _():
            dg_ref[...] = jnp.zeros_like(dg_ref)
            db_ref[...] = jnp.zeros_like(db_ref)

        z = z_ref[...]
        mu = jnp.mean(z, axis=-1, keepdims=True)
        zc_ = z - mu
        rstd = lax.rsqrt(jnp.mean(zc_ * zc_, axis=-1, keepdims=True) + EPS)
        zn = zc_ * rstd
        y = zn * g_ref[...] + b_ref[...]
        sg = _sigmoid(y)
        dy = d_ref[...] * (sg * (1.0 + y * (1.0 - sg)))
        dg_ref[...] += jnp.sum(dy * zn, axis=0, keepdims=True)
        db_ref[...] += jnp.sum(dy, axis=0, keepdims=True)
        dzn = dy * g_ref[...]
        dz_ref[...] = rstd * (dzn - jnp.mean(dzn, axis=-1, keepdims=True)
                              - zn * jnp.mean(dzn * zn, axis=-1, keepdims=True))

    row = pl.BlockSpec((ROWS, c), lambda i: (i, 0))
    vec = pl.BlockSpec((1, c), lambda i: (0, 0))
    v = jax.ShapeDtypeStruct((1, c), F32)
    return pl.pallas_call(
        body, name=name, grid=(t // ROWS,), in_specs=[row, row, vec, vec], out_specs=(row, vec, vec),
        out_shape=(jax.ShapeDtypeStruct((t, c), F32), v, v), compiler_params=_cparams(("arbitrary",)),
    )(dzs, zc, g.reshape(1, c), b.reshape(1, c))


def _bucket_table():
    qi = np.arange(BLOCK)[:, None]
    kj = np.arange(2 * BLOCK)[None, :]
    off = qi + BLOCK - kj
    band = (off >= 0) & (off <= SUB_WINDOW)
    max_exact = NUM_BUCKETS // 2
    out = []
    for d in DILATIONS:
        dist = (np.clip(off, 0, SUB_WINDOW) * d).astype(np.int32)
        nf = np.maximum(dist, 1).astype(np.float32)
        large = max_exact + (np.log(nf / np.float32(max_exact)) / np.float32(math.log(MAX_REL_DISTANCE / max_exact))
                             * np.float32(NUM_BUCKETS - max_exact)).astype(np.int32)
        large = np.minimum(large, NUM_BUCKETS - 1)
        bucket = np.where(dist < max_exact, dist, large)
        out.append(np.where(band, bucket, -1))
    return np.stack(out).astype(np.int32)


def _bias_expand(rel_bias, buckets, hpg, name):
    nh = N_GROUPS * hpg

    def body(rb_ref, bk_ref, o_ref):
        h = pl.program_id(0)
        bk = bk_ref[0]
        acc = jnp.full((BLOCK, 2 * BLOCK), NEG_INF, F32)
        for bb in range(NUM_BUCKETS):
            acc = jnp.where(bk == bb, rb_ref[bb, h], acc)
        o_ref[0] = acc

    return pl.pallas_call(
        body, name=name, grid=(nh,),
        in_specs=[pl.BlockSpec(memory_space=pltpu.SMEM),
                  pl.BlockSpec((1, BLOCK, 2 * BLOCK), lambda h: (h // hpg, 0, 0))],
        out_specs=pl.BlockSpec((1, BLOCK, 2 * BLOCK), lambda h: (h, 0, 0)),
        out_shape=jax.ShapeDtypeStruct((nh, BLOCK, 2 * BLOCK), F32), compiler_params=_cparams(("arbitrary",)),
    )(rel_bias, buckets)


def _bias_reduce(ds_sum, buckets, hpg, name):
    nh = N_GROUPS * hpg

    def body(ds_ref, bk_ref, o_ref):
        bk = bk_ref[0]
        dsv = ds_ref[0]
        lane = lax.broadcasted_iota(jnp.int32, (1, LANES), 1)
        row = jnp.zeros((1, LANES), F32)
        for bb in range(NUM_BUCKETS):
            tot = jnp.sum(jnp.sum(jnp.where(bk == bb, dsv, 0.0), axis=-1, keepdims=True), axis=0, keepdims=True)
            row = jnp.where(lane == bb, tot, row)
        o_ref[0] = row

    return pl.pallas_call(
        body, name=name, grid=(nh,),
        in_specs=[pl.BlockSpec((1, BLOCK, 2 * BLOCK), lambda h: (h, 0, 0)),
                  pl.BlockSpec((1, BLOCK, 2 * BLOCK), lambda h: (h // hpg, 0, 0))],
        out_specs=pl.BlockSpec((1, 1, LANES), lambda h: (h, 0, 0)),
        out_shape=jax.ShapeDtypeStruct((nh, 1, LANES), F32), compiler_params=_cparams(("arbitrary",)),
    )(ds_sum, buckets)


def _chunk_rows(c, d, nb):
    r, n = c // nb, c % nb
    if d == 1:
        return pl.ds(c * BLOCK, BLOCK)
    return pl.ds(r + n * BLOCK * d, BLOCK, stride=d)


def _segment_ones():
    i = lax.broadcasted_iota(jnp.int32, (LANES, LANES), 0) // HEAD_DIM
    j = lax.broadcasted_iota(jnp.int32, (LANES, LANES), 1) // HEAD_DIM
    return (i == j).astype(BF16)


def _segment_sum(v, seg):
    hi = v.astype(BF16)
    lo = (v - hi.astype(F32)).astype(BF16)
    return jnp.dot(hi, seg, preferred_element_type=F32) + jnp.dot(lo, seg, preferred_element_type=F32)


def _head_mean(v, seg):
    return _segment_sum(v, seg) * (1.0 / HEAD_DIM)


def _attn_fwd(u, qg2, kg2, bias, gi, cols, hpg, name):
    t = u.shape[0]
    d = DILATIONS[gi]
    nchunk = t // BLOCK
    nb = (t // d) // BLOCK
    hp = hpg // 2
    qc0, kc0, vc0 = [(c + gi * hpg * HEAD_DIM) // LANES for c in cols]
    contract_lanes = (((1,), (1,)), ((), ()))

    def body(q_ref, k_ref, v_ref, qg_ref, kg_ref, bias_ref, o_ref, lse_ref, qd, kd, vd, od, ld, sbuf):
        seg = _segment_ones()
        lane = lax.broadcasted_iota(jnp.int32, (1, LANES), 1)
        qg = qg_ref[...] * (HEAD_DIM ** -0.5)
        kg = kg_ref[...]
        kd[0:BLOCK, :] = jnp.zeros((BLOCK, LANES), BF16)
        vd[0:BLOCK, :] = jnp.zeros((BLOCK, LANES), BF16)
        od[...] = q_ref[...].astype(F32)
        ld[...] = k_ref[...].astype(F32)
        for c in range(nchunk):
            rows = _chunk_rows(c, d, nb)
            qv = od[rows, :]
            kv = ld[rows, :]
            qd[c * BLOCK:(c + 1) * BLOCK, :] = (qv * lax.rsqrt(_head_mean(qv * qv, seg) + EPS) * qg).astype(BF16)
            kd[(c + 1) * BLOCK:(c + 2) * BLOCK, :] = (kv * lax.rsqrt(_head_mean(kv * kv, seg) + EPS) * kg).astype(BF16)
        od[...] = v_ref[...].astype(F32)
        for c in range(nchunk):
            vd[(c + 1) * BLOCK:(c + 2) * BLOCK, :] = od[_chunk_rows(c, d, nb), :].astype(BF16)

        col = lax.broadcasted_iota(jnp.int32, (BLOCK, 2 * BLOCK), 1)
        for j in range(2):
            mj = jnp.logical_and(lane >= j * HEAD_DIM, lane < (j + 1) * HEAD_DIM)
            for c in range(nchunk):
                kw = kd[c * BLOCK:(c + 2) * BLOCK, :]
                kj = jnp.where(mj, kw, jnp.zeros_like(kw))
                s = lax.dot_general(qd[c * BLOCK:(c + 1) * BLOCK, :], kj, contract_lanes,
                                    preferred_element_type=F32) + bias_ref[j]
                if c % nb == 0:
                    s = jnp.where(col < BLOCK, NEG_INF, s)
                sbuf[c] = s
            for c in range(nchunk):
                rows = slice(c * BLOCK, (c + 1) * BLOCK)
                s = sbuf[c]
                mx = jnp.max(s, axis=-1, keepdims=True)
                p = jnp.exp(s - mx).astype(BF16)
                vw = vd[c * BLOCK:(c + 2) * BLOCK, :]
                oj = jnp.dot(p, jnp.where(mj, vw, jnp.ones_like(vw)), preferred_element_type=F32)
                l = pltpu.roll(oj, HEAD_DIM, 1)
                on = oj / l
                ls = mx + jnp.log(l)
                if j == 0:
                    od[rows, :] = on
                    ld[rows, :] = ls
                else:
                    od[rows, :] = jnp.where(mj, on, od[rows, :])
                    ld[rows, :] = jnp.where(mj, ls, ld[rows, :])

        for c in range(nchunk):
            rows = _chunk_rows(c, d, nb)
            o_ref[rows, :] = od[c * BLOCK:(c + 1) * BLOCK, :]
            lse_ref[rows, :] = ld[c * BLOCK:(c + 1) * BLOCK, :]

    ucol = lambda c0: pl.BlockSpec((t, LANES), lambda h: (0, c0 + h))
    vec = pl.BlockSpec((1, LANES), lambda h: (0, 0))
    oblk = pl.BlockSpec((t, LANES), lambda h: (0, h))
    osh = jax.ShapeDtypeStruct((t, hpg * HEAD_DIM), F32)
    return pl.pallas_call(
        body, name=name, grid=(hp,),
        in_specs=[ucol(qc0), ucol(kc0), ucol(vc0), vec, vec,
                  pl.BlockSpec((2, BLOCK, 2 * BLOCK), lambda h: (gi * hp + h, 0, 0))],
        out_specs=(oblk, oblk), out_shape=(osh, osh),
        scratch_shapes=[pltpu.VMEM((t, LANES), BF16), pltpu.VMEM((t + BLOCK, LANES), BF16),
                        pltpu.VMEM((t + BLOCK, LANES), BF16), pltpu.VMEM((t, LANES), F32), pltpu.VMEM((t, LANES), F32),
                        pltpu.VMEM((nchunk, BLOCK, 2 * BLOCK), F32)],
        compiler_params=_cparams(("arbitrary",)),
    )(u, u, u, qg2, kg2, bias)


def _attn_bwd(u, do_g, dd_g, lse_g, qg2, kg2, bias, ds_in, dqkv_in, gi, cols, hpg, name):
    t = u.shape[0]
    d = DILATIONS[gi]
    nchunk = t // BLOCK
    nb = (t // d) // BLOCK
    hp = hpg // 2
    qc0, kc0, vc0 = [(c + gi * hpg * HEAD_DIM) // LANES for c in cols]
    contract_lanes = (((1,), (1,)), ((), ()))
    contract_rows = (((0,), (0,)), ((), ()))
    qscale = HEAD_DIM ** -0.5

    def body(q_ref, k_ref, v_ref, do_ref, dd_ref, lse_ref, qg_ref, kg_ref, bias_ref, dsin_ref, _dq_in, _dk_in, _dv_in,
             dq_ref, dk_ref, dv_ref, dgq_ref, dgk_ref, dsout_ref,
             qd, kd, vd, dod, ddd, ld, dqd, dkd, dvd, dsacc, pbuf, dsbuf, qs, ks):
        seg = _segment_ones()
        lane = lax.broadcasted_iota(jnp.int32, (1, LANES), 1)
        qg = qg_ref[...] * qscale
        kg = kg_ref[...]
        kd[0:BLOCK, :] = jnp.zeros((BLOCK, LANES), BF16)
        vd[0:BLOCK, :] = jnp.zeros((BLOCK, LANES), BF16)
        dsacc[...] = jnp.zeros_like(dsacc)
        qs[...] = q_ref[...].astype(F32)
        ks[...] = k_ref[...].astype(F32)
        dqd[...] = v_ref[...].astype(F32)
        for c in range(nchunk):
            rows = _chunk_rows(c, d, nb)
            qv = qs[rows, :]
            kv = ks[rows, :]
            qd[c * BLOCK:(c + 1) * BLOCK, :] = (qv * lax.rsqrt(_head_mean(qv * qv, seg) + EPS) * qg).astype(BF16)
            kd[(c + 1) * BLOCK:(c + 2) * BLOCK, :] = (kv * lax.rsqrt(_head_mean(kv * kv, seg) + EPS) * kg).astype(BF16)
            vd[(c + 1) * BLOCK:(c + 2) * BLOCK, :] = dqd[rows, :].astype(BF16)
            dod[c * BLOCK:(c + 1) * BLOCK, :] = do_ref[rows, :].astype(BF16)
            ddd[c * BLOCK:(c + 1) * BLOCK, :] = dd_ref[rows, :]
            ld[c * BLOCK:(c + 1) * BLOCK, :] = lse_ref[rows, :]

        col = lax.broadcasted_iota(jnp.int32, (BLOCK, 2 * BLOCK), 1)
        for j in range(2):
            mj = jnp.logical_and(lane >= j * HEAD_DIM, lane < (j + 1) * HEAD_DIM)
            first = lane == j * HEAD_DIM
            for c in range(nchunk):
                rows = slice(c * BLOCK, (c + 1) * BLOCK)
                kw = kd[c * BLOCK:(c + 2) * BLOCK, :]
                vw = vd[c * BLOCK:(c + 2) * BLOCK, :]
                kj = jnp.where(mj, kw, jnp.zeros_like(kw))
                vj = jnp.where(mj, vw, jnp.zeros_like(vw))
                s = lax.dot_general(qd[rows, :], kj, contract_lanes, preferred_element_type=F32) + bias_ref[j]
                if c % nb == 0:
                    s = jnp.where(col < BLOCK, NEG_INF, s)
                dp = lax.dot_general(dod[rows, :], vj, contract_lanes, preferred_element_type=F32)
                lse_j = jnp.sum(jnp.where(first, ld[rows, :], 0.0), axis=-1, keepdims=True)
                dd_j = jnp.sum(jnp.where(first, ddd[rows, :], 0.0), axis=-1, keepdims=True)
                p = jnp.exp(s - lse_j)
                ds = p * (dp + dd_j)
                dsacc[j] += ds
                pbuf[j, c] = p.astype(BF16)
                dsbuf[j, c] = ds.astype(BF16)
        for c in range(nchunk):
            rows = slice(c * BLOCK, (c + 1) * BLOCK)
            has_next = c + 1 < nchunk and (c + 1) % nb != 0
            kw = kd[c * BLOCK:(c + 2) * BLOCK, :]
            dq = jnp.zeros((BLOCK, LANES), F32)
            dk = jnp.zeros((BLOCK, LANES), F32)
            dv = jnp.zeros((BLOCK, LANES), F32)
            both = slice(c * BLOCK, (c + 2) * BLOCK) if has_next else rows
            for j in range(2):
                mj = jnp.logical_and(lane >= j * HEAD_DIM, lane < (j + 1) * HEAD_DIM)
                dq = dq + jnp.dot(dsbuf[j, c], jnp.where(mj, kw, jnp.zeros_like(kw)), preferred_element_type=F32)
                dsk = dsbuf[j, c, :, BLOCK:]
                pk = pbuf[j, c, :, BLOCK:]
                if has_next:
                    dsk = jnp.concatenate([dsk, dsbuf[j, c + 1, :, :BLOCK]], axis=0)
                    pk = jnp.concatenate([pk, pbuf[j, c + 1, :, :BLOCK]], axis=0)
                qq = qd[both, :]
                dd = dod[both, :]
                dk = dk + lax.dot_general(dsk, jnp.where(mj, qq, jnp.zeros_like(qq)), contract_rows,
                                          preferred_element_type=F32)
                dv = dv + lax.dot_general(pk, jnp.where(mj, dd, jnp.zeros_like(dd)), contract_rows,
                                          preferred_element_type=F32)
            dqd[rows, :] = dq
            dkd[rows, :] = dk
            dvd[rows, :] = dv

        dsout_ref[...] = dsin_ref[...] + dsacc[...]

        dgq = jnp.zeros((1, LANES), F32)
        dgk = jnp.zeros((1, LANES), F32)
        for c in range(nchunk):
            rows = _chunk_rows(c, d, nb)
            qv = qs[rows, :]
            rq = lax.rsqrt(_head_mean(qv * qv, seg) + EPS)
            qh = qv * rq
            dy = dqd[c * BLOCK:(c + 1) * BLOCK, :]
            dgq = dgq + jnp.sum(dy * qh, axis=0, keepdims=True) * qscale
            dxh = dy * qg
            ddd[rows, :] = rq * (dxh - qh * _head_mean(dxh * qh, seg))
            kv = ks[rows, :]
            rk = lax.rsqrt(_head_mean(kv * kv, seg) + EPS)
            kh = kv * rk
            dy = dkd[c * BLOCK:(c + 1) * BLOCK, :]
            dgk = dgk + jnp.sum(dy * kh, axis=0, keepdims=True)
            dxh = dy * kg
            ld[rows, :] = rk * (dxh - kh * _head_mean(dxh * kh, seg))
        dq_ref[...] = ddd[...].astype(BF16)
        dk_ref[...] = ld[...].astype(BF16)
        for c in range(nchunk):
            ddd[_chunk_rows(c, d, nb), :] = dvd[c * BLOCK:(c + 1) * BLOCK, :]
        dv_ref[...] = ddd[...].astype(BF16)
        dgq_ref[0] = dgq
        dgk_ref[0] = dgk

    ucol = lambda c0: pl.BlockSpec((t, LANES), lambda h: (0, c0 + h))
    vec = pl.BlockSpec((1, LANES), lambda h: (0, 0))
    oblk = pl.BlockSpec((t, LANES), lambda h: (0, h))
    bblk = pl.BlockSpec((2, BLOCK, 2 * BLOCK), lambda h: (gi * hp + h, 0, 0))
    gblk = pl.BlockSpec((1, 1, LANES), lambda h: (h, 0, 0))
    osh = jax.ShapeDtypeStruct((t, N_GROUPS * hpg * HEAD_DIM), BF16)
    gsh = jax.ShapeDtypeStruct((hp, 1, LANES), F32)
    gcol = pl.BlockSpec((t, LANES), lambda h: (0, gi * hp + h))
    hbm = pl.BlockSpec(memory_space=pl.ANY)
    return pl.pallas_call(
        body, name=name, grid=(hp,),
        in_specs=[ucol(qc0), ucol(kc0), ucol(vc0), oblk, oblk, oblk, vec, vec, bblk, bblk, hbm, hbm, hbm],
        out_specs=(gcol, gcol, gcol, gblk, gblk, bblk),
        out_shape=(osh, osh, osh, gsh, gsh, jax.ShapeDtypeStruct(ds_in.shape, F32)),
        input_output_aliases={9: 5, 10: 0, 11: 1, 12: 2},
        scratch_shapes=[pltpu.VMEM((t, LANES), BF16), pltpu.VMEM((t + BLOCK, LANES), BF16),
                        pltpu.VMEM((t + BLOCK, LANES), BF16), pltpu.VMEM((t, LANES), BF16),
                        pltpu.VMEM((t, LANES), F32), pltpu.VMEM((t, LANES), F32), pltpu.VMEM((t, LANES), F32),
                        pltpu.VMEM((t, LANES), F32), pltpu.VMEM((t, LANES), F32),
                        pltpu.VMEM((2, BLOCK, 2 * BLOCK), F32), pltpu.VMEM((2, nchunk, BLOCK, 2 * BLOCK), BF16),
                        pltpu.VMEM((2, nchunk, BLOCK, 2 * BLOCK), BF16), pltpu.VMEM((t, LANES), F32),
                        pltpu.VMEM((t, LANES), F32)],
        compiler_params=_cparams(("arbitrary",)),
    )(u, u, u, do_g, dd_g, lse_g, qg2, kg2, bias, ds_in, *dqkv_in)


def _group_weights(l0, l1, l2):
    mx = jnp.maximum(jnp.maximum(l0, l1), l2)
    e0, e1, e2 = jnp.exp(l0 - mx), jnp.exp(l1 - mx), jnp.exp(l2 - mx)
    inv = 1.0 / (e0 + e1 + e2)
    return e0 * inv, e1 * inv, e2 * inv


def _combine_fwd(os_, lses, name):
    t, ao = os_[0].shape
    ROWS = _rows(t, 26 * ao)

    def body(o0, o1, o2, l0, l1, l2, o_ref):
        w0, w1, w2 = _group_weights(l0[...], l1[...], l2[...])
        o_ref[...] = (w0 * o0[...] + w1 * o1[...] + w2 * o2[...]).astype(BF16)

    row = pl.BlockSpec((ROWS, ao), lambda i: (i, 0))
    return pl.pallas_call(
        body, name=name, grid=(t // ROWS,), in_specs=[row] * 6, out_specs=row,
        out_shape=jax.ShapeDtypeStruct((t, ao), BF16), compiler_params=_cparams(("arbitrary",)),
    )(*os_, *lses)


def _combine_bwd(do, os_, lses, name):
    t, ao = do.shape
    idx = np.arange(ao) // HEAD_DIM
    seg = jnp.asarray((idx[:, None] == idx[None, :]).astype(np.float32), dtype=BF16)
    ROWS = _rows(t, 52 * ao)

    def body(do_ref, o0, o1, o2, l0, l1, l2, seg_ref, g0, g1, g2, d0, d1, d2):
        w0, w1, w2 = _group_weights(l0[...], l1[...], l2[...])
        dov = do_ref[...]
        o = w0 * o0[...] + w1 * o1[...] + w2 * o2[...]
        sd = _segment_sum(dov * o, seg_ref[...])
        for w, gref, dref in ((w0, g0, d0), (w1, g1, d1), (w2, g2, d2)):
            gref[...] = w * dov
            dref[...] = -(w * sd)

    row = pl.BlockSpec((ROWS, ao), lambda i: (i, 0))
    sh = jax.ShapeDtypeStruct((t, ao), F32)
    outs = pl.pallas_call(
        body, name=name, grid=(t // ROWS,), in_specs=[row] * 7 + [pl.BlockSpec((ao, ao), lambda i: (0, 0))],
        out_specs=(row,) * 6, out_shape=(sh,) * 6, compiler_params=_cparams(("arbitrary",)),
    )(do, *os_, *lses, seg)
    return outs[:3], outs[3:]


def _adamw(w, g, m, v, name):
    shape = w.shape
    cols = shape[-1]
    rows = int(np.prod(shape[:-1]))
    tr = rows if rows <= 512 else _tile_rows(rows)
    c1 = 1.0 - ADAM_B1 ** ADAM_STEP
    c2 = 1.0 - ADAM_B2 ** ADAM_STEP

    def body(w_ref, g_ref, m_ref, v_ref, d_ref, nm_ref, nv_ref):
        gv = g_ref[...]
        mn = ADAM_B1 * m_ref[...] + (1.0 - ADAM_B1) * gv
        vn = ADAM_B2 * v_ref[...] + (1.0 - ADAM_B2) * (gv * gv)
        nm_ref[...] = mn
        nv_ref[...] = vn
        d_ref[...] = -ADAM_LR * ((mn / c1) / (jnp.sqrt(vn / c2) + ADAM_EPS) + ADAM_WD * w_ref[...])

    blk = pl.BlockSpec((tr, cols), lambda i: (i, 0))
    sh = jax.ShapeDtypeStruct((rows, cols), F32)
    outs = pl.pallas_call(
        body, name=name, grid=(rows // tr,), in_specs=[blk] * 4, out_specs=(blk,) * 3, out_shape=(sh,) * 3,
        compiler_params=_cparams(("arbitrary",)),
    )(*[a.reshape(rows, cols) for a in (w, g, m, v)])
    return tuple(o.reshape(shape) for o in outs)


def _tile_rows(rows):
    for t in (512, 256, 128, 64, 32, 16, 8):
        if rows % t == 0:
            return t
    return rows


def _sum_slots(recv, parts, me, layers, l, name):
    _, rows, cols = recv.shape
    tr = rows if rows <= 512 else _tile_rows(rows)

    def body(me_ref, r_ref, own_ref, _, o_ref):
        acc = jnp.zeros(o_ref.shape, F32)
        for s in range(N_DEV):
            acc = acc + jnp.where(me_ref[0] == s, own_ref[...], r_ref[s]).astype(F32)
        o_ref[...] = acc

    return pl.pallas_call(
        body, name=name,
        grid_spec=pltpu.PrefetchScalarGridSpec(
            num_scalar_prefetch=1, grid=(rows // tr,),
            in_specs=[pl.BlockSpec((N_DEV, tr, cols), lambda i, me: (0, i, 0)),
                      pl.BlockSpec((None, tr, cols), lambda i, me: (me[0], i, 0)),
                      pl.BlockSpec(memory_space=pl.ANY)],
            out_specs=pl.BlockSpec((None, tr, cols), lambda i, me: (l, i, 0))),
        out_shape=jax.ShapeDtypeStruct(layers.shape, F32), input_output_aliases={3: 0},
        compiler_params=_cparams(("arbitrary",)),
    )(me.reshape(1), recv, parts, layers)


def _peer(k):
    x, y, c = lax.axis_index("x"), lax.axis_index("y"), lax.axis_index("c")
    return (1 - x if k & 4 else x, 1 - y if k & 2 else y, 1 - c if k & 1 else c)


def _dev_index(p):
    return 4 * p[0] + 2 * p[1] + p[2]


HBM_SPEC = pl.BlockSpec(memory_space=pltpu.HBM)
SEM_SPEC = pl.BlockSpec(memory_space=pltpu.SEMAPHORE)
ANY_SPEC = pl.BlockSpec(memory_space=pl.ANY)
CHIPS = (4, 2, 6)


def _remote(src, dst, send_sem, recv_sem, to):
    return pltpu.make_async_remote_copy(src_ref=src, dst_ref=dst, send_sem=send_sem, recv_sem=recv_sem,
                                        device_id=to, device_id_type=MESH)


def _hbm(a):
    return pltpu.with_memory_space_constraint(a, pltpu.HBM)


def _split_call(body, name, bufs, sems_in, sem_out_sizes, after):
    nb, ns, no = len(bufs), len(sems_in), len(sem_out_sizes)
    extra = [] if after is None else [after]

    def kern(*refs):
        pos = nb + ns + len(extra)
        body(refs[:nb], refs[nb:nb + ns], refs[pos:pos + no])
        token_ref = refs[pos + no + nb]
        token_ref[...] = jnp.zeros_like(token_ref)

    out_shape = (tuple(pltpu.SemaphoreType.DMA((s,)) for s in sem_out_sizes)
                 + tuple(pltpu.HBM(b.shape, b.dtype) for b in bufs) + (jax.ShapeDtypeStruct((8, LANES), F32),))
    res = pl.pallas_call(
        kern, name=name, out_shape=out_shape,
        in_specs=[HBM_SPEC] * nb + [SEM_SPEC] * ns + [ANY_SPEC] * len(extra),
        out_specs=(SEM_SPEC,) * no + (HBM_SPEC,) * nb + (pl.BlockSpec(memory_space=pltpu.VMEM),),
        input_output_aliases={i: no + i for i in range(nb)},
        compiler_params=pltpu.CompilerParams(has_side_effects=pltpu.SideEffectType.DATAFLOW_SIDE_EFFECTING),
    )(*bufs, *sems_in, *extra)
    return res[:no], res[no:no + nb], res[no + nb]


def _gather_start(shards, lands, after, name):
    n = len(shards)

    def body(bufs, _, sems):
        ins, lnd = bufs[:n], bufs[n:]
        d2d_s, d2d_r, ici_s, ici_r = sems
        me = _dev_index(_peer(0))
        for j, k in enumerate(CHIPS):
            for i in range(n):
                _remote(ins[i], lnd[i].at[me], ici_s.at[j], ici_r.at[j], _peer(k)).start()
        for i in range(n):
            _remote(ins[i], lnd[i].at[me], d2d_s.at[0], d2d_r.at[0], _peer(1)).start()

    return _split_call(body, name, [_hbm(a) for a in (*shards, *lands)], [], (1, 1, 3, 3), after)


def _gather_forward(n, bufs, ici_r, after, name):
    def body(refs, sems_in, sems):
        ins, lnd = refs[:n], refs[n:]
        (arrived,) = sems_in
        fwd_s, fwd_r = sems
        for j, k in enumerate(CHIPS):
            blk = _dev_index(_peer(k))
            for i in range(n):
                _remote(ins[i], lnd[i].at[blk], fwd_s.at[j], arrived.at[j], _peer(k)).wait_recv()
            for i in range(n):
                _remote(lnd[i].at[blk], lnd[i].at[blk], fwd_s.at[j], fwd_r.at[j], _peer(1)).start()

    return _split_call(body, name, bufs, [ici_r], (3, 3), after)


def _gather_finish(n, bufs, d2d_s, d2d_r, ici_s, fwd_s, fwd_r, after, name):
    def body(refs, sems_in, _):
        ins, lnd = refs[:n], refs[n:]
        d2d_send, d2d_recv, ici_send, fwd_send, fwd_recv = sems_in
        sib = _peer(1)
        for i in range(n):
            cp = _remote(ins[i], lnd[i].at[_dev_index(sib)], d2d_send.at[0], d2d_recv.at[0], sib)
            cp.wait_send()
            cp.wait_recv()
        for j, k in enumerate(CHIPS):
            passed = _dev_index(_peer(k))
            landed = _dev_index(_peer(k | 1))
            for i in range(n):
                _remote(ins[i], lnd[i].at[passed], ici_send.at[j], fwd_recv.at[j], _peer(k)).wait_send()
                cp = _remote(lnd[i].at[passed], lnd[i].at[landed], fwd_send.at[j], fwd_recv.at[j], sib)
                cp.wait_send()
                cp.wait_recv()

    _, out, token = _split_call(body, name, bufs, [d2d_s, d2d_r, ici_s, fwd_s, fwd_r], (), after)
    return out[n:], token


def _exchange_start(parts, lands, after, name):
    n = len(parts)

    def body(bufs, _, sems):
        src, lnd = bufs[:n], bufs[n:]
        send, recv = sems
        me = _dev_index(_peer(0))
        for k in (4, 5, 2, 3, 6, 7, 1):
            to = _peer(k)
            for i in range(n):
                _remote(src[i].at[_dev_index(to)], lnd[i].at[me], send.at[k - 1], recv.at[k - 1], to).start()

    return _split_call(body, name, [_hbm(a) for a in (*parts, *lands)], [], (7, 7), after)


def _exchange_finish(n, bufs, send, recv, after, name):
    def body(refs, sems_in, _):
        src, lnd = refs[:n], refs[n:]
        send_, recv_ = sems_in
        me = _dev_index(_peer(0))
        for k in range(1, N_DEV):
            frm = _peer(k)
            for i in range(n):
                cp = _remote(src[i].at[me], lnd[i].at[_dev_index(frm)], send_.at[k - 1], recv_.at[k - 1], frm)
                cp.wait_send()
                cp.wait_recv()

    _, out, token = _split_call(body, name, bufs, [send, recv], (), after)
    return out[:n], out[n:], token


def _all_reduce_small(v, name):
    rows = v.shape[0]

    def body(v_ref, o_ref, buf, send_sems, recv_sems):
        me = _dev_index(_peer(0))
        buf[me] = v_ref[...]
        copies = []
        for k in range(1, N_DEV):
            copies.append(pltpu.make_async_remote_copy(
                src_ref=v_ref, dst_ref=buf.at[me], send_sem=send_sems.at[k - 1], recv_sem=recv_sems.at[k - 1],
                device_id=_peer(k), device_id_type=MESH))
        for cp in copies:
            cp.start()
        for k in range(1, N_DEV):
            pltpu.make_async_remote_copy(
                src_ref=v_ref, dst_ref=buf.at[_dev_index(_peer(k))], send_sem=send_sems.at[k - 1],
                recv_sem=recv_sems.at[k - 1], device_id=_peer(k), device_id_type=MESH).wait_recv()
        for cp in copies:
            cp.wait_send()
        acc = buf[0]
        for s in range(1, N_DEV):
            acc = acc + buf[s]
        o_ref[...] = acc

    vm = pl.BlockSpec(memory_space=pltpu.VMEM)
    return pl.pallas_call(
        body, name=name, in_specs=[vm], out_specs=vm, out_shape=jax.ShapeDtypeStruct(v.shape, F32),
        scratch_shapes=[pltpu.VMEM((N_DEV, rows, LANES), F32), pltpu.SemaphoreType.DMA((7,)),
                        pltpu.SemaphoreType.DMA((7,))],
    )(v)


def _columns(cdim, ao):
    q_col = 2 * cdim
    attn_dim = N_GROUPS * ao
    return (q_col, q_col + attn_dim, q_col + 2 * attn_dim), q_col + 3 * attn_dim


def _layer_fwd(x, h1, sm, bg, get_rest, bias, hpg, next_gain):
    cdim = sm["conv_ln_g"].shape[0]
    ao = hpg * HEAD_DIM
    cols, gate_col = _columns(cdim, ao)
    qg2 = jnp.tile(sm["q_norm_g"], 2).reshape(1, LANES)
    kg2 = jnp.tile(sm["k_norm_g"], 2).reshape(1, LANES)
    u = _mm_in_pairs(h1, bg["w_in"], "mm_in")
    zc = _conv_fwd(u, bg["conv_dw_w"], sm["conv_dw_b"], cdim, "conv_fwd")
    zs = _ln_swish_fwd(zc, sm["conv_ln_g"], sm["conv_ln_b"], "ln_swish_fwd")
    os_, lses = [], []
    for gi in range(N_GROUPS):
        o_g, lse_g = _attn_fwd(u, qg2, kg2, bias, gi, cols, hpg, "attn_fwd_g%d" % gi)
        os_.append(o_g)
        lses.append(lse_g)
    o = _combine_fwd(os_, lses, "combine_fwd")
    bg = {**bg, **get_rest(o)}
    yc = _mm(zs, bg["w_conv_out"], name="mm_conv_out")
    ya = _mm(o, bg["w_attn_out"], name="mm_attn_out")
    mg = _gate_fwd(u, yc, ya, gate_col, "gate_fwd")
    x1, h2 = _mm(mg, bg["w_out"], epi="res_rms", extra=x, gain=sm["norm2_g"], name="mm_out")
    f = _mm(h2, bg["w_ff1"], out_dtype=BF16, name="mm_ff1")
    if next_gain is None:
        x2, h_next = _mm(f, bg["w_ff2"], a_relu2=True, epi="res", extra=x1, name="mm_ff2"), None
    else:
        x2, h_next = _mm(f, bg["w_ff2"], a_relu2=True, epi="res_rms", extra=x1, gain=next_gain, name="mm_ff2")
    saved = dict(x=x, h1=h1, u=u, zc=zc, zs=zs, yc=yc, os=os_, lses=lses, o=o, ya=ya, mg=mg, x1=x1, h2=h2, f=f,
                 qg2=qg2, kg2=kg2)
    return x2, h_next, saved, bg


GRAD_GROUPS = (("w_ff2", "w_ff1"), ("w_out", "w_conv_out", "w_attn_out", "conv_dw_w"), ("w_in",))


def _layer_bwd(dx, s, sm, bg, bias, ds_sum, after, emit):
    cdim = sm["conv_ln_g"].shape[0]
    ao = bg["w_attn_out"].shape[0]
    hpg = ao // HEAD_DIM
    hp = hpg // 2
    cols, gate_col = _columns(cdim, ao)
    g = {}
    df = _mm(dx, bg["w_ff2"], tb=True, epi="drelu2", extra=s["f"], out_dtype=BF16, after=after, name="mm_dff2")
    g["w_ff2"] = _mm(s["f"], dx, ta=True, a_relu2=True, out_dtype=BF16, name="mm_gw_ff2")
    g["w_ff1"] = _mm(s["h2"], df, ta=True, out_dtype=BF16, out_slots=True, name="mm_gw_ff1")
    after = emit(GRAD_GROUPS[0], g)
    dx1, dg2 = _mm(df, bg["w_ff1"], tb=True, epi="rms_bwd", extra=(s["x1"], dx), gain=sm["norm2_g"], after=after,
                   name="mm_dff1")
    g["norm2_g"] = dg2[0]
    dmg = _mm(dx1, bg["w_out"], tb=True, name="mm_dout")
    g["w_out"] = _mm(s["mg"], dx1, ta=True, out_dtype=BF16, name="mm_gw_out")
    dyc, dya, dugc, duga = _gate_bwd(dmg, s["u"], s["yc"], s["ya"], gate_col, "gate_bwd")
    dzs = _mm(dyc, bg["w_conv_out"], tb=True, name="mm_dconv_out")
    g["w_conv_out"] = _mm(s["zs"], dyc, ta=True, out_dtype=BF16, name="mm_gw_conv_out")
    do = _mm(dya, bg["w_attn_out"], tb=True, name="mm_dattn_out")
    g["w_attn_out"] = _mm(s["o"], dya, ta=True, out_dtype=BF16, name="mm_gw_attn_out")
    dzc, dlg, dlb = _ln_swish_bwd(dzs, s["zc"], sm["conv_ln_g"], sm["conv_ln_b"], "ln_swish_bwd")
    g["conv_ln_g"] = dlg[0]
    g["conv_ln_b"] = dlb[0]
    da, dgt, dcw, dcb = _conv_bwd(dzc, s["u"], bg["conv_dw_w"], cdim, "conv_bwd")
    g["conv_dw_w"] = dcw[:CONV_WIDTH].astype(BF16)
    g["conv_dw_b"] = dcb[0]
    after = emit(GRAD_GROUPS[1], g)
    do_gs, dd_gs = _combine_bwd(do, s["os"], s["lses"], "combine_bwd")
    dqkv = [lax.empty((dx.shape[0], N_GROUPS * ao), BF16) for _ in range(3)]
    dgq = jnp.zeros((HEAD_DIM,), F32)
    dgk = jnp.zeros((HEAD_DIM,), F32)
    for gi in range(N_GROUPS):
        *dqkv, gq, gk, ds_sum = _attn_bwd(s["u"], do_gs[gi], dd_gs[gi], s["lses"][gi], s["qg2"], s["kg2"], bias,
                                          ds_sum, dqkv, gi, cols, hpg, "attn_bwd_g%d" % gi)
        dgq = dgq + jnp.sum(gq.reshape(hp * 2, HEAD_DIM), axis=0)
        dgk = dgk + jnp.sum(gk.reshape(hp * 2, HEAD_DIM), axis=0)
    g["q_norm_g"] = dgq
    g["k_norm_g"] = dgk
    du = jnp.concatenate([da, dgt] + dqkv + [dugc, duga], axis=1)
    g["w_in"] = _mm_gw_in_pairs(s["h1"], du, after, "mm_gw_in")
    after = emit(GRAD_GROUPS[2], g)
    dx0, dg1 = _mm_din_pairs(du, bg["w_in"], s["x"], sm["norm1_g"], dx1, after, "mm_din")
    g["norm1_g"] = dg1[0]
    return dx0, g, ds_sum


BIG = ("w_in", "conv_dw_w", "w_conv_out", "w_attn_out", "w_out", "w_ff1", "w_ff2")
COL_SHARDED = ("w_in", "conv_dw_w", "w_conv_out", "w_attn_out", "w_ff1")
SMALL = ("rel_bias", "norm1_g", "q_norm_g", "k_norm_g", "conv_dw_b", "conv_ln_g", "conv_ln_b", "norm2_g")
WEIGHTS = ("rel_bias", "norm1_g", "w_in", "q_norm_g", "k_norm_g", "conv_dw_w", "conv_dw_b", "conv_ln_g", "conv_ln_b",
           "w_conv_out", "w_attn_out", "w_out", "norm2_g", "w_ff1", "w_ff2")


def _to_whole(name, gathered):
    n, a, b = gathered.shape
    if name in COL_SHARDED:
        return gathered.transpose(1, 0, 2).reshape(a, n * b)
    return gathered.reshape(n * a, b)


def _to_slots(name, whole):
    a, b = whole.shape
    if name in COL_SHARDED:
        return whole.reshape(a, N_DEV, b // N_DEV).transpose(1, 0, 2)
    return whole.reshape(N_DEV, a // N_DEV, b)


def _own_slot(block, me):
    land = lax.empty((N_DEV,) + block.shape, block.dtype)
    return lax.dynamic_update_slice(land, block[None], (me,) + (0,) * block.ndim)


def kernel(x, rel_bias, norm1_g, w_in, q_norm_g, k_norm_g, conv_dw_w, conv_dw_b, conv_ln_g, conv_ln_b, w_conv_out, w_attn_out, w_out, norm2_g, w_ff1, w_ff2, loss_target, m_rel_bias, m_norm1_g, m_w_in, m_q_norm_g, m_k_norm_g, m_conv_dw_w, m_conv_dw_b, m_conv_ln_g, m_conv_ln_b, m_w_conv_out, m_w_attn_out, m_w_out, m_norm2_g, m_w_ff1, m_w_ff2, v_rel_bias, v_norm1_g, v_w_in, v_q_norm_g, v_k_norm_g, v_conv_dw_w, v_conv_dw_b, v_conv_ln_g, v_conv_ln_b, v_w_conv_out, v_w_attn_out, v_w_out, v_norm2_g, v_w_ff1, v_w_ff2):
    w = dict(rel_bias=rel_bias, norm1_g=norm1_g, w_in=w_in, q_norm_g=q_norm_g, k_norm_g=k_norm_g, conv_dw_w=conv_dw_w,
             conv_dw_b=conv_dw_b, conv_ln_g=conv_ln_g, conv_ln_b=conv_ln_b, w_conv_out=w_conv_out,
             w_attn_out=w_attn_out, w_out=w_out, norm2_g=norm2_g, w_ff1=w_ff1, w_ff2=w_ff2)
    mom = dict(rel_bias=m_rel_bias, norm1_g=m_norm1_g, w_in=m_w_in, q_norm_g=m_q_norm_g, k_norm_g=m_k_norm_g,
               conv_dw_w=m_conv_dw_w, conv_dw_b=m_conv_dw_b, conv_ln_g=m_conv_ln_g, conv_ln_b=m_conv_ln_b,
               w_conv_out=m_w_conv_out, w_attn_out=m_w_attn_out, w_out=m_w_out, norm2_g=m_norm2_g, w_ff1=m_w_ff1,
               w_ff2=m_w_ff2)
    var = dict(rel_bias=v_rel_bias, norm1_g=v_norm1_g, w_in=v_w_in, q_norm_g=v_q_norm_g, k_norm_g=v_k_norm_g,
               conv_dw_w=v_conv_dw_w, conv_dw_b=v_conv_dw_b, conv_ln_g=v_conv_ln_g, conv_ln_b=v_conv_ln_b,
               w_conv_out=v_w_conv_out, w_attn_out=v_w_attn_out, w_out=v_w_out, norm2_g=v_norm2_g, w_ff1=v_w_ff1,
               w_ff2=v_w_ff2)

    depth = norm1_g.shape[0]
    me = 4 * lax.axis_index("x") + 2 * lax.axis_index("y") + lax.axis_index("c")
    odd_core = lax.axis_index("c") == 1
    hpg = w_attn_out.shape[1] // HEAD_DIM
    buckets = jnp.asarray(_bucket_table())
    bias = _bias_expand(rel_bias, buckets, hpg, "bias_expand")

    first_names = ("w_in", "conv_dw_w")
    rest_names = tuple(k for k in BIG if k not in first_names)

    def chain_start(l, names, after):
        shards = [w[k][l] if k == "conv_dw_w" else w[k][l].astype(BF16) for k in names]
        if "w_in" in names:
            i = names.index("w_in")
            shards[i] = jnp.where(odd_core, jnp.pad(shards[i], ((0, 0), (SHIFT, 0))),
                                  jnp.pad(shards[i], ((0, 0), (0, SHIFT))))
        sems, bufs, token = _gather_start(shards, [_own_slot(s, me) for s in shards], after,
                                          "gather_start_%s_l%d" % (names[0], l))
        return dict(l=l, names=names, sems=sems, bufs=bufs, token=token)

    def chain_forward(ch, after):
        fwd, bufs, token = _gather_forward(len(ch["names"]), ch["bufs"], ch["sems"][3], after,
                                           "gather_forward_%s_l%d" % (ch["names"][0], ch["l"]))
        ch.update(fwd=fwd, bufs=bufs)
        return token

    def chain_finish(ch, after):
        d2d_s, d2d_r, ici_s, _ = ch["sems"]
        gathered, _ = _gather_finish(len(ch["names"]), ch["bufs"], d2d_s, d2d_r, ici_s, ch["fwd"][0], ch["fwd"][1],
                                     after, "gather_finish_%s_l%d" % (ch["names"][0], ch["l"]))
        out = {k: a if k == "w_in" else _to_whole(k, a) for k, a in zip(ch["names"], gathered)}
        if "conv_dw_w" in out:
            out["conv_dw_w"] = jnp.pad(out["conv_dw_w"], ((0, CONV_TAPS_PADDED - CONV_WIDTH), (0, 0)))
        return out

    xs = x[0]
    h1 = _rms_fwd(xs, norm1_g[0], "rms1_fwd")
    saved, bigs, smalls = [], [], []
    chains = {}
    for l in range(depth):
        sm = {k: w[k][l] for k in SMALL if k != "rel_bias"}
        if l == 0:
            first = chain_start(0, first_names, None)
            token = chain_forward(first, None)
            rest = chain_start(0, rest_names, token)
            bg = chain_finish(first, rest["token"])

            def get_rest(o, rest=rest):
                token = chain_forward(rest, o)
                if depth > 1:
                    chains[1] = (chain_start(1, first_names, token),)
                    chains[1] += (chain_start(1, rest_names, chains[1][0]["token"]),)
                    token = chains[1][1]["token"]
                return chain_finish(rest, token)
        elif l == 1:
            first, rest = chains[1]
            token = chain_forward(first, xs)
            if depth > 2:
                chains[2] = chain_start(2, BIG, token)
                token = chains[2]["token"]
            bg = chain_finish(first, token)

            def get_rest(o, rest=rest):
                return chain_finish(rest, chain_forward(rest, o))
        else:
            token = chain_forward(chains[l], xs)
            if l + 1 < depth:
                chains[l + 1] = chain_start(l + 1, BIG, token)
                token = chains[l + 1]["token"]
            whole = chain_finish(chains[l], token)
            bg = {k: whole[k] for k in first_names}

            def get_rest(o, whole=whole):
                return {k: whole[k] for k in rest_names}
        xs, h1, sv, bg = _layer_fwd(xs, h1, sm, bg, get_rest, bias, hpg, norm1_g[l + 1] if l + 1 < depth else None)
        saved.append(sv)
        bigs.append(bg)
        smalls.append(sm)

    loss, dx = _loss_and_grad(xs, loss_target[0], "loss")

    ds_sum = jnp.zeros((N_GROUPS * hpg, BLOCK, 2 * BLOCK), F32)
    g = {k: [None] * depth for k in SMALL if k != "rel_bias"}
    sums = {k: lax.empty((depth, int(np.prod(w[k].shape[1:-1])), w[k].shape[-1] + (SHIFT if k == "w_in" else 0)), F32)
            for k in BIG}
    pending = []

    def finish_oldest(after):
        names, l, (send, recv), bufs = pending.pop(0)
        parts, recvd, token = _exchange_finish(len(names), bufs, send, recv, after,
                                               "exchange_finish_%s_l%d" % (names[0], l))
        for k, r, p in zip(names, recvd, parts):
            three = (N_DEV, -1, r.shape[-1])
            sums[k] = _sum_slots(r.reshape(three), p.reshape(three), me, sums[k], l, "sum_" + k)
        return token

    def make_emit(l):
        def emit(names, gl):
            parts = [gl[k] if k in ("w_ff1", "w_in") else _to_slots(k, gl[k]) for k in names]
            token = finish_oldest(parts[0]) if len(pending) >= len(GRAD_GROUPS) else None
            lands = [lax.empty(p.shape, p.dtype) for p in parts]
            sems, bufs, token = _exchange_start(parts, lands, token, "exchange_start_%s_l%d" % (names[0], l))
            pending.append((names, l, sems, bufs))
            return token
        return emit

    token = None
    for l in reversed(range(depth)):
        dx, gl, ds_sum = _layer_bwd(dx, saved[l], smalls[l], bigs[l], bias, ds_sum, token, make_emit(l))
        for k in g:
            g[k][l] = gl[k]
        token = None
    token = dx
    while pending:
        token = finish_oldest(token)
    grad_x = dx

    g = {k: jnp.stack(v) for k, v in g.items()}
    db = _bias_reduce(ds_sum, buckets, hpg, "bias_reduce")
    g["rel_bias"] = db[:, 0, :NUM_BUCKETS].T

    flat = jnp.concatenate([g[k].reshape(-1) for k in SMALL])
    nflat = flat.shape[0]
    rows = -(-nflat // (8 * LANES)) * 8
    packed = jnp.pad(flat, (0, rows * LANES - nflat)).reshape(rows, LANES)
    total = _all_reduce_small(packed, "reduce_small").reshape(-1)
    grad = {}
    off = 0
    for k in SMALL:
        size = int(np.prod(w[k].shape))
        grad[k] = total[off:off + size].reshape(w[k].shape)
        off += size

    for k in BIG:
        total = sums[k]
        if k == "w_in":
            total = jnp.where(odd_core, total[..., SHIFT:], total[..., :w_in.shape[-1]])
        grad[k] = total.reshape(w[k].shape)

    loss = lax.psum(loss[0, 0], ("x", "y", "c"))
    outs = {k: _adamw(w[k], grad[k], mom[k], var[k], "adamw_" + k) for k in WEIGHTS}
    return (loss, grad_x[None], *[grad[k] for k in WEIGHTS], *[outs[k][0] for k in WEIGHTS],
            *[outs[k][1] for k in WEIGHTS], *[outs[k][2] for k in WEIGHTS])
```

```python
import functools
import math

import numpy as np
import jax
import jax.numpy as jnp
from jax import lax
from jax.experimental import pallas as pl
from jax.experimental.pallas import tpu as pltpu

F32 = jnp.float32
BF16 = jnp.bfloat16

HEAD_DIM = 64
N_GROUPS = 3
DILATIONS = (1, 4, 16)
SUB_WINDOW = 128
BLOCK = 128
CONV_WIDTH = 31
CONV_TAPS_PADDED = 32
NUM_BUCKETS = 32
MAX_REL_DISTANCE = 2048
EPS = 1e-6
NEG_INF = -1e30
LANES = 128
SUBLANES = 8

ADAM_LR = 0.001
ADAM_B1 = 0.9
ADAM_B2 = 0.999
ADAM_EPS = 1e-08
ADAM_WD = 0.01
ADAM_STEP = 10

N_DEV = 8
VMEM_LIMIT = 56 * 1024 * 1024
MESH = pl.DeviceIdType.MESH


def _cparams(sem=None):
    return pltpu.CompilerParams(dimension_semantics=sem, vmem_limit_bytes=VMEM_LIMIT)


def _tile(n, target):
    if n <= target:
        return n
    t = (target // LANES) * LANES
    while t >= LANES:
        if n % t == 0:
            return t
        t -= LANES
    return n


def _sigmoid(v):
    return 1.0 / (1.0 + jnp.exp(-v))


MM_VMEM_BUDGET = 40 * 1024 * 1024


def _rms_apply(x, g):
    return x * lax.rsqrt(jnp.mean(x * x, axis=-1, keepdims=True) + EPS) * g


def _rms_grad(dh, x, g):
    r = lax.rsqrt(jnp.mean(x * x, axis=-1, keepdims=True) + EPS)
    xh = x * r
    dxh = dh * g
    dx = r * (dxh - xh * jnp.mean(dxh * xh, axis=-1, keepdims=True))
    return dx, jnp.sum(dh * xh, axis=0, keepdims=True)


def _mm_tiles(m, n, kdim, a_bytes, b_bytes, io_bytes, whole_rows=False, temps=2):
    def need(tm, tn, tk):
        blocks = 2 * (tm * tk * a_bytes + tk * tn * b_bytes + tm * tn * io_bytes)
        casts = (tm * tk * 2 if a_bytes == 4 else 0) + (tk * tn * 2 if b_bytes == 4 else 0)
        return blocks + casts + temps * tm * tn * 4

    tn = n if whole_rows else _tile(n, 1024)
    while True:
        fits = [(tm * tk, tm, tk) for tm in {_tile(m, c) for c in (1024, 512, 256, 128)}
                for tk in {_tile(kdim, c) for c in (2048, 1024, 512, 256)} if need(tm, tn, tk) <= MM_VMEM_BUDGET]
        if fits:
            _, tm, tk = max(fits)
            return tm, tn, tk
        assert not whole_rows and tn % 256 == 0, "no block size fits the VMEM budget"
        tn //= 2


def _mm(a, b, *, ta=False, tb=False, out_dtype=F32, epi=None, extra=(), gain=None, after=None, out_slots=False,
        a_relu2=False, name):
    extra = tuple(extra) if isinstance(extra, (tuple, list)) else (extra,)
    m = a.shape[1] if ta else a.shape[0]
    kdim = a.shape[0] if ta else a.shape[1]
    n = b.shape[0] if tb else b.shape[1]
    norm = epi in ("res_rms", "rms_bwd")
    io_bytes = (jnp.dtype(out_dtype).itemsize + sum(e.dtype.itemsize for e in extra) + (2 if epi == "res_rms" else 0))
    tm, tn, tk = _mm_tiles(m, n // N_DEV if out_slots else n, kdim, a.dtype.itemsize, b.dtype.itemsize, io_bytes,
                           whole_rows=norm, temps=6 if norm else 2)
    if out_slots:
        assert epi is None and tn == n // N_DEV
    nk = kdim // tk
    a_spec = pl.BlockSpec((tk, tm), lambda i, j, k: (k, i)) if ta else pl.BlockSpec((tm, tk), lambda i, j, k: (i, k))
    b_spec = pl.BlockSpec((tn, tk), lambda i, j, k: (j, k)) if tb else pl.BlockSpec((tk, tn), lambda i, j, k: (k, j))
    o_spec = (pl.BlockSpec((None, tm, tn), lambda i, j, k: (j, i, 0)) if out_slots
              else pl.BlockSpec((tm, tn), lambda i, j, k: (i, j)))
    v_spec = pl.BlockSpec((1, tn), lambda i, j, k: (0, j))
    dims = (((0 if ta else 1,), (1 if tb else 0,)), ((), ()))
    n_extra = len(extra)
    n_in = 2 + n_extra + (gain is not None) + (after is not None)
    n_out = 2 if norm else 1

    def body(*refs):
        a_ref, b_ref = refs[0], refs[1]
        e_refs = refs[2:2 + n_extra]
        g_ref = refs[2 + n_extra] if gain is not None else None
        outs = refs[n_in:n_in + n_out]

        def product():
            av = a_ref[...]
            if a_relu2:
                r = jnp.maximum(av.astype(F32), 0.0)
                av = r * r
            return lax.dot_general(av.astype(BF16), b_ref[...].astype(BF16), dims, preferred_element_type=F32)

        def finish(acc):
            if epi is None:
                outs[0][...] = acc.astype(outs[0].dtype)
            elif epi == "res":
                outs[0][...] = (e_refs[0][...] + acc).astype(outs[0].dtype)
            elif epi == "drelu2":
                outs[0][...] = (acc * (2.0 * jnp.maximum(e_refs[0][...].astype(F32), 0.0))).astype(outs[0].dtype)
            elif epi == "res_rms":
                x1 = e_refs[0][...] + acc
                outs[0][...] = x1
                outs[1][...] = _rms_apply(x1, g_ref[...]).astype(BF16)
            elif epi == "rms_bwd":
                dx, dg = _rms_grad(acc, e_refs[0][...], g_ref[...])
                outs[0][...] = e_refs[1][...] + dx
                i = pl.program_id(0)

                @pl.when(i == 0)
                def _():
                    outs[1][...] = dg

                @pl.when(i > 0)
                def _():
                    outs[1][...] += dg

        if nk == 1:
            finish(product())
            return
        acc_ref = refs[-1]
        k = pl.program_id(2)

        @pl.when(k == 0)
        def _():
            acc_ref[...] = product()

        @pl.when(jnp.logical_and(k > 0, k < nk - 1))
        def _():
            acc_ref[...] += product()

        @pl.when(k == nk - 1)
        def _():
            finish(acc_ref[...] + product())

    in_specs = ([a_spec, b_spec] + [o_spec] * n_extra + ([v_spec] if gain is not None else [])
                + ([pl.BlockSpec(memory_space=pl.ANY)] if after is not None else []))
    if epi == "res_rms":
        out_shape = (jax.ShapeDtypeStruct((m, n), F32), jax.ShapeDtypeStruct((m, n), BF16))
        out_specs = (o_spec, o_spec)
    elif epi == "rms_bwd":
        out_shape = (jax.ShapeDtypeStruct((m, n), F32), jax.ShapeDtypeStruct((1, n), F32))
        out_specs = (o_spec, v_spec)
    else:
        out_shape = jax.ShapeDtypeStruct((N_DEV, m, tn) if out_slots else (m, n), out_dtype)
        out_specs = o_spec
    args = (a, b) + extra + ((gain.reshape(1, n),) if gain is not None else ()) + ((after,) if after is not None else ())
    return pl.pallas_call(
        body, name=name, grid=(m // tm, n // tn, nk), in_specs=in_specs, out_specs=out_specs, out_shape=out_shape,
        scratch_shapes=[pltpu.VMEM((tm, tn), F32)] if nk > 1 else [],
        compiler_params=_cparams(("arbitrary", "arbitrary", "arbitrary")),
    )(*args)


SHIFT = HEAD_DIM


def _pair_blocks(e, o):
    wp = e.shape[-1]
    return e[:, :wp - LANES], e[:, wp - LANES:] + o[:, :LANES], o[:, LANES:]


def _mm_in_pairs(a, wg, name):
    t, kdim = a.shape
    wp = wg.shape[-1]
    ws = wp - SHIFT
    tm = _tile(t, 1024)

    def body(a_ref, e_ref, o_ref, u_ref):
        av = a_ref[...]
        lo, mid, hi = _pair_blocks(e_ref[...], o_ref[...])
        u_ref[:, :wp - LANES] = jnp.dot(av, lo, preferred_element_type=F32).astype(BF16)
        u_ref[:, wp - LANES:wp] = jnp.dot(av, mid, preferred_element_type=F32).astype(BF16)
        u_ref[:, wp:] = jnp.dot(av, hi, preferred_element_type=F32).astype(BF16)

    return pl.pallas_call(
        body, name=name, grid=(N_DEV // 2, t // tm),
        in_specs=[pl.BlockSpec((tm, kdim), lambda p, i: (i, 0)),
                  pl.BlockSpec((None, kdim, wp), lambda p, i: (2 * p, 0, 0)),
                  pl.BlockSpec((None, kdim, wp), lambda p, i: (2 * p + 1, 0, 0))],
        out_specs=pl.BlockSpec((tm, 2 * ws), lambda p, i: (i, p)),
        out_shape=jax.ShapeDtypeStruct((t, N_DEV * ws), BF16), compiler_params=_cparams(("arbitrary", "arbitrary")),
    )(a, wg, wg)


def _mm_din_pairs(du, wg, x, gain, dres, after, name):
    t = du.shape[0]
    _, kdim, wp = wg.shape
    ws = wp - SHIFT
    tm = _tile(t, 512)
    npair = N_DEV // 2
    lanes = (((1,), (1,)), ((), ()))
    extra = [] if after is None else [after]

    def body(d_ref, e_ref, o_ref, x_ref, g_ref, r_ref, *rest):
        dx_ref, dg_ref, acc_ref = rest[-3], rest[-2], rest[-1]
        i, p = pl.program_id(0), pl.program_id(1)
        lo, mid, hi = _pair_blocks(e_ref[...], o_ref[...])
        part = (lax.dot_general(d_ref[:, :wp - LANES], lo, lanes, preferred_element_type=F32)
                + lax.dot_general(d_ref[:, wp - LANES:wp], mid, lanes, preferred_element_type=F32)
                + lax.dot_general(d_ref[:, wp:], hi, lanes, preferred_element_type=F32))

        @pl.when(p == 0)
        def _():
            acc_ref[...] = part

        @pl.when(jnp.logical_and(p > 0, p < npair - 1))
        def _():
            acc_ref[...] += part

        @pl.when(p == npair - 1)
        def _():
            dx, dg = _rms_grad(acc_ref[...] + part, x_ref[...], g_ref[...])
            dx_ref[...] = r_ref[...] + dx

            @pl.when(i == 0)
            def _():
                dg_ref[...] = dg

            @pl.when(i > 0)
            def _():
                dg_ref[...] += dg

    row = pl.BlockSpec((tm, kdim), lambda i, p: (i, 0))
    vec = pl.BlockSpec((1, kdim), lambda i, p: (0, 0))
    return pl.pallas_call(
        body, name=name, grid=(t // tm, npair),
        in_specs=[pl.BlockSpec((tm, 2 * ws), lambda i, p: (i, p)),
                  pl.BlockSpec((None, kdim, wp), lambda i, p: (2 * p, 0, 0)),
                  pl.BlockSpec((None, kdim, wp), lambda i, p: (2 * p + 1, 0, 0)), row, vec, row]
        + [pl.BlockSpec(memory_space=pl.ANY)] * len(extra),
        out_specs=(row, vec),
        out_shape=(jax.ShapeDtypeStruct((t, kdim), F32), jax.ShapeDtypeStruct((1, kdim), F32)),
        scratch_shapes=[pltpu.VMEM((tm, kdim), F32)], compiler_params=_cparams(("arbitrary", "arbitrary")),
    )(du, wg, wg, x, gain.reshape(1, kdim), dres, *extra)


def _mm_gw_in_pairs(h, du, after, name):
    t, kdim = h.shape
    ws = du.shape[1] // N_DEV
    wp = ws + SHIFT
    tm = _tile(kdim, 512)
    rows = (((0,), (0,)), ((), ()))
    extra = [] if after is None else [after]

    def body(h_ref, d_ref, *rest):
        g_ref = rest[-1]
        g = lax.dot_general(h_ref[...], d_ref[...], rows, preferred_element_type=F32)
        g_ref[0] = g[:, :wp].astype(BF16)
        g_ref[1] = g[:, wp - LANES:].astype(BF16)

    return pl.pallas_call(
        body, name=name, grid=(N_DEV // 2, kdim // tm),
        in_specs=[pl.BlockSpec((t, tm), lambda p, i: (0, i)), pl.BlockSpec((t, 2 * ws), lambda p, i: (0, p))]
        + [pl.BlockSpec(memory_space=pl.ANY)] * len(extra),
        out_specs=pl.BlockSpec((2, tm, wp), lambda p, i: (p, i, 0)),
        out_shape=jax.ShapeDtypeStruct((N_DEV, kdim, wp), BF16), compiler_params=_cparams(("arbitrary", "arbitrary")),
    )(h, du, *extra)


ROW_BLOCK_BUDGET = 24 * 1024 * 1024


def _rows(t, row_bytes):
    rows = t
    while rows > 8 and (2 * rows * row_bytes > ROW_BLOCK_BUDGET or t % rows):
        rows //= 2
    return rows


def _rms_fwd(x, g, name):
    t, d = x.shape
    ROWS = _rows(t, 6 * d)

    def body(x_ref, g_ref, h_ref):
        h_ref[...] = _rms_apply(x_ref[...], g_ref[...]).astype(BF16)

    return pl.pallas_call(
        body, name=name, grid=(t // ROWS,),
        in_specs=[pl.BlockSpec((ROWS, d), lambda i: (i, 0)), pl.BlockSpec((1, d), lambda i: (0, 0))],
        out_specs=pl.BlockSpec((ROWS, d), lambda i: (i, 0)),
        out_shape=jax.ShapeDtypeStruct((t, d), BF16), compiler_params=_cparams(("arbitrary",)),
    )(x, g.reshape(1, d))


def _gate_fwd(u, yc, ya, gate_col, name):
    t, d = yc.shape
    td = math.gcd(_tile(d, 512), gate_col)
    nd = d // td
    c0 = gate_col // td
    ROWS = _rows(t, 14 * td)

    def body(gc_ref, ga_ref, yc_ref, ya_ref, m_ref):
        gc = _sigmoid(gc_ref[...].astype(F32))
        ga = _sigmoid(ga_ref[...].astype(F32))
        m_ref[...] = (gc * yc_ref[...] + ga * ya_ref[...]).astype(BF16)

    blk = pl.BlockSpec((ROWS, td), lambda i, j: (i, j))
    return pl.pallas_call(
        body, name=name, grid=(t // ROWS, nd),
        in_specs=[pl.BlockSpec((ROWS, td), lambda i, j: (i, c0 + j)),
                  pl.BlockSpec((ROWS, td), lambda i, j: (i, c0 + nd + j)), blk, blk],
        out_specs=blk, out_shape=jax.ShapeDtypeStruct((t, d), BF16),
        compiler_params=_cparams(("arbitrary", "arbitrary")),
    )(u, u, yc, ya)


def _gate_bwd(dm, u, yc, ya, gate_col, name):
    t, d = yc.shape
    td = math.gcd(_tile(d, 512), gate_col)
    nd = d // td
    c0 = gate_col // td
    ROWS = _rows(t, 24 * td)

    def body(dm_ref, gc_ref, ga_ref, yc_ref, ya_ref, dyc_ref, dya_ref, dugc_ref, duga_ref):
        dmv = dm_ref[...]
        gc = _sigmoid(gc_ref[...].astype(F32))
        ga = _sigmoid(ga_ref[...].astype(F32))
        dyc_ref[...] = (dmv * gc).astype(BF16)
        dya_ref[...] = (dmv * ga).astype(BF16)
        dugc_ref[...] = (dmv * yc_ref[...] * gc * (1.0 - gc)).astype(BF16)
        duga_ref[...] = (dmv * ya_ref[...] * ga * (1.0 - ga)).astype(BF16)

    blk = pl.BlockSpec((ROWS, td), lambda i, j: (i, j))
    o = jax.ShapeDtypeStruct((t, d), BF16)
    return pl.pallas_call(
        body, name=name, grid=(t // ROWS, nd),
        in_specs=[blk, pl.BlockSpec((ROWS, td), lambda i, j: (i, c0 + j)),
                  pl.BlockSpec((ROWS, td), lambda i, j: (i, c0 + nd + j)), blk, blk],
        out_specs=(blk, blk, blk, blk), out_shape=(o, o, o, o),
        compiler_params=_cparams(("arbitrary", "arbitrary")),
    )(dm, u, u, yc, ya)


def _loss_and_grad(y, target, name):
    t, d = y.shape
    ROWS = _rows(t, 12 * d)
    n = t // ROWS

    def body(y_ref, t_ref, loss_ref, dy_ref, acc_ref):
        i = pl.program_id(0)

        @pl.when(i == 0)
        def _():
            acc_ref[...] = jnp.zeros_like(acc_ref)

        diff = y_ref[...] - t_ref[...]
        dy_ref[...] = diff * (1.0 / d)
        acc_ref[...] += jnp.sum(diff * diff, axis=0, keepdims=True)

        @pl.when(i == n - 1)
        def _():
            loss_ref[...] = jnp.sum(acc_ref[...], axis=-1, keepdims=True) * (0.5 / d)

    row = pl.BlockSpec((ROWS, d), lambda i: (i, 0))
    return pl.pallas_call(
        body, name=name, grid=(n,), in_specs=[row, row],
        out_specs=(pl.BlockSpec((1, 1), lambda i: (0, 0)), row),
        out_shape=(jax.ShapeDtypeStruct((1, 1), F32), jax.ShapeDtypeStruct((t, d), F32)),
        scratch_shapes=[pltpu.VMEM((1, d), F32)], compiler_params=_cparams(("arbitrary",)),
    )(y, target)


HALO = 32


def _conv_fwd(u, w, b, cdim, name):
    t = u.shape[0]
    ncb = cdim // LANES
    nt = t // BLOCK

    def body(a_ref, g_ref, w_ref, b_ref, zc_ref, zpad):
        zpad[0:HALO, :] = jnp.zeros((HALO, LANES), F32)
        zpad[HALO:HALO + t, :] = a_ref[...].astype(F32) * _sigmoid(g_ref[...].astype(F32))
        wv = w_ref[...]
        bv = b_ref[...]

        def tile(i, carry):
            r0 = pl.multiple_of(i * BLOCK, BLOCK)
            win = zpad[pl.ds(r0, BLOCK + HALO), :]
            acc = jnp.zeros((BLOCK, LANES), F32) + bv
            for b in range(SUBLANES):
                sh = win if b == 0 else pltpu.roll(win, b, 0)
                for a in range(HALO // SUBLANES):
                    j = CONV_WIDTH - 1 - (SUBLANES * a + b)
                    if j >= 0:
                        lo = HALO - SUBLANES * a
                        acc = acc + wv[j:j + 1, :] * sh[lo:lo + BLOCK, :]
            zc_ref[pl.ds(r0, BLOCK), :] = acc
            return carry

        lax.fori_loop(0, nt, tile, 0)

    col = lambda off: pl.BlockSpec((t, LANES), lambda c: (0, off + c))
    return pl.pallas_call(
        body, name=name, grid=(ncb,),
        in_specs=[col(0), col(ncb), pl.BlockSpec((CONV_TAPS_PADDED, LANES), lambda c: (0, c)),
                  pl.BlockSpec((1, LANES), lambda c: (0, c))],
        out_specs=pl.BlockSpec((t, LANES), lambda c: (0, c)),
        out_shape=jax.ShapeDtypeStruct((t, cdim), F32),
        scratch_shapes=[pltpu.VMEM((t + HALO, LANES), F32)], compiler_params=_cparams(("arbitrary",)),
    )(u, u, w, b.reshape(1, cdim))


def _conv_bwd(dzc, u, w, cdim, name):
    t = u.shape[0]
    ncb = cdim // LANES
    nt = t // BLOCK
    win_rows = BLOCK + HALO

    def body(dzc_ref, a_ref, g_ref, w_ref, da_ref, dg_ref, dw_ref, db_ref, zpad, dpad):
        av = a_ref[...].astype(F32)
        sg = _sigmoid(g_ref[...].astype(F32))
        zpad[0:HALO, :] = jnp.zeros((HALO, LANES), F32)
        zpad[HALO:HALO + t, :] = av * sg
        dpad[0:t, :] = dzc_ref[...]
        dpad[t:t + HALO, :] = jnp.zeros((HALO, LANES), F32)
        dw_ref[...] = jnp.zeros_like(dw_ref)
        db_ref[...] = jnp.sum(dzc_ref[...], axis=0, keepdims=True)
        wv = w_ref[...]

        def tile(i, carry):
            r0 = pl.multiple_of(i * BLOCK, BLOCK)
            zwin = zpad[pl.ds(r0, win_rows), :]
            dwin = dpad[pl.ds(r0, win_rows), :]
            dcur = dwin[0:BLOCK, :]
            dz = jnp.zeros((BLOCK, LANES), F32)
            for b in range(SUBLANES):
                zs = zwin if b == 0 else pltpu.roll(zwin, b, 0)
                ds = dwin if b == 0 else pltpu.roll(dwin, win_rows - b, 0)
                for a in range(HALO // SUBLANES):
                    j = CONV_WIDTH - 1 - (SUBLANES * a + b)
                    if j >= 0:
                        lo = HALO - SUBLANES * a
                        dw_ref[j:j + 1, :] += jnp.sum(dcur * zs[lo:lo + BLOCK, :], axis=0, keepdims=True)
                        dz = dz + wv[j:j + 1, :] * ds[SUBLANES * a:SUBLANES * a + BLOCK, :]
            ac = a_ref[pl.ds(r0, BLOCK), :].astype(F32)
            sc = _sigmoid(g_ref[pl.ds(r0, BLOCK), :].astype(F32))
            da_ref[pl.ds(r0, BLOCK), :] = (dz * sc).astype(BF16)
            dg_ref[pl.ds(r0, BLOCK), :] = (dz * ac * sc * (1.0 - sc)).astype(BF16)
            return carry

        lax.fori_loop(0, nt, tile, 0)

    col = lambda off: pl.BlockSpec((t, LANES), lambda c: (0, off + c))
    wspec = pl.BlockSpec((CONV_TAPS_PADDED, LANES), lambda c: (0, c))
    o = jax.ShapeDtypeStruct((t, cdim), BF16)
    return pl.pallas_call(
        body, name=name, grid=(ncb,), in_specs=[col(0), col(0), col(ncb), wspec],
        out_specs=(col(0), col(0), wspec, pl.BlockSpec((1, LANES), lambda c: (0, c))),
        out_shape=(o, o, jax.ShapeDtypeStruct((CONV_TAPS_PADDED, cdim), F32), jax.ShapeDtypeStruct((1, cdim), F32)),
        scratch_shapes=[pltpu.VMEM((t + HALO, LANES), F32), pltpu.VMEM((t + HALO, LANES), F32)],
        compiler_params=_cparams(("arbitrary",)),
    )(dzc, u, u, w)


def _ln_swish_fwd(zc, g, b, name):
    t, c = zc.shape
    ROWS = _rows(t, 6 * c)

    def body(z_ref, g_ref, b_ref, o_ref):
        z = z_ref[...]
        mu = jnp.mean(z, axis=-1, keepdims=True)
        zc_ = z - mu
        zn = zc_ * lax.rsqrt(jnp.mean(zc_ * zc_, axis=-1, keepdims=True) + EPS)
        y = zn * g_ref[...] + b_ref[...]
        o_ref[...] = (y * _sigmoid(y)).astype(BF16)

    row = pl.BlockSpec((ROWS, c), lambda i: (i, 0))
    vec = pl.BlockSpec((1, c), lambda i: (0, 0))
    return pl.pallas_call(
        body, name=name, grid=(t // ROWS,), in_specs=[row, vec, vec], out_specs=row,
        out_shape=jax.ShapeDtypeStruct((t, c), BF16), compiler_params=_cparams(("arbitrary",)),
    )(zc, g.reshape(1, c), b.reshape(1, c))


def _ln_swish_bwd(dzs, zc, g, b, name):
    t, c = zc.shape
    ROWS = _rows(t, 12 * c)

    def body(d_ref, z_ref, g_ref, b_ref, dz_ref, dg_ref, db_ref):
        @pl.when(pl.program_id(0) == 0)
        def _():
            dg_ref[...] = jnp.zeros_like(dg_ref)
            db_ref[...] = jnp.zeros_like(db_ref)

        z = z_ref[...]
        mu = jnp.mean(z, axis=-1, keepdims=True)
        zc_ = z - mu
        rstd = lax.rsqrt(jnp.mean(zc_ * zc_, axis=-1, keepdims=True) + EPS)
        zn = zc_ * rstd
        y = zn * g_ref[...] + b_ref[...]
        sg = _sigmoid(y)
        dy = d_ref[...] * (sg * (1.0 + y * (1.0 - sg)))
        dg_ref[...] += jnp.sum(dy * zn, axis=0, keepdims=True)
        db_ref[...] += jnp.sum(dy, axis=0, keepdims=True)
        dzn = dy * g_ref[...]
        dz_ref[...] = rstd * (dzn - jnp.mean(dzn, axis=-1, keepdims=True)
                              - zn * jnp.mean(dzn * zn, axis=-1, keepdims=True))

    row = pl.BlockSpec((ROWS, c), lambda i: (i, 0))
    vec = pl.BlockSpec((1, c), lambda i: (0, 0))
    v = jax.ShapeDtypeStruct((1, c), F32)
    return pl.pallas_call(
        body, name=name, grid=(t // ROWS,), in_specs=[row, row, vec, vec], out_specs=(row, vec, vec),
        out_shape=(jax.ShapeDtypeStruct((t, c), F32), v, v), compiler_params=_cparams(("arbitrary",)),
    )(dzs, zc, g.reshape(1, c), b.reshape(1, c))


def _bucket_table():
    qi = np.arange(BLOCK)[:, None]
    kj = np.arange(2 * BLOCK)[None, :]
    off = qi + BLOCK - kj
    band = (off >= 0) & (off <= SUB_WINDOW)
    max_exact = NUM_BUCKETS // 2
    out = []
    for d in DILATIONS:
        dist = (np.clip(off, 0, SUB_WINDOW) * d).astype(np.int32)
        nf = np.maximum(dist, 1).astype(np.float32)
        large = max_exact + (np.log(nf / np.float32(max_exact)) / np.float32(math.log(MAX_REL_DISTANCE / max_exact))
                             * np.float32(NUM_BUCKETS - max_exact)).astype(np.int32)
        large = np.minimum(large, NUM_BUCKETS - 1)
        bucket = np.where(dist < max_exact, dist, large)
        out.append(np.where(band, bucket, -1))
    return np.stack(out).astype(np.int32)


def _bias_expand(rel_bias, buckets, hpg, name):
    nh = N_GROUPS * hpg

    def body(rb_ref, bk_ref, o_ref):
        h = pl.program_id(0)
        bk = bk_ref[0]
        acc = jnp.full((BLOCK, 2 * BLOCK), NEG_INF, F32)
        for bb in range(NUM_BUCKETS):
            acc = jnp.where(bk == bb, rb_ref[bb, h], acc)
        o_ref[0] = acc

    return pl.pallas_call(
        body, name=name, grid=(nh,),
        in_specs=[pl.BlockSpec(memory_space=pltpu.SMEM),
                  pl.BlockSpec((1, BLOCK, 2 * BLOCK), lambda h: (h // hpg, 0, 0))],
        out_specs=pl.BlockSpec((1, BLOCK, 2 * BLOCK), lambda h: (h, 0, 0)),
        out_shape=jax.ShapeDtypeStruct((nh, BLOCK, 2 * BLOCK), F32), compiler_params=_cparams(("arbitrary",)),
    )(rel_bias, buckets)


def _bias_reduce(ds_sum, buckets, hpg, name):
    nh = N_GROUPS * hpg

    def body(ds_ref, bk_ref, o_ref):
        bk = bk_ref[0]
        dsv = ds_ref[0]
        lane = lax.broadcasted_iota(jnp.int32, (1, LANES), 1)
        row = jnp.zeros((1, LANES), F32)
        for bb in range(NUM_BUCKETS):
            tot = jnp.sum(jnp.sum(jnp.where(bk == bb, dsv, 0.0), axis=-1, keepdims=True), axis=0, keepdims=True)
            row = jnp.where(lane == bb, tot, row)
        o_ref[0] = row

    return pl.pallas_call(
        body, name=name, grid=(nh,),
        in_specs=[pl.BlockSpec((1, BLOCK, 2 * BLOCK), lambda h: (h, 0, 0)),
                  pl.BlockSpec((1, BLOCK, 2 * BLOCK), lambda h: (h // hpg, 0, 0))],
        out_specs=pl.BlockSpec((1, 1, LANES), lambda h: (h, 0, 0)),
        out_shape=jax.ShapeDtypeStruct((nh, 1, LANES), F32), compiler_params=_cparams(("arbitrary",)),
    )(ds_sum, buckets)


def _chunk_rows(c, d, nb):
    r, n = c // nb, c % nb
    if d == 1:
        return pl.ds(c * BLOCK, BLOCK)
    return pl.ds(r + n * BLOCK * d, BLOCK, stride=d)


def _segment_ones():
    i = lax.broadcasted_iota(jnp.int32, (LANES, LANES), 0) // HEAD_DIM
    j = lax.broadcasted_iota(jnp.int32, (LANES, LANES), 1) // HEAD_DIM
    return (i == j).astype(BF16)


def _segment_sum(v, seg):
    hi = v.astype(BF16)
    lo = (v - hi.astype(F32)).astype(BF16)
    return jnp.dot(hi, seg, preferred_element_type=F32) + jnp.dot(lo, seg, preferred_element_type=F32)


def _head_mean(v, seg):
    return _segment_sum(v, seg) * (1.0 / HEAD_DIM)


def _attn_fwd(u, qg2, kg2, bias, gi, cols, hpg, name):
    t = u.shape[0]
    d = DILATIONS[gi]
    nchunk = t // BLOCK
    nb = (t // d) // BLOCK
    hp = hpg // 2
    qc0, kc0, vc0 = [(c + gi * hpg * HEAD_DIM) // LANES for c in cols]
    contract_lanes = (((1,), (1,)), ((), ()))

    def body(q_ref, k_ref, v_ref, qg_ref, kg_ref, bias_ref, o_ref, lse_ref, qd, kd, vd, od, ld, sbuf):
        seg = _segment_ones()
        lane = lax.broadcasted_iota(jnp.int32, (1, LANES), 1)
        qg = qg_ref[...] * (HEAD_DIM ** -0.5)
        kg = kg_ref[...]
        kd[0:BLOCK, :] = jnp.zeros((BLOCK, LANES), BF16)
        vd[0:BLOCK, :] = jnp.zeros((BLOCK, LANES), BF16)
        od[...] = q_ref[...].astype(F32)
        ld[...] = k_ref[...].astype(F32)
        for c in range(nchunk):
            rows = _chunk_rows(c, d, nb)
            qv = od[rows, :]
            kv = ld[rows, :]
            qd[c * BLOCK:(c + 1) * BLOCK, :] = (qv * lax.rsqrt(_head_mean(qv * qv, seg) + EPS) * qg).astype(BF16)
            kd[(c + 1) * BLOCK:(c + 2) * BLOCK, :] = (kv * lax.rsqrt(_head_mean(kv * kv, seg) + EPS) * kg).astype(BF16)
        od[...] = v_ref[...].astype(F32)
        for c in range(nchunk):
            vd[(c + 1) * BLOCK:(c + 2) * BLOCK, :] = od[_chunk_rows(c, d, nb), :].astype(BF16)

        col = lax.broadcasted_iota(jnp.int32, (BLOCK, 2 * BLOCK), 1)
        for j in range(2):
            mj = jnp.logical_and(lane >= j * HEAD_DIM, lane < (j + 1) * HEAD_DIM)
            for c in range(nchunk):
                kw = kd[c * BLOCK:(c + 2) * BLOCK, :]
                kj = jnp.where(mj, kw, jnp.zeros_like(kw))
                s = lax.dot_general(qd[c * BLOCK:(c + 1) * BLOCK, :], kj, contract_lanes,
                                    preferred_element_type=F32) + bias_ref[j]
                if c % nb == 0:
                    s = jnp.where(col < BLOCK, NEG_INF, s)
                sbuf[c] = s
            for c in range(nchunk):
                rows = slice(c * BLOCK, (c + 1) * BLOCK)
                s = sbuf[c]
                mx = jnp.max(s, axis=-1, keepdims=True)
                p = jnp.exp(s - mx).astype(BF16)
                vw = vd[c * BLOCK:(c + 2) * BLOCK, :]
                oj = jnp.dot(p, jnp.where(mj, vw, jnp.ones_like(vw)), preferred_element_type=F32)
                l = pltpu.roll(oj, HEAD_DIM, 1)
                on = oj / l
                ls = mx + jnp.log(l)
                if j == 0:
                    od[rows, :] = on
                    ld[rows, :] = ls
                else:
                    od[rows, :] = jnp.where(mj, on, od[rows, :])
                    ld[rows, :] = jnp.where(mj, ls, ld[rows, :])

        for c in range(nchunk):
            rows = _chunk_rows(c, d, nb)
            o_ref[rows, :] = od[c * BLOCK:(c + 1) * BLOCK, :]
            lse_ref[rows, :] = ld[c * BLOCK:(c + 1) * BLOCK, :]

    ucol = lambda c0: pl.BlockSpec((t, LANES), lambda h: (0, c0 + h))
    vec = pl.BlockSpec((1, LANES), lambda h: (0, 0))
    oblk = pl.BlockSpec((t, LANES), lambda h: (0, h))
    osh = jax.ShapeDtypeStruct((t, hpg * HEAD_DIM), F32)
    return pl.pallas_call(
        body, name=name, grid=(hp,),
        in_specs=[ucol(qc0), ucol(kc0), ucol(vc0), vec, vec,
                  pl.BlockSpec((2, BLOCK, 2 * BLOCK), lambda h: (gi * hp + h, 0, 0))],
        out_specs=(oblk, oblk), out_shape=(osh, osh),
        scratch_shapes=[pltpu.VMEM((t, LANES), BF16), pltpu.VMEM((t + BLOCK, LANES), BF16),
                        pltpu.VMEM((t + BLOCK, LANES), BF16), pltpu.VMEM((t, LANES), F32), pltpu.VMEM((t, LANES), F32),
                        pltpu.VMEM((nchunk, BLOCK, 2 * BLOCK), F32)],
        compiler_params=_cparams(("arbitrary",)),
    )(u, u, u, qg2, kg2, bias)


def _attn_bwd(u, do_g, dd_g, lse_g, qg2, kg2, bias, ds_in, dqkv_in, gi, cols, hpg, name):
    t = u.shape[0]
    d = DILATIONS[gi]
    nchunk = t // BLOCK
    nb = (t // d) // BLOCK
    hp = hpg // 2
    qc0, kc0, vc0 = [(c + gi * hpg * HEAD_DIM) // LANES for c in cols]
    contract_lanes = (((1,), (1,)), ((), ()))
    contract_rows = (((0,), (0,)), ((), ()))
    qscale = HEAD_DIM ** -0.5

    def body(q_ref, k_ref, v_ref, do_ref, dd_ref, lse_ref, qg_ref, kg_ref, bias_ref, dsin_ref, _dq_in, _dk_in, _dv_in,
             dq_ref, dk_ref, dv_ref, dgq_ref, dgk_ref, dsout_ref,
             qd, kd, vd, dod, ddd, ld, dqd, dkd, dvd, dsacc, pbuf, dsbuf, qs, ks):
        seg = _segment_ones()
        lane = lax.broadcasted_iota(jnp.int32, (1, LANES), 1)
        qg = qg_ref[...] * qscale
        kg = kg_ref[...]
        kd[0:BLOCK, :] = jnp.zeros((BLOCK, LANES), BF16)
        vd[0:BLOCK, :] = jnp.zeros((BLOCK, LANES), BF16)
        dsacc[...] = jnp.zeros_like(dsacc)
        qs[...] = q_ref[...].astype(F32)
        ks[...] = k_ref[...].astype(F32)
        dqd[...] = v_ref[...].astype(F32)
        for c in range(nchunk):
            rows = _chunk_rows(c, d, nb)
            qv = qs[rows, :]
            kv = ks[rows, :]
            qd[c * BLOCK:(c + 1) * BLOCK, :] = (qv * lax.rsqrt(_head_mean(qv * qv, seg) + EPS) * qg).astype(BF16)
            kd[(c + 1) * BLOCK:(c + 2) * BLOCK, :] = (kv * lax.rsqrt(_head_mean(kv * kv, seg) + EPS) * kg).astype(BF16)
            vd[(c + 1) * BLOCK:(c + 2) * BLOCK, :] = dqd[rows, :].astype(BF16)
            dod[c * BLOCK:(c + 1) * BLOCK, :] = do_ref[rows, :].astype(BF16)
            ddd[c * BLOCK:(c + 1) * BLOCK, :] = dd_ref[rows, :]
            ld[c * BLOCK:(c + 1) * BLOCK, :] = lse_ref[rows, :]

        col = lax.broadcasted_iota(jnp.int32, (BLOCK, 2 * BLOCK), 1)
        for j in range(2):
            mj = jnp.logical_and(lane >= j * HEAD_DIM, lane < (j + 1) * HEAD_DIM)
            first = lane == j * HEAD_DIM
            for c in range(nchunk):
                rows = slice(c * BLOCK, (c + 1) * BLOCK)
                kw = kd[c * BLOCK:(c + 2) * BLOCK, :]
                vw = vd[c * BLOCK:(c + 2) * BLOCK, :]
                kj = jnp.where(mj, kw, jnp.zeros_like(kw))
                vj = jnp.where(mj, vw, jnp.zeros_like(vw))
                s = lax.dot_general(qd[rows, :], kj, contract_lanes, preferred_element_type=F32) + bias_ref[j]
                if c % nb == 0:
                    s = jnp.where(col < BLOCK, NEG_INF, s)
                dp = lax.dot_general(dod[rows, :], vj, contract_lanes, preferred_element_type=F32)
                lse_j = jnp.sum(jnp.where(first, ld[rows, :], 0.0), axis=-1, keepdims=True)
                dd_j = jnp.sum(jnp.where(first, ddd[rows, :], 0.0), axis=-1, keepdims=True)
                p = jnp.exp(s - lse_j)
                ds = p * (dp + dd_j)
                dsacc[j] += ds
                pbuf[j, c] = p.astype(BF16)
                dsbuf[j, c] = ds.astype(BF16)
        for c in range(nchunk):
            rows = slice(c * BLOCK, (c + 1) * BLOCK)
            has_next = c + 1 < nchunk and (c + 1) % nb != 0
            kw = kd[c * BLOCK:(c + 2) * BLOCK, :]
            dq = jnp.zeros((BLOCK, LANES), F32)
            dk = jnp.zeros((BLOCK, LANES), F32)
            dv = jnp.zeros((BLOCK, LANES), F32)
            both = slice(c * BLOCK, (c + 2) * BLOCK) if has_next else rows
            for j in range(2):
                mj = jnp.logical_and(lane >= j * HEAD_DIM, lane < (j + 1) * HEAD_DIM)
                dq = dq + jnp.dot(dsbuf[j, c], jnp.where(mj, kw, jnp.zeros_like(kw)), preferred_element_type=F32)
                dsk = dsbuf[j, c, :, BLOCK:]
                pk = pbuf[j, c, :, BLOCK:]
                if has_next:
                    dsk = jnp.concatenate([dsk, dsbuf[j, c + 1, :, :BLOCK]], axis=0)
                    pk = jnp.concatenate([pk, pbuf[j, c + 1, :, :BLOCK]], axis=0)
                qq = qd[both, :]
                dd = dod[both, :]
                dk = dk + lax.dot_general(dsk, jnp.where(mj, qq, jnp.zeros_like(qq)), contract_rows,
                                          preferred_element_type=F32)
                dv = dv + lax.dot_general(pk, jnp.where(mj, dd, jnp.zeros_like(dd)), contract_rows,
                                          preferred_element_type=F32)
            dqd[rows, :] = dq
            dkd[rows, :] = dk
            dvd[rows, :] = dv

        dsout_ref[...] = dsin_ref[...] + dsacc[...]

        dgq = jnp.zeros((1, LANES), F32)
        dgk = jnp.zeros((1, LANES), F32)
        for c in range(nchunk):
            rows = _chunk_rows(c, d, nb)
            qv = qs[rows, :]
            rq = lax.rsqrt(_head_mean(qv * qv, seg) + EPS)
            qh = qv * rq
            dy = dqd[c * BLOCK:(c + 1) * BLOCK, :]
            dgq = dgq + jnp.sum(dy * qh, axis=0, keepdims=True) * qscale
            dxh = dy * qg
            ddd[rows, :] = rq * (dxh - qh * _head_mean(dxh * qh, seg))
            kv = ks[rows, :]
            rk = lax.rsqrt(_head_mean(kv * kv, seg) + EPS)
            kh = kv * rk
            dy = dkd[c * BLOCK:(c + 1) * BLOCK, :]
            dgk = dgk + jnp.sum(dy * kh, axis=0, keepdims=True)
            dxh = dy * kg
            ld[rows, :] = rk * (dxh - kh * _head_mean(dxh * kh, seg))
        dq_ref[...] = ddd[...].astype(BF16)
        dk_ref[...] = ld[...].astype(BF16)
        for c in range(nchunk):
            ddd[_chunk_rows(c, d, nb), :] = dvd[c * BLOCK:(c + 1) * BLOCK, :]
        dv_ref[...] = ddd[...].astype(BF16)
        dgq_ref[0] = dgq
        dgk_ref[0] = dgk

    ucol = lambda c0: pl.BlockSpec((t, LANES), lambda h: (0, c0 + h))
    vec = pl.BlockSpec((1, LANES), lambda h: (0, 0))
    oblk = pl.BlockSpec((t, LANES), lambda h: (0, h))
    bblk = pl.BlockSpec((2, BLOCK, 2 * BLOCK), lambda h: (gi * hp + h, 0, 0))
    gblk = pl.BlockSpec((1, 1, LANES), lambda h: (h, 0, 0))
    osh = jax.ShapeDtypeStruct((t, N_GROUPS * hpg * HEAD_DIM), BF16)
    gsh = jax.ShapeDtypeStruct((hp, 1, LANES), F32)
    gcol = pl.BlockSpec((t, LANES), lambda h: (0, gi * hp + h))
    hbm = pl.BlockSpec(memory_space=pl.ANY)
    return pl.pallas_call(
        body, name=name, grid=(hp,),
        in_specs=[ucol(qc0), ucol(kc0), ucol(vc0), oblk, oblk, oblk, vec, vec, bblk, bblk, hbm, hbm, hbm],
        out_specs=(gcol, gcol, gcol, gblk, gblk, bblk),
        out_shape=(osh, osh, osh, gsh, gsh, jax.ShapeDtypeStruct(ds_in.shape, F32)),
        input_output_aliases={9: 5, 10: 0, 11: 1, 12: 2},
        scratch_shapes=[pltpu.VMEM((t, LANES), BF16), pltpu.VMEM((t + BLOCK, LANES), BF16),
                        pltpu.VMEM((t + BLOCK, LANES), BF16), pltpu.VMEM((t, LANES), BF16),
                        pltpu.VMEM((t, LANES), F32), pltpu.VMEM((t, LANES), F32), pltpu.VMEM((t, LANES), F32),
                        pltpu.VMEM((t, LANES), F32), pltpu.VMEM((t, LANES), F32),
                        pltpu.VMEM((2, BLOCK, 2 * BLOCK), F32), pltpu.VMEM((2, nchunk, BLOCK, 2 * BLOCK), BF16),
                        pltpu.VMEM((2, nchunk, BLOCK, 2 * BLOCK), BF16), pltpu.VMEM((t, LANES), F32),
                        pltpu.VMEM((t, LANES), F32)],
        compiler_params=_cparams(("arbitrary",)),
    )(u, u, u, do_g, dd_g, lse_g, qg2, kg2, bias, ds_in, *dqkv_in)


def _group_weights(l0, l1, l2):
    mx = jnp.maximum(jnp.maximum(l0, l1), l2)
    e0, e1, e2 = jnp.exp(l0 - mx), jnp.exp(l1 - mx), jnp.exp(l2 - mx)
    inv = 1.0 / (e0 + e1 + e2)
    return e0 * inv, e1 * inv, e2 * inv


def _combine_fwd(os_, lses, name):
    t, ao = os_[0].shape
    ROWS = _rows(t, 26 * ao)

    def body(o0, o1, o2, l0, l1, l2, o_ref):
        w0, w1, w2 = _group_weights(l0[...], l1[...], l2[...])
        o_ref[...] = (w0 * o0[...] + w1 * o1[...] + w2 * o2[...]).astype(BF16)

    row = pl.BlockSpec((ROWS, ao), lambda i: (i, 0))
    return pl.pallas_call(
        body, name=name, grid=(t // ROWS,), in_specs=[row] * 6, out_specs=row,
        out_shape=jax.ShapeDtypeStruct((t, ao), BF16), compiler_params=_cparams(("arbitrary",)),
    )(*os_, *lses)


def _combine_bwd(do, os_, lses, name):
    t, ao = do.shape
    idx = np.arange(ao) // HEAD_DIM
    seg = jnp.asarray((idx[:, None] == idx[None, :]).astype(np.float32), dtype=BF16)
    ROWS = _rows(t, 52 * ao)

    def body(do_ref, o0, o1, o2, l0, l1, l2, seg_ref, g0, g1, g2, d0, d1, d2):
        w0, w1, w2 = _group_weights(l0[...], l1[...], l2[...])
        dov = do_ref[...]
        o = w0 * o0[...] + w1 * o1[...] + w2 * o2[...]
        sd = _segment_sum(dov * o, seg_ref[...])
        for w, gref, dref in ((w0, g0, d0), (w1, g1, d1), (w2, g2, d2)):
            gref[...] = w * dov
            dref[...] = -(w * sd)

    row = pl.BlockSpec((ROWS, ao), lambda i: (i, 0))
    sh = jax.ShapeDtypeStruct((t, ao), F32)
    outs = pl.pallas_call(
        body, name=name, grid=(t // ROWS,), in_specs=[row] * 7 + [pl.BlockSpec((ao, ao), lambda i: (0, 0))],
        out_specs=(row,) * 6, out_shape=(sh,) * 6, compiler_params=_cparams(("arbitrary",)),
    )(do, *os_, *lses, seg)
    return outs[:3], outs[3:]


def _adamw(w, g, m, v, name):
    shape = w.shape
    cols = shape[-1]
    rows = int(np.prod(shape[:-1]))
    tr = rows if rows <= 512 else _tile_rows(rows)
    c1 = 1.0 - ADAM_B1 ** ADAM_STEP
    c2 = 1.0 - ADAM_B2 ** ADAM_STEP

    def body(w_ref, g_ref, m_ref, v_ref, d_ref, nm_ref, nv_ref):
        gv = g_ref[...]
        mn = ADAM_B1 * m_ref[...] + (1.0 - ADAM_B1) * gv
        vn = ADAM_B2 * v_ref[...] + (1.0 - ADAM_B2) * (gv * gv)
        nm_ref[...] = mn
        nv_ref[...] = vn
        d_ref[...] = -ADAM_LR * ((mn / c1) / (jnp.sqrt(vn / c2) + ADAM_EPS) + ADAM_WD * w_ref[...])

    blk = pl.BlockSpec((tr, cols), lambda i: (i, 0))
    sh = jax.ShapeDtypeStruct((rows, cols), F32)
    outs = pl.pallas_call(
        body, name=name, grid=(rows // tr,), in_specs=[blk] * 4, out_specs=(blk,) * 3, out_shape=(sh,) * 3,
        compiler_params=_cparams(("arbitrary",)),
    )(*[a.reshape(rows, cols) for a in (w, g, m, v)])
    return tuple(o.reshape(shape) for o in outs)


def _tile_rows(rows):
    for t in (512, 256, 128, 64, 32, 16, 8):
        if rows % t == 0:
            return t
    return rows


def _sum_slots(recv, parts, me, layers, l, name):
    _, rows, cols = recv.shape
    tr = rows if rows <= 512 else _tile_rows(rows)

    def body(me_ref, r_ref, own_ref, _, o_ref):
        acc = jnp.zeros(o_ref.shape, F32)
        for s in range(N_DEV):
            acc = acc + jnp.where(me_ref[0] == s, own_ref[...], r_ref[s]).astype(F32)
        o_ref[...] = acc

    return pl.pallas_call(
        body, name=name,
        grid_spec=pltpu.PrefetchScalarGridSpec(
            num_scalar_prefetch=1, grid=(rows // tr,),
            in_specs=[pl.BlockSpec((N_DEV, tr, cols), lambda i, me: (0, i, 0)),
                      pl.BlockSpec((None, tr, cols), lambda i, me: (me[0], i, 0)),
                      pl.BlockSpec(memory_space=pl.ANY)],
            out_specs=pl.BlockSpec((None, tr, cols), lambda i, me: (l, i, 0))),
        out_shape=jax.ShapeDtypeStruct(layers.shape, F32), input_output_aliases={3: 0},
        compiler_params=_cparams(("arbitrary",)),
    )(me.reshape(1), recv, parts, layers)


def _peer(k):
    x, y, c = lax.axis_index("x"), lax.axis_index("y"), lax.axis_index("c")
    return (1 - x if k & 4 else x, 1 - y if k & 2 else y, 1 - c if k & 1 else c)


def _dev_index(p):
    return 4 * p[0] + 2 * p[1] + p[2]


HBM_SPEC = pl.BlockSpec(memory_space=pltpu.HBM)
SEM_SPEC = pl.BlockSpec(memory_space=pltpu.SEMAPHORE)
ANY_SPEC = pl.BlockSpec(memory_space=pl.ANY)
CHIPS = (4, 2, 6)


def _remote(src, dst, send_sem, recv_sem, to):
    return pltpu.make_async_remote_copy(src_ref=src, dst_ref=dst, send_sem=send_sem, recv_sem=recv_sem,
                                        device_id=to, device_id_type=MESH)


def _hbm(a):
    return pltpu.with_memory_space_constraint(a, pltpu.HBM)


def _split_call(body, name, bufs, sems_in, sem_out_sizes, after):
    nb, ns, no = len(bufs), len(sems_in), len(sem_out_sizes)
    extra = [] if after is None else [after]

    def kern(*refs):
        pos = nb + ns + len(extra)
        body(refs[:nb], refs[nb:nb + ns], refs[pos:pos + no])
        token_ref = refs[pos + no + nb]
        token_ref[...] = jnp.zeros_like(token_ref)

    out_shape = (tuple(pltpu.SemaphoreType.DMA((s,)) for s in sem_out_sizes)
                 + tuple(pltpu.HBM(b.shape, b.dtype) for b in bufs) + (jax.ShapeDtypeStruct((8, LANES), F32),))
    res = pl.pallas_call(
        kern, name=name, out_shape=out_shape,
        in_specs=[HBM_SPEC] * nb + [SEM_SPEC] * ns + [ANY_SPEC] * len(extra),
        out_specs=(SEM_SPEC,) * no + (HBM_SPEC,) * nb + (pl.BlockSpec(memory_space=pltpu.VMEM),),
        input_output_aliases={i: no + i for i in range(nb)},
        compiler_params=pltpu.CompilerParams(has_side_effects=pltpu.SideEffectType.DATAFLOW_SIDE_EFFECTING),
    )(*bufs, *sems_in, *extra)
    return res[:no], res[no:no + nb], res[no + nb]


def _gather_start(shards, lands, after, name):
    n = len(shards)

    def body(bufs, _, sems):
        ins, lnd = bufs[:n], bufs[n:]
        d2d_s, d2d_r, ici_s, ici_r = sems
        me = _dev_index(_peer(0))
        for j, k in enumerate(CHIPS):
            for i in range(n):
                _remote(ins[i], lnd[i].at[me], ici_s.at[j], ici_r.at[j], _peer(k)).start()
        for i in range(n):
            _remote(ins[i], lnd[i].at[me], d2d_s.at[0], d2d_r.at[0], _peer(1)).start()

    return _split_call(body, name, [_hbm(a) for a in (*shards, *lands)], [], (1, 1, 3, 3), after)


def _gather_forward(n, bufs, ici_r, after, name):
    def body(refs, sems_in, sems):
        ins, lnd = refs[:n], refs[n:]
        (arrived,) = sems_in
        fwd_s, fwd_r = sems
        for j, k in enumerate(CHIPS):
            blk = _dev_index(_peer(k))
            for i in range(n):
                _remote(ins[i], lnd[i].at[blk], fwd_s.at[j], arrived.at[j], _peer(k)).wait_recv()
            for i in range(n):
                _remote(lnd[i].at[blk], lnd[i].at[blk], fwd_s.at[j], fwd_r.at[j], _peer(1)).start()

    return _split_call(body, name, bufs, [ici_r], (3, 3), after)


def _gather_finish(n, bufs, d2d_s, d2d_r, ici_s, fwd_s, fwd_r, after, name):
    def body(refs, sems_in, _):
        ins, lnd = refs[:n], refs[n:]
        d2d_send, d2d_recv, ici_send, fwd_send, fwd_recv = sems_in
        sib = _peer(1)
        for i in range(n):
            cp = _remote(ins[i], lnd[i].at[_dev_index(sib)], d2d_send.at[0], d2d_recv.at[0], sib)
            cp.wait_send()
            cp.wait_recv()
        for j, k in enumerate(CHIPS):
            passed = _dev_index(_peer(k))
            landed = _dev_index(_peer(k | 1))
            for i in range(n):
                _remote(ins[i], lnd[i].at[passed], ici_send.at[j], fwd_recv.at[j], _peer(k)).wait_send()
                cp = _remote(lnd[i].at[passed], lnd[i].at[landed], fwd_send.at[j], fwd_recv.at[j], sib)
                cp.wait_send()
                cp.wait_recv()

    _, out, token = _split_call(body, name, bufs, [d2d_s, d2d_r, ici_s, fwd_s, fwd_r], (), after)
    return out[n:], token


def _exchange_start(parts, lands, after, name):
    n = len(parts)

    def body(bufs, _, sems):
        src, lnd = bufs[:n], bufs[n:]
        send, recv = sems
        me = _dev_index(_peer(0))
        for k in (4, 5, 2, 3, 6, 7, 1):
            to = _peer(k)
            for i in range(n):
                _remote(src[i].at[_dev_index(to)], lnd[i].at[me], send.at[k - 1], recv.at[k - 1], to).start()

    return _split_call(body, name, [_hbm(a) for a in (*parts, *lands)], [], (7, 7), after)


def _exchange_finish(n, bufs, send, recv, after, name):
    def body(refs, sems_in, _):
        src, lnd = refs[:n], refs[n:]
        send_, recv_ = sems_in
        me = _dev_index(_peer(0))
        for k in range(1, N_DEV):
            frm = _peer(k)
            for i in range(n):
                cp = _remote(src[i].at[me], lnd[i].at[_dev_index(frm)], send_.at[k - 1], recv_.at[k - 1], frm)
                cp.wait_send()
                cp.wait_recv()

    _, out, token = _split_call(body, name, bufs, [send, recv], (), after)
    return out[:n], out[n:], token


def _all_reduce_small(v, name):
    rows = v.shape[0]

    def body(v_ref, o_ref, buf, send_sems, recv_sems):
        me = _dev_index(_peer(0))
        buf[me] = v_ref[...]
        copies = []
        for k in range(1, N_DEV):
            copies.append(pltpu.make_async_remote_copy(
                src_ref=v_ref, dst_ref=buf.at[me], send_sem=send_sems.at[k - 1], recv_sem=recv_sems.at[k - 1],
                device_id=_peer(k), device_id_type=MESH))
        for cp in copies:
            cp.start()
        for k in range(1, N_DEV):
            pltpu.make_async_remote_copy(
                src_ref=v_ref, dst_ref=buf.at[_dev_index(_peer(k))], send_sem=send_sems.at[k - 1],
                recv_sem=recv_sems.at[k - 1], device_id=_peer(k), device_id_type=MESH).wait_recv()
        for cp in copies:
            cp.wait_send()
        acc = buf[0]
        for s in range(1, N_DEV):
            acc = acc + buf[s]
        o_ref[...] = acc

    vm = pl.BlockSpec(memory_space=pltpu.VMEM)
    return pl.pallas_call(
        body, name=name, in_specs=[vm], out_specs=vm, out_shape=jax.ShapeDtypeStruct(v.shape, F32),
        scratch_shapes=[pltpu.VMEM((N_DEV, rows, LANES), F32), pltpu.SemaphoreType.DMA((7,)),
                        pltpu.SemaphoreType.DMA((7,))],
    )(v)


def _columns(cdim, ao):
    q_col = 2 * cdim
    attn_dim = N_GROUPS * ao
    return (q_col, q_col + attn_dim, q_col + 2 * attn_dim), q_col + 3 * attn_dim


def _layer_fwd(x, h1, sm, bg, get_rest, bias, hpg, next_gain):
    cdim = sm["conv_ln_g"].shape[0]
    ao = hpg * HEAD_DIM
    cols, gate_col = _columns(cdim, ao)
    qg2 = jnp.tile(sm["q_norm_g"], 2).reshape(1, LANES)
    kg2 = jnp.tile(sm["k_norm_g"], 2).reshape(1, LANES)
    u = _mm_in_pairs(h1, bg["w_in"], "mm_in")
    zc = _conv_fwd(u, bg["conv_dw_w"], sm["conv_dw_b"], cdim, "conv_fwd")
    zs = _ln_swish_fwd(zc, sm["conv_ln_g"], sm["conv_ln_b"], "ln_swish_fwd")
    os_, lses = [], []
    for gi in range(N_GROUPS):
        o_g, lse_g = _attn_fwd(u, qg2, kg2, bias, gi, cols, hpg, "attn_fwd_g%d" % gi)
        os_.append(o_g)
        lses.append(lse_g)
    o = _combine_fwd(os_, lses, "combine_fwd")
    bg = {**bg, **get_rest(o)}
    yc = _mm(zs, bg["w_conv_out"], name="mm_conv_out")
    ya = _mm(o, bg["w_attn_out"], name="mm_attn_out")
    mg = _gate_fwd(u, yc, ya, gate_col, "gate_fwd")
    x1, h2 = _mm(mg, bg["w_out"], epi="res_rms", extra=x, gain=sm["norm2_g"], name="mm_out")
    f = _mm(h2, bg["w_ff1"], out_dtype=BF16, name="mm_ff1")
    if next_gain is None:
        x2, h_next = _mm(f, bg["w_ff2"], a_relu2=True, epi="res", extra=x1, name="mm_ff2"), None
    else:
        x2, h_next = _mm(f, bg["w_ff2"], a_relu2=True, epi="res_rms", extra=x1, gain=next_gain, name="mm_ff2")
    saved = dict(x=x, h1=h1, u=u, zc=zc, zs=zs, yc=yc, os=os_, lses=lses, o=o, ya=ya, mg=mg, x1=x1, h2=h2, f=f,
                 qg2=qg2, kg2=kg2)
    return x2, h_next, saved, bg


GRAD_GROUPS = (("w_ff2", "w_ff1"), ("w_out", "w_conv_out", "w_attn_out", "conv_dw_w"), ("w_in",))


def _layer_bwd(dx, s, sm, bg, bias, ds_sum, after, emit):
    cdim = sm["conv_ln_g"].shape[0]
    ao = bg["w_attn_out"].shape[0]
    hpg = ao // HEAD_DIM
    hp = hpg // 2
    cols, gate_col = _columns(cdim, ao)
    g = {}
    df = _mm(dx, bg["w_ff2"], tb=True, epi="drelu2", extra=s["f"], out_dtype=BF16, after=after, name="mm_dff2")
    g["w_ff2"] = _mm(s["f"], dx, ta=True, a_relu2=True, out_dtype=BF16, name="mm_gw_ff2")
    g["w_ff1"] = _mm(s["h2"], df, ta=True, out_dtype=BF16, out_slots=True, name="mm_gw_ff1")
    after = emit(GRAD_GROUPS[0], g)
    dx1, dg2 = _mm(df, bg["w_ff1"], tb=True, epi="rms_bwd", extra=(s["x1"], dx), gain=sm["norm2_g"], after=after,
                   name="mm_dff1")
    g["norm2_g"] = dg2[0]
    dmg = _mm(dx1, bg["w_out"], tb=True, name="mm_dout")
    g["w_out"] = _mm(s["mg"], dx1, ta=True, out_dtype=BF16, name="mm_gw_out")
    dyc, dya, dugc, duga = _gate_bwd(dmg, s["u"], s["yc"], s["ya"], gate_col, "gate_bwd")
    dzs = _mm(dyc, bg["w_conv_out"], tb=True, name="mm_dconv_out")
    g["w_conv_out"] = _mm(s["zs"], dyc, ta=True, out_dtype=BF16, name="mm_gw_conv_out")
    do = _mm(dya, bg["w_attn_out"], tb=True, name="mm_dattn_out")
    g["w_attn_out"] = _mm(s["o"], dya, ta=True, out_dtype=BF16, name="mm_gw_attn_out")
    dzc, dlg, dlb = _ln_swish_bwd(dzs, s["zc"], sm["conv_ln_g"], sm["conv_ln_b"], "ln_swish_bwd")
    g["conv_ln_g"] = dlg[0]
    g["conv_ln_b"] = dlb[0]
    da, dgt, dcw, dcb = _conv_bwd(dzc, s["u"], bg["conv_dw_w"], cdim, "conv_bwd")
    g["conv_dw_w"] = dcw[:CONV_WIDTH].astype(BF16)
    g["conv_dw_b"] = dcb[0]
    after = emit(GRAD_GROUPS[1], g)
    do_gs, dd_gs = _combine_bwd(do, s["os"], s["lses"], "combine_bwd")
    dqkv = [lax.empty((dx.shape[0], N_GROUPS * ao), BF16) for _ in range(3)]
    dgq = jnp.zeros((HEAD_DIM,), F32)
    dgk = jnp.zeros((HEAD_DIM,), F32)
    for gi in range(N_GROUPS):
        *dqkv, gq, gk, ds_sum = _attn_bwd(s["u"], do_gs[gi], dd_gs[gi], s["lses"][gi], s["qg2"], s["kg2"], bias,
                                          ds_sum, dqkv, gi, cols, hpg, "attn_bwd_g%d" % gi)
        dgq = dgq + jnp.sum(gq.reshape(hp * 2, HEAD_DIM), axis=0)
        dgk = dgk + jnp.sum(gk.reshape(hp * 2, HEAD_DIM), axis=0)
    g["q_norm_g"] = dgq
    g["k_norm_g"] = dgk
    du = jnp.concatenate([da, dgt] + dqkv + [dugc, duga], axis=1)
    g["w_in"] = _mm_gw_in_pairs(s["h1"], du, after, "mm_gw_in")
    after = emit(GRAD_GROUPS[2], g)
    dx0, dg1 = _mm_din_pairs(du, bg["w_in"], s["x"], sm["norm1_g"], dx1, after, "mm_din")
    g["norm1_g"] = dg1[0]
    return dx0, g, ds_sum


BIG = ("w_in", "conv_dw_w", "w_conv_out", "w_attn_out", "w_out", "w_ff1", "w_ff2")
COL_SHARDED = ("w_in", "conv_dw_w", "w_conv_out", "w_attn_out", "w_ff1")
SMALL = ("rel_bias", "norm1_g", "q_norm_g", "k_norm_g", "conv_dw_b", "conv_ln_g", "conv_ln_b", "norm2_g")
WEIGHTS = ("rel_bias", "norm1_g", "w_in", "q_norm_g", "k_norm_g", "conv_dw_w", "conv_dw_b", "conv_ln_g", "conv_ln_b",
           "w_conv_out", "w_attn_out", "w_out", "norm2_g", "w_ff1", "w_ff2")


def _to_whole(name, gathered):
    n, a, b = gathered.shape
    if name in COL_SHARDED:
        return gathered.transpose(1, 0, 2).reshape(a, n * b)
    return gathered.reshape(n * a, b)


def _to_slots(name, whole):
    a, b = whole.shape
    if name in COL_SHARDED:
        return whole.reshape(a, N_DEV, b // N_DEV).transpose(1, 0, 2)
    return whole.reshape(N_DEV, a // N_DEV, b)


def _own_slot(block, me):
    land = lax.empty((N_DEV,) + block.shape, block.dtype)
    return lax.dynamic_update_slice(land, block[None], (me,) + (0,) * block.ndim)


def kernel(x, rel_bias, norm1_g, w_in, q_norm_g, k_norm_g, conv_dw_w, conv_dw_b, conv_ln_g, conv_ln_b, w_conv_out, w_attn_out, w_out, norm2_g, w_ff1, w_ff2, loss_target, m_rel_bias, m_norm1_g, m_w_in, m_q_norm_g, m_k_norm_g, m_conv_dw_w, m_conv_dw_b, m_conv_ln_g, m_conv_ln_b, m_w_conv_out, m_w_attn_out, m_w_out, m_norm2_g, m_w_ff1, m_w_ff2, v_rel_bias, v_norm1_g, v_w_in, v_q_norm_g, v_k_norm_g, v_conv_dw_w, v_conv_dw_b, v_conv_ln_g, v_conv_ln_b, v_w_conv_out, v_w_attn_out, v_w_out, v_norm2_g, v_w_ff1, v_w_ff2):
    w = dict(rel_bias=rel_bias, norm1_g=norm1_g, w_in=w_in, q_norm_g=q_norm_g, k_norm_g=k_norm_g, conv_dw_w=conv_dw_w,
             conv_dw_b=conv_dw_b, conv_ln_g=conv_ln_g, conv_ln_b=conv_ln_b, w_conv_out=w_conv_out,
             w_attn_out=w_attn_out, w_out=w_out, norm2_g=norm2_g, w_ff1=w_ff1, w_ff2=w_ff2)
    mom = dict(rel_bias=m_rel_bias, norm1_g=m_norm1_g, w_in=m_w_in, q_norm_g=m_q_norm_g, k_norm_g=m_k_norm_g,
               conv_dw_w=m_conv_dw_w, conv_dw_b=m_conv_dw_b, conv_ln_g=m_conv_ln_g, conv_ln_b=m_conv_ln_b,
               w_conv_out=m_w_conv_out, w_attn_out=m_w_attn_out, w_out=m_w_out, norm2_g=m_norm2_g, w_ff1=m_w_ff1,
               w_ff2=m_w_ff2)
    var = dict(rel_bias=v_rel_bias, norm1_g=v_norm1_g, w_in=v_w_in, q_norm_g=v_q_norm_g, k_norm_g=v_k_norm_g,
               conv_dw_w=v_conv_dw_w, conv_dw_b=v_conv_dw_b, conv_ln_g=v_conv_ln_g, conv_ln_b=v_conv_ln_b,
               w_conv_out=v_w_conv_out, w_attn_out=v_w_attn_out, w_out=v_w_out, norm2_g=v_norm2_g, w_ff1=v_w_ff1,
               w_ff2=v_w_ff2)

    depth = norm1_g.shape[0]
    me = 4 * lax.axis_index("x") + 2 * lax.axis_index("y") + lax.axis_index("c")
    odd_core = lax.axis_index("c") == 1
    hpg = w_attn_out.shape[1] // HEAD_DIM
    buckets = jnp.asarray(_bucket_table())
    bias = _bias_expand(rel_bias, buckets, hpg, "bias_expand")

    first_names = ("w_in", "conv_dw_w")
    rest_names = tuple(k for k in BIG if k not in first_names)

    def chain_start(l, names, after):
        shards = [w[k][l] if k == "conv_dw_w" else w[k][l].astype(BF16) for k in names]
        if "w_in" in names:
            i = names.index("w_in")
            shards[i] = jnp.where(odd_core, jnp.pad(shards[i], ((0, 0), (SHIFT, 0))),
                                  jnp.pad(shards[i], ((0, 0), (0, SHIFT))))
        sems, bufs, token = _gather_start(shards, [_own_slot(s, me) for s in shards], after,
                                          "gather_start_%s_l%d" % (names[0], l))
        return dict(l=l, names=names, sems=sems, bufs=bufs, token=token)

    def chain_forward(ch, after):
        fwd, bufs, token = _gather_forward(len(ch["names"]), ch["bufs"], ch["sems"][3], after,
                                           "gather_forward_%s_l%d" % (ch["names"][0], ch["l"]))
        ch.update(fwd=fwd, bufs=bufs)
        return token

    def chain_finish(ch, after):
        d2d_s, d2d_r, ici_s, _ = ch["sems"]
        gathered, _ = _gather_finish(len(ch["names"]), ch["bufs"], d2d_s, d2d_r, ici_s, ch["fwd"][0], ch["fwd"][1],
                                     after, "gather_finish_%s_l%d" % (ch["names"][0], ch["l"]))
        out = {k: a if k == "w_in" else _to_whole(k, a) for k, a in zip(ch["names"], gathered)}
        if "conv_dw_w" in out:
            out["conv_dw_w"] = jnp.pad(out["conv_dw_w"], ((0, CONV_TAPS_PADDED - CONV_WIDTH), (0, 0)))
        return out

    xs = x[0]
    h1 = _rms_fwd(xs, norm1_g[0], "rms1_fwd")
    saved, bigs, smalls = [], [], []
    chains = {}
    for l in range(depth):
        sm = {k: w[k][l] for k in SMALL if k != "rel_bias"}
        if l == 0:
            first = chain_start(0, first_names, None)
            token = chain_forward(first, None)
            rest = chain_start(0, rest_names, token)
            bg = chain_finish(first, rest["token"])

            def get_rest(o, rest=rest):
                token = chain_forward(rest, o)
                if depth > 1:
                    chains[1] = (chain_start(1, first_names, token),)
                    chains[1] += (chain_start(1, rest_names, chains[1][0]["token"]),)
                    token = chains[1][1]["token"]
                return chain_finish(rest, token)
        elif l == 1:
            first, rest = chains[1]
            token = chain_forward(first, xs)
            if depth > 2:
                chains[2] = chain_start(2, BIG, token)
                token = chains[2]["token"]
            bg = chain_finish(first, token)

            def get_rest(o, rest=rest):
                return chain_finish(rest, chain_forward(rest, o))
        else:
            token = chain_forward(chains[l], xs)
            if l + 1 < depth:
                chains[l + 1] = chain_start(l + 1, BIG, token)
                token = chains[l + 1]["token"]
            whole = chain_finish(chains[l], token)
            bg = {k: whole[k] for k in first_names}

            def get_rest(o, whole=whole):
                return {k: whole[k] for k in rest_names}
        xs, h1, sv, bg = _layer_fwd(xs, h1, sm, bg, get_rest, bias, hpg, norm1_g[l + 1] if l + 1 < depth else None)
        saved.append(sv)
        bigs.append(bg)
        smalls.append(sm)

    loss, dx = _loss_and_grad(xs, loss_target[0], "loss")

    ds_sum = jnp.zeros((N_GROUPS * hpg, BLOCK, 2 * BLOCK), F32)
    g = {k: [None] * depth for k in SMALL if k != "rel_bias"}
    sums = {k: lax.empty((depth, int(np.prod(w[k].shape[1:-1])), w[k].shape[-1] + (SHIFT if k == "w_in" else 0)), F32)
            for k in BIG}
    pending = []

    def finish_oldest(after):
        names, l, (send, recv), bufs = pending.pop(0)
        parts, recvd, token = _exchange_finish(len(names), bufs, send, recv, after,
                                               "exchange_finish_%s_l%d" % (names[0], l))
        for k, r, p in zip(names, recvd, parts):
            three = (N_DEV, -1, r.shape[-1])
            sums[k] = _sum_slots(r.reshape(three), p.reshape(three), me, sums[k], l, "sum_" + k)
        return token

    def make_emit(l):
        def emit(names, gl):
            parts = [gl[k] if k in ("w_ff1", "w_in") else _to_slots(k, gl[k]) for k in names]
            token = finish_oldest(parts[0]) if len(pending) >= len(GRAD_GROUPS) else None
            lands = [lax.empty(p.shape, p.dtype) for p in parts]
            sems, bufs, token = _exchange_start(parts, lands, token, "exchange_start_%s_l%d" % (names[0], l))
            pending.append((names, l, sems, bufs))
            return token
        return emit

    token = None
    for l in reversed(range(depth)):
        dx, gl, ds_sum = _layer_bwd(dx, saved[l], smalls[l], bigs[l], bias, ds_sum, token, make_emit(l))
        for k in g:
            g[k][l] = gl[k]
        token = None
    grad_x = dx

    g = {k: jnp.stack(v) for k, v in g.items()}
    db = _bias_reduce(ds_sum, buckets, hpg, "bias_reduce")
    g["rel_bias"] = db[:, 0, :NUM_BUCKETS].T

    flat = jnp.concatenate([g[k].reshape(-1) for k in SMALL])
    nflat = flat.shape[0]
    rows = -(-nflat // (8 * LANES)) * 8
    packed = jnp.pad(flat, (0, rows * LANES - nflat)).reshape(rows, LANES)
    total = _all_reduce_small(packed, "reduce_small").reshape(-1)
    grad = {}
    off = 0
    for k in SMALL:
        size = int(np.prod(w[k].shape))
        grad[k] = total[off:off + size].reshape(w[k].shape)
        off += size

    loss = lax.psum(loss[0, 0], ("x", "y", "c"))
    outs = {k: _adamw(w[k], grad[k], mom[k], var[k], "adamw_" + k) for k in SMALL}
    token = outs[SMALL[-1]][0]
    while pending:
        names = pending[0][0]
        finish_oldest(token)
        for k in names:
            total = sums[k]
            if k == "w_in":
                total = jnp.where(odd_core, total[..., SHIFT:], total[..., :w_in.shape[-1]])
            grad[k] = total.reshape(w[k].shape)
            outs[k] = _adamw(w[k], grad[k], mom[k], var[k], "adamw_" + k)
            token = outs[k][0]
    return (loss, grad_x[None], *[grad[k] for k in WEIGHTS], *[outs[k][0] for k in WEIGHTS],
            *[outs[k][1] for k in WEIGHTS], *[outs[k][2] for k in WEIGHTS])
```

```python
import functools
import math

import numpy as np
import jax
import jax.numpy as jnp
from jax import lax
from jax.experimental import pallas as pl
from jax.experimental.pallas import tpu as pltpu

F32 = jnp.float32
BF16 = jnp.bfloat16

HEAD_DIM = 64
N_GROUPS = 3
DILATIONS = (1, 4, 16)
SUB_WINDOW = 128
BLOCK = 128
CONV_WIDTH = 31
CONV_TAPS_PADDED = 32
NUM_BUCKETS = 32
MAX_REL_DISTANCE = 2048
EPS = 1e-6
NEG_INF = -1e30
LANES = 128
SUBLANES = 8

ADAM_LR = 0.001
ADAM_B1 = 0.9
ADAM_B2 = 0.999
ADAM_EPS = 1e-08
ADAM_WD = 0.01
ADAM_STEP = 10

N_DEV = 8
VMEM_LIMIT = 56 * 1024 * 1024
MESH = pl.DeviceIdType.MESH


def _cparams(sem=None):
    return pltpu.CompilerParams(dimension_semantics=sem, vmem_limit_bytes=VMEM_LIMIT)


def _tile(n, target):
    if n <= target:
        return n
    t = (target // LANES) * LANES
    while t >= LANES:
        if n % t == 0:
            return t
        t -= LANES
    return n


def _sigmoid(v):
    return 1.0 / (1.0 + jnp.exp(-v))


MM_VMEM_BUDGET = 40 * 1024 * 1024


def _rms_apply(x, g):
    return x * lax.rsqrt(jnp.mean(x * x, axis=-1, keepdims=True) + EPS) * g


def _rms_grad(dh, x, g):
    r = lax.rsqrt(jnp.mean(x * x, axis=-1, keepdims=True) + EPS)
    xh = x * r
    dxh = dh * g
    dx = r * (dxh - xh * jnp.mean(dxh * xh, axis=-1, keepdims=True))
    return dx, jnp.sum(dh * xh, axis=0, keepdims=True)


def _mm_tiles(m, n, kdim, a_bytes, b_bytes, io_bytes, whole_rows=False, temps=2):
    def need(tm, tn, tk):
        blocks = 2 * (tm * tk * a_bytes + tk * tn * b_bytes + tm * tn * io_bytes)
        casts = (tm * tk * 2 if a_bytes == 4 else 0) + (tk * tn * 2 if b_bytes == 4 else 0)
        return blocks + casts + temps * tm * tn * 4

    tn = n if whole_rows else _tile(n, 1024)
    while True:
        fits = [(tm * tk, tm, tk) for tm in {_tile(m, c) for c in (1024, 512, 256, 128)}
                for tk in {_tile(kdim, c) for c in (2048, 1024, 512, 256)} if need(tm, tn, tk) <= MM_VMEM_BUDGET]
        if fits:
            _, tm, tk = max(fits)
            return tm, tn, tk
        assert not whole_rows and tn % 256 == 0, "no block size fits the VMEM budget"
        tn //= 2


def _mm(a, b, *, ta=False, tb=False, out_dtype=F32, epi=None, extra=(), gain=None, after=None, out_slots=False,
        a_relu2=False, name):
    extra = tuple(extra) if isinstance(extra, (tuple, list)) else (extra,)
    m = a.shape[1] if ta else a.shape[0]
    kdim = a.shape[0] if ta else a.shape[1]
    n = b.shape[0] if tb else b.shape[1]
    norm = epi in ("res_rms", "rms_bwd")
    io_bytes = (jnp.dtype(out_dtype).itemsize + sum(e.dtype.itemsize for e in extra) + (2 if epi == "res_rms" else 0))
    tm, tn, tk = _mm_tiles(m, n // N_DEV if out_slots else n, kdim, a.dtype.itemsize, b.dtype.itemsize, io_bytes,
                           whole_rows=norm, temps=6 if norm else 2)
    if out_slots:
        assert epi is None and tn == n // N_DEV
    nk = kdim // tk
    a_spec = pl.BlockSpec((tk, tm), lambda i, j, k: (k, i)) if ta else pl.BlockSpec((tm, tk), lambda i, j, k: (i, k))
    b_spec = pl.BlockSpec((tn, tk), lambda i, j, k: (j, k)) if tb else pl.BlockSpec((tk, tn), lambda i, j, k: (k, j))
    o_spec = (pl.BlockSpec((None, tm, tn), lambda i, j, k: (j, i, 0)) if out_slots
              else pl.BlockSpec((tm, tn), lambda i, j, k: (i, j)))
    v_spec = pl.BlockSpec((1, tn), lambda i, j, k: (0, j))
    dims = (((0 if ta else 1,), (1 if tb else 0,)), ((), ()))
    n_extra = len(extra)
    n_in = 2 + n_extra + (gain is not None) + (after is not None)
    n_out = 2 if norm else 1

    def body(*refs):
        a_ref, b_ref = refs[0], refs[1]
        e_refs = refs[2:2 + n_extra]
        g_ref = refs[2 + n_extra] if gain is not None else None
        outs = refs[n_in:n_in + n_out]

        def product():
            av = a_ref[...]
            if a_relu2:
                r = jnp.maximum(av.astype(F32), 0.0)
                av = r * r
            return lax.dot_general(av.astype(BF16), b_ref[...].astype(BF16), dims, preferred_element_type=F32)

        def finish(acc):
            if epi is None:
                outs[0][...] = acc.astype(outs[0].dtype)
            elif epi == "res":
                outs[0][...] = (e_refs[0][...] + acc).astype(outs[0].dtype)
            elif epi == "drelu2":
                outs[0][...] = (acc * (2.0 * jnp.maximum(e_refs[0][...].astype(F32), 0.0))).astype(outs[0].dtype)
            elif epi == "res_rms":
                x1 = e_refs[0][...] + acc
                outs[0][...] = x1
                outs[1][...] = _rms_apply(x1, g_ref[...]).astype(BF16)
            elif epi == "rms_bwd":
                dx, dg = _rms_grad(acc, e_refs[0][...], g_ref[...])
                outs[0][...] = e_refs[1][...] + dx
                i = pl.program_id(0)

                @pl.when(i == 0)
                def _():
                    outs[1][...] = dg

                @pl.when(i > 0)
                def _():
                    outs[1][...] += dg

        if nk == 1:
            finish(product())
            return
        acc_ref = refs[-1]
        k = pl.program_id(2)

        @pl.when(k == 0)
        def _():
            acc_ref[...] = product()

        @pl.when(jnp.logical_and(k > 0, k < nk - 1))
        def _():
            acc_ref[...] += product()

        @pl.when(k == nk - 1)
        def _():
            finish(acc_ref[...] + product())

    in_specs = ([a_spec, b_spec] + [o_spec] * n_extra + ([v_spec] if gain is not None else [])
                + ([pl.BlockSpec(memory_space=pl.ANY)] if after is not None else []))
    if epi == "res_rms":
        out_shape = (jax.ShapeDtypeStruct((m, n), F32), jax.ShapeDtypeStruct((m, n), BF16))
        out_specs = (o_spec, o_spec)
    elif epi == "rms_bwd":
        out_shape = (jax.ShapeDtypeStruct((m, n), F32), jax.ShapeDtypeStruct((1, n), F32))
        out_specs = (o_spec, v_spec)
    else:
        out_shape = jax.ShapeDtypeStruct((N_DEV, m, tn) if out_slots else (m, n), out_dtype)
        out_specs = o_spec
    args = (a, b) + extra + ((gain.reshape(1, n),) if gain is not None else ()) + ((after,) if after is not None else ())
    return pl.pallas_call(
        body, name=name, grid=(m // tm, n // tn, nk), in_specs=in_specs, out_specs=out_specs, out_shape=out_shape,
        scratch_shapes=[pltpu.VMEM((tm, tn), F32)] if nk > 1 else [],
        compiler_params=_cparams(("arbitrary", "arbitrary", "arbitrary")),
    )(*args)


SHIFT = HEAD_DIM


def _pair_blocks(e, o):
    wp = e.shape[-1]
    return e[:, :wp - LANES], e[:, wp - LANES:] + o[:, :LANES], o[:, LANES:]


def _mm_in_pairs(a, wg, name):
    t, kdim = a.shape
    wp = wg.shape[-1]
    ws = wp - SHIFT
    tm = _tile(t, 1024)

    def body(a_ref, e_ref, o_ref, u_ref):
        av = a_ref[...]
        lo, mid, hi = _pair_blocks(e_ref[...], o_ref[...])
        u_ref[:, :wp - LANES] = jnp.dot(av, lo, preferred_element_type=F32).astype(BF16)
        u_ref[:, wp - LANES:wp] = jnp.dot(av, mid, preferred_element_type=F32).astype(BF16)
        u_ref[:, wp:] = jnp.dot(av, hi, preferred_element_type=F32).astype(BF16)

    return pl.pallas_call(
        body, name=name, grid=(N_DEV // 2, t // tm),
        in_specs=[pl.BlockSpec((tm, kdim), lambda p, i: (i, 0)),
                  pl.BlockSpec((None, kdim, wp), lambda p, i: (2 * p, 0, 0)),
                  pl.BlockSpec((None, kdim, wp), lambda p, i: (2 * p + 1, 0, 0))],
        out_specs=pl.BlockSpec((tm, 2 * ws), lambda p, i: (i, p)),
        out_shape=jax.ShapeDtypeStruct((t, N_DEV * ws), BF16), compiler_params=_cparams(("arbitrary", "arbitrary")),
    )(a, wg, wg)


def _mm_din_pairs(du, wg, x, gain, dres, after, name):
    t = du.shape[0]
    _, kdim, wp = wg.shape
    ws = wp - SHIFT
    tm = _tile(t, 512)
    npair = N_DEV // 2
    lanes = (((1,), (1,)), ((), ()))
    extra = [] if after is None else [after]

    def body(d_ref, e_ref, o_ref, x_ref, g_ref, r_ref, *rest):
        dx_ref, dg_ref, acc_ref = rest[-3], rest[-2], rest[-1]
        i, p = pl.program_id(0), pl.program_id(1)
        lo, mid, hi = _pair_blocks(e_ref[...], o_ref[...])
        part = (lax.dot_general(d_ref[:, :wp - LANES], lo, lanes, preferred_element_type=F32)
                + lax.dot_general(d_ref[:, wp - LANES:wp], mid, lanes, preferred_element_type=F32)
                + lax.dot_general(d_ref[:, wp:], hi, lanes, preferred_element_type=F32))

        @pl.when(p == 0)
        def _():
            acc_ref[...] = part

        @pl.when(jnp.logical_and(p > 0, p < npair - 1))
        def _():
            acc_ref[...] += part

        @pl.when(p == npair - 1)
        def _():
            dx, dg = _rms_grad(acc_ref[...] + part, x_ref[...], g_ref[...])
            dx_ref[...] = r_ref[...] + dx

            @pl.when(i == 0)
            def _():
                dg_ref[...] = dg

            @pl.when(i > 0)
            def _():
                dg_ref[...] += dg

    row = pl.BlockSpec((tm, kdim), lambda i, p: (i, 0))
    vec = pl.BlockSpec((1, kdim), lambda i, p: (0, 0))
    return pl.pallas_call(
        body, name=name, grid=(t // tm, npair),
        in_specs=[pl.BlockSpec((tm, 2 * ws), lambda i, p: (i, p)),
                  pl.BlockSpec((None, kdim, wp), lambda i, p: (2 * p, 0, 0)),
                  pl.BlockSpec((None, kdim, wp), lambda i, p: (2 * p + 1, 0, 0)), row, vec, row]
        + [pl.BlockSpec(memory_space=pl.ANY)] * len(extra),
        out_specs=(row, vec),
        out_shape=(jax.ShapeDtypeStruct((t, kdim), F32), jax.ShapeDtypeStruct((1, kdim), F32)),
        scratch_shapes=[pltpu.VMEM((tm, kdim), F32)], compiler_params=_cparams(("arbitrary", "arbitrary")),
    )(du, wg, wg, x, gain.reshape(1, kdim), dres, *extra)


def _mm_gw_in_pairs(h, du, after, name):
    t, kdim = h.shape
    ws = du.shape[1] // N_DEV
    wp = ws + SHIFT
    tm = _tile(kdim, 512)
    rows = (((0,), (0,)), ((), ()))
    extra = [] if after is None else [after]

    def body(h_ref, d_ref, *rest):
        g_ref = rest[-1]
        g = lax.dot_general(h_ref[...], d_ref[...], rows, preferred_element_type=F32)
        g_ref[0] = g[:, :wp].astype(BF16)
        g_ref[1] = g[:, wp - LANES:].astype(BF16)

    return pl.pallas_call(
        body, name=name, grid=(N_DEV // 2, kdim // tm),
        in_specs=[pl.BlockSpec((t, tm), lambda p, i: (0, i)), pl.BlockSpec((t, 2 * ws), lambda p, i: (0, p))]
        + [pl.BlockSpec(memory_space=pl.ANY)] * len(extra),
        out_specs=pl.BlockSpec((2, tm, wp), lambda p, i: (p, i, 0)),
        out_shape=jax.ShapeDtypeStruct((N_DEV, kdim, wp), BF16), compiler_params=_cparams(("arbitrary", "arbitrary")),
    )(h, du, *extra)


ROW_BLOCK_BUDGET = 24 * 1024 * 1024


def _rows(t, row_bytes):
    rows = t
    while rows > 8 and (2 * rows * row_bytes > ROW_BLOCK_BUDGET or t % rows):
        rows //= 2
    return rows


def _rms_fwd(x, g, name):
    t, d = x.shape
    ROWS = _rows(t, 6 * d)

    def body(x_ref, g_ref, h_ref):
        h_ref[...] = _rms_apply(x_ref[...], g_ref[...]).astype(BF16)

    return pl.pallas_call(
        body, name=name, grid=(t // ROWS,),
        in_specs=[pl.BlockSpec((ROWS, d), lambda i: (i, 0)), pl.BlockSpec((1, d), lambda i: (0, 0))],
        out_specs=pl.BlockSpec((ROWS, d), lambda i: (i, 0)),
        out_shape=jax.ShapeDtypeStruct((t, d), BF16), compiler_params=_cparams(("arbitrary",)),
    )(x, g.reshape(1, d))


def _gate_fwd(u, yc, ya, gate_col, name):
    t, d = yc.shape
    td = math.gcd(_tile(d, 512), gate_col)
    nd = d // td
    c0 = gate_col // td
    ROWS = _rows(t, 14 * td)

    def body(gc_ref, ga_ref, yc_ref, ya_ref, m_ref):
        gc = _sigmoid(gc_ref[...].astype(F32))
        ga = _sigmoid(ga_ref[...].astype(F32))
        m_ref[...] = (gc * yc_ref[...] + ga * ya_ref[...]).astype(BF16)

    blk = pl.BlockSpec((ROWS, td), lambda i, j: (i, j))
    return pl.pallas_call(
        body, name=name, grid=(t // ROWS, nd),
        in_specs=[pl.BlockSpec((ROWS, td), lambda i, j: (i, c0 + j)),
                  pl.BlockSpec((ROWS, td), lambda i, j: (i, c0 + nd + j)), blk, blk],
        out_specs=blk, out_shape=jax.ShapeDtypeStruct((t, d), BF16),
        compiler_params=_cparams(("arbitrary", "arbitrary")),
    )(u, u, yc, ya)


def _gate_bwd(dm, u, yc, ya, gate_col, name):
    t, d = yc.shape
    td = math.gcd(_tile(d, 512), gate_col)
    nd = d // td
    c0 = gate_col // td
    ROWS = _rows(t, 24 * td)

    def body(dm_ref, gc_ref, ga_ref, yc_ref, ya_ref, dyc_ref, dya_ref, dugc_ref, duga_ref):
        dmv = dm_ref[...]
        gc = _sigmoid(gc_ref[...].astype(F32))
        ga = _sigmoid(ga_ref[...].astype(F32))
        dyc_ref[...] = (dmv * gc).astype(BF16)
        dya_ref[...] = (dmv * ga).astype(BF16)
        dugc_ref[...] = (dmv * yc_ref[...] * gc * (1.0 - gc)).astype(BF16)
        duga_ref[...] = (dmv * ya_ref[...] * ga * (1.0 - ga)).astype(BF16)

    blk = pl.BlockSpec((ROWS, td), lambda i, j: (i, j))
    o = jax.ShapeDtypeStruct((t, d), BF16)
    return pl.pallas_call(
        body, name=name, grid=(t // ROWS, nd),
        in_specs=[blk, pl.BlockSpec((ROWS, td), lambda i, j: (i, c0 + j)),
                  pl.BlockSpec((ROWS, td), lambda i, j: (i, c0 + nd + j)), blk, blk],
        out_specs=(blk, blk, blk, blk), out_shape=(o, o, o, o),
        compiler_params=_cparams(("arbitrary", "arbitrary")),
    )(dm, u, u, yc, ya)


def _loss_and_grad(y, target, name):
    t, d = y.shape
    ROWS = _rows(t, 12 * d)
    n = t // ROWS

    def body(y_ref, t_ref, loss_ref, dy_ref, acc_ref):
        i = pl.program_id(0)

        @pl.when(i == 0)
        def _():
            acc_ref[...] = jnp.zeros_like(acc_ref)

        diff = y_ref[...] - t_ref[...]
        dy_ref[...] = diff * (1.0 / d)
        acc_ref[...] += jnp.sum(diff * diff, axis=0, keepdims=True)

        @pl.when(i == n - 1)
        def _():
            loss_ref[...] = jnp.sum(acc_ref[...], axis=-1, keepdims=True) * (0.5 / d)

    row = pl.BlockSpec((ROWS, d), lambda i: (i, 0))
    return pl.pallas_call(
        body, name=name, grid=(n,), in_specs=[row, row],
        out_specs=(pl.BlockSpec((1, 1), lambda i: (0, 0)), row),
        out_shape=(jax.ShapeDtypeStruct((1, 1), F32), jax.ShapeDtypeStruct((t, d), F32)),
        scratch_shapes=[pltpu.VMEM((1, d), F32)], compiler_params=_cparams(("arbitrary",)),
    )(y, target)


HALO = 32


def _conv_fwd(u, w, b, cdim, name):
    t = u.shape[0]
    ncb = cdim // LANES
    nt = t // BLOCK

    def body(a_ref, g_ref, w_ref, b_ref, zc_ref, zpad):
        zpad[0:HALO, :] = jnp.zeros((HALO, LANES), F32)
        zpad[HALO:HALO + t, :] = a_ref[...].astype(F32) * _sigmoid(g_ref[...].astype(F32))
        wv = w_ref[...]
        bv = b_ref[...]

        def tile(i, carry):
            r0 = pl.multiple_of(i * BLOCK, BLOCK)
            win = zpad[pl.ds(r0, BLOCK + HALO), :]
            acc = jnp.zeros((BLOCK, LANES), F32) + bv
            for b in range(SUBLANES):
                sh = win if b == 0 else pltpu.roll(win, b, 0)
                for a in range(HALO // SUBLANES):
                    j = CONV_WIDTH - 1 - (SUBLANES * a + b)
                    if j >= 0:
                        lo = HALO - SUBLANES * a
                        acc = acc + wv[j:j + 1, :] * sh[lo:lo + BLOCK, :]
            zc_ref[pl.ds(r0, BLOCK), :] = acc
            return carry

        lax.fori_loop(0, nt, tile, 0)

    col = lambda off: pl.BlockSpec((t, LANES), lambda c: (0, off + c))
    return pl.pallas_call(
        body, name=name, grid=(ncb,),
        in_specs=[col(0), col(ncb), pl.BlockSpec((CONV_TAPS_PADDED, LANES), lambda c: (0, c)),
                  pl.BlockSpec((1, LANES), lambda c: (0, c))],
        out_specs=pl.BlockSpec((t, LANES), lambda c: (0, c)),
        out_shape=jax.ShapeDtypeStruct((t, cdim), F32),
        scratch_shapes=[pltpu.VMEM((t + HALO, LANES), F32)], compiler_params=_cparams(("arbitrary",)),
    )(u, u, w, b.reshape(1, cdim))


def _conv_bwd(dzc, u, w, cdim, name):
    t = u.shape[0]
    ncb = cdim // LANES
    nt = t // BLOCK
    win_rows = BLOCK + HALO

    def body(dzc_ref, a_ref, g_ref, w_ref, da_ref, dg_ref, dw_ref, db_ref, zpad, dpad):
        av = a_ref[...].astype(F32)
        sg = _sigmoid(g_ref[...].astype(F32))
        zpad[0:HALO, :] = jnp.zeros((HALO, LANES), F32)
        zpad[HALO:HALO + t, :] = av * sg
        dpad[0:t, :] = dzc_ref[...]
        dpad[t:t + HALO, :] = jnp.zeros((HALO, LANES), F32)
        dw_ref[...] = jnp.zeros_like(dw_ref)
        db_ref[...] = jnp.sum(dzc_ref[...], axis=0, keepdims=True)
        wv = w_ref[...]

        def tile(i, carry):
            r0 = pl.multiple_of(i * BLOCK, BLOCK)
            zwin = zpad[pl.ds(r0, win_rows), :]
            dwin = dpad[pl.ds(r0, win_rows), :]
            dcur = dwin[0:BLOCK, :]
            dz = jnp.zeros((BLOCK, LANES), F32)
            for b in range(SUBLANES):
                zs = zwin if b == 0 else pltpu.roll(zwin, b, 0)
                ds = dwin if b == 0 else pltpu.roll(dwin, win_rows - b, 0)
                for a in range(HALO // SUBLANES):
                    j = CONV_WIDTH - 1 - (SUBLANES * a + b)
                    if j >= 0:
                        lo = HALO - SUBLANES * a
                        dw_ref[j:j + 1, :] += jnp.sum(dcur * zs[lo:lo + BLOCK, :], axis=0, keepdims=True)
                        dz = dz + wv[j:j + 1, :] * ds[SUBLANES * a:SUBLANES * a + BLOCK, :]
            ac = a_ref[pl.ds(r0, BLOCK), :].astype(F32)
            sc = _sigmoid(g_ref[pl.ds(r0, BLOCK), :].astype(F32))
            da_ref[pl.ds(r0, BLOCK), :] = (dz * sc).astype(BF16)
            dg_ref[pl.ds(r0, BLOCK), :] = (dz * ac * sc * (1.0 - sc)).astype(BF16)
            return carry

        lax.fori_loop(0, nt, tile, 0)

    col = lambda off: pl.BlockSpec((t, LANES), lambda c: (0, off + c))
    wspec = pl.BlockSpec((CONV_TAPS_PADDED, LANES), lambda c: (0, c))
    o = jax.ShapeDtypeStruct((t, cdim), BF16)
    return pl.pallas_call(
        body, name=name, grid=(ncb,), in_specs=[col(0), col(0), col(ncb), wspec],
        out_specs=(col(0), col(0), wspec, pl.BlockSpec((1, LANES), lambda c: (0, c))),
        out_shape=(o, o, jax.ShapeDtypeStruct((CONV_TAPS_PADDED, cdim), F32), jax.ShapeDtypeStruct((1, cdim), F32)),
        scratch_shapes=[pltpu.VMEM((t + HALO, LANES), F32), pltpu.VMEM((t + HALO, LANES), F32)],
        compiler_params=_cparams(("arbitrary",)),
    )(dzc, u, u, w)


def _ln_swish_fwd(zc, g, b, name):
    t, c = zc.shape
    ROWS = _rows(t, 6 * c)

    def body(z_ref, g_ref, b_ref, o_ref):
        z = z_ref[...]
        mu = jnp.mean(z, axis=-1, keepdims=True)
        zc_ = z - mu
        zn = zc_ * lax.rsqrt(jnp.mean(zc_ * zc_, axis=-1, keepdims=True) + EPS)
        y = zn * g_ref[...] + b_ref[...]
        o_ref[...] = (y * _sigmoid(y)).astype(BF16)

    row = pl.BlockSpec((ROWS, c), lambda i: (i, 0))
    vec = pl.BlockSpec((1, c), lambda i: (0, 0))
    return pl.pallas_call(
        body, name=name, grid=(t // ROWS,), in_specs=[row, vec, vec], out_specs=row,
        out_shape=jax.ShapeDtypeStruct((t, c), BF16), compiler_params=_cparams(("arbitrary",)),
    )(zc, g.reshape(1, c), b.reshape(1, c))


def _ln_swish_bwd(dzs, zc, g, b, name):
    t, c = zc.shape
    ROWS = _rows(t, 12 * c)

    def body(d_ref, z_ref, g_ref, b_ref, dz_ref, dg_ref, db_ref):
        @pl.when(pl.program_id(0) == 0)
        def _():
            dg_ref[...] = jnp.zeros_like(dg_ref)
            db_ref[...] = jnp.zeros_like(db_ref)

        z = z_ref[...]
        mu = jnp.mean(z, axis=-1, keepdims=True)
        zc_ = z - mu
        rstd = lax.rsqrt(jnp.mean(zc_ * zc_, axis=-1, keepdims=True) + EPS)
        zn = zc_ * rstd
        y = zn * g_ref[...] + b_ref[...]
        sg = _sigmoid(y)
        dy = d_ref[...] * (sg * (1.0 + y * (1.0 - sg)))
        dg_ref[...] += jnp.sum(dy * zn, axis=0, keepdims=True)
        db_ref[...] += jnp.sum(dy, axis=0, keepdims=True)
        dzn = dy * g_ref[...]
        dz_ref[...] = rstd * (dzn - jnp.mean(dzn, axis=-1, keepdims=True)
                              - zn * jnp.mean(dzn * zn, axis=-1, keepdims=True))

    row = pl.BlockSpec((ROWS, c), lambda i: (i, 0))
    vec = pl.BlockSpec((1, c), lambda i: (0, 0))
    v = jax.ShapeDtypeStruct((1, c), F32)
    return pl.pallas_call(
        body, name=name, grid=(t // ROWS,), in_specs=[row, row, vec, vec], out_specs=(row, vec, vec),
        out_shape=(jax.ShapeDtypeStruct((t, c), F32), v, v), compiler_params=_cparams(("arbitrary",)),
    )(dzs, zc, g.reshape(1, c), b.reshape(1, c))


def _bucket_table():
    qi = np.arange(BLOCK)[:, None]
    kj = np.arange(2 * BLOCK)[None, :]
    off = qi + BLOCK - kj
    band = (off >= 0) & (off <= SUB_WINDOW)
    max_exact = NUM_BUCKETS // 2
    out = []
    for d in DILATIONS:
        dist = (np.clip(off, 0, SUB_WINDOW) * d).astype(np.int32)
        nf = np.maximum(dist, 1).astype(np.float32)
        large = max_exact + (np.log(nf / np.float32(max_exact)) / np.float32(math.log(MAX_REL_DISTANCE / max_exact))
                             * np.float32(NUM_BUCKETS - max_exact)).astype(np.int32)
        large = np.minimum(large, NUM_BUCKETS - 1)
        bucket = np.where(dist < max_exact, dist, large)
        out.append(np.where(band, bucket, -1))
    return np.stack(out).astype(np.int32)


def _bias_expand(rel_bias, buckets, hpg, name):
    nh = N_GROUPS * hpg

    def body(rb_ref, bk_ref, o_ref):
        h = pl.program_id(0)
        bk = bk_ref[0]
        acc = jnp.full((BLOCK, 2 * BLOCK), NEG_INF, F32)
        for bb in range(NUM_BUCKETS):
            acc = jnp.where(bk == bb, rb_ref[bb, h], acc)
        o_ref[0] = acc

    return pl.pallas_call(
        body, name=name, grid=(nh,),
        in_specs=[pl.BlockSpec(memory_space=pltpu.SMEM),
                  pl.BlockSpec((1, BLOCK, 2 * BLOCK), lambda h: (h // hpg, 0, 0))],
        out_specs=pl.BlockSpec((1, BLOCK, 2 * BLOCK), lambda h: (h, 0, 0)),
        out_shape=jax.ShapeDtypeStruct((nh, BLOCK, 2 * BLOCK), F32), compiler_params=_cparams(("arbitrary",)),
    )(rel_bias, buckets)


def _bias_reduce(ds_sum, buckets, hpg, name):
    nh = N_GROUPS * hpg

    def body(ds_ref, bk_ref, o_ref):
        bk = bk_ref[0]
        dsv = ds_ref[0]
        lane = lax.broadcasted_iota(jnp.int32, (1, LANES), 1)
        row = jnp.zeros((1, LANES), F32)
        for bb in range(NUM_BUCKETS):
            tot = jnp.sum(jnp.sum(jnp.where(bk == bb, dsv, 0.0), axis=-1, keepdims=True), axis=0, keepdims=True)
            row = jnp.where(lane == bb, tot, row)
        o_ref[0] = row

    return pl.pallas_call(
        body, name=name, grid=(nh,),
        in_specs=[pl.BlockSpec((1, BLOCK, 2 * BLOCK), lambda h: (h, 0, 0)),
                  pl.BlockSpec((1, BLOCK, 2 * BLOCK), lambda h: (h // hpg, 0, 0))],
        out_specs=pl.BlockSpec((1, 1, LANES), lambda h: (h, 0, 0)),
        out_shape=jax.ShapeDtypeStruct((nh, 1, LANES), F32), compiler_params=_cparams(("arbitrary",)),
    )(ds_sum, buckets)


def _chunk_rows(c, d, nb):
    r, n = c // nb, c % nb
    if d == 1:
        return pl.ds(c * BLOCK, BLOCK)
    return pl.ds(r + n * BLOCK * d, BLOCK, stride=d)


def _segment_ones():
    i = lax.broadcasted_iota(jnp.int32, (LANES, LANES), 0) // HEAD_DIM
    j = lax.broadcasted_iota(jnp.int32, (LANES, LANES), 1) // HEAD_DIM
    return (i == j).astype(BF16)


def _segment_sum(v, seg):
    hi = v.astype(BF16)
    lo = (v - hi.astype(F32)).astype(BF16)
    return jnp.dot(hi, seg, preferred_element_type=F32) + jnp.dot(lo, seg, preferred_element_type=F32)


def _head_mean(v, seg):
    return _segment_sum(v, seg) * (1.0 / HEAD_DIM)


def _attn_fwd(u, qg2, kg2, bias, gi, cols, hpg, name):
    t = u.shape[0]
    d = DILATIONS[gi]
    nchunk = t // BLOCK
    nb = (t // d) // BLOCK
    hp = hpg // 2
    qc0, kc0, vc0 = [(c + gi * hpg * HEAD_DIM) // LANES for c in cols]
    contract_lanes = (((1,), (1,)), ((), ()))

    def body(q_ref, k_ref, v_ref, qg_ref, kg_ref, bias_ref, o_ref, lse_ref, qd, kd, vd, od, ld, sbuf):
        seg = _segment_ones()
        lane = lax.broadcasted_iota(jnp.int32, (1, LANES), 1)
        qg = qg_ref[...] * (HEAD_DIM ** -0.5)
        kg = kg_ref[...]
        kd[0:BLOCK, :] = jnp.zeros((BLOCK, LANES), BF16)
        vd[0:BLOCK, :] = jnp.zeros((BLOCK, LANES), BF16)
        od[...] = q_ref[...].astype(F32)
        ld[...] = k_ref[...].astype(F32)
        for c in range(nchunk):
            rows = _chunk_rows(c, d, nb)
            qv = od[rows, :]
            kv = ld[rows, :]
            qd[c * BLOCK:(c + 1) * BLOCK, :] = (qv * lax.rsqrt(_head_mean(qv * qv, seg) + EPS) * qg).astype(BF16)
            kd[(c + 1) * BLOCK:(c + 2) * BLOCK, :] = (kv * lax.rsqrt(_head_mean(kv * kv, seg) + EPS) * kg).astype(BF16)
        od[...] = v_ref[...].astype(F32)
        for c in range(nchunk):
            vd[(c + 1) * BLOCK:(c + 2) * BLOCK, :] = od[_chunk_rows(c, d, nb), :].astype(BF16)

        col = lax.broadcasted_iota(jnp.int32, (BLOCK, 2 * BLOCK), 1)
        for j in range(2):
            mj = jnp.logical_and(lane >= j * HEAD_DIM, lane < (j + 1) * HEAD_DIM)
            for c in range(nchunk):
                kw = kd[c * BLOCK:(c + 2) * BLOCK, :]
                kj = jnp.where(mj, kw, jnp.zeros_like(kw))
                s = lax.dot_general(qd[c * BLOCK:(c + 1) * BLOCK, :], kj, contract_lanes,
                                    preferred_element_type=F32) + bias_ref[j]
                if c % nb == 0:
                    s = jnp.where(col < BLOCK, NEG_INF, s)
                sbuf[c] = s
            for c in range(nchunk):
                rows = slice(c * BLOCK, (c + 1) * BLOCK)
                s = sbuf[c]
                mx = jnp.max(s, axis=-1, keepdims=True)
                p = jnp.exp(s - mx).astype(BF16)
                vw = vd[c * BLOCK:(c + 2) * BLOCK, :]
                oj = jnp.dot(p, jnp.where(mj, vw, jnp.ones_like(vw)), preferred_element_type=F32)
                l = pltpu.roll(oj, HEAD_DIM, 1)
                on = oj / l
                ls = mx + jnp.log(l)
                if j == 0:
                    od[rows, :] = on
                    ld[rows, :] = ls
                else:
                    od[rows, :] = jnp.where(mj, on, od[rows, :])
                    ld[rows, :] = jnp.where(mj, ls, ld[rows, :])

        for c in range(nchunk):
            rows = _chunk_rows(c, d, nb)
            o_ref[rows, :] = od[c * BLOCK:(c + 1) * BLOCK, :]
            lse_ref[rows, :] = ld[c * BLOCK:(c + 1) * BLOCK, :]

    ucol = lambda c0: pl.BlockSpec((t, LANES), lambda h: (0, c0 + h))
    vec = pl.BlockSpec((1, LANES), lambda h: (0, 0))
    oblk = pl.BlockSpec((t, LANES), lambda h: (0, h))
    osh = jax.ShapeDtypeStruct((t, hpg * HEAD_DIM), F32)
    return pl.pallas_call(
        body, name=name, grid=(hp,),
        in_specs=[ucol(qc0), ucol(kc0), ucol(vc0), vec, vec,
                  pl.BlockSpec((2, BLOCK, 2 * BLOCK), lambda h: (gi * hp + h, 0, 0))],
        out_specs=(oblk, oblk), out_shape=(osh, osh),
        scratch_shapes=[pltpu.VMEM((t, LANES), BF16), pltpu.VMEM((t + BLOCK, LANES), BF16),
                        pltpu.VMEM((t + BLOCK, LANES), BF16), pltpu.VMEM((t, LANES), F32), pltpu.VMEM((t, LANES), F32),
                        pltpu.VMEM((nchunk, BLOCK, 2 * BLOCK), F32)],
        compiler_params=_cparams(("arbitrary",)),
    )(u, u, u, qg2, kg2, bias)


def _attn_bwd(u, do_g, dd_g, lse_g, qg2, kg2, bias, ds_in, dqkv_in, gi, cols, hpg, name):
    t = u.shape[0]
    d = DILATIONS[gi]
    nchunk = t // BLOCK
    nb = (t // d) // BLOCK
    hp = hpg // 2
    qc0, kc0, vc0 = [(c + gi * hpg * HEAD_DIM) // LANES for c in cols]
    contract_lanes = (((1,), (1,)), ((), ()))
    contract_rows = (((0,), (0,)), ((), ()))
    qscale = HEAD_DIM ** -0.5

    def body(q_ref, k_ref, v_ref, do_ref, dd_ref, lse_ref, qg_ref, kg_ref, bias_ref, dsin_ref, _dq_in, _dk_in, _dv_in,
             dq_ref, dk_ref, dv_ref, dgq_ref, dgk_ref, dsout_ref,
             qd, kd, vd, dod, ddd, ld, dqd, dkd, dvd, dsacc, pbuf, dsbuf, qs, ks):
        seg = _segment_ones()
        lane = lax.broadcasted_iota(jnp.int32, (1, LANES), 1)
        qg = qg_ref[...] * qscale
        kg = kg_ref[...]
        kd[0:BLOCK, :] = jnp.zeros((BLOCK, LANES), BF16)
        vd[0:BLOCK, :] = jnp.zeros((BLOCK, LANES), BF16)
        dsacc[...] = jnp.zeros_like(dsacc)
        qs[...] = q_ref[...].astype(F32)
        ks[...] = k_ref[...].astype(F32)
        dqd[...] = v_ref[...].astype(F32)
        for c in range(nchunk):
            rows = _chunk_rows(c, d, nb)
            qv = qs[rows, :]
            kv = ks[rows, :]
            qd[c * BLOCK:(c + 1) * BLOCK, :] = (qv * lax.rsqrt(_head_mean(qv * qv, seg) + EPS) * qg).astype(BF16)
            kd[(c + 1) * BLOCK:(c + 2) * BLOCK, :] = (kv * lax.rsqrt(_head_mean(kv * kv, seg) + EPS) * kg).astype(BF16)
            vd[(c + 1) * BLOCK:(c + 2) * BLOCK, :] = dqd[rows, :].astype(BF16)
            dod[c * BLOCK:(c + 1) * BLOCK, :] = do_ref[rows, :].astype(BF16)
            ddd[c * BLOCK:(c + 1) * BLOCK, :] = dd_ref[rows, :]
            ld[c * BLOCK:(c + 1) * BLOCK, :] = lse_ref[rows, :]

        col = lax.broadcasted_iota(jnp.int32, (BLOCK, 2 * BLOCK), 1)
        for j in range(2):
            mj = jnp.logical_and(lane >= j * HEAD_DIM, lane < (j + 1) * HEAD_DIM)
            first = lane == j * HEAD_DIM
            for c in range(nchunk):
                rows = slice(c * BLOCK, (c + 1) * BLOCK)
                kw = kd[c * BLOCK:(c + 2) * BLOCK, :]
                vw = vd[c * BLOCK:(c + 2) * BLOCK, :]
                kj = jnp.where(mj, kw, jnp.zeros_like(kw))
                vj = jnp.where(mj, vw, jnp.zeros_like(vw))
                s = lax.dot_general(qd[rows, :], kj, contract_lanes, preferred_element_type=F32) + bias_ref[j]
                if c % nb == 0:
                    s = jnp.where(col < BLOCK, NEG_INF, s)
                dp = lax.dot_general(dod[rows, :], vj, contract_lanes, preferred_element_type=F32)
                lse_j = jnp.sum(jnp.where(first, ld[rows, :], 0.0), axis=-1, keepdims=True)
                dd_j = jnp.sum(jnp.where(first, ddd[rows, :], 0.0), axis=-1, keepdims=True)
                p = jnp.exp(s - lse_j)
                ds = p * (dp + dd_j)
                dsacc[j] += ds
                pbuf[j, c] = p.astype(BF16)
                dsbuf[j, c] = ds.astype(BF16)
        for c in range(nchunk):
            rows = slice(c * BLOCK, (c + 1) * BLOCK)
            has_next = c + 1 < nchunk and (c + 1) % nb != 0
            kw = kd[c * BLOCK:(c + 2) * BLOCK, :]
            dq = jnp.zeros((BLOCK, LANES), F32)
            dk = jnp.zeros((BLOCK, LANES), F32)
            dv = jnp.zeros((BLOCK, LANES), F32)
            both = slice(c * BLOCK, (c + 2) * BLOCK) if has_next else rows
            for j in range(2):
                mj = jnp.logical_and(lane >= j * HEAD_DIM, lane < (j + 1) * HEAD_DIM)
                dq = dq + jnp.dot(dsbuf[j, c], jnp.where(mj, kw, jnp.zeros_like(kw)), preferred_element_type=F32)
                dsk = dsbuf[j, c, :, BLOCK:]
                pk = pbuf[j, c, :, BLOCK:]
                if has_next:
                    dsk = jnp.concatenate([dsk, dsbuf[j, c + 1, :, :BLOCK]], axis=0)
                    pk = jnp.concatenate([pk, pbuf[j, c + 1, :, :BLOCK]], axis=0)
                qq = qd[both, :]
                dd = dod[both, :]
                dk = dk + lax.dot_general(dsk, jnp.where(mj, qq, jnp.zeros_like(qq)), contract_rows,
                                          preferred_element_type=F32)
                dv = dv + lax.dot_general(pk, jnp.where(mj, dd, jnp.zeros_like(dd)), contract_rows,
                                          preferred_element_type=F32)
            dqd[rows, :] = dq
            dkd[rows, :] = dk
            dvd[rows, :] = dv

        dsout_ref[...] = dsin_ref[...] + dsacc[...]

        dgq = jnp.zeros((1, LANES), F32)
        dgk = jnp.zeros((1, LANES), F32)
        for c in range(nchunk):
            rows = _chunk_rows(c, d, nb)
            qv = qs[rows, :]
            rq = lax.rsqrt(_head_mean(qv * qv, seg) + EPS)
            qh = qv * rq
            dy = dqd[c * BLOCK:(c + 1) * BLOCK, :]
            dgq = dgq + jnp.sum(dy * qh, axis=0, keepdims=True) * qscale
            dxh = dy * qg
            ddd[rows, :] = rq * (dxh - qh * _head_mean(dxh * qh, seg))
            kv = ks[rows, :]
            rk = lax.rsqrt(_head_mean(kv * kv, seg) + EPS)
            kh = kv * rk
            dy = dkd[c * BLOCK:(c + 1) * BLOCK, :]
            dgk = dgk + jnp.sum(dy * kh, axis=0, keepdims=True)
            dxh = dy * kg
            ld[rows, :] = rk * (dxh - kh * _head_mean(dxh * kh, seg))
        dq_ref[...] = ddd[...].astype(BF16)
        dk_ref[...] = ld[...].astype(BF16)
        for c in range(nchunk):
            ddd[_chunk_rows(c, d, nb), :] = dvd[c * BLOCK:(c + 1) * BLOCK, :]
        dv_ref[...] = ddd[...].astype(BF16)
        dgq_ref[0] = dgq
        dgk_ref[0] = dgk

    ucol = lambda c0: pl.BlockSpec((t, LANES), lambda h: (0, c0 + h))
    vec = pl.BlockSpec((1, LANES), lambda h: (0, 0))
    oblk = pl.BlockSpec((t, LANES), lambda h: (0, h))
    bblk = pl.BlockSpec((2, BLOCK, 2 * BLOCK), lambda h: (gi * hp + h, 0, 0))
    gblk = pl.BlockSpec((1, 1, LANES), lambda h: (h, 0, 0))
    osh = jax.ShapeDtypeStruct((t, N_GROUPS * hpg * HEAD_DIM), BF16)
    gsh = jax.ShapeDtypeStruct((hp, 1, LANES), F32)
    gcol = pl.BlockSpec((t, LANES), lambda h: (0, gi * hp + h))
    hbm = pl.BlockSpec(memory_space=pl.ANY)
    return pl.pallas_call(
        body, name=name, grid=(hp,),
        in_specs=[ucol(qc0), ucol(kc0), ucol(vc0), oblk, oblk, oblk, vec, vec, bblk, bblk, hbm, hbm, hbm],
        out_specs=(gcol, gcol, gcol, gblk, gblk, bblk),
        out_shape=(osh, osh, osh, gsh, gsh, jax.ShapeDtypeStruct(ds_in.shape, F32)),
        input_output_aliases={9: 5, 10: 0, 11: 1, 12: 2},
        scratch_shapes=[pltpu.VMEM((t, LANES), BF16), pltpu.VMEM((t + BLOCK, LANES), BF16),
                        pltpu.VMEM((t + BLOCK, LANES), BF16), pltpu.VMEM((t, LANES), BF16),
                        pltpu.VMEM((t, LANES), F32), pltpu.VMEM((t, LANES), F32), pltpu.VMEM((t, LANES), F32),
                        pltpu.VMEM((t, LANES), F32), pltpu.VMEM((t, LANES), F32),
                        pltpu.VMEM((2, BLOCK, 2 * BLOCK), F32), pltpu.VMEM((2, nchunk, BLOCK, 2 * BLOCK), BF16),
                        pltpu.VMEM((2, nchunk, BLOCK, 2 * BLOCK), BF16), pltpu.VMEM((t, LANES), F32),
                        pltpu.VMEM((t, LANES), F32)],
        compiler_params=_cparams(("arbitrary",)),
    )(u, u, u, do_g, dd_g, lse_g, qg2, kg2, bias, ds_in, *dqkv_in)


def _group_weights(l0, l1, l2):
    mx = jnp.maximum(jnp.maximum(l0, l1), l2)
    e0, e1, e2 = jnp.exp(l0 - mx), jnp.exp(l1 - mx), jnp.exp(l2 - mx)
    inv = 1.0 / (e0 + e1 + e2)
    return e0 * inv, e1 * inv, e2 * inv


def _combine_fwd(os_, lses, name):
    t, ao = os_[0].shape
    ROWS = _rows(t, 26 * ao)

    def body(o0, o1, o2, l0, l1, l2, o_ref):
        w0, w1, w2 = _group_weights(l0[...], l1[...], l2[...])
        o_ref[...] = (w0 * o0[...] + w1 * o1[...] + w2 * o2[...]).astype(BF16)

    row = pl.BlockSpec((ROWS, ao), lambda i: (i, 0))
    return pl.pallas_call(
        body, name=name, grid=(t // ROWS,), in_specs=[row] * 6, out_specs=row,
        out_shape=jax.ShapeDtypeStruct((t, ao), BF16), compiler_params=_cparams(("arbitrary",)),
    )(*os_, *lses)


def _combine_bwd(do, os_, lses, name):
    t, ao = do.shape
    idx = np.arange(ao) // HEAD_DIM
    seg = jnp.asarray((idx[:, None] == idx[None, :]).astype(np.float32), dtype=BF16)
    ROWS = _rows(t, 52 * ao)

    def body(do_ref, o0, o1, o2, l0, l1, l2, seg_ref, g0, g1, g2, d0, d1, d2):
        w0, w1, w2 = _group_weights(l0[...], l1[...], l2[...])
        dov = do_ref[...]
        o = w0 * o0[...] + w1 * o1[...] + w2 * o2[...]
        sd = _segment_sum(dov * o, seg_ref[...])
        for w, gref, dref in ((w0, g0, d0), (w1, g1, d1), (w2, g2, d2)):
            gref[...] = w * dov
            dref[...] = -(w * sd)

    row = pl.BlockSpec((ROWS, ao), lambda i: (i, 0))
    sh = jax.ShapeDtypeStruct((t, ao), F32)
    outs = pl.pallas_call(
        body, name=name, grid=(t // ROWS,), in_specs=[row] * 7 + [pl.BlockSpec((ao, ao), lambda i: (0, 0))],
        out_specs=(row,) * 6, out_shape=(sh,) * 6, compiler_params=_cparams(("arbitrary",)),
    )(do, *os_, *lses, seg)
    return outs[:3], outs[3:]


def _adamw(w, g, m, v, name):
    shape = w.shape
    cols = shape[-1]
    rows = int(np.prod(shape[:-1]))
    tr = rows if rows <= 512 else _tile_rows(rows)
    c1 = 1.0 - ADAM_B1 ** ADAM_STEP
    c2 = 1.0 - ADAM_B2 ** ADAM_STEP

    def body(w_ref, g_ref, m_ref, v_ref, d_ref, nm_ref, nv_ref):
        gv = g_ref[...]
        mn = ADAM_B1 * m_ref[...] + (1.0 - ADAM_B1) * gv
        vn = ADAM_B2 * v_ref[...] + (1.0 - ADAM_B2) * (gv * gv)
        nm_ref[...] = mn
        nv_ref[...] = vn
        d_ref[...] = -ADAM_LR * ((mn / c1) / (jnp.sqrt(vn / c2) + ADAM_EPS) + ADAM_WD * w_ref[...])

    blk = pl.BlockSpec((tr, cols), lambda i: (i, 0))
    sh = jax.ShapeDtypeStruct((rows, cols), F32)
    outs = pl.pallas_call(
        body, name=name, grid=(rows // tr,), in_specs=[blk] * 4, out_specs=(blk,) * 3, out_shape=(sh,) * 3,
        compiler_params=_cparams(("arbitrary",)),
    )(*[a.reshape(rows, cols) for a in (w, g, m, v)])
    return tuple(o.reshape(shape) for o in outs)


def _tile_rows(rows):
    for t in (512, 256, 128, 64, 32, 16, 8):
        if rows % t == 0:
            return t
    return rows


def _sum_slots(recv, parts, me, layers, l, name):
    _, rows, cols = recv.shape
    tr = rows if rows <= 512 else _tile_rows(rows)

    def body(me_ref, r_ref, own_ref, _, o_ref):
        acc = jnp.zeros(o_ref.shape, F32)
        for s in range(N_DEV):
            acc = acc + jnp.where(me_ref[0] == s, own_ref[...], r_ref[s]).astype(F32)
        o_ref[...] = acc

    return pl.pallas_call(
        body, name=name,
        grid_spec=pltpu.PrefetchScalarGridSpec(
            num_scalar_prefetch=1, grid=(rows // tr,),
            in_specs=[pl.BlockSpec((N_DEV, tr, cols), lambda i, me: (0, i, 0)),
                      pl.BlockSpec((None, tr, cols), lambda i, me: (me[0], i, 0)),
                      pl.BlockSpec(memory_space=pl.ANY)],
            out_specs=pl.BlockSpec((None, tr, cols), lambda i, me: (l, i, 0))),
        out_shape=jax.ShapeDtypeStruct(layers.shape, F32), input_output_aliases={3: 0},
        compiler_params=_cparams(("arbitrary",)),
    )(me.reshape(1), recv, parts, layers)


def _peer(k):
    x, y, c = lax.axis_index("x"), lax.axis_index("y"), lax.axis_index("c")
    return (1 - x if k & 4 else x, 1 - y if k & 2 else y, 1 - c if k & 1 else c)


def _dev_index(p):
    return 4 * p[0] + 2 * p[1] + p[2]


HBM_SPEC = pl.BlockSpec(memory_space=pltpu.HBM)
SEM_SPEC = pl.BlockSpec(memory_space=pltpu.SEMAPHORE)
ANY_SPEC = pl.BlockSpec(memory_space=pl.ANY)
CHIPS = (4, 2, 6)


def _remote(src, dst, send_sem, recv_sem, to):
    return pltpu.make_async_remote_copy(src_ref=src, dst_ref=dst, send_sem=send_sem, recv_sem=recv_sem,
                                        device_id=to, device_id_type=MESH)


def _hbm(a):
    return pltpu.with_memory_space_constraint(a, pltpu.HBM)


def _split_call(body, name, bufs, sems_in, sem_out_sizes, after):
    nb, ns, no = len(bufs), len(sems_in), len(sem_out_sizes)
    extra = [] if after is None else [after]

    def kern(*refs):
        pos = nb + ns + len(extra)
        body(refs[:nb], refs[nb:nb + ns], refs[pos:pos + no])
        token_ref = refs[pos + no + nb]
        token_ref[...] = jnp.zeros_like(token_ref)

    out_shape = (tuple(pltpu.SemaphoreType.DMA((s,)) for s in sem_out_sizes)
                 + tuple(pltpu.HBM(b.shape, b.dtype) for b in bufs) + (jax.ShapeDtypeStruct((8, LANES), F32),))
    res = pl.pallas_call(
        kern, name=name, out_shape=out_shape,
        in_specs=[HBM_SPEC] * nb + [SEM_SPEC] * ns + [ANY_SPEC] * len(extra),
        out_specs=(SEM_SPEC,) * no + (HBM_SPEC,) * nb + (pl.BlockSpec(memory_space=pltpu.VMEM),),
        input_output_aliases={i: no + i for i in range(nb)},
        compiler_params=pltpu.CompilerParams(has_side_effects=pltpu.SideEffectType.DATAFLOW_SIDE_EFFECTING),
    )(*bufs, *sems_in, *extra)
    return res[:no], res[no:no + nb], res[no + nb]


def _gather_start(shards, lands, after, name):
    n = len(shards)

    def body(bufs, _, sems):
        ins, lnd = bufs[:n], bufs[n:]
        d2d_s, d2d_r, ici_s, ici_r = sems
        me = _dev_index(_peer(0))
        for j, k in enumerate(CHIPS):
            for i in range(n):
                _remote(ins[i], lnd[i].at[me], ici_s.at[j], ici_r.at[j], _peer(k)).start()
        for i in range(n):
            _remote(ins[i], lnd[i].at[me], d2d_s.at[0], d2d_r.at[0], _peer(1)).start()

    return _split_call(body, name, [_hbm(a) for a in (*shards, *lands)], [], (1, 1, 3, 3), after)


def _gather_forward(n, bufs, ici_r, after, name):
    def body(refs, sems_in, sems):
        ins, lnd = refs[:n], refs[n:]
        (arrived,) = sems_in
        fwd_s, fwd_r = sems
        for j, k in enumerate(CHIPS):
            blk = _dev_index(_peer(k))
            for i in range(n):
                _remote(ins[i], lnd[i].at[blk], fwd_s.at[j], arrived.at[j], _peer(k)).wait_recv()
            for i in range(n):
                _remote(lnd[i].at[blk], lnd[i].at[blk], fwd_s.at[j], fwd_r.at[j], _peer(1)).start()

    return _split_call(body, name, bufs, [ici_r], (3, 3), after)


def _gather_finish(n, bufs, d2d_s, d2d_r, ici_s, fwd_s, fwd_r, after, name):
    def body(refs, sems_in, _):
        ins, lnd = refs[:n], refs[n:]
        d2d_send, d2d_recv, ici_send, fwd_send, fwd_recv = sems_in
        sib = _peer(1)
        for i in range(n):
            cp = _remote(ins[i], lnd[i].at[_dev_index(sib)], d2d_send.at[0], d2d_recv.at[0], sib)
            cp.wait_send()
            cp.wait_recv()
        for j, k in enumerate(CHIPS):
            passed = _dev_index(_peer(k))
            landed = _dev_index(_peer(k | 1))
            for i in range(n):
                _remote(ins[i], lnd[i].at[passed], ici_send.at[j], fwd_recv.at[j], _peer(k)).wait_send()
                cp = _remote(lnd[i].at[passed], lnd[i].at[landed], fwd_send.at[j], fwd_recv.at[j], sib)
                cp.wait_send()
                cp.wait_recv()

    _, out, token = _split_call(body, name, bufs, [d2d_s, d2d_r, ici_s, fwd_s, fwd_r], (), after)
    return out[n:], token


def _exchange_start(parts, lands, after, name):
    n = len(parts)

    def body(bufs, _, sems):
        src, lnd = bufs[:n], bufs[n:]
        send, recv = sems
        me = _dev_index(_peer(0))
        for k in (4, 5, 2, 3, 6, 7, 1):
            to = _peer(k)
            for i in range(n):
                _remote(src[i].at[_dev_index(to)], lnd[i].at[me], send.at[k - 1], recv.at[k - 1], to).start()

    return _split_call(body, name, [_hbm(a) for a in (*parts, *lands)], [], (7, 7), after)


def _exchange_finish(n, bufs, send, recv, after, name):
    def body(refs, sems_in, _):
        src, lnd = refs[:n], refs[n:]
        send_, recv_ = sems_in
        me = _dev_index(_peer(0))
        for k in range(1, N_DEV):
            frm = _peer(k)
            for i in range(n):
                cp = _remote(src[i].at[me], lnd[i].at[_dev_index(frm)], send_.at[k - 1], recv_.at[k - 1], frm)
                cp.wait_send()
                cp.wait_recv()

    _, out, token = _split_call(body, name, bufs, [send, recv], (), after)
    return out[:n], out[n:], token


def _all_reduce_small(v, after, name):
    rows = v.shape[0]

    def body(v_ref, _, o_ref, buf, send_sems, recv_sems):
        me = _dev_index(_peer(0))
        buf[me] = v_ref[...]
        copies = []
        for k in range(1, N_DEV):
            copies.append(pltpu.make_async_remote_copy(
                src_ref=v_ref, dst_ref=buf.at[me], send_sem=send_sems.at[k - 1], recv_sem=recv_sems.at[k - 1],
                device_id=_peer(k), device_id_type=MESH))
        for cp in copies:
            cp.start()
        for k in range(1, N_DEV):
            pltpu.make_async_remote_copy(
                src_ref=v_ref, dst_ref=buf.at[_dev_index(_peer(k))], send_sem=send_sems.at[k - 1],
                recv_sem=recv_sems.at[k - 1], device_id=_peer(k), device_id_type=MESH).wait_recv()
        for cp in copies:
            cp.wait_send()
        acc = buf[0]
        for s in range(1, N_DEV):
            acc = acc + buf[s]
        o_ref[...] = acc

    vm = pl.BlockSpec(memory_space=pltpu.VMEM)
    return pl.pallas_call(
        body, name=name, in_specs=[vm, pl.BlockSpec(memory_space=pl.ANY)], out_specs=vm,
        out_shape=jax.ShapeDtypeStruct(v.shape, F32),
        scratch_shapes=[pltpu.VMEM((N_DEV, rows, LANES), F32), pltpu.SemaphoreType.DMA((7,)),
                        pltpu.SemaphoreType.DMA((7,))],
    )(v, after)


def _columns(cdim, ao):
    q_col = 2 * cdim
    attn_dim = N_GROUPS * ao
    return (q_col, q_col + attn_dim, q_col + 2 * attn_dim), q_col + 3 * attn_dim


def _layer_fwd(x, h1, sm, bg, get_rest, bias, hpg, next_gain):
    cdim = sm["conv_ln_g"].shape[0]
    ao = hpg * HEAD_DIM
    cols, gate_col = _columns(cdim, ao)
    qg2 = jnp.tile(sm["q_norm_g"], 2).reshape(1, LANES)
    kg2 = jnp.tile(sm["k_norm_g"], 2).reshape(1, LANES)
    u = _mm_in_pairs(h1, bg["w_in"], "mm_in")
    zc = _conv_fwd(u, bg["conv_dw_w"], sm["conv_dw_b"], cdim, "conv_fwd")
    zs = _ln_swish_fwd(zc, sm["conv_ln_g"], sm["conv_ln_b"], "ln_swish_fwd")
    os_, lses = [], []
    for gi in range(N_GROUPS):
        o_g, lse_g = _attn_fwd(u, qg2, kg2, bias, gi, cols, hpg, "attn_fwd_g%d" % gi)
        os_.append(o_g)
        lses.append(lse_g)
    o = _combine_fwd(os_, lses, "combine_fwd")
    bg = {**bg, **get_rest(o)}
    yc = _mm(zs, bg["w_conv_out"], name="mm_conv_out")
    ya = _mm(o, bg["w_attn_out"], name="mm_attn_out")
    mg = _gate_fwd(u, yc, ya, gate_col, "gate_fwd")
    x1, h2 = _mm(mg, bg["w_out"], epi="res_rms", extra=x, gain=sm["norm2_g"], name="mm_out")
    f = _mm(h2, bg["w_ff1"], out_dtype=BF16, name="mm_ff1")
    if next_gain is None:
        x2, h_next = _mm(f, bg["w_ff2"], a_relu2=True, epi="res", extra=x1, name="mm_ff2"), None
    else:
        x2, h_next = _mm(f, bg["w_ff2"], a_relu2=True, epi="res_rms", extra=x1, gain=next_gain, name="mm_ff2")
    saved = dict(x=x, h1=h1, u=u, zc=zc, zs=zs, yc=yc, os=os_, lses=lses, o=o, ya=ya, mg=mg, x1=x1, h2=h2, f=f,
                 qg2=qg2, kg2=kg2)
    return x2, h_next, saved, bg


GRAD_GROUPS = (("w_ff2", "w_ff1"), ("w_out", "w_conv_out", "w_attn_out", "conv_dw_w"), ("w_in",))


def _layer_bwd(dx, s, sm, bg, bias, ds_sum, after, emit):
    cdim = sm["conv_ln_g"].shape[0]
    ao = bg["w_attn_out"].shape[0]
    hpg = ao // HEAD_DIM
    hp = hpg // 2
    cols, gate_col = _columns(cdim, ao)
    g = {}
    df = _mm(dx, bg["w_ff2"], tb=True, epi="drelu2", extra=s["f"], out_dtype=BF16, after=after, name="mm_dff2")
    g["w_ff2"] = _mm(s["f"], dx, ta=True, a_relu2=True, out_dtype=BF16, name="mm_gw_ff2")
    g["w_ff1"] = _mm(s["h2"], df, ta=True, out_dtype=BF16, out_slots=True, name="mm_gw_ff1")
    after = emit(GRAD_GROUPS[0], g)
    dx1, dg2 = _mm(df, bg["w_ff1"], tb=True, epi="rms_bwd", extra=(s["x1"], dx), gain=sm["norm2_g"], after=after,
                   name="mm_dff1")
    g["norm2_g"] = dg2[0]
    dmg = _mm(dx1, bg["w_out"], tb=True, name="mm_dout")
    g["w_out"] = _mm(s["mg"], dx1, ta=True, out_dtype=BF16, name="mm_gw_out")
    dyc, dya, dugc, duga = _gate_bwd(dmg, s["u"], s["yc"], s["ya"], gate_col, "gate_bwd")
    dzs = _mm(dyc, bg["w_conv_out"], tb=True, name="mm_dconv_out")
    g["w_conv_out"] = _mm(s["zs"], dyc, ta=True, out_dtype=BF16, name="mm_gw_conv_out")
    do = _mm(dya, bg["w_attn_out"], tb=True, name="mm_dattn_out")
    g["w_attn_out"] = _mm(s["o"], dya, ta=True, out_dtype=BF16, name="mm_gw_attn_out")
    dzc, dlg, dlb = _ln_swish_bwd(dzs, s["zc"], sm["conv_ln_g"], sm["conv_ln_b"], "ln_swish_bwd")
    g["conv_ln_g"] = dlg[0]
    g["conv_ln_b"] = dlb[0]
    da, dgt, dcw, dcb = _conv_bwd(dzc, s["u"], bg["conv_dw_w"], cdim, "conv_bwd")
    g["conv_dw_w"] = dcw[:CONV_WIDTH].astype(BF16)
    g["conv_dw_b"] = dcb[0]
    after = emit(GRAD_GROUPS[1], g)
    do_gs, dd_gs = _combine_bwd(do, s["os"], s["lses"], "combine_bwd")
    dqkv = [lax.empty((dx.shape[0], N_GROUPS * ao), BF16) for _ in range(3)]
    dgq = jnp.zeros((HEAD_DIM,), F32)
    dgk = jnp.zeros((HEAD_DIM,), F32)
    for gi in range(N_GROUPS):
        *dqkv, gq, gk, ds_sum = _attn_bwd(s["u"], do_gs[gi], dd_gs[gi], s["lses"][gi], s["qg2"], s["kg2"], bias,
                                          ds_sum, dqkv, gi, cols, hpg, "attn_bwd_g%d" % gi)
        dgq = dgq + jnp.sum(gq.reshape(hp * 2, HEAD_DIM), axis=0)
        dgk = dgk + jnp.sum(gk.reshape(hp * 2, HEAD_DIM), axis=0)
    g["q_norm_g"] = dgq
    g["k_norm_g"] = dgk
    du = jnp.concatenate([da, dgt] + dqkv + [dugc, duga], axis=1)
    g["w_in"] = _mm_gw_in_pairs(s["h1"], du, after, "mm_gw_in")
    after = emit(GRAD_GROUPS[2], g)
    dx0, dg1 = _mm_din_pairs(du, bg["w_in"], s["x"], sm["norm1_g"], dx1, after, "mm_din")
    g["norm1_g"] = dg1[0]
    return dx0, g, ds_sum


BIG = ("w_in", "conv_dw_w", "w_conv_out", "w_attn_out", "w_out", "w_ff1", "w_ff2")
COL_SHARDED = ("w_in", "conv_dw_w", "w_conv_out", "w_attn_out", "w_ff1")
SMALL = ("rel_bias", "norm1_g", "q_norm_g", "k_norm_g", "conv_dw_b", "conv_ln_g", "conv_ln_b", "norm2_g")
WEIGHTS = ("rel_bias", "norm1_g", "w_in", "q_norm_g", "k_norm_g", "conv_dw_w", "conv_dw_b", "conv_ln_g", "conv_ln_b",
           "w_conv_out", "w_attn_out", "w_out", "norm2_g", "w_ff1", "w_ff2")


def _to_whole(name, gathered):
    n, a, b = gathered.shape
    if name in COL_SHARDED:
        return gathered.transpose(1, 0, 2).reshape(a, n * b)
    return gathered.reshape(n * a, b)


def _to_slots(name, whole):
    a, b = whole.shape
    if name in COL_SHARDED:
        return whole.reshape(a, N_DEV, b // N_DEV).transpose(1, 0, 2)
    return whole.reshape(N_DEV, a // N_DEV, b)


def _own_slot(block, me):
    land = lax.empty((N_DEV,) + block.shape, block.dtype)
    return lax.dynamic_update_slice(land, block[None], (me,) + (0,) * block.ndim)


def kernel(x, rel_bias, norm1_g, w_in, q_norm_g, k_norm_g, conv_dw_w, conv_dw_b, conv_ln_g, conv_ln_b, w_conv_out, w_attn_out, w_out, norm2_g, w_ff1, w_ff2, loss_target, m_rel_bias, m_norm1_g, m_w_in, m_q_norm_g, m_k_norm_g, m_conv_dw_w, m_conv_dw_b, m_conv_ln_g, m_conv_ln_b, m_w_conv_out, m_w_attn_out, m_w_out, m_norm2_g, m_w_ff1, m_w_ff2, v_rel_bias, v_norm1_g, v_w_in, v_q_norm_g, v_k_norm_g, v_conv_dw_w, v_conv_dw_b, v_conv_ln_g, v_conv_ln_b, v_w_conv_out, v_w_attn_out, v_w_out, v_norm2_g, v_w_ff1, v_w_ff2):
    w = dict(rel_bias=rel_bias, norm1_g=norm1_g, w_in=w_in, q_norm_g=q_norm_g, k_norm_g=k_norm_g, conv_dw_w=conv_dw_w,
             conv_dw_b=conv_dw_b, conv_ln_g=conv_ln_g, conv_ln_b=conv_ln_b, w_conv_out=w_conv_out,
             w_attn_out=w_attn_out, w_out=w_out, norm2_g=norm2_g, w_ff1=w_ff1, w_ff2=w_ff2)
    mom = dict(rel_bias=m_rel_bias, norm1_g=m_norm1_g, w_in=m_w_in, q_norm_g=m_q_norm_g, k_norm_g=m_k_norm_g,
               conv_dw_w=m_conv_dw_w, conv_dw_b=m_conv_dw_b, conv_ln_g=m_conv_ln_g, conv_ln_b=m_conv_ln_b,
               w_conv_out=m_w_conv_out, w_attn_out=m_w_attn_out, w_out=m_w_out, norm2_g=m_norm2_g, w_ff1=m_w_ff1,
               w_ff2=m_w_ff2)
    var = dict(rel_bias=v_rel_bias, norm1_g=v_norm1_g, w_in=v_w_in, q_norm_g=v_q_norm_g, k_norm_g=v_k_norm_g,
               conv_dw_w=v_conv_dw_w, conv_dw_b=v_conv_dw_b, conv_ln_g=v_conv_ln_g, conv_ln_b=v_conv_ln_b,
               w_conv_out=v_w_conv_out, w_attn_out=v_w_attn_out, w_out=v_w_out, norm2_g=v_norm2_g, w_ff1=v_w_ff1,
               w_ff2=v_w_ff2)

    depth = norm1_g.shape[0]
    me = 4 * lax.axis_index("x") + 2 * lax.axis_index("y") + lax.axis_index("c")
    odd_core = lax.axis_index("c") == 1
    hpg = w_attn_out.shape[1] // HEAD_DIM
    buckets = jnp.asarray(_bucket_table())
    bias = _bias_expand(rel_bias, buckets, hpg, "bias_expand")

    first_names = ("w_in", "conv_dw_w")
    rest_names = tuple(k for k in BIG if k not in first_names)

    def chain_start(l, names, after):
        shards = [w[k][l] if k == "conv_dw_w" else w[k][l].astype(BF16) for k in names]
        if "w_in" in names:
            i = names.index("w_in")
            shards[i] = jnp.where(odd_core, jnp.pad(shards[i], ((0, 0), (SHIFT, 0))),
                                  jnp.pad(shards[i], ((0, 0), (0, SHIFT))))
        sems, bufs, token = _gather_start(shards, [_own_slot(s, me) for s in shards], after,
                                          "gather_start_%s_l%d" % (names[0], l))
        return dict(l=l, names=names, sems=sems, bufs=bufs, token=token)

    def chain_forward(ch, after):
        fwd, bufs, token = _gather_forward(len(ch["names"]), ch["bufs"], ch["sems"][3], after,
                                           "gather_forward_%s_l%d" % (ch["names"][0], ch["l"]))
        ch.update(fwd=fwd, bufs=bufs)
        return token

    def chain_finish(ch, after):
        d2d_s, d2d_r, ici_s, _ = ch["sems"]
        gathered, _ = _gather_finish(len(ch["names"]), ch["bufs"], d2d_s, d2d_r, ici_s, ch["fwd"][0], ch["fwd"][1],
                                     after, "gather_finish_%s_l%d" % (ch["names"][0], ch["l"]))
        out = {k: a if k == "w_in" else _to_whole(k, a) for k, a in zip(ch["names"], gathered)}
        if "conv_dw_w" in out:
            out["conv_dw_w"] = jnp.pad(out["conv_dw_w"], ((0, CONV_TAPS_PADDED - CONV_WIDTH), (0, 0)))
        return out

    xs = x[0]
    h1 = _rms_fwd(xs, norm1_g[0], "rms1_fwd")
    saved, bigs, smalls = [], [], []
    chains = {}
    for l in range(depth):
        sm = {k: w[k][l] for k in SMALL if k != "rel_bias"}
        if l == 0:
            first = chain_start(0, first_names, None)
            token = chain_forward(first, None)
            rest = chain_start(0, rest_names, token)
            bg = chain_finish(first, rest["token"])

            def get_rest(o, rest=rest):
                token = chain_forward(rest, o)
                if depth > 1:
                    chains[1] = (chain_start(1, first_names, token),)
                    chains[1] += (chain_start(1, rest_names, chains[1][0]["token"]),)
                    token = chains[1][1]["token"]
                return chain_finish(rest, token)
        elif l == 1:
            first, rest = chains[1]
            token = chain_forward(first, xs)
            if depth > 2:
                chains[2] = chain_start(2, BIG, token)
                token = chains[2]["token"]
            bg = chain_finish(first, token)

            def get_rest(o, rest=rest):
                return chain_finish(rest, chain_forward(rest, o))
        else:
            token = chain_forward(chains[l], xs)
            if l + 1 < depth:
                chains[l + 1] = chain_start(l + 1, BIG, token)
                token = chains[l + 1]["token"]
            whole = chain_finish(chains[l], token)
            bg = {k: whole[k] for k in first_names}

            def get_rest(o, whole=whole):
                return {k: whole[k] for k in rest_names}
        xs, h1, sv, bg = _layer_fwd(xs, h1, sm, bg, get_rest, bias, hpg, norm1_g[l + 1] if l + 1 < depth else None)
        saved.append(sv)
        bigs.append(bg)
        smalls.append(sm)

    loss, dx = _loss_and_grad(xs, loss_target[0], "loss")

    ds_sum = jnp.zeros((N_GROUPS * hpg, BLOCK, 2 * BLOCK), F32)
    g = {k: [None] * depth for k in SMALL if k != "rel_bias"}
    sums = {k: lax.empty((depth, int(np.prod(w[k].shape[1:-1])), w[k].shape[-1] + (SHIFT if k == "w_in" else 0)), F32)
            for k in BIG}
    pending = []

    def finish_oldest(after):
        names, l, (send, recv), bufs = pending.pop(0)
        parts, recvd, token = _exchange_finish(len(names), bufs, send, recv, after,
                                               "exchange_finish_%s_l%d" % (names[0], l))
        for k, r, p in zip(names, recvd, parts):
            three = (N_DEV, -1, r.shape[-1])
            sums[k] = _sum_slots(r.reshape(three), p.reshape(three), me, sums[k], l, "sum_" + k)
        return token

    def make_emit(l):
        def emit(names, gl):
            parts = [gl[k] if k in ("w_ff1", "w_in") else _to_slots(k, gl[k]) for k in names]
            token = finish_oldest(parts[0]) if len(pending) >= len(GRAD_GROUPS) else None
            lands = [lax.empty(p.shape, p.dtype) for p in parts]
            sems, bufs, token = _exchange_start(parts, lands, token, "exchange_start_%s_l%d" % (names[0], l))
            pending.append((names, l, sems, bufs))
            return token
        return emit

    token = None
    for l in reversed(range(depth)):
        dx, gl, ds_sum = _layer_bwd(dx, saved[l], smalls[l], bigs[l], bias, ds_sum, token, make_emit(l))
        for k in g:
            g[k][l] = gl[k]
        token = None
    grad_x = dx

    g = {k: jnp.stack(v) for k, v in g.items()}
    db = _bias_reduce(ds_sum, buckets, hpg, "bias_reduce")
    g["rel_bias"] = db[:, 0, :NUM_BUCKETS].T

    flat = jnp.concatenate([g[k].reshape(-1) for k in SMALL])
    nflat = flat.shape[0]
    rows = -(-nflat // (8 * LANES)) * 8
    packed = jnp.pad(flat, (0, rows * LANES - nflat)).reshape(rows, LANES)
    grad, outs = {}, {}
    token = dx
    while pending:
        names = pending[0][0]
        finish_oldest(token)
        for k in names:
            total = sums[k]
            if k == "w_in":
                total = jnp.where(odd_core, total[..., SHIFT:], total[..., :w_in.shape[-1]])
            grad[k] = total.reshape(w[k].shape)
            outs[k] = _adamw(w[k], grad[k], mom[k], var[k], "adamw_" + k)
            token = outs[k][0]
    total = _all_reduce_small(packed, token, "reduce_small").reshape(-1)
    off = 0
    for k in SMALL:
        size = int(np.prod(w[k].shape))
        grad[k] = total[off:off + size].reshape(w[k].shape)
        outs[k] = _adamw(w[k], grad[k], mom[k], var[k], "adamw_" + k)
        off += size
    loss = lax.psum(loss[0, 0], ("x", "y", "c"))
    return (loss, grad_x[None], *[grad[k] for k in WEIGHTS], *[outs[k][0] for k in WEIGHTS],
            *[outs[k][1] for k in WEIGHTS], *[outs[k][2] for k in WEIGHTS])
```

```python
import functools
import math

import numpy as np
import jax
import jax.numpy as jnp
from jax import lax
from jax.experimental import pallas as pl
from jax.experimental.pallas import tpu as pltpu

F32 = jnp.float32
BF16 = jnp.bfloat16

HEAD_DIM = 64
N_GROUPS = 3
DILATIONS = (1, 4, 16)
SUB_WINDOW = 128
BLOCK = 128
CONV_WIDTH = 31
CONV_TAPS_PADDED = 32
NUM_BUCKETS = 32
MAX_REL_DISTANCE = 2048
EPS = 1e-6
NEG_INF = -1e30
LANES = 128
SUBLANES = 8

ADAM_LR = 0.001
ADAM_B1 = 0.9
ADAM_B2 = 0.999
ADAM_EPS = 1e-08
ADAM_WD = 0.01
ADAM_STEP = 10

N_DEV = 8
VMEM_LIMIT = 56 * 1024 * 1024
MESH = pl.DeviceIdType.MESH


def _cparams(sem=None):
    return pltpu.CompilerParams(dimension_semantics=sem, vmem_limit_bytes=VMEM_LIMIT)


def _tile(n, target):
    if n <= target:
        return n
    t = (target // LANES) * LANES
    while t >= LANES:
        if n % t == 0:
            return t
        t -= LANES
    return n


def _sigmoid(v):
    return 1.0 / (1.0 + jnp.exp(-v))


MM_VMEM_BUDGET = 40 * 1024 * 1024


def _rms_apply(x, g):
    return x * lax.rsqrt(jnp.mean(x * x, axis=-1, keepdims=True) + EPS) * g


def _rms_grad(dh, x, g):
    r = lax.rsqrt(jnp.mean(x * x, axis=-1, keepdims=True) + EPS)
    xh = x * r
    dxh = dh * g
    dx = r * (dxh - xh * jnp.mean(dxh * xh, axis=-1, keepdims=True))
    return dx, jnp.sum(dh * xh, axis=0, keepdims=True)


def _mm_tiles(m, n, kdim, a_bytes, b_bytes, io_bytes, whole_rows=False, temps=2):
    def need(tm, tn, tk):
        blocks = 2 * (tm * tk * a_bytes + tk * tn * b_bytes + tm * tn * io_bytes)
        casts = (tm * tk * 2 if a_bytes == 4 else 0) + (tk * tn * 2 if b_bytes == 4 else 0)
        return blocks + casts + temps * tm * tn * 4

    tn = n if whole_rows else _tile(n, 1024)
    while True:
        fits = [(tm * tk, tm, tk) for tm in {_tile(m, c) for c in (1024, 512, 256, 128)}
                for tk in {_tile(kdim, c) for c in (2048, 1024, 512, 256)} if need(tm, tn, tk) <= MM_VMEM_BUDGET]
        if fits:
            _, tm, tk = max(fits)
            return tm, tn, tk
        assert not whole_rows and tn % 256 == 0, "no block size fits the VMEM budget"
        tn //= 2


def _mm(a, b, *, ta=False, tb=False, out_dtype=F32, epi=None, extra=(), gain=None, after=None, out_slots=False,
        a_relu2=False, name):
    extra = tuple(extra) if isinstance(extra, (tuple, list)) else (extra,)
    m = a.shape[1] if ta else a.shape[0]
    kdim = a.shape[0] if ta else a.shape[1]
    n = b.shape[0] if tb else b.shape[1]
    norm = epi in ("res_rms", "rms_bwd")
    io_bytes = (jnp.dtype(out_dtype).itemsize + sum(e.dtype.itemsize for e in extra) + (2 if epi == "res_rms" else 0))
    tm, tn, tk = _mm_tiles(m, n // N_DEV if out_slots else n, kdim, a.dtype.itemsize, b.dtype.itemsize, io_bytes,
                           whole_rows=norm, temps=6 if norm else 2)
    if out_slots:
        assert epi is None and tn == n // N_DEV
    nk = kdim // tk
    a_spec = pl.BlockSpec((tk, tm), lambda i, j, k: (k, i)) if ta else pl.BlockSpec((tm, tk), lambda i, j, k: (i, k))
    b_spec = pl.BlockSpec((tn, tk), lambda i, j, k: (j, k)) if tb else pl.BlockSpec((tk, tn), lambda i, j, k: (k, j))
    o_spec = (pl.BlockSpec((None, tm, tn), lambda i, j, k: (j, i, 0)) if out_slots
              else pl.BlockSpec((tm, tn), lambda i, j, k: (i, j)))
    v_spec = pl.BlockSpec((1, tn), lambda i, j, k: (0, j))
    dims = (((0 if ta else 1,), (1 if tb else 0,)), ((), ()))
    n_extra = len(extra)
    n_in = 2 + n_extra + (gain is not None) + (after is not None)
    n_out = 2 if norm else 1

    def body(*refs):
        a_ref, b_ref = refs[0], refs[1]
        e_refs = refs[2:2 + n_extra]
        g_ref = refs[2 + n_extra] if gain is not None else None
        outs = refs[n_in:n_in + n_out]

        def product():
            av = a_ref[...]
            if a_relu2:
                r = jnp.maximum(av.astype(F32), 0.0)
                av = r * r
            return lax.dot_general(av.astype(BF16), b_ref[...].astype(BF16), dims, preferred_element_type=F32)

        def finish(acc):
            if epi is None:
                outs[0][...] = acc.astype(outs[0].dtype)
            elif epi == "res":
                outs[0][...] = (e_refs[0][...] + acc).astype(outs[0].dtype)
            elif epi == "drelu2":
                outs[0][...] = (acc * (2.0 * jnp.maximum(e_refs[0][...].astype(F32), 0.0))).astype(outs[0].dtype)
            elif epi == "res_rms":
                x1 = e_refs[0][...] + acc
                outs[0][...] = x1
                outs[1][...] = _rms_apply(x1, g_ref[...]).astype(BF16)
            elif epi == "rms_bwd":
                dx, dg = _rms_grad(acc, e_refs[0][...], g_ref[...])
                outs[0][...] = e_refs[1][...] + dx
                i = pl.program_id(0)

                @pl.when(i == 0)
                def _():
                    outs[1][...] = dg

                @pl.when(i > 0)
                def _():
                    outs[1][...] += dg

        if nk == 1:
            finish(product())
            return
        acc_ref = refs[-1]
        k = pl.program_id(2)

        @pl.when(k == 0)
        def _():
            acc_ref[...] = product()

        @pl.when(jnp.logical_and(k > 0, k < nk - 1))
        def _():
            acc_ref[...] += product()

        @pl.when(k == nk - 1)
        def _():
            finish(acc_ref[...] + product())

    in_specs = ([a_spec, b_spec] + [o_spec] * n_extra + ([v_spec] if gain is not None else [])
                + ([pl.BlockSpec(memory_space=pl.ANY)] if after is not None else []))
    if epi == "res_rms":
        out_shape = (jax.ShapeDtypeStruct((m, n), F32), jax.ShapeDtypeStruct((m, n), BF16))
        out_specs = (o_spec, o_spec)
    elif epi == "rms_bwd":
        out_shape = (jax.ShapeDtypeStruct((m, n), F32), jax.ShapeDtypeStruct((1, n), F32))
        out_specs = (o_spec, v_spec)
    else:
        out_shape = jax.ShapeDtypeStruct((N_DEV, m, tn) if out_slots else (m, n), out_dtype)
        out_specs = o_spec
    args = (a, b) + extra + ((gain.reshape(1, n),) if gain is not None else ()) + ((after,) if after is not None else ())
    return pl.pallas_call(
        body, name=name, grid=(m // tm, n // tn, nk), in_specs=in_specs, out_specs=out_specs, out_shape=out_shape,
        scratch_shapes=[pltpu.VMEM((tm, tn), F32)] if nk > 1 else [],
        compiler_params=_cparams(("arbitrary", "arbitrary", "arbitrary")),
    )(*args)


SHIFT = HEAD_DIM


def _pair_blocks(e, o):
    wp = e.shape[-1]
    return e[:, :wp - LANES], e[:, wp - LANES:] + o[:, :LANES], o[:, LANES:]


def _mm_in_pairs(a, wg, name):
    t, kdim = a.shape
    wp = wg.shape[-1]
    ws = wp - SHIFT
    tm = _tile(t, 1024)

    def body(a_ref, e_ref, o_ref, u_ref):
        av = a_ref[...]
        lo, mid, hi = _pair_blocks(e_ref[...], o_ref[...])
        u_ref[:, :wp - LANES] = jnp.dot(av, lo, preferred_element_type=F32).astype(BF16)
        u_ref[:, wp - LANES:wp] = jnp.dot(av, mid, preferred_element_type=F32).astype(BF16)
        u_ref[:, wp:] = jnp.dot(av, hi, preferred_element_type=F32).astype(BF16)

    return pl.pallas_call(
        body, name=name, grid=(N_DEV // 2, t // tm),
        in_specs=[pl.BlockSpec((tm, kdim), lambda p, i: (i, 0)),
                  pl.BlockSpec((None, kdim, wp), lambda p, i: (2 * p, 0, 0)),
                  pl.BlockSpec((None, kdim, wp), lambda p, i: (2 * p + 1, 0, 0))],
        out_specs=pl.BlockSpec((tm, 2 * ws), lambda p, i: (i, p)),
        out_shape=jax.ShapeDtypeStruct((t, N_DEV * ws), BF16), compiler_params=_cparams(("arbitrary", "arbitrary")),
    )(a, wg, wg)


def _mm_din_pairs(du, wg, x, gain, dres, after, name):
    t = du.shape[0]
    _, kdim, wp = wg.shape
    ws = wp - SHIFT
    tm = _tile(t, 512)
    npair = N_DEV // 2
    lanes = (((1,), (1,)), ((), ()))
    extra = [] if after is None else [after]

    def body(d_ref, e_ref, o_ref, x_ref, g_ref, r_ref, *rest):
        dx_ref, dg_ref, acc_ref = rest[-3], rest[-2], rest[-1]
        i, p = pl.program_id(0), pl.program_id(1)
        lo, mid, hi = _pair_blocks(e_ref[...], o_ref[...])
        part = (lax.dot_general(d_ref[:, :wp - LANES], lo, lanes, preferred_element_type=F32)
                + lax.dot_general(d_ref[:, wp - LANES:wp], mid, lanes, preferred_element_type=F32)
                + lax.dot_general(d_ref[:, wp:], hi, lanes, preferred_element_type=F32))

        @pl.when(p == 0)
        def _():
            acc_ref[...] = part

        @pl.when(jnp.logical_and(p > 0, p < npair - 1))
        def _():
            acc_ref[...] += part

        @pl.when(p == npair - 1)
        def _():
            dx, dg = _rms_grad(acc_ref[...] + part, x_ref[...], g_ref[...])
            dx_ref[...] = r_ref[...] + dx

            @pl.when(i == 0)
            def _():
                dg_ref[...] = dg

            @pl.when(i > 0)
            def _():
                dg_ref[...] += dg

    row = pl.BlockSpec((tm, kdim), lambda i, p: (i, 0))
    vec = pl.BlockSpec((1, kdim), lambda i, p: (0, 0))
    return pl.pallas_call(
        body, name=name, grid=(t // tm, npair),
        in_specs=[pl.BlockSpec((tm, 2 * ws), lambda i, p: (i, p)),
                  pl.BlockSpec((None, kdim, wp), lambda i, p: (2 * p, 0, 0)),
                  pl.BlockSpec((None, kdim, wp), lambda i, p: (2 * p + 1, 0, 0)), row, vec, row]
        + [pl.BlockSpec(memory_space=pl.ANY)] * len(extra),
        out_specs=(row, vec),
        out_shape=(jax.ShapeDtypeStruct((t, kdim), F32), jax.ShapeDtypeStruct((1, kdim), F32)),
        scratch_shapes=[pltpu.VMEM((tm, kdim), F32)], compiler_params=_cparams(("arbitrary", "arbitrary")),
    )(du, wg, wg, x, gain.reshape(1, kdim), dres, *extra)


def _mm_gw_in_pairs(h, du, after, name):
    t, kdim = h.shape
    ws = du.shape[1] // N_DEV
    wp = ws + SHIFT
    tm = _tile(kdim, 512)
    rows = (((0,), (0,)), ((), ()))
    extra = [] if after is None else [after]

    def body(h_ref, d_ref, *rest):
        g_ref = rest[-1]
        g = lax.dot_general(h_ref[...], d_ref[...], rows, preferred_element_type=F32)
        g_ref[0] = g[:, :wp].astype(BF16)
        g_ref[1] = g[:, wp - LANES:].astype(BF16)

    return pl.pallas_call(
        body, name=name, grid=(N_DEV // 2, kdim // tm),
        in_specs=[pl.BlockSpec((t, tm), lambda p, i: (0, i)), pl.BlockSpec((t, 2 * ws), lambda p, i: (0, p))]
        + [pl.BlockSpec(memory_space=pl.ANY)] * len(extra),
        out_specs=pl.BlockSpec((2, tm, wp), lambda p, i: (p, i, 0)),
        out_shape=jax.ShapeDtypeStruct((N_DEV, kdim, wp), BF16), compiler_params=_cparams(("arbitrary", "arbitrary")),
    )(h, du, *extra)


ROW_BLOCK_BUDGET = 24 * 1024 * 1024


def _rows(t, row_bytes):
    rows = t
    while rows > 8 and (2 * rows * row_bytes > ROW_BLOCK_BUDGET or t % rows):
        rows //= 2
    return rows


def _rms_fwd(x, g, name):
    t, d = x.shape
    ROWS = _rows(t, 6 * d)

    def body(x_ref, g_ref, h_ref):
        h_ref[...] = _rms_apply(x_ref[...], g_ref[...]).astype(BF16)

    return pl.pallas_call(
        body, name=name, grid=(t // ROWS,),
        in_specs=[pl.BlockSpec((ROWS, d), lambda i: (i, 0)), pl.BlockSpec((1, d), lambda i: (0, 0))],
        out_specs=pl.BlockSpec((ROWS, d), lambda i: (i, 0)),
        out_shape=jax.ShapeDtypeStruct((t, d), BF16), compiler_params=_cparams(("arbitrary",)),
    )(x, g.reshape(1, d))


def _gate_fwd(u, yc, ya, gate_col, name):
    t, d = yc.shape
    td = math.gcd(_tile(d, 512), gate_col)
    nd = d // td
    c0 = gate_col // td
    ROWS = _rows(t, 14 * td)

    def body(gc_ref, ga_ref, yc_ref, ya_ref, m_ref):
        gc = _sigmoid(gc_ref[...].astype(F32))
        ga = _sigmoid(ga_ref[...].astype(F32))
        m_ref[...] = (gc * yc_ref[...] + ga * ya_ref[...]).astype(BF16)

    blk = pl.BlockSpec((ROWS, td), lambda i, j: (i, j))
    return pl.pallas_call(
        body, name=name, grid=(t // ROWS, nd),
        in_specs=[pl.BlockSpec((ROWS, td), lambda i, j: (i, c0 + j)),
                  pl.BlockSpec((ROWS, td), lambda i, j: (i, c0 + nd + j)), blk, blk],
        out_specs=blk, out_shape=jax.ShapeDtypeStruct((t, d), BF16),
        compiler_params=_cparams(("arbitrary", "arbitrary")),
    )(u, u, yc, ya)


def _gate_bwd(dm, u, yc, ya, gate_col, name):
    t, d = yc.shape
    td = math.gcd(_tile(d, 512), gate_col)
    nd = d // td
    c0 = gate_col // td
    ROWS = _rows(t, 24 * td)

    def body(dm_ref, gc_ref, ga_ref, yc_ref, ya_ref, dyc_ref, dya_ref, dugc_ref, duga_ref):
        dmv = dm_ref[...]
        gc = _sigmoid(gc_ref[...].astype(F32))
        ga = _sigmoid(ga_ref[...].astype(F32))
        dyc_ref[...] = (dmv * gc).astype(BF16)
        dya_ref[...] = (dmv * ga).astype(BF16)
        dugc_ref[...] = (dmv * yc_ref[...] * gc * (1.0 - gc)).astype(BF16)
        duga_ref[...] = (dmv * ya_ref[...] * ga * (1.0 - ga)).astype(BF16)

    blk = pl.BlockSpec((ROWS, td), lambda i, j: (i, j))
    o = jax.ShapeDtypeStruct((t, d), BF16)
    return pl.pallas_call(
        body, name=name, grid=(t // ROWS, nd),
        in_specs=[blk, pl.BlockSpec((ROWS, td), lambda i, j: (i, c0 + j)),
                  pl.BlockSpec((ROWS, td), lambda i, j: (i, c0 + nd + j)), blk, blk],
        out_specs=(blk, blk, blk, blk), out_shape=(o, o, o, o),
        compiler_params=_cparams(("arbitrary", "arbitrary")),
    )(dm, u, u, yc, ya)


def _loss_and_grad(y, target, name):
    t, d = y.shape
    ROWS = _rows(t, 12 * d)
    n = t // ROWS

    def body(y_ref, t_ref, loss_ref, dy_ref, acc_ref):
        i = pl.program_id(0)

        @pl.when(i == 0)
        def _():
            acc_ref[...] = jnp.zeros_like(acc_ref)

        diff = y_ref[...] - t_ref[...]
        dy_ref[...] = diff * (1.0 / d)
        acc_ref[...] += jnp.sum(diff * diff, axis=0, keepdims=True)

        @pl.when(i == n - 1)
        def _():
            loss_ref[...] = jnp.sum(acc_ref[...], axis=-1, keepdims=True) * (0.5 / d)

    row = pl.BlockSpec((ROWS, d), lambda i: (i, 0))
    return pl.pallas_call(
        body, name=name, grid=(n,), in_specs=[row, row],
        out_specs=(pl.BlockSpec((1, 1), lambda i: (0, 0)), row),
        out_shape=(jax.ShapeDtypeStruct((1, 1), F32), jax.ShapeDtypeStruct((t, d), F32)),
        scratch_shapes=[pltpu.VMEM((1, d), F32)], compiler_params=_cparams(("arbitrary",)),
    )(y, target)


HALO = 32


def _conv_fwd(u, w, b, cdim, name):
    t = u.shape[0]
    ncb = cdim // LANES
    nt = t // BLOCK

    def body(a_ref, g_ref, w_ref, b_ref, zc_ref, zpad):
        zpad[0:HALO, :] = jnp.zeros((HALO, LANES), F32)
        zpad[HALO:HALO + t, :] = a_ref[...].astype(F32) * _sigmoid(g_ref[...].astype(F32))
        wv = w_ref[...]
        bv = b_ref[...]

        def tile(i, carry):
            r0 = pl.multiple_of(i * BLOCK, BLOCK)
            win = zpad[pl.ds(r0, BLOCK + HALO), :]
            acc = jnp.zeros((BLOCK, LANES), F32) + bv
            for b in range(SUBLANES):
                sh = win if b == 0 else pltpu.roll(win, b, 0)
                for a in range(HALO // SUBLANES):
                    j = CONV_WIDTH - 1 - (SUBLANES * a + b)
                    if j >= 0:
                        lo = HALO - SUBLANES * a
                        acc = acc + wv[j:j + 1, :] * sh[lo:lo + BLOCK, :]
            zc_ref[pl.ds(r0, BLOCK), :] = acc
            return carry

        lax.fori_loop(0, nt, tile, 0)

    col = lambda off: pl.BlockSpec((t, LANES), lambda c: (0, off + c))
    return pl.pallas_call(
        body, name=name, grid=(ncb,),
        in_specs=[col(0), col(ncb), pl.BlockSpec((CONV_TAPS_PADDED, LANES), lambda c: (0, c)),
                  pl.BlockSpec((1, LANES), lambda c: (0, c))],
        out_specs=pl.BlockSpec((t, LANES), lambda c: (0, c)),
        out_shape=jax.ShapeDtypeStruct((t, cdim), F32),
        scratch_shapes=[pltpu.VMEM((t + HALO, LANES), F32)], compiler_params=_cparams(("arbitrary",)),
    )(u, u, w, b.reshape(1, cdim))


def _conv_bwd(dzc, u, w, cdim, name):
    t = u.shape[0]
    ncb = cdim // LANES
    nt = t // BLOCK
    win_rows = BLOCK + HALO

    def body(dzc_ref, a_ref, g_ref, w_ref, da_ref, dg_ref, dw_ref, db_ref, zpad, dpad):
        av = a_ref[...].astype(F32)
        sg = _sigmoid(g_ref[...].astype(F32))
        zpad[0:HALO, :] = jnp.zeros((HALO, LANES), F32)
        zpad[HALO:HALO + t, :] = av * sg
        dpad[0:t, :] = dzc_ref[...]
        dpad[t:t + HALO, :] = jnp.zeros((HALO, LANES), F32)
        dw_ref[...] = jnp.zeros_like(dw_ref)
        db_ref[...] = jnp.sum(dzc_ref[...], axis=0, keepdims=True)
        wv = w_ref[...]

        def tile(i, carry):
            r0 = pl.multiple_of(i * BLOCK, BLOCK)
            zwin = zpad[pl.ds(r0, win_rows), :]
            dwin = dpad[pl.ds(r0, win_rows), :]
            dcur = dwin[0:BLOCK, :]
            dz = jnp.zeros((BLOCK, LANES), F32)
            for b in range(SUBLANES):
                zs = zwin if b == 0 else pltpu.roll(zwin, b, 0)
                ds = dwin if b == 0 else pltpu.roll(dwin, win_rows - b, 0)
                for a in range(HALO // SUBLANES):
                    j = CONV_WIDTH - 1 - (SUBLANES * a + b)
                    if j >= 0:
                        lo = HALO - SUBLANES * a
                        dw_ref[j:j + 1, :] += jnp.sum(dcur * zs[lo:lo + BLOCK, :], axis=0, keepdims=True)
                        dz = dz + wv[j:j + 1, :] * ds[SUBLANES * a:SUBLANES * a + BLOCK, :]
            ac = a_ref[pl.ds(r0, BLOCK), :].astype(F32)
            sc = _sigmoid(g_ref[pl.ds(r0, BLOCK), :].astype(F32))
            da_ref[pl.ds(r0, BLOCK), :] = (dz * sc).astype(BF16)
            dg_ref[pl.ds(r0, BLOCK), :] = (dz * ac * sc * (1.0 - sc)).astype(BF16)
            return carry

        lax.fori_loop(0, nt, tile, 0)

    col = lambda off: pl.BlockSpec((t, LANES), lambda c: (0, off + c))
    wspec = pl.BlockSpec((CONV_TAPS_PADDED, LANES), lambda c: (0, c))
    o = jax.ShapeDtypeStruct((t, cdim), BF16)
    return pl.pallas_call(
        body, name=name, grid=(ncb,), in_specs=[col(0), col(0), col(ncb), wspec],
        out_specs=(col(0), col(0), wspec, pl.BlockSpec((1, LANES), lambda c: (0, c))),
        out_shape=(o, o, jax.ShapeDtypeStruct((CONV_TAPS_PADDED, cdim), F32), jax.ShapeDtypeStruct((1, cdim), F32)),
        scratch_shapes=[pltpu.VMEM((t + HALO, LANES), F32), pltpu.VMEM((t + HALO, LANES), F32)],
        compiler_params=_cparams(("arbitrary",)),
    )(dzc, u, u, w)


def _ln_swish_fwd(zc, g, b, name):
    t, c = zc.shape
    ROWS = _rows(t, 6 * c)

    def body(z_ref, g_ref, b_ref, o_ref):
        z = z_ref[...]
        mu = jnp.mean(z, axis=-1, keepdims=True)
        zc_ = z - mu
        zn = zc_ * lax.rsqrt(jnp.mean(zc_ * zc_, axis=-1, keepdims=True) + EPS)
        y = zn * g_ref[...] + b_ref[...]
        o_ref[...] = (y * _sigmoid(y)).astype(BF16)

    row = pl.BlockSpec((ROWS, c), lambda i: (i, 0))
    vec = pl.BlockSpec((1, c), lambda i: (0, 0))
    return pl.pallas_call(
        body, name=name, grid=(t // ROWS,), in_specs=[row, vec, vec], out_specs=row,
        out_shape=jax.ShapeDtypeStruct((t, c), BF16), compiler_params=_cparams(("arbitrary",)),
    )(zc, g.reshape(1, c), b.reshape(1, c))


def _ln_swish_bwd(dzs, zc, g, b, name):
    t, c = zc.shape
    ROWS = _rows(t, 12 * c)

    def body(d_ref, z_ref, g_ref, b_ref, dz_ref, dg_ref, db_ref):
        @pl.when(pl.program_id(0) == 0)
        def _():
            dg_ref[...] = jnp.zeros_like(dg_ref)
            db_ref[...] = jnp.zeros_like(db_ref)

        z = z_ref[...]
        mu = jnp.mean(z, axis=-1, keepdims=True)
        zc_ = z - mu
        rstd = lax.rsqrt(jnp.mean(zc_ * zc_, axis=-1, keepdims=True) + EPS)
        zn = zc_ * rstd
        y = zn * g_ref[...] + b_ref[...]
        sg = _sigmoid(y)
        dy = d_ref[...] * (sg * (1.0 + y * (1.0 - sg)))
        dg_ref[...] += jnp.sum(dy * zn, axis=0, keepdims=True)
        db_ref[...] += jnp.sum(dy, axis=0, keepdims=True)
        dzn = dy * g_ref[...]
        dz_ref[...] = rstd * (dzn - jnp.mean(dzn, axis=-1, keepdims=True)
                              - zn * jnp.mean(dzn * zn, axis=-1, keepdims=True))

    row = pl.BlockSpec((ROWS, c), lambda i: (i, 0))
    vec = pl.BlockSpec((1, c), lambda i: (0, 0))
    v = jax.ShapeDtypeStruct((1, c), F32)
    return pl.pallas_call(
        body, name=name, grid=(t // ROWS,), in_specs=[row, row, vec, vec], out_specs=(row, vec, vec),
        out_shape=(jax.ShapeDtypeStruct((t, c), F32), v, v), compiler_params=_cparams(("arbitrary",)),
    )(dzs, zc, g.reshape(1, c), b.reshape(1, c))


def _bucket_table():
    qi = np.arange(BLOCK)[:, None]
    kj = np.arange(2 * BLOCK)[None, :]
    off = qi + BLOCK - kj
    band = (off >= 0) & (off <= SUB_WINDOW)
    max_exact = NUM_BUCKETS // 2
    out = []
    for d in DILATIONS:
        dist = (np.clip(off, 0, SUB_WINDOW) * d).astype(np.int32)
        nf = np.maximum(dist, 1).astype(np.float32)
        large = max_exact + (np.log(nf / np.float32(max_exact)) / np.float32(math.log(MAX_REL_DISTANCE / max_exact))
                             * np.float32(NUM_BUCKETS - max_exact)).astype(np.int32)
        large = np.minimum(large, NUM_BUCKETS - 1)
        bucket = np.where(dist < max_exact, dist, large)
        out.append(np.where(band, bucket, -1))
    return np.stack(out).astype(np.int32)


def _bias_expand(rel_bias, buckets, hpg, name):
    nh = N_GROUPS * hpg

    def body(rb_ref, bk_ref, o_ref):
        h = pl.program_id(0)
        bk = bk_ref[0]
        acc = jnp.full((BLOCK, 2 * BLOCK), NEG_INF, F32)
        for bb in range(NUM_BUCKETS):
            acc = jnp.where(bk == bb, rb_ref[bb, h], acc)
        o_ref[0] = acc

    return pl.pallas_call(
        body, name=name, grid=(nh,),
        in_specs=[pl.BlockSpec(memory_space=pltpu.SMEM),
                  pl.BlockSpec((1, BLOCK, 2 * BLOCK), lambda h: (h // hpg, 0, 0))],
        out_specs=pl.BlockSpec((1, BLOCK, 2 * BLOCK), lambda h: (h, 0, 0)),
        out_shape=jax.ShapeDtypeStruct((nh, BLOCK, 2 * BLOCK), F32), compiler_params=_cparams(("arbitrary",)),
    )(rel_bias, buckets)


def _bias_reduce(ds_sum, buckets, hpg, name):
    nh = N_GROUPS * hpg

    def body(ds_ref, bk_ref, o_ref):
        bk = bk_ref[0]
        dsv = ds_ref[0]
        lane = lax.broadcasted_iota(jnp.int32, (1, LANES), 1)
        row = jnp.zeros((1, LANES), F32)
        for bb in range(NUM_BUCKETS):
            tot = jnp.sum(jnp.sum(jnp.where(bk == bb, dsv, 0.0), axis=-1, keepdims=True), axis=0, keepdims=True)
            row = jnp.where(lane == bb, tot, row)
        o_ref[0] = row

    return pl.pallas_call(
        body, name=name, grid=(nh,),
        in_specs=[pl.BlockSpec((1, BLOCK, 2 * BLOCK), lambda h: (h, 0, 0)),
                  pl.BlockSpec((1, BLOCK, 2 * BLOCK), lambda h: (h // hpg, 0, 0))],
        out_specs=pl.BlockSpec((1, 1, LANES), lambda h: (h, 0, 0)),
        out_shape=jax.ShapeDtypeStruct((nh, 1, LANES), F32), compiler_params=_cparams(("arbitrary",)),
    )(ds_sum, buckets)


def _chunk_rows(c, d, nb):
    r, n = c // nb, c % nb
    if d == 1:
        return pl.ds(c * BLOCK, BLOCK)
    return pl.ds(r + n * BLOCK * d, BLOCK, stride=d)


def _segment_ones():
    i = lax.broadcasted_iota(jnp.int32, (LANES, LANES), 0) // HEAD_DIM
    j = lax.broadcasted_iota(jnp.int32, (LANES, LANES), 1) // HEAD_DIM
    return (i == j).astype(BF16)


def _segment_sum(v, seg):
    hi = v.astype(BF16)
    lo = (v - hi.astype(F32)).astype(BF16)
    return jnp.dot(hi, seg, preferred_element_type=F32) + jnp.dot(lo, seg, preferred_element_type=F32)


def _head_mean(v, seg):
    return _segment_sum(v, seg) * (1.0 / HEAD_DIM)


def _attn_fwd(u, qg2, kg2, bias, gi, cols, hpg, name):
    t = u.shape[0]
    d = DILATIONS[gi]
    nchunk = t // BLOCK
    nb = (t // d) // BLOCK
    hp = hpg // 2
    qc0, kc0, vc0 = [(c + gi * hpg * HEAD_DIM) // LANES for c in cols]
    contract_lanes = (((1,), (1,)), ((), ()))

    def body(q_ref, k_ref, v_ref, qg_ref, kg_ref, bias_ref, o_ref, lse_ref, qd, kd, vd, od, ld, sbuf):
        seg = _segment_ones()
        lane = lax.broadcasted_iota(jnp.int32, (1, LANES), 1)
        qg = qg_ref[...] * (HEAD_DIM ** -0.5)
        kg = kg_ref[...]
        kd[0:BLOCK, :] = jnp.zeros((BLOCK, LANES), BF16)
        vd[0:BLOCK, :] = jnp.zeros((BLOCK, LANES), BF16)
        od[...] = q_ref[...].astype(F32)
        ld[...] = k_ref[...].astype(F32)
        for c in range(nchunk):
            rows = _chunk_rows(c, d, nb)
            qv = od[rows, :]
            kv = ld[rows, :]
            qd[c * BLOCK:(c + 1) * BLOCK, :] = (qv * lax.rsqrt(_head_mean(qv * qv, seg) + EPS) * qg).astype(BF16)
            kd[(c + 1) * BLOCK:(c + 2) * BLOCK, :] = (kv * lax.rsqrt(_head_mean(kv * kv, seg) + EPS) * kg).astype(BF16)
        od[...] = v_ref[...].astype(F32)
        for c in range(nchunk):
            vd[(c + 1) * BLOCK:(c + 2) * BLOCK, :] = od[_chunk_rows(c, d, nb), :].astype(BF16)

        col = lax.broadcasted_iota(jnp.int32, (BLOCK, 2 * BLOCK), 1)
        for j in range(2):
            mj = jnp.logical_and(lane >= j * HEAD_DIM, lane < (j + 1) * HEAD_DIM)
            for c in range(nchunk):
                kw = kd[c * BLOCK:(c + 2) * BLOCK, :]
                kj = jnp.where(mj, kw, jnp.zeros_like(kw))
                s = lax.dot_general(qd[c * BLOCK:(c + 1) * BLOCK, :], kj, contract_lanes,
                                    preferred_element_type=F32) + bias_ref[j]
                if c % nb == 0:
                    s = jnp.where(col < BLOCK, NEG_INF, s)
                sbuf[c] = s
            for c in range(nchunk):
                rows = slice(c * BLOCK, (c + 1) * BLOCK)
                s = sbuf[c]
                mx = jnp.max(s, axis=-1, keepdims=True)
                p = jnp.exp(s - mx).astype(BF16)
                vw = vd[c * BLOCK:(c + 2) * BLOCK, :]
                oj = jnp.dot(p, jnp.where(mj, vw, jnp.ones_like(vw)), preferred_element_type=F32)
                l = pltpu.roll(oj, HEAD_DIM, 1)
                on = oj / l
                ls = mx + jnp.log(l)
                if j == 0:
                    od[rows, :] = on
                    ld[rows, :] = ls
                else:
                    od[rows, :] = jnp.where(mj, on, od[rows, :])
                    ld[rows, :] = jnp.where(mj, ls, ld[rows, :])

        for c in range(nchunk):
            rows = _chunk_rows(c, d, nb)
            o_ref[rows, :] = od[c * BLOCK:(c + 1) * BLOCK, :]
            lse_ref[rows, :] = ld[c * BLOCK:(c + 1) * BLOCK, :]

    ucol = lambda c0: pl.BlockSpec((t, LANES), lambda h: (0, c0 + h))
    vec = pl.BlockSpec((1, LANES), lambda h: (0, 0))
    oblk = pl.BlockSpec((t, LANES), lambda h: (0, h))
    osh = jax.ShapeDtypeStruct((t, hpg * HEAD_DIM), F32)
    return pl.pallas_call(
        body, name=name, grid=(hp,),
        in_specs=[ucol(qc0), ucol(kc0), ucol(vc0), vec, vec,
                  pl.BlockSpec((2, BLOCK, 2 * BLOCK), lambda h: (gi * hp + h, 0, 0))],
        out_specs=(oblk, oblk), out_shape=(osh, osh),
        scratch_shapes=[pltpu.VMEM((t, LANES), BF16), pltpu.VMEM((t + BLOCK, LANES), BF16),
                        pltpu.VMEM((t + BLOCK, LANES), BF16), pltpu.VMEM((t, LANES), F32), pltpu.VMEM((t, LANES), F32),
                        pltpu.VMEM((nchunk, BLOCK, 2 * BLOCK), F32)],
        compiler_params=_cparams(("arbitrary",)),
    )(u, u, u, qg2, kg2, bias)


def _attn_bwd(u, do_g, dd_g, lse_g, qg2, kg2, bias, ds_in, du_in, gi, cols, hpg, name):
    t = u.shape[0]
    d = DILATIONS[gi]
    nchunk = t // BLOCK
    nb = (t // d) // BLOCK
    hp = hpg // 2
    qc0, kc0, vc0 = [(c + gi * hpg * HEAD_DIM) // LANES for c in cols]
    contract_lanes = (((1,), (1,)), ((), ()))
    contract_rows = (((0,), (0,)), ((), ()))
    qscale = HEAD_DIM ** -0.5

    def body(q_ref, k_ref, v_ref, do_ref, dd_ref, lse_ref, qg_ref, kg_ref, bias_ref, dsin_ref, _du_in,
             dgq_ref, dgk_ref, dsout_ref, du_ref,
             qd, kd, vd, dod, ddd, ld, dqd, dkd, dvd, dsacc, pbuf, dsbuf, qs, ks, qst, kst, vst, out_sems):
        h = pl.program_id(0)

        def flush(step):
            return [pltpu.make_async_copy(
                st, du_ref.at[:, pl.ds(pl.multiple_of((c0 + step) * LANES, LANES), LANES)], out_sems.at[i])
                for i, (st, c0) in enumerate(((qst, qc0), (kst, kc0), (vst, vc0)))]

        seg = _segment_ones()
        lane = lax.broadcasted_iota(jnp.int32, (1, LANES), 1)
        qg = qg_ref[...] * qscale
        kg = kg_ref[...]
        kd[0:BLOCK, :] = jnp.zeros((BLOCK, LANES), BF16)
        vd[0:BLOCK, :] = jnp.zeros((BLOCK, LANES), BF16)
        dsacc[...] = jnp.zeros_like(dsacc)
        qs[...] = q_ref[...].astype(F32)
        ks[...] = k_ref[...].astype(F32)
        dqd[...] = v_ref[...].astype(F32)
        for c in range(nchunk):
            rows = _chunk_rows(c, d, nb)
            qv = qs[rows, :]
            kv = ks[rows, :]
            qd[c * BLOCK:(c + 1) * BLOCK, :] = (qv * lax.rsqrt(_head_mean(qv * qv, seg) + EPS) * qg).astype(BF16)
            kd[(c + 1) * BLOCK:(c + 2) * BLOCK, :] = (kv * lax.rsqrt(_head_mean(kv * kv, seg) + EPS) * kg).astype(BF16)
            vd[(c + 1) * BLOCK:(c + 2) * BLOCK, :] = dqd[rows, :].astype(BF16)
            dod[c * BLOCK:(c + 1) * BLOCK, :] = do_ref[rows, :].astype(BF16)
            ddd[c * BLOCK:(c + 1) * BLOCK, :] = dd_ref[rows, :]
            ld[c * BLOCK:(c + 1) * BLOCK, :] = lse_ref[rows, :]

        col = lax.broadcasted_iota(jnp.int32, (BLOCK, 2 * BLOCK), 1)
        for j in range(2):
            mj = jnp.logical_and(lane >= j * HEAD_DIM, lane < (j + 1) * HEAD_DIM)
            first = lane == j * HEAD_DIM
            for c in range(nchunk):
                rows = slice(c * BLOCK, (c + 1) * BLOCK)
                kw = kd[c * BLOCK:(c + 2) * BLOCK, :]
                vw = vd[c * BLOCK:(c + 2) * BLOCK, :]
                kj = jnp.where(mj, kw, jnp.zeros_like(kw))
                vj = jnp.where(mj, vw, jnp.zeros_like(vw))
                s = lax.dot_general(qd[rows, :], kj, contract_lanes, preferred_element_type=F32) + bias_ref[j]
                if c % nb == 0:
                    s = jnp.where(col < BLOCK, NEG_INF, s)
                dp = lax.dot_general(dod[rows, :], vj, contract_lanes, preferred_element_type=F32)
                lse_j = jnp.sum(jnp.where(first, ld[rows, :], 0.0), axis=-1, keepdims=True)
                dd_j = jnp.sum(jnp.where(first, ddd[rows, :], 0.0), axis=-1, keepdims=True)
                p = jnp.exp(s - lse_j)
                ds = p * (dp + dd_j)
                dsacc[j] += ds
                pbuf[j, c] = p.astype(BF16)
                dsbuf[j, c] = ds.astype(BF16)
        for c in range(nchunk):
            rows = slice(c * BLOCK, (c + 1) * BLOCK)
            has_next = c + 1 < nchunk and (c + 1) % nb != 0
            kw = kd[c * BLOCK:(c + 2) * BLOCK, :]
            dq = jnp.zeros((BLOCK, LANES), F32)
            dk = jnp.zeros((BLOCK, LANES), F32)
            dv = jnp.zeros((BLOCK, LANES), F32)
            both = slice(c * BLOCK, (c + 2) * BLOCK) if has_next else rows
            for j in range(2):
                mj = jnp.logical_and(lane >= j * HEAD_DIM, lane < (j + 1) * HEAD_DIM)
                dq = dq + jnp.dot(dsbuf[j, c], jnp.where(mj, kw, jnp.zeros_like(kw)), preferred_element_type=F32)
                dsk = dsbuf[j, c, :, BLOCK:]
                pk = pbuf[j, c, :, BLOCK:]
                if has_next:
                    dsk = jnp.concatenate([dsk, dsbuf[j, c + 1, :, :BLOCK]], axis=0)
                    pk = jnp.concatenate([pk, pbuf[j, c + 1, :, :BLOCK]], axis=0)
                qq = qd[both, :]
                dd = dod[both, :]
                dk = dk + lax.dot_general(dsk, jnp.where(mj, qq, jnp.zeros_like(qq)), contract_rows,
                                          preferred_element_type=F32)
                dv = dv + lax.dot_general(pk, jnp.where(mj, dd, jnp.zeros_like(dd)), contract_rows,
                                          preferred_element_type=F32)
            dqd[rows, :] = dq
            dkd[rows, :] = dk
            dvd[rows, :] = dv

        dsout_ref[...] = dsin_ref[...] + dsacc[...]

        dgq = jnp.zeros((1, LANES), F32)
        dgk = jnp.zeros((1, LANES), F32)
        for c in range(nchunk):
            rows = _chunk_rows(c, d, nb)
            qv = qs[rows, :]
            rq = lax.rsqrt(_head_mean(qv * qv, seg) + EPS)
            qh = qv * rq
            dy = dqd[c * BLOCK:(c + 1) * BLOCK, :]
            dgq = dgq + jnp.sum(dy * qh, axis=0, keepdims=True) * qscale
            dxh = dy * qg
            ddd[rows, :] = rq * (dxh - qh * _head_mean(dxh * qh, seg))
            kv = ks[rows, :]
            rk = lax.rsqrt(_head_mean(kv * kv, seg) + EPS)
            kh = kv * rk
            dy = dkd[c * BLOCK:(c + 1) * BLOCK, :]
            dgk = dgk + jnp.sum(dy * kh, axis=0, keepdims=True)
            dxh = dy * kg
            ld[rows, :] = rk * (dxh - kh * _head_mean(dxh * kh, seg))
        @pl.when(h > 0)
        def _():
            for cp in flush(h - 1):
                cp.wait()

        qst[...] = ddd[...].astype(BF16)
        kst[...] = ld[...].astype(BF16)
        for c in range(nchunk):
            ddd[_chunk_rows(c, d, nb), :] = dvd[c * BLOCK:(c + 1) * BLOCK, :]
        vst[...] = ddd[...].astype(BF16)
        for cp in flush(h):
            cp.start()

        @pl.when(h == hp - 1)
        def _():
            for cp in flush(h):
                cp.wait()

        dgq_ref[0] = dgq
        dgk_ref[0] = dgk

    ucol = lambda c0: pl.BlockSpec((t, LANES), lambda h: (0, c0 + h))
    vec = pl.BlockSpec((1, LANES), lambda h: (0, 0))
    oblk = pl.BlockSpec((t, LANES), lambda h: (0, h))
    bblk = pl.BlockSpec((2, BLOCK, 2 * BLOCK), lambda h: (gi * hp + h, 0, 0))
    gblk = pl.BlockSpec((1, 1, LANES), lambda h: (h, 0, 0))
    gsh = jax.ShapeDtypeStruct((hp, 1, LANES), F32)
    hbm = pl.BlockSpec(memory_space=pl.ANY)
    return pl.pallas_call(
        body, name=name, grid=(hp,),
        in_specs=[ucol(qc0), ucol(kc0), ucol(vc0), oblk, oblk, oblk, vec, vec, bblk, bblk, hbm],
        out_specs=(gblk, gblk, bblk, hbm),
        out_shape=(gsh, gsh, jax.ShapeDtypeStruct(ds_in.shape, F32), jax.ShapeDtypeStruct(du_in.shape, BF16)),
        input_output_aliases={9: 2, 10: 3},
        scratch_shapes=[pltpu.VMEM((t, LANES), BF16), pltpu.VMEM((t + BLOCK, LANES), BF16),
                        pltpu.VMEM((t + BLOCK, LANES), BF16), pltpu.VMEM((t, LANES), BF16),
                        pltpu.VMEM((t, LANES), F32), pltpu.VMEM((t, LANES), F32), pltpu.VMEM((t, LANES), F32),
                        pltpu.VMEM((t, LANES), F32), pltpu.VMEM((t, LANES), F32),
                        pltpu.VMEM((2, BLOCK, 2 * BLOCK), F32), pltpu.VMEM((2, nchunk, BLOCK, 2 * BLOCK), BF16),
                        pltpu.VMEM((2, nchunk, BLOCK, 2 * BLOCK), BF16), pltpu.VMEM((t, LANES), F32),
                        pltpu.VMEM((t, LANES), F32), pltpu.VMEM((t, LANES), BF16), pltpu.VMEM((t, LANES), BF16),
                        pltpu.VMEM((t, LANES), BF16), pltpu.SemaphoreType.DMA((3,))],
        compiler_params=_cparams(("arbitrary",)),
    )(u, u, u, do_g, dd_g, lse_g, qg2, kg2, bias, ds_in, du_in)


def _group_weights(l0, l1, l2):
    mx = jnp.maximum(jnp.maximum(l0, l1), l2)
    e0, e1, e2 = jnp.exp(l0 - mx), jnp.exp(l1 - mx), jnp.exp(l2 - mx)
    inv = 1.0 / (e0 + e1 + e2)
    return e0 * inv, e1 * inv, e2 * inv


def _combine_fwd(os_, lses, name):
    t, ao = os_[0].shape
    ROWS = _rows(t, 26 * ao)

    def body(o0, o1, o2, l0, l1, l2, o_ref):
        w0, w1, w2 = _group_weights(l0[...], l1[...], l2[...])
        o_ref[...] = (w0 * o0[...] + w1 * o1[...] + w2 * o2[...]).astype(BF16)

    row = pl.BlockSpec((ROWS, ao), lambda i: (i, 0))
    return pl.pallas_call(
        body, name=name, grid=(t // ROWS,), in_specs=[row] * 6, out_specs=row,
        out_shape=jax.ShapeDtypeStruct((t, ao), BF16), compiler_params=_cparams(("arbitrary",)),
    )(*os_, *lses)


def _combine_bwd(do, os_, lses, name):
    t, ao = do.shape
    idx = np.arange(ao) // HEAD_DIM
    seg = jnp.asarray((idx[:, None] == idx[None, :]).astype(np.float32), dtype=BF16)
    ROWS = _rows(t, 52 * ao)

    def body(do_ref, o0, o1, o2, l0, l1, l2, seg_ref, g0, g1, g2, d0, d1, d2):
        w0, w1, w2 = _group_weights(l0[...], l1[...], l2[...])
        dov = do_ref[...]
        o = w0 * o0[...] + w1 * o1[...] + w2 * o2[...]
        sd = _segment_sum(dov * o, seg_ref[...])
        for w, gref, dref in ((w0, g0, d0), (w1, g1, d1), (w2, g2, d2)):
            gref[...] = w * dov
            dref[...] = -(w * sd)

    row = pl.BlockSpec((ROWS, ao), lambda i: (i, 0))
    sh = jax.ShapeDtypeStruct((t, ao), F32)
    outs = pl.pallas_call(
        body, name=name, grid=(t // ROWS,), in_specs=[row] * 7 + [pl.BlockSpec((ao, ao), lambda i: (0, 0))],
        out_specs=(row,) * 6, out_shape=(sh,) * 6, compiler_params=_cparams(("arbitrary",)),
    )(do, *os_, *lses, seg)
    return outs[:3], outs[3:]


def _adamw(w, g, m, v, name):
    shape = w.shape
    cols = shape[-1]
    rows = int(np.prod(shape[:-1]))
    tr = rows if rows <= 512 else _tile_rows(rows)
    c1 = 1.0 - ADAM_B1 ** ADAM_STEP
    c2 = 1.0 - ADAM_B2 ** ADAM_STEP

    def body(w_ref, g_ref, m_ref, v_ref, d_ref, nm_ref, nv_ref):
        gv = g_ref[...]
        mn = ADAM_B1 * m_ref[...] + (1.0 - ADAM_B1) * gv
        vn = ADAM_B2 * v_ref[...] + (1.0 - ADAM_B2) * (gv * gv)
        nm_ref[...] = mn
        nv_ref[...] = vn
        d_ref[...] = -ADAM_LR * ((mn / c1) / (jnp.sqrt(vn / c2) + ADAM_EPS) + ADAM_WD * w_ref[...])

    blk = pl.BlockSpec((tr, cols), lambda i: (i, 0))
    sh = jax.ShapeDtypeStruct((rows, cols), F32)
    outs = pl.pallas_call(
        body, name=name, grid=(rows // tr,), in_specs=[blk] * 4, out_specs=(blk,) * 3, out_shape=(sh,) * 3,
        compiler_params=_cparams(("arbitrary",)),
    )(*[a.reshape(rows, cols) for a in (w, g, m, v)])
    return tuple(o.reshape(shape) for o in outs)


def _tile_rows(rows):
    for t in (512, 256, 128, 64, 32, 16, 8):
        if rows % t == 0:
            return t
    return rows


def _sum_slots(recv, parts, me, layers, l, name):
    _, rows, cols = recv.shape
    tr = rows if rows <= 512 else _tile_rows(rows)

    def body(me_ref, r_ref, own_ref, _, o_ref):
        acc = jnp.zeros(o_ref.shape, F32)
        for s in range(N_DEV):
            acc = acc + jnp.where(me_ref[0] == s, own_ref[...], r_ref[s]).astype(F32)
        o_ref[...] = acc

    return pl.pallas_call(
        body, name=name,
        grid_spec=pltpu.PrefetchScalarGridSpec(
            num_scalar_prefetch=1, grid=(rows // tr,),
            in_specs=[pl.BlockSpec((N_DEV, tr, cols), lambda i, me: (0, i, 0)),
                      pl.BlockSpec((None, tr, cols), lambda i, me: (me[0], i, 0)),
                      pl.BlockSpec(memory_space=pl.ANY)],
            out_specs=pl.BlockSpec((None, tr, cols), lambda i, me: (l, i, 0))),
        out_shape=jax.ShapeDtypeStruct(layers.shape, F32), input_output_aliases={3: 0},
        compiler_params=_cparams(("arbitrary",)),
    )(me.reshape(1), recv, parts, layers)


def _peer(k):
    x, y, c = lax.axis_index("x"), lax.axis_index("y"), lax.axis_index("c")
    return (1 - x if k & 4 else x, 1 - y if k & 2 else y, 1 - c if k & 1 else c)


def _dev_index(p):
    return 4 * p[0] + 2 * p[1] + p[2]


HBM_SPEC = pl.BlockSpec(memory_space=pltpu.HBM)
SEM_SPEC = pl.BlockSpec(memory_space=pltpu.SEMAPHORE)
ANY_SPEC = pl.BlockSpec(memory_space=pl.ANY)
CHIPS = (4, 2, 6)


def _remote(src, dst, send_sem, recv_sem, to):
    return pltpu.make_async_remote_copy(src_ref=src, dst_ref=dst, send_sem=send_sem, recv_sem=recv_sem,
                                        device_id=to, device_id_type=MESH)


def _hbm(a):
    return pltpu.with_memory_space_constraint(a, pltpu.HBM)


def _split_call(body, name, bufs, sems_in, sem_out_sizes, after):
    nb, ns, no = len(bufs), len(sems_in), len(sem_out_sizes)
    extra = [] if after is None else [after]

    def kern(*refs):
        pos = nb + ns + len(extra)
        body(refs[:nb], refs[nb:nb + ns], refs[pos:pos + no])
        token_ref = refs[pos + no + nb]
        token_ref[...] = jnp.zeros_like(token_ref)

    out_shape = (tuple(pltpu.SemaphoreType.DMA((s,)) for s in sem_out_sizes)
                 + tuple(pltpu.HBM(b.shape, b.dtype) for b in bufs) + (jax.ShapeDtypeStruct((8, LANES), F32),))
    res = pl.pallas_call(
        kern, name=name, out_shape=out_shape,
        in_specs=[HBM_SPEC] * nb + [SEM_SPEC] * ns + [ANY_SPEC] * len(extra),
        out_specs=(SEM_SPEC,) * no + (HBM_SPEC,) * nb + (pl.BlockSpec(memory_space=pltpu.VMEM),),
        input_output_aliases={i: no + i for i in range(nb)},
        compiler_params=pltpu.CompilerParams(has_side_effects=pltpu.SideEffectType.DATAFLOW_SIDE_EFFECTING),
    )(*bufs, *sems_in, *extra)
    return res[:no], res[no:no + nb], res[no + nb]


def _gather_start(shards, lands, after, name):
    n = len(shards)

    def body(bufs, _, sems):
        ins, lnd = bufs[:n], bufs[n:]
        d2d_s, d2d_r, ici_s, ici_r = sems
        me = _dev_index(_peer(0))
        for j, k in enumerate(CHIPS):
            for i in range(n):
                _remote(ins[i], lnd[i].at[me], ici_s.at[j], ici_r.at[j], _peer(k)).start()
        for i in range(n):
            _remote(ins[i], lnd[i].at[me], d2d_s.at[0], d2d_r.at[0], _peer(1)).start()

    return _split_call(body, name, [_hbm(a) for a in (*shards, *lands)], [], (1, 1, 3, 3), after)


def _gather_forward(n, bufs, ici_r, after, name):
    def body(refs, sems_in, sems):
        ins, lnd = refs[:n], refs[n:]
        (arrived,) = sems_in
        fwd_s, fwd_r = sems
        for j, k in enumerate(CHIPS):
            blk = _dev_index(_peer(k))
            for i in range(n):
                _remote(ins[i], lnd[i].at[blk], fwd_s.at[j], arrived.at[j], _peer(k)).wait_recv()
            for i in range(n):
                _remote(lnd[i].at[blk], lnd[i].at[blk], fwd_s.at[j], fwd_r.at[j], _peer(1)).start()

    return _split_call(body, name, bufs, [ici_r], (3, 3), after)


def _gather_finish(n, bufs, d2d_s, d2d_r, ici_s, fwd_s, fwd_r, after, name):
    def body(refs, sems_in, _):
        ins, lnd = refs[:n], refs[n:]
        d2d_send, d2d_recv, ici_send, fwd_send, fwd_recv = sems_in
        sib = _peer(1)
        for i in range(n):
            cp = _remote(ins[i], lnd[i].at[_dev_index(sib)], d2d_send.at[0], d2d_recv.at[0], sib)
            cp.wait_send()
            cp.wait_recv()
        for j, k in enumerate(CHIPS):
            passed = _dev_index(_peer(k))
            landed = _dev_index(_peer(k | 1))
            for i in range(n):
                _remote(ins[i], lnd[i].at[passed], ici_send.at[j], fwd_recv.at[j], _peer(k)).wait_send()
                cp = _remote(lnd[i].at[passed], lnd[i].at[landed], fwd_send.at[j], fwd_recv.at[j], sib)
                cp.wait_send()
                cp.wait_recv()

    _, out, token = _split_call(body, name, bufs, [d2d_s, d2d_r, ici_s, fwd_s, fwd_r], (), after)
    return out[n:], token


def _exchange_start(parts, lands, after, name):
    n = len(parts)

    def body(bufs, _, sems):
        src, lnd = bufs[:n], bufs[n:]
        send, recv = sems
        me = _dev_index(_peer(0))
        for k in (4, 5, 2, 3, 6, 7, 1):
            to = _peer(k)
            for i in range(n):
                _remote(src[i].at[_dev_index(to)], lnd[i].at[me], send.at[k - 1], recv.at[k - 1], to).start()

    return _split_call(body, name, [_hbm(a) for a in (*parts, *lands)], [], (7, 7), after)


def _exchange_finish(n, bufs, send, recv, after, name):
    def body(refs, sems_in, _):
        src, lnd = refs[:n], refs[n:]
        send_, recv_ = sems_in
        me = _dev_index(_peer(0))
        for k in range(1, N_DEV):
            frm = _peer(k)
            for i in range(n):
                cp = _remote(src[i].at[me], lnd[i].at[_dev_index(frm)], send_.at[k - 1], recv_.at[k - 1], frm)
                cp.wait_send()
                cp.wait_recv()

    _, out, token = _split_call(body, name, bufs, [send, recv], (), after)
    return out[:n], out[n:], token


def _all_reduce_small(v, after, name):
    rows = v.shape[0]

    def body(v_ref, _, o_ref, buf, send_sems, recv_sems):
        me = _dev_index(_peer(0))
        buf[me] = v_ref[...]
        copies = []
        for k in range(1, N_DEV):
            copies.append(pltpu.make_async_remote_copy(
                src_ref=v_ref, dst_ref=buf.at[me], send_sem=send_sems.at[k - 1], recv_sem=recv_sems.at[k - 1],
                device_id=_peer(k), device_id_type=MESH))
        for cp in copies:
            cp.start()
        for k in range(1, N_DEV):
            pltpu.make_async_remote_copy(
                src_ref=v_ref, dst_ref=buf.at[_dev_index(_peer(k))], send_sem=send_sems.at[k - 1],
                recv_sem=recv_sems.at[k - 1], device_id=_peer(k), device_id_type=MESH).wait_recv()
        for cp in copies:
            cp.wait_send()
        acc = buf[0]
        for s in range(1, N_DEV):
            acc = acc + buf[s]
        o_ref[...] = acc

    vm = pl.BlockSpec(memory_space=pltpu.VMEM)
    return pl.pallas_call(
        body, name=name, in_specs=[vm, pl.BlockSpec(memory_space=pl.ANY)], out_specs=vm,
        out_shape=jax.ShapeDtypeStruct(v.shape, F32),
        scratch_shapes=[pltpu.VMEM((N_DEV, rows, LANES), F32), pltpu.SemaphoreType.DMA((7,)),
                        pltpu.SemaphoreType.DMA((7,))],
    )(v, after)


def _columns(cdim, ao):
    q_col = 2 * cdim
    attn_dim = N_GROUPS * ao
    return (q_col, q_col + attn_dim, q_col + 2 * attn_dim), q_col + 3 * attn_dim


def _layer_fwd(x, h1, sm, bg, get_rest, bias, hpg, next_gain):
    cdim = sm["conv_ln_g"].shape[0]
    ao = hpg * HEAD_DIM
    cols, gate_col = _columns(cdim, ao)
    qg2 = jnp.tile(sm["q_norm_g"], 2).reshape(1, LANES)
    kg2 = jnp.tile(sm["k_norm_g"], 2).reshape(1, LANES)
    u = _mm_in_pairs(h1, bg["w_in"], "mm_in")
    zc = _conv_fwd(u, bg["conv_dw_w"], sm["conv_dw_b"], cdim, "conv_fwd")
    zs = _ln_swish_fwd(zc, sm["conv_ln_g"], sm["conv_ln_b"], "ln_swish_fwd")
    os_, lses = [], []
    for gi in range(N_GROUPS):
        o_g, lse_g = _attn_fwd(u, qg2, kg2, bias, gi, cols, hpg, "attn_fwd_g%d" % gi)
        os_.append(o_g)
        lses.append(lse_g)
    o = _combine_fwd(os_, lses, "combine_fwd")
    bg = {**bg, **get_rest(o)}
    yc = _mm(zs, bg["w_conv_out"], name="mm_conv_out")
    ya = _mm(o, bg["w_attn_out"], name="mm_attn_out")
    mg = _gate_fwd(u, yc, ya, gate_col, "gate_fwd")
    x1, h2 = _mm(mg, bg["w_out"], epi="res_rms", extra=x, gain=sm["norm2_g"], name="mm_out")
    f = _mm(h2, bg["w_ff1"], out_dtype=BF16, name="mm_ff1")
    if next_gain is None:
        x2, h_next = _mm(f, bg["w_ff2"], a_relu2=True, epi="res", extra=x1, name="mm_ff2"), None
    else:
        x2, h_next = _mm(f, bg["w_ff2"], a_relu2=True, epi="res_rms", extra=x1, gain=next_gain, name="mm_ff2")
    saved = dict(x=x, h1=h1, u=u, zc=zc, zs=zs, yc=yc, os=os_, lses=lses, o=o, ya=ya, mg=mg, x1=x1, h2=h2, f=f,
                 qg2=qg2, kg2=kg2)
    return x2, h_next, saved, bg


GRAD_GROUPS = (("w_ff2", "w_ff1"), ("w_out", "w_conv_out", "w_attn_out", "conv_dw_w"), ("w_in",))


def _layer_bwd(dx, s, sm, bg, bias, ds_sum, after, emit):
    cdim = sm["conv_ln_g"].shape[0]
    ao = bg["w_attn_out"].shape[0]
    hpg = ao // HEAD_DIM
    hp = hpg // 2
    cols, gate_col = _columns(cdim, ao)
    g = {}
    df = _mm(dx, bg["w_ff2"], tb=True, epi="drelu2", extra=s["f"], out_dtype=BF16, after=after, name="mm_dff2")
    g["w_ff2"] = _mm(s["f"], dx, ta=True, a_relu2=True, out_dtype=BF16, name="mm_gw_ff2")
    g["w_ff1"] = _mm(s["h2"], df, ta=True, out_dtype=BF16, out_slots=True, name="mm_gw_ff1")
    after = emit(GRAD_GROUPS[0], g)
    dx1, dg2 = _mm(df, bg["w_ff1"], tb=True, epi="rms_bwd", extra=(s["x1"], dx), gain=sm["norm2_g"], after=after,
                   name="mm_dff1")
    g["norm2_g"] = dg2[0]
    dmg = _mm(dx1, bg["w_out"], tb=True, name="mm_dout")
    g["w_out"] = _mm(s["mg"], dx1, ta=True, out_dtype=BF16, name="mm_gw_out")
    dyc, dya, dugc, duga = _gate_bwd(dmg, s["u"], s["yc"], s["ya"], gate_col, "gate_bwd")
    dzs = _mm(dyc, bg["w_conv_out"], tb=True, name="mm_dconv_out")
    g["w_conv_out"] = _mm(s["zs"], dyc, ta=True, out_dtype=BF16, name="mm_gw_conv_out")
    do = _mm(dya, bg["w_attn_out"], tb=True, name="mm_dattn_out")
    g["w_attn_out"] = _mm(s["o"], dya, ta=True, out_dtype=BF16, name="mm_gw_attn_out")
    dzc, dlg, dlb = _ln_swish_bwd(dzs, s["zc"], sm["conv_ln_g"], sm["conv_ln_b"], "ln_swish_bwd")
    g["conv_ln_g"] = dlg[0]
    g["conv_ln_b"] = dlb[0]
    da, dgt, dcw, dcb = _conv_bwd(dzc, s["u"], bg["conv_dw_w"], cdim, "conv_bwd")
    g["conv_dw_w"] = dcw[:CONV_WIDTH].astype(BF16)
    g["conv_dw_b"] = dcb[0]
    after = emit(GRAD_GROUPS[1], g)
    do_gs, dd_gs = _combine_bwd(do, s["os"], s["lses"], "combine_bwd")
    du = lax.empty(s["u"].shape, BF16)
    for col, piece in ((0, da), (cdim, dgt), (gate_col, dugc), (gate_col + dugc.shape[1], duga)):
        du = lax.dynamic_update_slice(du, piece, (0, col))
    dgq = jnp.zeros((HEAD_DIM,), F32)
    dgk = jnp.zeros((HEAD_DIM,), F32)
    for gi in range(N_GROUPS):
        gq, gk, ds_sum, du = _attn_bwd(s["u"], do_gs[gi], dd_gs[gi], s["lses"][gi], s["qg2"], s["kg2"], bias,
                                       ds_sum, du, gi, cols, hpg, "attn_bwd_g%d" % gi)
        dgq = dgq + jnp.sum(gq.reshape(hp * 2, HEAD_DIM), axis=0)
        dgk = dgk + jnp.sum(gk.reshape(hp * 2, HEAD_DIM), axis=0)
    g["q_norm_g"] = dgq
    g["k_norm_g"] = dgk
    g["w_in"] = _mm_gw_in_pairs(s["h1"], du, after, "mm_gw_in")
    after = emit(GRAD_GROUPS[2], g)
    dx0, dg1 = _mm_din_pairs(du, bg["w_in"], s["x"], sm["norm1_g"], dx1, after, "mm_din")
    g["norm1_g"] = dg1[0]
    return dx0, g, ds_sum


BIG = ("w_in", "conv_dw_w", "w_conv_out", "w_attn_out", "w_out", "w_ff1", "w_ff2")
COL_SHARDED = ("w_in", "conv_dw_w", "w_conv_out", "w_attn_out", "w_ff1")
SMALL = ("rel_bias", "norm1_g", "q_norm_g", "k_norm_g", "conv_dw_b", "conv_ln_g", "conv_ln_b", "norm2_g")
WEIGHTS = ("rel_bias", "norm1_g", "w_in", "q_norm_g", "k_norm_g", "conv_dw_w", "conv_dw_b", "conv_ln_g", "conv_ln_b",
           "w_conv_out", "w_attn_out", "w_out", "norm2_g", "w_ff1", "w_ff2")


def _to_whole(name, gathered):
    n, a, b = gathered.shape
    if name in COL_SHARDED:
        return gathered.transpose(1, 0, 2).reshape(a, n * b)
    return gathered.reshape(n * a, b)


def _to_slots(name, whole):
    a, b = whole.shape
    if name in COL_SHARDED:
        return whole.reshape(a, N_DEV, b // N_DEV).transpose(1, 0, 2)
    return whole.reshape(N_DEV, a // N_DEV, b)


def _own_slot(block, me):
    land = lax.empty((N_DEV,) + block.shape, block.dtype)
    return lax.dynamic_update_slice(land, block[None], (me,) + (0,) * block.ndim)


def kernel(x, rel_bias, norm1_g, w_in, q_norm_g, k_norm_g, conv_dw_w, conv_dw_b, conv_ln_g, conv_ln_b, w_conv_out, w_attn_out, w_out, norm2_g, w_ff1, w_ff2, loss_target, m_rel_bias, m_norm1_g, m_w_in, m_q_norm_g, m_k_norm_g, m_conv_dw_w, m_conv_dw_b, m_conv_ln_g, m_conv_ln_b, m_w_conv_out, m_w_attn_out, m_w_out, m_norm2_g, m_w_ff1, m_w_ff2, v_rel_bias, v_norm1_g, v_w_in, v_q_norm_g, v_k_norm_g, v_conv_dw_w, v_conv_dw_b, v_conv_ln_g, v_conv_ln_b, v_w_conv_out, v_w_attn_out, v_w_out, v_norm2_g, v_w_ff1, v_w_ff2):
    w = dict(rel_bias=rel_bias, norm1_g=norm1_g, w_in=w_in, q_norm_g=q_norm_g, k_norm_g=k_norm_g, conv_dw_w=conv_dw_w,
             conv_dw_b=conv_dw_b, conv_ln_g=conv_ln_g, conv_ln_b=conv_ln_b, w_conv_out=w_conv_out,
             w_attn_out=w_attn_out, w_out=w_out, norm2_g=norm2_g, w_ff1=w_ff1, w_ff2=w_ff2)
    mom = dict(rel_bias=m_rel_bias, norm1_g=m_norm1_g, w_in=m_w_in, q_norm_g=m_q_norm_g, k_norm_g=m_k_norm_g,
               conv_dw_w=m_conv_dw_w, conv_dw_b=m_conv_dw_b, conv_ln_g=m_conv_ln_g, conv_ln_b=m_conv_ln_b,
               w_conv_out=m_w_conv_out, w_attn_out=m_w_attn_out, w_out=m_w_out, norm2_g=m_norm2_g, w_ff1=m_w_ff1,
               w_ff2=m_w_ff2)
    var = dict(rel_bias=v_rel_bias, norm1_g=v_norm1_g, w_in=v_w_in, q_norm_g=v_q_norm_g, k_norm_g=v_k_norm_g,
               conv_dw_w=v_conv_dw_w, conv_dw_b=v_conv_dw_b, conv_ln_g=v_conv_ln_g, conv_ln_b=v_conv_ln_b,
               w_conv_out=v_w_conv_out, w_attn_out=v_w_attn_out, w_out=v_w_out, norm2_g=v_norm2_g, w_ff1=v_w_ff1,
               w_ff2=v_w_ff2)

    depth = norm1_g.shape[0]
    me = 4 * lax.axis_index("x") + 2 * lax.axis_index("y") + lax.axis_index("c")
    odd_core = lax.axis_index("c") == 1
    hpg = w_attn_out.shape[1] // HEAD_DIM
    buckets = jnp.asarray(_bucket_table())
    bias = _bias_expand(rel_bias, buckets, hpg, "bias_expand")

    first_names = ("w_in", "conv_dw_w")
    rest_names = tuple(k for k in BIG if k not in first_names)

    def chain_start(l, names, after):
        shards = [w[k][l] if k == "conv_dw_w" else w[k][l].astype(BF16) for k in names]
        if "w_in" in names:
            i = names.index("w_in")
            shards[i] = jnp.where(odd_core, jnp.pad(shards[i], ((0, 0), (SHIFT, 0))),
                                  jnp.pad(shards[i], ((0, 0), (0, SHIFT))))
        sems, bufs, token = _gather_start(shards, [_own_slot(s, me) for s in shards], after,
                                          "gather_start_%s_l%d" % (names[0], l))
        return dict(l=l, names=names, sems=sems, bufs=bufs, token=token)

    def chain_forward(ch, after):
        fwd, bufs, token = _gather_forward(len(ch["names"]), ch["bufs"], ch["sems"][3], after,
                                           "gather_forward_%s_l%d" % (ch["names"][0], ch["l"]))
        ch.update(fwd=fwd, bufs=bufs)
        return token

    def chain_finish(ch, after):
        d2d_s, d2d_r, ici_s, _ = ch["sems"]
        gathered, _ = _gather_finish(len(ch["names"]), ch["bufs"], d2d_s, d2d_r, ici_s, ch["fwd"][0], ch["fwd"][1],
                                     after, "gather_finish_%s_l%d" % (ch["names"][0], ch["l"]))
        out = {k: a if k == "w_in" else _to_whole(k, a) for k, a in zip(ch["names"], gathered)}
        if "conv_dw_w" in out:
            out["conv_dw_w"] = jnp.pad(out["conv_dw_w"], ((0, CONV_TAPS_PADDED - CONV_WIDTH), (0, 0)))
        return out

    xs = x[0]
    h1 = _rms_fwd(xs, norm1_g[0], "rms1_fwd")
    saved, bigs, smalls = [], [], []
    chains = {}
    for l in range(depth):
        sm = {k: w[k][l] for k in SMALL if k != "rel_bias"}
        if l == 0:
            first = chain_start(0, first_names, None)
            token = chain_forward(first, None)
            rest = chain_start(0, rest_names, token)
            bg = chain_finish(first, rest["token"])

            def get_rest(o, rest=rest):
                token = chain_forward(rest, o)
                if depth > 1:
                    chains[1] = (chain_start(1, first_names, token),)
                    chains[1] += (chain_start(1, rest_names, chains[1][0]["token"]),)
                    token = chains[1][1]["token"]
                return chain_finish(rest, token)
        elif l == 1:
            first, rest = chains[1]
            token = chain_forward(first, xs)
            if depth > 2:
                chains[2] = chain_start(2, BIG, token)
                token = chains[2]["token"]
            bg = chain_finish(first, token)

            def get_rest(o, rest=rest):
                return chain_finish(rest, chain_forward(rest, o))
        else:
            token = chain_forward(chains[l], xs)
            if l + 1 < depth:
                chains[l + 1] = chain_start(l + 1, BIG, token)
                token = chains[l + 1]["token"]
            whole = chain_finish(chains[l], token)
            bg = {k: whole[k] for k in first_names}

            def get_rest(o, whole=whole):
                return {k: whole[k] for k in rest_names}
        xs, h1, sv, bg = _layer_fwd(xs, h1, sm, bg, get_rest, bias, hpg, norm1_g[l + 1] if l + 1 < depth else None)
        saved.append(sv)
        bigs.append(bg)
        smalls.append(sm)

    loss, dx = _loss_and_grad(xs, loss_target[0], "loss")

    ds_sum = jnp.zeros((N_GROUPS * hpg, BLOCK, 2 * BLOCK), F32)
    g = {k: [None] * depth for k in SMALL if k != "rel_bias"}
    sums = {k: lax.empty((depth, int(np.prod(w[k].shape[1:-1])), w[k].shape[-1] + (SHIFT if k == "w_in" else 0)), F32)
            for k in BIG}
    pending = []

    def finish_oldest(after):
        names, l, (send, recv), bufs = pending.pop(0)
        parts, recvd, token = _exchange_finish(len(names), bufs, send, recv, after,
                                               "exchange_finish_%s_l%d" % (names[0], l))
        for k, r, p in zip(names, recvd, parts):
            three = (N_DEV, -1, r.shape[-1])
            sums[k] = _sum_slots(r.reshape(three), p.reshape(three), me, sums[k], l, "sum_" + k)
        return token

    def make_emit(l):
        def emit(names, gl):
            parts = [gl[k] if k in ("w_ff1", "w_in") else _to_slots(k, gl[k]) for k in names]
            token = finish_oldest(parts[0]) if len(pending) >= len(GRAD_GROUPS) else None
            lands = [lax.empty(p.shape, p.dtype) for p in parts]
            sems, bufs, token = _exchange_start(parts, lands, token, "exchange_start_%s_l%d" % (names[0], l))
            pending.append((names, l, sems, bufs))
            return token
        return emit

    token = None
    for l in reversed(range(depth)):
        dx, gl, ds_sum = _layer_bwd(dx, saved[l], smalls[l], bigs[l], bias, ds_sum, token, make_emit(l))
        for k in g:
            g[k][l] = gl[k]
        token = None
    grad_x = dx

    g = {k: jnp.stack(v) for k, v in g.items()}
    db = _bias_reduce(ds_sum, buckets, hpg, "bias_reduce")
    g["rel_bias"] = db[:, 0, :NUM_BUCKETS].T

    flat = jnp.concatenate([g[k].reshape(-1) for k in SMALL])
    nflat = flat.shape[0]
    rows = -(-nflat // (8 * LANES)) * 8
    packed = jnp.pad(flat, (0, rows * LANES - nflat)).reshape(rows, LANES)
    grad, outs = {}, {}
    token = dx
    while pending:
        names = pending[0][0]
        finish_oldest(token)
        for k in names:
            total = sums[k]
            if k == "w_in":
                total = jnp.where(odd_core, total[..., SHIFT:], total[..., :w_in.shape[-1]])
            grad[k] = total.reshape(w[k].shape)
            outs[k] = _adamw(w[k], grad[k], mom[k], var[k], "adamw_" + k)
            token = outs[k][0]
    total = _all_reduce_small(packed, token, "reduce_small").reshape(-1)
    off = 0
    for k in SMALL:
        size = int(np.prod(w[k].shape))
        grad[k] = total[off:off + size].reshape(w[k].shape)
        outs[k] = _adamw(w[k], grad[k], mom[k], var[k], "adamw_" + k)
        off += size
    loss = lax.psum(loss[0, 0], ("x", "y", "c"))
    return (loss, grad_x[None], *[grad[k] for k in WEIGHTS], *[outs[k][0] for k in WEIGHTS],
            *[outs[k][1] for k in WEIGHTS], *[outs[k][2] for k in WEIGHTS])
```

```python
import functools
import math

import numpy as np
import jax
import jax.numpy as jnp
from jax import lax
from jax.experimental import pallas as pl
from jax.experimental.pallas import tpu as pltpu

F32 = jnp.float32
BF16 = jnp.bfloat16

HEAD_DIM = 64
N_GROUPS = 3
DILATIONS = (1, 4, 16)
SUB_WINDOW = 128
BLOCK = 128
CONV_WIDTH = 31
CONV_TAPS_PADDED = 32
NUM_BUCKETS = 32
MAX_REL_DISTANCE = 2048
EPS = 1e-6
NEG_INF = -1e30
LANES = 128
SUBLANES = 8

ADAM_LR = 0.001
ADAM_B1 = 0.9
ADAM_B2 = 0.999
ADAM_EPS = 1e-08
ADAM_WD = 0.01
ADAM_STEP = 10

N_DEV = 8
VMEM_LIMIT = 56 * 1024 * 1024
MESH = pl.DeviceIdType.MESH


def _cparams(sem=None):
    return pltpu.CompilerParams(dimension_semantics=sem, vmem_limit_bytes=VMEM_LIMIT)


def _tile(n, target):
    if n <= target:
        return n
    t = (target // LANES) * LANES
    while t >= LANES:
        if n % t == 0:
            return t
        t -= LANES
    return n


def _sigmoid(v):
    return 1.0 / (1.0 + jnp.exp(-v))


MM_VMEM_BUDGET = 40 * 1024 * 1024


def _rms_apply(x, g):
    return x * lax.rsqrt(jnp.mean(x * x, axis=-1, keepdims=True) + EPS) * g


def _rms_grad(dh, x, g):
    r = lax.rsqrt(jnp.mean(x * x, axis=-1, keepdims=True) + EPS)
    xh = x * r
    dxh = dh * g
    dx = r * (dxh - xh * jnp.mean(dxh * xh, axis=-1, keepdims=True))
    return dx, jnp.sum(dh * xh, axis=0, keepdims=True)


def _mm_tiles(m, n, kdim, a_bytes, b_bytes, io_bytes, whole_rows=False, temps=2):
    def need(tm, tn, tk):
        blocks = 2 * (tm * tk * a_bytes + tk * tn * b_bytes + tm * tn * io_bytes)
        casts = (tm * tk * 2 if a_bytes == 4 else 0) + (tk * tn * 2 if b_bytes == 4 else 0)
        return blocks + casts + temps * tm * tn * 4

    tn = n if whole_rows else _tile(n, 1024)
    while True:
        fits = [(tm * tk, tm, tk) for tm in {_tile(m, c) for c in (1024, 512, 256, 128)}
                for tk in {_tile(kdim, c) for c in (2048, 1024, 512, 256)} if need(tm, tn, tk) <= MM_VMEM_BUDGET]
        if fits:
            _, tm, tk = max(fits)
            return tm, tn, tk
        assert not whole_rows and tn % 256 == 0, "no block size fits the VMEM budget"
        tn //= 2


def _mm(a, b, *, ta=False, tb=False, out_dtype=F32, epi=None, extra=(), gain=None, after=None, out_slots=False,
        a_relu2=False, name):
    extra = tuple(extra) if isinstance(extra, (tuple, list)) else (extra,)
    m = a.shape[1] if ta else a.shape[0]
    kdim = a.shape[0] if ta else a.shape[1]
    n = b.shape[0] if tb else b.shape[1]
    norm = epi in ("res_rms", "rms_bwd")
    io_bytes = (jnp.dtype(out_dtype).itemsize + sum(e.dtype.itemsize for e in extra) + (2 if epi == "res_rms" else 0))
    tm, tn, tk = _mm_tiles(m, n // N_DEV if out_slots else n, kdim, a.dtype.itemsize, b.dtype.itemsize, io_bytes,
                           whole_rows=norm, temps=6 if norm else 2)
    if out_slots:
        assert epi is None and tn == n // N_DEV
    nk = kdim // tk
    a_spec = pl.BlockSpec((tk, tm), lambda i, j, k: (k, i)) if ta else pl.BlockSpec((tm, tk), lambda i, j, k: (i, k))
    b_spec = pl.BlockSpec((tn, tk), lambda i, j, k: (j, k)) if tb else pl.BlockSpec((tk, tn), lambda i, j, k: (k, j))
    o_spec = (pl.BlockSpec((None, tm, tn), lambda i, j, k: (j, i, 0)) if out_slots
              else pl.BlockSpec((tm, tn), lambda i, j, k: (i, j)))
    v_spec = pl.BlockSpec((1, tn), lambda i, j, k: (0, j))
    dims = (((0 if ta else 1,), (1 if tb else 0,)), ((), ()))
    n_extra = len(extra)
    n_in = 2 + n_extra + (gain is not None) + (after is not None)
    n_out = 2 if norm else 1

    def body(*refs):
        a_ref, b_ref = refs[0], refs[1]
        e_refs = refs[2:2 + n_extra]
        g_ref = refs[2 + n_extra] if gain is not None else None
        outs = refs[n_in:n_in + n_out]

        def product():
            av = a_ref[...]
            if a_relu2:
                r = jnp.maximum(av.astype(F32), 0.0)
                av = r * r
            return lax.dot_general(av.astype(BF16), b_ref[...].astype(BF16), dims, preferred_element_type=F32)

        def finish(acc):
            if epi is None:
                outs[0][...] = acc.astype(outs[0].dtype)
            elif epi == "res":
                outs[0][...] = (e_refs[0][...] + acc).astype(outs[0].dtype)
            elif epi == "drelu2":
                outs[0][...] = (acc * (2.0 * jnp.maximum(e_refs[0][...].astype(F32), 0.0))).astype(outs[0].dtype)
            elif epi == "res_rms":
                x1 = e_refs[0][...] + acc
                outs[0][...] = x1
                outs[1][...] = _rms_apply(x1, g_ref[...]).astype(BF16)
            elif epi == "rms_bwd":
                dx, dg = _rms_grad(acc, e_refs[0][...], g_ref[...])
                outs[0][...] = e_refs[1][...] + dx
                i = pl.program_id(0)

                @pl.when(i == 0)
                def _():
                    outs[1][...] = dg

                @pl.when(i > 0)
                def _():
                    outs[1][...] += dg

        if nk == 1:
            finish(product())
            return
        acc_ref = refs[-1]
        k = pl.program_id(2)

        @pl.when(k == 0)
        def _():
            acc_ref[...] = product()

        @pl.when(jnp.logical_and(k > 0, k < nk - 1))
        def _():
            acc_ref[...] += product()

        @pl.when(k == nk - 1)
        def _():
            finish(acc_ref[...] + product())

    in_specs = ([a_spec, b_spec] + [o_spec] * n_extra + ([v_spec] if gain is not None else [])
                + ([pl.BlockSpec(memory_space=pl.ANY)] if after is not None else []))
    if epi == "res_rms":
        out_shape = (jax.ShapeDtypeStruct((m, n), F32), jax.ShapeDtypeStruct((m, n), BF16))
        out_specs = (o_spec, o_spec)
    elif epi == "rms_bwd":
        out_shape = (jax.ShapeDtypeStruct((m, n), F32), jax.ShapeDtypeStruct((1, n), F32))
        out_specs = (o_spec, v_spec)
    else:
        out_shape = jax.ShapeDtypeStruct((N_DEV, m, tn) if out_slots else (m, n), out_dtype)
        out_specs = o_spec
    args = (a, b) + extra + ((gain.reshape(1, n),) if gain is not None else ()) + ((after,) if after is not None else ())
    return pl.pallas_call(
        body, name=name, grid=(m // tm, n // tn, nk), in_specs=in_specs, out_specs=out_specs, out_shape=out_shape,
        scratch_shapes=[pltpu.VMEM((tm, tn), F32)] if nk > 1 else [],
        compiler_params=_cparams(("arbitrary", "arbitrary", "arbitrary")),
    )(*args)


SHIFT = HEAD_DIM


def _pair_blocks(e, o):
    wp = e.shape[-1]
    return e[:, :wp - LANES], e[:, wp - LANES:] + o[:, :LANES], o[:, LANES:]


def _mm_in_pairs(a, wg, name):
    t, kdim = a.shape
    wp = wg.shape[-1]
    ws = wp - SHIFT
    tm = _tile(t, 1024)

    def body(a_ref, e_ref, o_ref, u_ref):
        av = a_ref[...]
        lo, mid, hi = _pair_blocks(e_ref[...], o_ref[...])
        u_ref[:, :wp - LANES] = jnp.dot(av, lo, preferred_element_type=F32).astype(BF16)
        u_ref[:, wp - LANES:wp] = jnp.dot(av, mid, preferred_element_type=F32).astype(BF16)
        u_ref[:, wp:] = jnp.dot(av, hi, preferred_element_type=F32).astype(BF16)

    return pl.pallas_call(
        body, name=name, grid=(N_DEV // 2, t // tm),
        in_specs=[pl.BlockSpec((tm, kdim), lambda p, i: (i, 0)),
                  pl.BlockSpec((None, kdim, wp), lambda p, i: (2 * p, 0, 0)),
                  pl.BlockSpec((None, kdim, wp), lambda p, i: (2 * p + 1, 0, 0))],
        out_specs=pl.BlockSpec((tm, 2 * ws), lambda p, i: (i, p)),
        out_shape=jax.ShapeDtypeStruct((t, N_DEV * ws), BF16), compiler_params=_cparams(("arbitrary", "arbitrary")),
    )(a, wg, wg)


def _mm_din_pairs(du, wg, x, gain, dres, after, name):
    t = du.shape[0]
    _, kdim, wp = wg.shape
    ws = wp - SHIFT
    tm = _tile(t, 512)
    npair = N_DEV // 2
    lanes = (((1,), (1,)), ((), ()))
    extra = [] if after is None else [after]

    def body(d_ref, e_ref, o_ref, x_ref, g_ref, r_ref, *rest):
        dx_ref, dg_ref, acc_ref = rest[-3], rest[-2], rest[-1]
        i, p = pl.program_id(0), pl.program_id(1)
        lo, mid, hi = _pair_blocks(e_ref[...], o_ref[...])
        part = (lax.dot_general(d_ref[:, :wp - LANES], lo, lanes, preferred_element_type=F32)
                + lax.dot_general(d_ref[:, wp - LANES:wp], mid, lanes, preferred_element_type=F32)
                + lax.dot_general(d_ref[:, wp:], hi, lanes, preferred_element_type=F32))

        @pl.when(p == 0)
        def _():
            acc_ref[...] = part

        @pl.when(jnp.logical_and(p > 0, p < npair - 1))
        def _():
            acc_ref[...] += part

        @pl.when(p == npair - 1)
        def _():
            dx, dg = _rms_grad(acc_ref[...] + part, x_ref[...], g_ref[...])
            dx_ref[...] = r_ref[...] + dx

            @pl.when(i == 0)
            def _():
                dg_ref[...] = dg

            @pl.when(i > 0)
            def _():
                dg_ref[...] += dg

    row = pl.BlockSpec((tm, kdim), lambda i, p: (i, 0))
    vec = pl.BlockSpec((1, kdim), lambda i, p: (0, 0))
    return pl.pallas_call(
        body, name=name, grid=(t // tm, npair),
        in_specs=[pl.BlockSpec((tm, 2 * ws), lambda i, p: (i, p)),
                  pl.BlockSpec((None, kdim, wp), lambda i, p: (2 * p, 0, 0)),
                  pl.BlockSpec((None, kdim, wp), lambda i, p: (2 * p + 1, 0, 0)), row, vec, row]
        + [pl.BlockSpec(memory_space=pl.ANY)] * len(extra),
        out_specs=(row, vec),
        out_shape=(jax.ShapeDtypeStruct((t, kdim), F32), jax.ShapeDtypeStruct((1, kdim), F32)),
        scratch_shapes=[pltpu.VMEM((tm, kdim), F32)], compiler_params=_cparams(("arbitrary", "arbitrary")),
    )(du, wg, wg, x, gain.reshape(1, kdim), dres, *extra)


def _mm_gw_in_pairs(h, du, after, name):
    t, kdim = h.shape
    ws = du.shape[1] // N_DEV
    wp = ws + SHIFT
    tm = _tile(kdim, 512)
    rows = (((0,), (0,)), ((), ()))
    extra = [] if after is None else [after]

    def body(h_ref, d_ref, *rest):
        g_ref = rest[-1]
        g = lax.dot_general(h_ref[...], d_ref[...], rows, preferred_element_type=F32)
        g_ref[0] = g[:, :wp].astype(BF16)
        g_ref[1] = g[:, wp - LANES:].astype(BF16)

    return pl.pallas_call(
        body, name=name, grid=(N_DEV // 2, kdim // tm),
        in_specs=[pl.BlockSpec((t, tm), lambda p, i: (0, i)), pl.BlockSpec((t, 2 * ws), lambda p, i: (0, p))]
        + [pl.BlockSpec(memory_space=pl.ANY)] * len(extra),
        out_specs=pl.BlockSpec((2, tm, wp), lambda p, i: (p, i, 0)),
        out_shape=jax.ShapeDtypeStruct((N_DEV, kdim, wp), BF16), compiler_params=_cparams(("arbitrary", "arbitrary")),
    )(h, du, *extra)


ROW_BLOCK_BUDGET = 24 * 1024 * 1024


def _rows(t, row_bytes):
    rows = t
    while rows > 8 and (2 * rows * row_bytes > ROW_BLOCK_BUDGET or t % rows):
        rows //= 2
    return rows


def _rms_fwd(x, g, name):
    t, d = x.shape
    ROWS = _rows(t, 6 * d)

    def body(x_ref, g_ref, h_ref):
        h_ref[...] = _rms_apply(x_ref[...], g_ref[...]).astype(BF16)

    return pl.pallas_call(
        body, name=name, grid=(t // ROWS,),
        in_specs=[pl.BlockSpec((ROWS, d), lambda i: (i, 0)), pl.BlockSpec((1, d), lambda i: (0, 0))],
        out_specs=pl.BlockSpec((ROWS, d), lambda i: (i, 0)),
        out_shape=jax.ShapeDtypeStruct((t, d), BF16), compiler_params=_cparams(("arbitrary",)),
    )(x, g.reshape(1, d))


def _gated_out(zs, o, wc, wa, u, gate_col, name):
    t = zs.shape[0]
    d = wc.shape[1]
    td = math.gcd(_tile(d, 512), gate_col)
    nd = d // td
    c0 = gate_col // td
    tm = _tile(t, 1024)

    def body(zs_ref, o_ref, wc_ref, wa_ref, gc_ref, ga_ref, m_ref, yc_ref, ya_ref):
        yc = jnp.dot(zs_ref[...], wc_ref[...], preferred_element_type=F32)
        ya = jnp.dot(o_ref[...], wa_ref[...], preferred_element_type=F32)
        gc = _sigmoid(gc_ref[...].astype(F32))
        ga = _sigmoid(ga_ref[...].astype(F32))
        m_ref[...] = (gc * yc + ga * ya).astype(BF16)
        yc_ref[...] = yc.astype(BF16)
        ya_ref[...] = ya.astype(BF16)

    blk = pl.BlockSpec((tm, td), lambda i, j: (i, j))
    sh = jax.ShapeDtypeStruct((t, d), BF16)
    return pl.pallas_call(
        body, name=name, grid=(t // tm, nd),
        in_specs=[pl.BlockSpec((tm, zs.shape[1]), lambda i, j: (i, 0)), pl.BlockSpec((tm, o.shape[1]), lambda i, j: (i, 0)),
                  pl.BlockSpec((wc.shape[0], td), lambda i, j: (0, j)), pl.BlockSpec((wa.shape[0], td), lambda i, j: (0, j)),
                  pl.BlockSpec((tm, td), lambda i, j: (i, c0 + j)), pl.BlockSpec((tm, td), lambda i, j: (i, c0 + nd + j))],
        out_specs=(blk, blk, blk), out_shape=(sh, sh, sh), compiler_params=_cparams(("arbitrary", "arbitrary")),
    )(zs, o, wc, wa, u, u)


def _gate_bwd(dm, u, yc, ya, gate_col, name):
    t, d = yc.shape
    td = math.gcd(_tile(d, 512), gate_col)
    nd = d // td
    c0 = gate_col // td
    ROWS = _rows(t, 20 * td)

    def body(dm_ref, gc_ref, ga_ref, yc_ref, ya_ref, dyc_ref, dya_ref, dugc_ref, duga_ref):
        dmv = dm_ref[...]
        gc = _sigmoid(gc_ref[...].astype(F32))
        ga = _sigmoid(ga_ref[...].astype(F32))
        dyc_ref[...] = (dmv * gc).astype(BF16)
        dya_ref[...] = (dmv * ga).astype(BF16)
        dugc_ref[...] = (dmv * yc_ref[...].astype(F32) * gc * (1.0 - gc)).astype(BF16)
        duga_ref[...] = (dmv * ya_ref[...].astype(F32) * ga * (1.0 - ga)).astype(BF16)

    blk = pl.BlockSpec((ROWS, td), lambda i, j: (i, j))
    o = jax.ShapeDtypeStruct((t, d), BF16)
    return pl.pallas_call(
        body, name=name, grid=(t // ROWS, nd),
        in_specs=[blk, pl.BlockSpec((ROWS, td), lambda i, j: (i, c0 + j)),
                  pl.BlockSpec((ROWS, td), lambda i, j: (i, c0 + nd + j)), blk, blk],
        out_specs=(blk, blk, blk, blk), out_shape=(o, o, o, o),
        compiler_params=_cparams(("arbitrary", "arbitrary")),
    )(dm, u, u, yc, ya)


def _loss_and_grad(y, target, name):
    t, d = y.shape
    ROWS = _rows(t, 12 * d)
    n = t // ROWS

    def body(y_ref, t_ref, loss_ref, dy_ref, acc_ref):
        i = pl.program_id(0)

        @pl.when(i == 0)
        def _():
            acc_ref[...] = jnp.zeros_like(acc_ref)

        diff = y_ref[...] - t_ref[...]
        dy_ref[...] = diff * (1.0 / d)
        acc_ref[...] += jnp.sum(diff * diff, axis=0, keepdims=True)

        @pl.when(i == n - 1)
        def _():
            loss_ref[...] = jnp.sum(acc_ref[...], axis=-1, keepdims=True) * (0.5 / d)

    row = pl.BlockSpec((ROWS, d), lambda i: (i, 0))
    return pl.pallas_call(
        body, name=name, grid=(n,), in_specs=[row, row],
        out_specs=(pl.BlockSpec((1, 1), lambda i: (0, 0)), row),
        out_shape=(jax.ShapeDtypeStruct((1, 1), F32), jax.ShapeDtypeStruct((t, d), F32)),
        scratch_shapes=[pltpu.VMEM((1, d), F32)], compiler_params=_cparams(("arbitrary",)),
    )(y, target)


HALO = 32


def _conv_fwd(u, w, b, cdim, name):
    t = u.shape[0]
    ncb = cdim // LANES
    nt = t // BLOCK

    def body(a_ref, g_ref, w_ref, b_ref, zc_ref, zpad):
        zpad[0:HALO, :] = jnp.zeros((HALO, LANES), F32)
        zpad[HALO:HALO + t, :] = a_ref[...].astype(F32) * _sigmoid(g_ref[...].astype(F32))
        wv = w_ref[...]
        bv = b_ref[...]

        def tile(i, carry):
            r0 = pl.multiple_of(i * BLOCK, BLOCK)
            win = zpad[pl.ds(r0, BLOCK + HALO), :]
            acc = jnp.zeros((BLOCK, LANES), F32) + bv
            for b in range(SUBLANES):
                sh = win if b == 0 else pltpu.roll(win, b, 0)
                for a in range(HALO // SUBLANES):
                    j = CONV_WIDTH - 1 - (SUBLANES * a + b)
                    if j >= 0:
                        lo = HALO - SUBLANES * a
                        acc = acc + wv[j:j + 1, :] * sh[lo:lo + BLOCK, :]
            zc_ref[pl.ds(r0, BLOCK), :] = acc
            return carry

        lax.fori_loop(0, nt, tile, 0)

    col = lambda off: pl.BlockSpec((t, LANES), lambda c: (0, off + c))
    return pl.pallas_call(
        body, name=name, grid=(ncb,),
        in_specs=[col(0), col(ncb), pl.BlockSpec((CONV_TAPS_PADDED, LANES), lambda c: (0, c)),
                  pl.BlockSpec((1, LANES), lambda c: (0, c))],
        out_specs=pl.BlockSpec((t, LANES), lambda c: (0, c)),
        out_shape=jax.ShapeDtypeStruct((t, cdim), F32),
        scratch_shapes=[pltpu.VMEM((t + HALO, LANES), F32)], compiler_params=_cparams(("arbitrary",)),
    )(u, u, w, b.reshape(1, cdim))


def _conv_bwd(dzc, u, w, cdim, name):
    t = u.shape[0]
    ncb = cdim // LANES
    nt = t // BLOCK
    win_rows = BLOCK + HALO

    def body(dzc_ref, a_ref, g_ref, w_ref, da_ref, dg_ref, dw_ref, db_ref, zpad, dpad):
        av = a_ref[...].astype(F32)
        sg = _sigmoid(g_ref[...].astype(F32))
        zpad[0:HALO, :] = jnp.zeros((HALO, LANES), F32)
        zpad[HALO:HALO + t, :] = av * sg
        dpad[0:t, :] = dzc_ref[...]
        dpad[t:t + HALO, :] = jnp.zeros((HALO, LANES), F32)
        dw_ref[...] = jnp.zeros_like(dw_ref)
        db_ref[...] = jnp.sum(dzc_ref[...], axis=0, keepdims=True)
        wv = w_ref[...]

        def tile(i, carry):
            r0 = pl.multiple_of(i * BLOCK, BLOCK)
            zwin = zpad[pl.ds(r0, win_rows), :]
            dwin = dpad[pl.ds(r0, win_rows), :]
            dcur = dwin[0:BLOCK, :]
            dz = jnp.zeros((BLOCK, LANES), F32)
            for b in range(SUBLANES):
                zs = zwin if b == 0 else pltpu.roll(zwin, b, 0)
                ds = dwin if b == 0 else pltpu.roll(dwin, win_rows - b, 0)
                for a in range(HALO // SUBLANES):
                    j = CONV_WIDTH - 1 - (SUBLANES * a + b)
                    if j >= 0:
                        lo = HALO - SUBLANES * a
                        dw_ref[j:j + 1, :] += jnp.sum(dcur * zs[lo:lo + BLOCK, :], axis=0, keepdims=True)
                        dz = dz + wv[j:j + 1, :] * ds[SUBLANES * a:SUBLANES * a + BLOCK, :]
            ac = a_ref[pl.ds(r0, BLOCK), :].astype(F32)
            sc = _sigmoid(g_ref[pl.ds(r0, BLOCK), :].astype(F32))
            da_ref[pl.ds(r0, BLOCK), :] = (dz * sc).astype(BF16)
            dg_ref[pl.ds(r0, BLOCK), :] = (dz * ac * sc * (1.0 - sc)).astype(BF16)
            return carry

        lax.fori_loop(0, nt, tile, 0)

    col = lambda off: pl.BlockSpec((t, LANES), lambda c: (0, off + c))
    wspec = pl.BlockSpec((CONV_TAPS_PADDED, LANES), lambda c: (0, c))
    o = jax.ShapeDtypeStruct((t, cdim), BF16)
    return pl.pallas_call(
        body, name=name, grid=(ncb,), in_specs=[col(0), col(0), col(ncb), wspec],
        out_specs=(col(0), col(0), wspec, pl.BlockSpec((1, LANES), lambda c: (0, c))),
        out_shape=(o, o, jax.ShapeDtypeStruct((CONV_TAPS_PADDED, cdim), F32), jax.ShapeDtypeStruct((1, cdim), F32)),
        scratch_shapes=[pltpu.VMEM((t + HALO, LANES), F32), pltpu.VMEM((t + HALO, LANES), F32)],
        compiler_params=_cparams(("arbitrary",)),
    )(dzc, u, u, w)


def _ln_swish_fwd(zc, g, b, name):
    t, c = zc.shape
    ROWS = _rows(t, 6 * c)

    def body(z_ref, g_ref, b_ref, o_ref):
        z = z_ref[...]
        mu = jnp.mean(z, axis=-1, keepdims=True)
        zc_ = z - mu
        zn = zc_ * lax.rsqrt(jnp.mean(zc_ * zc_, axis=-1, keepdims=True) + EPS)
        y = zn * g_ref[...] + b_ref[...]
        o_ref[...] = (y * _sigmoid(y)).astype(BF16)

    row = pl.BlockSpec((ROWS, c), lambda i: (i, 0))
    vec = pl.BlockSpec((1, c), lambda i: (0, 0))
    return pl.pallas_call(
        body, name=name, grid=(t // ROWS,), in_specs=[row, vec, vec], out_specs=row,
        out_shape=jax.ShapeDtypeStruct((t, c), BF16), compiler_params=_cparams(("arbitrary",)),
    )(zc, g.reshape(1, c), b.reshape(1, c))


def _ln_swish_bwd(dzs, zc, g, b, name):
    t, c = zc.shape
    ROWS = _rows(t, 12 * c)

    def body(d_ref, z_ref, g_ref, b_ref, dz_ref, dg_ref, db_ref):
        @pl.when(pl.program_id(0) == 0)
        def _():
            dg_ref[...] = jnp.zeros_like(dg_ref)
            db_ref[...] = jnp.zeros_like(db_ref)

        z = z_ref[...]
        mu = jnp.mean(z, axis=-1, keepdims=True)
        zc_ = z - mu
        rstd = lax.rsqrt(jnp.mean(zc_ * zc_, axis=-1, keepdims=True) + EPS)
        zn = zc_ * rstd
        y = zn * g_ref[...] + b_ref[...]
        sg = _sigmoid(y)
        dy = d_ref[...] * (sg * (1.0 + y * (1.0 - sg)))
        dg_ref[...] += jnp.sum(dy * zn, axis=0, keepdims=True)
        db_ref[...] += jnp.sum(dy, axis=0, keepdims=True)
        dzn = dy * g_ref[...]
        dz_ref[...] = rstd * (dzn - jnp.mean(dzn, axis=-1, keepdims=True)
                              - zn * jnp.mean(dzn * zn, axis=-1, keepdims=True))

    row = pl.BlockSpec((ROWS, c), lambda i: (i, 0))
    vec = pl.BlockSpec((1, c), lambda i: (0, 0))
    v = jax.ShapeDtypeStruct((1, c), F32)
    return pl.pallas_call(
        body, name=name, grid=(t // ROWS,), in_specs=[row, row, vec, vec], out_specs=(row, vec, vec),
        out_shape=(jax.ShapeDtypeStruct((t, c), F32), v, v), compiler_params=_cparams(("arbitrary",)),
    )(dzs, zc, g.reshape(1, c), b.reshape(1, c))


def _bucket_table():
    qi = np.arange(BLOCK)[:, None]
    kj = np.arange(2 * BLOCK)[None, :]
    off = qi + BLOCK - kj
    band = (off >= 0) & (off <= SUB_WINDOW)
    max_exact = NUM_BUCKETS // 2
    out = []
    for d in DILATIONS:
        dist = (np.clip(off, 0, SUB_WINDOW) * d).astype(np.int32)
        nf = np.maximum(dist, 1).astype(np.float32)
        large = max_exact + (np.log(nf / np.float32(max_exact)) / np.float32(math.log(MAX_REL_DISTANCE / max_exact))
                             * np.float32(NUM_BUCKETS - max_exact)).astype(np.int32)
        large = np.minimum(large, NUM_BUCKETS - 1)
        bucket = np.where(dist < max_exact, dist, large)
        out.append(np.where(band, bucket, -1))
    return np.stack(out).astype(np.int32)


def _bias_expand(rel_bias, buckets, hpg, name):
    nh = N_GROUPS * hpg

    def body(rb_ref, bk_ref, o_ref):
        h = pl.program_id(0)
        bk = bk_ref[0]
        acc = jnp.full((BLOCK, 2 * BLOCK), NEG_INF, F32)
        for bb in range(NUM_BUCKETS):
            acc = jnp.where(bk == bb, rb_ref[bb, h], acc)
        o_ref[0] = acc

    return pl.pallas_call(
        body, name=name, grid=(nh,),
        in_specs=[pl.BlockSpec(memory_space=pltpu.SMEM),
                  pl.BlockSpec((1, BLOCK, 2 * BLOCK), lambda h: (h // hpg, 0, 0))],
        out_specs=pl.BlockSpec((1, BLOCK, 2 * BLOCK), lambda h: (h, 0, 0)),
        out_shape=jax.ShapeDtypeStruct((nh, BLOCK, 2 * BLOCK), F32), compiler_params=_cparams(("arbitrary",)),
    )(rel_bias, buckets)


def _bias_reduce(ds_sum, buckets, hpg, name):
    nh = N_GROUPS * hpg

    def body(ds_ref, bk_ref, o_ref):
        bk = bk_ref[0]
        dsv = ds_ref[0]
        lane = lax.broadcasted_iota(jnp.int32, (1, LANES), 1)
        row = jnp.zeros((1, LANES), F32)
        for bb in range(NUM_BUCKETS):
            tot = jnp.sum(jnp.sum(jnp.where(bk == bb, dsv, 0.0), axis=-1, keepdims=True), axis=0, keepdims=True)
            row = jnp.where(lane == bb, tot, row)
        o_ref[0] = row

    return pl.pallas_call(
        body, name=name, grid=(nh,),
        in_specs=[pl.BlockSpec((1, BLOCK, 2 * BLOCK), lambda h: (h, 0, 0)),
                  pl.BlockSpec((1, BLOCK, 2 * BLOCK), lambda h: (h // hpg, 0, 0))],
        out_specs=pl.BlockSpec((1, 1, LANES), lambda h: (h, 0, 0)),
        out_shape=jax.ShapeDtypeStruct((nh, 1, LANES), F32), compiler_params=_cparams(("arbitrary",)),
    )(ds_sum, buckets)


def _chunk_rows(c, d, nb):
    r, n = c // nb, c % nb
    if d == 1:
        return pl.ds(c * BLOCK, BLOCK)
    return pl.ds(r + n * BLOCK * d, BLOCK, stride=d)


def _segment_ones():
    i = lax.broadcasted_iota(jnp.int32, (LANES, LANES), 0) // HEAD_DIM
    j = lax.broadcasted_iota(jnp.int32, (LANES, LANES), 1) // HEAD_DIM
    return (i == j).astype(BF16)


def _segment_sum(v, seg):
    hi = v.astype(BF16)
    lo = (v - hi.astype(F32)).astype(BF16)
    return jnp.dot(hi, seg, preferred_element_type=F32) + jnp.dot(lo, seg, preferred_element_type=F32)


def _head_mean(v, seg):
    return _segment_sum(v, seg) * (1.0 / HEAD_DIM)


def _attn_fwd(u, qg2, kg2, bias, gi, cols, hpg, name):
    t = u.shape[0]
    d = DILATIONS[gi]
    nchunk = t // BLOCK
    nb = (t // d) // BLOCK
    hp = hpg // 2
    qc0, kc0, vc0 = [(c + gi * hpg * HEAD_DIM) // LANES for c in cols]
    contract_lanes = (((1,), (1,)), ((), ()))

    def body(q_ref, k_ref, v_ref, qg_ref, kg_ref, bias_ref, o_ref, lse_ref, qd, kd, vd, od, ld, sbuf):
        seg = _segment_ones()
        lane = lax.broadcasted_iota(jnp.int32, (1, LANES), 1)
        qg = qg_ref[...] * (HEAD_DIM ** -0.5)
        kg = kg_ref[...]
        kd[0:BLOCK, :] = jnp.zeros((BLOCK, LANES), BF16)
        vd[0:BLOCK, :] = jnp.zeros((BLOCK, LANES), BF16)
        od[...] = q_ref[...].astype(F32)
        ld[...] = k_ref[...].astype(F32)
        for c in range(nchunk):
            rows = _chunk_rows(c, d, nb)
            qv = od[rows, :]
            kv = ld[rows, :]
            qd[c * BLOCK:(c + 1) * BLOCK, :] = (qv * lax.rsqrt(_head_mean(qv * qv, seg) + EPS) * qg).astype(BF16)
            kd[(c + 1) * BLOCK:(c + 2) * BLOCK, :] = (kv * lax.rsqrt(_head_mean(kv * kv, seg) + EPS) * kg).astype(BF16)
        od[...] = v_ref[...].astype(F32)
        for c in range(nchunk):
            vd[(c + 1) * BLOCK:(c + 2) * BLOCK, :] = od[_chunk_rows(c, d, nb), :].astype(BF16)

        col = lax.broadcasted_iota(jnp.int32, (BLOCK, 2 * BLOCK), 1)
        for j in range(2):
            mj = jnp.logical_and(lane >= j * HEAD_DIM, lane < (j + 1) * HEAD_DIM)
            for c in range(nchunk):
                kw = kd[c * BLOCK:(c + 2) * BLOCK, :]
                kj = jnp.where(mj, kw, jnp.zeros_like(kw))
                s = lax.dot_general(qd[c * BLOCK:(c + 1) * BLOCK, :], kj, contract_lanes,
                                    preferred_element_type=F32) + bias_ref[j]
                if c % nb == 0:
                    s = jnp.where(col < BLOCK, NEG_INF, s)
                sbuf[c] = s
            for c in range(nchunk):
                rows = slice(c * BLOCK, (c + 1) * BLOCK)
                s = sbuf[c]
                mx = jnp.max(s, axis=-1, keepdims=True)
                p = jnp.exp(s - mx).astype(BF16)
                vw = vd[c * BLOCK:(c + 2) * BLOCK, :]
                oj = jnp.dot(p, jnp.where(mj, vw, jnp.ones_like(vw)), preferred_element_type=F32)
                l = pltpu.roll(oj, HEAD_DIM, 1)
                on = oj / l
                ls = mx + jnp.log(l)
                if j == 0:
                    od[rows, :] = on
                    ld[rows, :] = ls
                else:
                    od[rows, :] = jnp.where(mj, on, od[rows, :])
                    ld[rows, :] = jnp.where(mj, ls, ld[rows, :])

        for c in range(nchunk):
            rows = _chunk_rows(c, d, nb)
            o_ref[rows, :] = od[c * BLOCK:(c + 1) * BLOCK, :]
            lse_ref[rows, :] = ld[c * BLOCK:(c + 1) * BLOCK, :]

    ucol = lambda c0: pl.BlockSpec((t, LANES), lambda h: (0, c0 + h))
    vec = pl.BlockSpec((1, LANES), lambda h: (0, 0))
    oblk = pl.BlockSpec((t, LANES), lambda h: (0, h))
    osh = jax.ShapeDtypeStruct((t, hpg * HEAD_DIM), F32)
    return pl.pallas_call(
        body, name=name, grid=(hp,),
        in_specs=[ucol(qc0), ucol(kc0), ucol(vc0), vec, vec,
                  pl.BlockSpec((2, BLOCK, 2 * BLOCK), lambda h: (gi * hp + h, 0, 0))],
        out_specs=(oblk, oblk), out_shape=(osh, osh),
        scratch_shapes=[pltpu.VMEM((t, LANES), BF16), pltpu.VMEM((t + BLOCK, LANES), BF16),
                        pltpu.VMEM((t + BLOCK, LANES), BF16), pltpu.VMEM((t, LANES), F32), pltpu.VMEM((t, LANES), F32),
                        pltpu.VMEM((nchunk, BLOCK, 2 * BLOCK), F32)],
        compiler_params=_cparams(("arbitrary",)),
    )(u, u, u, qg2, kg2, bias)


def _attn_bwd(u, do_g, dd_g, lse_g, qg2, kg2, bias, ds_in, du_in, gi, cols, hpg, name):
    t = u.shape[0]
    d = DILATIONS[gi]
    nchunk = t // BLOCK
    nb = (t // d) // BLOCK
    hp = hpg // 2
    qc0, kc0, vc0 = [(c + gi * hpg * HEAD_DIM) // LANES for c in cols]
    contract_lanes = (((1,), (1,)), ((), ()))
    contract_rows = (((0,), (0,)), ((), ()))
    qscale = HEAD_DIM ** -0.5

    def body(q_ref, k_ref, v_ref, do_ref, dd_ref, lse_ref, qg_ref, kg_ref, bias_ref, dsin_ref, _du_in,
             dgq_ref, dgk_ref, dsout_ref, du_ref,
             qd, kd, vd, dod, ddd, ld, dqd, dkd, dvd, dsacc, pbuf, dsbuf, qs, ks, qst, kst, vst, out_sems):
        h = pl.program_id(0)

        def flush(step):
            return [pltpu.make_async_copy(
                st, du_ref.at[:, pl.ds(pl.multiple_of((c0 + step) * LANES, LANES), LANES)], out_sems.at[i])
                for i, (st, c0) in enumerate(((qst, qc0), (kst, kc0), (vst, vc0)))]

        seg = _segment_ones()
        lane = lax.broadcasted_iota(jnp.int32, (1, LANES), 1)
        qg = qg_ref[...] * qscale
        kg = kg_ref[...]
        kd[0:BLOCK, :] = jnp.zeros((BLOCK, LANES), BF16)
        vd[0:BLOCK, :] = jnp.zeros((BLOCK, LANES), BF16)
        dsacc[...] = jnp.zeros_like(dsacc)
        qs[...] = q_ref[...].astype(F32)
        ks[...] = k_ref[...].astype(F32)
        dqd[...] = v_ref[...].astype(F32)
        for c in range(nchunk):
            rows = _chunk_rows(c, d, nb)
            qv = qs[rows, :]
            kv = ks[rows, :]
            qd[c * BLOCK:(c + 1) * BLOCK, :] = (qv * lax.rsqrt(_head_mean(qv * qv, seg) + EPS) * qg).astype(BF16)
            kd[(c + 1) * BLOCK:(c + 2) * BLOCK, :] = (kv * lax.rsqrt(_head_mean(kv * kv, seg) + EPS) * kg).astype(BF16)
            vd[(c + 1) * BLOCK:(c + 2) * BLOCK, :] = dqd[rows, :].astype(BF16)
            dod[c * BLOCK:(c + 1) * BLOCK, :] = do_ref[rows, :].astype(BF16)
            ddd[c * BLOCK:(c + 1) * BLOCK, :] = dd_ref[rows, :]
            ld[c * BLOCK:(c + 1) * BLOCK, :] = lse_ref[rows, :]

        col = lax.broadcasted_iota(jnp.int32, (BLOCK, 2 * BLOCK), 1)
        for j in range(2):
            mj = jnp.logical_and(lane >= j * HEAD_DIM, lane < (j + 1) * HEAD_DIM)
            first = lane == j * HEAD_DIM
            for c in range(nchunk):
                rows = slice(c * BLOCK, (c + 1) * BLOCK)
                kw = kd[c * BLOCK:(c + 2) * BLOCK, :]
                vw = vd[c * BLOCK:(c + 2) * BLOCK, :]
                kj = jnp.where(mj, kw, jnp.zeros_like(kw))
                vj = jnp.where(mj, vw, jnp.zeros_like(vw))
                s = lax.dot_general(qd[rows, :], kj, contract_lanes, preferred_element_type=F32) + bias_ref[j]
                if c % nb == 0:
                    s = jnp.where(col < BLOCK, NEG_INF, s)
                dp = lax.dot_general(dod[rows, :], vj, contract_lanes, preferred_element_type=F32)
                lse_j = jnp.sum(jnp.where(first, ld[rows, :], 0.0), axis=-1, keepdims=True)
                dd_j = jnp.sum(jnp.where(first, ddd[rows, :], 0.0), axis=-1, keepdims=True)
                p = jnp.exp(s - lse_j)
                ds = p * (dp + dd_j)
                dsacc[j] += ds
                pbuf[j, c] = p.astype(BF16)
                dsbuf[j, c] = ds.astype(BF16)
        for c in range(nchunk):
            rows = slice(c * BLOCK, (c + 1) * BLOCK)
            has_next = c + 1 < nchunk and (c + 1) % nb != 0
            kw = kd[c * BLOCK:(c + 2) * BLOCK, :]
            dq = jnp.zeros((BLOCK, LANES), F32)
            dk = jnp.zeros((BLOCK, LANES), F32)
            dv = jnp.zeros((BLOCK, LANES), F32)
            both = slice(c * BLOCK, (c + 2) * BLOCK) if has_next else rows
            for j in range(2):
                mj = jnp.logical_and(lane >= j * HEAD_DIM, lane < (j + 1) * HEAD_DIM)
                dq = dq + jnp.dot(dsbuf[j, c], jnp.where(mj, kw, jnp.zeros_like(kw)), preferred_element_type=F32)
                dsk = dsbuf[j, c, :, BLOCK:]
                pk = pbuf[j, c, :, BLOCK:]
                if has_next:
                    dsk = jnp.concatenate([dsk, dsbuf[j, c + 1, :, :BLOCK]], axis=0)
                    pk = jnp.concatenate([pk, pbuf[j, c + 1, :, :BLOCK]], axis=0)
                qq = qd[both, :]
                dd = dod[both, :]
                dk = dk + lax.dot_general(dsk, jnp.where(mj, qq, jnp.zeros_like(qq)), contract_rows,
                                          preferred_element_type=F32)
                dv = dv + lax.dot_general(pk, jnp.where(mj, dd, jnp.zeros_like(dd)), contract_rows,
                                          preferred_element_type=F32)
            dqd[rows, :] = dq
            dkd[rows, :] = dk
            dvd[rows, :] = dv

        dsout_ref[...] = dsin_ref[...] + dsacc[...]

        dgq = jnp.zeros((1, LANES), F32)
        dgk = jnp.zeros((1, LANES), F32)
        for c in range(nchunk):
            rows = _chunk_rows(c, d, nb)
            qv = qs[rows, :]
            rq = lax.rsqrt(_head_mean(qv * qv, seg) + EPS)
            qh = qv * rq
            dy = dqd[c * BLOCK:(c + 1) * BLOCK, :]
            dgq = dgq + jnp.sum(dy * qh, axis=0, keepdims=True) * qscale
            dxh = dy * qg
            ddd[rows, :] = rq * (dxh - qh * _head_mean(dxh * qh, seg))
            kv = ks[rows, :]
            rk = lax.rsqrt(_head_mean(kv * kv, seg) + EPS)
            kh = kv * rk
            dy = dkd[c * BLOCK:(c + 1) * BLOCK, :]
            dgk = dgk + jnp.sum(dy * kh, axis=0, keepdims=True)
            dxh = dy * kg
            ld[rows, :] = rk * (dxh - kh * _head_mean(dxh * kh, seg))
        @pl.when(h > 0)
        def _():
            for cp in flush(h - 1):
                cp.wait()

        qst[...] = ddd[...].astype(BF16)
        kst[...] = ld[...].astype(BF16)
        for c in range(nchunk):
            ddd[_chunk_rows(c, d, nb), :] = dvd[c * BLOCK:(c + 1) * BLOCK, :]
        vst[...] = ddd[...].astype(BF16)
        for cp in flush(h):
            cp.start()

        @pl.when(h == hp - 1)
        def _():
            for cp in flush(h):
                cp.wait()

        dgq_ref[0] = dgq
        dgk_ref[0] = dgk

    ucol = lambda c0: pl.BlockSpec((t, LANES), lambda h: (0, c0 + h))
    vec = pl.BlockSpec((1, LANES), lambda h: (0, 0))
    oblk = pl.BlockSpec((t, LANES), lambda h: (0, h))
    bblk = pl.BlockSpec((2, BLOCK, 2 * BLOCK), lambda h: (gi * hp + h, 0, 0))
    gblk = pl.BlockSpec((1, 1, LANES), lambda h: (h, 0, 0))
    gsh = jax.ShapeDtypeStruct((hp, 1, LANES), F32)
    hbm = pl.BlockSpec(memory_space=pl.ANY)
    return pl.pallas_call(
        body, name=name, grid=(hp,),
        in_specs=[ucol(qc0), ucol(kc0), ucol(vc0), oblk, oblk, oblk, vec, vec, bblk, bblk, hbm],
        out_specs=(gblk, gblk, bblk, hbm),
        out_shape=(gsh, gsh, jax.ShapeDtypeStruct(ds_in.shape, F32), jax.ShapeDtypeStruct(du_in.shape, BF16)),
        input_output_aliases={9: 2, 10: 3},
        scratch_shapes=[pltpu.VMEM((t, LANES), BF16), pltpu.VMEM((t + BLOCK, LANES), BF16),
                        pltpu.VMEM((t + BLOCK, LANES), BF16), pltpu.VMEM((t, LANES), BF16),
                        pltpu.VMEM((t, LANES), F32), pltpu.VMEM((t, LANES), F32), pltpu.VMEM((t, LANES), F32),
                        pltpu.VMEM((t, LANES), F32), pltpu.VMEM((t, LANES), F32),
                        pltpu.VMEM((2, BLOCK, 2 * BLOCK), F32), pltpu.VMEM((2, nchunk, BLOCK, 2 * BLOCK), BF16),
                        pltpu.VMEM((2, nchunk, BLOCK, 2 * BLOCK), BF16), pltpu.VMEM((t, LANES), F32),
                        pltpu.VMEM((t, LANES), F32), pltpu.VMEM((t, LANES), BF16), pltpu.VMEM((t, LANES), BF16),
                        pltpu.VMEM((t, LANES), BF16), pltpu.SemaphoreType.DMA((3,))],
        compiler_params=_cparams(("arbitrary",)),
    )(u, u, u, do_g, dd_g, lse_g, qg2, kg2, bias, ds_in, du_in)


def _group_weights(l0, l1, l2):
    mx = jnp.maximum(jnp.maximum(l0, l1), l2)
    e0, e1, e2 = jnp.exp(l0 - mx), jnp.exp(l1 - mx), jnp.exp(l2 - mx)
    inv = 1.0 / (e0 + e1 + e2)
    return e0 * inv, e1 * inv, e2 * inv


def _combine_fwd(os_, lses, name):
    t, ao = os_[0].shape
    ROWS = _rows(t, 26 * ao)

    def body(o0, o1, o2, l0, l1, l2, o_ref):
        w0, w1, w2 = _group_weights(l0[...], l1[...], l2[...])
        o_ref[...] = (w0 * o0[...] + w1 * o1[...] + w2 * o2[...]).astype(BF16)

    row = pl.BlockSpec((ROWS, ao), lambda i: (i, 0))
    return pl.pallas_call(
        body, name=name, grid=(t // ROWS,), in_specs=[row] * 6, out_specs=row,
        out_shape=jax.ShapeDtypeStruct((t, ao), BF16), compiler_params=_cparams(("arbitrary",)),
    )(*os_, *lses)


def _combine_bwd(do, os_, lses, name):
    t, ao = do.shape
    idx = np.arange(ao) // HEAD_DIM
    seg = jnp.asarray((idx[:, None] == idx[None, :]).astype(np.float32), dtype=BF16)
    ROWS = _rows(t, 52 * ao)

    def body(do_ref, o0, o1, o2, l0, l1, l2, seg_ref, g0, g1, g2, d0, d1, d2):
        w0, w1, w2 = _group_weights(l0[...], l1[...], l2[...])
        dov = do_ref[...]
        o = w0 * o0[...] + w1 * o1[...] + w2 * o2[...]
        sd = _segment_sum(dov * o, seg_ref[...])
        for w, gref, dref in ((w0, g0, d0), (w1, g1, d1), (w2, g2, d2)):
            gref[...] = w * dov
            dref[...] = -(w * sd)

    row = pl.BlockSpec((ROWS, ao), lambda i: (i, 0))
    sh = jax.ShapeDtypeStruct((t, ao), F32)
    outs = pl.pallas_call(
        body, name=name, grid=(t // ROWS,), in_specs=[row] * 7 + [pl.BlockSpec((ao, ao), lambda i: (0, 0))],
        out_specs=(row,) * 6, out_shape=(sh,) * 6, compiler_params=_cparams(("arbitrary",)),
    )(do, *os_, *lses, seg)
    return outs[:3], outs[3:]


def _adamw(w, g, m, v, name):
    shape = w.shape
    cols = shape[-1]
    rows = int(np.prod(shape[:-1]))
    tr = rows if rows <= 512 else _tile_rows(rows)
    c1 = 1.0 - ADAM_B1 ** ADAM_STEP
    c2 = 1.0 - ADAM_B2 ** ADAM_STEP

    def body(w_ref, g_ref, m_ref, v_ref, d_ref, nm_ref, nv_ref):
        gv = g_ref[...]
        mn = ADAM_B1 * m_ref[...] + (1.0 - ADAM_B1) * gv
        vn = ADAM_B2 * v_ref[...] + (1.0 - ADAM_B2) * (gv * gv)
        nm_ref[...] = mn
        nv_ref[...] = vn
        d_ref[...] = -ADAM_LR * ((mn / c1) / (jnp.sqrt(vn / c2) + ADAM_EPS) + ADAM_WD * w_ref[...])

    blk = pl.BlockSpec((tr, cols), lambda i: (i, 0))
    sh = jax.ShapeDtypeStruct((rows, cols), F32)
    outs = pl.pallas_call(
        body, name=name, grid=(rows // tr,), in_specs=[blk] * 4, out_specs=(blk,) * 3, out_shape=(sh,) * 3,
        compiler_params=_cparams(("arbitrary",)),
    )(*[a.reshape(rows, cols) for a in (w, g, m, v)])
    return tuple(o.reshape(shape) for o in outs)


def _tile_rows(rows):
    for t in (512, 256, 128, 64, 32, 16, 8):
        if rows % t == 0:
            return t
    return rows


def _sum_slots(recv, parts, me, layers, l, name):
    _, rows, cols = recv.shape
    tr = rows if rows <= 512 else _tile_rows(rows)

    def body(me_ref, r_ref, own_ref, _, o_ref):
        acc = jnp.zeros(o_ref.shape, F32)
        for s in range(N_DEV):
            acc = acc + jnp.where(me_ref[0] == s, own_ref[...], r_ref[s]).astype(F32)
        o_ref[...] = acc

    return pl.pallas_call(
        body, name=name,
        grid_spec=pltpu.PrefetchScalarGridSpec(
            num_scalar_prefetch=1, grid=(rows // tr,),
            in_specs=[pl.BlockSpec((N_DEV, tr, cols), lambda i, me: (0, i, 0)),
                      pl.BlockSpec((None, tr, cols), lambda i, me: (me[0], i, 0)),
                      pl.BlockSpec(memory_space=pl.ANY)],
            out_specs=pl.BlockSpec((None, tr, cols), lambda i, me: (l, i, 0))),
        out_shape=jax.ShapeDtypeStruct(layers.shape, F32), input_output_aliases={3: 0},
        compiler_params=_cparams(("arbitrary",)),
    )(me.reshape(1), recv, parts, layers)


def _peer(k):
    x, y, c = lax.axis_index("x"), lax.axis_index("y"), lax.axis_index("c")
    return (1 - x if k & 4 else x, 1 - y if k & 2 else y, 1 - c if k & 1 else c)


def _dev_index(p):
    return 4 * p[0] + 2 * p[1] + p[2]


HBM_SPEC = pl.BlockSpec(memory_space=pltpu.HBM)
SEM_SPEC = pl.BlockSpec(memory_space=pltpu.SEMAPHORE)
ANY_SPEC = pl.BlockSpec(memory_space=pl.ANY)
CHIPS = (4, 2, 6)


def _remote(src, dst, send_sem, recv_sem, to):
    return pltpu.make_async_remote_copy(src_ref=src, dst_ref=dst, send_sem=send_sem, recv_sem=recv_sem,
                                        device_id=to, device_id_type=MESH)


def _hbm(a):
    return pltpu.with_memory_space_constraint(a, pltpu.HBM)


def _split_call(body, name, bufs, sems_in, sem_out_sizes, after):
    nb, ns, no = len(bufs), len(sems_in), len(sem_out_sizes)
    extra = [] if after is None else [after]

    def kern(*refs):
        pos = nb + ns + len(extra)
        body(refs[:nb], refs[nb:nb + ns], refs[pos:pos + no])
        token_ref = refs[pos + no + nb]
        token_ref[...] = jnp.zeros_like(token_ref)

    out_shape = (tuple(pltpu.SemaphoreType.DMA((s,)) for s in sem_out_sizes)
                 + tuple(pltpu.HBM(b.shape, b.dtype) for b in bufs) + (jax.ShapeDtypeStruct((8, LANES), F32),))
    res = pl.pallas_call(
        kern, name=name, out_shape=out_shape,
        in_specs=[HBM_SPEC] * nb + [SEM_SPEC] * ns + [ANY_SPEC] * len(extra),
        out_specs=(SEM_SPEC,) * no + (HBM_SPEC,) * nb + (pl.BlockSpec(memory_space=pltpu.VMEM),),
        input_output_aliases={i: no + i for i in range(nb)},
        compiler_params=pltpu.CompilerParams(has_side_effects=pltpu.SideEffectType.DATAFLOW_SIDE_EFFECTING),
    )(*bufs, *sems_in, *extra)
    return res[:no], res[no:no + nb], res[no + nb]


def _gather_start(shards, lands, after, name):
    n = len(shards)

    def body(bufs, _, sems):
        ins, lnd = bufs[:n], bufs[n:]
        d2d_s, d2d_r, ici_s, ici_r = sems
        me = _dev_index(_peer(0))
        for j, k in enumerate(CHIPS):
            for i in range(n):
                _remote(ins[i], lnd[i].at[me], ici_s.at[j], ici_r.at[j], _peer(k)).start()
        for i in range(n):
            _remote(ins[i], lnd[i].at[me], d2d_s.at[0], d2d_r.at[0], _peer(1)).start()

    return _split_call(body, name, [_hbm(a) for a in (*shards, *lands)], [], (1, 1, 3, 3), after)


def _gather_forward(n, bufs, ici_r, after, name):
    def body(refs, sems_in, sems):
        ins, lnd = refs[:n], refs[n:]
        (arrived,) = sems_in
        fwd_s, fwd_r = sems
        for j, k in enumerate(CHIPS):
            blk = _dev_index(_peer(k))
            for i in range(n):
                _remote(ins[i], lnd[i].at[blk], fwd_s.at[j], arrived.at[j], _peer(k)).wait_recv()
            for i in range(n):
                _remote(lnd[i].at[blk], lnd[i].at[blk], fwd_s.at[j], fwd_r.at[j], _peer(1)).start()

    return _split_call(body, name, bufs, [ici_r], (3, 3), after)


def _gather_finish(n, bufs, d2d_s, d2d_r, ici_s, fwd_s, fwd_r, after, name):
    def body(refs, sems_in, _):
        ins, lnd = refs[:n], refs[n:]
        d2d_send, d2d_recv, ici_send, fwd_send, fwd_recv = sems_in
        sib = _peer(1)
        for i in range(n):
            cp = _remote(ins[i], lnd[i].at[_dev_index(sib)], d2d_send.at[0], d2d_recv.at[0], sib)
            cp.wait_send()
            cp.wait_recv()
        for j, k in enumerate(CHIPS):
            passed = _dev_index(_peer(k))
            landed = _dev_index(_peer(k | 1))
            for i in range(n):
                _remote(ins[i], lnd[i].at[passed], ici_send.at[j], fwd_recv.at[j], _peer(k)).wait_send()
                cp = _remote(lnd[i].at[passed], lnd[i].at[landed], fwd_send.at[j], fwd_recv.at[j], sib)
                cp.wait_send()
                cp.wait_recv()

    _, out, token = _split_call(body, name, bufs, [d2d_s, d2d_r, ici_s, fwd_s, fwd_r], (), after)
    return out[n:], token


def _exchange_start(parts, lands, after, name):
    n = len(parts)

    def body(bufs, _, sems):
        src, lnd = bufs[:n], bufs[n:]
        send, recv = sems
        me = _dev_index(_peer(0))
        for k in (4, 5, 2, 3, 6, 7, 1):
            to = _peer(k)
            for i in range(n):
                _remote(src[i].at[_dev_index(to)], lnd[i].at[me], send.at[k - 1], recv.at[k - 1], to).start()

    return _split_call(body, name, [_hbm(a) for a in (*parts, *lands)], [], (7, 7), after)


def _exchange_finish(n, bufs, send, recv, after, name):
    def body(refs, sems_in, _):
        src, lnd = refs[:n], refs[n:]
        send_, recv_ = sems_in
        me = _dev_index(_peer(0))
        for k in range(1, N_DEV):
            frm = _peer(k)
            for i in range(n):
                cp = _remote(src[i].at[me], lnd[i].at[_dev_index(frm)], send_.at[k - 1], recv_.at[k - 1], frm)
                cp.wait_send()
                cp.wait_recv()

    _, out, token = _split_call(body, name, bufs, [send, recv], (), after)
    return out[:n], out[n:], token


def _all_reduce_small(v, after, name):
    rows = v.shape[0]

    def body(v_ref, _, o_ref, buf, send_sems, recv_sems):
        me = _dev_index(_peer(0))
        buf[me] = v_ref[...]
        copies = []
        for k in range(1, N_DEV):
            copies.append(pltpu.make_async_remote_copy(
                src_ref=v_ref, dst_ref=buf.at[me], send_sem=send_sems.at[k - 1], recv_sem=recv_sems.at[k - 1],
                device_id=_peer(k), device_id_type=MESH))
        for cp in copies:
            cp.start()
        for k in range(1, N_DEV):
            pltpu.make_async_remote_copy(
                src_ref=v_ref, dst_ref=buf.at[_dev_index(_peer(k))], send_sem=send_sems.at[k - 1],
                recv_sem=recv_sems.at[k - 1], device_id=_peer(k), device_id_type=MESH).wait_recv()
        for cp in copies:
            cp.wait_send()
        acc = buf[0]
        for s in range(1, N_DEV):
            acc = acc + buf[s]
        o_ref[...] = acc

    vm = pl.BlockSpec(memory_space=pltpu.VMEM)
    return pl.pallas_call(
        body, name=name, in_specs=[vm, pl.BlockSpec(memory_space=pl.ANY)], out_specs=vm,
        out_shape=jax.ShapeDtypeStruct(v.shape, F32),
        scratch_shapes=[pltpu.VMEM((N_DEV, rows, LANES), F32), pltpu.SemaphoreType.DMA((7,)),
                        pltpu.SemaphoreType.DMA((7,))],
    )(v, after)


def _columns(cdim, ao):
    q_col = 2 * cdim
    attn_dim = N_GROUPS * ao
    return (q_col, q_col + attn_dim, q_col + 2 * attn_dim), q_col + 3 * attn_dim


def _layer_fwd(x, h1, sm, bg, get_rest, bias, hpg, next_gain):
    cdim = sm["conv_ln_g"].shape[0]
    ao = hpg * HEAD_DIM
    cols, gate_col = _columns(cdim, ao)
    qg2 = jnp.tile(sm["q_norm_g"], 2).reshape(1, LANES)
    kg2 = jnp.tile(sm["k_norm_g"], 2).reshape(1, LANES)
    u = _mm_in_pairs(h1, bg["w_in"], "mm_in")
    zc = _conv_fwd(u, bg["conv_dw_w"], sm["conv_dw_b"], cdim, "conv_fwd")
    zs = _ln_swish_fwd(zc, sm["conv_ln_g"], sm["conv_ln_b"], "ln_swish_fwd")
    os_, lses = [], []
    for gi in range(N_GROUPS):
        o_g, lse_g = _attn_fwd(u, qg2, kg2, bias, gi, cols, hpg, "attn_fwd_g%d" % gi)
        os_.append(o_g)
        lses.append(lse_g)
    o = _combine_fwd(os_, lses, "combine_fwd")
    bg = {**bg, **get_rest(o)}
    mg, yc, ya = _gated_out(zs, o, bg["w_conv_out"], bg["w_attn_out"], u, gate_col, "gated_out")
    x1, h2 = _mm(mg, bg["w_out"], epi="res_rms", extra=x, gain=sm["norm2_g"], name="mm_out")
    f = _mm(h2, bg["w_ff1"], out_dtype=BF16, name="mm_ff1")
    if next_gain is None:
        x2, h_next = _mm(f, bg["w_ff2"], a_relu2=True, epi="res", extra=x1, name="mm_ff2"), None
    else:
        x2, h_next = _mm(f, bg["w_ff2"], a_relu2=True, epi="res_rms", extra=x1, gain=next_gain, name="mm_ff2")
    saved = dict(x=x, h1=h1, u=u, zc=zc, zs=zs, yc=yc, os=os_, lses=lses, o=o, ya=ya, mg=mg, x1=x1, h2=h2, f=f,
                 qg2=qg2, kg2=kg2)
    return x2, h_next, saved, bg


GRAD_GROUPS = (("w_ff2", "w_ff1"), ("w_out", "w_conv_out", "w_attn_out", "conv_dw_w"), ("w_in",))


def _layer_bwd(dx, s, sm, bg, bias, ds_sum, after, emit):
    cdim = sm["conv_ln_g"].shape[0]
    ao = bg["w_attn_out"].shape[0]
    hpg = ao // HEAD_DIM
    cols, gate_col = _columns(cdim, ao)
    g = {}
    df = _mm(dx, bg["w_ff2"], tb=True, epi="drelu2", extra=s["f"], out_dtype=BF16, after=after, name="mm_dff2")
    g["w_ff2"] = _mm(s["f"], dx, ta=True, a_relu2=True, out_dtype=BF16, name="mm_gw_ff2")
    g["w_ff1"] = _mm(s["h2"], df, ta=True, out_dtype=BF16, out_slots=True, name="mm_gw_ff1")
    after = emit(GRAD_GROUPS[0], g)
    dx1, dg2 = _mm(df, bg["w_ff1"], tb=True, epi="rms_bwd", extra=(s["x1"], dx), gain=sm["norm2_g"], after=after,
                   name="mm_dff1")
    g["norm2_g"] = dg2[0]
    dmg = _mm(dx1, bg["w_out"], tb=True, name="mm_dout")
    g["w_out"] = _mm(s["mg"], dx1, ta=True, out_dtype=BF16, name="mm_gw_out")
    dyc, dya, dugc, duga = _gate_bwd(dmg, s["u"], s["yc"], s["ya"], gate_col, "gate_bwd")
    dzs = _mm(dyc, bg["w_conv_out"], tb=True, name="mm_dconv_out")
    g["w_conv_out"] = _mm(s["zs"], dyc, ta=True, out_dtype=BF16, name="mm_gw_conv_out")
    do = _mm(dya, bg["w_attn_out"], tb=True, name="mm_dattn_out")
    g["w_attn_out"] = _mm(s["o"], dya, ta=True, out_dtype=BF16, name="mm_gw_attn_out")
    dzc, dlg, dlb = _ln_swish_bwd(dzs, s["zc"], sm["conv_ln_g"], sm["conv_ln_b"], "ln_swish_bwd")
    g["conv_ln_g"] = dlg[0]
    g["conv_ln_b"] = dlb[0]
    da, dgt, dcw, dcb = _conv_bwd(dzc, s["u"], bg["conv_dw_w"], cdim, "conv_bwd")
    g["conv_dw_w"] = dcw[:CONV_WIDTH].astype(BF16)
    g["conv_dw_b"] = dcb[0]
    after = emit(GRAD_GROUPS[1], g)
    do_gs, dd_gs = _combine_bwd(do, s["os"], s["lses"], "combine_bwd")
    du = lax.empty(s["u"].shape, BF16)
    for col, piece in ((0, da), (cdim, dgt), (gate_col, dugc), (gate_col + dugc.shape[1], duga)):
        du = lax.dynamic_update_slice(du, piece, (0, col))
    gqs, gks = [], []
    for gi in range(N_GROUPS):
        gq, gk, ds_sum, du = _attn_bwd(s["u"], do_gs[gi], dd_gs[gi], s["lses"][gi], s["qg2"], s["kg2"], bias,
                                       ds_sum, du, gi, cols, hpg, "attn_bwd_g%d" % gi)
        gqs.append(gq)
        gks.append(gk)
    g["q_norm_g"] = jnp.concatenate(gqs)
    g["k_norm_g"] = jnp.concatenate(gks)
    g["w_in"] = _mm_gw_in_pairs(s["h1"], du, after, "mm_gw_in")
    after = emit(GRAD_GROUPS[2], g)
    dx0, dg1 = _mm_din_pairs(du, bg["w_in"], s["x"], sm["norm1_g"], dx1, after, "mm_din")
    g["norm1_g"] = dg1[0]
    return dx0, g, ds_sum


BIG = ("w_in", "conv_dw_w", "w_conv_out", "w_attn_out", "w_out", "w_ff1", "w_ff2")
COL_SHARDED = ("w_in", "conv_dw_w", "w_conv_out", "w_attn_out", "w_ff1")
SMALL = ("rel_bias", "norm1_g", "q_norm_g", "k_norm_g", "conv_dw_b", "conv_ln_g", "conv_ln_b", "norm2_g")
WEIGHTS = ("rel_bias", "norm1_g", "w_in", "q_norm_g", "k_norm_g", "conv_dw_w", "conv_dw_b", "conv_ln_g", "conv_ln_b",
           "w_conv_out", "w_attn_out", "w_out", "norm2_g", "w_ff1", "w_ff2")


def _to_whole(name, gathered):
    n, a, b = gathered.shape
    if name in COL_SHARDED:
        return gathered.transpose(1, 0, 2).reshape(a, n * b)
    return gathered.reshape(n * a, b)


def _to_slots(name, whole):
    a, b = whole.shape
    if name in COL_SHARDED:
        return whole.reshape(a, N_DEV, b // N_DEV).transpose(1, 0, 2)
    return whole.reshape(N_DEV, a // N_DEV, b)


def _own_slot(block, me):
    land = lax.empty((N_DEV,) + block.shape, block.dtype)
    return lax.dynamic_update_slice(land, block[None], (me,) + (0,) * block.ndim)


def kernel(x, rel_bias, norm1_g, w_in, q_norm_g, k_norm_g, conv_dw_w, conv_dw_b, conv_ln_g, conv_ln_b, w_conv_out, w_attn_out, w_out, norm2_g, w_ff1, w_ff2, loss_target, m_rel_bias, m_norm1_g, m_w_in, m_q_norm_g, m_k_norm_g, m_conv_dw_w, m_conv_dw_b, m_conv_ln_g, m_conv_ln_b, m_w_conv_out, m_w_attn_out, m_w_out, m_norm2_g, m_w_ff1, m_w_ff2, v_rel_bias, v_norm1_g, v_w_in, v_q_norm_g, v_k_norm_g, v_conv_dw_w, v_conv_dw_b, v_conv_ln_g, v_conv_ln_b, v_w_conv_out, v_w_attn_out, v_w_out, v_norm2_g, v_w_ff1, v_w_ff2):
    w = dict(rel_bias=rel_bias, norm1_g=norm1_g, w_in=w_in, q_norm_g=q_norm_g, k_norm_g=k_norm_g, conv_dw_w=conv_dw_w,
             conv_dw_b=conv_dw_b, conv_ln_g=conv_ln_g, conv_ln_b=conv_ln_b, w_conv_out=w_conv_out,
             w_attn_out=w_attn_out, w_out=w_out, norm2_g=norm2_g, w_ff1=w_ff1, w_ff2=w_ff2)
    mom = dict(rel_bias=m_rel_bias, norm1_g=m_norm1_g, w_in=m_w_in, q_norm_g=m_q_norm_g, k_norm_g=m_k_norm_g,
               conv_dw_w=m_conv_dw_w, conv_dw_b=m_conv_dw_b, conv_ln_g=m_conv_ln_g, conv_ln_b=m_conv_ln_b,
               w_conv_out=m_w_conv_out, w_attn_out=m_w_attn_out, w_out=m_w_out, norm2_g=m_norm2_g, w_ff1=m_w_ff1,
               w_ff2=m_w_ff2)
    var = dict(rel_bias=v_rel_bias, norm1_g=v_norm1_g, w_in=v_w_in, q_norm_g=v_q_norm_g, k_norm_g=v_k_norm_g,
               conv_dw_w=v_conv_dw_w, conv_dw_b=v_conv_dw_b, conv_ln_g=v_conv_ln_g, conv_ln_b=v_conv_ln_b,
               w_conv_out=v_w_conv_out, w_attn_out=v_w_attn_out, w_out=v_w_out, norm2_g=v_norm2_g, w_ff1=v_w_ff1,
               w_ff2=v_w_ff2)

    depth = norm1_g.shape[0]
    me = 4 * lax.axis_index("x") + 2 * lax.axis_index("y") + lax.axis_index("c")
    odd_core = lax.axis_index("c") == 1
    hpg = w_attn_out.shape[1] // HEAD_DIM
    buckets = jnp.asarray(_bucket_table())
    bias = _bias_expand(rel_bias, buckets, hpg, "bias_expand")

    first_names = ("w_in", "conv_dw_w")
    rest_names = tuple(k for k in BIG if k not in first_names)

    def chain_start(l, names, after):
        shards = [w[k][l] if k == "conv_dw_w" else w[k][l].astype(BF16) for k in names]
        if "w_in" in names:
            i = names.index("w_in")
            shards[i] = jnp.where(odd_core, jnp.pad(shards[i], ((0, 0), (SHIFT, 0))),
                                  jnp.pad(shards[i], ((0, 0), (0, SHIFT))))
        sems, bufs, token = _gather_start(shards, [_own_slot(s, me) for s in shards], after,
                                          "gather_start_%s_l%d" % (names[0], l))
        return dict(l=l, names=names, sems=sems, bufs=bufs, token=token)

    def chain_forward(ch, after):
        fwd, bufs, token = _gather_forward(len(ch["names"]), ch["bufs"], ch["sems"][3], after,
                                           "gather_forward_%s_l%d" % (ch["names"][0], ch["l"]))
        ch.update(fwd=fwd, bufs=bufs)
        return token

    def chain_finish(ch, after):
        d2d_s, d2d_r, ici_s, _ = ch["sems"]
        gathered, _ = _gather_finish(len(ch["names"]), ch["bufs"], d2d_s, d2d_r, ici_s, ch["fwd"][0], ch["fwd"][1],
                                     after, "gather_finish_%s_l%d" % (ch["names"][0], ch["l"]))
        out = {k: a if k == "w_in" else _to_whole(k, a) for k, a in zip(ch["names"], gathered)}
        if "conv_dw_w" in out:
            out["conv_dw_w"] = jnp.pad(out["conv_dw_w"], ((0, CONV_TAPS_PADDED - CONV_WIDTH), (0, 0)))
        return out

    xs = x[0]
    h1 = _rms_fwd(xs, norm1_g[0], "rms1_fwd")
    saved, bigs, smalls = [], [], []
    chains = {}
    for l in range(depth):
        sm = {k: w[k][l] for k in SMALL if k != "rel_bias"}
        if l == 0:
            first = chain_start(0, first_names, None)
            token = chain_forward(first, None)
            rest = chain_start(0, rest_names, token)
            bg = chain_finish(first, rest["token"])

            def get_rest(o, rest=rest):
                token = chain_forward(rest, o)
                if depth > 1:
                    chains[1] = (chain_start(1, first_names, token),)
                    chains[1] += (chain_start(1, rest_names, chains[1][0]["token"]),)
                    token = chains[1][1]["token"]
                return chain_finish(rest, token)
        elif l == 1:
            first, rest = chains[1]
            token = chain_forward(first, xs)
            if depth > 2:
                chains[2] = chain_start(2, BIG, token)
                token = chains[2]["token"]
            bg = chain_finish(first, token)

            def get_rest(o, rest=rest):
                return chain_finish(rest, chain_forward(rest, o))
        else:
            token = chain_forward(chains[l], xs)
            if l + 1 < depth:
                chains[l + 1] = chain_start(l + 1, BIG, token)
                token = chains[l + 1]["token"]
            whole = chain_finish(chains[l], token)
            bg = {k: whole[k] for k in first_names}

            def get_rest(o, whole=whole):
                return {k: whole[k] for k in rest_names}
        xs, h1, sv, bg = _layer_fwd(xs, h1, sm, bg, get_rest, bias, hpg, norm1_g[l + 1] if l + 1 < depth else None)
        saved.append(sv)
        bigs.append(bg)
        smalls.append(sm)

    loss, dx = _loss_and_grad(xs, loss_target[0], "loss")

    ds_sum = jnp.zeros((N_GROUPS * hpg, BLOCK, 2 * BLOCK), F32)
    g = {k: [None] * depth for k in SMALL if k != "rel_bias"}
    sums = {k: lax.empty((depth, int(np.prod(w[k].shape[1:-1])), w[k].shape[-1] + (SHIFT if k == "w_in" else 0)), F32)
            for k in BIG}
    pending = []

    def finish_oldest(after):
        names, l, (send, recv), bufs = pending.pop(0)
        parts, recvd, token = _exchange_finish(len(names), bufs, send, recv, after,
                                               "exchange_finish_%s_l%d" % (names[0], l))
        for k, r, p in zip(names, recvd, parts):
            three = (N_DEV, -1, r.shape[-1])
            sums[k] = _sum_slots(r.reshape(three), p.reshape(three), me, sums[k], l, "sum_" + k)
        return token

    def make_emit(l):
        def emit(names, gl):
            parts = [gl[k] if k in ("w_ff1", "w_in") else _to_slots(k, gl[k]) for k in names]
            token = finish_oldest(parts[0]) if len(pending) >= len(GRAD_GROUPS) else None
            lands = [lax.empty(p.shape, p.dtype) for p in parts]
            sems, bufs, token = _exchange_start(parts, lands, token, "exchange_start_%s_l%d" % (names[0], l))
            pending.append((names, l, sems, bufs))
            return token
        return emit

    token = None
    for l in reversed(range(depth)):
        dx, gl, ds_sum = _layer_bwd(dx, saved[l], smalls[l], bigs[l], bias, ds_sum, token, make_emit(l))
        for k in g:
            g[k][l] = gl[k]
        token = None
    grad_x = dx

    g = {k: jnp.stack(v) for k, v in g.items()}
    for k in ("q_norm_g", "k_norm_g"):
        g[k] = jnp.sum(g[k].reshape(depth, -1, HEAD_DIM), axis=1)
    db = _bias_reduce(ds_sum, buckets, hpg, "bias_reduce")
    g["rel_bias"] = db[:, 0, :NUM_BUCKETS].T

    flat = jnp.concatenate([g[k].reshape(-1) for k in SMALL])
    nflat = flat.shape[0]
    rows = -(-nflat // (8 * LANES)) * 8
    packed = jnp.pad(flat, (0, rows * LANES - nflat)).reshape(rows, LANES)
    grad, outs = {}, {}
    token = dx
    while pending:
        names = pending[0][0]
        finish_oldest(token)
        for k in names:
            total = sums[k]
            if k == "w_in":
                total = jnp.where(odd_core, total[..., SHIFT:], total[..., :w_in.shape[-1]])
            grad[k] = total.reshape(w[k].shape)
            outs[k] = _adamw(w[k], grad[k], mom[k], var[k], "adamw_" + k)
            token = outs[k][0]
    total = _all_reduce_small(packed, token, "reduce_small").reshape(-1)
    off = 0
    for k in SMALL:
        size = int(np.prod(w[k].shape))
        grad[k] = total[off:off + size].reshape(w[k].shape)
        outs[k] = _adamw(w[k], grad[k], mom[k], var[k], "adamw_" + k)
        off += size
    loss = lax.psum(loss[0, 0], ("x", "y", "c"))
    return (loss, grad_x[None], *[grad[k] for k in WEIGHTS], *[outs[k][0] for k in WEIGHTS],
            *[outs[k][1] for k in WEIGHTS], *[outs[k][2] for k in WEIGHTS])
```

```python
import functools
import math

import numpy as np
import jax
import jax.numpy as jnp
from jax import lax
from jax.experimental import pallas as pl
from jax.experimental.pallas import tpu as pltpu

F32 = jnp.float32
BF16 = jnp.bfloat16

HEAD_DIM = 64
N_GROUPS = 3
DILATIONS = (1, 4, 16)
SUB_WINDOW = 128
BLOCK = 128
CONV_WIDTH = 31
CONV_TAPS_PADDED = 32
NUM_BUCKETS = 32
MAX_REL_DISTANCE = 2048
EPS = 1e-6
NEG_INF = -1e30
LANES = 128
SUBLANES = 8

ADAM_LR = 0.001
ADAM_B1 = 0.9
ADAM_B2 = 0.999
ADAM_EPS = 1e-08
ADAM_WD = 0.01
ADAM_STEP = 10

N_DEV = 8
VMEM_LIMIT = 56 * 1024 * 1024
MESH = pl.DeviceIdType.MESH


def _cparams(sem=None):
    return pltpu.CompilerParams(dimension_semantics=sem, vmem_limit_bytes=VMEM_LIMIT)


def _tile(n, target):
    if n <= target:
        return n
    t = (target // LANES) * LANES
    while t >= LANES:
        if n % t == 0:
            return t
        t -= LANES
    return n


def _sigmoid(v):
    return 1.0 / (1.0 + jnp.exp(-v))


MM_VMEM_BUDGET = 40 * 1024 * 1024


def _rms_apply(x, g):
    return x * lax.rsqrt(jnp.mean(x * x, axis=-1, keepdims=True) + EPS) * g


def _rms_grad(dh, x, g):
    r = lax.rsqrt(jnp.mean(x * x, axis=-1, keepdims=True) + EPS)
    xh = x * r
    dxh = dh * g
    dx = r * (dxh - xh * jnp.mean(dxh * xh, axis=-1, keepdims=True))
    return dx, jnp.sum(dh * xh, axis=0, keepdims=True)


def _mm_tiles(m, n, kdim, a_bytes, b_bytes, io_bytes, whole_rows=False, temps=2):
    def need(tm, tn, tk):
        blocks = 2 * (tm * tk * a_bytes + tk * tn * b_bytes + tm * tn * io_bytes)
        casts = (tm * tk * 2 if a_bytes == 4 else 0) + (tk * tn * 2 if b_bytes == 4 else 0)
        return blocks + casts + temps * tm * tn * 4

    tn = n if whole_rows else _tile(n, 1024)
    while True:
        fits = [(tm * tk, tm, tk) for tm in {_tile(m, c) for c in (1024, 512, 256, 128)}
                for tk in {_tile(kdim, c) for c in (2048, 1024, 512, 256)} if need(tm, tn, tk) <= MM_VMEM_BUDGET]
        if fits:
            _, tm, tk = max(fits)
            return tm, tn, tk
        assert not whole_rows and tn % 256 == 0, "no block size fits the VMEM budget"
        tn //= 2


def _mm(a, b, *, ta=False, tb=False, out_dtype=F32, epi=None, extra=(), gain=None, after=None, out_slots=False,
        a_relu2=False, name):
    extra = tuple(extra) if isinstance(extra, (tuple, list)) else (extra,)
    m = a.shape[1] if ta else a.shape[0]
    kdim = a.shape[0] if ta else a.shape[1]
    n = b.shape[0] if tb else b.shape[1]
    norm = epi in ("res_rms", "rms_bwd")
    io_bytes = (jnp.dtype(out_dtype).itemsize + sum(e.dtype.itemsize for e in extra) + (2 if epi == "res_rms" else 0))
    tm, tn, tk = _mm_tiles(m, n // N_DEV if out_slots else n, kdim, a.dtype.itemsize, b.dtype.itemsize, io_bytes,
                           whole_rows=norm, temps=6 if norm else 2)
    if out_slots:
        assert epi is None and tn == n // N_DEV
    nk = kdim // tk
    a_spec = pl.BlockSpec((tk, tm), lambda i, j, k: (k, i)) if ta else pl.BlockSpec((tm, tk), lambda i, j, k: (i, k))
    b_spec = pl.BlockSpec((tn, tk), lambda i, j, k: (j, k)) if tb else pl.BlockSpec((tk, tn), lambda i, j, k: (k, j))
    o_spec = (pl.BlockSpec((None, tm, tn), lambda i, j, k: (j, i, 0)) if out_slots
              else pl.BlockSpec((tm, tn), lambda i, j, k: (i, j)))
    v_spec = pl.BlockSpec((1, tn), lambda i, j, k: (0, j))
    dims = (((0 if ta else 1,), (1 if tb else 0,)), ((), ()))
    n_extra = len(extra)
    n_in = 2 + n_extra + (gain is not None) + (after is not None)
    n_out = 2 if norm else 1

    def body(*refs):
        a_ref, b_ref = refs[0], refs[1]
        e_refs = refs[2:2 + n_extra]
        g_ref = refs[2 + n_extra] if gain is not None else None
        outs = refs[n_in:n_in + n_out]

        def product():
            av = a_ref[...]
            if a_relu2:
                r = jnp.maximum(av.astype(F32), 0.0)
                av = r * r
            return lax.dot_general(av.astype(BF16), b_ref[...].astype(BF16), dims, preferred_element_type=F32)

        def finish(acc):
            if epi is None:
                outs[0][...] = acc.astype(outs[0].dtype)
            elif epi == "res":
                outs[0][...] = (e_refs[0][...] + acc).astype(outs[0].dtype)
            elif epi == "drelu2":
                outs[0][...] = (acc * (2.0 * jnp.maximum(e_refs[0][...].astype(F32), 0.0))).astype(outs[0].dtype)
            elif epi == "res_rms":
                x1 = e_refs[0][...] + acc
                outs[0][...] = x1
                outs[1][...] = _rms_apply(x1, g_ref[...]).astype(BF16)
            elif epi == "rms_bwd":
                dx, dg = _rms_grad(acc, e_refs[0][...], g_ref[...])
                outs[0][...] = e_refs[1][...] + dx
                i = pl.program_id(0)

                @pl.when(i == 0)
                def _():
                    outs[1][...] = dg

                @pl.when(i > 0)
                def _():
                    outs[1][...] += dg

        if nk == 1:
            finish(product())
            return
        acc_ref = refs[-1]
        k = pl.program_id(2)

        @pl.when(k == 0)
        def _():
            acc_ref[...] = product()

        @pl.when(jnp.logical_and(k > 0, k < nk - 1))
        def _():
            acc_ref[...] += product()

        @pl.when(k == nk - 1)
        def _():
            finish(acc_ref[...] + product())

    in_specs = ([a_spec, b_spec] + [o_spec] * n_extra + ([v_spec] if gain is not None else [])
                + ([pl.BlockSpec(memory_space=pl.ANY)] if after is not None else []))
    if epi == "res_rms":
        out_shape = (jax.ShapeDtypeStruct((m, n), F32), jax.ShapeDtypeStruct((m, n), BF16))
        out_specs = (o_spec, o_spec)
    elif epi == "rms_bwd":
        out_shape = (jax.ShapeDtypeStruct((m, n), F32), jax.ShapeDtypeStruct((1, n), F32))
        out_specs = (o_spec, v_spec)
    else:
        out_shape = jax.ShapeDtypeStruct((N_DEV, m, tn) if out_slots else (m, n), out_dtype)
        out_specs = o_spec
    args = (a, b) + extra + ((gain.reshape(1, n),) if gain is not None else ()) + ((after,) if after is not None else ())
    return pl.pallas_call(
        body, name=name, grid=(m // tm, n // tn, nk), in_specs=in_specs, out_specs=out_specs, out_shape=out_shape,
        scratch_shapes=[pltpu.VMEM((tm, tn), F32)] if nk > 1 else [],
        compiler_params=_cparams(("arbitrary", "arbitrary", "arbitrary")),
    )(*args)


SHIFT = HEAD_DIM


def _pair_blocks(e, o):
    wp = e.shape[-1]
    return e[:, :wp - LANES], e[:, wp - LANES:] + o[:, :LANES], o[:, LANES:]


def _mm_in_pairs(a, wg, name):
    t, kdim = a.shape
    wp = wg.shape[-1]
    ws = wp - SHIFT
    tm = _tile(t, 1024)

    def body(a_ref, e_ref, o_ref, u_ref):
        av = a_ref[...]
        lo, mid, hi = _pair_blocks(e_ref[...], o_ref[...])
        u_ref[:, :wp - LANES] = jnp.dot(av, lo, preferred_element_type=F32).astype(BF16)
        u_ref[:, wp - LANES:wp] = jnp.dot(av, mid, preferred_element_type=F32).astype(BF16)
        u_ref[:, wp:] = jnp.dot(av, hi, preferred_element_type=F32).astype(BF16)

    return pl.pallas_call(
        body, name=name, grid=(N_DEV // 2, t // tm),
        in_specs=[pl.BlockSpec((tm, kdim), lambda p, i: (i, 0)),
                  pl.BlockSpec((None, kdim, wp), lambda p, i: (2 * p, 0, 0)),
                  pl.BlockSpec((None, kdim, wp), lambda p, i: (2 * p + 1, 0, 0))],
        out_specs=pl.BlockSpec((tm, 2 * ws), lambda p, i: (i, p)),
        out_shape=jax.ShapeDtypeStruct((t, N_DEV * ws), BF16), compiler_params=_cparams(("arbitrary", "arbitrary")),
    )(a, wg, wg)


def _mm_din_pairs(du, wg, x, gain, dres, after, name):
    t = du.shape[0]
    _, kdim, wp = wg.shape
    ws = wp - SHIFT
    tm = _tile(t, 512)
    npair = N_DEV // 2
    lanes = (((1,), (1,)), ((), ()))
    extra = [] if after is None else [after]

    def body(d_ref, e_ref, o_ref, x_ref, g_ref, r_ref, *rest):
        dx_ref, dg_ref, acc_ref = rest[-3], rest[-2], rest[-1]
        i, p = pl.program_id(0), pl.program_id(1)
        lo, mid, hi = _pair_blocks(e_ref[...], o_ref[...])
        part = (lax.dot_general(d_ref[:, :wp - LANES], lo, lanes, preferred_element_type=F32)
                + lax.dot_general(d_ref[:, wp - LANES:wp], mid, lanes, preferred_element_type=F32)
                + lax.dot_general(d_ref[:, wp:], hi, lanes, preferred_element_type=F32))

        @pl.when(p == 0)
        def _():
            acc_ref[...] = part

        @pl.when(jnp.logical_and(p > 0, p < npair - 1))
        def _():
            acc_ref[...] += part

        @pl.when(p == npair - 1)
        def _():
            dx, dg = _rms_grad(acc_ref[...] + part, x_ref[...], g_ref[...])
            dx_ref[...] = r_ref[...] + dx

            @pl.when(i == 0)
            def _():
                dg_ref[...] = dg

            @pl.when(i > 0)
            def _():
                dg_ref[...] += dg

    row = pl.BlockSpec((tm, kdim), lambda i, p: (i, 0))
    vec = pl.BlockSpec((1, kdim), lambda i, p: (0, 0))
    return pl.pallas_call(
        body, name=name, grid=(t // tm, npair),
        in_specs=[pl.BlockSpec((tm, 2 * ws), lambda i, p: (i, p)),
                  pl.BlockSpec((None, kdim, wp), lambda i, p: (2 * p, 0, 0)),
                  pl.BlockSpec((None, kdim, wp), lambda i, p: (2 * p + 1, 0, 0)), row, vec, row]
        + [pl.BlockSpec(memory_space=pl.ANY)] * len(extra),
        out_specs=(row, vec),
        out_shape=(jax.ShapeDtypeStruct((t, kdim), F32), jax.ShapeDtypeStruct((1, kdim), F32)),
        scratch_shapes=[pltpu.VMEM((tm, kdim), F32)], compiler_params=_cparams(("arbitrary", "arbitrary")),
    )(du, wg, wg, x, gain.reshape(1, kdim), dres, *extra)


def _mm_gw_in_pairs(h, du, after, name):
    t, kdim = h.shape
    ws = du.shape[1] // N_DEV
    wp = ws + SHIFT
    tm = _tile(kdim, 512)
    rows = (((0,), (0,)), ((), ()))
    extra = [] if after is None else [after]

    def body(h_ref, d_ref, *rest):
        g_ref = rest[-1]
        g = lax.dot_general(h_ref[...], d_ref[...], rows, preferred_element_type=F32)
        g_ref[0] = g[:, :wp].astype(BF16)
        g_ref[1] = g[:, wp - LANES:].astype(BF16)

    return pl.pallas_call(
        body, name=name, grid=(N_DEV // 2, kdim // tm),
        in_specs=[pl.BlockSpec((t, tm), lambda p, i: (0, i)), pl.BlockSpec((t, 2 * ws), lambda p, i: (0, p))]
        + [pl.BlockSpec(memory_space=pl.ANY)] * len(extra),
        out_specs=pl.BlockSpec((2, tm, wp), lambda p, i: (p, i, 0)),
        out_shape=jax.ShapeDtypeStruct((N_DEV, kdim, wp), BF16), compiler_params=_cparams(("arbitrary", "arbitrary")),
    )(h, du, *extra)


ROW_BLOCK_BUDGET = 24 * 1024 * 1024


def _rows(t, row_bytes):
    rows = t
    while rows > 8 and (2 * rows * row_bytes > ROW_BLOCK_BUDGET or t % rows):
        rows //= 2
    return rows


def _rms_fwd(x, g, name):
    t, d = x.shape
    ROWS = _rows(t, 6 * d)

    def body(x_ref, g_ref, h_ref):
        h_ref[...] = _rms_apply(x_ref[...], g_ref[...]).astype(BF16)

    return pl.pallas_call(
        body, name=name, grid=(t // ROWS,),
        in_specs=[pl.BlockSpec((ROWS, d), lambda i: (i, 0)), pl.BlockSpec((1, d), lambda i: (0, 0))],
        out_specs=pl.BlockSpec((ROWS, d), lambda i: (i, 0)),
        out_shape=jax.ShapeDtypeStruct((t, d), BF16), compiler_params=_cparams(("arbitrary",)),
    )(x, g.reshape(1, d))


def _gated_out(zs, o, wc, wa, u, gate_col, name):
    t = zs.shape[0]
    d = wc.shape[1]
    td = math.gcd(_tile(d, 512), gate_col)
    nd = d // td
    c0 = gate_col // td
    tm = _tile(t, 1024)

    def body(zs_ref, o_ref, wc_ref, wa_ref, gc_ref, ga_ref, m_ref, yc_ref, ya_ref):
        yc = jnp.dot(zs_ref[...], wc_ref[...], preferred_element_type=F32)
        ya = jnp.dot(o_ref[...], wa_ref[...], preferred_element_type=F32)
        gc = _sigmoid(gc_ref[...].astype(F32))
        ga = _sigmoid(ga_ref[...].astype(F32))
        m_ref[...] = (gc * yc + ga * ya).astype(BF16)
        yc_ref[...] = yc.astype(BF16)
        ya_ref[...] = ya.astype(BF16)

    blk = pl.BlockSpec((tm, td), lambda i, j: (i, j))
    sh = jax.ShapeDtypeStruct((t, d), BF16)
    return pl.pallas_call(
        body, name=name, grid=(t // tm, nd),
        in_specs=[pl.BlockSpec((tm, zs.shape[1]), lambda i, j: (i, 0)), pl.BlockSpec((tm, o.shape[1]), lambda i, j: (i, 0)),
                  pl.BlockSpec((wc.shape[0], td), lambda i, j: (0, j)), pl.BlockSpec((wa.shape[0], td), lambda i, j: (0, j)),
                  pl.BlockSpec((tm, td), lambda i, j: (i, c0 + j)), pl.BlockSpec((tm, td), lambda i, j: (i, c0 + nd + j))],
        out_specs=(blk, blk, blk), out_shape=(sh, sh, sh), compiler_params=_cparams(("arbitrary", "arbitrary")),
    )(zs, o, wc, wa, u, u)


def _gate_bwd(dm, u, yc, ya, gate_col, name):
    t, d = yc.shape
    td = math.gcd(_tile(d, 512), gate_col)
    nd = d // td
    c0 = gate_col // td
    ROWS = _rows(t, 20 * td)

    def body(dm_ref, gc_ref, ga_ref, yc_ref, ya_ref, dyc_ref, dya_ref, dugc_ref, duga_ref):
        dmv = dm_ref[...]
        gc = _sigmoid(gc_ref[...].astype(F32))
        ga = _sigmoid(ga_ref[...].astype(F32))
        dyc_ref[...] = (dmv * gc).astype(BF16)
        dya_ref[...] = (dmv * ga).astype(BF16)
        dugc_ref[...] = (dmv * yc_ref[...].astype(F32) * gc * (1.0 - gc)).astype(BF16)
        duga_ref[...] = (dmv * ya_ref[...].astype(F32) * ga * (1.0 - ga)).astype(BF16)

    blk = pl.BlockSpec((ROWS, td), lambda i, j: (i, j))
    o = jax.ShapeDtypeStruct((t, d), BF16)
    return pl.pallas_call(
        body, name=name, grid=(t // ROWS, nd),
        in_specs=[blk, pl.BlockSpec((ROWS, td), lambda i, j: (i, c0 + j)),
                  pl.BlockSpec((ROWS, td), lambda i, j: (i, c0 + nd + j)), blk, blk],
        out_specs=(blk, blk, blk, blk), out_shape=(o, o, o, o),
        compiler_params=_cparams(("arbitrary", "arbitrary")),
    )(dm, u, u, yc, ya)


def _loss_and_grad(y, target, name):
    t, d = y.shape
    ROWS = _rows(t, 12 * d)
    n = t // ROWS

    def body(y_ref, t_ref, loss_ref, dy_ref, acc_ref):
        i = pl.program_id(0)

        @pl.when(i == 0)
        def _():
            acc_ref[...] = jnp.zeros_like(acc_ref)

        diff = y_ref[...] - t_ref[...]
        dy_ref[...] = diff * (1.0 / d)
        acc_ref[...] += jnp.sum(diff * diff, axis=0, keepdims=True)

        @pl.when(i == n - 1)
        def _():
            loss_ref[...] = jnp.sum(acc_ref[...], axis=-1, keepdims=True) * (0.5 / d)

    row = pl.BlockSpec((ROWS, d), lambda i: (i, 0))
    return pl.pallas_call(
        body, name=name, grid=(n,), in_specs=[row, row],
        out_specs=(pl.BlockSpec((1, 1), lambda i: (0, 0)), row),
        out_shape=(jax.ShapeDtypeStruct((1, 1), F32), jax.ShapeDtypeStruct((t, d), F32)),
        scratch_shapes=[pltpu.VMEM((1, d), F32)], compiler_params=_cparams(("arbitrary",)),
    )(y, target)


HALO = 32


def _conv_fwd(u, w, b, cdim, name):
    t = u.shape[0]
    ncb = cdim // LANES
    nt = t // BLOCK

    def body(a_ref, g_ref, w_ref, b_ref, zc_ref, zpad):
        zpad[0:HALO, :] = jnp.zeros((HALO, LANES), F32)
        zpad[HALO:HALO + t, :] = a_ref[...].astype(F32) * _sigmoid(g_ref[...].astype(F32))
        wv = w_ref[...]
        bv = b_ref[...]

        def tile(i, carry):
            r0 = pl.multiple_of(i * BLOCK, BLOCK)
            win = zpad[pl.ds(r0, BLOCK + HALO), :]
            acc = jnp.zeros((BLOCK, LANES), F32) + bv
            for b in range(SUBLANES):
                sh = win if b == 0 else pltpu.roll(win, b, 0)
                for a in range(HALO // SUBLANES):
                    j = CONV_WIDTH - 1 - (SUBLANES * a + b)
                    if j >= 0:
                        lo = HALO - SUBLANES * a
                        acc = acc + wv[j:j + 1, :] * sh[lo:lo + BLOCK, :]
            zc_ref[pl.ds(r0, BLOCK), :] = acc
            return carry

        lax.fori_loop(0, nt, tile, 0)

    col = lambda off: pl.BlockSpec((t, LANES), lambda c: (0, off + c))
    return pl.pallas_call(
        body, name=name, grid=(ncb,),
        in_specs=[col(0), col(ncb), pl.BlockSpec((CONV_TAPS_PADDED, LANES), lambda c: (0, c)),
                  pl.BlockSpec((1, LANES), lambda c: (0, c))],
        out_specs=pl.BlockSpec((t, LANES), lambda c: (0, c)),
        out_shape=jax.ShapeDtypeStruct((t, cdim), F32),
        scratch_shapes=[pltpu.VMEM((t + HALO, LANES), F32)], compiler_params=_cparams(("arbitrary",)),
    )(u, u, w, b.reshape(1, cdim))


def _conv_bwd(dzc, u, w, cdim, name):
    t = u.shape[0]
    ncb = cdim // LANES
    nt = t // BLOCK
    win_rows = BLOCK + HALO

    def body(dzc_ref, a_ref, g_ref, w_ref, da_ref, dg_ref, dw_ref, db_ref, zpad, dpad):
        av = a_ref[...].astype(F32)
        sg = _sigmoid(g_ref[...].astype(F32))
        zpad[0:HALO, :] = jnp.zeros((HALO, LANES), F32)
        zpad[HALO:HALO + t, :] = av * sg
        dpad[0:t, :] = dzc_ref[...]
        dpad[t:t + HALO, :] = jnp.zeros((HALO, LANES), F32)
        dw_ref[...] = jnp.zeros_like(dw_ref)
        db_ref[...] = jnp.sum(dzc_ref[...], axis=0, keepdims=True)
        wv = w_ref[...]

        def tile(i, carry):
            r0 = pl.multiple_of(i * BLOCK, BLOCK)
            zwin = zpad[pl.ds(r0, win_rows), :]
            dwin = dpad[pl.ds(r0, win_rows), :]
            dcur = dwin[0:BLOCK, :]
            dz = jnp.zeros((BLOCK, LANES), F32)
            for b in range(SUBLANES):
                zs = zwin if b == 0 else pltpu.roll(zwin, b, 0)
                ds = dwin if b == 0 else pltpu.roll(dwin, win_rows - b, 0)
                for a in range(HALO // SUBLANES):
                    j = CONV_WIDTH - 1 - (SUBLANES * a + b)
                    if j >= 0:
                        lo = HALO - SUBLANES * a
                        dw_ref[j:j + 1, :] += jnp.sum(dcur * zs[lo:lo + BLOCK, :], axis=0, keepdims=True)
                        dz = dz + wv[j:j + 1, :] * ds[SUBLANES * a:SUBLANES * a + BLOCK, :]
            ac = a_ref[pl.ds(r0, BLOCK), :].astype(F32)
            sc = _sigmoid(g_ref[pl.ds(r0, BLOCK), :].astype(F32))
            da_ref[pl.ds(r0, BLOCK), :] = (dz * sc).astype(BF16)
            dg_ref[pl.ds(r0, BLOCK), :] = (dz * ac * sc * (1.0 - sc)).astype(BF16)
            return carry

        lax.fori_loop(0, nt, tile, 0)

    col = lambda off: pl.BlockSpec((t, LANES), lambda c: (0, off + c))
    wspec = pl.BlockSpec((CONV_TAPS_PADDED, LANES), lambda c: (0, c))
    o = jax.ShapeDtypeStruct((t, cdim), BF16)
    return pl.pallas_call(
        body, name=name, grid=(ncb,), in_specs=[col(0), col(0), col(ncb), wspec],
        out_specs=(col(0), col(0), wspec, pl.BlockSpec((1, LANES), lambda c: (0, c))),
        out_shape=(o, o, jax.ShapeDtypeStruct((CONV_TAPS_PADDED, cdim), F32), jax.ShapeDtypeStruct((1, cdim), F32)),
        scratch_shapes=[pltpu.VMEM((t + HALO, LANES), F32), pltpu.VMEM((t + HALO, LANES), F32)],
        compiler_params=_cparams(("arbitrary",)),
    )(dzc, u, u, w)


def _ln_swish_fwd(zc, g, b, name):
    t, c = zc.shape
    ROWS = _rows(t, 6 * c)

    def body(z_ref, g_ref, b_ref, o_ref):
        z = z_ref[...]
        mu = jnp.mean(z, axis=-1, keepdims=True)
        zc_ = z - mu
        zn = zc_ * lax.rsqrt(jnp.mean(zc_ * zc_, axis=-1, keepdims=True) + EPS)
        y = zn * g_ref[...] + b_ref[...]
        o_ref[...] = (y * _sigmoid(y)).astype(BF16)

    row = pl.BlockSpec((ROWS, c), lambda i: (i, 0))
    vec = pl.BlockSpec((1, c), lambda i: (0, 0))
    return pl.pallas_call(
        body, name=name, grid=(t // ROWS,), in_specs=[row, vec, vec], out_specs=row,
        out_shape=jax.ShapeDtypeStruct((t, c), BF16), compiler_params=_cparams(("arbitrary",)),
    )(zc, g.reshape(1, c), b.reshape(1, c))


def _ln_swish_bwd(dzs, zc, g, b, name):
    t, c = zc.shape
    ROWS = _rows(t, 12 * c)

    def body(d_ref, z_ref, g_ref, b_ref, dz_ref, dg_ref, db_ref):
        @pl.when(pl.program_id(0) == 0)
        def _():
            dg_ref[...] = jnp.zeros_like(dg_ref)
            db_ref[...] = jnp.zeros_like(db_ref)

        z = z_ref[...]
        mu = jnp.mean(z, axis=-1, keepdims=True)
        zc_ = z - mu
        rstd = lax.rsqrt(jnp.mean(zc_ * zc_, axis=-1, keepdims=True) + EPS)
        zn = zc_ * rstd
        y = zn * g_ref[...] + b_ref[...]
        sg = _sigmoid(y)
        dy = d_ref[...] * (sg * (1.0 + y * (1.0 - sg)))
        dg_ref[...] += jnp.sum(dy * zn, axis=0, keepdims=True)
        db_ref[...] += jnp.sum(dy, axis=0, keepdims=True)
        dzn = dy * g_ref[...]
        dz_ref[...] = rstd * (dzn - jnp.mean(dzn, axis=-1, keepdims=True)
                              - zn * jnp.mean(dzn * zn, axis=-1, keepdims=True))

    row = pl.BlockSpec((ROWS, c), lambda i: (i, 0))
    vec = pl.BlockSpec((1, c), lambda i: (0, 0))
    v = jax.ShapeDtypeStruct((1, c), F32)
    return pl.pallas_call(
        body, name=name, grid=(t // ROWS,), in_specs=[row, row, vec, vec], out_specs=(row, vec, vec),
        out_shape=(jax.ShapeDtypeStruct((t, c), F32), v, v), compiler_params=_cparams(("arbitrary",)),
    )(dzs, zc, g.reshape(1, c), b.reshape(1, c))


def _bucket_table():
    qi = np.arange(BLOCK)[:, None]
    kj = np.arange(2 * BLOCK)[None, :]
    off = qi + BLOCK - kj
    band = (off >= 0) & (off <= SUB_WINDOW)
    max_exact = NUM_BUCKETS // 2
    out = []
    for d in DILATIONS:
        dist = (np.clip(off, 0, SUB_WINDOW) * d).astype(np.int32)
        nf = np.maximum(dist, 1).astype(np.float32)
        large = max_exact + (np.log(nf / np.float32(max_exact)) / np.float32(math.log(MAX_REL_DISTANCE / max_exact))
                             * np.float32(NUM_BUCKETS - max_exact)).astype(np.int32)
        large = np.minimum(large, NUM_BUCKETS - 1)
        bucket = np.where(dist < max_exact, dist, large)
        out.append(np.where(band, bucket, -1))
    return np.stack(out).astype(np.int32)


def _bias_expand(rel_bias, buckets, hpg, name):
    nh = N_GROUPS * hpg

    def body(rb_ref, bk_ref, o_ref):
        h = pl.program_id(0)
        bk = bk_ref[0]
        acc = jnp.full((BLOCK, 2 * BLOCK), NEG_INF, F32)
        for bb in range(NUM_BUCKETS):
            acc = jnp.where(bk == bb, rb_ref[bb, h], acc)
        o_ref[0] = acc

    return pl.pallas_call(
        body, name=name, grid=(nh,),
        in_specs=[pl.BlockSpec(memory_space=pltpu.SMEM),
                  pl.BlockSpec((1, BLOCK, 2 * BLOCK), lambda h: (h // hpg, 0, 0))],
        out_specs=pl.BlockSpec((1, BLOCK, 2 * BLOCK), lambda h: (h, 0, 0)),
        out_shape=jax.ShapeDtypeStruct((nh, BLOCK, 2 * BLOCK), F32), compiler_params=_cparams(("arbitrary",)),
    )(rel_bias, buckets)


def _bias_reduce(ds_sum, buckets, hpg, name):
    nh = N_GROUPS * hpg

    def body(ds_ref, bk_ref, o_ref):
        bk = bk_ref[0]
        dsv = ds_ref[0]
        lane = lax.broadcasted_iota(jnp.int32, (1, LANES), 1)
        row = jnp.zeros((1, LANES), F32)
        for bb in range(NUM_BUCKETS):
            tot = jnp.sum(jnp.sum(jnp.where(bk == bb, dsv, 0.0), axis=-1, keepdims=True), axis=0, keepdims=True)
            row = jnp.where(lane == bb, tot, row)
        o_ref[0] = row

    return pl.pallas_call(
        body, name=name, grid=(nh,),
        in_specs=[pl.BlockSpec((1, BLOCK, 2 * BLOCK), lambda h: (h, 0, 0)),
                  pl.BlockSpec((1, BLOCK, 2 * BLOCK), lambda h: (h // hpg, 0, 0))],
        out_specs=pl.BlockSpec((1, 1, LANES), lambda h: (h, 0, 0)),
        out_shape=jax.ShapeDtypeStruct((nh, 1, LANES), F32), compiler_params=_cparams(("arbitrary",)),
    )(ds_sum, buckets)


def _chunk_rows(c, d, nb):
    r, n = c // nb, c % nb
    if d == 1:
        return pl.ds(c * BLOCK, BLOCK)
    return pl.ds(r + n * BLOCK * d, BLOCK, stride=d)


def _segment_ones():
    i = lax.broadcasted_iota(jnp.int32, (LANES, LANES), 0) // HEAD_DIM
    j = lax.broadcasted_iota(jnp.int32, (LANES, LANES), 1) // HEAD_DIM
    return (i == j).astype(BF16)


def _segment_sum(v, seg):
    hi = v.astype(BF16)
    lo = (v - hi.astype(F32)).astype(BF16)
    return jnp.dot(hi, seg, preferred_element_type=F32) + jnp.dot(lo, seg, preferred_element_type=F32)


def _head_mean(v, seg):
    return _segment_sum(v, seg) * (1.0 / HEAD_DIM)


def _attn_fwd(u, qg2, kg2, bias, gi, cols, hpg, name):
    t = u.shape[0]
    d = DILATIONS[gi]
    nchunk = t // BLOCK
    nb = (t // d) // BLOCK
    hp = hpg // 2
    qc0, kc0, vc0 = [(c + gi * hpg * HEAD_DIM) // LANES for c in cols]
    contract_lanes = (((1,), (1,)), ((), ()))

    def body(q_ref, k_ref, v_ref, qg_ref, kg_ref, bias_ref, o_ref, lse_ref, qd, kd, vd, od, ld, sbuf):
        seg = _segment_ones()
        lane = lax.broadcasted_iota(jnp.int32, (1, LANES), 1)
        qg = qg_ref[...] * (HEAD_DIM ** -0.5)
        kg = kg_ref[...]
        kd[0:BLOCK, :] = jnp.zeros((BLOCK, LANES), BF16)
        vd[0:BLOCK, :] = jnp.zeros((BLOCK, LANES), BF16)
        od[...] = q_ref[...].astype(F32)
        ld[...] = k_ref[...].astype(F32)
        for c in range(nchunk):
            rows = _chunk_rows(c, d, nb)
            qv = od[rows, :]
            kv = ld[rows, :]
            qd[c * BLOCK:(c + 1) * BLOCK, :] = (qv * lax.rsqrt(_head_mean(qv * qv, seg) + EPS) * qg).astype(BF16)
            kd[(c + 1) * BLOCK:(c + 2) * BLOCK, :] = (kv * lax.rsqrt(_head_mean(kv * kv, seg) + EPS) * kg).astype(BF16)
        od[...] = v_ref[...].astype(F32)
        for c in range(nchunk):
            vd[(c + 1) * BLOCK:(c + 2) * BLOCK, :] = od[_chunk_rows(c, d, nb), :].astype(BF16)

        col = lax.broadcasted_iota(jnp.int32, (BLOCK, 2 * BLOCK), 1)
        for j in range(2):
            mj = jnp.logical_and(lane >= j * HEAD_DIM, lane < (j + 1) * HEAD_DIM)
            for c in range(nchunk):
                kw = kd[c * BLOCK:(c + 2) * BLOCK, :]
                kj = jnp.where(mj, kw, jnp.zeros_like(kw))
                s = lax.dot_general(qd[c * BLOCK:(c + 1) * BLOCK, :], kj, contract_lanes,
                                    preferred_element_type=F32) + bias_ref[j]
                if c % nb == 0:
                    s = jnp.where(col < BLOCK, NEG_INF, s)
                sbuf[c] = s
            for c in range(nchunk):
                rows = slice(c * BLOCK, (c + 1) * BLOCK)
                s = sbuf[c]
                mx = jnp.max(s, axis=-1, keepdims=True)
                p = jnp.exp(s - mx).astype(BF16)
                vw = vd[c * BLOCK:(c + 2) * BLOCK, :]
                oj = jnp.dot(p, jnp.where(mj, vw, jnp.ones_like(vw)), preferred_element_type=F32)
                l = pltpu.roll(oj, HEAD_DIM, 1)
                on = oj / l
                ls = mx + jnp.log(l)
                if j == 0:
                    od[rows, :] = on
                    ld[rows, :] = ls
                else:
                    od[rows, :] = jnp.where(mj, on, od[rows, :])
                    ld[rows, :] = jnp.where(mj, ls, ld[rows, :])

        for c in range(nchunk):
            rows = _chunk_rows(c, d, nb)
            o_ref[rows, :] = od[c * BLOCK:(c + 1) * BLOCK, :]
            lse_ref[rows, :] = ld[c * BLOCK:(c + 1) * BLOCK, :]

    ucol = lambda c0: pl.BlockSpec((t, LANES), lambda h: (0, c0 + h))
    vec = pl.BlockSpec((1, LANES), lambda h: (0, 0))
    oblk = pl.BlockSpec((t, LANES), lambda h: (0, h))
    osh = jax.ShapeDtypeStruct((t, hpg * HEAD_DIM), F32)
    return pl.pallas_call(
        body, name=name, grid=(hp,),
        in_specs=[ucol(qc0), ucol(kc0), ucol(vc0), vec, vec,
                  pl.BlockSpec((2, BLOCK, 2 * BLOCK), lambda h: (gi * hp + h, 0, 0))],
        out_specs=(oblk, oblk), out_shape=(osh, osh),
        scratch_shapes=[pltpu.VMEM((t, LANES), BF16), pltpu.VMEM((t + BLOCK, LANES), BF16),
                        pltpu.VMEM((t + BLOCK, LANES), BF16), pltpu.VMEM((t, LANES), F32), pltpu.VMEM((t, LANES), F32),
                        pltpu.VMEM((nchunk, BLOCK, 2 * BLOCK), F32)],
        compiler_params=_cparams(("arbitrary",)),
    )(u, u, u, qg2, kg2, bias)


def _attn_bwd(u, do_g, dd_g, lse_g, qg2, kg2, bias, ds_in, du_in, gi, cols, hpg, name):
    t = u.shape[0]
    d = DILATIONS[gi]
    nchunk = t // BLOCK
    nb = (t // d) // BLOCK
    hp = hpg // 2
    qc0, kc0, vc0 = [(c + gi * hpg * HEAD_DIM) // LANES for c in cols]
    contract_lanes = (((1,), (1,)), ((), ()))
    contract_rows = (((0,), (0,)), ((), ()))
    qscale = HEAD_DIM ** -0.5

    def body(q_ref, k_ref, v_ref, do_ref, dd_ref, lse_ref, qg_ref, kg_ref, bias_ref, dsin_ref, _du_in,
             dgq_ref, dgk_ref, dsout_ref, du_ref,
             qd, kd, vd, dod, ddd, ld, dqd, dkd, dvd, dsacc, pbuf, dsbuf, qs, ks, qst, kst, vst, out_sems):
        h = pl.program_id(0)

        def flush(step):
            return [pltpu.make_async_copy(
                st, du_ref.at[:, pl.ds(pl.multiple_of((c0 + step) * LANES, LANES), LANES)], out_sems.at[i])
                for i, (st, c0) in enumerate(((qst, qc0), (kst, kc0), (vst, vc0)))]

        seg = _segment_ones()
        lane = lax.broadcasted_iota(jnp.int32, (1, LANES), 1)
        qg = qg_ref[...] * qscale
        kg = kg_ref[...]
        kd[0:BLOCK, :] = jnp.zeros((BLOCK, LANES), BF16)
        vd[0:BLOCK, :] = jnp.zeros((BLOCK, LANES), BF16)
        dsacc[...] = jnp.zeros_like(dsacc)
        qs[...] = q_ref[...].astype(F32)
        ks[...] = k_ref[...].astype(F32)
        dqd[...] = v_ref[...].astype(F32)
        for c in range(nchunk):
            rows = _chunk_rows(c, d, nb)
            qv = qs[rows, :]
            kv = ks[rows, :]
            qd[c * BLOCK:(c + 1) * BLOCK, :] = (qv * lax.rsqrt(_head_mean(qv * qv, seg) + EPS) * qg).astype(BF16)
            kd[(c + 1) * BLOCK:(c + 2) * BLOCK, :] = (kv * lax.rsqrt(_head_mean(kv * kv, seg) + EPS) * kg).astype(BF16)
            vd[(c + 1) * BLOCK:(c + 2) * BLOCK, :] = dqd[rows, :].astype(BF16)
            dod[c * BLOCK:(c + 1) * BLOCK, :] = do_ref[rows, :].astype(BF16)
            ddd[c * BLOCK:(c + 1) * BLOCK, :] = dd_ref[rows, :]
            ld[c * BLOCK:(c + 1) * BLOCK, :] = lse_ref[rows, :]

        col = lax.broadcasted_iota(jnp.int32, (BLOCK, 2 * BLOCK), 1)
        for j in range(2):
            mj = jnp.logical_and(lane >= j * HEAD_DIM, lane < (j + 1) * HEAD_DIM)
            first = lane == j * HEAD_DIM
            for c in range(nchunk):
                rows = slice(c * BLOCK, (c + 1) * BLOCK)
                kw = kd[c * BLOCK:(c + 2) * BLOCK, :]
                vw = vd[c * BLOCK:(c + 2) * BLOCK, :]
                kj = jnp.where(mj, kw, jnp.zeros_like(kw))
                vj = jnp.where(mj, vw, jnp.zeros_like(vw))
                s = lax.dot_general(qd[rows, :], kj, contract_lanes, preferred_element_type=F32) + bias_ref[j]
                if c % nb == 0:
                    s = jnp.where(col < BLOCK, NEG_INF, s)
                dp = lax.dot_general(dod[rows, :], vj, contract_lanes, preferred_element_type=F32)
                lse_j = jnp.sum(jnp.where(first, ld[rows, :], 0.0), axis=-1, keepdims=True)
                dd_j = jnp.sum(jnp.where(first, ddd[rows, :], 0.0), axis=-1, keepdims=True)
                p = jnp.exp(s - lse_j)
                ds = p * (dp + dd_j)
                dsacc[j] += ds
                pbuf[j, c] = p.astype(BF16)
                dsbuf[j, c] = ds.astype(BF16)
        for c in range(nchunk):
            rows = slice(c * BLOCK, (c + 1) * BLOCK)
            has_next = c + 1 < nchunk and (c + 1) % nb != 0
            kw = kd[c * BLOCK:(c + 2) * BLOCK, :]
            dq = jnp.zeros((BLOCK, LANES), F32)
            dk = jnp.zeros((BLOCK, LANES), F32)
            dv = jnp.zeros((BLOCK, LANES), F32)
            both = slice(c * BLOCK, (c + 2) * BLOCK) if has_next else rows
            for j in range(2):
                mj = jnp.logical_and(lane >= j * HEAD_DIM, lane < (j + 1) * HEAD_DIM)
                dq = dq + jnp.dot(dsbuf[j, c], jnp.where(mj, kw, jnp.zeros_like(kw)), preferred_element_type=F32)
                dsk = dsbuf[j, c, :, BLOCK:]
                pk = pbuf[j, c, :, BLOCK:]
                if has_next:
                    dsk = jnp.concatenate([dsk, dsbuf[j, c + 1, :, :BLOCK]], axis=0)
                    pk = jnp.concatenate([pk, pbuf[j, c + 1, :, :BLOCK]], axis=0)
                qq = qd[both, :]
                dd = dod[both, :]
                dk = dk + lax.dot_general(dsk, jnp.where(mj, qq, jnp.zeros_like(qq)), contract_rows,
                                          preferred_element_type=F32)
                dv = dv + lax.dot_general(pk, jnp.where(mj, dd, jnp.zeros_like(dd)), contract_rows,
                                          preferred_element_type=F32)
            dqd[rows, :] = dq
            dkd[rows, :] = dk
            dvd[rows, :] = dv

        dsout_ref[...] = dsin_ref[...] + dsacc[...]

        dgq = jnp.zeros((1, LANES), F32)
        dgk = jnp.zeros((1, LANES), F32)
        for c in range(nchunk):
            rows = _chunk_rows(c, d, nb)
            qv = qs[rows, :]
            rq = lax.rsqrt(_head_mean(qv * qv, seg) + EPS)
            qh = qv * rq
            dy = dqd[c * BLOCK:(c + 1) * BLOCK, :]
            dgq = dgq + jnp.sum(dy * qh, axis=0, keepdims=True) * qscale
            dxh = dy * qg
            ddd[rows, :] = rq * (dxh - qh * _head_mean(dxh * qh, seg))
            kv = ks[rows, :]
            rk = lax.rsqrt(_head_mean(kv * kv, seg) + EPS)
            kh = kv * rk
            dy = dkd[c * BLOCK:(c + 1) * BLOCK, :]
            dgk = dgk + jnp.sum(dy * kh, axis=0, keepdims=True)
            dxh = dy * kg
            ld[rows, :] = rk * (dxh - kh * _head_mean(dxh * kh, seg))
        @pl.when(h > 0)
        def _():
            for cp in flush(h - 1):
                cp.wait()

        qst[...] = ddd[...].astype(BF16)
        kst[...] = ld[...].astype(BF16)
        for c in range(nchunk):
            ddd[_chunk_rows(c, d, nb), :] = dvd[c * BLOCK:(c + 1) * BLOCK, :]
        vst[...] = ddd[...].astype(BF16)
        for cp in flush(h):
            cp.start()

        @pl.when(h == hp - 1)
        def _():
            for cp in flush(h):
                cp.wait()

        dgq_ref[0] = dgq
        dgk_ref[0] = dgk

    ucol = lambda c0: pl.BlockSpec((t, LANES), lambda h: (0, c0 + h))
    vec = pl.BlockSpec((1, LANES), lambda h: (0, 0))
    oblk = pl.BlockSpec((t, LANES), lambda h: (0, h))
    bblk = pl.BlockSpec((2, BLOCK, 2 * BLOCK), lambda h: (gi * hp + h, 0, 0))
    gblk = pl.BlockSpec((1, 1, LANES), lambda h: (h, 0, 0))
    gsh = jax.ShapeDtypeStruct((hp, 1, LANES), F32)
    hbm = pl.BlockSpec(memory_space=pl.ANY)
    return pl.pallas_call(
        body, name=name, grid=(hp,),
        in_specs=[ucol(qc0), ucol(kc0), ucol(vc0), oblk, oblk, oblk, vec, vec, bblk, bblk, hbm],
        out_specs=(gblk, gblk, bblk, hbm),
        out_shape=(gsh, gsh, jax.ShapeDtypeStruct(ds_in.shape, F32), jax.ShapeDtypeStruct(du_in.shape, BF16)),
        input_output_aliases={9: 2, 10: 3},
        scratch_shapes=[pltpu.VMEM((t, LANES), BF16), pltpu.VMEM((t + BLOCK, LANES), BF16),
                        pltpu.VMEM((t + BLOCK, LANES), BF16), pltpu.VMEM((t, LANES), BF16),
                        pltpu.VMEM((t, LANES), F32), pltpu.VMEM((t, LANES), F32), pltpu.VMEM((t, LANES), F32),
                        pltpu.VMEM((t, LANES), F32), pltpu.VMEM((t, LANES), F32),
                        pltpu.VMEM((2, BLOCK, 2 * BLOCK), F32), pltpu.VMEM((2, nchunk, BLOCK, 2 * BLOCK), BF16),
                        pltpu.VMEM((2, nchunk, BLOCK, 2 * BLOCK), BF16), pltpu.VMEM((t, LANES), F32),
                        pltpu.VMEM((t, LANES), F32), pltpu.VMEM((t, LANES), BF16), pltpu.VMEM((t, LANES), BF16),
                        pltpu.VMEM((t, LANES), BF16), pltpu.SemaphoreType.DMA((3,))],
        compiler_params=_cparams(("arbitrary",)),
    )(u, u, u, do_g, dd_g, lse_g, qg2, kg2, bias, ds_in, du_in)


def _group_weights(l0, l1, l2):
    mx = jnp.maximum(jnp.maximum(l0, l1), l2)
    e0, e1, e2 = jnp.exp(l0 - mx), jnp.exp(l1 - mx), jnp.exp(l2 - mx)
    inv = 1.0 / (e0 + e1 + e2)
    return e0 * inv, e1 * inv, e2 * inv


def _combine_fwd(os_, lses, name):
    t, ao = os_[0].shape
    ROWS = _rows(t, 26 * ao)

    def body(o0, o1, o2, l0, l1, l2, o_ref):
        w0, w1, w2 = _group_weights(l0[...], l1[...], l2[...])
        o_ref[...] = (w0 * o0[...] + w1 * o1[...] + w2 * o2[...]).astype(BF16)

    row = pl.BlockSpec((ROWS, ao), lambda i: (i, 0))
    return pl.pallas_call(
        body, name=name, grid=(t // ROWS,), in_specs=[row] * 6, out_specs=row,
        out_shape=jax.ShapeDtypeStruct((t, ao), BF16), compiler_params=_cparams(("arbitrary",)),
    )(*os_, *lses)


def _combine_bwd(do, os_, lses, name):
    t, ao = do.shape
    idx = np.arange(ao) // HEAD_DIM
    seg = jnp.asarray((idx[:, None] == idx[None, :]).astype(np.float32), dtype=BF16)
    ROWS = _rows(t, 52 * ao)

    def body(do_ref, o0, o1, o2, l0, l1, l2, seg_ref, g0, g1, g2, d0, d1, d2):
        w0, w1, w2 = _group_weights(l0[...], l1[...], l2[...])
        dov = do_ref[...]
        o = w0 * o0[...] + w1 * o1[...] + w2 * o2[...]
        sd = _segment_sum(dov * o, seg_ref[...])
        for w, gref, dref in ((w0, g0, d0), (w1, g1, d1), (w2, g2, d2)):
            gref[...] = w * dov
            dref[...] = -(w * sd)

    row = pl.BlockSpec((ROWS, ao), lambda i: (i, 0))
    sh = jax.ShapeDtypeStruct((t, ao), F32)
    outs = pl.pallas_call(
        body, name=name, grid=(t // ROWS,), in_specs=[row] * 7 + [pl.BlockSpec((ao, ao), lambda i: (0, 0))],
        out_specs=(row,) * 6, out_shape=(sh,) * 6, compiler_params=_cparams(("arbitrary",)),
    )(do, *os_, *lses, seg)
    return outs[:3], outs[3:]


def _adamw(w, g, m, v, name):
    shape = w.shape
    cols = shape[-1]
    rows = int(np.prod(shape[:-1]))
    tr = rows if rows <= 512 else _tile_rows(rows)
    c1 = 1.0 - ADAM_B1 ** ADAM_STEP
    c2 = 1.0 - ADAM_B2 ** ADAM_STEP

    def body(w_ref, g_ref, m_ref, v_ref, d_ref, nm_ref, nv_ref):
        gv = g_ref[...]
        mn = ADAM_B1 * m_ref[...] + (1.0 - ADAM_B1) * gv
        vn = ADAM_B2 * v_ref[...] + (1.0 - ADAM_B2) * (gv * gv)
        nm_ref[...] = mn
        nv_ref[...] = vn
        d_ref[...] = -ADAM_LR * ((mn / c1) / (jnp.sqrt(vn / c2) + ADAM_EPS) + ADAM_WD * w_ref[...])

    blk = pl.BlockSpec((tr, cols), lambda i: (i, 0))
    sh = jax.ShapeDtypeStruct((rows, cols), F32)
    outs = pl.pallas_call(
        body, name=name, grid=(rows // tr,), in_specs=[blk] * 4, out_specs=(blk,) * 3, out_shape=(sh,) * 3,
        compiler_params=_cparams(("arbitrary",)),
    )(*[a.reshape(rows, cols) for a in (w, g, m, v)])
    return tuple(o.reshape(shape) for o in outs)


def _tile_rows(rows):
    for t in (512, 256, 128, 64, 32, 16, 8):
        if rows % t == 0:
            return t
    return rows


def _sum_slots(recv, parts, me, layers, l, name):
    _, rows, cols = recv.shape
    tr = rows if rows <= 512 else _tile_rows(rows)

    def body(me_ref, r_ref, own_ref, _, o_ref):
        acc = jnp.zeros(o_ref.shape, F32)
        for s in range(N_DEV):
            acc = acc + jnp.where(me_ref[0] == s, own_ref[...], r_ref[s]).astype(F32)
        o_ref[...] = acc

    return pl.pallas_call(
        body, name=name,
        grid_spec=pltpu.PrefetchScalarGridSpec(
            num_scalar_prefetch=1, grid=(rows // tr,),
            in_specs=[pl.BlockSpec((N_DEV, tr, cols), lambda i, me: (0, i, 0)),
                      pl.BlockSpec((None, tr, cols), lambda i, me: (me[0], i, 0)),
                      pl.BlockSpec(memory_space=pl.ANY)],
            out_specs=pl.BlockSpec((None, tr, cols), lambda i, me: (l, i, 0))),
        out_shape=jax.ShapeDtypeStruct(layers.shape, F32), input_output_aliases={3: 0},
        compiler_params=_cparams(("arbitrary",)),
    )(me.reshape(1), recv, parts, layers)


def _peer(k):
    x, y, c = lax.axis_index("x"), lax.axis_index("y"), lax.axis_index("c")
    return (1 - x if k & 4 else x, 1 - y if k & 2 else y, 1 - c if k & 1 else c)


def _dev_index(p):
    return 4 * p[0] + 2 * p[1] + p[2]


HBM_SPEC = pl.BlockSpec(memory_space=pltpu.HBM)
SEM_SPEC = pl.BlockSpec(memory_space=pltpu.SEMAPHORE)
ANY_SPEC = pl.BlockSpec(memory_space=pl.ANY)
CHIPS = (4, 2, 6)


def _remote(src, dst, send_sem, recv_sem, to):
    return pltpu.make_async_remote_copy(src_ref=src, dst_ref=dst, send_sem=send_sem, recv_sem=recv_sem,
                                        device_id=to, device_id_type=MESH)


def _hbm(a):
    return pltpu.with_memory_space_constraint(a, pltpu.HBM)


def _split_call(body, name, bufs, sems_in, sem_out_sizes, after):
    nb, ns, no = len(bufs), len(sems_in), len(sem_out_sizes)
    extra = [] if after is None else [after]

    def kern(*refs):
        pos = nb + ns + len(extra)
        body(refs[:nb], refs[nb:nb + ns], refs[pos:pos + no])
        token_ref = refs[pos + no + nb]
        token_ref[...] = jnp.zeros_like(token_ref)

    out_shape = (tuple(pltpu.SemaphoreType.DMA((s,)) for s in sem_out_sizes)
                 + tuple(pltpu.HBM(b.shape, b.dtype) for b in bufs) + (jax.ShapeDtypeStruct((8, LANES), F32),))
    res = pl.pallas_call(
        kern, name=name, out_shape=out_shape,
        in_specs=[HBM_SPEC] * nb + [SEM_SPEC] * ns + [ANY_SPEC] * len(extra),
        out_specs=(SEM_SPEC,) * no + (HBM_SPEC,) * nb + (pl.BlockSpec(memory_space=pltpu.VMEM),),
        input_output_aliases={i: no + i for i in range(nb)},
        compiler_params=pltpu.CompilerParams(has_side_effects=pltpu.SideEffectType.DATAFLOW_SIDE_EFFECTING),
    )(*bufs, *sems_in, *extra)
    return res[:no], res[no:no + nb], res[no + nb]


def _gather_start(shards, lands, after, name):
    n = len(shards)

    def body(bufs, _, sems):
        ins, lnd = bufs[:n], bufs[n:]
        d2d_s, d2d_r, ici_s, ici_r = sems
        me = _dev_index(_peer(0))
        for j, k in enumerate(CHIPS):
            for i in range(n):
                _remote(ins[i], lnd[i].at[me], ici_s.at[j], ici_r.at[j], _peer(k)).start()
        for i in range(n):
            _remote(ins[i], lnd[i].at[me], d2d_s.at[0], d2d_r.at[0], _peer(1)).start()

    return _split_call(body, name, [_hbm(a) for a in (*shards, *lands)], [], (1, 1, 3, 3), after)


def _gather_forward(n, bufs, ici_r, after, name):
    def body(refs, sems_in, sems):
        ins, lnd = refs[:n], refs[n:]
        (arrived,) = sems_in
        fwd_s, fwd_r = sems
        for j, k in enumerate(CHIPS):
            blk = _dev_index(_peer(k))
            for i in range(n):
                _remote(ins[i], lnd[i].at[blk], fwd_s.at[j], arrived.at[j], _peer(k)).wait_recv()
            for i in range(n):
                _remote(lnd[i].at[blk], lnd[i].at[blk], fwd_s.at[j], fwd_r.at[j], _peer(1)).start()

    return _split_call(body, name, bufs, [ici_r], (3, 3), after)


def _gather_finish(n, bufs, d2d_s, d2d_r, ici_s, fwd_s, fwd_r, after, name):
    def body(refs, sems_in, _):
        ins, lnd = refs[:n], refs[n:]
        d2d_send, d2d_recv, ici_send, fwd_send, fwd_recv = sems_in
        sib = _peer(1)
        for i in range(n):
            cp = _remote(ins[i], lnd[i].at[_dev_index(sib)], d2d_send.at[0], d2d_recv.at[0], sib)
            cp.wait_send()
            cp.wait_recv()
        for j, k in enumerate(CHIPS):
            passed = _dev_index(_peer(k))
            landed = _dev_index(_peer(k | 1))
            for i in range(n):
                _remote(ins[i], lnd[i].at[passed], ici_send.at[j], fwd_recv.at[j], _peer(k)).wait_send()
                cp = _remote(lnd[i].at[passed], lnd[i].at[landed], fwd_send.at[j], fwd_recv.at[j], sib)
                cp.wait_send()
                cp.wait_recv()

    _, out, token = _split_call(body, name, bufs, [d2d_s, d2d_r, ici_s, fwd_s, fwd_r], (), after)
    return out[n:], token


def _exchange_start(parts, lands, after, name):
    n = len(parts)

    def body(bufs, _, sems):
        src, lnd = bufs[:n], bufs[n:]
        send, recv = sems
        me = _dev_index(_peer(0))
        for k in (4, 5, 2, 3, 6, 7, 1):
            to = _peer(k)
            for i in range(n):
                _remote(src[i].at[_dev_index(to)], lnd[i].at[me], send.at[k - 1], recv.at[k - 1], to).start()

    return _split_call(body, name, [_hbm(a) for a in (*parts, *lands)], [], (7, 7), after)


def _exchange_finish(n, bufs, send, recv, after, name):
    def body(refs, sems_in, _):
        src, lnd = refs[:n], refs[n:]
        send_, recv_ = sems_in
        me = _dev_index(_peer(0))
        for k in range(1, N_DEV):
            frm = _peer(k)
            for i in range(n):
                cp = _remote(src[i].at[me], lnd[i].at[_dev_index(frm)], send_.at[k - 1], recv_.at[k - 1], frm)
                cp.wait_send()
                cp.wait_recv()

    _, out, token = _split_call(body, name, bufs, [send, recv], (), after)
    return out[:n], out[n:], token


def _all_reduce_small(v, after, name):
    rows = v.shape[0]

    def body(v_ref, _, o_ref, buf, send_sems, recv_sems):
        me = _dev_index(_peer(0))
        buf[me] = v_ref[...]
        copies = []
        for k in range(1, N_DEV):
            copies.append(pltpu.make_async_remote_copy(
                src_ref=v_ref, dst_ref=buf.at[me], send_sem=send_sems.at[k - 1], recv_sem=recv_sems.at[k - 1],
                device_id=_peer(k), device_id_type=MESH))
        for cp in copies:
            cp.start()
        for k in range(1, N_DEV):
            pltpu.make_async_remote_copy(
                src_ref=v_ref, dst_ref=buf.at[_dev_index(_peer(k))], send_sem=send_sems.at[k - 1],
                recv_sem=recv_sems.at[k - 1], device_id=_peer(k), device_id_type=MESH).wait_recv()
        for cp in copies:
            cp.wait_send()
        acc = buf[0]
        for s in range(1, N_DEV):
            acc = acc + buf[s]
        o_ref[...] = acc

    vm = pl.BlockSpec(memory_space=pltpu.VMEM)
    return pl.pallas_call(
        body, name=name, in_specs=[vm, pl.BlockSpec(memory_space=pl.ANY)], out_specs=vm,
        out_shape=jax.ShapeDtypeStruct(v.shape, F32),
        scratch_shapes=[pltpu.VMEM((N_DEV, rows, LANES), F32), pltpu.SemaphoreType.DMA((7,)),
                        pltpu.SemaphoreType.DMA((7,))],
    )(v, after)


def _columns(cdim, ao):
    q_col = 2 * cdim
    attn_dim = N_GROUPS * ao
    return (q_col, q_col + attn_dim, q_col + 2 * attn_dim), q_col + 3 * attn_dim


def _layer_fwd(x, h1, sm, bg, get_rest, bias, hpg, next_gain, at_ff1=None):
    cdim = sm["conv_ln_g"].shape[0]
    ao = hpg * HEAD_DIM
    cols, gate_col = _columns(cdim, ao)
    qg2 = jnp.tile(sm["q_norm_g"], 2).reshape(1, LANES)
    kg2 = jnp.tile(sm["k_norm_g"], 2).reshape(1, LANES)
    u = _mm_in_pairs(h1, bg["w_in"], "mm_in")
    zc = _conv_fwd(u, bg["conv_dw_w"], sm["conv_dw_b"], cdim, "conv_fwd")
    zs = _ln_swish_fwd(zc, sm["conv_ln_g"], sm["conv_ln_b"], "ln_swish_fwd")
    os_, lses = [], []
    for gi in range(N_GROUPS):
        o_g, lse_g = _attn_fwd(u, qg2, kg2, bias, gi, cols, hpg, "attn_fwd_g%d" % gi)
        os_.append(o_g)
        lses.append(lse_g)
    o = _combine_fwd(os_, lses, "combine_fwd")
    bg = {**bg, **get_rest(o)}
    mg, yc, ya = _gated_out(zs, o, bg["w_conv_out"], bg["w_attn_out"], u, gate_col, "gated_out")
    x1, h2 = _mm(mg, bg["w_out"], epi="res_rms", extra=x, gain=sm["norm2_g"], name="mm_out")
    f = _mm(h2, bg["w_ff1"], out_dtype=BF16, name="mm_ff1")
    if at_ff1 is not None:
        at_ff1(f)
    if next_gain is None:
        x2, h_next = _mm(f, bg["w_ff2"], a_relu2=True, epi="res", extra=x1, name="mm_ff2"), None
    else:
        x2, h_next = _mm(f, bg["w_ff2"], a_relu2=True, epi="res_rms", extra=x1, gain=next_gain, name="mm_ff2")
    saved = dict(x=x, h1=h1, u=u, zc=zc, zs=zs, yc=yc, os=os_, lses=lses, o=o, ya=ya, mg=mg, x1=x1, h2=h2, f=f,
                 qg2=qg2, kg2=kg2)
    return x2, h_next, saved, bg


GRAD_GROUPS = (("w_ff2", "w_ff1"), ("w_out", "w_conv_out", "w_attn_out", "conv_dw_w", "w_in"))


def _layer_bwd(dx, s, sm, bg, bias, ds_sum, after, emit):
    cdim = sm["conv_ln_g"].shape[0]
    ao = bg["w_attn_out"].shape[0]
    hpg = ao // HEAD_DIM
    cols, gate_col = _columns(cdim, ao)
    g = {}
    df = _mm(dx, bg["w_ff2"], tb=True, epi="drelu2", extra=s["f"], out_dtype=BF16, after=after, name="mm_dff2")
    g["w_ff2"] = _mm(s["f"], dx, ta=True, a_relu2=True, out_dtype=BF16, name="mm_gw_ff2")
    g["w_ff1"] = _mm(s["h2"], df, ta=True, out_dtype=BF16, out_slots=True, name="mm_gw_ff1")
    after = emit(GRAD_GROUPS[0], g)
    dx1, dg2 = _mm(df, bg["w_ff1"], tb=True, epi="rms_bwd", extra=(s["x1"], dx), gain=sm["norm2_g"], after=after,
                   name="mm_dff1")
    g["norm2_g"] = dg2[0]
    dmg = _mm(dx1, bg["w_out"], tb=True, name="mm_dout")
    g["w_out"] = _mm(s["mg"], dx1, ta=True, out_dtype=BF16, name="mm_gw_out")
    dyc, dya, dugc, duga = _gate_bwd(dmg, s["u"], s["yc"], s["ya"], gate_col, "gate_bwd")
    dzs = _mm(dyc, bg["w_conv_out"], tb=True, name="mm_dconv_out")
    g["w_conv_out"] = _mm(s["zs"], dyc, ta=True, out_dtype=BF16, name="mm_gw_conv_out")
    do = _mm(dya, bg["w_attn_out"], tb=True, name="mm_dattn_out")
    g["w_attn_out"] = _mm(s["o"], dya, ta=True, out_dtype=BF16, name="mm_gw_attn_out")
    dzc, dlg, dlb = _ln_swish_bwd(dzs, s["zc"], sm["conv_ln_g"], sm["conv_ln_b"], "ln_swish_bwd")
    g["conv_ln_g"] = dlg[0]
    g["conv_ln_b"] = dlb[0]
    da, dgt, dcw, dcb = _conv_bwd(dzc, s["u"], bg["conv_dw_w"], cdim, "conv_bwd")
    g["conv_dw_w"] = dcw[:CONV_WIDTH].astype(BF16)
    g["conv_dw_b"] = dcb[0]
    do_gs, dd_gs = _combine_bwd(do, s["os"], s["lses"], "combine_bwd")
    du = lax.empty(s["u"].shape, BF16)
    for col, piece in ((0, da), (cdim, dgt), (gate_col, dugc), (gate_col + dugc.shape[1], duga)):
        du = lax.dynamic_update_slice(du, piece, (0, col))
    gqs, gks = [], []
    for gi in range(N_GROUPS):
        gq, gk, ds_sum, du = _attn_bwd(s["u"], do_gs[gi], dd_gs[gi], s["lses"][gi], s["qg2"], s["kg2"], bias,
                                       ds_sum, du, gi, cols, hpg, "attn_bwd_g%d" % gi)
        gqs.append(gq)
        gks.append(gk)
    g["q_norm_g"] = jnp.concatenate(gqs)
    g["k_norm_g"] = jnp.concatenate(gks)
    g["w_in"] = _mm_gw_in_pairs(s["h1"], du, after, "mm_gw_in")
    after = emit(GRAD_GROUPS[1], g)
    dx0, dg1 = _mm_din_pairs(du, bg["w_in"], s["x"], sm["norm1_g"], dx1, after, "mm_din")
    g["norm1_g"] = dg1[0]
    return dx0, g, ds_sum


BIG = ("w_in", "conv_dw_w", "w_conv_out", "w_attn_out", "w_out", "w_ff1", "w_ff2")
COL_SHARDED = ("w_in", "conv_dw_w", "w_conv_out", "w_attn_out", "w_ff1")
SMALL = ("rel_bias", "norm1_g", "q_norm_g", "k_norm_g", "conv_dw_b", "conv_ln_g", "conv_ln_b", "norm2_g")
WEIGHTS = ("rel_bias", "norm1_g", "w_in", "q_norm_g", "k_norm_g", "conv_dw_w", "conv_dw_b", "conv_ln_g", "conv_ln_b",
           "w_conv_out", "w_attn_out", "w_out", "norm2_g", "w_ff1", "w_ff2")


def _to_whole(name, gathered):
    n, a, b = gathered.shape
    if name in COL_SHARDED:
        return gathered.transpose(1, 0, 2).reshape(a, n * b)
    return gathered.reshape(n * a, b)


def _to_slots(name, whole):
    a, b = whole.shape
    if name in COL_SHARDED:
        return whole.reshape(a, N_DEV, b // N_DEV).transpose(1, 0, 2)
    return whole.reshape(N_DEV, a // N_DEV, b)


def _own_slot(block, me):
    land = lax.empty((N_DEV,) + block.shape, block.dtype)
    return lax.dynamic_update_slice(land, block[None], (me,) + (0,) * block.ndim)


def kernel(x, rel_bias, norm1_g, w_in, q_norm_g, k_norm_g, conv_dw_w, conv_dw_b, conv_ln_g, conv_ln_b, w_conv_out, w_attn_out, w_out, norm2_g, w_ff1, w_ff2, loss_target, m_rel_bias, m_norm1_g, m_w_in, m_q_norm_g, m_k_norm_g, m_conv_dw_w, m_conv_dw_b, m_conv_ln_g, m_conv_ln_b, m_w_conv_out, m_w_attn_out, m_w_out, m_norm2_g, m_w_ff1, m_w_ff2, v_rel_bias, v_norm1_g, v_w_in, v_q_norm_g, v_k_norm_g, v_conv_dw_w, v_conv_dw_b, v_conv_ln_g, v_conv_ln_b, v_w_conv_out, v_w_attn_out, v_w_out, v_norm2_g, v_w_ff1, v_w_ff2):
    w = dict(rel_bias=rel_bias, norm1_g=norm1_g, w_in=w_in, q_norm_g=q_norm_g, k_norm_g=k_norm_g, conv_dw_w=conv_dw_w,
             conv_dw_b=conv_dw_b, conv_ln_g=conv_ln_g, conv_ln_b=conv_ln_b, w_conv_out=w_conv_out,
             w_attn_out=w_attn_out, w_out=w_out, norm2_g=norm2_g, w_ff1=w_ff1, w_ff2=w_ff2)
    mom = dict(rel_bias=m_rel_bias, norm1_g=m_norm1_g, w_in=m_w_in, q_norm_g=m_q_norm_g, k_norm_g=m_k_norm_g,
               conv_dw_w=m_conv_dw_w, conv_dw_b=m_conv_dw_b, conv_ln_g=m_conv_ln_g, conv_ln_b=m_conv_ln_b,
               w_conv_out=m_w_conv_out, w_attn_out=m_w_attn_out, w_out=m_w_out, norm2_g=m_norm2_g, w_ff1=m_w_ff1,
               w_ff2=m_w_ff2)
    var = dict(rel_bias=v_rel_bias, norm1_g=v_norm1_g, w_in=v_w_in, q_norm_g=v_q_norm_g, k_norm_g=v_k_norm_g,
               conv_dw_w=v_conv_dw_w, conv_dw_b=v_conv_dw_b, conv_ln_g=v_conv_ln_g, conv_ln_b=v_conv_ln_b,
               w_conv_out=v_w_conv_out, w_attn_out=v_w_attn_out, w_out=v_w_out, norm2_g=v_norm2_g, w_ff1=v_w_ff1,
               w_ff2=v_w_ff2)

    depth = norm1_g.shape[0]
    me = 4 * lax.axis_index("x") + 2 * lax.axis_index("y") + lax.axis_index("c")
    odd_core = lax.axis_index("c") == 1
    hpg = w_attn_out.shape[1] // HEAD_DIM
    buckets = jnp.asarray(_bucket_table())
    bias = _bias_expand(rel_bias, buckets, hpg, "bias_expand")

    first_names = ("w_in", "conv_dw_w")
    rest_names = tuple(k for k in BIG if k not in first_names)

    def chain_start(l, names, after):
        shards = [w[k][l] if k == "conv_dw_w" else w[k][l].astype(BF16) for k in names]
        if "w_in" in names:
            i = names.index("w_in")
            shards[i] = jnp.where(odd_core, jnp.pad(shards[i], ((0, 0), (SHIFT, 0))),
                                  jnp.pad(shards[i], ((0, 0), (0, SHIFT))))
        sems, bufs, token = _gather_start(shards, [_own_slot(s, me) for s in shards], after,
                                          "gather_start_%s_l%d" % (names[0], l))
        return dict(l=l, names=names, sems=sems, bufs=bufs, token=token)

    def chain_forward(ch, after):
        fwd, bufs, token = _gather_forward(len(ch["names"]), ch["bufs"], ch["sems"][3], after,
                                           "gather_forward_%s_l%d" % (ch["names"][0], ch["l"]))
        ch.update(fwd=fwd, bufs=bufs)
        return token

    def chain_finish(ch, after):
        d2d_s, d2d_r, ici_s, _ = ch["sems"]
        gathered, _ = _gather_finish(len(ch["names"]), ch["bufs"], d2d_s, d2d_r, ici_s, ch["fwd"][0], ch["fwd"][1],
                                     after, "gather_finish_%s_l%d" % (ch["names"][0], ch["l"]))
        out = {k: a if k == "w_in" else _to_whole(k, a) for k, a in zip(ch["names"], gathered)}
        if "conv_dw_w" in out:
            out["conv_dw_w"] = jnp.pad(out["conv_dw_w"], ((0, CONV_TAPS_PADDED - CONV_WIDTH), (0, 0)))
        return out

    xs = x[0]
    h1 = _rms_fwd(xs, norm1_g[0], "rms1_fwd")
    saved, bigs, smalls = [], [], []
    chains = {}
    for l in range(depth):
        sm = {k: w[k][l] for k in SMALL if k != "rel_bias"}
        if l == 0:
            first = chain_start(0, first_names, None)
            token = chain_forward(first, None)
            rest = chain_start(0, rest_names, token)
            bg = chain_finish(first, rest["token"])

            def get_rest(o, rest=rest):
                token = chain_forward(rest, o)
                if depth > 1:
                    chains[1] = (chain_start(1, first_names, token),)
                    chains[1] += (chain_start(1, rest_names, chains[1][0]["token"]),)
                    token = chains[1][1]["token"]
                return chain_finish(rest, token)
        elif l == 1:
            first, rest = chains[1]
            token = chain_forward(first, xs)
            if depth > 2:
                chains[2] = chain_start(2, BIG, token)
                token = chains[2]["token"]
            bg = chain_finish(first, token)

            def get_rest(o, rest=rest):
                return chain_finish(rest, chain_forward(rest, o))
        else:
            token = xs if "fwd" in chains[l] else chain_forward(chains[l], xs)
            if l + 1 < depth:
                chains[l + 1] = chain_start(l + 1, BIG, token)
                token = chains[l + 1]["token"]
            whole = chain_finish(chains[l], token)
            bg = {k: whole[k] for k in first_names}

            def get_rest(o, whole=whole):
                return {k: whole[k] for k in rest_names}
        at_ff1 = (lambda f, nxt=l + 1: chain_forward(chains[nxt], f)) if 2 <= l < depth - 1 else None
        xs, h1, sv, bg = _layer_fwd(xs, h1, sm, bg, get_rest, bias, hpg, norm1_g[l + 1] if l + 1 < depth else None,
                                    at_ff1)
        saved.append(sv)
        bigs.append(bg)
        smalls.append(sm)

    loss, dx = _loss_and_grad(xs, loss_target[0], "loss")

    ds_sum = jnp.zeros((N_GROUPS * hpg, BLOCK, 2 * BLOCK), F32)
    g = {k: [None] * depth for k in SMALL if k != "rel_bias"}
    sums = {k: lax.empty((depth, int(np.prod(w[k].shape[1:-1])), w[k].shape[-1] + (SHIFT if k == "w_in" else 0)), F32)
            for k in BIG}
    pending = []

    def finish_oldest(after):
        names, l, (send, recv), bufs = pending.pop(0)
        parts, recvd, token = _exchange_finish(len(names), bufs, send, recv, after,
                                               "exchange_finish_%s_l%d" % (names[0], l))
        for k, r, p in zip(names, recvd, parts):
            three = (N_DEV, -1, r.shape[-1])
            sums[k] = _sum_slots(r.reshape(three), p.reshape(three), me, sums[k], l, "sum_" + k)
        return token

    def make_emit(l):
        def emit(names, gl):
            parts = [gl[k] if k in ("w_ff1", "w_in") else _to_slots(k, gl[k]) for k in names]
            token = finish_oldest(parts[0]) if len(pending) >= len(GRAD_GROUPS) else None
            lands = [lax.empty(p.shape, p.dtype) for p in parts]
            sems, bufs, token = _exchange_start(parts, lands, token, "exchange_start_%s_l%d" % (names[0], l))
            pending.append((names, l, sems, bufs))
            return token
        return emit

    token = None
    for l in reversed(range(depth)):
        dx, gl, ds_sum = _layer_bwd(dx, saved[l], smalls[l], bigs[l], bias, ds_sum, token, make_emit(l))
        for k in g:
            g[k][l] = gl[k]
        token = None
    grad_x = dx

    g = {k: jnp.stack(v) for k, v in g.items()}
    for k in ("q_norm_g", "k_norm_g"):
        g[k] = jnp.sum(g[k].reshape(depth, -1, HEAD_DIM), axis=1)
    db = _bias_reduce(ds_sum, buckets, hpg, "bias_reduce")
    g["rel_bias"] = db[:, 0, :NUM_BUCKETS].T

    flat = jnp.concatenate([g[k].reshape(-1) for k in SMALL])
    nflat = flat.shape[0]
    rows = -(-nflat // (8 * LANES)) * 8
    packed = jnp.pad(flat, (0, rows * LANES - nflat)).reshape(rows, LANES)
    grad, outs = {}, {}
    token = dx
    while pending:
        names = pending[0][0]
        finish_oldest(token)
        for k in names:
            total = sums[k]
            if k == "w_in":
                total = jnp.where(odd_core, total[..., SHIFT:], total[..., :w_in.shape[-1]])
            grad[k] = total.reshape(w[k].shape)
            outs[k] = _adamw(w[k], grad[k], mom[k], var[k], "adamw_" + k)
            token = outs[k][0]
    total = _all_reduce_small(packed, token, "reduce_small").reshape(-1)
    off = 0
    for k in SMALL:
        size = int(np.prod(w[k].shape))
        grad[k] = total[off:off + size].reshape(w[k].shape)
        outs[k] = _adamw(w[k], grad[k], mom[k], var[k], "adamw_" + k)
        off += size
    loss = lax.psum(loss[0, 0], ("x", "y", "c"))
    return (loss, grad_x[None], *[grad[k] for k in WEIGHTS], *[outs[k][0] for k in WEIGHTS],
            *[outs[k][1] for k in WEIGHTS], *[outs[k][2] for k in WEIGHTS])
```

```python
import functools
import math

import numpy as np
import jax
import jax.numpy as jnp
from jax import lax
from jax.experimental import pallas as pl
from jax.experimental.pallas import tpu as pltpu

F32 = jnp.float32
BF16 = jnp.bfloat16

HEAD_DIM = 64
N_GROUPS = 3
DILATIONS = (1, 4, 16)
SUB_WINDOW = 128
BLOCK = 128
CONV_WIDTH = 31
CONV_TAPS_PADDED = 32
NUM_BUCKETS = 32
MAX_REL_DISTANCE = 2048
EPS = 1e-6
NEG_INF = -1e30
LANES = 128
SUBLANES = 8

ADAM_LR = 0.001
ADAM_B1 = 0.9
ADAM_B2 = 0.999
ADAM_EPS = 1e-08
ADAM_WD = 0.01
ADAM_STEP = 10

N_DEV = 8
VMEM_LIMIT = 56 * 1024 * 1024
MESH = pl.DeviceIdType.MESH


def _cparams(sem=None):
    return pltpu.CompilerParams(dimension_semantics=sem, vmem_limit_bytes=VMEM_LIMIT)


def _tile(n, target):
    if n <= target:
        return n
    t = (target // LANES) * LANES
    while t >= LANES:
        if n % t == 0:
            return t
        t -= LANES
    return n


def _sigmoid(v):
    return 1.0 / (1.0 + jnp.exp(-v))


MM_VMEM_BUDGET = 40 * 1024 * 1024


def _rms_apply(x, g):
    return x * lax.rsqrt(jnp.mean(x * x, axis=-1, keepdims=True) + EPS) * g


def _rms_grad(dh, x, g):
    r = lax.rsqrt(jnp.mean(x * x, axis=-1, keepdims=True) + EPS)
    xh = x * r
    dxh = dh * g
    dx = r * (dxh - xh * jnp.mean(dxh * xh, axis=-1, keepdims=True))
    return dx, jnp.sum(dh * xh, axis=0, keepdims=True)


def _mm_tiles(m, n, kdim, a_bytes, b_bytes, io_bytes, whole_rows=False, temps=2):
    def need(tm, tn, tk):
        blocks = 2 * (tm * tk * a_bytes + tk * tn * b_bytes + tm * tn * io_bytes)
        casts = (tm * tk * 2 if a_bytes == 4 else 0) + (tk * tn * 2 if b_bytes == 4 else 0)
        return blocks + casts + temps * tm * tn * 4

    tn = n if whole_rows else _tile(n, 1024)
    while True:
        fits = [(tm * tk, tm, tk) for tm in {_tile(m, c) for c in (1024, 512, 256, 128)}
                for tk in {_tile(kdim, c) for c in (2048, 1024, 512, 256)} if need(tm, tn, tk) <= MM_VMEM_BUDGET]
        if fits:
            _, tm, tk = max(fits)
            return tm, tn, tk
        assert not whole_rows and tn % 256 == 0, "no block size fits the VMEM budget"
        tn //= 2


def _mm(a, b, *, ta=False, tb=False, out_dtype=F32, epi=None, extra=(), gain=None, after=None, out_slots=False,
        a_relu2=False, name):
    extra = tuple(extra) if isinstance(extra, (tuple, list)) else (extra,)
    m = a.shape[1] if ta else a.shape[0]
    kdim = a.shape[0] if ta else a.shape[1]
    n = b.shape[0] if tb else b.shape[1]
    norm = epi in ("res_rms", "rms_bwd")
    io_bytes = (jnp.dtype(out_dtype).itemsize + sum(e.dtype.itemsize for e in extra) + (2 if epi == "res_rms" else 0))
    tm, tn, tk = _mm_tiles(m, n // N_DEV if out_slots else n, kdim, a.dtype.itemsize, b.dtype.itemsize, io_bytes,
                           whole_rows=norm, temps=6 if norm else 2)
    if out_slots:
        assert epi is None and tn == n // N_DEV
    nk = kdim // tk
    a_spec = pl.BlockSpec((tk, tm), lambda i, j, k: (k, i)) if ta else pl.BlockSpec((tm, tk), lambda i, j, k: (i, k))
    b_spec = pl.BlockSpec((tn, tk), lambda i, j, k: (j, k)) if tb else pl.BlockSpec((tk, tn), lambda i, j, k: (k, j))
    o_spec = (pl.BlockSpec((None, tm, tn), lambda i, j, k: (j, i, 0)) if out_slots
              else pl.BlockSpec((tm, tn), lambda i, j, k: (i, j)))
    v_spec = pl.BlockSpec((1, tn), lambda i, j, k: (0, j))
    dims = (((0 if ta else 1,), (1 if tb else 0,)), ((), ()))
    n_extra = len(extra)
    n_in = 2 + n_extra + (gain is not None) + (after is not None)
    n_out = 2 if norm else 1

    def body(*refs):
        a_ref, b_ref = refs[0], refs[1]
        e_refs = refs[2:2 + n_extra]
        g_ref = refs[2 + n_extra] if gain is not None else None
        outs = refs[n_in:n_in + n_out]

        def product():
            av = a_ref[...]
            if a_relu2:
                r = jnp.maximum(av.astype(F32), 0.0)
                av = r * r
            return lax.dot_general(av.astype(BF16), b_ref[...].astype(BF16), dims, preferred_element_type=F32)

        def finish(acc):
            if epi is None:
                outs[0][...] = acc.astype(outs[0].dtype)
            elif epi == "res":
                outs[0][...] = (e_refs[0][...] + acc).astype(outs[0].dtype)
            elif epi == "drelu2":
                outs[0][...] = (acc * (2.0 * jnp.maximum(e_refs[0][...].astype(F32), 0.0))).astype(outs[0].dtype)
            elif epi == "res_rms":
                x1 = e_refs[0][...] + acc
                outs[0][...] = x1
                outs[1][...] = _rms_apply(x1, g_ref[...]).astype(BF16)
            elif epi == "rms_bwd":
                dx, dg = _rms_grad(acc, e_refs[0][...], g_ref[...])
                outs[0][...] = e_refs[1][...] + dx
                i = pl.program_id(0)

                @pl.when(i == 0)
                def _():
                    outs[1][...] = dg

                @pl.when(i > 0)
                def _():
                    outs[1][...] += dg

        if nk == 1:
            finish(product())
            return
        acc_ref = refs[-1]
        k = pl.program_id(2)

        @pl.when(k == 0)
        def _():
            acc_ref[...] = product()

        @pl.when(jnp.logical_and(k > 0, k < nk - 1))
        def _():
            acc_ref[...] += product()

        @pl.when(k == nk - 1)
        def _():
            finish(acc_ref[...] + product())

    in_specs = ([a_spec, b_spec] + [o_spec] * n_extra + ([v_spec] if gain is not None else [])
                + ([pl.BlockSpec(memory_space=pl.ANY)] if after is not None else []))
    if epi == "res_rms":
        out_shape = (jax.ShapeDtypeStruct((m, n), F32), jax.ShapeDtypeStruct((m, n), BF16))
        out_specs = (o_spec, o_spec)
    elif epi == "rms_bwd":
        out_shape = (jax.ShapeDtypeStruct((m, n), F32), jax.ShapeDtypeStruct((1, n), F32))
        out_specs = (o_spec, v_spec)
    else:
        out_shape = jax.ShapeDtypeStruct((N_DEV, m, tn) if out_slots else (m, n), out_dtype)
        out_specs = o_spec
    args = (a, b) + extra + ((gain.reshape(1, n),) if gain is not None else ()) + ((after,) if after is not None else ())
    return pl.pallas_call(
        body, name=name, grid=(m // tm, n // tn, nk), in_specs=in_specs, out_specs=out_specs, out_shape=out_shape,
        scratch_shapes=[pltpu.VMEM((tm, tn), F32)] if nk > 1 else [],
        compiler_params=_cparams(("arbitrary", "arbitrary", "arbitrary")),
    )(*args)


SHIFT = HEAD_DIM


def _pair_blocks(e, o):
    wp = e.shape[-1]
    return e[:, :wp - LANES], e[:, wp - LANES:] + o[:, :LANES], o[:, LANES:]


def _mm_in_pairs(a, wg, name):
    t, kdim = a.shape
    wp = wg.shape[-1]
    ws = wp - SHIFT
    tm = _tile(t, 1024)

    def body(a_ref, e_ref, o_ref, u_ref):
        av = a_ref[...]
        lo, mid, hi = _pair_blocks(e_ref[...], o_ref[...])
        u_ref[:, :wp - LANES] = jnp.dot(av, lo, preferred_element_type=F32).astype(BF16)
        u_ref[:, wp - LANES:wp] = jnp.dot(av, mid, preferred_element_type=F32).astype(BF16)
        u_ref[:, wp:] = jnp.dot(av, hi, preferred_element_type=F32).astype(BF16)

    return pl.pallas_call(
        body, name=name, grid=(N_DEV // 2, t // tm),
        in_specs=[pl.BlockSpec((tm, kdim), lambda p, i: (i, 0)),
                  pl.BlockSpec((None, kdim, wp), lambda p, i: (2 * p, 0, 0)),
                  pl.BlockSpec((None, kdim, wp), lambda p, i: (2 * p + 1, 0, 0))],
        out_specs=pl.BlockSpec((tm, 2 * ws), lambda p, i: (i, p)),
        out_shape=jax.ShapeDtypeStruct((t, N_DEV * ws), BF16), compiler_params=_cparams(("arbitrary", "arbitrary")),
    )(a, wg, wg)


def _mm_din_pairs(du, wg, x, gain, dres, after, name):
    t = du.shape[0]
    _, kdim, wp = wg.shape
    ws = wp - SHIFT
    tm = _tile(t, 512)
    npair = N_DEV // 2
    lanes = (((1,), (1,)), ((), ()))
    extra = [] if after is None else [after]

    def body(d_ref, e_ref, o_ref, x_ref, g_ref, r_ref, *rest):
        dx_ref, dg_ref, acc_ref = rest[-3], rest[-2], rest[-1]
        i, p = pl.program_id(0), pl.program_id(1)
        lo, mid, hi = _pair_blocks(e_ref[...], o_ref[...])
        part = (lax.dot_general(d_ref[:, :wp - LANES], lo, lanes, preferred_element_type=F32)
                + lax.dot_general(d_ref[:, wp - LANES:wp], mid, lanes, preferred_element_type=F32)
                + lax.dot_general(d_ref[:, wp:], hi, lanes, preferred_element_type=F32))

        @pl.when(p == 0)
        def _():
            acc_ref[...] = part

        @pl.when(jnp.logical_and(p > 0, p < npair - 1))
        def _():
            acc_ref[...] += part

        @pl.when(p == npair - 1)
        def _():
            dx, dg = _rms_grad(acc_ref[...] + part, x_ref[...], g_ref[...])
            dx_ref[...] = r_ref[...] + dx

            @pl.when(i == 0)
            def _():
                dg_ref[...] = dg

            @pl.when(i > 0)
            def _():
                dg_ref[...] += dg

    row = pl.BlockSpec((tm, kdim), lambda i, p: (i, 0))
    vec = pl.BlockSpec((1, kdim), lambda i, p: (0, 0))
    return pl.pallas_call(
        body, name=name, grid=(t // tm, npair),
        in_specs=[pl.BlockSpec((tm, 2 * ws), lambda i, p: (i, p)),
                  pl.BlockSpec((None, kdim, wp), lambda i, p: (2 * p, 0, 0)),
                  pl.BlockSpec((None, kdim, wp), lambda i, p: (2 * p + 1, 0, 0)), row, vec, row]
        + [pl.BlockSpec(memory_space=pl.ANY)] * len(extra),
        out_specs=(row, vec),
        out_shape=(jax.ShapeDtypeStruct((t, kdim), F32), jax.ShapeDtypeStruct((1, kdim), F32)),
        scratch_shapes=[pltpu.VMEM((tm, kdim), F32)], compiler_params=_cparams(("arbitrary", "arbitrary")),
    )(du, wg, wg, x, gain.reshape(1, kdim), dres, *extra)


def _mm_gw_in_pairs(h, du, after, name):
    t, kdim = h.shape
    ws = du.shape[1] // N_DEV
    wp = ws + SHIFT
    tm = _tile(kdim, 512)
    rows = (((0,), (0,)), ((), ()))
    extra = [] if after is None else [after]

    def body(h_ref, d_ref, *rest):
        g_ref = rest[-1]
        g = lax.dot_general(h_ref[...], d_ref[...], rows, preferred_element_type=F32)
        g_ref[0] = g[:, :wp].astype(BF16)
        g_ref[1] = g[:, wp - LANES:].astype(BF16)

    return pl.pallas_call(
        body, name=name, grid=(N_DEV // 2, kdim // tm),
        in_specs=[pl.BlockSpec((t, tm), lambda p, i: (0, i)), pl.BlockSpec((t, 2 * ws), lambda p, i: (0, p))]
        + [pl.BlockSpec(memory_space=pl.ANY)] * len(extra),
        out_specs=pl.BlockSpec((2, tm, wp), lambda p, i: (p, i, 0)),
        out_shape=jax.ShapeDtypeStruct((N_DEV, kdim, wp), BF16), compiler_params=_cparams(("arbitrary", "arbitrary")),
    )(h, du, *extra)


ROW_BLOCK_BUDGET = 24 * 1024 * 1024


def _rows(t, row_bytes):
    rows = t
    while rows > 8 and (2 * rows * row_bytes > ROW_BLOCK_BUDGET or t % rows):
        rows //= 2
    return rows


def _rms_fwd(x, g, name):
    t, d = x.shape
    ROWS = _rows(t, 6 * d)

    def body(x_ref, g_ref, h_ref):
        h_ref[...] = _rms_apply(x_ref[...], g_ref[...]).astype(BF16)

    return pl.pallas_call(
        body, name=name, grid=(t // ROWS,),
        in_specs=[pl.BlockSpec((ROWS, d), lambda i: (i, 0)), pl.BlockSpec((1, d), lambda i: (0, 0))],
        out_specs=pl.BlockSpec((ROWS, d), lambda i: (i, 0)),
        out_shape=jax.ShapeDtypeStruct((t, d), BF16), compiler_params=_cparams(("arbitrary",)),
    )(x, g.reshape(1, d))


def _gated_out(zs, o, wc, wa, u, gate_col, name):
    t = zs.shape[0]
    d = wc.shape[1]
    td = math.gcd(_tile(d, 512), gate_col)
    nd = d // td
    c0 = gate_col // td
    tm = _tile(t, 1024)

    def body(zs_ref, o_ref, wc_ref, wa_ref, gc_ref, ga_ref, m_ref, yc_ref, ya_ref):
        yc = jnp.dot(zs_ref[...], wc_ref[...], preferred_element_type=F32)
        ya = jnp.dot(o_ref[...], wa_ref[...], preferred_element_type=F32)
        gc = _sigmoid(gc_ref[...].astype(F32))
        ga = _sigmoid(ga_ref[...].astype(F32))
        m_ref[...] = (gc * yc + ga * ya).astype(BF16)
        yc_ref[...] = yc.astype(BF16)
        ya_ref[...] = ya.astype(BF16)

    blk = pl.BlockSpec((tm, td), lambda i, j: (i, j))
    sh = jax.ShapeDtypeStruct((t, d), BF16)
    return pl.pallas_call(
        body, name=name, grid=(t // tm, nd),
        in_specs=[pl.BlockSpec((tm, zs.shape[1]), lambda i, j: (i, 0)), pl.BlockSpec((tm, o.shape[1]), lambda i, j: (i, 0)),
                  pl.BlockSpec((wc.shape[0], td), lambda i, j: (0, j)), pl.BlockSpec((wa.shape[0], td), lambda i, j: (0, j)),
                  pl.BlockSpec((tm, td), lambda i, j: (i, c0 + j)), pl.BlockSpec((tm, td), lambda i, j: (i, c0 + nd + j))],
        out_specs=(blk, blk, blk), out_shape=(sh, sh, sh), compiler_params=_cparams(("arbitrary", "arbitrary")),
    )(zs, o, wc, wa, u, u)


def _gate_bwd(dm, u, yc, ya, gate_col, name):
    t, d = yc.shape
    td = math.gcd(_tile(d, 512), gate_col)
    nd = d // td
    c0 = gate_col // td
    ROWS = _rows(t, 20 * td)

    def body(dm_ref, gc_ref, ga_ref, yc_ref, ya_ref, dyc_ref, dya_ref, dugc_ref, duga_ref):
        dmv = dm_ref[...]
        gc = _sigmoid(gc_ref[...].astype(F32))
        ga = _sigmoid(ga_ref[...].astype(F32))
        dyc_ref[...] = (dmv * gc).astype(BF16)
        dya_ref[...] = (dmv * ga).astype(BF16)
        dugc_ref[...] = (dmv * yc_ref[...].astype(F32) * gc * (1.0 - gc)).astype(BF16)
        duga_ref[...] = (dmv * ya_ref[...].astype(F32) * ga * (1.0 - ga)).astype(BF16)

    blk = pl.BlockSpec((ROWS, td), lambda i, j: (i, j))
    o = jax.ShapeDtypeStruct((t, d), BF16)
    return pl.pallas_call(
        body, name=name, grid=(t // ROWS, nd),
        in_specs=[blk, pl.BlockSpec((ROWS, td), lambda i, j: (i, c0 + j)),
                  pl.BlockSpec((ROWS, td), lambda i, j: (i, c0 + nd + j)), blk, blk],
        out_specs=(blk, blk, blk, blk), out_shape=(o, o, o, o),
        compiler_params=_cparams(("arbitrary", "arbitrary")),
    )(dm, u, u, yc, ya)


def _loss_and_grad(y, target, name):
    t, d = y.shape
    ROWS = _rows(t, 12 * d)
    n = t // ROWS

    def body(y_ref, t_ref, loss_ref, dy_ref, acc_ref):
        i = pl.program_id(0)

        @pl.when(i == 0)
        def _():
            acc_ref[...] = jnp.zeros_like(acc_ref)

        diff = y_ref[...] - t_ref[...]
        dy_ref[...] = diff * (1.0 / d)
        acc_ref[...] += jnp.sum(diff * diff, axis=0, keepdims=True)

        @pl.when(i == n - 1)
        def _():
            loss_ref[...] = jnp.sum(acc_ref[...], axis=-1, keepdims=True) * (0.5 / d)

    row = pl.BlockSpec((ROWS, d), lambda i: (i, 0))
    return pl.pallas_call(
        body, name=name, grid=(n,), in_specs=[row, row],
        out_specs=(pl.BlockSpec((1, 1), lambda i: (0, 0)), row),
        out_shape=(jax.ShapeDtypeStruct((1, 1), F32), jax.ShapeDtypeStruct((t, d), F32)),
        scratch_shapes=[pltpu.VMEM((1, d), F32)], compiler_params=_cparams(("arbitrary",)),
    )(y, target)


HALO = 32


def _conv_fwd(u, w, b, cdim, name):
    t = u.shape[0]
    ncb = cdim // LANES
    nt = t // BLOCK

    def body(a_ref, g_ref, w_ref, b_ref, zc_ref, zpad):
        zpad[0:HALO, :] = jnp.zeros((HALO, LANES), F32)
        zpad[HALO:HALO + t, :] = a_ref[...].astype(F32) * _sigmoid(g_ref[...].astype(F32))
        wv = w_ref[...]
        bv = b_ref[...]

        def tile(i, carry):
            r0 = pl.multiple_of(i * BLOCK, BLOCK)
            win = zpad[pl.ds(r0, BLOCK + HALO), :]
            acc = jnp.zeros((BLOCK, LANES), F32) + bv
            for b in range(SUBLANES):
                sh = win if b == 0 else pltpu.roll(win, b, 0)
                for a in range(HALO // SUBLANES):
                    j = CONV_WIDTH - 1 - (SUBLANES * a + b)
                    if j >= 0:
                        lo = HALO - SUBLANES * a
                        acc = acc + wv[j:j + 1, :] * sh[lo:lo + BLOCK, :]
            zc_ref[pl.ds(r0, BLOCK), :] = acc
            return carry

        lax.fori_loop(0, nt, tile, 0)

    col = lambda off: pl.BlockSpec((t, LANES), lambda c: (0, off + c))
    return pl.pallas_call(
        body, name=name, grid=(ncb,),
        in_specs=[col(0), col(ncb), pl.BlockSpec((CONV_TAPS_PADDED, LANES), lambda c: (0, c)),
                  pl.BlockSpec((1, LANES), lambda c: (0, c))],
        out_specs=pl.BlockSpec((t, LANES), lambda c: (0, c)),
        out_shape=jax.ShapeDtypeStruct((t, cdim), F32),
        scratch_shapes=[pltpu.VMEM((t + HALO, LANES), F32)], compiler_params=_cparams(("arbitrary",)),
    )(u, u, w, b.reshape(1, cdim))


def _conv_bwd(dzc, u, w, cdim, name):
    t = u.shape[0]
    ncb = cdim // LANES
    nt = t // BLOCK
    win_rows = BLOCK + HALO

    def body(dzc_ref, a_ref, g_ref, w_ref, da_ref, dg_ref, dw_ref, db_ref, zpad, dpad):
        av = a_ref[...].astype(F32)
        sg = _sigmoid(g_ref[...].astype(F32))
        zpad[0:HALO, :] = jnp.zeros((HALO, LANES), F32)
        zpad[HALO:HALO + t, :] = av * sg
        dpad[0:t, :] = dzc_ref[...]
        dpad[t:t + HALO, :] = jnp.zeros((HALO, LANES), F32)
        dw_ref[...] = jnp.zeros_like(dw_ref)
        db_ref[...] = jnp.sum(dzc_ref[...], axis=0, keepdims=True)
        wv = w_ref[...]

        def tile(i, carry):
            r0 = pl.multiple_of(i * BLOCK, BLOCK)
            zwin = zpad[pl.ds(r0, win_rows), :]
            dwin = dpad[pl.ds(r0, win_rows), :]
            dcur = dwin[0:BLOCK, :]
            dz = jnp.zeros((BLOCK, LANES), F32)
            for b in range(SUBLANES):
                zs = zwin if b == 0 else pltpu.roll(zwin, b, 0)
                ds = dwin if b == 0 else pltpu.roll(dwin, win_rows - b, 0)
                for a in range(HALO // SUBLANES):
                    j = CONV_WIDTH - 1 - (SUBLANES * a + b)
                    if j >= 0:
                        lo = HALO - SUBLANES * a
                        dw_ref[j:j + 1, :] += jnp.sum(dcur * zs[lo:lo + BLOCK, :], axis=0, keepdims=True)
                        dz = dz + wv[j:j + 1, :] * ds[SUBLANES * a:SUBLANES * a + BLOCK, :]
            ac = a_ref[pl.ds(r0, BLOCK), :].astype(F32)
            sc = _sigmoid(g_ref[pl.ds(r0, BLOCK), :].astype(F32))
            da_ref[pl.ds(r0, BLOCK), :] = (dz * sc).astype(BF16)
            dg_ref[pl.ds(r0, BLOCK), :] = (dz * ac * sc * (1.0 - sc)).astype(BF16)
            return carry

        lax.fori_loop(0, nt, tile, 0)

    col = lambda off: pl.BlockSpec((t, LANES), lambda c: (0, off + c))
    wspec = pl.BlockSpec((CONV_TAPS_PADDED, LANES), lambda c: (0, c))
    o = jax.ShapeDtypeStruct((t, cdim), BF16)
    return pl.pallas_call(
        body, name=name, grid=(ncb,), in_specs=[col(0), col(0), col(ncb), wspec],
        out_specs=(col(0), col(0), wspec, pl.BlockSpec((1, LANES), lambda c: (0, c))),
        out_shape=(o, o, jax.ShapeDtypeStruct((CONV_TAPS_PADDED, cdim), F32), jax.ShapeDtypeStruct((1, cdim), F32)),
        scratch_shapes=[pltpu.VMEM((t + HALO, LANES), F32), pltpu.VMEM((t + HALO, LANES), F32)],
        compiler_params=_cparams(("arbitrary",)),
    )(dzc, u, u, w)


def _ln_swish_fwd(zc, g, b, name):
    t, c = zc.shape
    ROWS = _rows(t, 6 * c)

    def body(z_ref, g_ref, b_ref, o_ref):
        z = z_ref[...]
        mu = jnp.mean(z, axis=-1, keepdims=True)
        zc_ = z - mu
        zn = zc_ * lax.rsqrt(jnp.mean(zc_ * zc_, axis=-1, keepdims=True) + EPS)
        y = zn * g_ref[...] + b_ref[...]
        o_ref[...] = (y * _sigmoid(y)).astype(BF16)

    row = pl.BlockSpec((ROWS, c), lambda i: (i, 0))
    vec = pl.BlockSpec((1, c), lambda i: (0, 0))
    return pl.pallas_call(
        body, name=name, grid=(t // ROWS,), in_specs=[row, vec, vec], out_specs=row,
        out_shape=jax.ShapeDtypeStruct((t, c), BF16), compiler_params=_cparams(("arbitrary",)),
    )(zc, g.reshape(1, c), b.reshape(1, c))


def _ln_swish_bwd(dzs, zc, g, b, name):
    t, c = zc.shape
    ROWS = _rows(t, 12 * c)

    def body(d_ref, z_ref, g_ref, b_ref, dz_ref, dg_ref, db_ref):
        @pl.when(pl.program_id(0) == 0)
        def _():
            dg_ref[...] = jnp.zeros_like(dg_ref)
            db_ref[...] = jnp.zeros_like(db_ref)

        z = z_ref[...]
        mu = jnp.mean(z, axis=-1, keepdims=True)
        zc_ = z - mu
        rstd = lax.rsqrt(jnp.mean(zc_ * zc_, axis=-1, keepdims=True) + EPS)
        zn = zc_ * rstd
        y = zn * g_ref[...] + b_ref[...]
        sg = _sigmoid(y)
        dy = d_ref[...] * (sg * (1.0 + y * (1.0 - sg)))
        dg_ref[...] += jnp.sum(dy * zn, axis=0, keepdims=True)
        db_ref[...] += jnp.sum(dy, axis=0, keepdims=True)
        dzn = dy * g_ref[...]
        dz_ref[...] = rstd * (dzn - jnp.mean(dzn, axis=-1, keepdims=True)
                              - zn * jnp.mean(dzn * zn, axis=-1, keepdims=True))

    row = pl.BlockSpec((ROWS, c), lambda i: (i, 0))
    vec = pl.BlockSpec((1, c), lambda i: (0, 0))
    v = jax.ShapeDtypeStruct((1, c), F32)
    return pl.pallas_call(
        body, name=name, grid=(t // ROWS,), in_specs=[row, row, vec, vec], out_specs=(row, vec, vec),
        out_shape=(jax.ShapeDtypeStruct((t, c), F32), v, v), compiler_params=_cparams(("arbitrary",)),
    )(dzs, zc, g.reshape(1, c), b.reshape(1, c))


def _bucket_table():
    qi = np.arange(BLOCK)[:, None]
    kj = np.arange(2 * BLOCK)[None, :]
    off = qi + BLOCK - kj
    band = (off >= 0) & (off <= SUB_WINDOW)
    max_exact = NUM_BUCKETS // 2
    out = []
    for d in DILATIONS:
        dist = (np.clip(off, 0, SUB_WINDOW) * d).astype(np.int32)
        nf = np.maximum(dist, 1).astype(np.float32)
        large = max_exact + (np.log(nf / np.float32(max_exact)) / np.float32(math.log(MAX_REL_DISTANCE / max_exact))
                             * np.float32(NUM_BUCKETS - max_exact)).astype(np.int32)
        large = np.minimum(large, NUM_BUCKETS - 1)
        bucket = np.where(dist < max_exact, dist, large)
        out.append(np.where(band, bucket, -1))
    return np.stack(out).astype(np.int32)


def _bias_expand(rel_bias, buckets, hpg, name):
    nh = N_GROUPS * hpg

    def body(rb_ref, bk_ref, o_ref):
        h = pl.program_id(0)
        bk = bk_ref[0]
        acc = jnp.full((BLOCK, 2 * BLOCK), NEG_INF, F32)
        for bb in range(NUM_BUCKETS):
            acc = jnp.where(bk == bb, rb_ref[bb, h], acc)
        o_ref[0] = acc

    return pl.pallas_call(
        body, name=name, grid=(nh,),
        in_specs=[pl.BlockSpec(memory_space=pltpu.SMEM),
                  pl.BlockSpec((1, BLOCK, 2 * BLOCK), lambda h: (h // hpg, 0, 0))],
        out_specs=pl.BlockSpec((1, BLOCK, 2 * BLOCK), lambda h: (h, 0, 0)),
        out_shape=jax.ShapeDtypeStruct((nh, BLOCK, 2 * BLOCK), F32), compiler_params=_cparams(("arbitrary",)),
    )(rel_bias, buckets)


def _bias_reduce(ds_sum, buckets, hpg, name):
    nh = N_GROUPS * hpg

    def body(ds_ref, bk_ref, o_ref):
        bk = bk_ref[0]
        dsv = ds_ref[0]
        lane = lax.broadcasted_iota(jnp.int32, (1, LANES), 1)
        row = jnp.zeros((1, LANES), F32)
        for bb in range(NUM_BUCKETS):
            tot = jnp.sum(jnp.sum(jnp.where(bk == bb, dsv, 0.0), axis=-1, keepdims=True), axis=0, keepdims=True)
            row = jnp.where(lane == bb, tot, row)
        o_ref[0] = row

    return pl.pallas_call(
        body, name=name, grid=(nh,),
        in_specs=[pl.BlockSpec((1, BLOCK, 2 * BLOCK), lambda h: (h, 0, 0)),
                  pl.BlockSpec((1, BLOCK, 2 * BLOCK), lambda h: (h // hpg, 0, 0))],
        out_specs=pl.BlockSpec((1, 1, LANES), lambda h: (h, 0, 0)),
        out_shape=jax.ShapeDtypeStruct((nh, 1, LANES), F32), compiler_params=_cparams(("arbitrary",)),
    )(ds_sum, buckets)


def _chunk_rows(c, d, nb):
    r, n = c // nb, c % nb
    if d == 1:
        return pl.ds(c * BLOCK, BLOCK)
    return pl.ds(r + n * BLOCK * d, BLOCK, stride=d)


def _segment_ones():
    i = lax.broadcasted_iota(jnp.int32, (LANES, LANES), 0) // HEAD_DIM
    j = lax.broadcasted_iota(jnp.int32, (LANES, LANES), 1) // HEAD_DIM
    return (i == j).astype(BF16)


def _segment_sum(v, seg):
    hi = v.astype(BF16)
    lo = (v - hi.astype(F32)).astype(BF16)
    return jnp.dot(hi, seg, preferred_element_type=F32) + jnp.dot(lo, seg, preferred_element_type=F32)


def _head_mean(v, seg):
    return _segment_sum(v, seg) * (1.0 / HEAD_DIM)


def _attn_fwd(u, qg2, kg2, bias, gi, cols, hpg, name):
    t = u.shape[0]
    d = DILATIONS[gi]
    nchunk = t // BLOCK
    nb = (t // d) // BLOCK
    hp = hpg // 2
    qc0, kc0, vc0 = [(c + gi * hpg * HEAD_DIM) // LANES for c in cols]
    contract_lanes = (((1,), (1,)), ((), ()))

    def body(q_ref, k_ref, v_ref, qg_ref, kg_ref, bias_ref, o_ref, lse_ref, qd, kd, vd, od, ld, sbuf):
        seg = _segment_ones()
        lane = lax.broadcasted_iota(jnp.int32, (1, LANES), 1)
        qg = qg_ref[...] * (HEAD_DIM ** -0.5)
        kg = kg_ref[...]
        kd[0:BLOCK, :] = jnp.zeros((BLOCK, LANES), BF16)
        vd[0:BLOCK, :] = jnp.zeros((BLOCK, LANES), BF16)
        od[...] = q_ref[...].astype(F32)
        ld[...] = k_ref[...].astype(F32)
        for c in range(nchunk):
            rows = _chunk_rows(c, d, nb)
            qv = od[rows, :]
            kv = ld[rows, :]
            qd[c * BLOCK:(c + 1) * BLOCK, :] = (qv * lax.rsqrt(_head_mean(qv * qv, seg) + EPS) * qg).astype(BF16)
            kd[(c + 1) * BLOCK:(c + 2) * BLOCK, :] = (kv * lax.rsqrt(_head_mean(kv * kv, seg) + EPS) * kg).astype(BF16)
        od[...] = v_ref[...].astype(F32)
        for c in range(nchunk):
            vd[(c + 1) * BLOCK:(c + 2) * BLOCK, :] = od[_chunk_rows(c, d, nb), :].astype(BF16)

        col = lax.broadcasted_iota(jnp.int32, (BLOCK, 2 * BLOCK), 1)
        for j in range(2):
            mj = jnp.logical_and(lane >= j * HEAD_DIM, lane < (j + 1) * HEAD_DIM)
            for c in range(nchunk):
                kw = kd[c * BLOCK:(c + 2) * BLOCK, :]
                kj = jnp.where(mj, kw, jnp.zeros_like(kw))
                s = lax.dot_general(qd[c * BLOCK:(c + 1) * BLOCK, :], kj, contract_lanes,
                                    preferred_element_type=F32) + bias_ref[j]
                if c % nb == 0:
                    s = jnp.where(col < BLOCK, NEG_INF, s)
                sbuf[c] = s
            for c in range(nchunk):
                rows = slice(c * BLOCK, (c + 1) * BLOCK)
                s = sbuf[c]
                mx = jnp.max(s, axis=-1, keepdims=True)
                p = jnp.exp(s - mx).astype(BF16)
                vw = vd[c * BLOCK:(c + 2) * BLOCK, :]
                oj = jnp.dot(p, jnp.where(mj, vw, jnp.ones_like(vw)), preferred_element_type=F32)
                l = pltpu.roll(oj, HEAD_DIM, 1)
                on = oj / l
                ls = mx + jnp.log(l)
                if j == 0:
                    od[rows, :] = on
                    ld[rows, :] = ls
                else:
                    od[rows, :] = jnp.where(mj, on, od[rows, :])
                    ld[rows, :] = jnp.where(mj, ls, ld[rows, :])

        for c in range(nchunk):
            rows = _chunk_rows(c, d, nb)
            o_ref[rows, :] = od[c * BLOCK:(c + 1) * BLOCK, :]
            lse_ref[rows, :] = ld[c * BLOCK:(c + 1) * BLOCK, :]

    ucol = lambda c0: pl.BlockSpec((t, LANES), lambda h: (0, c0 + h))
    vec = pl.BlockSpec((1, LANES), lambda h: (0, 0))
    oblk = pl.BlockSpec((t, LANES), lambda h: (0, h))
    osh = jax.ShapeDtypeStruct((t, hpg * HEAD_DIM), F32)
    return pl.pallas_call(
        body, name=name, grid=(hp,),
        in_specs=[ucol(qc0), ucol(kc0), ucol(vc0), vec, vec,
                  pl.BlockSpec((2, BLOCK, 2 * BLOCK), lambda h: (gi * hp + h, 0, 0))],
        out_specs=(oblk, oblk), out_shape=(osh, osh),
        scratch_shapes=[pltpu.VMEM((t, LANES), BF16), pltpu.VMEM((t + BLOCK, LANES), BF16),
                        pltpu.VMEM((t + BLOCK, LANES), BF16), pltpu.VMEM((t, LANES), F32), pltpu.VMEM((t, LANES), F32),
                        pltpu.VMEM((nchunk, BLOCK, 2 * BLOCK), F32)],
        compiler_params=_cparams(("arbitrary",)),
    )(u, u, u, qg2, kg2, bias)


def _attn_bwd(u, do_g, dd_g, lse_g, qg2, kg2, bias, ds_in, du_in, gi, cols, hpg, name):
    t = u.shape[0]
    d = DILATIONS[gi]
    nchunk = t // BLOCK
    nb = (t // d) // BLOCK
    hp = hpg // 2
    qc0, kc0, vc0 = [(c + gi * hpg * HEAD_DIM) // LANES for c in cols]
    contract_lanes = (((1,), (1,)), ((), ()))
    contract_rows = (((0,), (0,)), ((), ()))
    qscale = HEAD_DIM ** -0.5

    def body(q_ref, k_ref, v_ref, do_ref, dd_ref, lse_ref, qg_ref, kg_ref, bias_ref, dsin_ref, _du_in,
             dgq_ref, dgk_ref, dsout_ref, du_ref,
             qd, kd, vd, dod, ddd, ld, dqd, dkd, dvd, dsacc, pbuf, dsbuf, qs, ks, qst, kst, vst, out_sems):
        h = pl.program_id(0)

        def flush(step):
            return [pltpu.make_async_copy(
                st, du_ref.at[:, pl.ds(pl.multiple_of((c0 + step) * LANES, LANES), LANES)], out_sems.at[i])
                for i, (st, c0) in enumerate(((qst, qc0), (kst, kc0), (vst, vc0)))]

        seg = _segment_ones()
        lane = lax.broadcasted_iota(jnp.int32, (1, LANES), 1)
        qg = qg_ref[...] * qscale
        kg = kg_ref[...]
        kd[0:BLOCK, :] = jnp.zeros((BLOCK, LANES), BF16)
        vd[0:BLOCK, :] = jnp.zeros((BLOCK, LANES), BF16)
        dsacc[...] = jnp.zeros_like(dsacc)
        qs[...] = q_ref[...].astype(F32)
        ks[...] = k_ref[...].astype(F32)
        dqd[...] = v_ref[...].astype(F32)
        for c in range(nchunk):
            rows = _chunk_rows(c, d, nb)
            qv = qs[rows, :]
            kv = ks[rows, :]
            qd[c * BLOCK:(c + 1) * BLOCK, :] = (qv * lax.rsqrt(_head_mean(qv * qv, seg) + EPS) * qg).astype(BF16)
            kd[(c + 1) * BLOCK:(c + 2) * BLOCK, :] = (kv * lax.rsqrt(_head_mean(kv * kv, seg) + EPS) * kg).astype(BF16)
            vd[(c + 1) * BLOCK:(c + 2) * BLOCK, :] = dqd[rows, :].astype(BF16)
            dod[c * BLOCK:(c + 1) * BLOCK, :] = do_ref[rows, :].astype(BF16)
            ddd[c * BLOCK:(c + 1) * BLOCK, :] = dd_ref[rows, :]
            ld[c * BLOCK:(c + 1) * BLOCK, :] = lse_ref[rows, :]

        col = lax.broadcasted_iota(jnp.int32, (BLOCK, 2 * BLOCK), 1)
        for j in range(2):
            mj = jnp.logical_and(lane >= j * HEAD_DIM, lane < (j + 1) * HEAD_DIM)
            first = lane == j * HEAD_DIM
            for c in range(nchunk):
                rows = slice(c * BLOCK, (c + 1) * BLOCK)
                kw = kd[c * BLOCK:(c + 2) * BLOCK, :]
                vw = vd[c * BLOCK:(c + 2) * BLOCK, :]
                kj = jnp.where(mj, kw, jnp.zeros_like(kw))
                vj = jnp.where(mj, vw, jnp.zeros_like(vw))
                s = lax.dot_general(qd[rows, :], kj, contract_lanes, preferred_element_type=F32) + bias_ref[j]
                if c % nb == 0:
                    s = jnp.where(col < BLOCK, NEG_INF, s)
                dp = lax.dot_general(dod[rows, :], vj, contract_lanes, preferred_element_type=F32)
                lse_j = jnp.sum(jnp.where(first, ld[rows, :], 0.0), axis=-1, keepdims=True)
                dd_j = jnp.sum(jnp.where(first, ddd[rows, :], 0.0), axis=-1, keepdims=True)
                p = jnp.exp(s - lse_j)
                ds = p * (dp + dd_j)
                dsacc[j] += ds
                pbuf[j, c] = p.astype(BF16)
                dsbuf[j, c] = ds.astype(BF16)
        for c in range(nchunk):
            rows = slice(c * BLOCK, (c + 1) * BLOCK)
            has_next = c + 1 < nchunk and (c + 1) % nb != 0
            kw = kd[c * BLOCK:(c + 2) * BLOCK, :]
            dq = jnp.zeros((BLOCK, LANES), F32)
            dk = jnp.zeros((BLOCK, LANES), F32)
            dv = jnp.zeros((BLOCK, LANES), F32)
            both = slice(c * BLOCK, (c + 2) * BLOCK) if has_next else rows
            for j in range(2):
                mj = jnp.logical_and(lane >= j * HEAD_DIM, lane < (j + 1) * HEAD_DIM)
                dq = dq + jnp.dot(dsbuf[j, c], jnp.where(mj, kw, jnp.zeros_like(kw)), preferred_element_type=F32)
                dsk = dsbuf[j, c, :, BLOCK:]
                pk = pbuf[j, c, :, BLOCK:]
                if has_next:
                    dsk = jnp.concatenate([dsk, dsbuf[j, c + 1, :, :BLOCK]], axis=0)
                    pk = jnp.concatenate([pk, pbuf[j, c + 1, :, :BLOCK]], axis=0)
                qq = qd[both, :]
                dd = dod[both, :]
                dk = dk + lax.dot_general(dsk, jnp.where(mj, qq, jnp.zeros_like(qq)), contract_rows,
                                          preferred_element_type=F32)
                dv = dv + lax.dot_general(pk, jnp.where(mj, dd, jnp.zeros_like(dd)), contract_rows,
                                          preferred_element_type=F32)
            dqd[rows, :] = dq
            dkd[rows, :] = dk
            dvd[rows, :] = dv

        dsout_ref[...] = dsin_ref[...] + dsacc[...]

        dgq = jnp.zeros((1, LANES), F32)
        dgk = jnp.zeros((1, LANES), F32)
        for c in range(nchunk):
            rows = _chunk_rows(c, d, nb)
            qv = qs[rows, :]
            rq = lax.rsqrt(_head_mean(qv * qv, seg) + EPS)
            qh = qv * rq
            dy = dqd[c * BLOCK:(c + 1) * BLOCK, :]
            dgq = dgq + jnp.sum(dy * qh, axis=0, keepdims=True) * qscale
            dxh = dy * qg
            ddd[rows, :] = rq * (dxh - qh * _head_mean(dxh * qh, seg))
            kv = ks[rows, :]
            rk = lax.rsqrt(_head_mean(kv * kv, seg) + EPS)
            kh = kv * rk
            dy = dkd[c * BLOCK:(c + 1) * BLOCK, :]
            dgk = dgk + jnp.sum(dy * kh, axis=0, keepdims=True)
            dxh = dy * kg
            ld[rows, :] = rk * (dxh - kh * _head_mean(dxh * kh, seg))
        @pl.when(h > 0)
        def _():
            for cp in flush(h - 1):
                cp.wait()

        qst[...] = ddd[...].astype(BF16)
        kst[...] = ld[...].astype(BF16)
        for c in range(nchunk):
            ddd[_chunk_rows(c, d, nb), :] = dvd[c * BLOCK:(c + 1) * BLOCK, :]
        vst[...] = ddd[...].astype(BF16)
        for cp in flush(h):
            cp.start()

        @pl.when(h == hp - 1)
        def _():
            for cp in flush(h):
                cp.wait()

        dgq_ref[0] = dgq
        dgk_ref[0] = dgk

    ucol = lambda c0: pl.BlockSpec((t, LANES), lambda h: (0, c0 + h))
    vec = pl.BlockSpec((1, LANES), lambda h: (0, 0))
    oblk = pl.BlockSpec((t, LANES), lambda h: (0, h))
    bblk = pl.BlockSpec((2, BLOCK, 2 * BLOCK), lambda h: (gi * hp + h, 0, 0))
    gblk = pl.BlockSpec((1, 1, LANES), lambda h: (h, 0, 0))
    gsh = jax.ShapeDtypeStruct((hp, 1, LANES), F32)
    hbm = pl.BlockSpec(memory_space=pl.ANY)
    return pl.pallas_call(
        body, name=name, grid=(hp,),
        in_specs=[ucol(qc0), ucol(kc0), ucol(vc0), oblk, oblk, oblk, vec, vec, bblk, bblk, hbm],
        out_specs=(gblk, gblk, bblk, hbm),
        out_shape=(gsh, gsh, jax.ShapeDtypeStruct(ds_in.shape, F32), jax.ShapeDtypeStruct(du_in.shape, BF16)),
        input_output_aliases={9: 2, 10: 3},
        scratch_shapes=[pltpu.VMEM((t, LANES), BF16), pltpu.VMEM((t + BLOCK, LANES), BF16),
                        pltpu.VMEM((t + BLOCK, LANES), BF16), pltpu.VMEM((t, LANES), BF16),
                        pltpu.VMEM((t, LANES), F32), pltpu.VMEM((t, LANES), F32), pltpu.VMEM((t, LANES), F32),
                        pltpu.VMEM((t, LANES), F32), pltpu.VMEM((t, LANES), F32),
                        pltpu.VMEM((2, BLOCK, 2 * BLOCK), F32), pltpu.VMEM((2, nchunk, BLOCK, 2 * BLOCK), BF16),
                        pltpu.VMEM((2, nchunk, BLOCK, 2 * BLOCK), BF16), pltpu.VMEM((t, LANES), F32),
                        pltpu.VMEM((t, LANES), F32), pltpu.VMEM((t, LANES), BF16), pltpu.VMEM((t, LANES), BF16),
                        pltpu.VMEM((t, LANES), BF16), pltpu.SemaphoreType.DMA((3,))],
        compiler_params=_cparams(("arbitrary",)),
    )(u, u, u, do_g, dd_g, lse_g, qg2, kg2, bias, ds_in, du_in)


def _group_weights(l0, l1, l2):
    mx = jnp.maximum(jnp.maximum(l0, l1), l2)
    e0, e1, e2 = jnp.exp(l0 - mx), jnp.exp(l1 - mx), jnp.exp(l2 - mx)
    inv = 1.0 / (e0 + e1 + e2)
    return e0 * inv, e1 * inv, e2 * inv


def _combine_fwd(os_, lses, name):
    t, ao = os_[0].shape
    ROWS = _rows(t, 26 * ao)

    def body(o0, o1, o2, l0, l1, l2, o_ref):
        w0, w1, w2 = _group_weights(l0[...], l1[...], l2[...])
        o_ref[...] = (w0 * o0[...] + w1 * o1[...] + w2 * o2[...]).astype(BF16)

    row = pl.BlockSpec((ROWS, ao), lambda i: (i, 0))
    return pl.pallas_call(
        body, name=name, grid=(t // ROWS,), in_specs=[row] * 6, out_specs=row,
        out_shape=jax.ShapeDtypeStruct((t, ao), BF16), compiler_params=_cparams(("arbitrary",)),
    )(*os_, *lses)


def _combine_bwd(do, os_, lses, name):
    t, ao = do.shape
    idx = np.arange(ao) // HEAD_DIM
    seg = jnp.asarray((idx[:, None] == idx[None, :]).astype(np.float32), dtype=BF16)
    ROWS = _rows(t, 52 * ao)

    def body(do_ref, o0, o1, o2, l0, l1, l2, seg_ref, g0, g1, g2, d0, d1, d2):
        w0, w1, w2 = _group_weights(l0[...], l1[...], l2[...])
        dov = do_ref[...]
        o = w0 * o0[...] + w1 * o1[...] + w2 * o2[...]
        sd = _segment_sum(dov * o, seg_ref[...])
        for w, gref, dref in ((w0, g0, d0), (w1, g1, d1), (w2, g2, d2)):
            gref[...] = w * dov
            dref[...] = -(w * sd)

    row = pl.BlockSpec((ROWS, ao), lambda i: (i, 0))
    sh = jax.ShapeDtypeStruct((t, ao), F32)
    outs = pl.pallas_call(
        body, name=name, grid=(t // ROWS,), in_specs=[row] * 7 + [pl.BlockSpec((ao, ao), lambda i: (0, 0))],
        out_specs=(row,) * 6, out_shape=(sh,) * 6, compiler_params=_cparams(("arbitrary",)),
    )(do, *os_, *lses, seg)
    return outs[:3], outs[3:]


def _adamw(w, g, m, v, name):
    shape = w.shape
    cols = shape[-1]
    rows = int(np.prod(shape[:-1]))
    tr = rows if rows <= 512 else _tile_rows(rows)
    c1 = 1.0 - ADAM_B1 ** ADAM_STEP
    c2 = 1.0 - ADAM_B2 ** ADAM_STEP

    def body(w_ref, g_ref, m_ref, v_ref, d_ref, nm_ref, nv_ref):
        gv = g_ref[...]
        mn = ADAM_B1 * m_ref[...] + (1.0 - ADAM_B1) * gv
        vn = ADAM_B2 * v_ref[...] + (1.0 - ADAM_B2) * (gv * gv)
        nm_ref[...] = mn
        nv_ref[...] = vn
        d_ref[...] = -ADAM_LR * ((mn / c1) / (jnp.sqrt(vn / c2) + ADAM_EPS) + ADAM_WD * w_ref[...])

    blk = pl.BlockSpec((tr, cols), lambda i: (i, 0))
    sh = jax.ShapeDtypeStruct((rows, cols), F32)
    outs = pl.pallas_call(
        body, name=name, grid=(rows // tr,), in_specs=[blk] * 4, out_specs=(blk,) * 3, out_shape=(sh,) * 3,
        compiler_params=_cparams(("arbitrary",)),
    )(*[a.reshape(rows, cols) for a in (w, g, m, v)])
    return tuple(o.reshape(shape) for o in outs)


def _tile_rows(rows):
    for t in (512, 256, 128, 64, 32, 16, 8):
        if rows % t == 0:
            return t
    return rows


def _sum_slots(recv, parts, me, layers, l, name):
    _, rows, cols = recv.shape
    tr = rows if rows <= 512 else _tile_rows(rows)

    def body(me_ref, r_ref, own_ref, _, o_ref):
        acc = jnp.zeros(o_ref.shape, F32)
        for s in range(N_DEV):
            acc = acc + jnp.where(me_ref[0] == s, own_ref[...], r_ref[s]).astype(F32)
        o_ref[...] = acc

    return pl.pallas_call(
        body, name=name,
        grid_spec=pltpu.PrefetchScalarGridSpec(
            num_scalar_prefetch=1, grid=(rows // tr,),
            in_specs=[pl.BlockSpec((N_DEV, tr, cols), lambda i, me: (0, i, 0)),
                      pl.BlockSpec((None, tr, cols), lambda i, me: (me[0], i, 0)),
                      pl.BlockSpec(memory_space=pl.ANY)],
            out_specs=pl.BlockSpec((None, tr, cols), lambda i, me: (l, i, 0))),
        out_shape=jax.ShapeDtypeStruct(layers.shape, F32), input_output_aliases={3: 0},
        compiler_params=_cparams(("arbitrary",)),
    )(me.reshape(1), recv, parts, layers)


def _peer(k):
    x, y, c = lax.axis_index("x"), lax.axis_index("y"), lax.axis_index("c")
    return (1 - x if k & 4 else x, 1 - y if k & 2 else y, 1 - c if k & 1 else c)


def _dev_index(p):
    return 4 * p[0] + 2 * p[1] + p[2]


HBM_SPEC = pl.BlockSpec(memory_space=pltpu.HBM)
SEM_SPEC = pl.BlockSpec(memory_space=pltpu.SEMAPHORE)
ANY_SPEC = pl.BlockSpec(memory_space=pl.ANY)
CHIPS = (4, 2, 6)


def _remote(src, dst, send_sem, recv_sem, to):
    return pltpu.make_async_remote_copy(src_ref=src, dst_ref=dst, send_sem=send_sem, recv_sem=recv_sem,
                                        device_id=to, device_id_type=MESH)


def _hbm(a):
    return pltpu.with_memory_space_constraint(a, pltpu.HBM)


def _split_call(body, name, bufs, sems_in, sem_out_sizes, after):
    nb, ns, no = len(bufs), len(sems_in), len(sem_out_sizes)
    extra = [] if after is None else list(after) if isinstance(after, (tuple, list)) else [after]

    def kern(*refs):
        pos = nb + ns + len(extra)
        body(refs[:nb], refs[nb:nb + ns], refs[pos:pos + no])
        token_ref = refs[pos + no + nb]
        token_ref[...] = jnp.zeros_like(token_ref)

    out_shape = (tuple(pltpu.SemaphoreType.DMA((s,)) for s in sem_out_sizes)
                 + tuple(pltpu.HBM(b.shape, b.dtype) for b in bufs) + (jax.ShapeDtypeStruct((8, LANES), F32),))
    res = pl.pallas_call(
        kern, name=name, out_shape=out_shape,
        in_specs=[HBM_SPEC] * nb + [SEM_SPEC] * ns + [ANY_SPEC] * len(extra),
        out_specs=(SEM_SPEC,) * no + (HBM_SPEC,) * nb + (pl.BlockSpec(memory_space=pltpu.VMEM),),
        input_output_aliases={i: no + i for i in range(nb)},
        compiler_params=pltpu.CompilerParams(has_side_effects=pltpu.SideEffectType.DATAFLOW_SIDE_EFFECTING),
    )(*bufs, *sems_in, *extra)
    return res[:no], res[no:no + nb], res[no + nb]


def _gather_start(shards, lands, after, name):
    n = len(shards)

    def body(bufs, _, sems):
        ins, lnd = bufs[:n], bufs[n:]
        d2d_s, d2d_r, ici_s, ici_r = sems
        me = _dev_index(_peer(0))
        for j, k in enumerate(CHIPS):
            for i in range(n):
                _remote(ins[i], lnd[i].at[me], ici_s.at[j], ici_r.at[j], _peer(k)).start()
        for i in range(n):
            _remote(ins[i], lnd[i].at[me], d2d_s.at[0], d2d_r.at[0], _peer(1)).start()

    return _split_call(body, name, [_hbm(a) for a in (*shards, *lands)], [], (1, 1, 3, 3), after)


def _gather_forward(n, bufs, ici_r, after, name):
    def body(refs, sems_in, sems):
        ins, lnd = refs[:n], refs[n:]
        (arrived,) = sems_in
        fwd_s, fwd_r = sems
        for j, k in enumerate(CHIPS):
            blk = _dev_index(_peer(k))
            for i in range(n):
                _remote(ins[i], lnd[i].at[blk], fwd_s.at[j], arrived.at[j], _peer(k)).wait_recv()
            for i in range(n):
                _remote(lnd[i].at[blk], lnd[i].at[blk], fwd_s.at[j], fwd_r.at[j], _peer(1)).start()

    return _split_call(body, name, bufs, [ici_r], (3, 3), after)


def _gather_finish(n, bufs, d2d_s, d2d_r, ici_s, fwd_s, fwd_r, after, name):
    def body(refs, sems_in, _):
        ins, lnd = refs[:n], refs[n:]
        d2d_send, d2d_recv, ici_send, fwd_send, fwd_recv = sems_in
        sib = _peer(1)
        for i in range(n):
            cp = _remote(ins[i], lnd[i].at[_dev_index(sib)], d2d_send.at[0], d2d_recv.at[0], sib)
            cp.wait_send()
            cp.wait_recv()
        for j, k in enumerate(CHIPS):
            passed = _dev_index(_peer(k))
            landed = _dev_index(_peer(k | 1))
            for i in range(n):
                _remote(ins[i], lnd[i].at[passed], ici_send.at[j], fwd_recv.at[j], _peer(k)).wait_send()
                cp = _remote(lnd[i].at[passed], lnd[i].at[landed], fwd_send.at[j], fwd_recv.at[j], sib)
                cp.wait_send()
                cp.wait_recv()

    _, out, token = _split_call(body, name, bufs, [d2d_s, d2d_r, ici_s, fwd_s, fwd_r], (), after)
    return out[n:], token


def _exchange_start(parts, lands, after, name):
    n = len(parts)

    def body(bufs, _, sems):
        src, lnd = bufs[:n], bufs[n:]
        send, recv = sems
        me = _dev_index(_peer(0))
        for k in (4, 5, 2, 3, 6, 7, 1):
            to = _peer(k)
            for i in range(n):
                _remote(src[i].at[_dev_index(to)], lnd[i].at[me], send.at[k - 1], recv.at[k - 1], to).start()

    return _split_call(body, name, [_hbm(a) for a in (*parts, *lands)], [], (7, 7), after)


def _exchange_finish(n, bufs, send, recv, after, name):
    def body(refs, sems_in, _):
        src, lnd = refs[:n], refs[n:]
        send_, recv_ = sems_in
        me = _dev_index(_peer(0))
        for k in range(1, N_DEV):
            frm = _peer(k)
            for i in range(n):
                cp = _remote(src[i].at[me], lnd[i].at[_dev_index(frm)], send_.at[k - 1], recv_.at[k - 1], frm)
                cp.wait_send()
                cp.wait_recv()

    _, out, token = _split_call(body, name, bufs, [send, recv], (), after)
    return out[:n], out[n:], token


def _all_reduce_small(v, after, name):
    rows = v.shape[0]

    def body(v_ref, _, o_ref, buf, send_sems, recv_sems):
        me = _dev_index(_peer(0))
        buf[me] = v_ref[...]
        copies = []
        for k in range(1, N_DEV):
            copies.append(pltpu.make_async_remote_copy(
                src_ref=v_ref, dst_ref=buf.at[me], send_sem=send_sems.at[k - 1], recv_sem=recv_sems.at[k - 1],
                device_id=_peer(k), device_id_type=MESH))
        for cp in copies:
            cp.start()
        for k in range(1, N_DEV):
            pltpu.make_async_remote_copy(
                src_ref=v_ref, dst_ref=buf.at[_dev_index(_peer(k))], send_sem=send_sems.at[k - 1],
                recv_sem=recv_sems.at[k - 1], device_id=_peer(k), device_id_type=MESH).wait_recv()
        for cp in copies:
            cp.wait_send()
        acc = buf[0]
        for s in range(1, N_DEV):
            acc = acc + buf[s]
        o_ref[...] = acc

    vm = pl.BlockSpec(memory_space=pltpu.VMEM)
    return pl.pallas_call(
        body, name=name, in_specs=[vm, pl.BlockSpec(memory_space=pl.ANY)], out_specs=vm,
        out_shape=jax.ShapeDtypeStruct(v.shape, F32),
        scratch_shapes=[pltpu.VMEM((N_DEV, rows, LANES), F32), pltpu.SemaphoreType.DMA((7,)),
                        pltpu.SemaphoreType.DMA((7,))],
    )(v, after)


def _columns(cdim, ao):
    q_col = 2 * cdim
    attn_dim = N_GROUPS * ao
    return (q_col, q_col + attn_dim, q_col + 2 * attn_dim), q_col + 3 * attn_dim


def _layer_fwd(x, h1, sm, bg, get_rest, bias, hpg, next_gain, at_ff1=None):
    cdim = sm["conv_ln_g"].shape[0]
    ao = hpg * HEAD_DIM
    cols, gate_col = _columns(cdim, ao)
    qg2 = jnp.tile(sm["q_norm_g"], 2).reshape(1, LANES)
    kg2 = jnp.tile(sm["k_norm_g"], 2).reshape(1, LANES)
    u = _mm_in_pairs(h1, bg["w_in"], "mm_in")
    zc = _conv_fwd(u, bg["conv_dw_w"], sm["conv_dw_b"], cdim, "conv_fwd")
    zs = _ln_swish_fwd(zc, sm["conv_ln_g"], sm["conv_ln_b"], "ln_swish_fwd")
    os_, lses = [], []
    for gi in range(N_GROUPS):
        o_g, lse_g = _attn_fwd(u, qg2, kg2, bias, gi, cols, hpg, "attn_fwd_g%d" % gi)
        os_.append(o_g)
        lses.append(lse_g)
    o = _combine_fwd(os_, lses, "combine_fwd")
    bg = {**bg, **get_rest(o)}
    mg, yc, ya = _gated_out(zs, o, bg["w_conv_out"], bg["w_attn_out"], u, gate_col, "gated_out")
    x1, h2 = _mm(mg, bg["w_out"], epi="res_rms", extra=x, gain=sm["norm2_g"], name="mm_out")
    f = _mm(h2, bg["w_ff1"], out_dtype=BF16, name="mm_ff1")
    after = at_ff1(f) if at_ff1 is not None else None
    if next_gain is None:
        x2, h_next = _mm(f, bg["w_ff2"], a_relu2=True, epi="res", extra=x1, after=after, name="mm_ff2"), None
    else:
        x2, h_next = _mm(f, bg["w_ff2"], a_relu2=True, epi="res_rms", extra=x1, gain=next_gain, after=after,
                         name="mm_ff2")
    saved = dict(x=x, h1=h1, u=u, zc=zc, zs=zs, yc=yc, os=os_, lses=lses, o=o, ya=ya, mg=mg, x1=x1, h2=h2, f=f,
                 qg2=qg2, kg2=kg2)
    return x2, h_next, saved, bg


GRAD_GROUPS = (("w_ff2", "w_ff1"), ("w_out", "w_conv_out", "w_attn_out", "conv_dw_w"), ("w_in",))


def _layer_bwd(dx, s, sm, bg, bias, ds_sum, after, emit):
    cdim = sm["conv_ln_g"].shape[0]
    ao = bg["w_attn_out"].shape[0]
    hpg = ao // HEAD_DIM
    cols, gate_col = _columns(cdim, ao)
    g = {}
    df = _mm(dx, bg["w_ff2"], tb=True, epi="drelu2", extra=s["f"], out_dtype=BF16, after=after, name="mm_dff2")
    g["w_ff2"] = _mm(s["f"], dx, ta=True, a_relu2=True, out_dtype=BF16, name="mm_gw_ff2")
    g["w_ff1"] = _mm(s["h2"], df, ta=True, out_dtype=BF16, out_slots=True, name="mm_gw_ff1")
    after = emit(GRAD_GROUPS[0], g)
    dx1, dg2 = _mm(df, bg["w_ff1"], tb=True, epi="rms_bwd", extra=(s["x1"], dx), gain=sm["norm2_g"], after=after,
                   name="mm_dff1")
    g["norm2_g"] = dg2[0]
    dmg = _mm(dx1, bg["w_out"], tb=True, name="mm_dout")
    g["w_out"] = _mm(s["mg"], dx1, ta=True, out_dtype=BF16, name="mm_gw_out")
    dyc, dya, dugc, duga = _gate_bwd(dmg, s["u"], s["yc"], s["ya"], gate_col, "gate_bwd")
    dzs = _mm(dyc, bg["w_conv_out"], tb=True, name="mm_dconv_out")
    g["w_conv_out"] = _mm(s["zs"], dyc, ta=True, out_dtype=BF16, name="mm_gw_conv_out")
    do = _mm(dya, bg["w_attn_out"], tb=True, name="mm_dattn_out")
    g["w_attn_out"] = _mm(s["o"], dya, ta=True, out_dtype=BF16, name="mm_gw_attn_out")
    dzc, dlg, dlb = _ln_swish_bwd(dzs, s["zc"], sm["conv_ln_g"], sm["conv_ln_b"], "ln_swish_bwd")
    g["conv_ln_g"] = dlg[0]
    g["conv_ln_b"] = dlb[0]
    da, dgt, dcw, dcb = _conv_bwd(dzc, s["u"], bg["conv_dw_w"], cdim, "conv_bwd")
    g["conv_dw_w"] = dcw[:CONV_WIDTH].astype(BF16)
    g["conv_dw_b"] = dcb[0]
    after = emit(GRAD_GROUPS[1], g)
    do_gs, dd_gs = _combine_bwd(do, s["os"], s["lses"], "combine_bwd")
    du = lax.empty(s["u"].shape, BF16)
    for col, piece in ((0, da), (cdim, dgt), (gate_col, dugc), (gate_col + dugc.shape[1], duga)):
        du = lax.dynamic_update_slice(du, piece, (0, col))
    gqs, gks = [], []
    for gi in range(N_GROUPS):
        gq, gk, ds_sum, du = _attn_bwd(s["u"], do_gs[gi], dd_gs[gi], s["lses"][gi], s["qg2"], s["kg2"], bias,
                                       ds_sum, du, gi, cols, hpg, "attn_bwd_g%d" % gi)
        gqs.append(gq)
        gks.append(gk)
    g["q_norm_g"] = jnp.concatenate(gqs)
    g["k_norm_g"] = jnp.concatenate(gks)
    g["w_in"] = _mm_gw_in_pairs(s["h1"], du, after, "mm_gw_in")
    after = emit(GRAD_GROUPS[2], g)
    dx0, dg1 = _mm_din_pairs(du, bg["w_in"], s["x"], sm["norm1_g"], dx1, after, "mm_din")
    g["norm1_g"] = dg1[0]
    return dx0, g, ds_sum


BIG = ("w_in", "conv_dw_w", "w_conv_out", "w_attn_out", "w_out", "w_ff1", "w_ff2")
COL_SHARDED = ("w_in", "conv_dw_w", "w_conv_out", "w_attn_out", "w_ff1")
SMALL = ("rel_bias", "norm1_g", "q_norm_g", "k_norm_g", "conv_dw_b", "conv_ln_g", "conv_ln_b", "norm2_g")
WEIGHTS = ("rel_bias", "norm1_g", "w_in", "q_norm_g", "k_norm_g", "conv_dw_w", "conv_dw_b", "conv_ln_g", "conv_ln_b",
           "w_conv_out", "w_attn_out", "w_out", "norm2_g", "w_ff1", "w_ff2")


def _to_whole(name, gathered):
    n, a, b = gathered.shape
    if name in COL_SHARDED:
        return gathered.transpose(1, 0, 2).reshape(a, n * b)
    return gathered.reshape(n * a, b)


def _to_slots(name, whole):
    a, b = whole.shape
    if name in COL_SHARDED:
        return whole.reshape(a, N_DEV, b // N_DEV).transpose(1, 0, 2)
    return whole.reshape(N_DEV, a // N_DEV, b)


def _own_slot(block, me):
    land = lax.empty((N_DEV,) + block.shape, block.dtype)
    return lax.dynamic_update_slice(land, block[None], (me,) + (0,) * block.ndim)


def kernel(x, rel_bias, norm1_g, w_in, q_norm_g, k_norm_g, conv_dw_w, conv_dw_b, conv_ln_g, conv_ln_b, w_conv_out, w_attn_out, w_out, norm2_g, w_ff1, w_ff2, loss_target, m_rel_bias, m_norm1_g, m_w_in, m_q_norm_g, m_k_norm_g, m_conv_dw_w, m_conv_dw_b, m_conv_ln_g, m_conv_ln_b, m_w_conv_out, m_w_attn_out, m_w_out, m_norm2_g, m_w_ff1, m_w_ff2, v_rel_bias, v_norm1_g, v_w_in, v_q_norm_g, v_k_norm_g, v_conv_dw_w, v_conv_dw_b, v_conv_ln_g, v_conv_ln_b, v_w_conv_out, v_w_attn_out, v_w_out, v_norm2_g, v_w_ff1, v_w_ff2):
    w = dict(rel_bias=rel_bias, norm1_g=norm1_g, w_in=w_in, q_norm_g=q_norm_g, k_norm_g=k_norm_g, conv_dw_w=conv_dw_w,
             conv_dw_b=conv_dw_b, conv_ln_g=conv_ln_g, conv_ln_b=conv_ln_b, w_conv_out=w_conv_out,
             w_attn_out=w_attn_out, w_out=w_out, norm2_g=norm2_g, w_ff1=w_ff1, w_ff2=w_ff2)
    mom = dict(rel_bias=m_rel_bias, norm1_g=m_norm1_g, w_in=m_w_in, q_norm_g=m_q_norm_g, k_norm_g=m_k_norm_g,
               conv_dw_w=m_conv_dw_w, conv_dw_b=m_conv_dw_b, conv_ln_g=m_conv_ln_g, conv_ln_b=m_conv_ln_b,
               w_conv_out=m_w_conv_out, w_attn_out=m_w_attn_out, w_out=m_w_out, norm2_g=m_norm2_g, w_ff1=m_w_ff1,
               w_ff2=m_w_ff2)
    var = dict(rel_bias=v_rel_bias, norm1_g=v_norm1_g, w_in=v_w_in, q_norm_g=v_q_norm_g, k_norm_g=v_k_norm_g,
               conv_dw_w=v_conv_dw_w, conv_dw_b=v_conv_dw_b, conv_ln_g=v_conv_ln_g, conv_ln_b=v_conv_ln_b,
               w_conv_out=v_w_conv_out, w_attn_out=v_w_attn_out, w_out=v_w_out, norm2_g=v_norm2_g, w_ff1=v_w_ff1,
               w_ff2=v_w_ff2)

    depth = norm1_g.shape[0]
    me = 4 * lax.axis_index("x") + 2 * lax.axis_index("y") + lax.axis_index("c")
    odd_core = lax.axis_index("c") == 1
    hpg = w_attn_out.shape[1] // HEAD_DIM
    buckets = jnp.asarray(_bucket_table())
    bias = _bias_expand(rel_bias, buckets, hpg, "bias_expand")

    first_names = ("w_in", "conv_dw_w")
    rest_names = tuple(k for k in BIG if k not in first_names)

    def chain_start(l, names, after):
        shards = [w[k][l] if k == "conv_dw_w" else w[k][l].astype(BF16) for k in names]
        if "w_in" in names:
            i = names.index("w_in")
            shards[i] = jnp.where(odd_core, jnp.pad(shards[i], ((0, 0), (SHIFT, 0))),
                                  jnp.pad(shards[i], ((0, 0), (0, SHIFT))))
        sems, bufs, token = _gather_start(shards, [_own_slot(s, me) for s in shards], after,
                                          "gather_start_%s_l%d" % (names[0], l))
        return dict(l=l, names=names, sems=sems, bufs=bufs, token=token)

    def chain_forward(ch, after):
        fwd, bufs, token = _gather_forward(len(ch["names"]), ch["bufs"], ch["sems"][3], after,
                                           "gather_forward_%s_l%d" % (ch["names"][0], ch["l"]))
        ch.update(fwd=fwd, bufs=bufs)
        return token

    def chain_finish(ch, after):
        d2d_s, d2d_r, ici_s, _ = ch["sems"]
        gathered, _ = _gather_finish(len(ch["names"]), ch["bufs"], d2d_s, d2d_r, ici_s, ch["fwd"][0], ch["fwd"][1],
                                     after, "gather_finish_%s_l%d" % (ch["names"][0], ch["l"]))
        out = {k: a if k == "w_in" else _to_whole(k, a) for k, a in zip(ch["names"], gathered)}
        if "conv_dw_w" in out:
            out["conv_dw_w"] = jnp.pad(out["conv_dw_w"], ((0, CONV_TAPS_PADDED - CONV_WIDTH), (0, 0)))
        return out

    xs = x[0]
    h1 = _rms_fwd(xs, norm1_g[0], "rms1_fwd")
    saved, bigs, smalls = [], [], []
    chains = {}
    for l in range(depth):
        sm = {k: w[k][l] for k in SMALL if k != "rel_bias"}
        if l == 0:
            first = chain_start(0, first_names, None)
            token = chain_forward(first, (h1, bias))
            rest = chain_start(0, rest_names, token)
            bg = chain_finish(first, rest["token"])

            def get_rest(o, rest=rest):
                token = chain_forward(rest, o)
                if depth > 1:
                    chains[1] = (chain_start(1, first_names, token),)
                    chains[1] += (chain_start(1, rest_names, chains[1][0]["token"]),)
                    token = chains[1][1]["token"]
                return chain_finish(rest, token)
        elif l == 1:
            first, rest = chains[1]
            token = chain_forward(first, xs)
            if depth > 2:
                chains[2] = chain_start(2, BIG, token)
                token = chains[2]["token"]
            bg = chain_finish(first, token)

            def get_rest(o, rest=rest):
                return chain_finish(rest, chain_forward(rest, o))
        else:
            token = xs if "fwd" in chains[l] else chain_forward(chains[l], xs)
            if l + 1 < depth:
                chains[l + 1] = chain_start(l + 1, BIG, token)
                token = chains[l + 1]["token"]
            whole = chain_finish(chains[l], token)
            bg = {k: whole[k] for k in first_names}

            def get_rest(o, whole=whole):
                return {k: whole[k] for k in rest_names}
        at_ff1 = (lambda f, nxt=l + 1: chain_forward(chains[nxt], f)) if 2 <= l < depth - 1 else None
        xs, h1, sv, bg = _layer_fwd(xs, h1, sm, bg, get_rest, bias, hpg, norm1_g[l + 1] if l + 1 < depth else None,
                                    at_ff1)
        saved.append(sv)
        bigs.append(bg)
        smalls.append(sm)

    loss, dx = _loss_and_grad(xs, loss_target[0], "loss")

    ds_sum = jnp.zeros((N_GROUPS * hpg, BLOCK, 2 * BLOCK), F32)
    g = {k: [None] * depth for k in SMALL if k != "rel_bias"}
    sums = {k: lax.empty((depth, int(np.prod(w[k].shape[1:-1])), w[k].shape[-1] + (SHIFT if k == "w_in" else 0)), F32)
            for k in BIG}
    pending = []

    def finish_oldest(after):
        names, l, (send, recv), bufs = pending.pop(0)
        parts, recvd, token = _exchange_finish(len(names), bufs, send, recv, after,
                                               "exchange_finish_%s_l%d" % (names[0], l))
        for k, r, p in zip(names, recvd, parts):
            three = (N_DEV, -1, r.shape[-1])
            sums[k] = _sum_slots(r.reshape(three), p.reshape(three), me, sums[k], l, "sum_" + k)
        return token

    def make_emit(l):
        def emit(names, gl):
            parts = [gl[k] if k in ("w_ff1", "w_in") else _to_slots(k, gl[k]) for k in names]
            token = finish_oldest(parts[0]) if len(pending) >= len(GRAD_GROUPS) else None
            lands = [lax.empty(p.shape, p.dtype) for p in parts]
            sems, bufs, token = _exchange_start(parts, lands, token, "exchange_start_%s_l%d" % (names[0], l))
            pending.append((names, l, sems, bufs))
            return token
        return emit

    token = None
    for l in reversed(range(depth)):
        dx, gl, ds_sum = _layer_bwd(dx, saved[l], smalls[l], bigs[l], bias, ds_sum, token, make_emit(l))
        for k in g:
            g[k][l] = gl[k]
        token = None
    grad_x = dx

    g = {k: jnp.stack(v) for k, v in g.items()}
    for k in ("q_norm_g", "k_norm_g"):
        g[k] = jnp.sum(g[k].reshape(depth, -1, HEAD_DIM), axis=1)
    db = _bias_reduce(ds_sum, buckets, hpg, "bias_reduce")
    g["rel_bias"] = db[:, 0, :NUM_BUCKETS].T

    flat = jnp.concatenate([g[k].reshape(-1) for k in SMALL])
    nflat = flat.shape[0]
    rows = -(-nflat // (8 * LANES)) * 8
    packed = jnp.pad(flat, (0, rows * LANES - nflat)).reshape(rows, LANES)
    grad, outs = {}, {}
    token = dx
    while pending:
        names = pending[0][0]
        finish_oldest(token)
        for k in names:
            total = sums[k]
            if k == "w_in":
                total = jnp.where(odd_core, total[..., SHIFT:], total[..., :w_in.shape[-1]])
            grad[k] = total.reshape(w[k].shape)
            outs[k] = _adamw(w[k], grad[k], mom[k], var[k], "adamw_" + k)
            token = outs[k][0]
    total = _all_reduce_small(packed, token, "reduce_small").reshape(-1)
    off = 0
    for k in SMALL:
        size = int(np.prod(w[k].shape))
        grad[k] = total[off:off + size].reshape(w[k].shape)
        outs[k] = _adamw(w[k], grad[k], mom[k], var[k], "adamw_" + k)
        off += size
    loss = lax.psum(loss[0, 0], ("x", "y", "c"))
    return (loss, grad_x[None], *[grad[k] for k in WEIGHTS], *[outs[k][0] for k in WEIGHTS],
            *[outs[k][1] for k in WEIGHTS], *[outs[k][2] for k in WEIGHTS])
```

```python
import functools
import math

import numpy as np
import jax
import jax.numpy as jnp
from jax import lax
from jax.experimental import pallas as pl
from jax.experimental.pallas import tpu as pltpu

F32 = jnp.float32
BF16 = jnp.bfloat16

HEAD_DIM = 64
N_GROUPS = 3
DILATIONS = (1, 4, 16)
SUB_WINDOW = 128
BLOCK = 128
CONV_WIDTH = 31
CONV_TAPS_PADDED = 32
NUM_BUCKETS = 32
MAX_REL_DISTANCE = 2048
EPS = 1e-6
NEG_INF = -1e30
LANES = 128
SUBLANES = 8

ADAM_LR = 0.001
ADAM_B1 = 0.9
ADAM_B2 = 0.999
ADAM_EPS = 1e-08
ADAM_WD = 0.01
ADAM_STEP = 10

N_DEV = 8
VMEM_LIMIT = 56 * 1024 * 1024
MESH = pl.DeviceIdType.MESH


def _cparams(sem=None):
    return pltpu.CompilerParams(dimension_semantics=sem, vmem_limit_bytes=VMEM_LIMIT)


def _tile(n, target):
    if n <= target:
        return n
    t = (target // LANES) * LANES
    while t >= LANES:
        if n % t == 0:
            return t
        t -= LANES
    return n


def _sigmoid(v):
    return 1.0 / (1.0 + jnp.exp(-v))


MM_VMEM_BUDGET = 40 * 1024 * 1024


def _rms_apply(x, g):
    return x * lax.rsqrt(jnp.mean(x * x, axis=-1, keepdims=True) + EPS) * g


def _rms_grad(dh, x, g):
    r = lax.rsqrt(jnp.mean(x * x, axis=-1, keepdims=True) + EPS)
    xh = x * r
    dxh = dh * g
    dx = r * (dxh - xh * jnp.mean(dxh * xh, axis=-1, keepdims=True))
    return dx, jnp.sum(dh * xh, axis=0, keepdims=True)


def _mm_tiles(m, n, kdim, a_bytes, b_bytes, io_bytes, whole_rows=False, temps=2):
    def need(tm, tn, tk):
        blocks = 2 * (tm * tk * a_bytes + tk * tn * b_bytes + tm * tn * io_bytes)
        casts = (tm * tk * 2 if a_bytes == 4 else 0) + (tk * tn * 2 if b_bytes == 4 else 0)
        return blocks + casts + temps * tm * tn * 4

    tn = n if whole_rows else _tile(n, 1024)
    while True:
        fits = [(tm * tk, tm, tk) for tm in {_tile(m, c) for c in (1024, 512, 256, 128)}
                for tk in {_tile(kdim, c) for c in (2048, 1024, 512, 256)} if need(tm, tn, tk) <= MM_VMEM_BUDGET]
        if fits:
            _, tm, tk = max(fits)
            return tm, tn, tk
        assert not whole_rows and tn % 256 == 0, "no block size fits the VMEM budget"
        tn //= 2


def _mm(a, b, *, ta=False, tb=False, out_dtype=F32, epi=None, extra=(), gain=None, after=None, out_slots=False,
        a_relu2=False, name):
    extra = tuple(extra) if isinstance(extra, (tuple, list)) else (extra,)
    m = a.shape[1] if ta else a.shape[0]
    kdim = a.shape[0] if ta else a.shape[1]
    n = b.shape[0] if tb else b.shape[1]
    norm = epi in ("res_rms", "rms_bwd")
    io_bytes = (jnp.dtype(out_dtype).itemsize + sum(e.dtype.itemsize for e in extra) + (2 if epi == "res_rms" else 0))
    tm, tn, tk = _mm_tiles(m, n // N_DEV if out_slots else n, kdim, a.dtype.itemsize, b.dtype.itemsize, io_bytes,
                           whole_rows=norm, temps=6 if norm else 2)
    if out_slots:
        assert epi is None and tn == n // N_DEV
    nk = kdim // tk
    a_spec = pl.BlockSpec((tk, tm), lambda i, j, k: (k, i)) if ta else pl.BlockSpec((tm, tk), lambda i, j, k: (i, k))
    b_spec = pl.BlockSpec((tn, tk), lambda i, j, k: (j, k)) if tb else pl.BlockSpec((tk, tn), lambda i, j, k: (k, j))
    o_spec = (pl.BlockSpec((None, tm, tn), lambda i, j, k: (j, i, 0)) if out_slots
              else pl.BlockSpec((tm, tn), lambda i, j, k: (i, j)))
    v_spec = pl.BlockSpec((1, tn), lambda i, j, k: (0, j))
    dims = (((0 if ta else 1,), (1 if tb else 0,)), ((), ()))
    n_extra = len(extra)
    n_in = 2 + n_extra + (gain is not None) + (after is not None)
    n_out = 2 if norm else 1

    def body(*refs):
        a_ref, b_ref = refs[0], refs[1]
        e_refs = refs[2:2 + n_extra]
        g_ref = refs[2 + n_extra] if gain is not None else None
        outs = refs[n_in:n_in + n_out]

        def product():
            av = a_ref[...]
            if a_relu2:
                r = jnp.maximum(av.astype(F32), 0.0)
                av = r * r
            return lax.dot_general(av.astype(BF16), b_ref[...].astype(BF16), dims, preferred_element_type=F32)

        def finish(acc):
            if epi is None:
                outs[0][...] = acc.astype(outs[0].dtype)
            elif epi == "res":
                outs[0][...] = (e_refs[0][...] + acc).astype(outs[0].dtype)
            elif epi == "drelu2":
                outs[0][...] = (acc * (2.0 * jnp.maximum(e_refs[0][...].astype(F32), 0.0))).astype(outs[0].dtype)
            elif epi == "res_rms":
                x1 = e_refs[0][...] + acc
                outs[0][...] = x1
                outs[1][...] = _rms_apply(x1, g_ref[...]).astype(BF16)
            elif epi == "rms_bwd":
                dx, dg = _rms_grad(acc, e_refs[0][...], g_ref[...])
                outs[0][...] = e_refs[1][...] + dx
                i = pl.program_id(0)

                @pl.when(i == 0)
                def _():
                    outs[1][...] = dg

                @pl.when(i > 0)
                def _():
                    outs[1][...] += dg

        if nk == 1:
            finish(product())
            return
        acc_ref = refs[-1]
        k = pl.program_id(2)

        @pl.when(k == 0)
        def _():
            acc_ref[...] = product()

        @pl.when(jnp.logical_and(k > 0, k < nk - 1))
        def _():
            acc_ref[...] += product()

        @pl.when(k == nk - 1)
        def _():
            finish(acc_ref[...] + product())

    in_specs = ([a_spec, b_spec] + [o_spec] * n_extra + ([v_spec] if gain is not None else [])
                + ([pl.BlockSpec(memory_space=pl.ANY)] if after is not None else []))
    if epi == "res_rms":
        out_shape = (jax.ShapeDtypeStruct((m, n), F32), jax.ShapeDtypeStruct((m, n), BF16))
        out_specs = (o_spec, o_spec)
    elif epi == "rms_bwd":
        out_shape = (jax.ShapeDtypeStruct((m, n), F32), jax.ShapeDtypeStruct((1, n), F32))
        out_specs = (o_spec, v_spec)
    else:
        out_shape = jax.ShapeDtypeStruct((N_DEV, m, tn) if out_slots else (m, n), out_dtype)
        out_specs = o_spec
    args = (a, b) + extra + ((gain.reshape(1, n),) if gain is not None else ()) + ((after,) if after is not None else ())
    return pl.pallas_call(
        body, name=name, grid=(m // tm, n // tn, nk), in_specs=in_specs, out_specs=out_specs, out_shape=out_shape,
        scratch_shapes=[pltpu.VMEM((tm, tn), F32)] if nk > 1 else [],
        compiler_params=_cparams(("arbitrary", "arbitrary", "arbitrary")),
    )(*args)


SHIFT = HEAD_DIM


def _pair_blocks(e, o):
    wp = e.shape[-1]
    return e[:, :wp - LANES], e[:, wp - LANES:] + o[:, :LANES], o[:, LANES:]


def _mm_in_pairs(a, wg, name):
    t, kdim = a.shape
    wp = wg.shape[-1]
    ws = wp - SHIFT
    tm = _tile(t, 1024)

    def body(a_ref, e_ref, o_ref, u_ref):
        av = a_ref[...]
        lo, mid, hi = _pair_blocks(e_ref[...], o_ref[...])
        u_ref[:, :wp - LANES] = jnp.dot(av, lo, preferred_element_type=F32).astype(BF16)
        u_ref[:, wp - LANES:wp] = jnp.dot(av, mid, preferred_element_type=F32).astype(BF16)
        u_ref[:, wp:] = jnp.dot(av, hi, preferred_element_type=F32).astype(BF16)

    return pl.pallas_call(
        body, name=name, grid=(N_DEV // 2, t // tm),
        in_specs=[pl.BlockSpec((tm, kdim), lambda p, i: (i, 0)),
                  pl.BlockSpec((None, kdim, wp), lambda p, i: (2 * p, 0, 0)),
                  pl.BlockSpec((None, kdim, wp), lambda p, i: (2 * p + 1, 0, 0))],
        out_specs=pl.BlockSpec((tm, 2 * ws), lambda p, i: (i, p)),
        out_shape=jax.ShapeDtypeStruct((t, N_DEV * ws), BF16), compiler_params=_cparams(("arbitrary", "arbitrary")),
    )(a, wg, wg)


def _mm_din_pairs(du, wg, x, gain, dres, after, name):
    t = du.shape[0]
    _, kdim, wp = wg.shape
    ws = wp - SHIFT
    tm = _tile(t, 512)
    npair = N_DEV // 2
    lanes = (((1,), (1,)), ((), ()))
    extra = [] if after is None else [after]

    def body(d_ref, e_ref, o_ref, x_ref, g_ref, r_ref, *rest):
        dx_ref, dg_ref, acc_ref = rest[-3], rest[-2], rest[-1]
        i, p = pl.program_id(0), pl.program_id(1)
        lo, mid, hi = _pair_blocks(e_ref[...], o_ref[...])
        part = (lax.dot_general(d_ref[:, :wp - LANES], lo, lanes, preferred_element_type=F32)
                + lax.dot_general(d_ref[:, wp - LANES:wp], mid, lanes, preferred_element_type=F32)
                + lax.dot_general(d_ref[:, wp:], hi, lanes, preferred_element_type=F32))

        @pl.when(p == 0)
        def _():
            acc_ref[...] = part

        @pl.when(jnp.logical_and(p > 0, p < npair - 1))
        def _():
            acc_ref[...] += part

        @pl.when(p == npair - 1)
        def _():
            dx, dg = _rms_grad(acc_ref[...] + part, x_ref[...], g_ref[...])
            dx_ref[...] = r_ref[...] + dx

            @pl.when(i == 0)
            def _():
                dg_ref[...] = dg

            @pl.when(i > 0)
            def _():
                dg_ref[...] += dg

    row = pl.BlockSpec((tm, kdim), lambda i, p: (i, 0))
    vec = pl.BlockSpec((1, kdim), lambda i, p: (0, 0))
    return pl.pallas_call(
        body, name=name, grid=(t // tm, npair),
        in_specs=[pl.BlockSpec((tm, 2 * ws), lambda i, p: (i, p)),
                  pl.BlockSpec((None, kdim, wp), lambda i, p: (2 * p, 0, 0)),
                  pl.BlockSpec((None, kdim, wp), lambda i, p: (2 * p + 1, 0, 0)), row, vec, row]
        + [pl.BlockSpec(memory_space=pl.ANY)] * len(extra),
        out_specs=(row, vec),
        out_shape=(jax.ShapeDtypeStruct((t, kdim), F32), jax.ShapeDtypeStruct((1, kdim), F32)),
        scratch_shapes=[pltpu.VMEM((tm, kdim), F32)], compiler_params=_cparams(("arbitrary", "arbitrary")),
    )(du, wg, wg, x, gain.reshape(1, kdim), dres, *extra)


def _mm_gw_in_pairs(h, du, after, name):
    t, kdim = h.shape
    ws = du.shape[1] // N_DEV
    wp = ws + SHIFT
    tm = _tile(kdim, 512)
    rows = (((0,), (0,)), ((), ()))
    extra = [] if after is None else [after]

    def body(h_ref, d_ref, *rest):
        g_ref = rest[-1]
        g = lax.dot_general(h_ref[...], d_ref[...], rows, preferred_element_type=F32)
        g_ref[0] = g[:, :wp].astype(BF16)
        g_ref[1] = g[:, wp - LANES:].astype(BF16)

    return pl.pallas_call(
        body, name=name, grid=(N_DEV // 2, kdim // tm),
        in_specs=[pl.BlockSpec((t, tm), lambda p, i: (0, i)), pl.BlockSpec((t, 2 * ws), lambda p, i: (0, p))]
        + [pl.BlockSpec(memory_space=pl.ANY)] * len(extra),
        out_specs=pl.BlockSpec((2, tm, wp), lambda p, i: (p, i, 0)),
        out_shape=jax.ShapeDtypeStruct((N_DEV, kdim, wp), BF16), compiler_params=_cparams(("arbitrary", "arbitrary")),
    )(h, du, *extra)


ROW_BLOCK_BUDGET = 24 * 1024 * 1024


def _rows(t, row_bytes):
    rows = t
    while rows > 8 and (2 * rows * row_bytes > ROW_BLOCK_BUDGET or t % rows):
        rows //= 2
    return rows


def _rms_fwd(x, g, after, name):
    t, d = x.shape
    ROWS = _rows(t, 6 * d)

    def body(x_ref, g_ref, _, h_ref):
        h_ref[...] = _rms_apply(x_ref[...], g_ref[...]).astype(BF16)

    return pl.pallas_call(
        body, name=name, grid=(t // ROWS,),
        in_specs=[pl.BlockSpec((ROWS, d), lambda i: (i, 0)), pl.BlockSpec((1, d), lambda i: (0, 0)),
                  pl.BlockSpec(memory_space=pl.ANY)],
        out_specs=pl.BlockSpec((ROWS, d), lambda i: (i, 0)),
        out_shape=jax.ShapeDtypeStruct((t, d), BF16), compiler_params=_cparams(("arbitrary",)),
    )(x, g.reshape(1, d), after)


def _gated_out(zs, o, wc, wa, u, gate_col, name):
    t = zs.shape[0]
    d = wc.shape[1]
    td = math.gcd(_tile(d, 512), gate_col)
    nd = d // td
    c0 = gate_col // td
    tm = _tile(t, 1024)

    def body(zs_ref, o_ref, wc_ref, wa_ref, gc_ref, ga_ref, m_ref, yc_ref, ya_ref):
        yc = jnp.dot(zs_ref[...], wc_ref[...], preferred_element_type=F32)
        ya = jnp.dot(o_ref[...], wa_ref[...], preferred_element_type=F32)
        gc = _sigmoid(gc_ref[...].astype(F32))
        ga = _sigmoid(ga_ref[...].astype(F32))
        m_ref[...] = (gc * yc + ga * ya).astype(BF16)
        yc_ref[...] = yc.astype(BF16)
        ya_ref[...] = ya.astype(BF16)

    blk = pl.BlockSpec((tm, td), lambda i, j: (i, j))
    sh = jax.ShapeDtypeStruct((t, d), BF16)
    return pl.pallas_call(
        body, name=name, grid=(t // tm, nd),
        in_specs=[pl.BlockSpec((tm, zs.shape[1]), lambda i, j: (i, 0)), pl.BlockSpec((tm, o.shape[1]), lambda i, j: (i, 0)),
                  pl.BlockSpec((wc.shape[0], td), lambda i, j: (0, j)), pl.BlockSpec((wa.shape[0], td), lambda i, j: (0, j)),
                  pl.BlockSpec((tm, td), lambda i, j: (i, c0 + j)), pl.BlockSpec((tm, td), lambda i, j: (i, c0 + nd + j))],
        out_specs=(blk, blk, blk), out_shape=(sh, sh, sh), compiler_params=_cparams(("arbitrary", "arbitrary")),
    )(zs, o, wc, wa, u, u)


def _gate_bwd(dm, u, yc, ya, gate_col, name):
    t, d = yc.shape
    td = math.gcd(_tile(d, 512), gate_col)
    nd = d // td
    c0 = gate_col // td
    ROWS = _rows(t, 20 * td)

    def body(dm_ref, gc_ref, ga_ref, yc_ref, ya_ref, dyc_ref, dya_ref, dugc_ref, duga_ref):
        dmv = dm_ref[...]
        gc = _sigmoid(gc_ref[...].astype(F32))
        ga = _sigmoid(ga_ref[...].astype(F32))
        dyc_ref[...] = (dmv * gc).astype(BF16)
        dya_ref[...] = (dmv * ga).astype(BF16)
        dugc_ref[...] = (dmv * yc_ref[...].astype(F32) * gc * (1.0 - gc)).astype(BF16)
        duga_ref[...] = (dmv * ya_ref[...].astype(F32) * ga * (1.0 - ga)).astype(BF16)

    blk = pl.BlockSpec((ROWS, td), lambda i, j: (i, j))
    o = jax.ShapeDtypeStruct((t, d), BF16)
    return pl.pallas_call(
        body, name=name, grid=(t // ROWS, nd),
        in_specs=[blk, pl.BlockSpec((ROWS, td), lambda i, j: (i, c0 + j)),
                  pl.BlockSpec((ROWS, td), lambda i, j: (i, c0 + nd + j)), blk, blk],
        out_specs=(blk, blk, blk, blk), out_shape=(o, o, o, o),
        compiler_params=_cparams(("arbitrary", "arbitrary")),
    )(dm, u, u, yc, ya)


def _loss_and_grad(y, target, name):
    t, d = y.shape
    ROWS = _rows(t, 12 * d)
    n = t // ROWS

    def body(y_ref, t_ref, loss_ref, dy_ref, acc_ref):
        i = pl.program_id(0)

        @pl.when(i == 0)
        def _():
            acc_ref[...] = jnp.zeros_like(acc_ref)

        diff = y_ref[...] - t_ref[...]
        dy_ref[...] = diff * (1.0 / d)
        acc_ref[...] += jnp.sum(diff * diff, axis=0, keepdims=True)

        @pl.when(i == n - 1)
        def _():
            loss_ref[...] = jnp.sum(acc_ref[...], axis=-1, keepdims=True) * (0.5 / d)

    row = pl.BlockSpec((ROWS, d), lambda i: (i, 0))
    return pl.pallas_call(
        body, name=name, grid=(n,), in_specs=[row, row],
        out_specs=(pl.BlockSpec((1, 1), lambda i: (0, 0)), row),
        out_shape=(jax.ShapeDtypeStruct((1, 1), F32), jax.ShapeDtypeStruct((t, d), F32)),
        scratch_shapes=[pltpu.VMEM((1, d), F32)], compiler_params=_cparams(("arbitrary",)),
    )(y, target)


HALO = 32


def _conv_fwd(u, w, b, cdim, name):
    t = u.shape[0]
    ncb = cdim // LANES
    nt = t // BLOCK

    def body(a_ref, g_ref, w_ref, b_ref, zc_ref, zpad):
        zpad[0:HALO, :] = jnp.zeros((HALO, LANES), F32)
        zpad[HALO:HALO + t, :] = a_ref[...].astype(F32) * _sigmoid(g_ref[...].astype(F32))
        wv = w_ref[...]
        bv = b_ref[...]

        def tile(i, carry):
            r0 = pl.multiple_of(i * BLOCK, BLOCK)
            win = zpad[pl.ds(r0, BLOCK + HALO), :]
            acc = jnp.zeros((BLOCK, LANES), F32) + bv
            for b in range(SUBLANES):
                sh = win if b == 0 else pltpu.roll(win, b, 0)
                for a in range(HALO // SUBLANES):
                    j = CONV_WIDTH - 1 - (SUBLANES * a + b)
                    if j >= 0:
                        lo = HALO - SUBLANES * a
                        acc = acc + wv[j:j + 1, :] * sh[lo:lo + BLOCK, :]
            zc_ref[pl.ds(r0, BLOCK), :] = acc
            return carry

        lax.fori_loop(0, nt, tile, 0)

    col = lambda off: pl.BlockSpec((t, LANES), lambda c: (0, off + c))
    return pl.pallas_call(
        body, name=name, grid=(ncb,),
        in_specs=[col(0), col(ncb), pl.BlockSpec((CONV_TAPS_PADDED, LANES), lambda c: (0, c)),
                  pl.BlockSpec((1, LANES), lambda c: (0, c))],
        out_specs=pl.BlockSpec((t, LANES), lambda c: (0, c)),
        out_shape=jax.ShapeDtypeStruct((t, cdim), F32),
        scratch_shapes=[pltpu.VMEM((t + HALO, LANES), F32)], compiler_params=_cparams(("arbitrary",)),
    )(u, u, w, b.reshape(1, cdim))


def _conv_bwd(dzc, u, w, cdim, name):
    t = u.shape[0]
    ncb = cdim // LANES
    nt = t // BLOCK
    win_rows = BLOCK + HALO

    def body(dzc_ref, a_ref, g_ref, w_ref, da_ref, dg_ref, dw_ref, db_ref, zpad, dpad):
        av = a_ref[...].astype(F32)
        sg = _sigmoid(g_ref[...].astype(F32))
        zpad[0:HALO, :] = jnp.zeros((HALO, LANES), F32)
        zpad[HALO:HALO + t, :] = av * sg
        dpad[0:t, :] = dzc_ref[...]
        dpad[t:t + HALO, :] = jnp.zeros((HALO, LANES), F32)
        dw_ref[...] = jnp.zeros_like(dw_ref)
        db_ref[...] = jnp.sum(dzc_ref[...], axis=0, keepdims=True)
        wv = w_ref[...]

        def tile(i, carry):
            r0 = pl.multiple_of(i * BLOCK, BLOCK)
            zwin = zpad[pl.ds(r0, win_rows), :]
            dwin = dpad[pl.ds(r0, win_rows), :]
            dcur = dwin[0:BLOCK, :]
            dz = jnp.zeros((BLOCK, LANES), F32)
            for b in range(SUBLANES):
                zs = zwin if b == 0 else pltpu.roll(zwin, b, 0)
                ds = dwin if b == 0 else pltpu.roll(dwin, win_rows - b, 0)
                for a in range(HALO // SUBLANES):
                    j = CONV_WIDTH - 1 - (SUBLANES * a + b)
                    if j >= 0:
                        lo = HALO - SUBLANES * a
                        dw_ref[j:j + 1, :] += jnp.sum(dcur * zs[lo:lo + BLOCK, :], axis=0, keepdims=True)
                        dz = dz + wv[j:j + 1, :] * ds[SUBLANES * a:SUBLANES * a + BLOCK, :]
            ac = a_ref[pl.ds(r0, BLOCK), :].astype(F32)
            sc = _sigmoid(g_ref[pl.ds(r0, BLOCK), :].astype(F32))
            da_ref[pl.ds(r0, BLOCK), :] = (dz * sc).astype(BF16)
            dg_ref[pl.ds(r0, BLOCK), :] = (dz * ac * sc * (1.0 - sc)).astype(BF16)
            return carry

        lax.fori_loop(0, nt, tile, 0)

    col = lambda off: pl.BlockSpec((t, LANES), lambda c: (0, off + c))
    wspec = pl.BlockSpec((CONV_TAPS_PADDED, LANES), lambda c: (0, c))
    o = jax.ShapeDtypeStruct((t, cdim), BF16)
    return pl.pallas_call(
        body, name=name, grid=(ncb,), in_specs=[col(0), col(0), col(ncb), wspec],
        out_specs=(col(0), col(0), wspec, pl.BlockSpec((1, LANES), lambda c: (0, c))),
        out_shape=(o, o, jax.ShapeDtypeStruct((CONV_TAPS_PADDED, cdim), F32), jax.ShapeDtypeStruct((1, cdim), F32)),
        scratch_shapes=[pltpu.VMEM((t + HALO, LANES), F32), pltpu.VMEM((t + HALO, LANES), F32)],
        compiler_params=_cparams(("arbitrary",)),
    )(dzc, u, u, w)


def _ln_swish_fwd(zc, g, b, name):
    t, c = zc.shape
    ROWS = _rows(t, 6 * c)

    def body(z_ref, g_ref, b_ref, o_ref):
        z = z_ref[...]
        mu = jnp.mean(z, axis=-1, keepdims=True)
        zc_ = z - mu
        zn = zc_ * lax.rsqrt(jnp.mean(zc_ * zc_, axis=-1, keepdims=True) + EPS)
        y = zn * g_ref[...] + b_ref[...]
        o_ref[...] = (y * _sigmoid(y)).astype(BF16)

    row = pl.BlockSpec((ROWS, c), lambda i: (i, 0))
    vec = pl.BlockSpec((1, c), lambda i: (0, 0))
    return pl.pallas_call(
        body, name=name, grid=(t // ROWS,), in_specs=[row, vec, vec], out_specs=row,
        out_shape=jax.ShapeDtypeStruct((t, c), BF16), compiler_params=_cparams(("arbitrary",)),
    )(zc, g.reshape(1, c), b.reshape(1, c))


def _ln_swish_bwd(dzs, zc, g, b, name):
    t, c = zc.shape
    ROWS = _rows(t, 12 * c)

    def body(d_ref, z_ref, g_ref, b_ref, dz_ref, dg_ref, db_ref):
        @pl.when(pl.program_id(0) == 0)
        def _():
            dg_ref[...] = jnp.zeros_like(dg_ref)
            db_ref[...] = jnp.zeros_like(db_ref)

        z = z_ref[...]
        mu = jnp.mean(z, axis=-1, keepdims=True)
        zc_ = z - mu
        rstd = lax.rsqrt(jnp.mean(zc_ * zc_, axis=-1, keepdims=True) + EPS)
        zn = zc_ * rstd
        y = zn * g_ref[...] + b_ref[...]
        sg = _sigmoid(y)
        dy = d_ref[...] * (sg * (1.0 + y * (1.0 - sg)))
        dg_ref[...] += jnp.sum(dy * zn, axis=0, keepdims=True)
        db_ref[...] += jnp.sum(dy, axis=0, keepdims=True)
        dzn = dy * g_ref[...]
        dz_ref[...] = rstd * (dzn - jnp.mean(dzn, axis=-1, keepdims=True)
                              - zn * jnp.mean(dzn * zn, axis=-1, keepdims=True))

    row = pl.BlockSpec((ROWS, c), lambda i: (i, 0))
    vec = pl.BlockSpec((1, c), lambda i: (0, 0))
    v = jax.ShapeDtypeStruct((1, c), F32)
    return pl.pallas_call(
        body, name=name, grid=(t // ROWS,), in_specs=[row, row, vec, vec], out_specs=(row, vec, vec),
        out_shape=(jax.ShapeDtypeStruct((t, c), F32), v, v), compiler_params=_cparams(("arbitrary",)),
    )(dzs, zc, g.reshape(1, c), b.reshape(1, c))


def _bucket_table():
    qi = np.arange(BLOCK)[:, None]
    kj = np.arange(2 * BLOCK)[None, :]
    off = qi + BLOCK - kj
    band = (off >= 0) & (off <= SUB_WINDOW)
    max_exact = NUM_BUCKETS // 2
    out = []
    for d in DILATIONS:
        dist = (np.clip(off, 0, SUB_WINDOW) * d).astype(np.int32)
        nf = np.maximum(dist, 1).astype(np.float32)
        large = max_exact + (np.log(nf / np.float32(max_exact)) / np.float32(math.log(MAX_REL_DISTANCE / max_exact))
                             * np.float32(NUM_BUCKETS - max_exact)).astype(np.int32)
        large = np.minimum(large, NUM_BUCKETS - 1)
        bucket = np.where(dist < max_exact, dist, large)
        out.append(np.where(band, bucket, -1))
    return np.stack(out).astype(np.int32)


def _bias_expand(rel_bias, buckets, hpg, after, name):
    nh = N_GROUPS * hpg

    def body(rb_ref, bk_ref, _, o_ref):
        h = pl.program_id(0)
        bk = bk_ref[0]
        acc = jnp.full((BLOCK, 2 * BLOCK), NEG_INF, F32)
        for bb in range(NUM_BUCKETS):
            acc = jnp.where(bk == bb, rb_ref[bb, h], acc)
        o_ref[0] = acc

    return pl.pallas_call(
        body, name=name, grid=(nh,),
        in_specs=[pl.BlockSpec(memory_space=pltpu.SMEM),
                  pl.BlockSpec((1, BLOCK, 2 * BLOCK), lambda h: (h // hpg, 0, 0)), pl.BlockSpec(memory_space=pl.ANY)],
        out_specs=pl.BlockSpec((1, BLOCK, 2 * BLOCK), lambda h: (h, 0, 0)),
        out_shape=jax.ShapeDtypeStruct((nh, BLOCK, 2 * BLOCK), F32), compiler_params=_cparams(("arbitrary",)),
    )(rel_bias, buckets, after)


def _bias_reduce(ds_sum, buckets, hpg, name):
    nh = N_GROUPS * hpg

    def body(ds_ref, bk_ref, o_ref):
        bk = bk_ref[0]
        dsv = ds_ref[0]
        lane = lax.broadcasted_iota(jnp.int32, (1, LANES), 1)
        row = jnp.zeros((1, LANES), F32)
        for bb in range(NUM_BUCKETS):
            tot = jnp.sum(jnp.sum(jnp.where(bk == bb, dsv, 0.0), axis=-1, keepdims=True), axis=0, keepdims=True)
            row = jnp.where(lane == bb, tot, row)
        o_ref[0] = row

    return pl.pallas_call(
        body, name=name, grid=(nh,),
        in_specs=[pl.BlockSpec((1, BLOCK, 2 * BLOCK), lambda h: (h, 0, 0)),
                  pl.BlockSpec((1, BLOCK, 2 * BLOCK), lambda h: (h // hpg, 0, 0))],
        out_specs=pl.BlockSpec((1, 1, LANES), lambda h: (h, 0, 0)),
        out_shape=jax.ShapeDtypeStruct((nh, 1, LANES), F32), compiler_params=_cparams(("arbitrary",)),
    )(ds_sum, buckets)


def _chunk_rows(c, d, nb):
    r, n = c // nb, c % nb
    if d == 1:
        return pl.ds(c * BLOCK, BLOCK)
    return pl.ds(r + n * BLOCK * d, BLOCK, stride=d)


def _segment_ones():
    i = lax.broadcasted_iota(jnp.int32, (LANES, LANES), 0) // HEAD_DIM
    j = lax.broadcasted_iota(jnp.int32, (LANES, LANES), 1) // HEAD_DIM
    return (i == j).astype(BF16)


def _segment_sum(v, seg):
    hi = v.astype(BF16)
    lo = (v - hi.astype(F32)).astype(BF16)
    return jnp.dot(hi, seg, preferred_element_type=F32) + jnp.dot(lo, seg, preferred_element_type=F32)


def _head_mean(v, seg):
    return _segment_sum(v, seg) * (1.0 / HEAD_DIM)


def _attn_fwd(u, qg2, kg2, bias, gi, cols, hpg, name):
    t = u.shape[0]
    d = DILATIONS[gi]
    nchunk = t // BLOCK
    nb = (t // d) // BLOCK
    hp = hpg // 2
    qc0, kc0, vc0 = [(c + gi * hpg * HEAD_DIM) // LANES for c in cols]
    contract_lanes = (((1,), (1,)), ((), ()))

    def body(q_ref, k_ref, v_ref, qg_ref, kg_ref, bias_ref, o_ref, lse_ref, qd, kd, vd, od, ld, sbuf):
        seg = _segment_ones()
        lane = lax.broadcasted_iota(jnp.int32, (1, LANES), 1)
        qg = qg_ref[...] * (HEAD_DIM ** -0.5)
        kg = kg_ref[...]
        kd[0:BLOCK, :] = jnp.zeros((BLOCK, LANES), BF16)
        vd[0:BLOCK, :] = jnp.zeros((BLOCK, LANES), BF16)
        od[...] = q_ref[...].astype(F32)
        ld[...] = k_ref[...].astype(F32)
        for c in range(nchunk):
            rows = _chunk_rows(c, d, nb)
            qv = od[rows, :]
            kv = ld[rows, :]
            qd[c * BLOCK:(c + 1) * BLOCK, :] = (qv * lax.rsqrt(_head_mean(qv * qv, seg) + EPS) * qg).astype(BF16)
            kd[(c + 1) * BLOCK:(c + 2) * BLOCK, :] = (kv * lax.rsqrt(_head_mean(kv * kv, seg) + EPS) * kg).astype(BF16)
        od[...] = v_ref[...].astype(F32)
        for c in range(nchunk):
            vd[(c + 1) * BLOCK:(c + 2) * BLOCK, :] = od[_chunk_rows(c, d, nb), :].astype(BF16)

        col = lax.broadcasted_iota(jnp.int32, (BLOCK, 2 * BLOCK), 1)
        for j in range(2):
            mj = jnp.logical_and(lane >= j * HEAD_DIM, lane < (j + 1) * HEAD_DIM)
            for c in range(nchunk):
                kw = kd[c * BLOCK:(c + 2) * BLOCK, :]
                kj = jnp.where(mj, kw, jnp.zeros_like(kw))
                s = lax.dot_general(qd[c * BLOCK:(c + 1) * BLOCK, :], kj, contract_lanes,
                                    preferred_element_type=F32) + bias_ref[j]
                if c % nb == 0:
                    s = jnp.where(col < BLOCK, NEG_INF, s)
                sbuf[c] = s
            for c in range(nchunk):
                rows = slice(c * BLOCK, (c + 1) * BLOCK)
                s = sbuf[c]
                mx = jnp.max(s, axis=-1, keepdims=True)
                p = jnp.exp(s - mx).astype(BF16)
                vw = vd[c * BLOCK:(c + 2) * BLOCK, :]
                oj = jnp.dot(p, jnp.where(mj, vw, jnp.ones_like(vw)), preferred_element_type=F32)
                l = pltpu.roll(oj, HEAD_DIM, 1)
                on = oj / l
                ls = mx + jnp.log(l)
                if j == 0:
                    od[rows, :] = on
                    ld[rows, :] = ls
                else:
                    od[rows, :] = jnp.where(mj, on, od[rows, :])
                    ld[rows, :] = jnp.where(mj, ls, ld[rows, :])

        for c in range(nchunk):
            rows = _chunk_rows(c, d, nb)
            o_ref[rows, :] = od[c * BLOCK:(c + 1) * BLOCK, :]
            lse_ref[rows, :] = ld[c * BLOCK:(c + 1) * BLOCK, :]

    ucol = lambda c0: pl.BlockSpec((t, LANES), lambda h: (0, c0 + h))
    vec = pl.BlockSpec((1, LANES), lambda h: (0, 0))
    oblk = pl.BlockSpec((t, LANES), lambda h: (0, h))
    osh = jax.ShapeDtypeStruct((t, hpg * HEAD_DIM), F32)
    return pl.pallas_call(
        body, name=name, grid=(hp,),
        in_specs=[ucol(qc0), ucol(kc0), ucol(vc0), vec, vec,
                  pl.BlockSpec((2, BLOCK, 2 * BLOCK), lambda h: (gi * hp + h, 0, 0))],
        out_specs=(oblk, oblk), out_shape=(osh, osh),
        scratch_shapes=[pltpu.VMEM((t, LANES), BF16), pltpu.VMEM((t + BLOCK, LANES), BF16),
                        pltpu.VMEM((t + BLOCK, LANES), BF16), pltpu.VMEM((t, LANES), F32), pltpu.VMEM((t, LANES), F32),
                        pltpu.VMEM((nchunk, BLOCK, 2 * BLOCK), F32)],
        compiler_params=_cparams(("arbitrary",)),
    )(u, u, u, qg2, kg2, bias)


def _attn_bwd(u, do_g, dd_g, lse_g, qg2, kg2, bias, ds_in, du_in, gi, cols, hpg, name):
    t = u.shape[0]
    d = DILATIONS[gi]
    nchunk = t // BLOCK
    nb = (t // d) // BLOCK
    hp = hpg // 2
    qc0, kc0, vc0 = [(c + gi * hpg * HEAD_DIM) // LANES for c in cols]
    contract_lanes = (((1,), (1,)), ((), ()))
    contract_rows = (((0,), (0,)), ((), ()))
    qscale = HEAD_DIM ** -0.5

    def body(q_ref, k_ref, v_ref, do_ref, dd_ref, lse_ref, qg_ref, kg_ref, bias_ref, dsin_ref, _du_in,
             dgq_ref, dgk_ref, dsout_ref, du_ref,
             qd, kd, vd, dod, ddd, ld, dqd, dkd, dvd, dsacc, pbuf, dsbuf, qs, ks, qst, kst, vst, out_sems):
        h = pl.program_id(0)

        def flush(step):
            return [pltpu.make_async_copy(
                st, du_ref.at[:, pl.ds(pl.multiple_of((c0 + step) * LANES, LANES), LANES)], out_sems.at[i])
                for i, (st, c0) in enumerate(((qst, qc0), (kst, kc0), (vst, vc0)))]

        seg = _segment_ones()
        lane = lax.broadcasted_iota(jnp.int32, (1, LANES), 1)
        qg = qg_ref[...] * qscale
        kg = kg_ref[...]
        kd[0:BLOCK, :] = jnp.zeros((BLOCK, LANES), BF16)
        vd[0:BLOCK, :] = jnp.zeros((BLOCK, LANES), BF16)
        dsacc[...] = jnp.zeros_like(dsacc)
        qs[...] = q_ref[...].astype(F32)
        ks[...] = k_ref[...].astype(F32)
        dqd[...] = v_ref[...].astype(F32)
        for c in range(nchunk):
            rows = _chunk_rows(c, d, nb)
            qv = qs[rows, :]
            kv = ks[rows, :]
            qd[c * BLOCK:(c + 1) * BLOCK, :] = (qv * lax.rsqrt(_head_mean(qv * qv, seg) + EPS) * qg).astype(BF16)
            kd[(c + 1) * BLOCK:(c + 2) * BLOCK, :] = (kv * lax.rsqrt(_head_mean(kv * kv, seg) + EPS) * kg).astype(BF16)
            vd[(c + 1) * BLOCK:(c + 2) * BLOCK, :] = dqd[rows, :].astype(BF16)
            dod[c * BLOCK:(c + 1) * BLOCK, :] = do_ref[rows, :].astype(BF16)
            ddd[c * BLOCK:(c + 1) * BLOCK, :] = dd_ref[rows, :]
            ld[c * BLOCK:(c + 1) * BLOCK, :] = lse_ref[rows, :]

        col = lax.broadcasted_iota(jnp.int32, (BLOCK, 2 * BLOCK), 1)
        for j in range(2):
            mj = jnp.logical_and(lane >= j * HEAD_DIM, lane < (j + 1) * HEAD_DIM)
            first = lane == j * HEAD_DIM
            for c in range(nchunk):
                rows = slice(c * BLOCK, (c + 1) * BLOCK)
                kw = kd[c * BLOCK:(c + 2) * BLOCK, :]
                vw = vd[c * BLOCK:(c + 2) * BLOCK, :]
                kj = jnp.where(mj, kw, jnp.zeros_like(kw))
                vj = jnp.where(mj, vw, jnp.zeros_like(vw))
                s = lax.dot_general(qd[rows, :], kj, contract_lanes, preferred_element_type=F32) + bias_ref[j]
                if c % nb == 0:
                    s = jnp.where(col < BLOCK, NEG_INF, s)
                dp = lax.dot_general(dod[rows, :], vj, contract_lanes, preferred_element_type=F32)
                lse_j = jnp.sum(jnp.where(first, ld[rows, :], 0.0), axis=-1, keepdims=True)
                dd_j = jnp.sum(jnp.where(first, ddd[rows, :], 0.0), axis=-1, keepdims=True)
                p = jnp.exp(s - lse_j)
                ds = p * (dp + dd_j)
                dsacc[j] += ds
                pbuf[j, c] = p.astype(BF16)
                dsbuf[j, c] = ds.astype(BF16)
        for c in range(nchunk):
            rows = slice(c * BLOCK, (c + 1) * BLOCK)
            has_next = c + 1 < nchunk and (c + 1) % nb != 0
            kw = kd[c * BLOCK:(c + 2) * BLOCK, :]
            dq = jnp.zeros((BLOCK, LANES), F32)
            dk = jnp.zeros((BLOCK, LANES), F32)
            dv = jnp.zeros((BLOCK, LANES), F32)
            both = slice(c * BLOCK, (c + 2) * BLOCK) if has_next else rows
            for j in range(2):
                mj = jnp.logical_and(lane >= j * HEAD_DIM, lane < (j + 1) * HEAD_DIM)
                dq = dq + jnp.dot(dsbuf[j, c], jnp.where(mj, kw, jnp.zeros_like(kw)), preferred_element_type=F32)
                dsk = dsbuf[j, c, :, BLOCK:]
                pk = pbuf[j, c, :, BLOCK:]
                if has_next:
                    dsk = jnp.concatenate([dsk, dsbuf[j, c + 1, :, :BLOCK]], axis=0)
                    pk = jnp.concatenate([pk, pbuf[j, c + 1, :, :BLOCK]], axis=0)
                qq = qd[both, :]
                dd = dod[both, :]
                dk = dk + lax.dot_general(dsk, jnp.where(mj, qq, jnp.zeros_like(qq)), contract_rows,
                                          preferred_element_type=F32)
                dv = dv + lax.dot_general(pk, jnp.where(mj, dd, jnp.zeros_like(dd)), contract_rows,
                                          preferred_element_type=F32)
            dqd[rows, :] = dq
            dkd[rows, :] = dk
            dvd[rows, :] = dv

        dsout_ref[...] = dsin_ref[...] + dsacc[...]

        dgq = jnp.zeros((1, LANES), F32)
        dgk = jnp.zeros((1, LANES), F32)
        for c in range(nchunk):
            rows = _chunk_rows(c, d, nb)
            qv = qs[rows, :]
            rq = lax.rsqrt(_head_mean(qv * qv, seg) + EPS)
            qh = qv * rq
            dy = dqd[c * BLOCK:(c + 1) * BLOCK, :]
            dgq = dgq + jnp.sum(dy * qh, axis=0, keepdims=True) * qscale
            dxh = dy * qg
            ddd[rows, :] = rq * (dxh - qh * _head_mean(dxh * qh, seg))
            kv = ks[rows, :]
            rk = lax.rsqrt(_head_mean(kv * kv, seg) + EPS)
            kh = kv * rk
            dy = dkd[c * BLOCK:(c + 1) * BLOCK, :]
            dgk = dgk + jnp.sum(dy * kh, axis=0, keepdims=True)
            dxh = dy * kg
            ld[rows, :] = rk * (dxh - kh * _head_mean(dxh * kh, seg))
        @pl.when(h > 0)
        def _():
            for cp in flush(h - 1):
                cp.wait()

        qst[...] = ddd[...].astype(BF16)
        kst[...] = ld[...].astype(BF16)
        for c in range(nchunk):
            ddd[_chunk_rows(c, d, nb), :] = dvd[c * BLOCK:(c + 1) * BLOCK, :]
        vst[...] = ddd[...].astype(BF16)
        for cp in flush(h):
            cp.start()

        @pl.when(h == hp - 1)
        def _():
            for cp in flush(h):
                cp.wait()

        dgq_ref[0] = dgq
        dgk_ref[0] = dgk

    ucol = lambda c0: pl.BlockSpec((t, LANES), lambda h: (0, c0 + h))
    vec = pl.BlockSpec((1, LANES), lambda h: (0, 0))
    oblk = pl.BlockSpec((t, LANES), lambda h: (0, h))
    bblk = pl.BlockSpec((2, BLOCK, 2 * BLOCK), lambda h: (gi * hp + h, 0, 0))
    gblk = pl.BlockSpec((1, 1, LANES), lambda h: (h, 0, 0))
    gsh = jax.ShapeDtypeStruct((hp, 1, LANES), F32)
    hbm = pl.BlockSpec(memory_space=pl.ANY)
    return pl.pallas_call(
        body, name=name, grid=(hp,),
        in_specs=[ucol(qc0), ucol(kc0), ucol(vc0), oblk, oblk, oblk, vec, vec, bblk, bblk, hbm],
        out_specs=(gblk, gblk, bblk, hbm),
        out_shape=(gsh, gsh, jax.ShapeDtypeStruct(ds_in.shape, F32), jax.ShapeDtypeStruct(du_in.shape, BF16)),
        input_output_aliases={9: 2, 10: 3},
        scratch_shapes=[pltpu.VMEM((t, LANES), BF16), pltpu.VMEM((t + BLOCK, LANES), BF16),
                        pltpu.VMEM((t + BLOCK, LANES), BF16), pltpu.VMEM((t, LANES), BF16),
                        pltpu.VMEM((t, LANES), F32), pltpu.VMEM((t, LANES), F32), pltpu.VMEM((t, LANES), F32),
                        pltpu.VMEM((t, LANES), F32), pltpu.VMEM((t, LANES), F32),
                        pltpu.VMEM((2, BLOCK, 2 * BLOCK), F32), pltpu.VMEM((2, nchunk, BLOCK, 2 * BLOCK), BF16),
                        pltpu.VMEM((2, nchunk, BLOCK, 2 * BLOCK), BF16), pltpu.VMEM((t, LANES), F32),
                        pltpu.VMEM((t, LANES), F32), pltpu.VMEM((t, LANES), BF16), pltpu.VMEM((t, LANES), BF16),
                        pltpu.VMEM((t, LANES), BF16), pltpu.SemaphoreType.DMA((3,))],
        compiler_params=_cparams(("arbitrary",)),
    )(u, u, u, do_g, dd_g, lse_g, qg2, kg2, bias, ds_in, du_in)


def _group_weights(l0, l1, l2):
    mx = jnp.maximum(jnp.maximum(l0, l1), l2)
    e0, e1, e2 = jnp.exp(l0 - mx), jnp.exp(l1 - mx), jnp.exp(l2 - mx)
    inv = 1.0 / (e0 + e1 + e2)
    return e0 * inv, e1 * inv, e2 * inv


def _combine_fwd(os_, lses, name):
    t, ao = os_[0].shape
    ROWS = _rows(t, 26 * ao)

    def body(o0, o1, o2, l0, l1, l2, o_ref):
        w0, w1, w2 = _group_weights(l0[...], l1[...], l2[...])
        o_ref[...] = (w0 * o0[...] + w1 * o1[...] + w2 * o2[...]).astype(BF16)

    row = pl.BlockSpec((ROWS, ao), lambda i: (i, 0))
    return pl.pallas_call(
        body, name=name, grid=(t // ROWS,), in_specs=[row] * 6, out_specs=row,
        out_shape=jax.ShapeDtypeStruct((t, ao), BF16), compiler_params=_cparams(("arbitrary",)),
    )(*os_, *lses)


def _combine_bwd(do, os_, lses, name):
    t, ao = do.shape
    idx = np.arange(ao) // HEAD_DIM
    seg = jnp.asarray((idx[:, None] == idx[None, :]).astype(np.float32), dtype=BF16)
    ROWS = _rows(t, 52 * ao)

    def body(do_ref, o0, o1, o2, l0, l1, l2, seg_ref, g0, g1, g2, d0, d1, d2):
        w0, w1, w2 = _group_weights(l0[...], l1[...], l2[...])
        dov = do_ref[...]
        o = w0 * o0[...] + w1 * o1[...] + w2 * o2[...]
        sd = _segment_sum(dov * o, seg_ref[...])
        for w, gref, dref in ((w0, g0, d0), (w1, g1, d1), (w2, g2, d2)):
            gref[...] = w * dov
            dref[...] = -(w * sd)

    row = pl.BlockSpec((ROWS, ao), lambda i: (i, 0))
    sh = jax.ShapeDtypeStruct((t, ao), F32)
    outs = pl.pallas_call(
        body, name=name, grid=(t // ROWS,), in_specs=[row] * 7 + [pl.BlockSpec((ao, ao), lambda i: (0, 0))],
        out_specs=(row,) * 6, out_shape=(sh,) * 6, compiler_params=_cparams(("arbitrary",)),
    )(do, *os_, *lses, seg)
    return outs[:3], outs[3:]


def _adamw(w, g, m, v, name):
    shape = w.shape
    cols = shape[-1]
    rows = int(np.prod(shape[:-1]))
    tr = rows if rows <= 512 else _tile_rows(rows)
    c1 = 1.0 - ADAM_B1 ** ADAM_STEP
    c2 = 1.0 - ADAM_B2 ** ADAM_STEP

    def body(w_ref, g_ref, m_ref, v_ref, d_ref, nm_ref, nv_ref):
        gv = g_ref[...]
        mn = ADAM_B1 * m_ref[...] + (1.0 - ADAM_B1) * gv
        vn = ADAM_B2 * v_ref[...] + (1.0 - ADAM_B2) * (gv * gv)
        nm_ref[...] = mn
        nv_ref[...] = vn
        d_ref[...] = -ADAM_LR * ((mn / c1) / (jnp.sqrt(vn / c2) + ADAM_EPS) + ADAM_WD * w_ref[...])

    blk = pl.BlockSpec((tr, cols), lambda i: (i, 0))
    sh = jax.ShapeDtypeStruct((rows, cols), F32)
    outs = pl.pallas_call(
        body, name=name, grid=(rows // tr,), in_specs=[blk] * 4, out_specs=(blk,) * 3, out_shape=(sh,) * 3,
        compiler_params=_cparams(("arbitrary",)),
    )(*[a.reshape(rows, cols) for a in (w, g, m, v)])
    return tuple(o.reshape(shape) for o in outs)


def _tile_rows(rows):
    for t in (512, 256, 128, 64, 32, 16, 8):
        if rows % t == 0:
            return t
    return rows


def _sum_slots(recv, parts, me, layers, l, name):
    _, rows, cols = recv.shape
    tr = rows if rows <= 512 else _tile_rows(rows)

    def body(me_ref, r_ref, own_ref, _, o_ref):
        acc = jnp.zeros(o_ref.shape, F32)
        for s in range(N_DEV):
            acc = acc + jnp.where(me_ref[0] == s, own_ref[...], r_ref[s]).astype(F32)
        o_ref[...] = acc

    return pl.pallas_call(
        body, name=name,
        grid_spec=pltpu.PrefetchScalarGridSpec(
            num_scalar_prefetch=1, grid=(rows // tr,),
            in_specs=[pl.BlockSpec((N_DEV, tr, cols), lambda i, me: (0, i, 0)),
                      pl.BlockSpec((None, tr, cols), lambda i, me: (me[0], i, 0)),
                      pl.BlockSpec(memory_space=pl.ANY)],
            out_specs=pl.BlockSpec((None, tr, cols), lambda i, me: (l, i, 0))),
        out_shape=jax.ShapeDtypeStruct(layers.shape, F32), input_output_aliases={3: 0},
        compiler_params=_cparams(("arbitrary",)),
    )(me.reshape(1), recv, parts, layers)


def _peer(k):
    x, y, c = lax.axis_index("x"), lax.axis_index("y"), lax.axis_index("c")
    return (1 - x if k & 4 else x, 1 - y if k & 2 else y, 1 - c if k & 1 else c)


def _dev_index(p):
    return 4 * p[0] + 2 * p[1] + p[2]


HBM_SPEC = pl.BlockSpec(memory_space=pltpu.HBM)
SEM_SPEC = pl.BlockSpec(memory_space=pltpu.SEMAPHORE)
ANY_SPEC = pl.BlockSpec(memory_space=pl.ANY)
CHIPS = (4, 2, 6)


def _remote(src, dst, send_sem, recv_sem, to):
    return pltpu.make_async_remote_copy(src_ref=src, dst_ref=dst, send_sem=send_sem, recv_sem=recv_sem,
                                        device_id=to, device_id_type=MESH)


def _hbm(a):
    return pltpu.with_memory_space_constraint(a, pltpu.HBM)


def _split_call(body, name, bufs, sems_in, sem_out_sizes, after):
    nb, ns, no = len(bufs), len(sems_in), len(sem_out_sizes)
    extra = [] if after is None else list(after) if isinstance(after, (tuple, list)) else [after]

    def kern(*refs):
        pos = nb + ns + len(extra)
        body(refs[:nb], refs[nb:nb + ns], refs[pos:pos + no])
        token_ref = refs[pos + no + nb]
        token_ref[...] = jnp.zeros_like(token_ref)

    out_shape = (tuple(pltpu.SemaphoreType.DMA((s,)) for s in sem_out_sizes)
                 + tuple(pltpu.HBM(b.shape, b.dtype) for b in bufs) + (jax.ShapeDtypeStruct((8, LANES), F32),))
    res = pl.pallas_call(
        kern, name=name, out_shape=out_shape,
        in_specs=[HBM_SPEC] * nb + [SEM_SPEC] * ns + [ANY_SPEC] * len(extra),
        out_specs=(SEM_SPEC,) * no + (HBM_SPEC,) * nb + (pl.BlockSpec(memory_space=pltpu.VMEM),),
        input_output_aliases={i: no + i for i in range(nb)},
        compiler_params=pltpu.CompilerParams(has_side_effects=pltpu.SideEffectType.DATAFLOW_SIDE_EFFECTING),
    )(*bufs, *sems_in, *extra)
    return res[:no], res[no:no + nb], res[no + nb]


def _gather_start(shards, lands, after, name):
    n = len(shards)

    def body(bufs, _, sems):
        ins, lnd = bufs[:n], bufs[n:]
        d2d_s, d2d_r, ici_s, ici_r = sems
        me = _dev_index(_peer(0))
        for j, k in enumerate(CHIPS):
            for i in range(n):
                _remote(ins[i], lnd[i].at[me], ici_s.at[j], ici_r.at[j], _peer(k)).start()
        for i in range(n):
            _remote(ins[i], lnd[i].at[me], d2d_s.at[0], d2d_r.at[0], _peer(1)).start()

    return _split_call(body, name, [_hbm(a) for a in (*shards, *lands)], [], (1, 1, 3, 3), after)


def _gather_forward(n, bufs, ici_r, after, name):
    def body(refs, sems_in, sems):
        ins, lnd = refs[:n], refs[n:]
        (arrived,) = sems_in
        fwd_s, fwd_r = sems
        for j, k in enumerate(CHIPS):
            blk = _dev_index(_peer(k))
            for i in range(n):
                _remote(ins[i], lnd[i].at[blk], fwd_s.at[j], arrived.at[j], _peer(k)).wait_recv()
            for i in range(n):
                _remote(lnd[i].at[blk], lnd[i].at[blk], fwd_s.at[j], fwd_r.at[j], _peer(1)).start()

    return _split_call(body, name, bufs, [ici_r], (3, 3), after)


def _gather_finish(n, bufs, d2d_s, d2d_r, ici_s, fwd_s, fwd_r, after, name):
    def body(refs, sems_in, _):
        ins, lnd = refs[:n], refs[n:]
        d2d_send, d2d_recv, ici_send, fwd_send, fwd_recv = sems_in
        sib = _peer(1)
        for i in range(n):
            cp = _remote(ins[i], lnd[i].at[_dev_index(sib)], d2d_send.at[0], d2d_recv.at[0], sib)
            cp.wait_send()
            cp.wait_recv()
        for j, k in enumerate(CHIPS):
            passed = _dev_index(_peer(k))
            landed = _dev_index(_peer(k | 1))
            for i in range(n):
                _remote(ins[i], lnd[i].at[passed], ici_send.at[j], fwd_recv.at[j], _peer(k)).wait_send()
                cp = _remote(lnd[i].at[passed], lnd[i].at[landed], fwd_send.at[j], fwd_recv.at[j], sib)
                cp.wait_send()
                cp.wait_recv()

    _, out, token = _split_call(body, name, bufs, [d2d_s, d2d_r, ici_s, fwd_s, fwd_r], (), after)
    return out[n:], token


def _exchange_start(parts, lands, after, name):
    n = len(parts)

    def body(bufs, _, sems):
        src, lnd = bufs[:n], bufs[n:]
        send, recv = sems
        me = _dev_index(_peer(0))
        for k in (4, 5, 2, 3, 6, 7, 1):
            to = _peer(k)
            for i in range(n):
                _remote(src[i].at[_dev_index(to)], lnd[i].at[me], send.at[k - 1], recv.at[k - 1], to).start()

    return _split_call(body, name, [_hbm(a) for a in (*parts, *lands)], [], (7, 7), after)


def _exchange_finish(n, bufs, send, recv, after, name):
    def body(refs, sems_in, _):
        src, lnd = refs[:n], refs[n:]
        send_, recv_ = sems_in
        me = _dev_index(_peer(0))
        for k in range(1, N_DEV):
            frm = _peer(k)
            for i in range(n):
                cp = _remote(src[i].at[me], lnd[i].at[_dev_index(frm)], send_.at[k - 1], recv_.at[k - 1], frm)
                cp.wait_send()
                cp.wait_recv()

    _, out, token = _split_call(body, name, bufs, [send, recv], (), after)
    return out[:n], out[n:], token


def _all_reduce_small(v, after, name):
    rows = v.shape[0]

    def body(v_ref, _, o_ref, buf, send_sems, recv_sems):
        me = _dev_index(_peer(0))
        buf[me] = v_ref[...]
        copies = []
        for k in range(1, N_DEV):
            copies.append(pltpu.make_async_remote_copy(
                src_ref=v_ref, dst_ref=buf.at[me], send_sem=send_sems.at[k - 1], recv_sem=recv_sems.at[k - 1],
                device_id=_peer(k), device_id_type=MESH))
        for cp in copies:
            cp.start()
        for k in range(1, N_DEV):
            pltpu.make_async_remote_copy(
                src_ref=v_ref, dst_ref=buf.at[_dev_index(_peer(k))], send_sem=send_sems.at[k - 1],
                recv_sem=recv_sems.at[k - 1], device_id=_peer(k), device_id_type=MESH).wait_recv()
        for cp in copies:
            cp.wait_send()
        acc = buf[0]
        for s in range(1, N_DEV):
            acc = acc + buf[s]
        o_ref[...] = acc

    vm = pl.BlockSpec(memory_space=pltpu.VMEM)
    return pl.pallas_call(
        body, name=name, in_specs=[vm, pl.BlockSpec(memory_space=pl.ANY)], out_specs=vm,
        out_shape=jax.ShapeDtypeStruct(v.shape, F32),
        scratch_shapes=[pltpu.VMEM((N_DEV, rows, LANES), F32), pltpu.SemaphoreType.DMA((7,)),
                        pltpu.SemaphoreType.DMA((7,))],
    )(v, after)


def _columns(cdim, ao):
    q_col = 2 * cdim
    attn_dim = N_GROUPS * ao
    return (q_col, q_col + attn_dim, q_col + 2 * attn_dim), q_col + 3 * attn_dim


def _layer_fwd(x, h1, sm, bg, get_rest, bias, hpg, next_gain, at_ff1=None):
    cdim = sm["conv_ln_g"].shape[0]
    ao = hpg * HEAD_DIM
    cols, gate_col = _columns(cdim, ao)
    qg2 = jnp.tile(sm["q_norm_g"], 2).reshape(1, LANES)
    kg2 = jnp.tile(sm["k_norm_g"], 2).reshape(1, LANES)
    u = _mm_in_pairs(h1, bg["w_in"], "mm_in")
    zc = _conv_fwd(u, bg["conv_dw_w"], sm["conv_dw_b"], cdim, "conv_fwd")
    zs = _ln_swish_fwd(zc, sm["conv_ln_g"], sm["conv_ln_b"], "ln_swish_fwd")
    os_, lses = [], []
    for gi in range(N_GROUPS):
        o_g, lse_g = _attn_fwd(u, qg2, kg2, bias, gi, cols, hpg, "attn_fwd_g%d" % gi)
        os_.append(o_g)
        lses.append(lse_g)
    o = _combine_fwd(os_, lses, "combine_fwd")
    bg = {**bg, **get_rest(o)}
    mg, yc, ya = _gated_out(zs, o, bg["w_conv_out"], bg["w_attn_out"], u, gate_col, "gated_out")
    x1, h2 = _mm(mg, bg["w_out"], epi="res_rms", extra=x, gain=sm["norm2_g"], name="mm_out")
    f = _mm(h2, bg["w_ff1"], out_dtype=BF16, name="mm_ff1")
    after = at_ff1(f) if at_ff1 is not None else None
    if next_gain is None:
        x2, h_next = _mm(f, bg["w_ff2"], a_relu2=True, epi="res", extra=x1, after=after, name="mm_ff2"), None
    else:
        x2, h_next = _mm(f, bg["w_ff2"], a_relu2=True, epi="res_rms", extra=x1, gain=next_gain, after=after,
                         name="mm_ff2")
    saved = dict(x=x, h1=h1, u=u, zc=zc, zs=zs, yc=yc, os=os_, lses=lses, o=o, ya=ya, mg=mg, x1=x1, h2=h2, f=f,
                 qg2=qg2, kg2=kg2)
    return x2, h_next, saved, bg


GRAD_GROUPS = (("w_ff2", "w_ff1"), ("w_out", "w_conv_out", "w_attn_out", "conv_dw_w"), ("w_in",))


def _layer_bwd(dx, s, sm, bg, bias, ds_sum, after, emit):
    cdim = sm["conv_ln_g"].shape[0]
    ao = bg["w_attn_out"].shape[0]
    hpg = ao // HEAD_DIM
    cols, gate_col = _columns(cdim, ao)
    g = {}
    df = _mm(dx, bg["w_ff2"], tb=True, epi="drelu2", extra=s["f"], out_dtype=BF16, after=after, name="mm_dff2")
    g["w_ff2"] = _mm(s["f"], dx, ta=True, a_relu2=True, out_dtype=BF16, name="mm_gw_ff2")
    g["w_ff1"] = _mm(s["h2"], df, ta=True, out_dtype=BF16, out_slots=True, name="mm_gw_ff1")
    after = emit(GRAD_GROUPS[0], g)
    dx1, dg2 = _mm(df, bg["w_ff1"], tb=True, epi="rms_bwd", extra=(s["x1"], dx), gain=sm["norm2_g"], after=after,
                   name="mm_dff1")
    g["norm2_g"] = dg2[0]
    dmg = _mm(dx1, bg["w_out"], tb=True, name="mm_dout")
    g["w_out"] = _mm(s["mg"], dx1, ta=True, out_dtype=BF16, name="mm_gw_out")
    dyc, dya, dugc, duga = _gate_bwd(dmg, s["u"], s["yc"], s["ya"], gate_col, "gate_bwd")
    dzs = _mm(dyc, bg["w_conv_out"], tb=True, name="mm_dconv_out")
    g["w_conv_out"] = _mm(s["zs"], dyc, ta=True, out_dtype=BF16, name="mm_gw_conv_out")
    do = _mm(dya, bg["w_attn_out"], tb=True, name="mm_dattn_out")
    g["w_attn_out"] = _mm(s["o"], dya, ta=True, out_dtype=BF16, name="mm_gw_attn_out")
    dzc, dlg, dlb = _ln_swish_bwd(dzs, s["zc"], sm["conv_ln_g"], sm["conv_ln_b"], "ln_swish_bwd")
    g["conv_ln_g"] = dlg[0]
    g["conv_ln_b"] = dlb[0]
    da, dgt, dcw, dcb = _conv_bwd(dzc, s["u"], bg["conv_dw_w"], cdim, "conv_bwd")
    g["conv_dw_w"] = dcw[:CONV_WIDTH].astype(BF16)
    g["conv_dw_b"] = dcb[0]
    after = emit(GRAD_GROUPS[1], g)
    do_gs, dd_gs = _combine_bwd(do, s["os"], s["lses"], "combine_bwd")
    du = lax.empty(s["u"].shape, BF16)
    for col, piece in ((0, da), (cdim, dgt), (gate_col, dugc), (gate_col + dugc.shape[1], duga)):
        du = lax.dynamic_update_slice(du, piece, (0, col))
    gqs, gks = [], []
    for gi in range(N_GROUPS):
        gq, gk, ds_sum, du = _attn_bwd(s["u"], do_gs[gi], dd_gs[gi], s["lses"][gi], s["qg2"], s["kg2"], bias,
                                       ds_sum, du, gi, cols, hpg, "attn_bwd_g%d" % gi)
        gqs.append(gq)
        gks.append(gk)
    g["q_norm_g"] = jnp.concatenate(gqs)
    g["k_norm_g"] = jnp.concatenate(gks)
    g["w_in"] = _mm_gw_in_pairs(s["h1"], du, after, "mm_gw_in")
    after = emit(GRAD_GROUPS[2], g)
    dx0, dg1 = _mm_din_pairs(du, bg["w_in"], s["x"], sm["norm1_g"], dx1, after, "mm_din")
    g["norm1_g"] = dg1[0]
    return dx0, g, ds_sum


BIG = ("w_in", "conv_dw_w", "w_conv_out", "w_attn_out", "w_out", "w_ff1", "w_ff2")
COL_SHARDED = ("w_in", "conv_dw_w", "w_conv_out", "w_attn_out", "w_ff1")
SMALL = ("rel_bias", "norm1_g", "q_norm_g", "k_norm_g", "conv_dw_b", "conv_ln_g", "conv_ln_b", "norm2_g")
WEIGHTS = ("rel_bias", "norm1_g", "w_in", "q_norm_g", "k_norm_g", "conv_dw_w", "conv_dw_b", "conv_ln_g", "conv_ln_b",
           "w_conv_out", "w_attn_out", "w_out", "norm2_g", "w_ff1", "w_ff2")


def _to_whole(name, gathered):
    n, a, b = gathered.shape
    if name in COL_SHARDED:
        return gathered.transpose(1, 0, 2).reshape(a, n * b)
    return gathered.reshape(n * a, b)


def _to_slots(name, whole):
    a, b = whole.shape
    if name in COL_SHARDED:
        return whole.reshape(a, N_DEV, b // N_DEV).transpose(1, 0, 2)
    return whole.reshape(N_DEV, a // N_DEV, b)


def _own_slot(block, me):
    land = lax.empty((N_DEV,) + block.shape, block.dtype)
    return lax.dynamic_update_slice(land, block[None], (me,) + (0,) * block.ndim)


def kernel(x, rel_bias, norm1_g, w_in, q_norm_g, k_norm_g, conv_dw_w, conv_dw_b, conv_ln_g, conv_ln_b, w_conv_out, w_attn_out, w_out, norm2_g, w_ff1, w_ff2, loss_target, m_rel_bias, m_norm1_g, m_w_in, m_q_norm_g, m_k_norm_g, m_conv_dw_w, m_conv_dw_b, m_conv_ln_g, m_conv_ln_b, m_w_conv_out, m_w_attn_out, m_w_out, m_norm2_g, m_w_ff1, m_w_ff2, v_rel_bias, v_norm1_g, v_w_in, v_q_norm_g, v_k_norm_g, v_conv_dw_w, v_conv_dw_b, v_conv_ln_g, v_conv_ln_b, v_w_conv_out, v_w_attn_out, v_w_out, v_norm2_g, v_w_ff1, v_w_ff2):
    w = dict(rel_bias=rel_bias, norm1_g=norm1_g, w_in=w_in, q_norm_g=q_norm_g, k_norm_g=k_norm_g, conv_dw_w=conv_dw_w,
             conv_dw_b=conv_dw_b, conv_ln_g=conv_ln_g, conv_ln_b=conv_ln_b, w_conv_out=w_conv_out,
             w_attn_out=w_attn_out, w_out=w_out, norm2_g=norm2_g, w_ff1=w_ff1, w_ff2=w_ff2)
    mom = dict(rel_bias=m_rel_bias, norm1_g=m_norm1_g, w_in=m_w_in, q_norm_g=m_q_norm_g, k_norm_g=m_k_norm_g,
               conv_dw_w=m_conv_dw_w, conv_dw_b=m_conv_dw_b, conv_ln_g=m_conv_ln_g, conv_ln_b=m_conv_ln_b,
               w_conv_out=m_w_conv_out, w_attn_out=m_w_attn_out, w_out=m_w_out, norm2_g=m_norm2_g, w_ff1=m_w_ff1,
               w_ff2=m_w_ff2)
    var = dict(rel_bias=v_rel_bias, norm1_g=v_norm1_g, w_in=v_w_in, q_norm_g=v_q_norm_g, k_norm_g=v_k_norm_g,
               conv_dw_w=v_conv_dw_w, conv_dw_b=v_conv_dw_b, conv_ln_g=v_conv_ln_g, conv_ln_b=v_conv_ln_b,
               w_conv_out=v_w_conv_out, w_attn_out=v_w_attn_out, w_out=v_w_out, norm2_g=v_norm2_g, w_ff1=v_w_ff1,
               w_ff2=v_w_ff2)

    depth = norm1_g.shape[0]
    me = 4 * lax.axis_index("x") + 2 * lax.axis_index("y") + lax.axis_index("c")
    odd_core = lax.axis_index("c") == 1
    hpg = w_attn_out.shape[1] // HEAD_DIM
    buckets = jnp.asarray(_bucket_table())

    first_names = ("w_in", "conv_dw_w")
    rest_names = tuple(k for k in BIG if k not in first_names)

    def chain_start(l, names, after):
        shards = [w[k][l] if k == "conv_dw_w" else w[k][l].astype(BF16) for k in names]
        if "w_in" in names:
            i = names.index("w_in")
            shards[i] = jnp.where(odd_core, jnp.pad(shards[i], ((0, 0), (SHIFT, 0))),
                                  jnp.pad(shards[i], ((0, 0), (0, SHIFT))))
        sems, bufs, token = _gather_start(shards, [_own_slot(s, me) for s in shards], after,
                                          "gather_start_%s_l%d" % (names[0], l))
        return dict(l=l, names=names, sems=sems, bufs=bufs, token=token)

    def chain_forward(ch, after):
        fwd, bufs, token = _gather_forward(len(ch["names"]), ch["bufs"], ch["sems"][3], after,
                                           "gather_forward_%s_l%d" % (ch["names"][0], ch["l"]))
        ch.update(fwd=fwd, bufs=bufs)
        return token

    def chain_finish(ch, after):
        d2d_s, d2d_r, ici_s, _ = ch["sems"]
        gathered, _ = _gather_finish(len(ch["names"]), ch["bufs"], d2d_s, d2d_r, ici_s, ch["fwd"][0], ch["fwd"][1],
                                     after, "gather_finish_%s_l%d" % (ch["names"][0], ch["l"]))
        out = {k: a if k == "w_in" else _to_whole(k, a) for k, a in zip(ch["names"], gathered)}
        if "conv_dw_w" in out:
            out["conv_dw_w"] = jnp.pad(out["conv_dw_w"], ((0, CONV_TAPS_PADDED - CONV_WIDTH), (0, 0)))
        return out

    xs = x[0]
    first = chain_start(0, first_names, None)
    h1 = _rms_fwd(xs, norm1_g[0], first["token"], "rms1_fwd")
    bias = _bias_expand(rel_bias, buckets, hpg, h1, "bias_expand")
    saved, bigs, smalls = [], [], []
    chains = {}
    for l in range(depth):
        sm = {k: w[k][l] for k in SMALL if k != "rel_bias"}
        if l == 0:
            token = chain_forward(first, bias)
            rest = chain_start(0, rest_names, token)
            bg = chain_finish(first, rest["token"])

            def get_rest(o, rest=rest):
                token = chain_forward(rest, o)
                if depth > 1:
                    chains[1] = (chain_start(1, first_names, token),)
                    chains[1] += (chain_start(1, rest_names, chains[1][0]["token"]),)
                    token = chains[1][1]["token"]
                return chain_finish(rest, token)
        elif l == 1:
            first, rest = chains[1]
            token = chain_forward(first, xs)
            if depth > 2:
                chains[2] = chain_start(2, BIG, token)
                token = chains[2]["token"]
            bg = chain_finish(first, token)

            def get_rest(o, rest=rest):
                return chain_finish(rest, chain_forward(rest, o))
        else:
            token = xs if "fwd" in chains[l] else chain_forward(chains[l], xs)
            if l + 1 < depth:
                chains[l + 1] = chain_start(l + 1, BIG, token)
                token = chains[l + 1]["token"]
            whole = chain_finish(chains[l], token)
            bg = {k: whole[k] for k in first_names}

            def get_rest(o, whole=whole):
                return {k: whole[k] for k in rest_names}
        at_ff1 = (lambda f, nxt=l + 1: chain_forward(chains[nxt], f)) if 2 <= l < depth - 1 else None
        xs, h1, sv, bg = _layer_fwd(xs, h1, sm, bg, get_rest, bias, hpg, norm1_g[l + 1] if l + 1 < depth else None,
                                    at_ff1)
        saved.append(sv)
        bigs.append(bg)
        smalls.append(sm)

    loss, dx = _loss_and_grad(xs, loss_target[0], "loss")

    ds_sum = jnp.zeros((N_GROUPS * hpg, BLOCK, 2 * BLOCK), F32)
    g = {k: [None] * depth for k in SMALL if k != "rel_bias"}
    sums = {k: lax.empty((depth, int(np.prod(w[k].shape[1:-1])), w[k].shape[-1] + (SHIFT if k == "w_in" else 0)), F32)
            for k in BIG}
    pending = []

    def finish_oldest(after):
        names, l, (send, recv), bufs = pending.pop(0)
        parts, recvd, token = _exchange_finish(len(names), bufs, send, recv, after,
                                               "exchange_finish_%s_l%d" % (names[0], l))
        for k, r, p in zip(names, recvd, parts):
            three = (N_DEV, -1, r.shape[-1])
            sums[k] = _sum_slots(r.reshape(three), p.reshape(three), me, sums[k], l, "sum_" + k)
        return token

    def make_emit(l):
        def emit(names, gl):
            parts = [gl[k] if k in ("w_ff1", "w_in") else _to_slots(k, gl[k]) for k in names]
            token = finish_oldest(parts[0]) if len(pending) >= len(GRAD_GROUPS) else None
            lands = [lax.empty(p.shape, p.dtype) for p in parts]
            sems, bufs, token = _exchange_start(parts, lands, token, "exchange_start_%s_l%d" % (names[0], l))
            pending.append((names, l, sems, bufs))
            return token
        return emit

    token = None
    for l in reversed(range(depth)):
        dx, gl, ds_sum = _layer_bwd(dx, saved[l], smalls[l], bigs[l], bias, ds_sum, token, make_emit(l))
        for k in g:
            g[k][l] = gl[k]
        token = None
    grad_x = dx

    g = {k: jnp.stack(v) for k, v in g.items()}
    for k in ("q_norm_g", "k_norm_g"):
        g[k] = jnp.sum(g[k].reshape(depth, -1, HEAD_DIM), axis=1)
    db = _bias_reduce(ds_sum, buckets, hpg, "bias_reduce")
    g["rel_bias"] = db[:, 0, :NUM_BUCKETS].T

    flat = jnp.concatenate([g[k].reshape(-1) for k in SMALL])
    nflat = flat.shape[0]
    rows = -(-nflat // (8 * LANES)) * 8
    packed = jnp.pad(flat, (0, rows * LANES - nflat)).reshape(rows, LANES)
    grad, outs = {}, {}
    token = dx
    while pending:
        names = pending[0][0]
        finish_oldest(token)
        for k in names:
            total = sums[k]
            if k == "w_in":
                total = jnp.where(odd_core, total[..., SHIFT:], total[..., :w_in.shape[-1]])
            grad[k] = total.reshape(w[k].shape)
            outs[k] = _adamw(w[k], grad[k], mom[k], var[k], "adamw_" + k)
            token = outs[k][0]
    total = _all_reduce_small(packed, token, "reduce_small").reshape(-1)
    off = 0
    for k in SMALL:
        size = int(np.prod(w[k].shape))
        grad[k] = total[off:off + size].reshape(w[k].shape)
        outs[k] = _adamw(w[k], grad[k], mom[k], var[k], "adamw_" + k)
        off += size
    loss = lax.psum(loss[0, 0], ("x", "y", "c"))
    return (loss, grad_x[None], *[grad[k] for k in WEIGHTS], *[outs[k][0] for k in WEIGHTS],
            *[outs[k][1] for k in WEIGHTS], *[outs[k][2] for k in WEIGHTS])
```

```python
import functools
import math

import numpy as np
import jax
import jax.numpy as jnp
from jax import lax
from jax.experimental import pallas as pl
from jax.experimental.pallas import tpu as pltpu

F32 = jnp.float32
BF16 = jnp.bfloat16

HEAD_DIM = 64
N_GROUPS = 3
DILATIONS = (1, 4, 16)
SUB_WINDOW = 128
BLOCK = 128
CONV_WIDTH = 31
CONV_TAPS_PADDED = 32
NUM_BUCKETS = 32
MAX_REL_DISTANCE = 2048
EPS = 1e-6
NEG_INF = -1e30
LANES = 128
SUBLANES = 8

ADAM_LR = 0.001
ADAM_B1 = 0.9
ADAM_B2 = 0.999
ADAM_EPS = 1e-08
ADAM_WD = 0.01
ADAM_STEP = 10

N_DEV = 8
VMEM_LIMIT = 56 * 1024 * 1024
MESH = pl.DeviceIdType.MESH


def _cparams(sem=None):
    return pltpu.CompilerParams(dimension_semantics=sem, vmem_limit_bytes=VMEM_LIMIT)


def _tile(n, target):
    if n <= target:
        return n
    t = (target // LANES) * LANES
    while t >= LANES:
        if n % t == 0:
            return t
        t -= LANES
    return n


def _sigmoid(v):
    return 1.0 / (1.0 + jnp.exp(-v))


MM_VMEM_BUDGET = 40 * 1024 * 1024


def _rms_apply(x, g):
    return x * lax.rsqrt(jnp.mean(x * x, axis=-1, keepdims=True) + EPS) * g


def _rms_grad(dh, x, g):
    r = lax.rsqrt(jnp.mean(x * x, axis=-1, keepdims=True) + EPS)
    xh = x * r
    dxh = dh * g
    dx = r * (dxh - xh * jnp.mean(dxh * xh, axis=-1, keepdims=True))
    return dx, jnp.sum(dh * xh, axis=0, keepdims=True)


def _mm_tiles(m, n, kdim, a_bytes, b_bytes, io_bytes, whole_rows=False, temps=2):
    def need(tm, tn, tk):
        blocks = 2 * (tm * tk * a_bytes + tk * tn * b_bytes + tm * tn * io_bytes)
        casts = (tm * tk * 2 if a_bytes == 4 else 0) + (tk * tn * 2 if b_bytes == 4 else 0)
        return blocks + casts + temps * tm * tn * 4

    tn = n if whole_rows else _tile(n, 1024)
    while True:
        fits = [(tm * tk, tm, tk) for tm in {_tile(m, c) for c in (1024, 512, 256, 128)}
                for tk in {_tile(kdim, c) for c in (2048, 1024, 512, 256)} if need(tm, tn, tk) <= MM_VMEM_BUDGET]
        if fits:
            _, tm, tk = max(fits)
            return tm, tn, tk
        assert not whole_rows and tn % 256 == 0, "no block size fits the VMEM budget"
        tn //= 2


def _mm(a, b, *, ta=False, tb=False, out_dtype=F32, epi=None, extra=(), gain=None, after=None, out_slots=False,
        a_relu2=False, name):
    extra = tuple(extra) if isinstance(extra, (tuple, list)) else (extra,)
    m = a.shape[1] if ta else a.shape[0]
    kdim = a.shape[0] if ta else a.shape[1]
    n = b.shape[0] if tb else b.shape[1]
    norm = epi in ("res_rms", "rms_bwd")
    io_bytes = (jnp.dtype(out_dtype).itemsize + sum(e.dtype.itemsize for e in extra) + (2 if epi == "res_rms" else 0))
    tm, tn, tk = _mm_tiles(m, n // N_DEV if out_slots else n, kdim, a.dtype.itemsize, b.dtype.itemsize, io_bytes,
                           whole_rows=norm, temps=6 if norm else 2)
    if out_slots:
        assert epi is None and tn == n // N_DEV
    nk = kdim // tk
    a_spec = pl.BlockSpec((tk, tm), lambda i, j, k: (k, i)) if ta else pl.BlockSpec((tm, tk), lambda i, j, k: (i, k))
    b_spec = pl.BlockSpec((tn, tk), lambda i, j, k: (j, k)) if tb else pl.BlockSpec((tk, tn), lambda i, j, k: (k, j))
    o_spec = (pl.BlockSpec((None, tm, tn), lambda i, j, k: (j, i, 0)) if out_slots
              else pl.BlockSpec((tm, tn), lambda i, j, k: (i, j)))
    v_spec = pl.BlockSpec((1, tn), lambda i, j, k: (0, j))
    dims = (((0 if ta else 1,), (1 if tb else 0,)), ((), ()))
    n_extra = len(extra)
    n_in = 2 + n_extra + (gain is not None) + (after is not None)
    n_out = 2 if norm else 1

    def body(*refs):
        a_ref, b_ref = refs[0], refs[1]
        e_refs = refs[2:2 + n_extra]
        g_ref = refs[2 + n_extra] if gain is not None else None
        outs = refs[n_in:n_in + n_out]

        def product():
            av = a_ref[...]
            if a_relu2:
                r = jnp.maximum(av.astype(F32), 0.0)
                av = r * r
            return lax.dot_general(av.astype(BF16), b_ref[...].astype(BF16), dims, preferred_element_type=F32)

        def finish(acc):
            if epi is None:
                outs[0][...] = acc.astype(outs[0].dtype)
            elif epi == "res":
                outs[0][...] = (e_refs[0][...] + acc).astype(outs[0].dtype)
            elif epi == "drelu2":
                outs[0][...] = (acc * (2.0 * jnp.maximum(e_refs[0][...].astype(F32), 0.0))).astype(outs[0].dtype)
            elif epi == "res_rms":
                x1 = e_refs[0][...] + acc
                outs[0][...] = x1
                outs[1][...] = _rms_apply(x1, g_ref[...]).astype(BF16)
            elif epi == "rms_bwd":
                dx, dg = _rms_grad(acc, e_refs[0][...], g_ref[...])
                outs[0][...] = e_refs[1][...] + dx
                i = pl.program_id(0)

                @pl.when(i == 0)
                def _():
                    outs[1][...] = dg

                @pl.when(i > 0)
                def _():
                    outs[1][...] += dg

        if nk == 1:
            finish(product())
            return
        acc_ref = refs[-1]
        k = pl.program_id(2)

        @pl.when(k == 0)
        def _():
            acc_ref[...] = product()

        @pl.when(jnp.logical_and(k > 0, k < nk - 1))
        def _():
            acc_ref[...] += product()

        @pl.when(k == nk - 1)
        def _():
            finish(acc_ref[...] + product())

    in_specs = ([a_spec, b_spec] + [o_spec] * n_extra + ([v_spec] if gain is not None else [])
                + ([pl.BlockSpec(memory_space=pl.ANY)] if after is not None else []))
    if epi == "res_rms":
        out_shape = (jax.ShapeDtypeStruct((m, n), F32), jax.ShapeDtypeStruct((m, n), BF16))
        out_specs = (o_spec, o_spec)
    elif epi == "rms_bwd":
        out_shape = (jax.ShapeDtypeStruct((m, n), F32), jax.ShapeDtypeStruct((1, n), F32))
        out_specs = (o_spec, v_spec)
    else:
        out_shape = jax.ShapeDtypeStruct((N_DEV, m, tn) if out_slots else (m, n), out_dtype)
        out_specs = o_spec
    args = (a, b) + extra + ((gain.reshape(1, n),) if gain is not None else ()) + ((after,) if after is not None else ())
    return pl.pallas_call(
        body, name=name, grid=(m // tm, n // tn, nk), in_specs=in_specs, out_specs=out_specs, out_shape=out_shape,
        scratch_shapes=[pltpu.VMEM((tm, tn), F32)] if nk > 1 else [],
        compiler_params=_cparams(("arbitrary", "arbitrary", "arbitrary")),
    )(*args)


SHIFT = HEAD_DIM


def _pair_blocks(e, o):
    wp = e.shape[-1]
    return e[:, :wp - LANES], e[:, wp - LANES:] + o[:, :LANES], o[:, LANES:]


def _mm_in_pairs(a, wg, name):
    t, kdim = a.shape
    wp = wg.shape[-1]
    ws = wp - SHIFT
    tm = _tile(t, 1024)

    def body(a_ref, e_ref, o_ref, u_ref):
        av = a_ref[...]
        lo, mid, hi = _pair_blocks(e_ref[...], o_ref[...])
        u_ref[:, :wp - LANES] = jnp.dot(av, lo, preferred_element_type=F32).astype(BF16)
        u_ref[:, wp - LANES:wp] = jnp.dot(av, mid, preferred_element_type=F32).astype(BF16)
        u_ref[:, wp:] = jnp.dot(av, hi, preferred_element_type=F32).astype(BF16)

    return pl.pallas_call(
        body, name=name, grid=(N_DEV // 2, t // tm),
        in_specs=[pl.BlockSpec((tm, kdim), lambda p, i: (i, 0)),
                  pl.BlockSpec((None, kdim, wp), lambda p, i: (2 * p, 0, 0)),
                  pl.BlockSpec((None, kdim, wp), lambda p, i: (2 * p + 1, 0, 0))],
        out_specs=pl.BlockSpec((tm, 2 * ws), lambda p, i: (i, p)),
        out_shape=jax.ShapeDtypeStruct((t, N_DEV * ws), BF16), compiler_params=_cparams(("arbitrary", "arbitrary")),
    )(a, wg, wg)


def _mm_din_pairs(du, wg, x, gain, dres, after, name):
    t = du.shape[0]
    _, kdim, wp = wg.shape
    ws = wp - SHIFT
    tm = _tile(t, 512)
    npair = N_DEV // 2
    lanes = (((1,), (1,)), ((), ()))
    extra = [] if after is None else [after]

    def body(d_ref, e_ref, o_ref, x_ref, g_ref, r_ref, *rest):
        dx_ref, dg_ref, acc_ref = rest[-3], rest[-2], rest[-1]
        i, p = pl.program_id(0), pl.program_id(1)
        lo, mid, hi = _pair_blocks(e_ref[...], o_ref[...])
        part = (lax.dot_general(d_ref[:, :wp - LANES], lo, lanes, preferred_element_type=F32)
                + lax.dot_general(d_ref[:, wp - LANES:wp], mid, lanes, preferred_element_type=F32)
                + lax.dot_general(d_ref[:, wp:], hi, lanes, preferred_element_type=F32))

        @pl.when(p == 0)
        def _():
            acc_ref[...] = part

        @pl.when(jnp.logical_and(p > 0, p < npair - 1))
        def _():
            acc_ref[...] += part

        @pl.when(p == npair - 1)
        def _():
            dx, dg = _rms_grad(acc_ref[...] + part, x_ref[...], g_ref[...])
            dx_ref[...] = r_ref[...] + dx

            @pl.when(i == 0)
            def _():
                dg_ref[...] = dg

            @pl.when(i > 0)
            def _():
                dg_ref[...] += dg

    row = pl.BlockSpec((tm, kdim), lambda i, p: (i, 0))
    vec = pl.BlockSpec((1, kdim), lambda i, p: (0, 0))
    return pl.pallas_call(
        body, name=name, grid=(t // tm, npair),
        in_specs=[pl.BlockSpec((tm, 2 * ws), lambda i, p: (i, p)),
                  pl.BlockSpec((None, kdim, wp), lambda i, p: (2 * p, 0, 0)),
                  pl.BlockSpec((None, kdim, wp), lambda i, p: (2 * p + 1, 0, 0)), row, vec, row]
        + [pl.BlockSpec(memory_space=pl.ANY)] * len(extra),
        out_specs=(row, vec),
        out_shape=(jax.ShapeDtypeStruct((t, kdim), F32), jax.ShapeDtypeStruct((1, kdim), F32)),
        scratch_shapes=[pltpu.VMEM((tm, kdim), F32)], compiler_params=_cparams(("arbitrary", "arbitrary")),
    )(du, wg, wg, x, gain.reshape(1, kdim), dres, *extra)


def _mm_gw_in_pairs(h, du, after, name):
    t, kdim = h.shape
    ws = du.shape[1] // N_DEV
    wp = ws + SHIFT
    tm = _tile(kdim, 512)
    rows = (((0,), (0,)), ((), ()))
    extra = [] if after is None else [after]

    def body(h_ref, d_ref, *rest):
        g_ref = rest[-1]
        g = lax.dot_general(h_ref[...], d_ref[...], rows, preferred_element_type=F32)
        g_ref[0] = g[:, :wp].astype(BF16)
        g_ref[1] = g[:, wp - LANES:].astype(BF16)

    return pl.pallas_call(
        body, name=name, grid=(N_DEV // 2, kdim // tm),
        in_specs=[pl.BlockSpec((t, tm), lambda p, i: (0, i)), pl.BlockSpec((t, 2 * ws), lambda p, i: (0, p))]
        + [pl.BlockSpec(memory_space=pl.ANY)] * len(extra),
        out_specs=pl.BlockSpec((2, tm, wp), lambda p, i: (p, i, 0)),
        out_shape=jax.ShapeDtypeStruct((N_DEV, kdim, wp), BF16), compiler_params=_cparams(("arbitrary", "arbitrary")),
    )(h, du, *extra)


ROW_BLOCK_BUDGET = 24 * 1024 * 1024


def _rows(t, row_bytes):
    rows = t
    while rows > 8 and (2 * rows * row_bytes > ROW_BLOCK_BUDGET or t % rows):
        rows //= 2
    return rows


def _rms_fwd(x, g, after, name):
    t, d = x.shape
    ROWS = _rows(t, 6 * d)

    def body(x_ref, g_ref, _, h_ref):
        h_ref[...] = _rms_apply(x_ref[...], g_ref[...]).astype(BF16)

    return pl.pallas_call(
        body, name=name, grid=(t // ROWS,),
        in_specs=[pl.BlockSpec((ROWS, d), lambda i: (i, 0)), pl.BlockSpec((1, d), lambda i: (0, 0)),
                  pl.BlockSpec(memory_space=pl.ANY)],
        out_specs=pl.BlockSpec((ROWS, d), lambda i: (i, 0)),
        out_shape=jax.ShapeDtypeStruct((t, d), BF16), compiler_params=_cparams(("arbitrary",)),
    )(x, g.reshape(1, d), after)


def _gated_out(zs, o, wc, wa, u, gate_col, name):
    t = zs.shape[0]
    d = wc.shape[1]
    td = math.gcd(_tile(d, 512), gate_col)
    nd = d // td
    c0 = gate_col // td
    tm = _tile(t, 1024)

    def body(zs_ref, o_ref, wc_ref, wa_ref, gc_ref, ga_ref, m_ref, yc_ref, ya_ref):
        yc = jnp.dot(zs_ref[...], wc_ref[...], preferred_element_type=F32)
        ya = jnp.dot(o_ref[...], wa_ref[...], preferred_element_type=F32)
        gc = _sigmoid(gc_ref[...].astype(F32))
        ga = _sigmoid(ga_ref[...].astype(F32))
        m_ref[...] = (gc * yc + ga * ya).astype(BF16)
        yc_ref[...] = yc.astype(BF16)
        ya_ref[...] = ya.astype(BF16)

    blk = pl.BlockSpec((tm, td), lambda i, j: (i, j))
    sh = jax.ShapeDtypeStruct((t, d), BF16)
    return pl.pallas_call(
        body, name=name, grid=(t // tm, nd),
        in_specs=[pl.BlockSpec((tm, zs.shape[1]), lambda i, j: (i, 0)), pl.BlockSpec((tm, o.shape[1]), lambda i, j: (i, 0)),
                  pl.BlockSpec((wc.shape[0], td), lambda i, j: (0, j)), pl.BlockSpec((wa.shape[0], td), lambda i, j: (0, j)),
                  pl.BlockSpec((tm, td), lambda i, j: (i, c0 + j)), pl.BlockSpec((tm, td), lambda i, j: (i, c0 + nd + j))],
        out_specs=(blk, blk, blk), out_shape=(sh, sh, sh), compiler_params=_cparams(("arbitrary", "arbitrary")),
    )(zs, o, wc, wa, u, u)


def _gate_bwd(dm, u, yc, ya, gate_col, name):
    t, d = yc.shape
    td = math.gcd(_tile(d, 512), gate_col)
    nd = d // td
    c0 = gate_col // td
    ROWS = _rows(t, 20 * td)

    def body(dm_ref, gc_ref, ga_ref, yc_ref, ya_ref, dyc_ref, dya_ref, dugc_ref, duga_ref):
        dmv = dm_ref[...]
        gc = _sigmoid(gc_ref[...].astype(F32))
        ga = _sigmoid(ga_ref[...].astype(F32))
        dyc_ref[...] = (dmv * gc).astype(BF16)
        dya_ref[...] = (dmv * ga).astype(BF16)
        dugc_ref[...] = (dmv * yc_ref[...].astype(F32) * gc * (1.0 - gc)).astype(BF16)
        duga_ref[...] = (dmv * ya_ref[...].astype(F32) * ga * (1.0 - ga)).astype(BF16)

    blk = pl.BlockSpec((ROWS, td), lambda i, j: (i, j))
    o = jax.ShapeDtypeStruct((t, d), BF16)
    return pl.pallas_call(
        body, name=name, grid=(t // ROWS, nd),
        in_specs=[blk, pl.BlockSpec((ROWS, td), lambda i, j: (i, c0 + j)),
                  pl.BlockSpec((ROWS, td), lambda i, j: (i, c0 + nd + j)), blk, blk],
        out_specs=(blk, blk, blk, blk), out_shape=(o, o, o, o),
        compiler_params=_cparams(("arbitrary", "arbitrary")),
    )(dm, u, u, yc, ya)


def _loss_and_grad(y, target, name):
    t, d = y.shape
    ROWS = _rows(t, 12 * d)
    n = t // ROWS

    def body(y_ref, t_ref, loss_ref, dy_ref, acc_ref):
        i = pl.program_id(0)

        @pl.when(i == 0)
        def _():
            acc_ref[...] = jnp.zeros_like(acc_ref)

        diff = y_ref[...] - t_ref[...]
        dy_ref[...] = diff * (1.0 / d)
        acc_ref[...] += jnp.sum(diff * diff, axis=0, keepdims=True)

        @pl.when(i == n - 1)
        def _():
            loss_ref[...] = jnp.sum(acc_ref[...], axis=-1, keepdims=True) * (0.5 / d)

    row = pl.BlockSpec((ROWS, d), lambda i: (i, 0))
    return pl.pallas_call(
        body, name=name, grid=(n,), in_specs=[row, row],
        out_specs=(pl.BlockSpec((1, 1), lambda i: (0, 0)), row),
        out_shape=(jax.ShapeDtypeStruct((1, 1), F32), jax.ShapeDtypeStruct((t, d), F32)),
        scratch_shapes=[pltpu.VMEM((1, d), F32)], compiler_params=_cparams(("arbitrary",)),
    )(y, target)


HALO = 32


def _conv_fwd(u, w, b, cdim, name):
    t = u.shape[0]
    ncb = cdim // LANES
    nt = t // BLOCK

    def body(a_ref, g_ref, w_ref, b_ref, zc_ref, zpad):
        zpad[0:HALO, :] = jnp.zeros((HALO, LANES), F32)
        zpad[HALO:HALO + t, :] = a_ref[...].astype(F32) * _sigmoid(g_ref[...].astype(F32))
        wv = w_ref[...]
        bv = b_ref[...]

        def tile(i, carry):
            r0 = pl.multiple_of(i * BLOCK, BLOCK)
            win = zpad[pl.ds(r0, BLOCK + HALO), :]
            acc = jnp.zeros((BLOCK, LANES), F32) + bv
            for b in range(SUBLANES):
                sh = win if b == 0 else pltpu.roll(win, b, 0)
                for a in range(HALO // SUBLANES):
                    j = CONV_WIDTH - 1 - (SUBLANES * a + b)
                    if j >= 0:
                        lo = HALO - SUBLANES * a
                        acc = acc + wv[j:j + 1, :] * sh[lo:lo + BLOCK, :]
            zc_ref[pl.ds(r0, BLOCK), :] = acc
            return carry

        lax.fori_loop(0, nt, tile, 0)

    col = lambda off: pl.BlockSpec((t, LANES), lambda c: (0, off + c))
    return pl.pallas_call(
        body, name=name, grid=(ncb,),
        in_specs=[col(0), col(ncb), pl.BlockSpec((CONV_TAPS_PADDED, LANES), lambda c: (0, c)),
                  pl.BlockSpec((1, LANES), lambda c: (0, c))],
        out_specs=pl.BlockSpec((t, LANES), lambda c: (0, c)),
        out_shape=jax.ShapeDtypeStruct((t, cdim), F32),
        scratch_shapes=[pltpu.VMEM((t + HALO, LANES), F32)], compiler_params=_cparams(("arbitrary",)),
    )(u, u, w, b.reshape(1, cdim))


def _conv_bwd(dzc, u, w, cdim, name):
    t = u.shape[0]
    ncb = cdim // LANES
    nt = t // BLOCK
    win_rows = BLOCK + HALO

    def body(dzc_ref, a_ref, g_ref, w_ref, da_ref, dg_ref, dw_ref, db_ref, zpad, dpad):
        av = a_ref[...].astype(F32)
        sg = _sigmoid(g_ref[...].astype(F32))
        zpad[0:HALO, :] = jnp.zeros((HALO, LANES), F32)
        zpad[HALO:HALO + t, :] = av * sg
        dpad[0:t, :] = dzc_ref[...]
        dpad[t:t + HALO, :] = jnp.zeros((HALO, LANES), F32)
        dw_ref[...] = jnp.zeros_like(dw_ref)
        db_ref[...] = jnp.sum(dzc_ref[...], axis=0, keepdims=True)
        wv = w_ref[...]

        def tile(i, carry):
            r0 = pl.multiple_of(i * BLOCK, BLOCK)
            zwin = zpad[pl.ds(r0, win_rows), :]
            dwin = dpad[pl.ds(r0, win_rows), :]
            dcur = dwin[0:BLOCK, :]
            dz = jnp.zeros((BLOCK, LANES), F32)
            for b in range(SUBLANES):
                zs = zwin if b == 0 else pltpu.roll(zwin, b, 0)
                ds = dwin if b == 0 else pltpu.roll(dwin, win_rows - b, 0)
                for a in range(HALO // SUBLANES):
                    j = CONV_WIDTH - 1 - (SUBLANES * a + b)
                    if j >= 0:
                        lo = HALO - SUBLANES * a
                        dw_ref[j:j + 1, :] += jnp.sum(dcur * zs[lo:lo + BLOCK, :], axis=0, keepdims=True)
                        dz = dz + wv[j:j + 1, :] * ds[SUBLANES * a:SUBLANES * a + BLOCK, :]
            ac = a_ref[pl.ds(r0, BLOCK), :].astype(F32)
            sc = _sigmoid(g_ref[pl.ds(r0, BLOCK), :].astype(F32))
            da_ref[pl.ds(r0, BLOCK), :] = (dz * sc).astype(BF16)
            dg_ref[pl.ds(r0, BLOCK), :] = (dz * ac * sc * (1.0 - sc)).astype(BF16)
            return carry

        lax.fori_loop(0, nt, tile, 0)

    col = lambda off: pl.BlockSpec((t, LANES), lambda c: (0, off + c))
    wspec = pl.BlockSpec((CONV_TAPS_PADDED, LANES), lambda c: (0, c))
    o = jax.ShapeDtypeStruct((t, cdim), BF16)
    return pl.pallas_call(
        body, name=name, grid=(ncb,), in_specs=[col(0), col(0), col(ncb), wspec],
        out_specs=(col(0), col(0), wspec, pl.BlockSpec((1, LANES), lambda c: (0, c))),
        out_shape=(o, o, jax.ShapeDtypeStruct((CONV_TAPS_PADDED, cdim), F32), jax.ShapeDtypeStruct((1, cdim), F32)),
        scratch_shapes=[pltpu.VMEM((t + HALO, LANES), F32), pltpu.VMEM((t + HALO, LANES), F32)],
        compiler_params=_cparams(("arbitrary",)),
    )(dzc, u, u, w)


def _ln_swish_fwd(zc, g, b, name):
    t, c = zc.shape
    ROWS = _rows(t, 6 * c)

    def body(z_ref, g_ref, b_ref, o_ref):
        z = z_ref[...]
        mu = jnp.mean(z, axis=-1, keepdims=True)
        zc_ = z - mu
        zn = zc_ * lax.rsqrt(jnp.mean(zc_ * zc_, axis=-1, keepdims=True) + EPS)
        y = zn * g_ref[...] + b_ref[...]
        o_ref[...] = (y * _sigmoid(y)).astype(BF16)

    row = pl.BlockSpec((ROWS, c), lambda i: (i, 0))
    vec = pl.BlockSpec((1, c), lambda i: (0, 0))
    return pl.pallas_call(
        body, name=name, grid=(t // ROWS,), in_specs=[row, vec, vec], out_specs=row,
        out_shape=jax.ShapeDtypeStruct((t, c), BF16), compiler_params=_cparams(("arbitrary",)),
    )(zc, g.reshape(1, c), b.reshape(1, c))


def _ln_swish_bwd(dzs, zc, g, b, name):
    t, c = zc.shape
    ROWS = _rows(t, 12 * c)

    def body(d_ref, z_ref, g_ref, b_ref, dz_ref, dg_ref, db_ref):
        @pl.when(pl.program_id(0) == 0)
        def _():
            dg_ref[...] = jnp.zeros_like(dg_ref)
            db_ref[...] = jnp.zeros_like(db_ref)

        z = z_ref[...]
        mu = jnp.mean(z, axis=-1, keepdims=True)
        zc_ = z - mu
        rstd = lax.rsqrt(jnp.mean(zc_ * zc_, axis=-1, keepdims=True) + EPS)
        zn = zc_ * rstd
        y = zn * g_ref[...] + b_ref[...]
        sg = _sigmoid(y)
        dy = d_ref[...] * (sg * (1.0 + y * (1.0 - sg)))
        dg_ref[...] += jnp.sum(dy * zn, axis=0, keepdims=True)
        db_ref[...] += jnp.sum(dy, axis=0, keepdims=True)
        dzn = dy * g_ref[...]
        dz_ref[...] = rstd * (dzn - jnp.mean(dzn, axis=-1, keepdims=True)
                              - zn * jnp.mean(dzn * zn, axis=-1, keepdims=True))

    row = pl.BlockSpec((ROWS, c), lambda i: (i, 0))
    vec = pl.BlockSpec((1, c), lambda i: (0, 0))
    v = jax.ShapeDtypeStruct((1, c), F32)
    return pl.pallas_call(
        body, name=name, grid=(t // ROWS,), in_specs=[row, row, vec, vec], out_specs=(row, vec, vec),
        out_shape=(jax.ShapeDtypeStruct((t, c), F32), v, v), compiler_params=_cparams(("arbitrary",)),
    )(dzs, zc, g.reshape(1, c), b.reshape(1, c))


def _bucket_table():
    qi = np.arange(BLOCK)[:, None]
    kj = np.arange(2 * BLOCK)[None, :]
    off = qi + BLOCK - kj
    band = (off >= 0) & (off <= SUB_WINDOW)
    max_exact = NUM_BUCKETS // 2
    out = []
    for d in DILATIONS:
        dist = (np.clip(off, 0, SUB_WINDOW) * d).astype(np.int32)
        nf = np.maximum(dist, 1).astype(np.float32)
        large = max_exact + (np.log(nf / np.float32(max_exact)) / np.float32(math.log(MAX_REL_DISTANCE / max_exact))
                             * np.float32(NUM_BUCKETS - max_exact)).astype(np.int32)
        large = np.minimum(large, NUM_BUCKETS - 1)
        bucket = np.where(dist < max_exact, dist, large)
        out.append(np.where(band, bucket, -1))
    return np.stack(out).astype(np.int32)


def _bias_expand(rel_bias, buckets, hpg, after, name):
    nh = N_GROUPS * hpg

    def body(rb_ref, bk_ref, _, o_ref):
        h = pl.program_id(0)
        bk = bk_ref[0]
        acc = jnp.full((BLOCK, 2 * BLOCK), NEG_INF, F32)
        for bb in range(NUM_BUCKETS):
            acc = jnp.where(bk == bb, rb_ref[bb, h], acc)
        o_ref[0] = acc

    return pl.pallas_call(
        body, name=name, grid=(nh,),
        in_specs=[pl.BlockSpec(memory_space=pltpu.SMEM),
                  pl.BlockSpec((1, BLOCK, 2 * BLOCK), lambda h: (h // hpg, 0, 0)), pl.BlockSpec(memory_space=pl.ANY)],
        out_specs=pl.BlockSpec((1, BLOCK, 2 * BLOCK), lambda h: (h, 0, 0)),
        out_shape=jax.ShapeDtypeStruct((nh, BLOCK, 2 * BLOCK), F32), compiler_params=_cparams(("arbitrary",)),
    )(rel_bias, buckets, after)


def _bias_reduce(ds_sum, buckets, hpg, name):
    nh = N_GROUPS * hpg

    def body(ds_ref, bk_ref, o_ref):
        bk = bk_ref[0]
        dsv = ds_ref[0]
        lane = lax.broadcasted_iota(jnp.int32, (1, LANES), 1)
        row = jnp.zeros((1, LANES), F32)
        for bb in range(NUM_BUCKETS):
            tot = jnp.sum(jnp.sum(jnp.where(bk == bb, dsv, 0.0), axis=-1, keepdims=True), axis=0, keepdims=True)
            row = jnp.where(lane == bb, tot, row)
        o_ref[0] = row

    return pl.pallas_call(
        body, name=name, grid=(nh,),
        in_specs=[pl.BlockSpec((1, BLOCK, 2 * BLOCK), lambda h: (h, 0, 0)),
                  pl.BlockSpec((1, BLOCK, 2 * BLOCK), lambda h: (h // hpg, 0, 0))],
        out_specs=pl.BlockSpec((1, 1, LANES), lambda h: (h, 0, 0)),
        out_shape=jax.ShapeDtypeStruct((nh, 1, LANES), F32), compiler_params=_cparams(("arbitrary",)),
    )(ds_sum, buckets)


def _chunk_rows(c, d, nb):
    r, n = c // nb, c % nb
    if d == 1:
        return pl.ds(c * BLOCK, BLOCK)
    return pl.ds(r + n * BLOCK * d, BLOCK, stride=d)


def _segment_ones():
    i = lax.broadcasted_iota(jnp.int32, (LANES, LANES), 0) // HEAD_DIM
    j = lax.broadcasted_iota(jnp.int32, (LANES, LANES), 1) // HEAD_DIM
    return (i == j).astype(BF16)


def _segment_sum(v, seg):
    hi = v.astype(BF16)
    lo = (v - hi.astype(F32)).astype(BF16)
    return jnp.dot(hi, seg, preferred_element_type=F32) + jnp.dot(lo, seg, preferred_element_type=F32)


def _head_mean(v, seg):
    return _segment_sum(v, seg) * (1.0 / HEAD_DIM)


def _attn_fwd(u, qg2, kg2, bias, gi, cols, hpg, name):
    t = u.shape[0]
    d = DILATIONS[gi]
    nchunk = t // BLOCK
    nb = (t // d) // BLOCK
    hp = hpg // 2
    qc0, kc0, vc0 = [(c + gi * hpg * HEAD_DIM) // LANES for c in cols]
    contract_lanes = (((1,), (1,)), ((), ()))

    def body(q_ref, k_ref, v_ref, qg_ref, kg_ref, bias_ref, o_ref, lse_ref, qd, kd, vd, od, ld, sbuf):
        seg = _segment_ones()
        lane = lax.broadcasted_iota(jnp.int32, (1, LANES), 1)
        qg = qg_ref[...] * (HEAD_DIM ** -0.5)
        kg = kg_ref[...]
        kd[0:BLOCK, :] = jnp.zeros((BLOCK, LANES), BF16)
        vd[0:BLOCK, :] = jnp.zeros((BLOCK, LANES), BF16)
        od[...] = q_ref[...].astype(F32)
        ld[...] = k_ref[...].astype(F32)
        for c in range(nchunk):
            rows = _chunk_rows(c, d, nb)
            qv = od[rows, :]
            kv = ld[rows, :]
            qd[c * BLOCK:(c + 1) * BLOCK, :] = (qv * lax.rsqrt(_head_mean(qv * qv, seg) + EPS) * qg).astype(BF16)
            kd[(c + 1) * BLOCK:(c + 2) * BLOCK, :] = (kv * lax.rsqrt(_head_mean(kv * kv, seg) + EPS) * kg).astype(BF16)
        od[...] = v_ref[...].astype(F32)
        for c in range(nchunk):
            vd[(c + 1) * BLOCK:(c + 2) * BLOCK, :] = od[_chunk_rows(c, d, nb), :].astype(BF16)

        col = lax.broadcasted_iota(jnp.int32, (BLOCK, 2 * BLOCK), 1)
        for j in range(2):
            mj = jnp.logical_and(lane >= j * HEAD_DIM, lane < (j + 1) * HEAD_DIM)
            for c in range(nchunk):
                kw = kd[c * BLOCK:(c + 2) * BLOCK, :]
                kj = jnp.where(mj, kw, jnp.zeros_like(kw))
                s = lax.dot_general(qd[c * BLOCK:(c + 1) * BLOCK, :], kj, contract_lanes,
                                    preferred_element_type=F32) + bias_ref[j]
                if c % nb == 0:
                    s = jnp.where(col < BLOCK, NEG_INF, s)
                sbuf[c] = s
            for c in range(nchunk):
                rows = slice(c * BLOCK, (c + 1) * BLOCK)
                s = sbuf[c]
                mx = jnp.max(s, axis=-1, keepdims=True)
                p = jnp.exp(s - mx).astype(BF16)
                vw = vd[c * BLOCK:(c + 2) * BLOCK, :]
                oj = jnp.dot(p, jnp.where(mj, vw, jnp.ones_like(vw)), preferred_element_type=F32)
                l = pltpu.roll(oj, HEAD_DIM, 1)
                on = oj / l
                ls = mx + jnp.log(l)
                if j == 0:
                    od[rows, :] = on
                    ld[rows, :] = ls
                else:
                    od[rows, :] = jnp.where(mj, on, od[rows, :])
                    ld[rows, :] = jnp.where(mj, ls, ld[rows, :])

        for c in range(nchunk):
            rows = _chunk_rows(c, d, nb)
            o_ref[rows, :] = od[c * BLOCK:(c + 1) * BLOCK, :]
            lse_ref[rows, :] = ld[c * BLOCK:(c + 1) * BLOCK, :]

    ucol = lambda c0: pl.BlockSpec((t, LANES), lambda h: (0, c0 + h))
    vec = pl.BlockSpec((1, LANES), lambda h: (0, 0))
    oblk = pl.BlockSpec((t, LANES), lambda h: (0, h))
    osh = jax.ShapeDtypeStruct((t, hpg * HEAD_DIM), F32)
    return pl.pallas_call(
        body, name=name, grid=(hp,),
        in_specs=[ucol(qc0), ucol(kc0), ucol(vc0), vec, vec,
                  pl.BlockSpec((2, BLOCK, 2 * BLOCK), lambda h: (gi * hp + h, 0, 0))],
        out_specs=(oblk, oblk), out_shape=(osh, osh),
        scratch_shapes=[pltpu.VMEM((t, LANES), BF16), pltpu.VMEM((t + BLOCK, LANES), BF16),
                        pltpu.VMEM((t + BLOCK, LANES), BF16), pltpu.VMEM((t, LANES), F32), pltpu.VMEM((t, LANES), F32),
                        pltpu.VMEM((nchunk, BLOCK, 2 * BLOCK), F32)],
        compiler_params=_cparams(("arbitrary",)),
    )(u, u, u, qg2, kg2, bias)


def _attn_bwd(u, do_g, dd_g, lse_g, qg2, kg2, bias, ds_in, du_in, gi, cols, hpg, name):
    t = u.shape[0]
    d = DILATIONS[gi]
    nchunk = t // BLOCK
    nb = (t // d) // BLOCK
    hp = hpg // 2
    qc0, kc0, vc0 = [(c + gi * hpg * HEAD_DIM) // LANES for c in cols]
    contract_lanes = (((1,), (1,)), ((), ()))
    contract_rows = (((0,), (0,)), ((), ()))
    qscale = HEAD_DIM ** -0.5

    def body(q_ref, k_ref, v_ref, do_ref, dd_ref, lse_ref, qg_ref, kg_ref, bias_ref, dsin_ref, _du_in,
             dgq_ref, dgk_ref, dsout_ref, du_ref,
             qd, kd, vd, dod, ddd, ld, dqd, dkd, dvd, dsacc, pbuf, dsbuf, qs, ks, qst, kst, vst, out_sems):
        h = pl.program_id(0)

        def flush(step):
            return [pltpu.make_async_copy(
                st, du_ref.at[:, pl.ds(pl.multiple_of((c0 + step) * LANES, LANES), LANES)], out_sems.at[i])
                for i, (st, c0) in enumerate(((qst, qc0), (kst, kc0), (vst, vc0)))]

        seg = _segment_ones()
        lane = lax.broadcasted_iota(jnp.int32, (1, LANES), 1)
        qg = qg_ref[...] * qscale
        kg = kg_ref[...]
        kd[0:BLOCK, :] = jnp.zeros((BLOCK, LANES), BF16)
        vd[0:BLOCK, :] = jnp.zeros((BLOCK, LANES), BF16)
        dsacc[...] = jnp.zeros_like(dsacc)
        qs[...] = q_ref[...].astype(F32)
        ks[...] = k_ref[...].astype(F32)
        dqd[...] = v_ref[...].astype(F32)
        for c in range(nchunk):
            rows = _chunk_rows(c, d, nb)
            qv = qs[rows, :]
            kv = ks[rows, :]
            qd[c * BLOCK:(c + 1) * BLOCK, :] = (qv * lax.rsqrt(_head_mean(qv * qv, seg) + EPS) * qg).astype(BF16)
            kd[(c + 1) * BLOCK:(c + 2) * BLOCK, :] = (kv * lax.rsqrt(_head_mean(kv * kv, seg) + EPS) * kg).astype(BF16)
            vd[(c + 1) * BLOCK:(c + 2) * BLOCK, :] = dqd[rows, :].astype(BF16)
            dod[c * BLOCK:(c + 1) * BLOCK, :] = do_ref[rows, :].astype(BF16)
            ddd[c * BLOCK:(c + 1) * BLOCK, :] = dd_ref[rows, :]
            ld[c * BLOCK:(c + 1) * BLOCK, :] = lse_ref[rows, :]

        col = lax.broadcasted_iota(jnp.int32, (BLOCK, 2 * BLOCK), 1)
        for j in range(2):
            mj = jnp.logical_and(lane >= j * HEAD_DIM, lane < (j + 1) * HEAD_DIM)
            first = lane == j * HEAD_DIM
            for c in range(nchunk):
                rows = slice(c * BLOCK, (c + 1) * BLOCK)
                kw = kd[c * BLOCK:(c + 2) * BLOCK, :]
                vw = vd[c * BLOCK:(c + 2) * BLOCK, :]
                kj = jnp.where(mj, kw, jnp.zeros_like(kw))
                vj = jnp.where(mj, vw, jnp.zeros_like(vw))
                s = lax.dot_general(qd[rows, :], kj, contract_lanes, preferred_element_type=F32) + bias_ref[j]
                if c % nb == 0:
                    s = jnp.where(col < BLOCK, NEG_INF, s)
                dp = lax.dot_general(dod[rows, :], vj, contract_lanes, preferred_element_type=F32)
                lse_j = jnp.sum(jnp.where(first, ld[rows, :], 0.0), axis=-1, keepdims=True)
                dd_j = jnp.sum(jnp.where(first, ddd[rows, :], 0.0), axis=-1, keepdims=True)
                p = jnp.exp(s - lse_j)
                ds = p * (dp + dd_j)
                dsacc[j] += ds
                pbuf[j, c] = p.astype(BF16)
                dsbuf[j, c] = ds.astype(BF16)
        for c in range(nchunk):
            rows = slice(c * BLOCK, (c + 1) * BLOCK)
            has_next = c + 1 < nchunk and (c + 1) % nb != 0
            kw = kd[c * BLOCK:(c + 2) * BLOCK, :]
            dq = jnp.zeros((BLOCK, LANES), F32)
            dk = jnp.zeros((BLOCK, LANES), F32)
            dv = jnp.zeros((BLOCK, LANES), F32)
            both = slice(c * BLOCK, (c + 2) * BLOCK) if has_next else rows
            for j in range(2):
                mj = jnp.logical_and(lane >= j * HEAD_DIM, lane < (j + 1) * HEAD_DIM)
                dq = dq + jnp.dot(dsbuf[j, c], jnp.where(mj, kw, jnp.zeros_like(kw)), preferred_element_type=F32)
                dsk = dsbuf[j, c, :, BLOCK:]
                pk = pbuf[j, c, :, BLOCK:]
                if has_next:
                    dsk = jnp.concatenate([dsk, dsbuf[j, c + 1, :, :BLOCK]], axis=0)
                    pk = jnp.concatenate([pk, pbuf[j, c + 1, :, :BLOCK]], axis=0)
                qq = qd[both, :]
                dd = dod[both, :]
                dk = dk + lax.dot_general(dsk, jnp.where(mj, qq, jnp.zeros_like(qq)), contract_rows,
                                          preferred_element_type=F32)
                dv = dv + lax.dot_general(pk, jnp.where(mj, dd, jnp.zeros_like(dd)), contract_rows,
                                          preferred_element_type=F32)
            dqd[rows, :] = dq
            dkd[rows, :] = dk
            dvd[rows, :] = dv

        dsout_ref[...] = dsin_ref[...] + dsacc[...]

        dgq = jnp.zeros((1, LANES), F32)
        dgk = jnp.zeros((1, LANES), F32)
        for c in range(nchunk):
            rows = _chunk_rows(c, d, nb)
            qv = qs[rows, :]
            rq = lax.rsqrt(_head_mean(qv * qv, seg) + EPS)
            qh = qv * rq
            dy = dqd[c * BLOCK:(c + 1) * BLOCK, :]
            dgq = dgq + jnp.sum(dy * qh, axis=0, keepdims=True) * qscale
            dxh = dy * qg
            ddd[rows, :] = rq * (dxh - qh * _head_mean(dxh * qh, seg))
            kv = ks[rows, :]
            rk = lax.rsqrt(_head_mean(kv * kv, seg) + EPS)
            kh = kv * rk
            dy = dkd[c * BLOCK:(c + 1) * BLOCK, :]
            dgk = dgk + jnp.sum(dy * kh, axis=0, keepdims=True)
            dxh = dy * kg
            ld[rows, :] = rk * (dxh - kh * _head_mean(dxh * kh, seg))
        @pl.when(h > 0)
        def _():
            for cp in flush(h - 1):
                cp.wait()

        qst[...] = ddd[...].astype(BF16)
        kst[...] = ld[...].astype(BF16)
        for c in range(nchunk):
            ddd[_chunk_rows(c, d, nb), :] = dvd[c * BLOCK:(c + 1) * BLOCK, :]
        vst[...] = ddd[...].astype(BF16)
        for cp in flush(h):
            cp.start()

        @pl.when(h == hp - 1)
        def _():
            for cp in flush(h):
                cp.wait()

        dgq_ref[0] = dgq
        dgk_ref[0] = dgk

    ucol = lambda c0: pl.BlockSpec((t, LANES), lambda h: (0, c0 + h))
    vec = pl.BlockSpec((1, LANES), lambda h: (0, 0))
    oblk = pl.BlockSpec((t, LANES), lambda h: (0, h))
    bblk = pl.BlockSpec((2, BLOCK, 2 * BLOCK), lambda h: (gi * hp + h, 0, 0))
    gblk = pl.BlockSpec((1, 1, LANES), lambda h: (h, 0, 0))
    gsh = jax.ShapeDtypeStruct((hp, 1, LANES), F32)
    hbm = pl.BlockSpec(memory_space=pl.ANY)
    return pl.pallas_call(
        body, name=name, grid=(hp,),
        in_specs=[ucol(qc0), ucol(kc0), ucol(vc0), oblk, oblk, oblk, vec, vec, bblk, bblk, hbm],
        out_specs=(gblk, gblk, bblk, hbm),
        out_shape=(gsh, gsh, jax.ShapeDtypeStruct(ds_in.shape, F32), jax.ShapeDtypeStruct(du_in.shape, BF16)),
        input_output_aliases={9: 2, 10: 3},
        scratch_shapes=[pltpu.VMEM((t, LANES), BF16), pltpu.VMEM((t + BLOCK, LANES), BF16),
                        pltpu.VMEM((t + BLOCK, LANES), BF16), pltpu.VMEM((t, LANES), BF16),
                        pltpu.VMEM((t, LANES), F32), pltpu.VMEM((t, LANES), F32), pltpu.VMEM((t, LANES), F32),
                        pltpu.VMEM((t, LANES), F32), pltpu.VMEM((t, LANES), F32),
                        pltpu.VMEM((2, BLOCK, 2 * BLOCK), F32), pltpu.VMEM((2, nchunk, BLOCK, 2 * BLOCK), BF16),
                        pltpu.VMEM((2, nchunk, BLOCK, 2 * BLOCK), BF16), pltpu.VMEM((t, LANES), F32),
                        pltpu.VMEM((t, LANES), F32), pltpu.VMEM((t, LANES), BF16), pltpu.VMEM((t, LANES), BF16),
                        pltpu.VMEM((t, LANES), BF16), pltpu.SemaphoreType.DMA((3,))],
        compiler_params=_cparams(("arbitrary",)),
    )(u, u, u, do_g, dd_g, lse_g, qg2, kg2, bias, ds_in, du_in)


def _group_weights(l0, l1, l2):
    mx = jnp.maximum(jnp.maximum(l0, l1), l2)
    e0, e1, e2 = jnp.exp(l0 - mx), jnp.exp(l1 - mx), jnp.exp(l2 - mx)
    inv = 1.0 / (e0 + e1 + e2)
    return e0 * inv, e1 * inv, e2 * inv


def _combine_fwd(os_, lses, name):
    t, ao = os_[0].shape
    ROWS = _rows(t, 26 * ao)

    def body(o0, o1, o2, l0, l1, l2, o_ref):
        w0, w1, w2 = _group_weights(l0[...], l1[...], l2[...])
        o_ref[...] = (w0 * o0[...] + w1 * o1[...] + w2 * o2[...]).astype(BF16)

    row = pl.BlockSpec((ROWS, ao), lambda i: (i, 0))
    return pl.pallas_call(
        body, name=name, grid=(t // ROWS,), in_specs=[row] * 6, out_specs=row,
        out_shape=jax.ShapeDtypeStruct((t, ao), BF16), compiler_params=_cparams(("arbitrary",)),
    )(*os_, *lses)


def _combine_bwd(do, os_, lses, name):
    t, ao = do.shape
    idx = np.arange(ao) // HEAD_DIM
    seg = jnp.asarray((idx[:, None] == idx[None, :]).astype(np.float32), dtype=BF16)
    ROWS = _rows(t, 52 * ao)

    def body(do_ref, o0, o1, o2, l0, l1, l2, seg_ref, g0, g1, g2, d0, d1, d2):
        w0, w1, w2 = _group_weights(l0[...], l1[...], l2[...])
        dov = do_ref[...]
        o = w0 * o0[...] + w1 * o1[...] + w2 * o2[...]
        sd = _segment_sum(dov * o, seg_ref[...])
        for w, gref, dref in ((w0, g0, d0), (w1, g1, d1), (w2, g2, d2)):
            gref[...] = w * dov
            dref[...] = -(w * sd)

    row = pl.BlockSpec((ROWS, ao), lambda i: (i, 0))
    sh = jax.ShapeDtypeStruct((t, ao), F32)
    outs = pl.pallas_call(
        body, name=name, grid=(t // ROWS,), in_specs=[row] * 7 + [pl.BlockSpec((ao, ao), lambda i: (0, 0))],
        out_specs=(row,) * 6, out_shape=(sh,) * 6, compiler_params=_cparams(("arbitrary",)),
    )(do, *os_, *lses, seg)
    return outs[:3], outs[3:]


def _adamw(w, g, m, v, name):
    shape = w.shape
    cols = shape[-1]
    rows = int(np.prod(shape[:-1]))
    tr = rows if rows <= 512 else _tile_rows(rows)
    c1 = 1.0 - ADAM_B1 ** ADAM_STEP
    c2 = 1.0 - ADAM_B2 ** ADAM_STEP

    def body(w_ref, g_ref, m_ref, v_ref, d_ref, nm_ref, nv_ref):
        gv = g_ref[...]
        mn = ADAM_B1 * m_ref[...] + (1.0 - ADAM_B1) * gv
        vn = ADAM_B2 * v_ref[...] + (1.0 - ADAM_B2) * (gv * gv)
        nm_ref[...] = mn
        nv_ref[...] = vn
        d_ref[...] = -ADAM_LR * ((mn / c1) / (jnp.sqrt(vn / c2) + ADAM_EPS) + ADAM_WD * w_ref[...])

    blk = pl.BlockSpec((tr, cols), lambda i: (i, 0))
    sh = jax.ShapeDtypeStruct((rows, cols), F32)
    outs = pl.pallas_call(
        body, name=name, grid=(rows // tr,), in_specs=[blk] * 4, out_specs=(blk,) * 3, out_shape=(sh,) * 3,
        compiler_params=_cparams(("arbitrary",)),
    )(*[a.reshape(rows, cols) for a in (w, g, m, v)])
    return tuple(o.reshape(shape) for o in outs)


def _tile_rows(rows):
    for t in (512, 256, 128, 64, 32, 16, 8):
        if rows % t == 0:
            return t
    return rows


def _sum_slots(recv, parts, me, layers, l, name):
    _, rows, cols = recv.shape
    tr = rows if rows <= 512 else _tile_rows(rows)

    def body(me_ref, r_ref, own_ref, _, o_ref):
        acc = jnp.zeros(o_ref.shape, F32)
        for s in range(N_DEV):
            acc = acc + jnp.where(me_ref[0] == s, own_ref[...], r_ref[s]).astype(F32)
        o_ref[...] = acc

    return pl.pallas_call(
        body, name=name,
        grid_spec=pltpu.PrefetchScalarGridSpec(
            num_scalar_prefetch=1, grid=(rows // tr,),
            in_specs=[pl.BlockSpec((N_DEV, tr, cols), lambda i, me: (0, i, 0)),
                      pl.BlockSpec((None, tr, cols), lambda i, me: (me[0], i, 0)),
                      pl.BlockSpec(memory_space=pl.ANY)],
            out_specs=pl.BlockSpec((None, tr, cols), lambda i, me: (l, i, 0))),
        out_shape=jax.ShapeDtypeStruct(layers.shape, F32), input_output_aliases={3: 0},
        compiler_params=_cparams(("arbitrary",)),
    )(me.reshape(1), recv, parts, layers)


def _peer(k):
    x, y, c = lax.axis_index("x"), lax.axis_index("y"), lax.axis_index("c")
    return (1 - x if k & 4 else x, 1 - y if k & 2 else y, 1 - c if k & 1 else c)


def _dev_index(p):
    return 4 * p[0] + 2 * p[1] + p[2]


HBM_SPEC = pl.BlockSpec(memory_space=pltpu.HBM)
SEM_SPEC = pl.BlockSpec(memory_space=pltpu.SEMAPHORE)
ANY_SPEC = pl.BlockSpec(memory_space=pl.ANY)
CHIPS = (4, 2, 6)


def _remote(src, dst, send_sem, recv_sem, to):
    return pltpu.make_async_remote_copy(src_ref=src, dst_ref=dst, send_sem=send_sem, recv_sem=recv_sem,
                                        device_id=to, device_id_type=MESH)


def _hbm(a):
    return pltpu.with_memory_space_constraint(a, pltpu.HBM)


def _split_call(body, name, bufs, sems_in, sem_out_sizes, after):
    nb, ns, no = len(bufs), len(sems_in), len(sem_out_sizes)
    extra = [] if after is None else list(after) if isinstance(after, (tuple, list)) else [after]

    def kern(*refs):
        pos = nb + ns + len(extra)
        body(refs[:nb], refs[nb:nb + ns], refs[pos:pos + no])
        token_ref = refs[pos + no + nb]
        token_ref[...] = jnp.zeros_like(token_ref)

    out_shape = (tuple(pltpu.SemaphoreType.DMA((s,)) for s in sem_out_sizes)
                 + tuple(pltpu.HBM(b.shape, b.dtype) for b in bufs) + (jax.ShapeDtypeStruct((8, LANES), F32),))
    res = pl.pallas_call(
        kern, name=name, out_shape=out_shape,
        in_specs=[HBM_SPEC] * nb + [SEM_SPEC] * ns + [ANY_SPEC] * len(extra),
        out_specs=(SEM_SPEC,) * no + (HBM_SPEC,) * nb + (pl.BlockSpec(memory_space=pltpu.VMEM),),
        input_output_aliases={i: no + i for i in range(nb)},
        compiler_params=pltpu.CompilerParams(has_side_effects=pltpu.SideEffectType.DATAFLOW_SIDE_EFFECTING),
    )(*bufs, *sems_in, *extra)
    return res[:no], res[no:no + nb], res[no + nb]


def _gather_start(shards, lands, after, name):
    n = len(shards)

    def body(bufs, _, sems):
        ins, lnd = bufs[:n], bufs[n:]
        d2d_s, d2d_r, ici_s, ici_r = sems
        me = _dev_index(_peer(0))
        for j, k in enumerate(CHIPS):
            for i in range(n):
                _remote(ins[i], lnd[i].at[me], ici_s.at[j], ici_r.at[j], _peer(k)).start()
        for i in range(n):
            _remote(ins[i], lnd[i].at[me], d2d_s.at[0], d2d_r.at[0], _peer(1)).start()

    return _split_call(body, name, [_hbm(a) for a in (*shards, *lands)], [], (1, 1, 3, 3), after)


def _gather_forward(n, bufs, ici_r, after, name):
    def body(refs, sems_in, sems):
        ins, lnd = refs[:n], refs[n:]
        (arrived,) = sems_in
        fwd_s, fwd_r = sems
        for j, k in enumerate(CHIPS):
            blk = _dev_index(_peer(k))
            for i in range(n):
                _remote(ins[i], lnd[i].at[blk], fwd_s.at[j], arrived.at[j], _peer(k)).wait_recv()
            for i in range(n):
                _remote(lnd[i].at[blk], lnd[i].at[blk], fwd_s.at[j], fwd_r.at[j], _peer(1)).start()

    return _split_call(body, name, bufs, [ici_r], (3, 3), after)


def _gather_finish(n, bufs, d2d_s, d2d_r, ici_s, fwd_s, fwd_r, after, name):
    def body(refs, sems_in, _):
        ins, lnd = refs[:n], refs[n:]
        d2d_send, d2d_recv, ici_send, fwd_send, fwd_recv = sems_in
        sib = _peer(1)
        for i in range(n):
            cp = _remote(ins[i], lnd[i].at[_dev_index(sib)], d2d_send.at[0], d2d_recv.at[0], sib)
            cp.wait_send()
            cp.wait_recv()
        for j, k in enumerate(CHIPS):
            passed = _dev_index(_peer(k))
            landed = _dev_index(_peer(k | 1))
            for i in range(n):
                _remote(ins[i], lnd[i].at[passed], ici_send.at[j], fwd_recv.at[j], _peer(k)).wait_send()
                cp = _remote(lnd[i].at[passed], lnd[i].at[landed], fwd_send.at[j], fwd_recv.at[j], sib)
                cp.wait_send()
                cp.wait_recv()

    _, out, token = _split_call(body, name, bufs, [d2d_s, d2d_r, ici_s, fwd_s, fwd_r], (), after)
    return out[n:], token


def _exchange_start(parts, lands, after, name):
    n = len(parts)

    def body(bufs, _, sems):
        src, lnd = bufs[:n], bufs[n:]
        send, recv = sems
        me = _dev_index(_peer(0))
        for k in (4, 5, 2, 3, 6, 7, 1):
            to = _peer(k)
            for i in range(n):
                _remote(src[i].at[_dev_index(to)], lnd[i].at[me], send.at[k - 1], recv.at[k - 1], to).start()

    return _split_call(body, name, [_hbm(a) for a in (*parts, *lands)], [], (7, 7), after)


def _exchange_finish(n, bufs, send, recv, after, name):
    def body(refs, sems_in, _):
        src, lnd = refs[:n], refs[n:]
        send_, recv_ = sems_in
        me = _dev_index(_peer(0))
        for k in range(1, N_DEV):
            frm = _peer(k)
            for i in range(n):
                cp = _remote(src[i].at[me], lnd[i].at[_dev_index(frm)], send_.at[k - 1], recv_.at[k - 1], frm)
                cp.wait_send()
                cp.wait_recv()

    _, out, token = _split_call(body, name, bufs, [send, recv], (), after)
    return out[:n], out[n:], token


def _all_reduce_small(v, after, name):
    rows = v.shape[0]

    def body(v_ref, _, o_ref, buf, send_sems, recv_sems):
        me = _dev_index(_peer(0))
        buf[me] = v_ref[...]
        copies = []
        for k in range(1, N_DEV):
            copies.append(pltpu.make_async_remote_copy(
                src_ref=v_ref, dst_ref=buf.at[me], send_sem=send_sems.at[k - 1], recv_sem=recv_sems.at[k - 1],
                device_id=_peer(k), device_id_type=MESH))
        for cp in copies:
            cp.start()
        for k in range(1, N_DEV):
            pltpu.make_async_remote_copy(
                src_ref=v_ref, dst_ref=buf.at[_dev_index(_peer(k))], send_sem=send_sems.at[k - 1],
                recv_sem=recv_sems.at[k - 1], device_id=_peer(k), device_id_type=MESH).wait_recv()
        for cp in copies:
            cp.wait_send()
        acc = buf[0]
        for s in range(1, N_DEV):
            acc = acc + buf[s]
        o_ref[...] = acc

    vm = pl.BlockSpec(memory_space=pltpu.VMEM)
    return pl.pallas_call(
        body, name=name, in_specs=[vm, pl.BlockSpec(memory_space=pl.ANY)], out_specs=vm,
        out_shape=jax.ShapeDtypeStruct(v.shape, F32),
        scratch_shapes=[pltpu.VMEM((N_DEV, rows, LANES), F32), pltpu.SemaphoreType.DMA((7,)),
                        pltpu.SemaphoreType.DMA((7,))],
    )(v, after)


def _columns(cdim, ao):
    q_col = 2 * cdim
    attn_dim = N_GROUPS * ao
    return (q_col, q_col + attn_dim, q_col + 2 * attn_dim), q_col + 3 * attn_dim


def _layer_fwd(x, h1, sm, bg, get_rest, bias, hpg, next_gain, at_ff1=None):
    cdim = sm["conv_ln_g"].shape[0]
    ao = hpg * HEAD_DIM
    cols, gate_col = _columns(cdim, ao)
    qg2 = jnp.tile(sm["q_norm_g"], 2).reshape(1, LANES)
    kg2 = jnp.tile(sm["k_norm_g"], 2).reshape(1, LANES)
    u = _mm_in_pairs(h1, bg["w_in"], "mm_in")
    zc = _conv_fwd(u, bg["conv_dw_w"], sm["conv_dw_b"], cdim, "conv_fwd")
    zs = _ln_swish_fwd(zc, sm["conv_ln_g"], sm["conv_ln_b"], "ln_swish_fwd")
    os_, lses = [], []
    for gi in range(N_GROUPS):
        o_g, lse_g = _attn_fwd(u, qg2, kg2, bias, gi, cols, hpg, "attn_fwd_g%d" % gi)
        os_.append(o_g)
        lses.append(lse_g)
    o = _combine_fwd(os_, lses, "combine_fwd")
    bg = {**bg, **get_rest(o)}
    mg, yc, ya = _gated_out(zs, o, bg["w_conv_out"], bg["w_attn_out"], u, gate_col, "gated_out")
    x1, h2 = _mm(mg, bg["w_out"], epi="res_rms", extra=x, gain=sm["norm2_g"], name="mm_out")
    f = _mm(h2, bg["w_ff1"], out_dtype=BF16, name="mm_ff1")
    after = at_ff1(f) if at_ff1 is not None else None
    if next_gain is None:
        x2, h_next = _mm(f, bg["w_ff2"], a_relu2=True, epi="res", extra=x1, after=after, name="mm_ff2"), None
    else:
        x2, h_next = _mm(f, bg["w_ff2"], a_relu2=True, epi="res_rms", extra=x1, gain=next_gain, after=after,
                         name="mm_ff2")
    saved = dict(x=x, h1=h1, u=u, zc=zc, zs=zs, yc=yc, os=os_, lses=lses, o=o, ya=ya, mg=mg, x1=x1, h2=h2, f=f,
                 qg2=qg2, kg2=kg2)
    return x2, h_next, saved, bg


GRAD_GROUPS = (("w_ff2", "w_ff1"), ("w_out", "w_conv_out", "w_attn_out", "conv_dw_w"), ("w_in",))


def _layer_bwd(dx, s, sm, bg, bias, ds_sum, after, emit):
    cdim = sm["conv_ln_g"].shape[0]
    ao = bg["w_attn_out"].shape[0]
    hpg = ao // HEAD_DIM
    cols, gate_col = _columns(cdim, ao)
    g = {}
    df = _mm(dx, bg["w_ff2"], tb=True, epi="drelu2", extra=s["f"], out_dtype=BF16, after=after, name="mm_dff2")
    g["w_ff2"] = _mm(s["f"], dx, ta=True, a_relu2=True, out_dtype=BF16, name="mm_gw_ff2")
    g["w_ff1"] = _mm(s["h2"], df, ta=True, out_dtype=BF16, out_slots=True, name="mm_gw_ff1")
    after = emit(GRAD_GROUPS[0], g)
    dx1, dg2 = _mm(df, bg["w_ff1"], tb=True, epi="rms_bwd", extra=(s["x1"], dx), gain=sm["norm2_g"], after=after,
                   name="mm_dff1")
    g["norm2_g"] = dg2[0]
    dmg = _mm(dx1, bg["w_out"], tb=True, name="mm_dout")
    g["w_out"] = _mm(s["mg"], dx1, ta=True, out_dtype=BF16, name="mm_gw_out")
    dyc, dya, dugc, duga = _gate_bwd(dmg, s["u"], s["yc"], s["ya"], gate_col, "gate_bwd")
    dzs = _mm(dyc, bg["w_conv_out"], tb=True, name="mm_dconv_out")
    g["w_conv_out"] = _mm(s["zs"], dyc, ta=True, out_dtype=BF16, name="mm_gw_conv_out")
    do = _mm(dya, bg["w_attn_out"], tb=True, name="mm_dattn_out")
    g["w_attn_out"] = _mm(s["o"], dya, ta=True, out_dtype=BF16, name="mm_gw_attn_out")
    dzc, dlg, dlb = _ln_swish_bwd(dzs, s["zc"], sm["conv_ln_g"], sm["conv_ln_b"], "ln_swish_bwd")
    g["conv_ln_g"] = dlg[0]
    g["conv_ln_b"] = dlb[0]
    da, dgt, dcw, dcb = _conv_bwd(dzc, s["u"], bg["conv_dw_w"], cdim, "conv_bwd")
    g["conv_dw_w"] = dcw[:CONV_WIDTH].astype(BF16)
    g["conv_dw_b"] = dcb[0]
    after = emit(GRAD_GROUPS[1], g)
    do_gs, dd_gs = _combine_bwd(do, s["os"], s["lses"], "combine_bwd")
    du = lax.empty(s["u"].shape, BF16)
    for col, piece in ((0, da), (cdim, dgt), (gate_col, dugc), (gate_col + dugc.shape[1], duga)):
        du = lax.dynamic_update_slice(du, piece, (0, col))
    gqs, gks = [], []
    for gi in range(N_GROUPS):
        gq, gk, ds_sum, du = _attn_bwd(s["u"], do_gs[gi], dd_gs[gi], s["lses"][gi], s["qg2"], s["kg2"], bias,
                                       ds_sum, du, gi, cols, hpg, "attn_bwd_g%d" % gi)
        gqs.append(gq)
        gks.append(gk)
    g["q_norm_g"] = jnp.concatenate(gqs)
    g["k_norm_g"] = jnp.concatenate(gks)
    g["w_in"] = _mm_gw_in_pairs(s["h1"], du, after, "mm_gw_in")
    after = emit(GRAD_GROUPS[2], g)
    dx0, dg1 = _mm_din_pairs(du, bg["w_in"], s["x"], sm["norm1_g"], dx1, after, "mm_din")
    g["norm1_g"] = dg1[0]
    return dx0, g, ds_sum


BIG = ("w_in", "conv_dw_w", "w_conv_out", "w_attn_out", "w_out", "w_ff1", "w_ff2")
COL_SHARDED = ("w_in", "conv_dw_w", "w_conv_out", "w_attn_out", "w_ff1")
SMALL = ("rel_bias", "norm1_g", "q_norm_g", "k_norm_g", "conv_dw_b", "conv_ln_g", "conv_ln_b", "norm2_g")
WEIGHTS = ("rel_bias", "norm1_g", "w_in", "q_norm_g", "k_norm_g", "conv_dw_w", "conv_dw_b", "conv_ln_g", "conv_ln_b",
           "w_conv_out", "w_attn_out", "w_out", "norm2_g", "w_ff1", "w_ff2")


def _to_whole(name, gathered):
    n, a, b = gathered.shape
    if name in COL_SHARDED:
        return gathered.transpose(1, 0, 2).reshape(a, n * b)
    return gathered.reshape(n * a, b)


def _to_slots(name, whole):
    a, b = whole.shape
    if name in COL_SHARDED:
        return whole.reshape(a, N_DEV, b // N_DEV).transpose(1, 0, 2)
    return whole.reshape(N_DEV, a // N_DEV, b)


def _own_slot(block, me):
    land = lax.empty((N_DEV,) + block.shape, block.dtype)
    return lax.dynamic_update_slice(land, block[None], (me,) + (0,) * block.ndim)


def kernel(x, rel_bias, norm1_g, w_in, q_norm_g, k_norm_g, conv_dw_w, conv_dw_b, conv_ln_g, conv_ln_b, w_conv_out, w_attn_out, w_out, norm2_g, w_ff1, w_ff2, loss_target, m_rel_bias, m_norm1_g, m_w_in, m_q_norm_g, m_k_norm_g, m_conv_dw_w, m_conv_dw_b, m_conv_ln_g, m_conv_ln_b, m_w_conv_out, m_w_attn_out, m_w_out, m_norm2_g, m_w_ff1, m_w_ff2, v_rel_bias, v_norm1_g, v_w_in, v_q_norm_g, v_k_norm_g, v_conv_dw_w, v_conv_dw_b, v_conv_ln_g, v_conv_ln_b, v_w_conv_out, v_w_attn_out, v_w_out, v_norm2_g, v_w_ff1, v_w_ff2):
    w = dict(rel_bias=rel_bias, norm1_g=norm1_g, w_in=w_in, q_norm_g=q_norm_g, k_norm_g=k_norm_g, conv_dw_w=conv_dw_w,
             conv_dw_b=conv_dw_b, conv_ln_g=conv_ln_g, conv_ln_b=conv_ln_b, w_conv_out=w_conv_out,
             w_attn_out=w_attn_out, w_out=w_out, norm2_g=norm2_g, w_ff1=w_ff1, w_ff2=w_ff2)
    mom = dict(rel_bias=m_rel_bias, norm1_g=m_norm1_g, w_in=m_w_in, q_norm_g=m_q_norm_g, k_norm_g=m_k_norm_g,
               conv_dw_w=m_conv_dw_w, conv_dw_b=m_conv_dw_b, conv_ln_g=m_conv_ln_g, conv_ln_b=m_conv_ln_b,
               w_conv_out=m_w_conv_out, w_attn_out=m_w_attn_out, w_out=m_w_out, norm2_g=m_norm2_g, w_ff1=m_w_ff1,
               w_ff2=m_w_ff2)
    var = dict(rel_bias=v_rel_bias, norm1_g=v_norm1_g, w_in=v_w_in, q_norm_g=v_q_norm_g, k_norm_g=v_k_norm_g,
               conv_dw_w=v_conv_dw_w, conv_dw_b=v_conv_dw_b, conv_ln_g=v_conv_ln_g, conv_ln_b=v_conv_ln_b,
               w_conv_out=v_w_conv_out, w_attn_out=v_w_attn_out, w_out=v_w_out, norm2_g=v_norm2_g, w_ff1=v_w_ff1,
               w_ff2=v_w_ff2)

    depth = norm1_g.shape[0]
    me = 4 * lax.axis_index("x") + 2 * lax.axis_index("y") + lax.axis_index("c")
    odd_core = lax.axis_index("c") == 1
    hpg = w_attn_out.shape[1] // HEAD_DIM
    buckets = jnp.asarray(_bucket_table())

    first_names = ("w_in", "conv_dw_w")
    rest_names = tuple(k for k in BIG if k not in first_names)

    def chain_start(l, names, after):
        shards = [w[k][l] if k == "conv_dw_w" else w[k][l].astype(BF16) for k in names]
        if "w_in" in names:
            i = names.index("w_in")
            shards[i] = jnp.where(odd_core, jnp.pad(shards[i], ((0, 0), (SHIFT, 0))),
                                  jnp.pad(shards[i], ((0, 0), (0, SHIFT))))
        sems, bufs, token = _gather_start(shards, [_own_slot(s, me) for s in shards], after,
                                          "gather_start_%s_l%d" % (names[0], l))
        return dict(l=l, names=names, sems=sems, bufs=bufs, token=token)

    def chain_forward(ch, after):
        fwd, bufs, token = _gather_forward(len(ch["names"]), ch["bufs"], ch["sems"][3], after,
                                           "gather_forward_%s_l%d" % (ch["names"][0], ch["l"]))
        ch.update(fwd=fwd, bufs=bufs)
        return token

    def chain_finish(ch, after):
        d2d_s, d2d_r, ici_s, _ = ch["sems"]
        gathered, _ = _gather_finish(len(ch["names"]), ch["bufs"], d2d_s, d2d_r, ici_s, ch["fwd"][0], ch["fwd"][1],
                                     after, "gather_finish_%s_l%d" % (ch["names"][0], ch["l"]))
        out = {k: a if k == "w_in" else _to_whole(k, a) for k, a in zip(ch["names"], gathered)}
        if "conv_dw_w" in out:
            out["conv_dw_w"] = jnp.pad(out["conv_dw_w"], ((0, CONV_TAPS_PADDED - CONV_WIDTH), (0, 0)))
        return out

    xs = x[0]
    first = chain_start(0, first_names, None)
    h1 = _rms_fwd(xs, norm1_g[0], first["token"], "rms1_fwd")
    bias = _bias_expand(rel_bias, buckets, hpg, h1, "bias_expand")
    saved, bigs, smalls = [], [], []
    chains = {}
    for l in range(depth):
        sm = {k: w[k][l] for k in SMALL if k != "rel_bias"}
        if l == 0:
            token = chain_forward(first, bias)
            rest = chain_start(0, rest_names, token)
            bg = chain_finish(first, rest["token"])

            def get_rest(o, rest=rest):
                token = chain_forward(rest, o)
                if depth > 1:
                    chains[1] = (chain_start(1, first_names, token),)
                    chains[1] += (chain_start(1, rest_names, chains[1][0]["token"]),)
                    token = chains[1][1]["token"]
                return chain_finish(rest, token)
        elif l == 1:
            first, rest = chains[1]
            token = chain_forward(first, xs)
            if depth > 2:
                chains[2] = chain_start(2, BIG, token)
                token = chains[2]["token"]
            bg = chain_finish(first, token)

            def get_rest(o, rest=rest):
                return chain_finish(rest, chain_forward(rest, o))
        else:
            token = xs if "fwd" in chains[l] else chain_forward(chains[l], xs)
            if l + 1 < depth:
                chains[l + 1] = chain_start(l + 1, BIG, token)
                token = chains[l + 1]["token"]
            whole = chain_finish(chains[l], token)
            bg = {k: whole[k] for k in first_names}

            def get_rest(o, whole=whole):
                return {k: whole[k] for k in rest_names}
        at_ff1 = (lambda f, nxt=l + 1: chain_forward(chains[nxt], f)) if 2 <= l < depth - 1 else None
        xs, h1, sv, bg = _layer_fwd(xs, h1, sm, bg, get_rest, bias, hpg, norm1_g[l + 1] if l + 1 < depth else None,
                                    at_ff1)
        saved.append(sv)
        bigs.append(bg)
        smalls.append(sm)

    loss, dx = _loss_and_grad(xs, loss_target[0], "loss")

    ds_sum = jnp.zeros((N_GROUPS * hpg, BLOCK, 2 * BLOCK), F32)
    g = {k: [None] * depth for k in SMALL if k != "rel_bias"}
    sums = {k: lax.empty((depth, int(np.prod(w[k].shape[1:-1])), w[k].shape[-1] + (SHIFT if k == "w_in" else 0)), F32)
            for k in BIG}
    pending = []

    def finish_oldest(after):
        names, l, (send, recv), bufs = pending.pop(0)
        parts, recvd, token = _exchange_finish(len(names), bufs, send, recv, after,
                                               "exchange_finish_%s_l%d" % (names[0], l))
        for k, r, p in zip(names, recvd, parts):
            three = (N_DEV, -1, r.shape[-1])
            sums[k] = _sum_slots(r.reshape(three), p.reshape(three), me, sums[k], l, "sum_" + k)
        return token

    def make_emit(l):
        def emit(names, gl):
            parts = [gl[k] if k in ("w_ff1", "w_in") else _to_slots(k, gl[k]) for k in names]
            token = finish_oldest(parts[0]) if len(pending) >= len(GRAD_GROUPS) else None
            lands = [lax.empty(p.shape, p.dtype) for p in parts]
            sems, bufs, token = _exchange_start(parts, lands, token, "exchange_start_%s_l%d" % (names[0], l))
            pending.append((names, l, sems, bufs))
            return token
        return emit

    token = None
    for l in reversed(range(depth)):
        dx, gl, ds_sum = _layer_bwd(dx, saved[l], smalls[l], bigs[l], bias, ds_sum, token, make_emit(l))
        for k in g:
            g[k][l] = gl[k]
        token = None
    grad_x = dx

    g = {k: jnp.stack(v) for k, v in g.items()}
    for k in ("q_norm_g", "k_norm_g"):
        g[k] = jnp.sum(g[k].reshape(depth, -1, HEAD_DIM), axis=1)
    db = _bias_reduce(ds_sum, buckets, hpg, "bias_reduce")
    g["rel_bias"] = db[:, 0, :NUM_BUCKETS].T

    flat = jnp.concatenate([g[k].reshape(-1) for k in SMALL])
    nflat = flat.shape[0]
    rows = -(-nflat // (8 * LANES)) * 8
    packed = jnp.pad(flat, (0, rows * LANES - nflat)).reshape(rows, LANES)
    grad, outs = {}, {}
    token = dx
    while pending:
        names = pending[0][0]
        finish_oldest(token)
        for k in names:
            total = sums[k]
            if k == "w_in":
                total = jnp.where(odd_core, total[..., SHIFT:], total[..., :w_in.shape[-1]])
            grad[k] = total.reshape(w[k].shape)
            outs[k] = _adamw(w[k], grad[k], mom[k], var[k], "adamw_" + k)
        token = tuple(outs[k][0] for k in names)
    total = _all_reduce_small(packed, token[-1], "reduce_small").reshape(-1)
    off = 0
    for k in SMALL:
        size = int(np.prod(w[k].shape))
        grad[k] = total[off:off + size].reshape(w[k].shape)
        outs[k] = _adamw(w[k], grad[k], mom[k], var[k], "adamw_" + k)
        off += size
    loss = lax.psum(loss[0, 0], ("x", "y", "c"))
    return (loss, grad_x[None], *[grad[k] for k in WEIGHTS], *[outs[k][0] for k in WEIGHTS],
            *[outs[k][1] for k in WEIGHTS], *[outs[k][2] for k in WEIGHTS])
```

```python
import functools
import math

import numpy as np
import jax
import jax.numpy as jnp
from jax import lax
from jax.experimental import pallas as pl
from jax.experimental.pallas import tpu as pltpu

F32 = jnp.float32
BF16 = jnp.bfloat16

HEAD_DIM = 64
N_GROUPS = 3
DILATIONS = (1, 4, 16)
SUB_WINDOW = 128
BLOCK = 128
CONV_WIDTH = 31
CONV_TAPS_PADDED = 32
NUM_BUCKETS = 32
MAX_REL_DISTANCE = 2048
EPS = 1e-6
NEG_INF = -1e30
LANES = 128
SUBLANES = 8

ADAM_LR = 0.001
ADAM_B1 = 0.9
ADAM_B2 = 0.999
ADAM_EPS = 1e-08
ADAM_WD = 0.01
ADAM_STEP = 10

N_DEV = 8
VMEM_LIMIT = 56 * 1024 * 1024
MESH = pl.DeviceIdType.MESH


def _cparams(sem=None):
    return pltpu.CompilerParams(dimension_semantics=sem, vmem_limit_bytes=VMEM_LIMIT)


def _tile(n, target):
    if n <= target:
        return n
    t = (target // LANES) * LANES
    while t >= LANES:
        if n % t == 0:
            return t
        t -= LANES
    return n


def _sigmoid(v):
    return 1.0 / (1.0 + jnp.exp(-v))


MM_VMEM_BUDGET = 40 * 1024 * 1024


def _rms_apply(x, g):
    return x * lax.rsqrt(jnp.mean(x * x, axis=-1, keepdims=True) + EPS) * g


def _rms_grad(dh, x, g):
    r = lax.rsqrt(jnp.mean(x * x, axis=-1, keepdims=True) + EPS)
    xh = x * r
    dxh = dh * g
    dx = r * (dxh - xh * jnp.mean(dxh * xh, axis=-1, keepdims=True))
    return dx, jnp.sum(dh * xh, axis=0, keepdims=True)


def _mm_tiles(m, n, kdim, a_bytes, b_bytes, io_bytes, whole_rows=False, temps=2):
    def need(tm, tn, tk):
        blocks = 2 * (tm * tk * a_bytes + tk * tn * b_bytes + tm * tn * io_bytes)
        casts = (tm * tk * 2 if a_bytes == 4 else 0) + (tk * tn * 2 if b_bytes == 4 else 0)
        return blocks + casts + temps * tm * tn * 4

    tn = n if whole_rows else _tile(n, 1024)
    while True:
        fits = [(tm * tk, tm, tk) for tm in {_tile(m, c) for c in (1024, 512, 256, 128)}
                for tk in {_tile(kdim, c) for c in (2048, 1024, 512, 256)} if need(tm, tn, tk) <= MM_VMEM_BUDGET]
        if fits:
            _, tm, tk = max(fits)
            return tm, tn, tk
        assert not whole_rows and tn % 256 == 0, "no block size fits the VMEM budget"
        tn //= 2


def _mm(a, b, *, ta=False, tb=False, out_dtype=F32, epi=None, extra=(), gain=None, after=None, out_slots=False,
        a_relu2=False, name):
    extra = tuple(extra) if isinstance(extra, (tuple, list)) else (extra,)
    m = a.shape[1] if ta else a.shape[0]
    kdim = a.shape[0] if ta else a.shape[1]
    n = b.shape[0] if tb else b.shape[1]
    norm = epi in ("res_rms", "rms_bwd")
    io_bytes = (jnp.dtype(out_dtype).itemsize + sum(e.dtype.itemsize for e in extra) + (2 if epi == "res_rms" else 0))
    tm, tn, tk = _mm_tiles(m, n // N_DEV if out_slots else n, kdim, a.dtype.itemsize, b.dtype.itemsize, io_bytes,
                           whole_rows=norm, temps=6 if norm else 2)
    if out_slots:
        assert epi is None and tn == n // N_DEV
    nk = kdim // tk
    a_spec = pl.BlockSpec((tk, tm), lambda i, j, k: (k, i)) if ta else pl.BlockSpec((tm, tk), lambda i, j, k: (i, k))
    b_spec = pl.BlockSpec((tn, tk), lambda i, j, k: (j, k)) if tb else pl.BlockSpec((tk, tn), lambda i, j, k: (k, j))
    o_spec = (pl.BlockSpec((None, tm, tn), lambda i, j, k: (j, i, 0)) if out_slots
              else pl.BlockSpec((tm, tn), lambda i, j, k: (i, j)))
    v_spec = pl.BlockSpec((1, tn), lambda i, j, k: (0, j))
    dims = (((0 if ta else 1,), (1 if tb else 0,)), ((), ()))
    n_extra = len(extra)
    n_in = 2 + n_extra + (gain is not None) + (after is not None)
    n_out = 2 if norm else 1

    def body(*refs):
        a_ref, b_ref = refs[0], refs[1]
        e_refs = refs[2:2 + n_extra]
        g_ref = refs[2 + n_extra] if gain is not None else None
        outs = refs[n_in:n_in + n_out]

        def product():
            av = a_ref[...]
            if a_relu2:
                r = jnp.maximum(av.astype(F32), 0.0)
                av = r * r
            return lax.dot_general(av.astype(BF16), b_ref[...].astype(BF16), dims, preferred_element_type=F32)

        def finish(acc):
            if epi is None:
                outs[0][...] = acc.astype(outs[0].dtype)
            elif epi == "res":
                outs[0][...] = (e_refs[0][...] + acc).astype(outs[0].dtype)
            elif epi == "drelu2":
                outs[0][...] = (acc * (2.0 * jnp.maximum(e_refs[0][...].astype(F32), 0.0))).astype(outs[0].dtype)
            elif epi == "res_rms":
                x1 = e_refs[0][...] + acc
                outs[0][...] = x1
                outs[1][...] = _rms_apply(x1, g_ref[...]).astype(BF16)
            elif epi == "rms_bwd":
                dx, dg = _rms_grad(acc, e_refs[0][...], g_ref[...])
                outs[0][...] = e_refs[1][...] + dx
                i = pl.program_id(0)

                @pl.when(i == 0)
                def _():
                    outs[1][...] = dg

                @pl.when(i > 0)
                def _():
                    outs[1][...] += dg

        if nk == 1:
            finish(product())
            return
        acc_ref = refs[-1]
        k = pl.program_id(2)

        @pl.when(k == 0)
        def _():
            acc_ref[...] = product()

        @pl.when(jnp.logical_and(k > 0, k < nk - 1))
        def _():
            acc_ref[...] += product()

        @pl.when(k == nk - 1)
        def _():
            finish(acc_ref[...] + product())

    in_specs = ([a_spec, b_spec] + [o_spec] * n_extra + ([v_spec] if gain is not None else [])
                + ([pl.BlockSpec(memory_space=pl.ANY)] if after is not None else []))
    if epi == "res_rms":
        out_shape = (jax.ShapeDtypeStruct((m, n), F32), jax.ShapeDtypeStruct((m, n), BF16))
        out_specs = (o_spec, o_spec)
    elif epi == "rms_bwd":
        out_shape = (jax.ShapeDtypeStruct((m, n), F32), jax.ShapeDtypeStruct((1, n), F32))
        out_specs = (o_spec, v_spec)
    else:
        out_shape = jax.ShapeDtypeStruct((N_DEV, m, tn) if out_slots else (m, n), out_dtype)
        out_specs = o_spec
    args = (a, b) + extra + ((gain.reshape(1, n),) if gain is not None else ()) + ((after,) if after is not None else ())
    return pl.pallas_call(
        body, name=name, grid=(m // tm, n // tn, nk), in_specs=in_specs, out_specs=out_specs, out_shape=out_shape,
        scratch_shapes=[pltpu.VMEM((tm, tn), F32)] if nk > 1 else [],
        compiler_params=_cparams(("arbitrary", "arbitrary", "arbitrary")),
    )(*args)


SHIFT = HEAD_DIM


def _pair_blocks(e, o):
    wp = e.shape[-1]
    return e[:, :wp - LANES], e[:, wp - LANES:] + o[:, :LANES], o[:, LANES:]


def _mm_in_pairs(a, wg, name):
    t, kdim = a.shape
    wp = wg.shape[-1]
    ws = wp - SHIFT
    tm = _tile(t, 1024)

    def body(a_ref, e_ref, o_ref, u_ref):
        av = a_ref[...]
        lo, mid, hi = _pair_blocks(e_ref[...], o_ref[...])
        u_ref[:, :wp - LANES] = jnp.dot(av, lo, preferred_element_type=F32).astype(BF16)
        u_ref[:, wp - LANES:wp] = jnp.dot(av, mid, preferred_element_type=F32).astype(BF16)
        u_ref[:, wp:] = jnp.dot(av, hi, preferred_element_type=F32).astype(BF16)

    return pl.pallas_call(
        body, name=name, grid=(N_DEV // 2, t // tm),
        in_specs=[pl.BlockSpec((tm, kdim), lambda p, i: (i, 0)),
                  pl.BlockSpec((None, kdim, wp), lambda p, i: (2 * p, 0, 0)),
                  pl.BlockSpec((None, kdim, wp), lambda p, i: (2 * p + 1, 0, 0))],
        out_specs=pl.BlockSpec((tm, 2 * ws), lambda p, i: (i, p)),
        out_shape=jax.ShapeDtypeStruct((t, N_DEV * ws), BF16), compiler_params=_cparams(("arbitrary", "arbitrary")),
    )(a, wg, wg)


def _mm_din_pairs(du, wg, x, gain, dres, after, name):
    t = du.shape[0]
    _, kdim, wp = wg.shape
    ws = wp - SHIFT
    tm = _tile(t, 512)
    npair = N_DEV // 2
    lanes = (((1,), (1,)), ((), ()))
    extra = [] if after is None else [after]

    def body(d_ref, e_ref, o_ref, x_ref, g_ref, r_ref, *rest):
        dx_ref, dg_ref, acc_ref = rest[-3], rest[-2], rest[-1]
        i, p = pl.program_id(0), pl.program_id(1)
        lo, mid, hi = _pair_blocks(e_ref[...], o_ref[...])
        part = (lax.dot_general(d_ref[:, :wp - LANES], lo, lanes, preferred_element_type=F32)
                + lax.dot_general(d_ref[:, wp - LANES:wp], mid, lanes, preferred_element_type=F32)
                + lax.dot_general(d_ref[:, wp:], hi, lanes, preferred_element_type=F32))

        @pl.when(p == 0)
        def _():
            acc_ref[...] = part

        @pl.when(jnp.logical_and(p > 0, p < npair - 1))
        def _():
            acc_ref[...] += part

        @pl.when(p == npair - 1)
        def _():
            dx, dg = _rms_grad(acc_ref[...] + part, x_ref[...], g_ref[...])
            dx_ref[...] = r_ref[...] + dx

            @pl.when(i == 0)
            def _():
                dg_ref[...] = dg

            @pl.when(i > 0)
            def _():
                dg_ref[...] += dg

    row = pl.BlockSpec((tm, kdim), lambda i, p: (i, 0))
    vec = pl.BlockSpec((1, kdim), lambda i, p: (0, 0))
    return pl.pallas_call(
        body, name=name, grid=(t // tm, npair),
        in_specs=[pl.BlockSpec((tm, 2 * ws), lambda i, p: (i, p)),
                  pl.BlockSpec((None, kdim, wp), lambda i, p: (2 * p, 0, 0)),
                  pl.BlockSpec((None, kdim, wp), lambda i, p: (2 * p + 1, 0, 0)), row, vec, row]
        + [pl.BlockSpec(memory_space=pl.ANY)] * len(extra),
        out_specs=(row, vec),
        out_shape=(jax.ShapeDtypeStruct((t, kdim), F32), jax.ShapeDtypeStruct((1, kdim), F32)),
        scratch_shapes=[pltpu.VMEM((tm, kdim), F32)], compiler_params=_cparams(("arbitrary", "arbitrary")),
    )(du, wg, wg, x, gain.reshape(1, kdim), dres, *extra)


def _mm_gw_in_pairs(h, du, after, name):
    t, kdim = h.shape
    ws = du.shape[1] // N_DEV
    wp = ws + SHIFT
    tm = _tile(kdim, 512)
    rows = (((0,), (0,)), ((), ()))
    extra = [] if after is None else [after]

    def body(h_ref, d_ref, *rest):
        g_ref = rest[-1]
        g = lax.dot_general(h_ref[...], d_ref[...], rows, preferred_element_type=F32)
        g_ref[0] = g[:, :wp].astype(BF16)
        g_ref[1] = g[:, wp - LANES:].astype(BF16)

    return pl.pallas_call(
        body, name=name, grid=(N_DEV // 2, kdim // tm),
        in_specs=[pl.BlockSpec((t, tm), lambda p, i: (0, i)), pl.BlockSpec((t, 2 * ws), lambda p, i: (0, p))]
        + [pl.BlockSpec(memory_space=pl.ANY)] * len(extra),
        out_specs=pl.BlockSpec((2, tm, wp), lambda p, i: (p, i, 0)),
        out_shape=jax.ShapeDtypeStruct((N_DEV, kdim, wp), BF16), compiler_params=_cparams(("arbitrary", "arbitrary")),
    )(h, du, *extra)


ROW_BLOCK_BUDGET = 24 * 1024 * 1024


def _rows(t, row_bytes):
    rows = t
    while rows > 8 and (2 * rows * row_bytes > ROW_BLOCK_BUDGET or t % rows):
        rows //= 2
    return rows


def _rms_fwd(x, g, after, name):
    t, d = x.shape
    ROWS = _rows(t, 6 * d)

    def body(x_ref, g_ref, _, h_ref):
        h_ref[...] = _rms_apply(x_ref[...], g_ref[...]).astype(BF16)

    return pl.pallas_call(
        body, name=name, grid=(t // ROWS,),
        in_specs=[pl.BlockSpec((ROWS, d), lambda i: (i, 0)), pl.BlockSpec((1, d), lambda i: (0, 0)),
                  pl.BlockSpec(memory_space=pl.ANY)],
        out_specs=pl.BlockSpec((ROWS, d), lambda i: (i, 0)),
        out_shape=jax.ShapeDtypeStruct((t, d), BF16), compiler_params=_cparams(("arbitrary",)),
    )(x, g.reshape(1, d), after)


def _gated_out(zs, o, wc, wa, u, gate_col, name):
    t = zs.shape[0]
    d = wc.shape[1]
    td = math.gcd(_tile(d, 512), gate_col)
    nd = d // td
    c0 = gate_col // td
    tm = _tile(t, 1024)

    def body(zs_ref, o_ref, wc_ref, wa_ref, gc_ref, ga_ref, m_ref, yc_ref, ya_ref):
        yc = jnp.dot(zs_ref[...], wc_ref[...], preferred_element_type=F32)
        ya = jnp.dot(o_ref[...], wa_ref[...], preferred_element_type=F32)
        gc = _sigmoid(gc_ref[...].astype(F32))
        ga = _sigmoid(ga_ref[...].astype(F32))
        m_ref[...] = (gc * yc + ga * ya).astype(BF16)
        yc_ref[...] = yc.astype(BF16)
        ya_ref[...] = ya.astype(BF16)

    blk = pl.BlockSpec((tm, td), lambda i, j: (i, j))
    sh = jax.ShapeDtypeStruct((t, d), BF16)
    return pl.pallas_call(
        body, name=name, grid=(t // tm, nd),
        in_specs=[pl.BlockSpec((tm, zs.shape[1]), lambda i, j: (i, 0)), pl.BlockSpec((tm, o.shape[1]), lambda i, j: (i, 0)),
                  pl.BlockSpec((wc.shape[0], td), lambda i, j: (0, j)), pl.BlockSpec((wa.shape[0], td), lambda i, j: (0, j)),
                  pl.BlockSpec((tm, td), lambda i, j: (i, c0 + j)), pl.BlockSpec((tm, td), lambda i, j: (i, c0 + nd + j))],
        out_specs=(blk, blk, blk), out_shape=(sh, sh, sh), compiler_params=_cparams(("arbitrary", "arbitrary")),
    )(zs, o, wc, wa, u, u)


def _gate_bwd(dm, u, yc, ya, gate_col, name):
    t, d = yc.shape
    td = math.gcd(_tile(d, 512), gate_col)
    nd = d // td
    c0 = gate_col // td
    ROWS = _rows(t, 20 * td)

    def body(dm_ref, gc_ref, ga_ref, yc_ref, ya_ref, dyc_ref, dya_ref, dugc_ref, duga_ref):
        dmv = dm_ref[...]
        gc = _sigmoid(gc_ref[...].astype(F32))
        ga = _sigmoid(ga_ref[...].astype(F32))
        dyc_ref[...] = (dmv * gc).astype(BF16)
        dya_ref[...] = (dmv * ga).astype(BF16)
        dugc_ref[...] = (dmv * yc_ref[...].astype(F32) * gc * (1.0 - gc)).astype(BF16)
        duga_ref[...] = (dmv * ya_ref[...].astype(F32) * ga * (1.0 - ga)).astype(BF16)

    blk = pl.BlockSpec((ROWS, td), lambda i, j: (i, j))
    o = jax.ShapeDtypeStruct((t, d), BF16)
    return pl.pallas_call(
        body, name=name, grid=(t // ROWS, nd),
        in_specs=[blk, pl.BlockSpec((ROWS, td), lambda i, j: (i, c0 + j)),
                  pl.BlockSpec((ROWS, td), lambda i, j: (i, c0 + nd + j)), blk, blk],
        out_specs=(blk, blk, blk, blk), out_shape=(o, o, o, o),
        compiler_params=_cparams(("arbitrary", "arbitrary")),
    )(dm, u, u, yc, ya)


def _loss_and_grad(y, target, name):
    t, d = y.shape
    ROWS = _rows(t, 12 * d)
    n = t // ROWS

    def body(y_ref, t_ref, loss_ref, dy_ref, acc_ref):
        i = pl.program_id(0)

        @pl.when(i == 0)
        def _():
            acc_ref[...] = jnp.zeros_like(acc_ref)

        diff = y_ref[...] - t_ref[...]
        dy_ref[...] = diff * (1.0 / d)
        acc_ref[...] += jnp.sum(diff * diff, axis=0, keepdims=True)

        @pl.when(i == n - 1)
        def _():
            loss_ref[...] = jnp.sum(acc_ref[...], axis=-1, keepdims=True) * (0.5 / d)

    row = pl.BlockSpec((ROWS, d), lambda i: (i, 0))
    return pl.pallas_call(
        body, name=name, grid=(n,), in_specs=[row, row],
        out_specs=(pl.BlockSpec((1, 1), lambda i: (0, 0)), row),
        out_shape=(jax.ShapeDtypeStruct((1, 1), F32), jax.ShapeDtypeStruct((t, d), F32)),
        scratch_shapes=[pltpu.VMEM((1, d), F32)], compiler_params=_cparams(("arbitrary",)),
    )(y, target)


HALO = 32


def _conv_fwd(u, w, b, cdim, name):
    t = u.shape[0]
    ncb = cdim // LANES
    nt = t // BLOCK

    def body(a_ref, g_ref, w_ref, b_ref, zc_ref, zpad):
        zpad[0:HALO, :] = jnp.zeros((HALO, LANES), F32)
        zpad[HALO:HALO + t, :] = a_ref[...].astype(F32) * _sigmoid(g_ref[...].astype(F32))
        wv = w_ref[...]
        bv = b_ref[...]

        def tile(i, carry):
            r0 = pl.multiple_of(i * BLOCK, BLOCK)
            win = zpad[pl.ds(r0, BLOCK + HALO), :]
            acc = jnp.zeros((BLOCK, LANES), F32) + bv
            for b in range(SUBLANES):
                sh = win if b == 0 else pltpu.roll(win, b, 0)
                for a in range(HALO // SUBLANES):
                    j = CONV_WIDTH - 1 - (SUBLANES * a + b)
                    if j >= 0:
                        lo = HALO - SUBLANES * a
                        acc = acc + wv[j:j + 1, :] * sh[lo:lo + BLOCK, :]
            zc_ref[pl.ds(r0, BLOCK), :] = acc
            return carry

        lax.fori_loop(0, nt, tile, 0)

    col = lambda off: pl.BlockSpec((t, LANES), lambda c: (0, off + c))
    return pl.pallas_call(
        body, name=name, grid=(ncb,),
        in_specs=[col(0), col(ncb), pl.BlockSpec((CONV_TAPS_PADDED, LANES), lambda c: (0, c)),
                  pl.BlockSpec((1, LANES), lambda c: (0, c))],
        out_specs=pl.BlockSpec((t, LANES), lambda c: (0, c)),
        out_shape=jax.ShapeDtypeStruct((t, cdim), F32),
        scratch_shapes=[pltpu.VMEM((t + HALO, LANES), F32)], compiler_params=_cparams(("arbitrary",)),
    )(u, u, w, b.reshape(1, cdim))


def _conv_bwd(dzc, u, w, cdim, name):
    t = u.shape[0]
    ncb = cdim // LANES
    nt = t // BLOCK
    win_rows = BLOCK + HALO

    def body(dzc_ref, a_ref, g_ref, w_ref, da_ref, dg_ref, dw_ref, db_ref, zpad, dpad):
        av = a_ref[...].astype(F32)
        sg = _sigmoid(g_ref[...].astype(F32))
        zpad[0:HALO, :] = jnp.zeros((HALO, LANES), F32)
        zpad[HALO:HALO + t, :] = av * sg
        dpad[0:t, :] = dzc_ref[...]
        dpad[t:t + HALO, :] = jnp.zeros((HALO, LANES), F32)
        dw_ref[...] = jnp.zeros_like(dw_ref)
        db_ref[...] = jnp.sum(dzc_ref[...], axis=0, keepdims=True)
        wv = w_ref[...]

        def tile(i, carry):
            r0 = pl.multiple_of(i * BLOCK, BLOCK)
            zwin = zpad[pl.ds(r0, win_rows), :]
            dwin = dpad[pl.ds(r0, win_rows), :]
            dcur = dwin[0:BLOCK, :]
            dz = jnp.zeros((BLOCK, LANES), F32)
            for b in range(SUBLANES):
                zs = zwin if b == 0 else pltpu.roll(zwin, b, 0)
                ds = dwin if b == 0 else pltpu.roll(dwin, win_rows - b, 0)
                for a in range(HALO // SUBLANES):
                    j = CONV_WIDTH - 1 - (SUBLANES * a + b)
                    if j >= 0:
                        lo = HALO - SUBLANES * a
                        dw_ref[j:j + 1, :] += jnp.sum(dcur * zs[lo:lo + BLOCK, :], axis=0, keepdims=True)
                        dz = dz + wv[j:j + 1, :] * ds[SUBLANES * a:SUBLANES * a + BLOCK, :]
            ac = a_ref[pl.ds(r0, BLOCK), :].astype(F32)
            sc = _sigmoid(g_ref[pl.ds(r0, BLOCK), :].astype(F32))
            da_ref[pl.ds(r0, BLOCK), :] = (dz * sc).astype(BF16)
            dg_ref[pl.ds(r0, BLOCK), :] = (dz * ac * sc * (1.0 - sc)).astype(BF16)
            return carry

        lax.fori_loop(0, nt, tile, 0)

    col = lambda off: pl.BlockSpec((t, LANES), lambda c: (0, off + c))
    wspec = pl.BlockSpec((CONV_TAPS_PADDED, LANES), lambda c: (0, c))
    o = jax.ShapeDtypeStruct((t, cdim), BF16)
    return pl.pallas_call(
        body, name=name, grid=(ncb,), in_specs=[col(0), col(0), col(ncb), wspec],
        out_specs=(col(0), col(0), wspec, pl.BlockSpec((1, LANES), lambda c: (0, c))),
        out_shape=(o, o, jax.ShapeDtypeStruct((CONV_TAPS_PADDED, cdim), F32), jax.ShapeDtypeStruct((1, cdim), F32)),
        scratch_shapes=[pltpu.VMEM((t + HALO, LANES), F32), pltpu.VMEM((t + HALO, LANES), F32)],
        compiler_params=_cparams(("arbitrary",)),
    )(dzc, u, u, w)


def _ln_swish_fwd(zc, g, b, name):
    t, c = zc.shape
    ROWS = _rows(t, 6 * c)

    def body(z_ref, g_ref, b_ref, o_ref):
        z = z_ref[...]
        mu = jnp.mean(z, axis=-1, keepdims=True)
        zc_ = z - mu
        zn = zc_ * lax.rsqrt(jnp.mean(zc_ * zc_, axis=-1, keepdims=True) + EPS)
        y = zn * g_ref[...] + b_ref[...]
        o_ref[...] = (y * _sigmoid(y)).astype(BF16)

    row = pl.BlockSpec((ROWS, c), lambda i: (i, 0))
    vec = pl.BlockSpec((1, c), lambda i: (0, 0))
    return pl.pallas_call(
        body, name=name, grid=(t // ROWS,), in_specs=[row, vec, vec], out_specs=row,
        out_shape=jax.ShapeDtypeStruct((t, c), BF16), compiler_params=_cparams(("arbitrary",)),
    )(zc, g.reshape(1, c), b.reshape(1, c))


def _ln_swish_bwd(dzs, zc, g, b, name):
    t, c = zc.shape
    ROWS = _rows(t, 12 * c)

    def body(d_ref, z_ref, g_ref, b_ref, dz_ref, dg_ref, db_ref):
        @pl.when(pl.program_id(0) == 0)
        def _():
            dg_ref[...] = jnp.zeros_like(dg_ref)
            db_ref[...] = jnp.zeros_like(db_ref)

        z = z_ref[...]
        mu = jnp.mean(z, axis=-1, keepdims=True)
        zc_ = z - mu
        rstd = lax.rsqrt(jnp.mean(zc_ * zc_, axis=-1, keepdims=True) + EPS)
        zn = zc_ * rstd
        y = zn * g_ref[...] + b_ref[...]
        sg = _sigmoid(y)
        dy = d_ref[...] * (sg * (1.0 + y * (1.0 - sg)))
        dg_ref[...] += jnp.sum(dy * zn, axis=0, keepdims=True)
        db_ref[...] += jnp.sum(dy, axis=0, keepdims=True)
        dzn = dy * g_ref[...]
        dz_ref[...] = rstd * (dzn - jnp.mean(dzn, axis=-1, keepdims=True)
                              - zn * jnp.mean(dzn * zn, axis=-1, keepdims=True))

    row = pl.BlockSpec((ROWS, c), lambda i: (i, 0))
    vec = pl.BlockSpec((1, c), lambda i: (0, 0))
    v = jax.ShapeDtypeStruct((1, c), F32)
    return pl.pallas_call(
        body, name=name, grid=(t // ROWS,), in_specs=[row, row, vec, vec], out_specs=(row, vec, vec),
        out_shape=(jax.ShapeDtypeStruct((t, c), F32), v, v), compiler_params=_cparams(("arbitrary",)),
    )(dzs, zc, g.reshape(1, c), b.reshape(1, c))


def _bucket_table():
    qi = np.arange(BLOCK)[:, None]
    kj = np.arange(2 * BLOCK)[None, :]
    off = qi + BLOCK - kj
    band = (off >= 0) & (off <= SUB_WINDOW)
    max_exact = NUM_BUCKETS // 2
    out = []
    for d in DILATIONS:
        dist = (np.clip(off, 0, SUB_WINDOW) * d).astype(np.int32)
        nf = np.maximum(dist, 1).astype(np.float32)
        large = max_exact + (np.log(nf / np.float32(max_exact)) / np.float32(math.log(MAX_REL_DISTANCE / max_exact))
                             * np.float32(NUM_BUCKETS - max_exact)).astype(np.int32)
        large = np.minimum(large, NUM_BUCKETS - 1)
        bucket = np.where(dist < max_exact, dist, large)
        out.append(np.where(band, bucket, -1))
    return np.stack(out).astype(np.int32)


def _bias_expand(rel_bias, buckets, hpg, after, name):
    nh = N_GROUPS * hpg

    def body(rb_ref, bk_ref, _, o_ref):
        h = pl.program_id(0)
        bk = bk_ref[0]
        acc = jnp.full((BLOCK, 2 * BLOCK), NEG_INF, F32)
        for bb in range(NUM_BUCKETS):
            acc = jnp.where(bk == bb, rb_ref[bb, h], acc)
        o_ref[0] = acc

    return pl.pallas_call(
        body, name=name, grid=(nh,),
        in_specs=[pl.BlockSpec(memory_space=pltpu.SMEM),
                  pl.BlockSpec((1, BLOCK, 2 * BLOCK), lambda h: (h // hpg, 0, 0)), pl.BlockSpec(memory_space=pl.ANY)],
        out_specs=pl.BlockSpec((1, BLOCK, 2 * BLOCK), lambda h: (h, 0, 0)),
        out_shape=jax.ShapeDtypeStruct((nh, BLOCK, 2 * BLOCK), F32), compiler_params=_cparams(("arbitrary",)),
    )(rel_bias, buckets, after)


def _bias_reduce(ds_sum, buckets, hpg, name):
    nh = N_GROUPS * hpg

    def body(ds_ref, bk_ref, o_ref):
        bk = bk_ref[0]
        dsv = ds_ref[0]
        lane = lax.broadcasted_iota(jnp.int32, (1, LANES), 1)
        row = jnp.zeros((1, LANES), F32)
        for bb in range(NUM_BUCKETS):
            tot = jnp.sum(jnp.sum(jnp.where(bk == bb, dsv, 0.0), axis=-1, keepdims=True), axis=0, keepdims=True)
            row = jnp.where(lane == bb, tot, row)
        o_ref[0] = row

    return pl.pallas_call(
        body, name=name, grid=(nh,),
        in_specs=[pl.BlockSpec((1, BLOCK, 2 * BLOCK), lambda h: (h, 0, 0)),
                  pl.BlockSpec((1, BLOCK, 2 * BLOCK), lambda h: (h // hpg, 0, 0))],
        out_specs=pl.BlockSpec((1, 1, LANES), lambda h: (h, 0, 0)),
        out_shape=jax.ShapeDtypeStruct((nh, 1, LANES), F32), compiler_params=_cparams(("arbitrary",)),
    )(ds_sum, buckets)


def _chunk_rows(c, d, nb):
    r, n = c // nb, c % nb
    if d == 1:
        return pl.ds(c * BLOCK, BLOCK)
    return pl.ds(r + n * BLOCK * d, BLOCK, stride=d)


def _segment_ones():
    i = lax.broadcasted_iota(jnp.int32, (LANES, LANES), 0) // HEAD_DIM
    j = lax.broadcasted_iota(jnp.int32, (LANES, LANES), 1) // HEAD_DIM
    return (i == j).astype(BF16)


def _segment_sum(v, seg):
    hi = v.astype(BF16)
    lo = (v - hi.astype(F32)).astype(BF16)
    return jnp.dot(hi, seg, preferred_element_type=F32) + jnp.dot(lo, seg, preferred_element_type=F32)


def _head_mean(v, seg):
    return _segment_sum(v, seg) * (1.0 / HEAD_DIM)


def _attn_fwd(u, qg2, kg2, bias, gi, cols, hpg, name):
    t = u.shape[0]
    d = DILATIONS[gi]
    nchunk = t // BLOCK
    nb = (t // d) // BLOCK
    hp = hpg // 2
    qc0, kc0, vc0 = [(c + gi * hpg * HEAD_DIM) // LANES for c in cols]
    contract_lanes = (((1,), (1,)), ((), ()))

    def body(q_ref, k_ref, v_ref, qg_ref, kg_ref, bias_ref, o_ref, lse_ref, qd, kd, vd, od, ld, sbuf):
        seg = _segment_ones()
        lane = lax.broadcasted_iota(jnp.int32, (1, LANES), 1)
        qg = qg_ref[...] * (HEAD_DIM ** -0.5)
        kg = kg_ref[...]
        kd[0:BLOCK, :] = jnp.zeros((BLOCK, LANES), BF16)
        vd[0:BLOCK, :] = jnp.zeros((BLOCK, LANES), BF16)
        od[...] = q_ref[...].astype(F32)
        ld[...] = k_ref[...].astype(F32)
        for c in range(nchunk):
            rows = _chunk_rows(c, d, nb)
            qv = od[rows, :]
            kv = ld[rows, :]
            qd[c * BLOCK:(c + 1) * BLOCK, :] = (qv * lax.rsqrt(_head_mean(qv * qv, seg) + EPS) * qg).astype(BF16)
            kd[(c + 1) * BLOCK:(c + 2) * BLOCK, :] = (kv * lax.rsqrt(_head_mean(kv * kv, seg) + EPS) * kg).astype(BF16)
        od[...] = v_ref[...].astype(F32)
        for c in range(nchunk):
            vd[(c + 1) * BLOCK:(c + 2) * BLOCK, :] = od[_chunk_rows(c, d, nb), :].astype(BF16)

        col = lax.broadcasted_iota(jnp.int32, (BLOCK, 2 * BLOCK), 1)
        for j in range(2):
            mj = jnp.logical_and(lane >= j * HEAD_DIM, lane < (j + 1) * HEAD_DIM)
            for c in range(nchunk):
                kw = kd[c * BLOCK:(c + 2) * BLOCK, :]
                kj = jnp.where(mj, kw, jnp.zeros_like(kw))
                s = lax.dot_general(qd[c * BLOCK:(c + 1) * BLOCK, :], kj, contract_lanes,
                                    preferred_element_type=F32) + bias_ref[j]
                if c % nb == 0:
                    s = jnp.where(col < BLOCK, NEG_INF, s)
                sbuf[c] = s
            for c in range(nchunk):
                rows = slice(c * BLOCK, (c + 1) * BLOCK)
                s = sbuf[c]
                mx = jnp.max(s, axis=-1, keepdims=True)
                p = jnp.exp(s - mx).astype(BF16)
                vw = vd[c * BLOCK:(c + 2) * BLOCK, :]
                oj = jnp.dot(p, jnp.where(mj, vw, jnp.ones_like(vw)), preferred_element_type=F32)
                l = pltpu.roll(oj, HEAD_DIM, 1)
                on = oj / l
                ls = mx + jnp.log(l)
                if j == 0:
                    od[rows, :] = on
                    ld[rows, :] = ls
                else:
                    od[rows, :] = jnp.where(mj, on, od[rows, :])
                    ld[rows, :] = jnp.where(mj, ls, ld[rows, :])

        for c in range(nchunk):
            rows = _chunk_rows(c, d, nb)
            o_ref[rows, :] = od[c * BLOCK:(c + 1) * BLOCK, :]
            lse_ref[rows, :] = ld[c * BLOCK:(c + 1) * BLOCK, :]

    ucol = lambda c0: pl.BlockSpec((t, LANES), lambda h: (0, c0 + h))
    vec = pl.BlockSpec((1, LANES), lambda h: (0, 0))
    oblk = pl.BlockSpec((t, LANES), lambda h: (0, h))
    osh = jax.ShapeDtypeStruct((t, hpg * HEAD_DIM), F32)
    return pl.pallas_call(
        body, name=name, grid=(hp,),
        in_specs=[ucol(qc0), ucol(kc0), ucol(vc0), vec, vec,
                  pl.BlockSpec((2, BLOCK, 2 * BLOCK), lambda h: (gi * hp + h, 0, 0))],
        out_specs=(oblk, oblk), out_shape=(osh, osh),
        scratch_shapes=[pltpu.VMEM((t, LANES), BF16), pltpu.VMEM((t + BLOCK, LANES), BF16),
                        pltpu.VMEM((t + BLOCK, LANES), BF16), pltpu.VMEM((t, LANES), F32), pltpu.VMEM((t, LANES), F32),
                        pltpu.VMEM((nchunk, BLOCK, 2 * BLOCK), F32)],
        compiler_params=_cparams(("arbitrary",)),
    )(u, u, u, qg2, kg2, bias)


def _attn_bwd(u, do_g, dd_g, lse_g, qg2, kg2, bias, ds_in, du_in, gi, cols, hpg, name):
    t = u.shape[0]
    d = DILATIONS[gi]
    nchunk = t // BLOCK
    nb = (t // d) // BLOCK
    hp = hpg // 2
    qc0, kc0, vc0 = [(c + gi * hpg * HEAD_DIM) // LANES for c in cols]
    contract_lanes = (((1,), (1,)), ((), ()))
    contract_rows = (((0,), (0,)), ((), ()))
    qscale = HEAD_DIM ** -0.5

    def body(q_ref, k_ref, v_ref, do_ref, dd_ref, lse_ref, qg_ref, kg_ref, bias_ref, dsin_ref, _du_in,
             dgq_ref, dgk_ref, dsout_ref, du_ref,
             qd, kd, vd, dod, ddd, ld, dqd, dkd, dvd, dsacc, pbuf, dsbuf, qs, ks, qst, kst, vst, out_sems):
        h = pl.program_id(0)

        def flush(step):
            return [pltpu.make_async_copy(
                st, du_ref.at[:, pl.ds(pl.multiple_of((c0 + step) * LANES, LANES), LANES)], out_sems.at[i])
                for i, (st, c0) in enumerate(((qst, qc0), (kst, kc0), (vst, vc0)))]

        seg = _segment_ones()
        lane = lax.broadcasted_iota(jnp.int32, (1, LANES), 1)
        qg = qg_ref[...] * qscale
        kg = kg_ref[...]
        kd[0:BLOCK, :] = jnp.zeros((BLOCK, LANES), BF16)
        vd[0:BLOCK, :] = jnp.zeros((BLOCK, LANES), BF16)
        dsacc[...] = jnp.zeros_like(dsacc)
        qs[...] = q_ref[...].astype(F32)
        ks[...] = k_ref[...].astype(F32)
        dqd[...] = v_ref[...].astype(F32)
        for c in range(nchunk):
            rows = _chunk_rows(c, d, nb)
            qv = qs[rows, :]
            kv = ks[rows, :]
            qd[c * BLOCK:(c + 1) * BLOCK, :] = (qv * lax.rsqrt(_head_mean(qv * qv, seg) + EPS) * qg).astype(BF16)
            kd[(c + 1) * BLOCK:(c + 2) * BLOCK, :] = (kv * lax.rsqrt(_head_mean(kv * kv, seg) + EPS) * kg).astype(BF16)
            vd[(c + 1) * BLOCK:(c + 2) * BLOCK, :] = dqd[rows, :].astype(BF16)
            dod[c * BLOCK:(c + 1) * BLOCK, :] = do_ref[rows, :].astype(BF16)
            ddd[c * BLOCK:(c + 1) * BLOCK, :] = dd_ref[rows, :]
            ld[c * BLOCK:(c + 1) * BLOCK, :] = lse_ref[rows, :]

        col = lax.broadcasted_iota(jnp.int32, (BLOCK, 2 * BLOCK), 1)
        for j in range(2):
            mj = jnp.logical_and(lane >= j * HEAD_DIM, lane < (j + 1) * HEAD_DIM)
            first = lane == j * HEAD_DIM
            for c in range(nchunk):
                rows = slice(c * BLOCK, (c + 1) * BLOCK)
                kw = kd[c * BLOCK:(c + 2) * BLOCK, :]
                vw = vd[c * BLOCK:(c + 2) * BLOCK, :]
                kj = jnp.where(mj, kw, jnp.zeros_like(kw))
                vj = jnp.where(mj, vw, jnp.zeros_like(vw))
                s = lax.dot_general(qd[rows, :], kj, contract_lanes, preferred_element_type=F32) + bias_ref[j]
                if c % nb == 0:
                    s = jnp.where(col < BLOCK, NEG_INF, s)
                dp = lax.dot_general(dod[rows, :], vj, contract_lanes, preferred_element_type=F32)
                lse_j = jnp.sum(jnp.where(first, ld[rows, :], 0.0), axis=-1, keepdims=True)
                dd_j = jnp.sum(jnp.where(first, ddd[rows, :], 0.0), axis=-1, keepdims=True)
                p = jnp.exp(s - lse_j)
                ds = p * (dp + dd_j)
                dsacc[j] += ds
                pbuf[j, c] = p.astype(BF16)
                dsbuf[j, c] = ds.astype(BF16)
        for c in range(nchunk):
            rows = slice(c * BLOCK, (c + 1) * BLOCK)
            has_next = c + 1 < nchunk and (c + 1) % nb != 0
            kw = kd[c * BLOCK:(c + 2) * BLOCK, :]
            dq = jnp.zeros((BLOCK, LANES), F32)
            dk = jnp.zeros((BLOCK, LANES), F32)
            dv = jnp.zeros((BLOCK, LANES), F32)
            both = slice(c * BLOCK, (c + 2) * BLOCK) if has_next else rows
            for j in range(2):
                mj = jnp.logical_and(lane >= j * HEAD_DIM, lane < (j + 1) * HEAD_DIM)
                dq = dq + jnp.dot(dsbuf[j, c], jnp.where(mj, kw, jnp.zeros_like(kw)), preferred_element_type=F32)
                dsk = dsbuf[j, c, :, BLOCK:]
                pk = pbuf[j, c, :, BLOCK:]
                if has_next:
                    dsk = jnp.concatenate([dsk, dsbuf[j, c + 1, :, :BLOCK]], axis=0)
                    pk = jnp.concatenate([pk, pbuf[j, c + 1, :, :BLOCK]], axis=0)
                qq = qd[both, :]
                dd = dod[both, :]
                dk = dk + lax.dot_general(dsk, jnp.where(mj, qq, jnp.zeros_like(qq)), contract_rows,
                                          preferred_element_type=F32)
                dv = dv + lax.dot_general(pk, jnp.where(mj, dd, jnp.zeros_like(dd)), contract_rows,
                                          preferred_element_type=F32)
            dqd[rows, :] = dq
            dkd[rows, :] = dk
            dvd[rows, :] = dv

        dsout_ref[...] = dsin_ref[...] + dsacc[...]

        dgq = jnp.zeros((1, LANES), F32)
        dgk = jnp.zeros((1, LANES), F32)
        for c in range(nchunk):
            rows = _chunk_rows(c, d, nb)
            qv = qs[rows, :]
            rq = lax.rsqrt(_head_mean(qv * qv, seg) + EPS)
            qh = qv * rq
            dy = dqd[c * BLOCK:(c + 1) * BLOCK, :]
            dgq = dgq + jnp.sum(dy * qh, axis=0, keepdims=True) * qscale
            dxh = dy * qg
            ddd[rows, :] = rq * (dxh - qh * _head_mean(dxh * qh, seg))
            kv = ks[rows, :]
            rk = lax.rsqrt(_head_mean(kv * kv, seg) + EPS)
            kh = kv * rk
            dy = dkd[c * BLOCK:(c + 1) * BLOCK, :]
            dgk = dgk + jnp.sum(dy * kh, axis=0, keepdims=True)
            dxh = dy * kg
            ld[rows, :] = rk * (dxh - kh * _head_mean(dxh * kh, seg))
        @pl.when(h > 0)
        def _():
            for cp in flush(h - 1):
                cp.wait()

        qst[...] = ddd[...].astype(BF16)
        kst[...] = ld[...].astype(BF16)
        for c in range(nchunk):
            ddd[_chunk_rows(c, d, nb), :] = dvd[c * BLOCK:(c + 1) * BLOCK, :]
        vst[...] = ddd[...].astype(BF16)
        for cp in flush(h):
            cp.start()

        @pl.when(h == hp - 1)
        def _():
            for cp in flush(h):
                cp.wait()

        dgq_ref[0] = dgq
        dgk_ref[0] = dgk

    ucol = lambda c0: pl.BlockSpec((t, LANES), lambda h: (0, c0 + h))
    vec = pl.BlockSpec((1, LANES), lambda h: (0, 0))
    oblk = pl.BlockSpec((t, LANES), lambda h: (0, h))
    bblk = pl.BlockSpec((2, BLOCK, 2 * BLOCK), lambda h: (gi * hp + h, 0, 0))
    gblk = pl.BlockSpec((1, 1, LANES), lambda h: (h, 0, 0))
    gsh = jax.ShapeDtypeStruct((hp, 1, LANES), F32)
    hbm = pl.BlockSpec(memory_space=pl.ANY)
    return pl.pallas_call(
        body, name=name, grid=(hp,),
        in_specs=[ucol(qc0), ucol(kc0), ucol(vc0), oblk, oblk, oblk, vec, vec, bblk, bblk, hbm],
        out_specs=(gblk, gblk, bblk, hbm),
        out_shape=(gsh, gsh, jax.ShapeDtypeStruct(ds_in.shape, F32), jax.ShapeDtypeStruct(du_in.shape, BF16)),
        input_output_aliases={9: 2, 10: 3},
        scratch_shapes=[pltpu.VMEM((t, LANES), BF16), pltpu.VMEM((t + BLOCK, LANES), BF16),
                        pltpu.VMEM((t + BLOCK, LANES), BF16), pltpu.VMEM((t, LANES), BF16),
                        pltpu.VMEM((t, LANES), F32), pltpu.VMEM((t, LANES), F32), pltpu.VMEM((t, LANES), F32),
                        pltpu.VMEM((t, LANES), F32), pltpu.VMEM((t, LANES), F32),
                        pltpu.VMEM((2, BLOCK, 2 * BLOCK), F32), pltpu.VMEM((2, nchunk, BLOCK, 2 * BLOCK), BF16),
                        pltpu.VMEM((2, nchunk, BLOCK, 2 * BLOCK), BF16), pltpu.VMEM((t, LANES), F32),
                        pltpu.VMEM((t, LANES), F32), pltpu.VMEM((t, LANES), BF16), pltpu.VMEM((t, LANES), BF16),
                        pltpu.VMEM((t, LANES), BF16), pltpu.SemaphoreType.DMA((3,))],
        compiler_params=_cparams(("arbitrary",)),
    )(u, u, u, do_g, dd_g, lse_g, qg2, kg2, bias, ds_in, du_in)


def _group_weights(l0, l1, l2):
    mx = jnp.maximum(jnp.maximum(l0, l1), l2)
    e0, e1, e2 = jnp.exp(l0 - mx), jnp.exp(l1 - mx), jnp.exp(l2 - mx)
    inv = 1.0 / (e0 + e1 + e2)
    return e0 * inv, e1 * inv, e2 * inv


def _combine_fwd(os_, lses, name):
    t, ao = os_[0].shape
    ROWS = _rows(t, 26 * ao)

    def body(o0, o1, o2, l0, l1, l2, o_ref):
        w0, w1, w2 = _group_weights(l0[...], l1[...], l2[...])
        o_ref[...] = (w0 * o0[...] + w1 * o1[...] + w2 * o2[...]).astype(BF16)

    row = pl.BlockSpec((ROWS, ao), lambda i: (i, 0))
    return pl.pallas_call(
        body, name=name, grid=(t // ROWS,), in_specs=[row] * 6, out_specs=row,
        out_shape=jax.ShapeDtypeStruct((t, ao), BF16), compiler_params=_cparams(("arbitrary",)),
    )(*os_, *lses)


def _combine_bwd(do, os_, lses, name):
    t, ao = do.shape
    idx = np.arange(ao) // HEAD_DIM
    seg = jnp.asarray((idx[:, None] == idx[None, :]).astype(np.float32), dtype=BF16)
    ROWS = _rows(t, 52 * ao)

    def body(do_ref, o0, o1, o2, l0, l1, l2, seg_ref, g0, g1, g2, d0, d1, d2):
        w0, w1, w2 = _group_weights(l0[...], l1[...], l2[...])
        dov = do_ref[...]
        o = w0 * o0[...] + w1 * o1[...] + w2 * o2[...]
        sd = _segment_sum(dov * o, seg_ref[...])
        for w, gref, dref in ((w0, g0, d0), (w1, g1, d1), (w2, g2, d2)):
            gref[...] = w * dov
            dref[...] = -(w * sd)

    row = pl.BlockSpec((ROWS, ao), lambda i: (i, 0))
    sh = jax.ShapeDtypeStruct((t, ao), F32)
    outs = pl.pallas_call(
        body, name=name, grid=(t // ROWS,), in_specs=[row] * 7 + [pl.BlockSpec((ao, ao), lambda i: (0, 0))],
        out_specs=(row,) * 6, out_shape=(sh,) * 6, compiler_params=_cparams(("arbitrary",)),
    )(do, *os_, *lses, seg)
    return outs[:3], outs[3:]


def _adamw(w, g, m, v, name):
    shape = w.shape
    cols = shape[-1]
    rows = int(np.prod(shape[:-1]))
    tr = rows if rows <= 512 else _tile_rows(rows)
    c1 = 1.0 - ADAM_B1 ** ADAM_STEP
    c2 = 1.0 - ADAM_B2 ** ADAM_STEP

    def body(w_ref, g_ref, m_ref, v_ref, d_ref, nm_ref, nv_ref):
        gv = g_ref[...]
        mn = ADAM_B1 * m_ref[...] + (1.0 - ADAM_B1) * gv
        vn = ADAM_B2 * v_ref[...] + (1.0 - ADAM_B2) * (gv * gv)
        nm_ref[...] = mn
        nv_ref[...] = vn
        d_ref[...] = -ADAM_LR * ((mn / c1) / (jnp.sqrt(vn / c2) + ADAM_EPS) + ADAM_WD * w_ref[...])

    blk = pl.BlockSpec((tr, cols), lambda i: (i, 0))
    sh = jax.ShapeDtypeStruct((rows, cols), F32)
    outs = pl.pallas_call(
        body, name=name, grid=(rows // tr,), in_specs=[blk] * 4, out_specs=(blk,) * 3, out_shape=(sh,) * 3,
        compiler_params=_cparams(("arbitrary",)),
    )(*[a.reshape(rows, cols) for a in (w, g, m, v)])
    return tuple(o.reshape(shape) for o in outs)


def _tile_rows(rows):
    for t in (512, 256, 128, 64, 32, 16, 8):
        if rows % t == 0:
            return t
    return rows


def _sum_slots(recv, parts, me, layers, l, name):
    _, rows, cols = recv.shape
    tr = rows if rows <= 512 else _tile_rows(rows)

    def body(me_ref, r_ref, own_ref, _, o_ref):
        acc = jnp.zeros(o_ref.shape, F32)
        for s in range(N_DEV):
            acc = acc + jnp.where(me_ref[0] == s, own_ref[...], r_ref[s]).astype(F32)
        o_ref[...] = acc

    return pl.pallas_call(
        body, name=name,
        grid_spec=pltpu.PrefetchScalarGridSpec(
            num_scalar_prefetch=1, grid=(rows // tr,),
            in_specs=[pl.BlockSpec((N_DEV, tr, cols), lambda i, me: (0, i, 0)),
                      pl.BlockSpec((None, tr, cols), lambda i, me: (me[0], i, 0)),
                      pl.BlockSpec(memory_space=pl.ANY)],
            out_specs=pl.BlockSpec((None, tr, cols), lambda i, me: (l, i, 0))),
        out_shape=jax.ShapeDtypeStruct(layers.shape, F32), input_output_aliases={3: 0},
        compiler_params=_cparams(("arbitrary",)),
    )(me.reshape(1), recv, parts, layers)


def _peer(k):
    x, y, c = lax.axis_index("x"), lax.axis_index("y"), lax.axis_index("c")
    return (1 - x if k & 4 else x, 1 - y if k & 2 else y, 1 - c if k & 1 else c)


def _dev_index(p):
    return 4 * p[0] + 2 * p[1] + p[2]


HBM_SPEC = pl.BlockSpec(memory_space=pltpu.HBM)
SEM_SPEC = pl.BlockSpec(memory_space=pltpu.SEMAPHORE)
ANY_SPEC = pl.BlockSpec(memory_space=pl.ANY)
CHIPS = (4, 2, 6)


def _remote(src, dst, send_sem, recv_sem, to):
    return pltpu.make_async_remote_copy(src_ref=src, dst_ref=dst, send_sem=send_sem, recv_sem=recv_sem,
                                        device_id=to, device_id_type=MESH)


def _hbm(a):
    return pltpu.with_memory_space_constraint(a, pltpu.HBM)


def _split_call(body, name, bufs, sems_in, sem_out_sizes, after):
    nb, ns, no = len(bufs), len(sems_in), len(sem_out_sizes)
    extra = [] if after is None else list(after) if isinstance(after, (tuple, list)) else [after]

    def kern(*refs):
        pos = nb + ns + len(extra)
        body(refs[:nb], refs[nb:nb + ns], refs[pos:pos + no])
        token_ref = refs[pos + no + nb]
        token_ref[...] = jnp.zeros_like(token_ref)

    out_shape = (tuple(pltpu.SemaphoreType.DMA((s,)) for s in sem_out_sizes)
                 + tuple(pltpu.HBM(b.shape, b.dtype) for b in bufs) + (jax.ShapeDtypeStruct((8, LANES), F32),))
    res = pl.pallas_call(
        kern, name=name, out_shape=out_shape,
        in_specs=[HBM_SPEC] * nb + [SEM_SPEC] * ns + [ANY_SPEC] * len(extra),
        out_specs=(SEM_SPEC,) * no + (HBM_SPEC,) * nb + (pl.BlockSpec(memory_space=pltpu.VMEM),),
        input_output_aliases={i: no + i for i in range(nb)},
        compiler_params=pltpu.CompilerParams(has_side_effects=pltpu.SideEffectType.DATAFLOW_SIDE_EFFECTING),
    )(*bufs, *sems_in, *extra)
    return res[:no], res[no:no + nb], res[no + nb]


def _gather_start(shards, lands, after, name):
    n = len(shards)

    def body(bufs, _, sems):
        ins, lnd = bufs[:n], bufs[n:]
        d2d_s, d2d_r, ici_s, ici_r = sems
        me = _dev_index(_peer(0))
        for j, k in enumerate(CHIPS):
            for i in range(n):
                _remote(ins[i], lnd[i].at[me], ici_s.at[j], ici_r.at[j], _peer(k)).start()
        for i in range(n):
            _remote(ins[i], lnd[i].at[me], d2d_s.at[0], d2d_r.at[0], _peer(1)).start()

    return _split_call(body, name, [_hbm(a) for a in (*shards, *lands)], [], (1, 1, 3, 3), after)


def _gather_forward(n, bufs, ici_r, after, name):
    def body(refs, sems_in, sems):
        ins, lnd = refs[:n], refs[n:]
        (arrived,) = sems_in
        fwd_s, fwd_r = sems
        for j, k in enumerate(CHIPS):
            blk = _dev_index(_peer(k))
            for i in range(n):
                _remote(ins[i], lnd[i].at[blk], fwd_s.at[j], arrived.at[j], _peer(k)).wait_recv()
            for i in range(n):
                _remote(lnd[i].at[blk], lnd[i].at[blk], fwd_s.at[j], fwd_r.at[j], _peer(1)).start()

    return _split_call(body, name, bufs, [ici_r], (3, 3), after)


def _gather_finish(n, bufs, d2d_s, d2d_r, ici_s, fwd_s, fwd_r, after, name):
    def body(refs, sems_in, _):
        ins, lnd = refs[:n], refs[n:]
        d2d_send, d2d_recv, ici_send, fwd_send, fwd_recv = sems_in
        sib = _peer(1)
        for i in range(n):
            cp = _remote(ins[i], lnd[i].at[_dev_index(sib)], d2d_send.at[0], d2d_recv.at[0], sib)
            cp.wait_send()
            cp.wait_recv()
        for j, k in enumerate(CHIPS):
            passed = _dev_index(_peer(k))
            landed = _dev_index(_peer(k | 1))
            for i in range(n):
                _remote(ins[i], lnd[i].at[passed], ici_send.at[j], fwd_recv.at[j], _peer(k)).wait_send()
                cp = _remote(lnd[i].at[passed], lnd[i].at[landed], fwd_send.at[j], fwd_recv.at[j], sib)
                cp.wait_send()
                cp.wait_recv()

    _, out, token = _split_call(body, name, bufs, [d2d_s, d2d_r, ici_s, fwd_s, fwd_r], (), after)
    return out[n:], token


def _exchange_start(parts, lands, after, name):
    n = len(parts)

    def body(bufs, _, sems):
        src, lnd = bufs[:n], bufs[n:]
        send, recv = sems
        me = _dev_index(_peer(0))
        for k in (4, 5, 2, 3, 6, 7, 1):
            to = _peer(k)
            for i in range(n):
                _remote(src[i].at[_dev_index(to)], lnd[i].at[me], send.at[k - 1], recv.at[k - 1], to).start()

    return _split_call(body, name, [_hbm(a) for a in (*parts, *lands)], [], (7, 7), after)


def _exchange_finish(n, bufs, send, recv, after, name):
    def body(refs, sems_in, _):
        src, lnd = refs[:n], refs[n:]
        send_, recv_ = sems_in
        me = _dev_index(_peer(0))
        for k in range(1, N_DEV):
            frm = _peer(k)
            for i in range(n):
                cp = _remote(src[i].at[me], lnd[i].at[_dev_index(frm)], send_.at[k - 1], recv_.at[k - 1], frm)
                cp.wait_send()
                cp.wait_recv()

    _, out, token = _split_call(body, name, bufs, [send, recv], (), after)
    return out[:n], out[n:], token


def _all_reduce_small(v, after, name):
    rows = v.shape[0]

    def body(v_ref, _, o_ref, buf, send_sems, recv_sems):
        me = _dev_index(_peer(0))
        buf[me] = v_ref[...]
        copies = []
        for k in range(1, N_DEV):
            copies.append(pltpu.make_async_remote_copy(
                src_ref=v_ref, dst_ref=buf.at[me], send_sem=send_sems.at[k - 1], recv_sem=recv_sems.at[k - 1],
                device_id=_peer(k), device_id_type=MESH))
        for cp in copies:
            cp.start()
        for k in range(1, N_DEV):
            pltpu.make_async_remote_copy(
                src_ref=v_ref, dst_ref=buf.at[_dev_index(_peer(k))], send_sem=send_sems.at[k - 1],
                recv_sem=recv_sems.at[k - 1], device_id=_peer(k), device_id_type=MESH).wait_recv()
        for cp in copies:
            cp.wait_send()
        acc = buf[0]
        for s in range(1, N_DEV):
            acc = acc + buf[s]
        o_ref[...] = acc

    vm = pl.BlockSpec(memory_space=pltpu.VMEM)
    return pl.pallas_call(
        body, name=name, in_specs=[vm, pl.BlockSpec(memory_space=pl.ANY)], out_specs=vm,
        out_shape=jax.ShapeDtypeStruct(v.shape, F32),
        scratch_shapes=[pltpu.VMEM((N_DEV, rows, LANES), F32), pltpu.SemaphoreType.DMA((7,)),
                        pltpu.SemaphoreType.DMA((7,))],
    )(v, after)


def _columns(cdim, ao):
    q_col = 2 * cdim
    attn_dim = N_GROUPS * ao
    return (q_col, q_col + attn_dim, q_col + 2 * attn_dim), q_col + 3 * attn_dim


def _layer_fwd(x, h1, sm, bg, get_rest, bias, hpg, next_gain, at_ff1=None):
    cdim = sm["conv_ln_g"].shape[0]
    ao = hpg * HEAD_DIM
    cols, gate_col = _columns(cdim, ao)
    qg2 = jnp.tile(sm["q_norm_g"], 2).reshape(1, LANES)
    kg2 = jnp.tile(sm["k_norm_g"], 2).reshape(1, LANES)
    u = _mm_in_pairs(h1, bg["w_in"], "mm_in")
    zc = _conv_fwd(u, bg["conv_dw_w"], sm["conv_dw_b"], cdim, "conv_fwd")
    zs = _ln_swish_fwd(zc, sm["conv_ln_g"], sm["conv_ln_b"], "ln_swish_fwd")
    os_, lses = [], []
    for gi in range(N_GROUPS):
        o_g, lse_g = _attn_fwd(u, qg2, kg2, bias, gi, cols, hpg, "attn_fwd_g%d" % gi)
        os_.append(o_g)
        lses.append(lse_g)
    o = _combine_fwd(os_, lses, "combine_fwd")
    bg = {**bg, **get_rest(o)}
    mg, yc, ya = _gated_out(zs, o, bg["w_conv_out"], bg["w_attn_out"], u, gate_col, "gated_out")
    x1, h2 = _mm(mg, bg["w_out"], epi="res_rms", extra=x, gain=sm["norm2_g"], name="mm_out")
    f = _mm(h2, bg["w_ff1"], out_dtype=BF16, name="mm_ff1")
    after = at_ff1(f) if at_ff1 is not None else None
    if next_gain is None:
        x2, h_next = _mm(f, bg["w_ff2"], a_relu2=True, epi="res", extra=x1, after=after, name="mm_ff2"), None
    else:
        x2, h_next = _mm(f, bg["w_ff2"], a_relu2=True, epi="res_rms", extra=x1, gain=next_gain, after=after,
                         name="mm_ff2")
    saved = dict(x=x, h1=h1, u=u, zc=zc, zs=zs, yc=yc, os=os_, lses=lses, o=o, ya=ya, mg=mg, x1=x1, h2=h2, f=f,
                 qg2=qg2, kg2=kg2)
    return x2, h_next, saved, bg


GRAD_GROUPS = (("w_ff2", "w_ff1"), ("w_out", "w_conv_out", "w_attn_out", "conv_dw_w"), ("w_in",))


def _layer_bwd(dx, s, sm, bg, bias, ds_sum, after, emit):
    cdim = sm["conv_ln_g"].shape[0]
    ao = bg["w_attn_out"].shape[0]
    hpg = ao // HEAD_DIM
    cols, gate_col = _columns(cdim, ao)
    g = {}
    df = _mm(dx, bg["w_ff2"], tb=True, epi="drelu2", extra=s["f"], out_dtype=BF16, after=after, name="mm_dff2")
    g["w_ff2"] = _mm(s["f"], dx, ta=True, a_relu2=True, out_dtype=BF16, name="mm_gw_ff2")
    g["w_ff1"] = _mm(s["h2"], df, ta=True, out_dtype=BF16, out_slots=True, name="mm_gw_ff1")
    after = emit(GRAD_GROUPS[0], g)
    dx1, dg2 = _mm(df, bg["w_ff1"], tb=True, epi="rms_bwd", extra=(s["x1"], dx), gain=sm["norm2_g"], after=after,
                   name="mm_dff1")
    g["norm2_g"] = dg2[0]
    dmg = _mm(dx1, bg["w_out"], tb=True, name="mm_dout")
    g["w_out"] = _mm(s["mg"], dx1, ta=True, out_dtype=BF16, name="mm_gw_out")
    dyc, dya, dugc, duga = _gate_bwd(dmg, s["u"], s["yc"], s["ya"], gate_col, "gate_bwd")
    dzs = _mm(dyc, bg["w_conv_out"], tb=True, name="mm_dconv_out")
    g["w_conv_out"] = _mm(s["zs"], dyc, ta=True, out_dtype=BF16, name="mm_gw_conv_out")
    do = _mm(dya, bg["w_attn_out"], tb=True, name="mm_dattn_out")
    g["w_attn_out"] = _mm(s["o"], dya, ta=True, out_dtype=BF16, name="mm_gw_attn_out")
    dzc, dlg, dlb = _ln_swish_bwd(dzs, s["zc"], sm["conv_ln_g"], sm["conv_ln_b"], "ln_swish_bwd")
    g["conv_ln_g"] = dlg[0]
    g["conv_ln_b"] = dlb[0]
    da, dgt, dcw, dcb = _conv_bwd(dzc, s["u"], bg["conv_dw_w"], cdim, "conv_bwd")
    g["conv_dw_w"] = dcw[:CONV_WIDTH].astype(BF16)
    g["conv_dw_b"] = dcb[0]
    after = emit(GRAD_GROUPS[1], g)
    do_gs, dd_gs = _combine_bwd(do, s["os"], s["lses"], "combine_bwd")
    du = lax.empty(s["u"].shape, BF16)
    for col, piece in ((0, da), (cdim, dgt), (gate_col, dugc), (gate_col + dugc.shape[1], duga)):
        du = lax.dynamic_update_slice(du, piece, (0, col))
    gqs, gks = [], []
    for gi in range(N_GROUPS):
        gq, gk, ds_sum, du = _attn_bwd(s["u"], do_gs[gi], dd_gs[gi], s["lses"][gi], s["qg2"], s["kg2"], bias,
                                       ds_sum, du, gi, cols, hpg, "attn_bwd_g%d" % gi)
        gqs.append(gq)
        gks.append(gk)
    g["q_norm_g"] = jnp.concatenate(gqs)
    g["k_norm_g"] = jnp.concatenate(gks)
    g["w_in"] = _mm_gw_in_pairs(s["h1"], du, after, "mm_gw_in")
    after = emit(GRAD_GROUPS[2], g)
    dx0, dg1 = _mm_din_pairs(du, bg["w_in"], s["x"], sm["norm1_g"], dx1, after, "mm_din")
    g["norm1_g"] = dg1[0]
    return dx0, g, ds_sum


BIG = ("w_in", "conv_dw_w", "w_conv_out", "w_attn_out", "w_out", "w_ff1", "w_ff2")
COL_SHARDED = ("w_in", "conv_dw_w", "w_conv_out", "w_attn_out", "w_ff1")
SMALL = ("rel_bias", "norm1_g", "q_norm_g", "k_norm_g", "conv_dw_b", "conv_ln_g", "conv_ln_b", "norm2_g")
WEIGHTS = ("rel_bias", "norm1_g", "w_in", "q_norm_g", "k_norm_g", "conv_dw_w", "conv_dw_b", "conv_ln_g", "conv_ln_b",
           "w_conv_out", "w_attn_out", "w_out", "norm2_g", "w_ff1", "w_ff2")


def _to_whole(name, gathered):
    n, a, b = gathered.shape
    if name in COL_SHARDED:
        return gathered.transpose(1, 0, 2).reshape(a, n * b)
    return gathered.reshape(n * a, b)


def _to_slots(name, whole):
    a, b = whole.shape
    if name in COL_SHARDED:
        return whole.reshape(a, N_DEV, b // N_DEV).transpose(1, 0, 2)
    return whole.reshape(N_DEV, a // N_DEV, b)


def _own_slot(block, me):
    land = lax.empty((N_DEV,) + block.shape, block.dtype)
    return lax.dynamic_update_slice(land, block[None], (me,) + (0,) * block.ndim)


def kernel(x, rel_bias, norm1_g, w_in, q_norm_g, k_norm_g, conv_dw_w, conv_dw_b, conv_ln_g, conv_ln_b, w_conv_out, w_attn_out, w_out, norm2_g, w_ff1, w_ff2, loss_target, m_rel_bias, m_norm1_g, m_w_in, m_q_norm_g, m_k_norm_g, m_conv_dw_w, m_conv_dw_b, m_conv_ln_g, m_conv_ln_b, m_w_conv_out, m_w_attn_out, m_w_out, m_norm2_g, m_w_ff1, m_w_ff2, v_rel_bias, v_norm1_g, v_w_in, v_q_norm_g, v_k_norm_g, v_conv_dw_w, v_conv_dw_b, v_conv_ln_g, v_conv_ln_b, v_w_conv_out, v_w_attn_out, v_w_out, v_norm2_g, v_w_ff1, v_w_ff2):
    w = dict(rel_bias=rel_bias, norm1_g=norm1_g, w_in=w_in, q_norm_g=q_norm_g, k_norm_g=k_norm_g, conv_dw_w=conv_dw_w,
             conv_dw_b=conv_dw_b, conv_ln_g=conv_ln_g, conv_ln_b=conv_ln_b, w_conv_out=w_conv_out,
             w_attn_out=w_attn_out, w_out=w_out, norm2_g=norm2_g, w_ff1=w_ff1, w_ff2=w_ff2)
    mom = dict(rel_bias=m_rel_bias, norm1_g=m_norm1_g, w_in=m_w_in, q_norm_g=m_q_norm_g, k_norm_g=m_k_norm_g,
               conv_dw_w=m_conv_dw_w, conv_dw_b=m_conv_dw_b, conv_ln_g=m_conv_ln_g, conv_ln_b=m_conv_ln_b,
               w_conv_out=m_w_conv_out, w_attn_out=m_w_attn_out, w_out=m_w_out, norm2_g=m_norm2_g, w_ff1=m_w_ff1,
               w_ff2=m_w_ff2)
    var = dict(rel_bias=v_rel_bias, norm1_g=v_norm1_g, w_in=v_w_in, q_norm_g=v_q_norm_g, k_norm_g=v_k_norm_g,
               conv_dw_w=v_conv_dw_w, conv_dw_b=v_conv_dw_b, conv_ln_g=v_conv_ln_g, conv_ln_b=v_conv_ln_b,
               w_conv_out=v_w_conv_out, w_attn_out=v_w_attn_out, w_out=v_w_out, norm2_g=v_norm2_g, w_ff1=v_w_ff1,
               w_ff2=v_w_ff2)

    depth = norm1_g.shape[0]
    me = 4 * lax.axis_index("x") + 2 * lax.axis_index("y") + lax.axis_index("c")
    odd_core = lax.axis_index("c") == 1
    hpg = w_attn_out.shape[1] // HEAD_DIM
    buckets = jnp.asarray(_bucket_table())

    first_names = ("w_in", "conv_dw_w")
    rest_names = tuple(k for k in BIG if k not in first_names)

    prepared = {}

    def prepare(l, names):
        shards = [w[k][l] if k == "conv_dw_w" else w[k][l].astype(BF16) for k in names]
        if "w_in" in names:
            i = names.index("w_in")
            shards[i] = jnp.where(odd_core, jnp.pad(shards[i], ((0, 0), (SHIFT, 0))),
                                  jnp.pad(shards[i], ((0, 0), (0, SHIFT))))
        prepared[l, names] = (shards, [_own_slot(s, me) for s in shards])
        return prepared[l, names]

    def chain_start(l, names, after):
        shards, lands = prepared.pop((l, names), None) or prepare(l, names)
        sems, bufs, token = _gather_start(shards, lands, after, "gather_start_%s_l%d" % (names[0], l))
        return dict(l=l, names=names, sems=sems, bufs=bufs, token=token)

    def chain_forward(ch, after):
        fwd, bufs, token = _gather_forward(len(ch["names"]), ch["bufs"], ch["sems"][3], after,
                                           "gather_forward_%s_l%d" % (ch["names"][0], ch["l"]))
        ch.update(fwd=fwd, bufs=bufs)
        return token

    def chain_finish(ch, after):
        d2d_s, d2d_r, ici_s, _ = ch["sems"]
        gathered, _ = _gather_finish(len(ch["names"]), ch["bufs"], d2d_s, d2d_r, ici_s, ch["fwd"][0], ch["fwd"][1],
                                     after, "gather_finish_%s_l%d" % (ch["names"][0], ch["l"]))
        out = {k: a if k == "w_in" else _to_whole(k, a) for k, a in zip(ch["names"], gathered)}
        if "conv_dw_w" in out:
            out["conv_dw_w"] = jnp.pad(out["conv_dw_w"], ((0, CONV_TAPS_PADDED - CONV_WIDTH), (0, 0)))
        return out

    xs = x[0]
    first = chain_start(0, first_names, None)
    h1 = _rms_fwd(xs, norm1_g[0], first["token"], "rms1_fwd")
    bias = _bias_expand(rel_bias, buckets, hpg, h1, "bias_expand")
    ahead = [prepare(0, rest_names)] + [prepare(l, BIG) for l in range(2, depth)]
    if depth > 1:
        ahead += [prepare(1, first_names), prepare(1, rest_names)]
    early = tuple(land for _, lands in ahead for land in lands)
    saved, bigs, smalls = [], [], []
    chains = {}
    for l in range(depth):
        sm = {k: w[k][l] for k in SMALL if k != "rel_bias"}
        if l == 0:
            token = chain_forward(first, (bias,) + early)
            rest = chain_start(0, rest_names, token)
            bg = chain_finish(first, rest["token"])

            def get_rest(o, rest=rest):
                token = chain_forward(rest, o)
                if depth > 1:
                    chains[1] = (chain_start(1, first_names, token),)
                    chains[1] += (chain_start(1, rest_names, chains[1][0]["token"]),)
                    token = chains[1][1]["token"]
                return chain_finish(rest, token)
        elif l == 1:
            first, rest = chains[1]
            token = chain_forward(first, xs)
            if depth > 2:
                chains[2] = chain_start(2, BIG, token)
                token = chains[2]["token"]
            bg = chain_finish(first, token)

            def get_rest(o, rest=rest):
                return chain_finish(rest, chain_forward(rest, o))
        else:
            token = xs if "fwd" in chains[l] else chain_forward(chains[l], xs)
            if l + 1 < depth:
                chains[l + 1] = chain_start(l + 1, BIG, token)
                token = chains[l + 1]["token"]
            whole = chain_finish(chains[l], token)
            bg = {k: whole[k] for k in first_names}

            def get_rest(o, whole=whole):
                return {k: whole[k] for k in rest_names}
        at_ff1 = (lambda f, nxt=l + 1: chain_forward(chains[nxt], f)) if 2 <= l < depth - 1 else None
        xs, h1, sv, bg = _layer_fwd(xs, h1, sm, bg, get_rest, bias, hpg, norm1_g[l + 1] if l + 1 < depth else None,
                                    at_ff1)
        saved.append(sv)
        bigs.append(bg)
        smalls.append(sm)

    loss, dx = _loss_and_grad(xs, loss_target[0], "loss")

    ds_sum = jnp.zeros((N_GROUPS * hpg, BLOCK, 2 * BLOCK), F32)
    g = {k: [None] * depth for k in SMALL if k != "rel_bias"}
    sums = {k: lax.empty((depth, int(np.prod(w[k].shape[1:-1])), w[k].shape[-1] + (SHIFT if k == "w_in" else 0)), F32)
            for k in BIG}
    pending = []

    def finish_oldest(after):
        names, l, (send, recv), bufs = pending.pop(0)
        parts, recvd, token = _exchange_finish(len(names), bufs, send, recv, after,
                                               "exchange_finish_%s_l%d" % (names[0], l))
        for k, r, p in zip(names, recvd, parts):
            three = (N_DEV, -1, r.shape[-1])
            sums[k] = _sum_slots(r.reshape(three), p.reshape(three), me, sums[k], l, "sum_" + k)
        return token

    def make_emit(l):
        def emit(names, gl):
            parts = [gl[k] if k in ("w_ff1", "w_in") else _to_slots(k, gl[k]) for k in names]
            token = finish_oldest(parts[0]) if len(pending) >= len(GRAD_GROUPS) else None
            lands = [lax.empty(p.shape, p.dtype) for p in parts]
            sems, bufs, token = _exchange_start(parts, lands, token, "exchange_start_%s_l%d" % (names[0], l))
            pending.append((names, l, sems, bufs))
            return token
        return emit

    token = None
    for l in reversed(range(depth)):
        dx, gl, ds_sum = _layer_bwd(dx, saved[l], smalls[l], bigs[l], bias, ds_sum, token, make_emit(l))
        for k in g:
            g[k][l] = gl[k]
        token = None
    grad_x = dx

    g = {k: jnp.stack(v) for k, v in g.items()}
    for k in ("q_norm_g", "k_norm_g"):
        g[k] = jnp.sum(g[k].reshape(depth, -1, HEAD_DIM), axis=1)
    db = _bias_reduce(ds_sum, buckets, hpg, "bias_reduce")
    g["rel_bias"] = db[:, 0, :NUM_BUCKETS].T

    flat = jnp.concatenate([g[k].reshape(-1) for k in SMALL])
    nflat = flat.shape[0]
    rows = -(-nflat // (8 * LANES)) * 8
    packed = jnp.pad(flat, (0, rows * LANES - nflat)).reshape(rows, LANES)
    grad, outs = {}, {}
    token = dx
    while pending:
        names = pending[0][0]
        finish_oldest(token)
        for k in names:
            total = sums[k]
            if k == "w_in":
                total = jnp.where(odd_core, total[..., SHIFT:], total[..., :w_in.shape[-1]])
            grad[k] = total.reshape(w[k].shape)
            outs[k] = _adamw(w[k], grad[k], mom[k], var[k], "adamw_" + k)
        token = tuple(outs[k][0] for k in names)
    total = _all_reduce_small(packed, token[-1], "reduce_small").reshape(-1)
    off = 0
    for k in SMALL:
        size = int(np.prod(w[k].shape))
        grad[k] = total[off:off + size].reshape(w[k].shape)
        outs[k] = _adamw(w[k], grad[k], mom[k], var[k], "adamw_" + k)
        off += size
    loss = lax.psum(loss[0, 0], ("x", "y", "c"))
    return (loss, grad_x[None], *[grad[k] for k in WEIGHTS], *[outs[k][0] for k in WEIGHTS],
            *[outs[k][1] for k in WEIGHTS], *[outs[k][2] for k in WEIGHTS])
```

```python
import functools
import math

import numpy as np
import jax
import jax.numpy as jnp
from jax import lax
from jax.experimental import pallas as pl
from jax.experimental.pallas import tpu as pltpu

F32 = jnp.float32
BF16 = jnp.bfloat16

HEAD_DIM = 64
N_GROUPS = 3
DILATIONS = (1, 4, 16)
SUB_WINDOW = 128
BLOCK = 128
CONV_WIDTH = 31
CONV_TAPS_PADDED = 32
NUM_BUCKETS = 32
MAX_REL_DISTANCE = 2048
EPS = 1e-6
NEG_INF = -1e30
LANES = 128
SUBLANES = 8

ADAM_LR = 0.001
ADAM_B1 = 0.9
ADAM_B2 = 0.999
ADAM_EPS = 1e-08
ADAM_WD = 0.01
ADAM_STEP = 10

N_DEV = 8
VMEM_LIMIT = 56 * 1024 * 1024
MESH = pl.DeviceIdType.MESH


def _cparams(sem=None):
    return pltpu.CompilerParams(dimension_semantics=sem, vmem_limit_bytes=VMEM_LIMIT)


def _tile(n, target):
    if n <= target:
        return n
    t = (target // LANES) * LANES
    while t >= LANES:
        if n % t == 0:
            return t
        t -= LANES
    return n


def _sigmoid(v):
    return 1.0 / (1.0 + jnp.exp(-v))


MM_VMEM_BUDGET = 40 * 1024 * 1024


def _rms_apply(x, g):
    return x * lax.rsqrt(jnp.mean(x * x, axis=-1, keepdims=True) + EPS) * g


def _rms_grad(dh, x, g):
    r = lax.rsqrt(jnp.mean(x * x, axis=-1, keepdims=True) + EPS)
    xh = x * r
    dxh = dh * g
    dx = r * (dxh - xh * jnp.mean(dxh * xh, axis=-1, keepdims=True))
    return dx, jnp.sum(dh * xh, axis=0, keepdims=True)


def _mm_tiles(m, n, kdim, a_bytes, b_bytes, io_bytes, whole_rows=False, temps=2):
    def need(tm, tn, tk):
        blocks = 2 * (tm * tk * a_bytes + tk * tn * b_bytes + tm * tn * io_bytes)
        casts = (tm * tk * 2 if a_bytes == 4 else 0) + (tk * tn * 2 if b_bytes == 4 else 0)
        return blocks + casts + temps * tm * tn * 4

    tn = n if whole_rows else _tile(n, 1024)
    while True:
        fits = [(tm * tk, tm, tk) for tm in {_tile(m, c) for c in (1024, 512, 256, 128)}
                for tk in {_tile(kdim, c) for c in (2048, 1024, 512, 256)} if need(tm, tn, tk) <= MM_VMEM_BUDGET]
        if fits:
            _, tm, tk = max(fits)
            return tm, tn, tk
        assert not whole_rows and tn % 256 == 0, "no block size fits the VMEM budget"
        tn //= 2


def _mm(a, b, *, ta=False, tb=False, out_dtype=F32, epi=None, extra=(), gain=None, after=None, out_slots=False,
        a_relu2=False, name):
    extra = tuple(extra) if isinstance(extra, (tuple, list)) else (extra,)
    m = a.shape[1] if ta else a.shape[0]
    kdim = a.shape[0] if ta else a.shape[1]
    n = b.shape[0] if tb else b.shape[1]
    norm = epi in ("res_rms", "rms_bwd")
    io_bytes = (jnp.dtype(out_dtype).itemsize + sum(e.dtype.itemsize for e in extra) + (2 if epi == "res_rms" else 0))
    tm, tn, tk = _mm_tiles(m, n // N_DEV if out_slots else n, kdim, a.dtype.itemsize, b.dtype.itemsize, io_bytes,
                           whole_rows=norm, temps=6 if norm else 2)
    if out_slots:
        assert epi is None and tn == n // N_DEV
    nk = kdim // tk
    a_spec = pl.BlockSpec((tk, tm), lambda i, j, k: (k, i)) if ta else pl.BlockSpec((tm, tk), lambda i, j, k: (i, k))
    b_spec = pl.BlockSpec((tn, tk), lambda i, j, k: (j, k)) if tb else pl.BlockSpec((tk, tn), lambda i, j, k: (k, j))
    o_spec = (pl.BlockSpec((None, tm, tn), lambda i, j, k: (j, i, 0)) if out_slots
              else pl.BlockSpec((tm, tn), lambda i, j, k: (i, j)))
    v_spec = pl.BlockSpec((1, tn), lambda i, j, k: (0, j))
    dims = (((0 if ta else 1,), (1 if tb else 0,)), ((), ()))
    n_extra = len(extra)
    n_in = 2 + n_extra + (gain is not None) + (after is not None)
    n_out = 2 if norm else 1

    def body(*refs):
        a_ref, b_ref = refs[0], refs[1]
        e_refs = refs[2:2 + n_extra]
        g_ref = refs[2 + n_extra] if gain is not None else None
        outs = refs[n_in:n_in + n_out]

        def product():
            av = a_ref[...]
            if a_relu2:
                r = jnp.maximum(av.astype(F32), 0.0)
                av = r * r
            return lax.dot_general(av.astype(BF16), b_ref[...].astype(BF16), dims, preferred_element_type=F32)

        def finish(acc):
            if epi is None:
                outs[0][...] = acc.astype(outs[0].dtype)
            elif epi == "res":
                outs[0][...] = (e_refs[0][...] + acc).astype(outs[0].dtype)
            elif epi == "drelu2":
                outs[0][...] = (acc * (2.0 * jnp.maximum(e_refs[0][...].astype(F32), 0.0))).astype(outs[0].dtype)
            elif epi == "res_rms":
                x1 = e_refs[0][...] + acc
                outs[0][...] = x1
                outs[1][...] = _rms_apply(x1, g_ref[...]).astype(BF16)
            elif epi == "rms_bwd":
                dx, dg = _rms_grad(acc, e_refs[0][...], g_ref[...])
                outs[0][...] = e_refs[1][...] + dx
                i = pl.program_id(0)

                @pl.when(i == 0)
                def _():
                    outs[1][...] = dg

                @pl.when(i > 0)
                def _():
                    outs[1][...] += dg

        if nk == 1:
            finish(product())
            return
        acc_ref = refs[-1]
        k = pl.program_id(2)

        @pl.when(k == 0)
        def _():
            acc_ref[...] = product()

        @pl.when(jnp.logical_and(k > 0, k < nk - 1))
        def _():
            acc_ref[...] += product()

        @pl.when(k == nk - 1)
        def _():
            finish(acc_ref[...] + product())

    in_specs = ([a_spec, b_spec] + [o_spec] * n_extra + ([v_spec] if gain is not None else [])
                + ([pl.BlockSpec(memory_space=pl.ANY)] if after is not None else []))
    if epi == "res_rms":
        out_shape = (jax.ShapeDtypeStruct((m, n), F32), jax.ShapeDtypeStruct((m, n), BF16))
        out_specs = (o_spec, o_spec)
    elif epi == "rms_bwd":
        out_shape = (jax.ShapeDtypeStruct((m, n), F32), jax.ShapeDtypeStruct((1, n), F32))
        out_specs = (o_spec, v_spec)
    else:
        out_shape = jax.ShapeDtypeStruct((N_DEV, m, tn) if out_slots else (m, n), out_dtype)
        out_specs = o_spec
    args = (a, b) + extra + ((gain.reshape(1, n),) if gain is not None else ()) + ((after,) if after is not None else ())
    return pl.pallas_call(
        body, name=name, grid=(m // tm, n // tn, nk), in_specs=in_specs, out_specs=out_specs, out_shape=out_shape,
        scratch_shapes=[pltpu.VMEM((tm, tn), F32)] if nk > 1 else [],
        compiler_params=_cparams(("arbitrary", "arbitrary", "arbitrary")),
    )(*args)


SHIFT = HEAD_DIM


def _pair_blocks(e, o):
    wp = e.shape[-1]
    return e[:, :wp - LANES], e[:, wp - LANES:] + o[:, :LANES], o[:, LANES:]


def _mm_in_pairs(a, wg, name):
    t, kdim = a.shape
    wp = wg.shape[-1]
    ws = wp - SHIFT
    tm = _tile(t, 1024)

    def body(a_ref, e_ref, o_ref, u_ref):
        av = a_ref[...]
        lo, mid, hi = _pair_blocks(e_ref[...], o_ref[...])
        u_ref[:, :wp - LANES] = jnp.dot(av, lo, preferred_element_type=F32).astype(BF16)
        u_ref[:, wp - LANES:wp] = jnp.dot(av, mid, preferred_element_type=F32).astype(BF16)
        u_ref[:, wp:] = jnp.dot(av, hi, preferred_element_type=F32).astype(BF16)

    return pl.pallas_call(
        body, name=name, grid=(N_DEV // 2, t // tm),
        in_specs=[pl.BlockSpec((tm, kdim), lambda p, i: (i, 0)),
                  pl.BlockSpec((None, kdim, wp), lambda p, i: (2 * p, 0, 0)),
                  pl.BlockSpec((None, kdim, wp), lambda p, i: (2 * p + 1, 0, 0))],
        out_specs=pl.BlockSpec((tm, 2 * ws), lambda p, i: (i, p)),
        out_shape=jax.ShapeDtypeStruct((t, N_DEV * ws), BF16), compiler_params=_cparams(("arbitrary", "arbitrary")),
    )(a, wg, wg)


def _mm_din_pairs(du, wg, x, gain, dres, after, name):
    t = du.shape[0]
    _, kdim, wp = wg.shape
    ws = wp - SHIFT
    tm = _tile(t, 512)
    npair = N_DEV // 2
    lanes = (((1,), (1,)), ((), ()))
    extra = [] if after is None else [after]

    def body(d_ref, e_ref, o_ref, x_ref, g_ref, r_ref, *rest):
        dx_ref, dg_ref, acc_ref = rest[-3], rest[-2], rest[-1]
        i, p = pl.program_id(0), pl.program_id(1)
        lo, mid, hi = _pair_blocks(e_ref[...], o_ref[...])
        part = (lax.dot_general(d_ref[:, :wp - LANES], lo, lanes, preferred_element_type=F32)
                + lax.dot_general(d_ref[:, wp - LANES:wp], mid, lanes, preferred_element_type=F32)
                + lax.dot_general(d_ref[:, wp:], hi, lanes, preferred_element_type=F32))

        @pl.when(p == 0)
        def _():
            acc_ref[...] = part

        @pl.when(jnp.logical_and(p > 0, p < npair - 1))
        def _():
            acc_ref[...] += part

        @pl.when(p == npair - 1)
        def _():
            dx, dg = _rms_grad(acc_ref[...] + part, x_ref[...], g_ref[...])
            dx_ref[...] = r_ref[...] + dx

            @pl.when(i == 0)
            def _():
                dg_ref[...] = dg

            @pl.when(i > 0)
            def _():
                dg_ref[...] += dg

    row = pl.BlockSpec((tm, kdim), lambda i, p: (i, 0))
    vec = pl.BlockSpec((1, kdim), lambda i, p: (0, 0))
    return pl.pallas_call(
        body, name=name, grid=(t // tm, npair),
        in_specs=[pl.BlockSpec((tm, 2 * ws), lambda i, p: (i, p)),
                  pl.BlockSpec((None, kdim, wp), lambda i, p: (2 * p, 0, 0)),
                  pl.BlockSpec((None, kdim, wp), lambda i, p: (2 * p + 1, 0, 0)), row, vec, row]
        + [pl.BlockSpec(memory_space=pl.ANY)] * len(extra),
        out_specs=(row, vec),
        out_shape=(jax.ShapeDtypeStruct((t, kdim), F32), jax.ShapeDtypeStruct((1, kdim), F32)),
        scratch_shapes=[pltpu.VMEM((tm, kdim), F32)], compiler_params=_cparams(("arbitrary", "arbitrary")),
    )(du, wg, wg, x, gain.reshape(1, kdim), dres, *extra)


def _mm_gw_in_pairs(h, du, after, name):
    t, kdim = h.shape
    ws = du.shape[1] // N_DEV
    wp = ws + SHIFT
    tm = _tile(kdim, 512)
    rows = (((0,), (0,)), ((), ()))
    extra = [] if after is None else [after]

    def body(h_ref, d_ref, *rest):
        g_ref = rest[-1]
        g = lax.dot_general(h_ref[...], d_ref[...], rows, preferred_element_type=F32)
        g_ref[0] = g[:, :wp].astype(BF16)
        g_ref[1] = g[:, wp - LANES:].astype(BF16)

    return pl.pallas_call(
        body, name=name, grid=(N_DEV // 2, kdim // tm),
        in_specs=[pl.BlockSpec((t, tm), lambda p, i: (0, i)), pl.BlockSpec((t, 2 * ws), lambda p, i: (0, p))]
        + [pl.BlockSpec(memory_space=pl.ANY)] * len(extra),
        out_specs=pl.BlockSpec((2, tm, wp), lambda p, i: (p, i, 0)),
        out_shape=jax.ShapeDtypeStruct((N_DEV, kdim, wp), BF16), compiler_params=_cparams(("arbitrary", "arbitrary")),
    )(h, du, *extra)


ROW_BLOCK_BUDGET = 24 * 1024 * 1024


def _rows(t, row_bytes):
    rows = t
    while rows > 8 and (2 * rows * row_bytes > ROW_BLOCK_BUDGET or t % rows):
        rows //= 2
    return rows


def _rms_fwd(x, g, after, name):
    t, d = x.shape
    ROWS = _rows(t, 6 * d)

    def body(x_ref, g_ref, _, h_ref):
        h_ref[...] = _rms_apply(x_ref[...], g_ref[...]).astype(BF16)

    return pl.pallas_call(
        body, name=name, grid=(t // ROWS,),
        in_specs=[pl.BlockSpec((ROWS, d), lambda i: (i, 0)), pl.BlockSpec((1, d), lambda i: (0, 0)),
                  pl.BlockSpec(memory_space=pl.ANY)],
        out_specs=pl.BlockSpec((ROWS, d), lambda i: (i, 0)),
        out_shape=jax.ShapeDtypeStruct((t, d), BF16), compiler_params=_cparams(("arbitrary",)),
    )(x, g.reshape(1, d), after)


def _gated_out(zs, o, wc, wa, u, gate_col, name):
    t = zs.shape[0]
    d = wc.shape[1]
    td = math.gcd(_tile(d, 512), gate_col)
    nd = d // td
    c0 = gate_col // td
    tm = _tile(t, 1024)

    def body(zs_ref, o_ref, wc_ref, wa_ref, gc_ref, ga_ref, m_ref, yc_ref, ya_ref):
        yc = jnp.dot(zs_ref[...], wc_ref[...], preferred_element_type=F32)
        ya = jnp.dot(o_ref[...], wa_ref[...], preferred_element_type=F32)
        gc = _sigmoid(gc_ref[...].astype(F32))
        ga = _sigmoid(ga_ref[...].astype(F32))
        m_ref[...] = (gc * yc + ga * ya).astype(BF16)
        yc_ref[...] = yc.astype(BF16)
        ya_ref[...] = ya.astype(BF16)

    blk = pl.BlockSpec((tm, td), lambda i, j: (i, j))
    sh = jax.ShapeDtypeStruct((t, d), BF16)
    return pl.pallas_call(
        body, name=name, grid=(t // tm, nd),
        in_specs=[pl.BlockSpec((tm, zs.shape[1]), lambda i, j: (i, 0)), pl.BlockSpec((tm, o.shape[1]), lambda i, j: (i, 0)),
                  pl.BlockSpec((wc.shape[0], td), lambda i, j: (0, j)), pl.BlockSpec((wa.shape[0], td), lambda i, j: (0, j)),
                  pl.BlockSpec((tm, td), lambda i, j: (i, c0 + j)), pl.BlockSpec((tm, td), lambda i, j: (i, c0 + nd + j))],
        out_specs=(blk, blk, blk), out_shape=(sh, sh, sh), compiler_params=_cparams(("arbitrary", "arbitrary")),
    )(zs, o, wc, wa, u, u)


def _gate_bwd(dm, u, yc, ya, gate_col, name):
    t, d = yc.shape
    td = math.gcd(_tile(d, 512), gate_col)
    nd = d // td
    c0 = gate_col // td
    ROWS = _rows(t, 20 * td)

    def body(dm_ref, gc_ref, ga_ref, yc_ref, ya_ref, dyc_ref, dya_ref, dugc_ref, duga_ref):
        dmv = dm_ref[...]
        gc = _sigmoid(gc_ref[...].astype(F32))
        ga = _sigmoid(ga_ref[...].astype(F32))
        dyc_ref[...] = (dmv * gc).astype(BF16)
        dya_ref[...] = (dmv * ga).astype(BF16)
        dugc_ref[...] = (dmv * yc_ref[...].astype(F32) * gc * (1.0 - gc)).astype(BF16)
        duga_ref[...] = (dmv * ya_ref[...].astype(F32) * ga * (1.0 - ga)).astype(BF16)

    blk = pl.BlockSpec((ROWS, td), lambda i, j: (i, j))
    o = jax.ShapeDtypeStruct((t, d), BF16)
    return pl.pallas_call(
        body, name=name, grid=(t // ROWS, nd),
        in_specs=[blk, pl.BlockSpec((ROWS, td), lambda i, j: (i, c0 + j)),
                  pl.BlockSpec((ROWS, td), lambda i, j: (i, c0 + nd + j)), blk, blk],
        out_specs=(blk, blk, blk, blk), out_shape=(o, o, o, o),
        compiler_params=_cparams(("arbitrary", "arbitrary")),
    )(dm, u, u, yc, ya)


def _loss_and_grad(y, target, name):
    t, d = y.shape
    ROWS = _rows(t, 12 * d)
    n = t // ROWS

    def body(y_ref, t_ref, loss_ref, dy_ref, acc_ref):
        i = pl.program_id(0)

        @pl.when(i == 0)
        def _():
            acc_ref[...] = jnp.zeros_like(acc_ref)

        diff = y_ref[...] - t_ref[...]
        dy_ref[...] = diff * (1.0 / d)
        acc_ref[...] += jnp.sum(diff * diff, axis=0, keepdims=True)

        @pl.when(i == n - 1)
        def _():
            loss_ref[...] = jnp.sum(acc_ref[...], axis=-1, keepdims=True) * (0.5 / d)

    row = pl.BlockSpec((ROWS, d), lambda i: (i, 0))
    return pl.pallas_call(
        body, name=name, grid=(n,), in_specs=[row, row],
        out_specs=(pl.BlockSpec((1, 1), lambda i: (0, 0)), row),
        out_shape=(jax.ShapeDtypeStruct((1, 1), F32), jax.ShapeDtypeStruct((t, d), F32)),
        scratch_shapes=[pltpu.VMEM((1, d), F32)], compiler_params=_cparams(("arbitrary",)),
    )(y, target)


HALO = 32


def _conv_fwd(u, w, b, cdim, name):
    t = u.shape[0]
    ncb = cdim // LANES
    nt = t // BLOCK

    def body(a_ref, g_ref, w_ref, b_ref, zc_ref, zpad):
        zpad[0:HALO, :] = jnp.zeros((HALO, LANES), F32)
        zpad[HALO:HALO + t, :] = a_ref[...].astype(F32) * _sigmoid(g_ref[...].astype(F32))
        wv = w_ref[...]
        bv = b_ref[...]

        def tile(i, carry):
            r0 = pl.multiple_of(i * BLOCK, BLOCK)
            win = zpad[pl.ds(r0, BLOCK + HALO), :]
            acc = jnp.zeros((BLOCK, LANES), F32) + bv
            for b in range(SUBLANES):
                sh = win if b == 0 else pltpu.roll(win, b, 0)
                for a in range(HALO // SUBLANES):
                    j = CONV_WIDTH - 1 - (SUBLANES * a + b)
                    if j >= 0:
                        lo = HALO - SUBLANES * a
                        acc = acc + wv[j:j + 1, :] * sh[lo:lo + BLOCK, :]
            zc_ref[pl.ds(r0, BLOCK), :] = acc
            return carry

        lax.fori_loop(0, nt, tile, 0)

    col = lambda off: pl.BlockSpec((t, LANES), lambda c: (0, off + c))
    return pl.pallas_call(
        body, name=name, grid=(ncb,),
        in_specs=[col(0), col(ncb), pl.BlockSpec((CONV_TAPS_PADDED, LANES), lambda c: (0, c)),
                  pl.BlockSpec((1, LANES), lambda c: (0, c))],
        out_specs=pl.BlockSpec((t, LANES), lambda c: (0, c)),
        out_shape=jax.ShapeDtypeStruct((t, cdim), F32),
        scratch_shapes=[pltpu.VMEM((t + HALO, LANES), F32)], compiler_params=_cparams(("arbitrary",)),
    )(u, u, w, b.reshape(1, cdim))


def _conv_bwd(dzc, u, w, cdim, name):
    t = u.shape[0]
    ncb = cdim // LANES
    nt = t // BLOCK
    win_rows = BLOCK + HALO

    def body(dzc_ref, a_ref, g_ref, w_ref, da_ref, dg_ref, dw_ref, db_ref, zpad, dpad):
        av = a_ref[...].astype(F32)
        sg = _sigmoid(g_ref[...].astype(F32))
        zpad[0:HALO, :] = jnp.zeros((HALO, LANES), F32)
        zpad[HALO:HALO + t, :] = av * sg
        dpad[0:t, :] = dzc_ref[...]
        dpad[t:t + HALO, :] = jnp.zeros((HALO, LANES), F32)
        dw_ref[...] = jnp.zeros_like(dw_ref)
        db_ref[...] = jnp.sum(dzc_ref[...], axis=0, keepdims=True)
        wv = w_ref[...]

        def tile(i, carry):
            r0 = pl.multiple_of(i * BLOCK, BLOCK)
            zwin = zpad[pl.ds(r0, win_rows), :]
            dwin = dpad[pl.ds(r0, win_rows), :]
            dcur = dwin[0:BLOCK, :]
            dz = jnp.zeros((BLOCK, LANES), F32)
            for b in range(SUBLANES):
                zs = zwin if b == 0 else pltpu.roll(zwin, b, 0)
                ds = dwin if b == 0 else pltpu.roll(dwin, win_rows - b, 0)
                for a in range(HALO // SUBLANES):
                    j = CONV_WIDTH - 1 - (SUBLANES * a + b)
                    if j >= 0:
                        lo = HALO - SUBLANES * a
                        dw_ref[j:j + 1, :] += jnp.sum(dcur * zs[lo:lo + BLOCK, :], axis=0, keepdims=True)
                        dz = dz + wv[j:j + 1, :] * ds[SUBLANES * a:SUBLANES * a + BLOCK, :]
            ac = a_ref[pl.ds(r0, BLOCK), :].astype(F32)
            sc = _sigmoid(g_ref[pl.ds(r0, BLOCK), :].astype(F32))
            da_ref[pl.ds(r0, BLOCK), :] = (dz * sc).astype(BF16)
            dg_ref[pl.ds(r0, BLOCK), :] = (dz * ac * sc * (1.0 - sc)).astype(BF16)
            return carry

        lax.fori_loop(0, nt, tile, 0)

    col = lambda off: pl.BlockSpec((t, LANES), lambda c: (0, off + c))
    wspec = pl.BlockSpec((CONV_TAPS_PADDED, LANES), lambda c: (0, c))
    o = jax.ShapeDtypeStruct((t, cdim), BF16)
    return pl.pallas_call(
        body, name=name, grid=(ncb,), in_specs=[col(0), col(0), col(ncb), wspec],
        out_specs=(col(0), col(0), wspec, pl.BlockSpec((1, LANES), lambda c: (0, c))),
        out_shape=(o, o, jax.ShapeDtypeStruct((CONV_TAPS_PADDED, cdim), F32), jax.ShapeDtypeStruct((1, cdim), F32)),
        scratch_shapes=[pltpu.VMEM((t + HALO, LANES), F32), pltpu.VMEM((t + HALO, LANES), F32)],
        compiler_params=_cparams(("arbitrary",)),
    )(dzc, u, u, w)


def _ln_swish_fwd(zc, g, b, name):
    t, c = zc.shape
    ROWS = _rows(t, 6 * c)

    def body(z_ref, g_ref, b_ref, o_ref):
        z = z_ref[...]
        mu = jnp.mean(z, axis=-1, keepdims=True)
        zc_ = z - mu
        zn = zc_ * lax.rsqrt(jnp.mean(zc_ * zc_, axis=-1, keepdims=True) + EPS)
        y = zn * g_ref[...] + b_ref[...]
        o_ref[...] = (y * _sigmoid(y)).astype(BF16)

    row = pl.BlockSpec((ROWS, c), lambda i: (i, 0))
    vec = pl.BlockSpec((1, c), lambda i: (0, 0))
    return pl.pallas_call(
        body, name=name, grid=(t // ROWS,), in_specs=[row, vec, vec], out_specs=row,
        out_shape=jax.ShapeDtypeStruct((t, c), BF16), compiler_params=_cparams(("arbitrary",)),
    )(zc, g.reshape(1, c), b.reshape(1, c))


def _ln_swish_bwd(dzs, zc, g, b, name):
    t, c = zc.shape
    ROWS = _rows(t, 12 * c)

    def body(d_ref, z_ref, g_ref, b_ref, dz_ref, dg_ref, db_ref):
        @pl.when(pl.program_id(0) == 0)
        def _():
            dg_ref[...] = jnp.zeros_like(dg_ref)
            db_ref[...] = jnp.zeros_like(db_ref)

        z = z_ref[...]
        mu = jnp.mean(z, axis=-1, keepdims=True)
        zc_ = z - mu
        rstd = lax.rsqrt(jnp.mean(zc_ * zc_, axis=-1, keepdims=True) + EPS)
        zn = zc_ * rstd
        y = zn * g_ref[...] + b_ref[...]
        sg = _sigmoid(y)
        dy = d_ref[...] * (sg * (1.0 + y * (1.0 - sg)))
        dg_ref[...] += jnp.sum(dy * zn, axis=0, keepdims=True)
        db_ref[...] += jnp.sum(dy, axis=0, keepdims=True)
        dzn = dy * g_ref[...]
        dz_ref[...] = rstd * (dzn - jnp.mean(dzn, axis=-1, keepdims=True)
                              - zn * jnp.mean(dzn * zn, axis=-1, keepdims=True))

    row = pl.BlockSpec((ROWS, c), lambda i: (i, 0))
    vec = pl.BlockSpec((1, c), lambda i: (0, 0))
    v = jax.ShapeDtypeStruct((1, c), F32)
    return pl.pallas_call(
        body, name=name, grid=(t // ROWS,), in_specs=[row, row, vec, vec], out_specs=(row, vec, vec),
        out_shape=(jax.ShapeDtypeStruct((t, c), F32), v, v), compiler_params=_cparams(("arbitrary",)),
    )(dzs, zc, g.reshape(1, c), b.reshape(1, c))


def _bucket_table():
    qi = np.arange(BLOCK)[:, None]
    kj = np.arange(2 * BLOCK)[None, :]
    off = qi + BLOCK - kj
    band = (off >= 0) & (off <= SUB_WINDOW)
    max_exact = NUM_BUCKETS // 2
    out = []
    for d in DILATIONS:
        dist = (np.clip(off, 0, SUB_WINDOW) * d).astype(np.int32)
        nf = np.maximum(dist, 1).astype(np.float32)
        large = max_exact + (np.log(nf / np.float32(max_exact)) / np.float32(math.log(MAX_REL_DISTANCE / max_exact))
                             * np.float32(NUM_BUCKETS - max_exact)).astype(np.int32)
        large = np.minimum(large, NUM_BUCKETS - 1)
        bucket = np.where(dist < max_exact, dist, large)
        out.append(np.where(band, bucket, -1))
    return np.stack(out).astype(np.int32)


def _bias_expand(rel_bias, buckets, hpg, after, name):
    nh = N_GROUPS * hpg

    def body(rb_ref, bk_ref, _, o_ref):
        h = pl.program_id(0)
        bk = bk_ref[0]
        acc = jnp.full((BLOCK, 2 * BLOCK), NEG_INF, F32)
        for bb in range(NUM_BUCKETS):
            acc = jnp.where(bk == bb, rb_ref[bb, h], acc)
        o_ref[0] = acc

    return pl.pallas_call(
        body, name=name, grid=(nh,),
        in_specs=[pl.BlockSpec(memory_space=pltpu.SMEM),
                  pl.BlockSpec((1, BLOCK, 2 * BLOCK), lambda h: (h // hpg, 0, 0)), pl.BlockSpec(memory_space=pl.ANY)],
        out_specs=pl.BlockSpec((1, BLOCK, 2 * BLOCK), lambda h: (h, 0, 0)),
        out_shape=jax.ShapeDtypeStruct((nh, BLOCK, 2 * BLOCK), F32), compiler_params=_cparams(("arbitrary",)),
    )(rel_bias, buckets, after)


def _bias_reduce(ds_sum, buckets, hpg, name):
    nh = N_GROUPS * hpg

    def body(ds_ref, bk_ref, o_ref):
        bk = bk_ref[0]
        dsv = ds_ref[0]
        lane = lax.broadcasted_iota(jnp.int32, (1, LANES), 1)
        row = jnp.zeros((1, LANES), F32)
        for bb in range(NUM_BUCKETS):
            tot = jnp.sum(jnp.sum(jnp.where(bk == bb, dsv, 0.0), axis=-1, keepdims=True), axis=0, keepdims=True)
            row = jnp.where(lane == bb, tot, row)
        o_ref[0] = row

    return pl.pallas_call(
        body, name=name, grid=(nh,),
        in_specs=[pl.BlockSpec((1, BLOCK, 2 * BLOCK), lambda h: (h, 0, 0)),
                  pl.BlockSpec((1, BLOCK, 2 * BLOCK), lambda h: (h // hpg, 0, 0))],
        out_specs=pl.BlockSpec((1, 1, LANES), lambda h: (h, 0, 0)),
        out_shape=jax.ShapeDtypeStruct((nh, 1, LANES), F32), compiler_params=_cparams(("arbitrary",)),
    )(ds_sum, buckets)


def _chunk_rows(c, d, nb):
    r, n = c // nb, c % nb
    if d == 1:
        return pl.ds(c * BLOCK, BLOCK)
    return pl.ds(r + n * BLOCK * d, BLOCK, stride=d)


def _segment_ones():
    i = lax.broadcasted_iota(jnp.int32, (LANES, LANES), 0) // HEAD_DIM
    j = lax.broadcasted_iota(jnp.int32, (LANES, LANES), 1) // HEAD_DIM
    return (i == j).astype(BF16)


def _segment_sum(v, seg):
    hi = v.astype(BF16)
    lo = (v - hi.astype(F32)).astype(BF16)
    return jnp.dot(hi, seg, preferred_element_type=F32) + jnp.dot(lo, seg, preferred_element_type=F32)


def _head_mean(v, seg):
    return _segment_sum(v, seg) * (1.0 / HEAD_DIM)


def _attn_fwd(u, qg2, kg2, bias, gi, cols, hpg, name):
    t = u.shape[0]
    d = DILATIONS[gi]
    nchunk = t // BLOCK
    nb = (t // d) // BLOCK
    hp = hpg // 2
    qc0, kc0, vc0 = [(c + gi * hpg * HEAD_DIM) // LANES for c in cols]
    contract_lanes = (((1,), (1,)), ((), ()))

    def body(q_ref, k_ref, v_ref, qg_ref, kg_ref, bias_ref, o_ref, lse_ref, qd, kd, vd, od, ld, sbuf):
        seg = _segment_ones()
        lane = lax.broadcasted_iota(jnp.int32, (1, LANES), 1)
        qg = qg_ref[...] * (HEAD_DIM ** -0.5)
        kg = kg_ref[...]
        kd[0:BLOCK, :] = jnp.zeros((BLOCK, LANES), BF16)
        vd[0:BLOCK, :] = jnp.zeros((BLOCK, LANES), BF16)
        od[...] = q_ref[...].astype(F32)
        ld[...] = k_ref[...].astype(F32)
        for c in range(nchunk):
            rows = _chunk_rows(c, d, nb)
            qv = od[rows, :]
            kv = ld[rows, :]
            qd[c * BLOCK:(c + 1) * BLOCK, :] = (qv * lax.rsqrt(_head_mean(qv * qv, seg) + EPS) * qg).astype(BF16)
            kd[(c + 1) * BLOCK:(c + 2) * BLOCK, :] = (kv * lax.rsqrt(_head_mean(kv * kv, seg) + EPS) * kg).astype(BF16)
        od[...] = v_ref[...].astype(F32)
        for c in range(nchunk):
            vd[(c + 1) * BLOCK:(c + 2) * BLOCK, :] = od[_chunk_rows(c, d, nb), :].astype(BF16)

        col = lax.broadcasted_iota(jnp.int32, (BLOCK, 2 * BLOCK), 1)
        for j in range(2):
            mj = jnp.logical_and(lane >= j * HEAD_DIM, lane < (j + 1) * HEAD_DIM)
            for c in range(nchunk):
                kw = kd[c * BLOCK:(c + 2) * BLOCK, :]
                kj = jnp.where(mj, kw, jnp.zeros_like(kw))
                s = lax.dot_general(qd[c * BLOCK:(c + 1) * BLOCK, :], kj, contract_lanes,
                                    preferred_element_type=F32) + bias_ref[j]
                if c % nb == 0:
                    s = jnp.where(col < BLOCK, NEG_INF, s)
                sbuf[c] = s
            for c in range(nchunk):
                rows = slice(c * BLOCK, (c + 1) * BLOCK)
                s = sbuf[c]
                mx = jnp.max(s, axis=-1, keepdims=True)
                p = jnp.exp(s - mx).astype(BF16)
                vw = vd[c * BLOCK:(c + 2) * BLOCK, :]
                oj = jnp.dot(p, jnp.where(mj, vw, jnp.ones_like(vw)), preferred_element_type=F32)
                l = pltpu.roll(oj, HEAD_DIM, 1)
                on = oj / l
                ls = mx + jnp.log(l)
                if j == 0:
                    od[rows, :] = on
                    ld[rows, :] = ls
                else:
                    od[rows, :] = jnp.where(mj, on, od[rows, :])
                    ld[rows, :] = jnp.where(mj, ls, ld[rows, :])

        for c in range(nchunk):
            rows = _chunk_rows(c, d, nb)
            o_ref[rows, :] = od[c * BLOCK:(c + 1) * BLOCK, :]
            lse_ref[rows, :] = ld[c * BLOCK:(c + 1) * BLOCK, :]

    ucol = lambda c0: pl.BlockSpec((t, LANES), lambda h: (0, c0 + h))
    vec = pl.BlockSpec((1, LANES), lambda h: (0, 0))
    oblk = pl.BlockSpec((t, LANES), lambda h: (0, h))
    osh = jax.ShapeDtypeStruct((t, hpg * HEAD_DIM), F32)
    return pl.pallas_call(
        body, name=name, grid=(hp,),
        in_specs=[ucol(qc0), ucol(kc0), ucol(vc0), vec, vec,
                  pl.BlockSpec((2, BLOCK, 2 * BLOCK), lambda h: (gi * hp + h, 0, 0))],
        out_specs=(oblk, oblk), out_shape=(osh, osh),
        scratch_shapes=[pltpu.VMEM((t, LANES), BF16), pltpu.VMEM((t + BLOCK, LANES), BF16),
                        pltpu.VMEM((t + BLOCK, LANES), BF16), pltpu.VMEM((t, LANES), F32), pltpu.VMEM((t, LANES), F32),
                        pltpu.VMEM((nchunk, BLOCK, 2 * BLOCK), F32)],
        compiler_params=_cparams(("arbitrary",)),
    )(u, u, u, qg2, kg2, bias)


def _attn_bwd(u, do_g, dd_g, lse_g, qg2, kg2, bias, ds_in, du_in, gi, cols, hpg, name):
    t = u.shape[0]
    d = DILATIONS[gi]
    nchunk = t // BLOCK
    nb = (t // d) // BLOCK
    hp = hpg // 2
    qc0, kc0, vc0 = [(c + gi * hpg * HEAD_DIM) // LANES for c in cols]
    contract_lanes = (((1,), (1,)), ((), ()))
    contract_rows = (((0,), (0,)), ((), ()))
    qscale = HEAD_DIM ** -0.5

    def body(q_ref, k_ref, v_ref, do_ref, dd_ref, lse_ref, qg_ref, kg_ref, bias_ref, dsin_ref, _du_in,
             dgq_ref, dgk_ref, dsout_ref, du_ref,
             qd, kd, vd, dod, ddd, ld, dqd, dkd, dvd, dsacc, pbuf, dsbuf, qs, ks, qst, kst, vst, out_sems):
        h = pl.program_id(0)

        def flush(step):
            return [pltpu.make_async_copy(
                st, du_ref.at[:, pl.ds(pl.multiple_of((c0 + step) * LANES, LANES), LANES)], out_sems.at[i])
                for i, (st, c0) in enumerate(((qst, qc0), (kst, kc0), (vst, vc0)))]

        seg = _segment_ones()
        lane = lax.broadcasted_iota(jnp.int32, (1, LANES), 1)
        qg = qg_ref[...] * qscale
        kg = kg_ref[...]
        kd[0:BLOCK, :] = jnp.zeros((BLOCK, LANES), BF16)
        vd[0:BLOCK, :] = jnp.zeros((BLOCK, LANES), BF16)
        dsacc[...] = jnp.zeros_like(dsacc)
        qs[...] = q_ref[...].astype(F32)
        ks[...] = k_ref[...].astype(F32)
        dqd[...] = v_ref[...].astype(F32)
        dkd[...] = do_ref[...].astype(F32)
        for c in range(nchunk):
            rows = _chunk_rows(c, d, nb)
            qv = qs[rows, :]
            kv = ks[rows, :]
            qd[c * BLOCK:(c + 1) * BLOCK, :] = (qv * lax.rsqrt(_head_mean(qv * qv, seg) + EPS) * qg).astype(BF16)
            kd[(c + 1) * BLOCK:(c + 2) * BLOCK, :] = (kv * lax.rsqrt(_head_mean(kv * kv, seg) + EPS) * kg).astype(BF16)
            vd[(c + 1) * BLOCK:(c + 2) * BLOCK, :] = dqd[rows, :].astype(BF16)
            dod[c * BLOCK:(c + 1) * BLOCK, :] = dkd[rows, :].astype(BF16)
            ddd[c * BLOCK:(c + 1) * BLOCK, :] = dd_ref[rows, :]
            ld[c * BLOCK:(c + 1) * BLOCK, :] = lse_ref[rows, :]

        col = lax.broadcasted_iota(jnp.int32, (BLOCK, 2 * BLOCK), 1)
        for j in range(2):
            mj = jnp.logical_and(lane >= j * HEAD_DIM, lane < (j + 1) * HEAD_DIM)
            first = lane == j * HEAD_DIM
            for c in range(nchunk):
                rows = slice(c * BLOCK, (c + 1) * BLOCK)
                kw = kd[c * BLOCK:(c + 2) * BLOCK, :]
                vw = vd[c * BLOCK:(c + 2) * BLOCK, :]
                kj = jnp.where(mj, kw, jnp.zeros_like(kw))
                vj = jnp.where(mj, vw, jnp.zeros_like(vw))
                s = lax.dot_general(qd[rows, :], kj, contract_lanes, preferred_element_type=F32) + bias_ref[j]
                if c % nb == 0:
                    s = jnp.where(col < BLOCK, NEG_INF, s)
                dp = lax.dot_general(dod[rows, :], vj, contract_lanes, preferred_element_type=F32)
                lse_j = jnp.sum(jnp.where(first, ld[rows, :], 0.0), axis=-1, keepdims=True)
                dd_j = jnp.sum(jnp.where(first, ddd[rows, :], 0.0), axis=-1, keepdims=True)
                p = jnp.exp(s - lse_j)
                ds = p * (dp + dd_j)
                dsacc[j] += ds
                pbuf[j, c] = p.astype(BF16)
                dsbuf[j, c] = ds.astype(BF16)
        for c in range(nchunk):
            rows = slice(c * BLOCK, (c + 1) * BLOCK)
            has_next = c + 1 < nchunk and (c + 1) % nb != 0
            kw = kd[c * BLOCK:(c + 2) * BLOCK, :]
            dq = jnp.zeros((BLOCK, LANES), F32)
            dk = jnp.zeros((BLOCK, LANES), F32)
            dv = jnp.zeros((BLOCK, LANES), F32)
            both = slice(c * BLOCK, (c + 2) * BLOCK) if has_next else rows
            for j in range(2):
                mj = jnp.logical_and(lane >= j * HEAD_DIM, lane < (j + 1) * HEAD_DIM)
                dq = dq + jnp.dot(dsbuf[j, c], jnp.where(mj, kw, jnp.zeros_like(kw)), preferred_element_type=F32)
                dsk = dsbuf[j, c, :, BLOCK:]
                pk = pbuf[j, c, :, BLOCK:]
                if has_next:
                    dsk = jnp.concatenate([dsk, dsbuf[j, c + 1, :, :BLOCK]], axis=0)
                    pk = jnp.concatenate([pk, pbuf[j, c + 1, :, :BLOCK]], axis=0)
                qq = qd[both, :]
                dd = dod[both, :]
                dk = dk + lax.dot_general(dsk, jnp.where(mj, qq, jnp.zeros_like(qq)), contract_rows,
                                          preferred_element_type=F32)
                dv = dv + lax.dot_general(pk, jnp.where(mj, dd, jnp.zeros_like(dd)), contract_rows,
                                          preferred_element_type=F32)
            dqd[rows, :] = dq
            dkd[rows, :] = dk
            dvd[rows, :] = dv

        dsout_ref[...] = dsin_ref[...] + dsacc[...]

        dgq = jnp.zeros((1, LANES), F32)
        dgk = jnp.zeros((1, LANES), F32)
        for c in range(nchunk):
            rows = _chunk_rows(c, d, nb)
            qv = qs[rows, :]
            rq = lax.rsqrt(_head_mean(qv * qv, seg) + EPS)
            qh = qv * rq
            dy = dqd[c * BLOCK:(c + 1) * BLOCK, :]
            dgq = dgq + jnp.sum(dy * qh, axis=0, keepdims=True) * qscale
            dxh = dy * qg
            ddd[rows, :] = rq * (dxh - qh * _head_mean(dxh * qh, seg))
            kv = ks[rows, :]
            rk = lax.rsqrt(_head_mean(kv * kv, seg) + EPS)
            kh = kv * rk
            dy = dkd[c * BLOCK:(c + 1) * BLOCK, :]
            dgk = dgk + jnp.sum(dy * kh, axis=0, keepdims=True)
            dxh = dy * kg
            ld[rows, :] = rk * (dxh - kh * _head_mean(dxh * kh, seg))
        @pl.when(h > 0)
        def _():
            for cp in flush(h - 1):
                cp.wait()

        qst[...] = ddd[...].astype(BF16)
        kst[...] = ld[...].astype(BF16)
        for c in range(nchunk):
            ddd[_chunk_rows(c, d, nb), :] = dvd[c * BLOCK:(c + 1) * BLOCK, :]
        vst[...] = ddd[...].astype(BF16)
        for cp in flush(h):
            cp.start()

        @pl.when(h == hp - 1)
        def _():
            for cp in flush(h):
                cp.wait()

        dgq_ref[0] = dgq
        dgk_ref[0] = dgk

    ucol = lambda c0: pl.BlockSpec((t, LANES), lambda h: (0, c0 + h))
    vec = pl.BlockSpec((1, LANES), lambda h: (0, 0))
    oblk = pl.BlockSpec((t, LANES), lambda h: (0, h))
    bblk = pl.BlockSpec((2, BLOCK, 2 * BLOCK), lambda h: (gi * hp + h, 0, 0))
    gblk = pl.BlockSpec((1, 1, LANES), lambda h: (h, 0, 0))
    gsh = jax.ShapeDtypeStruct((hp, 1, LANES), F32)
    hbm = pl.BlockSpec(memory_space=pl.ANY)
    return pl.pallas_call(
        body, name=name, grid=(hp,),
        in_specs=[ucol(qc0), ucol(kc0), ucol(vc0), oblk, oblk, oblk, vec, vec, bblk, bblk, hbm],
        out_specs=(gblk, gblk, bblk, hbm),
        out_shape=(gsh, gsh, jax.ShapeDtypeStruct(ds_in.shape, F32), jax.ShapeDtypeStruct(du_in.shape, BF16)),
        input_output_aliases={9: 2, 10: 3},
        scratch_shapes=[pltpu.VMEM((t, LANES), BF16), pltpu.VMEM((t + BLOCK, LANES), BF16),
                        pltpu.VMEM((t + BLOCK, LANES), BF16), pltpu.VMEM((t, LANES), BF16),
                        pltpu.VMEM((t, LANES), F32), pltpu.VMEM((t, LANES), F32), pltpu.VMEM((t, LANES), F32),
                        pltpu.VMEM((t, LANES), F32), pltpu.VMEM((t, LANES), F32),
                        pltpu.VMEM((2, BLOCK, 2 * BLOCK), F32), pltpu.VMEM((2, nchunk, BLOCK, 2 * BLOCK), BF16),
                        pltpu.VMEM((2, nchunk, BLOCK, 2 * BLOCK), BF16), pltpu.VMEM((t, LANES), F32),
                        pltpu.VMEM((t, LANES), F32), pltpu.VMEM((t, LANES), BF16), pltpu.VMEM((t, LANES), BF16),
                        pltpu.VMEM((t, LANES), BF16), pltpu.SemaphoreType.DMA((3,))],
        compiler_params=_cparams(("arbitrary",)),
    )(u, u, u, do_g, dd_g, lse_g, qg2, kg2, bias, ds_in, du_in)


def _group_weights(l0, l1, l2):
    mx = jnp.maximum(jnp.maximum(l0, l1), l2)
    e0, e1, e2 = jnp.exp(l0 - mx), jnp.exp(l1 - mx), jnp.exp(l2 - mx)
    inv = 1.0 / (e0 + e1 + e2)
    return e0 * inv, e1 * inv, e2 * inv


def _combine_fwd(os_, lses, name):
    t, ao = os_[0].shape
    ROWS = _rows(t, 26 * ao)

    def body(o0, o1, o2, l0, l1, l2, o_ref):
        w0, w1, w2 = _group_weights(l0[...], l1[...], l2[...])
        o_ref[...] = (w0 * o0[...] + w1 * o1[...] + w2 * o2[...]).astype(BF16)

    row = pl.BlockSpec((ROWS, ao), lambda i: (i, 0))
    return pl.pallas_call(
        body, name=name, grid=(t // ROWS,), in_specs=[row] * 6, out_specs=row,
        out_shape=jax.ShapeDtypeStruct((t, ao), BF16), compiler_params=_cparams(("arbitrary",)),
    )(*os_, *lses)


def _combine_bwd(do, os_, lses, name):
    t, ao = do.shape
    idx = np.arange(ao) // HEAD_DIM
    seg = jnp.asarray((idx[:, None] == idx[None, :]).astype(np.float32), dtype=BF16)
    ROWS = _rows(t, 46 * ao)

    def body(do_ref, o0, o1, o2, l0, l1, l2, seg_ref, g0, g1, g2, d0, d1, d2):
        w0, w1, w2 = _group_weights(l0[...], l1[...], l2[...])
        dov = do_ref[...]
        o = w0 * o0[...] + w1 * o1[...] + w2 * o2[...]
        sd = _segment_sum(dov * o, seg_ref[...])
        for w, gref, dref in ((w0, g0, d0), (w1, g1, d1), (w2, g2, d2)):
            gref[...] = (w * dov).astype(BF16)
            dref[...] = -(w * sd)

    row = pl.BlockSpec((ROWS, ao), lambda i: (i, 0))
    sh = jax.ShapeDtypeStruct((t, ao), F32)
    sh16 = jax.ShapeDtypeStruct((t, ao), BF16)
    outs = pl.pallas_call(
        body, name=name, grid=(t // ROWS,), in_specs=[row] * 7 + [pl.BlockSpec((ao, ao), lambda i: (0, 0))],
        out_specs=(row,) * 6, out_shape=(sh16,) * 3 + (sh,) * 3, compiler_params=_cparams(("arbitrary",)),
    )(do, *os_, *lses, seg)
    return outs[:3], outs[3:]


def _adamw(w, g, m, v, name):
    shape = w.shape
    cols = shape[-1]
    rows = int(np.prod(shape[:-1]))
    tr = rows if rows <= 512 else _tile_rows(rows)
    c1 = 1.0 - ADAM_B1 ** ADAM_STEP
    c2 = 1.0 - ADAM_B2 ** ADAM_STEP

    def body(w_ref, g_ref, m_ref, v_ref, d_ref, nm_ref, nv_ref):
        gv = g_ref[...]
        mn = ADAM_B1 * m_ref[...] + (1.0 - ADAM_B1) * gv
        vn = ADAM_B2 * v_ref[...] + (1.0 - ADAM_B2) * (gv * gv)
        nm_ref[...] = mn
        nv_ref[...] = vn
        d_ref[...] = -ADAM_LR * ((mn / c1) / (jnp.sqrt(vn / c2) + ADAM_EPS) + ADAM_WD * w_ref[...])

    blk = pl.BlockSpec((tr, cols), lambda i: (i, 0))
    sh = jax.ShapeDtypeStruct((rows, cols), F32)
    outs = pl.pallas_call(
        body, name=name, grid=(rows // tr,), in_specs=[blk] * 4, out_specs=(blk,) * 3, out_shape=(sh,) * 3,
        compiler_params=_cparams(("arbitrary",)),
    )(*[a.reshape(rows, cols) for a in (w, g, m, v)])
    return tuple(o.reshape(shape) for o in outs)


def _tile_rows(rows):
    for t in (512, 256, 128, 64, 32, 16, 8):
        if rows % t == 0:
            return t
    return rows


def _sum_slots(recv, parts, me, layers, l, name):
    _, rows, cols = recv.shape
    tr = rows if rows <= 512 else _tile_rows(rows)

    def body(me_ref, r_ref, own_ref, _, o_ref):
        acc = jnp.zeros(o_ref.shape, F32)
        for s in range(N_DEV):
            acc = acc + jnp.where(me_ref[0] == s, own_ref[...], r_ref[s]).astype(F32)
        o_ref[...] = acc

    return pl.pallas_call(
        body, name=name,
        grid_spec=pltpu.PrefetchScalarGridSpec(
            num_scalar_prefetch=1, grid=(rows // tr,),
            in_specs=[pl.BlockSpec((N_DEV, tr, cols), lambda i, me: (0, i, 0)),
                      pl.BlockSpec((None, tr, cols), lambda i, me: (me[0], i, 0)),
                      pl.BlockSpec(memory_space=pl.ANY)],
            out_specs=pl.BlockSpec((None, tr, cols), lambda i, me: (l, i, 0))),
        out_shape=jax.ShapeDtypeStruct(layers.shape, F32), input_output_aliases={3: 0},
        compiler_params=_cparams(("arbitrary",)),
    )(me.reshape(1), recv, parts, layers)


def _peer(k):
    x, y, c = lax.axis_index("x"), lax.axis_index("y"), lax.axis_index("c")
    return (1 - x if k & 4 else x, 1 - y if k & 2 else y, 1 - c if k & 1 else c)


def _dev_index(p):
    return 4 * p[0] + 2 * p[1] + p[2]


HBM_SPEC = pl.BlockSpec(memory_space=pltpu.HBM)
SEM_SPEC = pl.BlockSpec(memory_space=pltpu.SEMAPHORE)
ANY_SPEC = pl.BlockSpec(memory_space=pl.ANY)
CHIPS = (4, 2, 6)


def _remote(src, dst, send_sem, recv_sem, to):
    return pltpu.make_async_remote_copy(src_ref=src, dst_ref=dst, send_sem=send_sem, recv_sem=recv_sem,
                                        device_id=to, device_id_type=MESH)


def _hbm(a):
    return pltpu.with_memory_space_constraint(a, pltpu.HBM)


def _split_call(body, name, bufs, sems_in, sem_out_sizes, after):
    nb, ns, no = len(bufs), len(sems_in), len(sem_out_sizes)
    extra = [] if after is None else list(after) if isinstance(after, (tuple, list)) else [after]

    def kern(*refs):
        pos = nb + ns + len(extra)
        body(refs[:nb], refs[nb:nb + ns], refs[pos:pos + no])
        token_ref = refs[pos + no + nb]
        token_ref[...] = jnp.zeros_like(token_ref)

    out_shape = (tuple(pltpu.SemaphoreType.DMA((s,)) for s in sem_out_sizes)
                 + tuple(pltpu.HBM(b.shape, b.dtype) for b in bufs) + (jax.ShapeDtypeStruct((8, LANES), F32),))
    res = pl.pallas_call(
        kern, name=name, out_shape=out_shape,
        in_specs=[HBM_SPEC] * nb + [SEM_SPEC] * ns + [ANY_SPEC] * len(extra),
        out_specs=(SEM_SPEC,) * no + (HBM_SPEC,) * nb + (pl.BlockSpec(memory_space=pltpu.VMEM),),
        input_output_aliases={i: no + i for i in range(nb)},
        compiler_params=pltpu.CompilerParams(has_side_effects=pltpu.SideEffectType.DATAFLOW_SIDE_EFFECTING),
    )(*bufs, *sems_in, *extra)
    return res[:no], res[no:no + nb], res[no + nb]


def _gather_start(shards, lands, after, name):
    n = len(shards)

    def body(bufs, _, sems):
        ins, lnd = bufs[:n], bufs[n:]
        d2d_s, d2d_r, ici_s, ici_r = sems
        me = _dev_index(_peer(0))
        for j, k in enumerate(CHIPS):
            for i in range(n):
                _remote(ins[i], lnd[i].at[me], ici_s.at[j], ici_r.at[j], _peer(k)).start()
        for i in range(n):
            _remote(ins[i], lnd[i].at[me], d2d_s.at[0], d2d_r.at[0], _peer(1)).start()

    return _split_call(body, name, [_hbm(a) for a in (*shards, *lands)], [], (1, 1, 3, 3), after)


def _gather_forward(n, bufs, ici_r, after, name):
    def body(refs, sems_in, sems):
        ins, lnd = refs[:n], refs[n:]
        (arrived,) = sems_in
        fwd_s, fwd_r = sems
        for j, k in enumerate(CHIPS):
            blk = _dev_index(_peer(k))
            for i in range(n):
                _remote(ins[i], lnd[i].at[blk], fwd_s.at[j], arrived.at[j], _peer(k)).wait_recv()
            for i in range(n):
                _remote(lnd[i].at[blk], lnd[i].at[blk], fwd_s.at[j], fwd_r.at[j], _peer(1)).start()

    return _split_call(body, name, bufs, [ici_r], (3, 3), after)


def _gather_finish(n, bufs, d2d_s, d2d_r, ici_s, fwd_s, fwd_r, after, name):
    def body(refs, sems_in, _):
        ins, lnd = refs[:n], refs[n:]
        d2d_send, d2d_recv, ici_send, fwd_send, fwd_recv = sems_in
        sib = _peer(1)
        for i in range(n):
            cp = _remote(ins[i], lnd[i].at[_dev_index(sib)], d2d_send.at[0], d2d_recv.at[0], sib)
            cp.wait_send()
            cp.wait_recv()
        for j, k in enumerate(CHIPS):
            passed = _dev_index(_peer(k))
            landed = _dev_index(_peer(k | 1))
            for i in range(n):
                _remote(ins[i], lnd[i].at[passed], ici_send.at[j], fwd_recv.at[j], _peer(k)).wait_send()
                cp = _remote(lnd[i].at[passed], lnd[i].at[landed], fwd_send.at[j], fwd_recv.at[j], sib)
                cp.wait_send()
                cp.wait_recv()

    _, out, token = _split_call(body, name, bufs, [d2d_s, d2d_r, ici_s, fwd_s, fwd_r], (), after)
    return out[n:], token


def _exchange_start(parts, lands, after, name):
    n = len(parts)

    def body(bufs, _, sems):
        src, lnd = bufs[:n], bufs[n:]
        send, recv = sems
        me = _dev_index(_peer(0))
        for k in (4, 5, 2, 3, 6, 7, 1):
            to = _peer(k)
            for i in range(n):
                _remote(src[i].at[_dev_index(to)], lnd[i].at[me], send.at[k - 1], recv.at[k - 1], to).start()

    return _split_call(body, name, [_hbm(a) for a in (*parts, *lands)], [], (7, 7), after)


def _exchange_finish(n, bufs, send, recv, after, name):
    def body(refs, sems_in, _):
        src, lnd = refs[:n], refs[n:]
        send_, recv_ = sems_in
        me = _dev_index(_peer(0))
        for k in range(1, N_DEV):
            frm = _peer(k)
            for i in range(n):
                cp = _remote(src[i].at[me], lnd[i].at[_dev_index(frm)], send_.at[k - 1], recv_.at[k - 1], frm)
                cp.wait_send()
                cp.wait_recv()

    _, out, token = _split_call(body, name, bufs, [send, recv], (), after)
    return out[:n], out[n:], token


def _all_reduce_small(v, after, name):
    rows = v.shape[0]

    def body(v_ref, _, o_ref, buf, send_sems, recv_sems):
        me = _dev_index(_peer(0))
        buf[me] = v_ref[...]
        copies = []
        for k in range(1, N_DEV):
            copies.append(pltpu.make_async_remote_copy(
                src_ref=v_ref, dst_ref=buf.at[me], send_sem=send_sems.at[k - 1], recv_sem=recv_sems.at[k - 1],
                device_id=_peer(k), device_id_type=MESH))
        for cp in copies:
            cp.start()
        for k in range(1, N_DEV):
            pltpu.make_async_remote_copy(
                src_ref=v_ref, dst_ref=buf.at[_dev_index(_peer(k))], send_sem=send_sems.at[k - 1],
                recv_sem=recv_sems.at[k - 1], device_id=_peer(k), device_id_type=MESH).wait_recv()
        for cp in copies:
            cp.wait_send()
        acc = buf[0]
        for s in range(1, N_DEV):
            acc = acc + buf[s]
        o_ref[...] = acc

    vm = pl.BlockSpec(memory_space=pltpu.VMEM)
    return pl.pallas_call(
        body, name=name, in_specs=[vm, pl.BlockSpec(memory_space=pl.ANY)], out_specs=vm,
        out_shape=jax.ShapeDtypeStruct(v.shape, F32),
        scratch_shapes=[pltpu.VMEM((N_DEV, rows, LANES), F32), pltpu.SemaphoreType.DMA((7,)),
                        pltpu.SemaphoreType.DMA((7,))],
    )(v, after)


def _columns(cdim, ao):
    q_col = 2 * cdim
    attn_dim = N_GROUPS * ao
    return (q_col, q_col + attn_dim, q_col + 2 * attn_dim), q_col + 3 * attn_dim


def _layer_fwd(x, h1, sm, bg, get_rest, bias, hpg, next_gain, at_ff1=None):
    cdim = sm["conv_ln_g"].shape[0]
    ao = hpg * HEAD_DIM
    cols, gate_col = _columns(cdim, ao)
    qg2 = jnp.tile(sm["q_norm_g"], 2).reshape(1, LANES)
    kg2 = jnp.tile(sm["k_norm_g"], 2).reshape(1, LANES)
    u = _mm_in_pairs(h1, bg["w_in"], "mm_in")
    zc = _conv_fwd(u, bg["conv_dw_w"], sm["conv_dw_b"], cdim, "conv_fwd")
    zs = _ln_swish_fwd(zc, sm["conv_ln_g"], sm["conv_ln_b"], "ln_swish_fwd")
    os_, lses = [], []
    for gi in range(N_GROUPS):
        o_g, lse_g = _attn_fwd(u, qg2, kg2, bias, gi, cols, hpg, "attn_fwd_g%d" % gi)
        os_.append(o_g)
        lses.append(lse_g)
    o = _combine_fwd(os_, lses, "combine_fwd")
    bg = {**bg, **get_rest(o)}
    mg, yc, ya = _gated_out(zs, o, bg["w_conv_out"], bg["w_attn_out"], u, gate_col, "gated_out")
    x1, h2 = _mm(mg, bg["w_out"], epi="res_rms", extra=x, gain=sm["norm2_g"], name="mm_out")
    f = _mm(h2, bg["w_ff1"], out_dtype=BF16, name="mm_ff1")
    after = at_ff1(f) if at_ff1 is not None else None
    if next_gain is None:
        x2, h_next = _mm(f, bg["w_ff2"], a_relu2=True, epi="res", extra=x1, after=after, name="mm_ff2"), None
    else:
        x2, h_next = _mm(f, bg["w_ff2"], a_relu2=True, epi="res_rms", extra=x1, gain=next_gain, after=after,
                         name="mm_ff2")
    saved = dict(x=x, h1=h1, u=u, zc=zc, zs=zs, yc=yc, os=os_, lses=lses, o=o, ya=ya, mg=mg, x1=x1, h2=h2, f=f,
                 qg2=qg2, kg2=kg2)
    return x2, h_next, saved, bg


GRAD_GROUPS = (("w_ff2", "w_ff1"), ("w_out", "w_conv_out", "w_attn_out", "conv_dw_w"), ("w_in",))


def _layer_bwd(dx, s, sm, bg, bias, ds_sum, after, emit):
    cdim = sm["conv_ln_g"].shape[0]
    ao = bg["w_attn_out"].shape[0]
    hpg = ao // HEAD_DIM
    cols, gate_col = _columns(cdim, ao)
    g = {}
    df = _mm(dx, bg["w_ff2"], tb=True, epi="drelu2", extra=s["f"], out_dtype=BF16, after=after, name="mm_dff2")
    g["w_ff2"] = _mm(s["f"], dx, ta=True, a_relu2=True, out_dtype=BF16, name="mm_gw_ff2")
    g["w_ff1"] = _mm(s["h2"], df, ta=True, out_dtype=BF16, out_slots=True, name="mm_gw_ff1")
    after = emit(GRAD_GROUPS[0], g)
    dx1, dg2 = _mm(df, bg["w_ff1"], tb=True, epi="rms_bwd", extra=(s["x1"], dx), gain=sm["norm2_g"], after=after,
                   name="mm_dff1")
    g["norm2_g"] = dg2[0]
    dmg = _mm(dx1, bg["w_out"], tb=True, name="mm_dout")
    g["w_out"] = _mm(s["mg"], dx1, ta=True, out_dtype=BF16, name="mm_gw_out")
    dyc, dya, dugc, duga = _gate_bwd(dmg, s["u"], s["yc"], s["ya"], gate_col, "gate_bwd")
    dzs = _mm(dyc, bg["w_conv_out"], tb=True, name="mm_dconv_out")
    g["w_conv_out"] = _mm(s["zs"], dyc, ta=True, out_dtype=BF16, name="mm_gw_conv_out")
    do = _mm(dya, bg["w_attn_out"], tb=True, name="mm_dattn_out")
    g["w_attn_out"] = _mm(s["o"], dya, ta=True, out_dtype=BF16, name="mm_gw_attn_out")
    dzc, dlg, dlb = _ln_swish_bwd(dzs, s["zc"], sm["conv_ln_g"], sm["conv_ln_b"], "ln_swish_bwd")
    g["conv_ln_g"] = dlg[0]
    g["conv_ln_b"] = dlb[0]
    da, dgt, dcw, dcb = _conv_bwd(dzc, s["u"], bg["conv_dw_w"], cdim, "conv_bwd")
    g["conv_dw_w"] = dcw[:CONV_WIDTH].astype(BF16)
    g["conv_dw_b"] = dcb[0]
    after = emit(GRAD_GROUPS[1], g)
    do_gs, dd_gs = _combine_bwd(do, s["os"], s["lses"], "combine_bwd")
    du = lax.empty(s["u"].shape, BF16)
    for col, piece in ((0, da), (cdim, dgt), (gate_col, dugc), (gate_col + dugc.shape[1], duga)):
        du = lax.dynamic_update_slice(du, piece, (0, col))
    gqs, gks = [], []
    for gi in range(N_GROUPS):
        gq, gk, ds_sum, du = _attn_bwd(s["u"], do_gs[gi], dd_gs[gi], s["lses"][gi], s["qg2"], s["kg2"], bias,
                                       ds_sum, du, gi, cols, hpg, "attn_bwd_g%d" % gi)
        gqs.append(gq)
        gks.append(gk)
    g["q_norm_g"] = jnp.concatenate(gqs)
    g["k_norm_g"] = jnp.concatenate(gks)
    g["w_in"] = _mm_gw_in_pairs(s["h1"], du, after, "mm_gw_in")
    after = emit(GRAD_GROUPS[2], g)
    dx0, dg1 = _mm_din_pairs(du, bg["w_in"], s["x"], sm["norm1_g"], dx1, after, "mm_din")
    g["norm1_g"] = dg1[0]
    return dx0, g, ds_sum


BIG = ("w_in", "conv_dw_w", "w_conv_out", "w_attn_out", "w_out", "w_ff1", "w_ff2")
COL_SHARDED = ("w_in", "conv_dw_w", "w_conv_out", "w_attn_out", "w_ff1")
SMALL = ("rel_bias", "norm1_g", "q_norm_g", "k_norm_g", "conv_dw_b", "conv_ln_g", "conv_ln_b", "norm2_g")
WEIGHTS = ("rel_bias", "norm1_g", "w_in", "q_norm_g", "k_norm_g", "conv_dw_w", "conv_dw_b", "conv_ln_g", "conv_ln_b",
           "w_conv_out", "w_attn_out", "w_out", "norm2_g", "w_ff1", "w_ff2")


def _to_whole(name, gathered):
    n, a, b = gathered.shape
    if name in COL_SHARDED:
        return gathered.transpose(1, 0, 2).reshape(a, n * b)
    return gathered.reshape(n * a, b)


def _to_slots(name, whole):
    a, b = whole.shape
    if name in COL_SHARDED:
        return whole.reshape(a, N_DEV, b // N_DEV).transpose(1, 0, 2)
    return whole.reshape(N_DEV, a // N_DEV, b)


def _own_slot(block, me):
    land = lax.empty((N_DEV,) + block.shape, block.dtype)
    return lax.dynamic_update_slice(land, block[None], (me,) + (0,) * block.ndim)


def kernel(x, rel_bias, norm1_g, w_in, q_norm_g, k_norm_g, conv_dw_w, conv_dw_b, conv_ln_g, conv_ln_b, w_conv_out, w_attn_out, w_out, norm2_g, w_ff1, w_ff2, loss_target, m_rel_bias, m_norm1_g, m_w_in, m_q_norm_g, m_k_norm_g, m_conv_dw_w, m_conv_dw_b, m_conv_ln_g, m_conv_ln_b, m_w_conv_out, m_w_attn_out, m_w_out, m_norm2_g, m_w_ff1, m_w_ff2, v_rel_bias, v_norm1_g, v_w_in, v_q_norm_g, v_k_norm_g, v_conv_dw_w, v_conv_dw_b, v_conv_ln_g, v_conv_ln_b, v_w_conv_out, v_w_attn_out, v_w_out, v_norm2_g, v_w_ff1, v_w_ff2):
    w = dict(rel_bias=rel_bias, norm1_g=norm1_g, w_in=w_in, q_norm_g=q_norm_g, k_norm_g=k_norm_g, conv_dw_w=conv_dw_w,
             conv_dw_b=conv_dw_b, conv_ln_g=conv_ln_g, conv_ln_b=conv_ln_b, w_conv_out=w_conv_out,
             w_attn_out=w_attn_out, w_out=w_out, norm2_g=norm2_g, w_ff1=w_ff1, w_ff2=w_ff2)
    mom = dict(rel_bias=m_rel_bias, norm1_g=m_norm1_g, w_in=m_w_in, q_norm_g=m_q_norm_g, k_norm_g=m_k_norm_g,
               conv_dw_w=m_conv_dw_w, conv_dw_b=m_conv_dw_b, conv_ln_g=m_conv_ln_g, conv_ln_b=m_conv_ln_b,
               w_conv_out=m_w_conv_out, w_attn_out=m_w_attn_out, w_out=m_w_out, norm2_g=m_norm2_g, w_ff1=m_w_ff1,
               w_ff2=m_w_ff2)
    var = dict(rel_bias=v_rel_bias, norm1_g=v_norm1_g, w_in=v_w_in, q_norm_g=v_q_norm_g, k_norm_g=v_k_norm_g,
               conv_dw_w=v_conv_dw_w, conv_dw_b=v_conv_dw_b, conv_ln_g=v_conv_ln_g, conv_ln_b=v_conv_ln_b,
               w_conv_out=v_w_conv_out, w_attn_out=v_w_attn_out, w_out=v_w_out, norm2_g=v_norm2_g, w_ff1=v_w_ff1,
               w_ff2=v_w_ff2)

    depth = norm1_g.shape[0]
    me = 4 * lax.axis_index("x") + 2 * lax.axis_index("y") + lax.axis_index("c")
    odd_core = lax.axis_index("c") == 1
    hpg = w_attn_out.shape[1] // HEAD_DIM
    buckets = jnp.asarray(_bucket_table())

    first_names = ("w_in", "conv_dw_w")
    rest_names = tuple(k for k in BIG if k not in first_names)

    def chain_start(l, names, after):
        shards = [w[k][l] if k == "conv_dw_w" else w[k][l].astype(BF16) for k in names]
        if "w_in" in names:
            i = names.index("w_in")
            shards[i] = jnp.where(odd_core, jnp.pad(shards[i], ((0, 0), (SHIFT, 0))),
                                  jnp.pad(shards[i], ((0, 0), (0, SHIFT))))
        sems, bufs, token = _gather_start(shards, [_own_slot(s, me) for s in shards], after,
                                          "gather_start_%s_l%d" % (names[0], l))
        return dict(l=l, names=names, sems=sems, bufs=bufs, token=token)

    def chain_forward(ch, after):
        fwd, bufs, token = _gather_forward(len(ch["names"]), ch["bufs"], ch["sems"][3], after,
                                           "gather_forward_%s_l%d" % (ch["names"][0], ch["l"]))
        ch.update(fwd=fwd, bufs=bufs)
        return token

    def chain_finish(ch, after):
        d2d_s, d2d_r, ici_s, _ = ch["sems"]
        gathered, _ = _gather_finish(len(ch["names"]), ch["bufs"], d2d_s, d2d_r, ici_s, ch["fwd"][0], ch["fwd"][1],
                                     after, "gather_finish_%s_l%d" % (ch["names"][0], ch["l"]))
        out = {k: a if k == "w_in" else _to_whole(k, a) for k, a in zip(ch["names"], gathered)}
        if "conv_dw_w" in out:
            out["conv_dw_w"] = jnp.pad(out["conv_dw_w"], ((0, CONV_TAPS_PADDED - CONV_WIDTH), (0, 0)))
        return out

    xs = x[0]
    first = chain_start(0, first_names, None)
    h1 = _rms_fwd(xs, norm1_g[0], first["token"], "rms1_fwd")
    bias = _bias_expand(rel_bias, buckets, hpg, h1, "bias_expand")
    saved, bigs, smalls = [], [], []
    chains = {}
    for l in range(depth):
        sm = {k: w[k][l] for k in SMALL if k != "rel_bias"}
        if l == 0:
            token = chain_forward(first, bias)
            rest = chain_start(0, rest_names, token)
            bg = chain_finish(first, rest["token"])

            def get_rest(o, rest=rest):
                token = chain_forward(rest, o)
                if depth > 1:
                    chains[1] = (chain_start(1, first_names, token),)
                    chains[1] += (chain_start(1, rest_names, chains[1][0]["token"]),)
                    token = chains[1][1]["token"]
                return chain_finish(rest, token)
        elif l == 1:
            first, rest = chains[1]
            token = chain_forward(first, xs)
            if depth > 2:
                chains[2] = chain_start(2, BIG, token)
                token = chains[2]["token"]
            bg = chain_finish(first, token)

            def get_rest(o, rest=rest):
                return chain_finish(rest, chain_forward(rest, o))
        else:
            token = xs if "fwd" in chains[l] else chain_forward(chains[l], xs)
            if l + 1 < depth:
                chains[l + 1] = chain_start(l + 1, BIG, token)
                token = chains[l + 1]["token"]
            whole = chain_finish(chains[l], token)
            bg = {k: whole[k] for k in first_names}

            def get_rest(o, whole=whole):
                return {k: whole[k] for k in rest_names}
        at_ff1 = (lambda f, nxt=l + 1: chain_forward(chains[nxt], f)) if 2 <= l < depth - 1 else None
        xs, h1, sv, bg = _layer_fwd(xs, h1, sm, bg, get_rest, bias, hpg, norm1_g[l + 1] if l + 1 < depth else None,
                                    at_ff1)
        saved.append(sv)
        bigs.append(bg)
        smalls.append(sm)

    loss, dx = _loss_and_grad(xs, loss_target[0], "loss")

    ds_sum = jnp.zeros((N_GROUPS * hpg, BLOCK, 2 * BLOCK), F32)
    g = {k: [None] * depth for k in SMALL if k != "rel_bias"}
    sums = {k: lax.empty((depth, int(np.prod(w[k].shape[1:-1])), w[k].shape[-1] + (SHIFT if k == "w_in" else 0)), F32)
            for k in BIG}
    pending = []

    def finish_oldest(after):
        names, l, (send, recv), bufs = pending.pop(0)
        parts, recvd, token = _exchange_finish(len(names), bufs, send, recv, after,
                                               "exchange_finish_%s_l%d" % (names[0], l))
        for k, r, p in zip(names, recvd, parts):
            three = (N_DEV, -1, r.shape[-1])
            sums[k] = _sum_slots(r.reshape(three), p.reshape(three), me, sums[k], l, "sum_" + k)
        return token

    def make_emit(l):
        def emit(names, gl):
            parts = [gl[k] if k in ("w_ff1", "w_in") else _to_slots(k, gl[k]) for k in names]
            token = finish_oldest(parts[0]) if len(pending) >= len(GRAD_GROUPS) else None
            lands = [lax.empty(p.shape, p.dtype) for p in parts]
            sems, bufs, token = _exchange_start(parts, lands, token, "exchange_start_%s_l%d" % (names[0], l))
            pending.append((names, l, sems, bufs))
            return token
        return emit

    token = None
    for l in reversed(range(depth)):
        dx, gl, ds_sum = _layer_bwd(dx, saved[l], smalls[l], bigs[l], bias, ds_sum, token, make_emit(l))
        for k in g:
            g[k][l] = gl[k]
        token = None
    grad_x = dx

    g = {k: jnp.stack(v) for k, v in g.items()}
    for k in ("q_norm_g", "k_norm_g"):
        g[k] = jnp.sum(g[k].reshape(depth, -1, HEAD_DIM), axis=1)
    db = _bias_reduce(ds_sum, buckets, hpg, "bias_reduce")
    g["rel_bias"] = db[:, 0, :NUM_BUCKETS].T

    flat = jnp.concatenate([g[k].reshape(-1) for k in SMALL])
    nflat = flat.shape[0]
    rows = -(-nflat // (8 * LANES)) * 8
    packed = jnp.pad(flat, (0, rows * LANES - nflat)).reshape(rows, LANES)
    grad, outs = {}, {}
    token = dx
    while pending:
        names = pending[0][0]
        finish_oldest(token)
        for k in names:
            total = sums[k]
            if k == "w_in":
                total = jnp.where(odd_core, total[..., SHIFT:], total[..., :w_in.shape[-1]])
            grad[k] = total.reshape(w[k].shape)
            outs[k] = _adamw(w[k], grad[k], mom[k], var[k], "adamw_" + k)
        token = tuple(outs[k][0] for k in names)
    total = _all_reduce_small(packed, token[-1], "reduce_small").reshape(-1)
    off = 0
    for k in SMALL:
        size = int(np.prod(w[k].shape))
        grad[k] = total[off:off + size].reshape(w[k].shape)
        outs[k] = _adamw(w[k], grad[k], mom[k], var[k], "adamw_" + k)
        off += size
    loss = lax.psum(loss[0, 0], ("x", "y", "c"))
    return (loss, grad_x[None], *[grad[k] for k in WEIGHTS], *[outs[k][0] for k in WEIGHTS],
            *[outs[k][1] for k in WEIGHTS], *[outs[k][2] for k in WEIGHTS])
```

```python
import functools
import math

import numpy as np
import jax
import jax.numpy as jnp
from jax import lax
from jax.experimental import pallas as pl
from jax.experimental.pallas import tpu as pltpu

F32 = jnp.float32
BF16 = jnp.bfloat16

HEAD_DIM = 64
N_GROUPS = 3
DILATIONS = (1, 4, 16)
SUB_WINDOW = 128
BLOCK = 128
CONV_WIDTH = 31
CONV_TAPS_PADDED = 32
NUM_BUCKETS = 32
MAX_REL_DISTANCE = 2048
EPS = 1e-6
NEG_INF = -1e30
LANES = 128
SUBLANES = 8

ADAM_LR = 0.001
ADAM_B1 = 0.9
ADAM_B2 = 0.999
ADAM_EPS = 1e-08
ADAM_WD = 0.01
ADAM_STEP = 10

N_DEV = 8
VMEM_LIMIT = 56 * 1024 * 1024
MESH = pl.DeviceIdType.MESH


def _cparams(sem=None):
    return pltpu.CompilerParams(dimension_semantics=sem, vmem_limit_bytes=VMEM_LIMIT)


def _tile(n, target):
    if n <= target:
        return n
    t = (target // LANES) * LANES
    while t >= LANES:
        if n % t == 0:
            return t
        t -= LANES
    return n


def _sigmoid(v):
    return 1.0 / (1.0 + jnp.exp(-v))


MM_VMEM_BUDGET = 40 * 1024 * 1024


def _rms_apply(x, g):
    return x * lax.rsqrt(jnp.mean(x * x, axis=-1, keepdims=True) + EPS) * g


def _rms_grad(dh, x, g):
    r = lax.rsqrt(jnp.mean(x * x, axis=-1, keepdims=True) + EPS)
    xh = x * r
    dxh = dh * g
    dx = r * (dxh - xh * jnp.mean(dxh * xh, axis=-1, keepdims=True))
    return dx, jnp.sum(dh * xh, axis=0, keepdims=True)


def _mm_tiles(m, n, kdim, a_bytes, b_bytes, io_bytes, whole_rows=False, temps=2):
    def need(tm, tn, tk):
        blocks = 2 * (tm * tk * a_bytes + tk * tn * b_bytes + tm * tn * io_bytes)
        casts = (tm * tk * 2 if a_bytes == 4 else 0) + (tk * tn * 2 if b_bytes == 4 else 0)
        return blocks + casts + temps * tm * tn * 4

    tn = n if whole_rows else _tile(n, 1024)
    while True:
        fits = [(tm * tk, tm, tk) for tm in {_tile(m, c) for c in (1024, 512, 256, 128)}
                for tk in {_tile(kdim, c) for c in (2048, 1024, 512, 256)} if need(tm, tn, tk) <= MM_VMEM_BUDGET]
        if fits:
            _, tm, tk = max(fits)
            return tm, tn, tk
        assert not whole_rows and tn % 256 == 0, "no block size fits the VMEM budget"
        tn //= 2


def _mm(a, b, *, ta=False, tb=False, out_dtype=F32, epi=None, extra=(), gain=None, after=None, out_slots=False,
        a_relu2=False, name):
    extra = tuple(extra) if isinstance(extra, (tuple, list)) else (extra,)
    m = a.shape[1] if ta else a.shape[0]
    kdim = a.shape[0] if ta else a.shape[1]
    n = b.shape[0] if tb else b.shape[1]
    norm = epi in ("res_rms", "rms_bwd")
    io_bytes = (jnp.dtype(out_dtype).itemsize + sum(e.dtype.itemsize for e in extra) + (2 if epi == "res_rms" else 0))
    tm, tn, tk = _mm_tiles(m, n // N_DEV if out_slots else n, kdim, a.dtype.itemsize, b.dtype.itemsize, io_bytes,
                           whole_rows=norm, temps=6 if norm else 2)
    if out_slots:
        assert epi is None and tn == n // N_DEV
    nk = kdim // tk
    a_spec = pl.BlockSpec((tk, tm), lambda i, j, k: (k, i)) if ta else pl.BlockSpec((tm, tk), lambda i, j, k: (i, k))
    b_spec = pl.BlockSpec((tn, tk), lambda i, j, k: (j, k)) if tb else pl.BlockSpec((tk, tn), lambda i, j, k: (k, j))
    o_spec = (pl.BlockSpec((None, tm, tn), lambda i, j, k: (j, i, 0)) if out_slots
              else pl.BlockSpec((tm, tn), lambda i, j, k: (i, j)))
    v_spec = pl.BlockSpec((1, tn), lambda i, j, k: (0, j))
    dims = (((0 if ta else 1,), (1 if tb else 0,)), ((), ()))
    n_extra = len(extra)
    n_in = 2 + n_extra + (gain is not None) + (after is not None)
    n_out = 2 if norm else 1

    def body(*refs):
        a_ref, b_ref = refs[0], refs[1]
        e_refs = refs[2:2 + n_extra]
        g_ref = refs[2 + n_extra] if gain is not None else None
        outs = refs[n_in:n_in + n_out]

        def product():
            av = a_ref[...]
            if a_relu2:
                r = jnp.maximum(av.astype(F32), 0.0)
                av = r * r
            return lax.dot_general(av.astype(BF16), b_ref[...].astype(BF16), dims, preferred_element_type=F32)

        def finish(acc):
            if epi is None:
                outs[0][...] = acc.astype(outs[0].dtype)
            elif epi == "res":
                outs[0][...] = (e_refs[0][...] + acc).astype(outs[0].dtype)
            elif epi == "drelu2":
                outs[0][...] = (acc * (2.0 * jnp.maximum(e_refs[0][...].astype(F32), 0.0))).astype(outs[0].dtype)
            elif epi == "res_rms":
                x1 = e_refs[0][...] + acc
                outs[0][...] = x1
                outs[1][...] = _rms_apply(x1, g_ref[...]).astype(BF16)
            elif epi == "rms_bwd":
                dx, dg = _rms_grad(acc, e_refs[0][...], g_ref[...])
                outs[0][...] = e_refs[1][...] + dx
                i = pl.program_id(0)

                @pl.when(i == 0)
                def _():
                    outs[1][...] = dg

                @pl.when(i > 0)
                def _():
                    outs[1][...] += dg

        if nk == 1:
            finish(product())
            return
        acc_ref = refs[-1]
        k = pl.program_id(2)

        @pl.when(k == 0)
        def _():
            acc_ref[...] = product()

        @pl.when(jnp.logical_and(k > 0, k < nk - 1))
        def _():
            acc_ref[...] += product()

        @pl.when(k == nk - 1)
        def _():
            finish(acc_ref[...] + product())

    in_specs = ([a_spec, b_spec] + [o_spec] * n_extra + ([v_spec] if gain is not None else [])
                + ([pl.BlockSpec(memory_space=pl.ANY)] if after is not None else []))
    if epi == "res_rms":
        out_shape = (jax.ShapeDtypeStruct((m, n), F32), jax.ShapeDtypeStruct((m, n), BF16))
        out_specs = (o_spec, o_spec)
    elif epi == "rms_bwd":
        out_shape = (jax.ShapeDtypeStruct((m, n), F32), jax.ShapeDtypeStruct((1, n), F32))
        out_specs = (o_spec, v_spec)
    else:
        out_shape = jax.ShapeDtypeStruct((N_DEV, m, tn) if out_slots else (m, n), out_dtype)
        out_specs = o_spec
    args = (a, b) + extra + ((gain.reshape(1, n),) if gain is not None else ()) + ((after,) if after is not None else ())
    return pl.pallas_call(
        body, name=name, grid=(m // tm, n // tn, nk), in_specs=in_specs, out_specs=out_specs, out_shape=out_shape,
        scratch_shapes=[pltpu.VMEM((tm, tn), F32)] if nk > 1 else [],
        compiler_params=_cparams(("arbitrary", "arbitrary", "arbitrary")),
    )(*args)


SHIFT = HEAD_DIM


def _pair_blocks(e, o):
    wp = e.shape[-1]
    return e[:, :wp - LANES], e[:, wp - LANES:] + o[:, :LANES], o[:, LANES:]


def _mm_in_pairs(a, wg, name):
    t, kdim = a.shape
    wp = wg.shape[-1]
    ws = wp - SHIFT
    tm = _tile(t, 1024)

    def body(a_ref, e_ref, o_ref, u_ref):
        av = a_ref[...]
        lo, mid, hi = _pair_blocks(e_ref[...], o_ref[...])
        u_ref[:, :wp - LANES] = jnp.dot(av, lo, preferred_element_type=F32).astype(BF16)
        u_ref[:, wp - LANES:wp] = jnp.dot(av, mid, preferred_element_type=F32).astype(BF16)
        u_ref[:, wp:] = jnp.dot(av, hi, preferred_element_type=F32).astype(BF16)

    return pl.pallas_call(
        body, name=name, grid=(N_DEV // 2, t // tm),
        in_specs=[pl.BlockSpec((tm, kdim), lambda p, i: (i, 0)),
                  pl.BlockSpec((None, kdim, wp), lambda p, i: (2 * p, 0, 0)),
                  pl.BlockSpec((None, kdim, wp), lambda p, i: (2 * p + 1, 0, 0))],
        out_specs=pl.BlockSpec((tm, 2 * ws), lambda p, i: (i, p)),
        out_shape=jax.ShapeDtypeStruct((t, N_DEV * ws), BF16), compiler_params=_cparams(("arbitrary", "arbitrary")),
    )(a, wg, wg)


def _mm_din_pairs(du, wg, x, gain, dres, after, name):
    t = du.shape[0]
    _, kdim, wp = wg.shape
    ws = wp - SHIFT
    tm = _tile(t, 512)
    npair = N_DEV // 2
    lanes = (((1,), (1,)), ((), ()))
    extra = [] if after is None else [after]

    def body(d_ref, e_ref, o_ref, x_ref, g_ref, r_ref, *rest):
        dx_ref, dg_ref, acc_ref = rest[-3], rest[-2], rest[-1]
        i, p = pl.program_id(0), pl.program_id(1)
        lo, mid, hi = _pair_blocks(e_ref[...], o_ref[...])
        part = (lax.dot_general(d_ref[:, :wp - LANES], lo, lanes, preferred_element_type=F32)
                + lax.dot_general(d_ref[:, wp - LANES:wp], mid, lanes, preferred_element_type=F32)
                + lax.dot_general(d_ref[:, wp:], hi, lanes, preferred_element_type=F32))

        @pl.when(p == 0)
        def _():
            acc_ref[...] = part

        @pl.when(jnp.logical_and(p > 0, p < npair - 1))
        def _():
            acc_ref[...] += part

        @pl.when(p == npair - 1)
        def _():
            dx, dg = _rms_grad(acc_ref[...] + part, x_ref[...], g_ref[...])
            dx_ref[...] = r_ref[...] + dx

            @pl.when(i == 0)
            def _():
                dg_ref[...] = dg

            @pl.when(i > 0)
            def _():
                dg_ref[...] += dg

    row = pl.BlockSpec((tm, kdim), lambda i, p: (i, 0))
    vec = pl.BlockSpec((1, kdim), lambda i, p: (0, 0))
    return pl.pallas_call(
        body, name=name, grid=(t // tm, npair),
        in_specs=[pl.BlockSpec((tm, 2 * ws), lambda i, p: (i, p)),
                  pl.BlockSpec((None, kdim, wp), lambda i, p: (2 * p, 0, 0)),
                  pl.BlockSpec((None, kdim, wp), lambda i, p: (2 * p + 1, 0, 0)), row, vec, row]
        + [pl.BlockSpec(memory_space=pl.ANY)] * len(extra),
        out_specs=(row, vec),
        out_shape=(jax.ShapeDtypeStruct((t, kdim), F32), jax.ShapeDtypeStruct((1, kdim), F32)),
        scratch_shapes=[pltpu.VMEM((tm, kdim), F32)], compiler_params=_cparams(("arbitrary", "arbitrary")),
    )(du, wg, wg, x, gain.reshape(1, kdim), dres, *extra)


def _mm_gw_in_pairs(h, du, after, name):
    t, kdim = h.shape
    ws = du.shape[1] // N_DEV
    wp = ws + SHIFT
    tm = _tile(kdim, 512)
    rows = (((0,), (0,)), ((), ()))
    extra = [] if after is None else [after]

    def body(h_ref, d_ref, *rest):
        g_ref = rest[-1]
        g = lax.dot_general(h_ref[...], d_ref[...], rows, preferred_element_type=F32)
        g_ref[0] = g[:, :wp].astype(BF16)
        g_ref[1] = g[:, wp - LANES:].astype(BF16)

    return pl.pallas_call(
        body, name=name, grid=(N_DEV // 2, kdim // tm),
        in_specs=[pl.BlockSpec((t, tm), lambda p, i: (0, i)), pl.BlockSpec((t, 2 * ws), lambda p, i: (0, p))]
        + [pl.BlockSpec(memory_space=pl.ANY)] * len(extra),
        out_specs=pl.BlockSpec((2, tm, wp), lambda p, i: (p, i, 0)),
        out_shape=jax.ShapeDtypeStruct((N_DEV, kdim, wp), BF16), compiler_params=_cparams(("arbitrary", "arbitrary")),
    )(h, du, *extra)


ROW_BLOCK_BUDGET = 24 * 1024 * 1024


def _rows(t, row_bytes):
    rows = t
    while rows > 8 and (2 * rows * row_bytes > ROW_BLOCK_BUDGET or t % rows):
        rows //= 2
    return rows


def _rms_fwd(x, g, after, name):
    t, d = x.shape
    ROWS = _rows(t, 6 * d)

    def body(x_ref, g_ref, _, h_ref):
        h_ref[...] = _rms_apply(x_ref[...], g_ref[...]).astype(BF16)

    return pl.pallas_call(
        body, name=name, grid=(t // ROWS,),
        in_specs=[pl.BlockSpec((ROWS, d), lambda i: (i, 0)), pl.BlockSpec((1, d), lambda i: (0, 0)),
                  pl.BlockSpec(memory_space=pl.ANY)],
        out_specs=pl.BlockSpec((ROWS, d), lambda i: (i, 0)),
        out_shape=jax.ShapeDtypeStruct((t, d), BF16), compiler_params=_cparams(("arbitrary",)),
    )(x, g.reshape(1, d), after)


def _gated_out(zs, o, wc, wa, u, gate_col, name):
    t = zs.shape[0]
    d = wc.shape[1]
    td = math.gcd(_tile(d, 512), gate_col)
    nd = d // td
    c0 = gate_col // td
    tm = _tile(t, 1024)

    def body(zs_ref, o_ref, wc_ref, wa_ref, gc_ref, ga_ref, m_ref, yc_ref, ya_ref):
        yc = jnp.dot(zs_ref[...], wc_ref[...], preferred_element_type=F32)
        ya = jnp.dot(o_ref[...], wa_ref[...], preferred_element_type=F32)
        gc = _sigmoid(gc_ref[...].astype(F32))
        ga = _sigmoid(ga_ref[...].astype(F32))
        m_ref[...] = (gc * yc + ga * ya).astype(BF16)
        yc_ref[...] = yc.astype(BF16)
        ya_ref[...] = ya.astype(BF16)

    blk = pl.BlockSpec((tm, td), lambda i, j: (i, j))
    sh = jax.ShapeDtypeStruct((t, d), BF16)
    return pl.pallas_call(
        body, name=name, grid=(t // tm, nd),
        in_specs=[pl.BlockSpec((tm, zs.shape[1]), lambda i, j: (i, 0)), pl.BlockSpec((tm, o.shape[1]), lambda i, j: (i, 0)),
                  pl.BlockSpec((wc.shape[0], td), lambda i, j: (0, j)), pl.BlockSpec((wa.shape[0], td), lambda i, j: (0, j)),
                  pl.BlockSpec((tm, td), lambda i, j: (i, c0 + j)), pl.BlockSpec((tm, td), lambda i, j: (i, c0 + nd + j))],
        out_specs=(blk, blk, blk), out_shape=(sh, sh, sh), compiler_params=_cparams(("arbitrary", "arbitrary")),
    )(zs, o, wc, wa, u, u)


def _gate_bwd(dm, u, yc, ya, gate_col, name):
    t, d = yc.shape
    td = math.gcd(_tile(d, 512), gate_col)
    nd = d // td
    c0 = gate_col // td
    ROWS = _rows(t, 20 * td)

    def body(dm_ref, gc_ref, ga_ref, yc_ref, ya_ref, dyc_ref, dya_ref, dugc_ref, duga_ref):
        dmv = dm_ref[...]
        gc = _sigmoid(gc_ref[...].astype(F32))
        ga = _sigmoid(ga_ref[...].astype(F32))
        dyc_ref[...] = (dmv * gc).astype(BF16)
        dya_ref[...] = (dmv * ga).astype(BF16)
        dugc_ref[...] = (dmv * yc_ref[...].astype(F32) * gc * (1.0 - gc)).astype(BF16)
        duga_ref[...] = (dmv * ya_ref[...].astype(F32) * ga * (1.0 - ga)).astype(BF16)

    blk = pl.BlockSpec((ROWS, td), lambda i, j: (i, j))
    o = jax.ShapeDtypeStruct((t, d), BF16)
    return pl.pallas_call(
        body, name=name, grid=(t // ROWS, nd),
        in_specs=[blk, pl.BlockSpec((ROWS, td), lambda i, j: (i, c0 + j)),
                  pl.BlockSpec((ROWS, td), lambda i, j: (i, c0 + nd + j)), blk, blk],
        out_specs=(blk, blk, blk, blk), out_shape=(o, o, o, o),
        compiler_params=_cparams(("arbitrary", "arbitrary")),
    )(dm, u, u, yc, ya)


def _loss_and_grad(y, target, name):
    t, d = y.shape
    ROWS = _rows(t, 12 * d)
    n = t // ROWS

    def body(y_ref, t_ref, loss_ref, dy_ref, acc_ref):
        i = pl.program_id(0)

        @pl.when(i == 0)
        def _():
            acc_ref[...] = jnp.zeros_like(acc_ref)

        diff = y_ref[...] - t_ref[...]
        dy_ref[...] = diff * (1.0 / d)
        acc_ref[...] += jnp.sum(diff * diff, axis=0, keepdims=True)

        @pl.when(i == n - 1)
        def _():
            loss_ref[...] = jnp.sum(acc_ref[...], axis=-1, keepdims=True) * (0.5 / d)

    row = pl.BlockSpec((ROWS, d), lambda i: (i, 0))
    return pl.pallas_call(
        body, name=name, grid=(n,), in_specs=[row, row],
        out_specs=(pl.BlockSpec((1, 1), lambda i: (0, 0)), row),
        out_shape=(jax.ShapeDtypeStruct((1, 1), F32), jax.ShapeDtypeStruct((t, d), F32)),
        scratch_shapes=[pltpu.VMEM((1, d), F32)], compiler_params=_cparams(("arbitrary",)),
    )(y, target)


HALO = 32


def _conv_fwd(u, w, b, cdim, name):
    t = u.shape[0]
    ncb = cdim // LANES
    nt = t // BLOCK

    def body(a_ref, g_ref, w_ref, b_ref, zc_ref, zpad):
        zpad[0:HALO, :] = jnp.zeros((HALO, LANES), F32)
        zpad[HALO:HALO + t, :] = a_ref[...].astype(F32) * _sigmoid(g_ref[...].astype(F32))
        wv = w_ref[...]
        bv = b_ref[...]

        def tile(i, carry):
            r0 = pl.multiple_of(i * BLOCK, BLOCK)
            win = zpad[pl.ds(r0, BLOCK + HALO), :]
            acc = jnp.zeros((BLOCK, LANES), F32) + bv
            for b in range(SUBLANES):
                sh = win if b == 0 else pltpu.roll(win, b, 0)
                for a in range(HALO // SUBLANES):
                    j = CONV_WIDTH - 1 - (SUBLANES * a + b)
                    if j >= 0:
                        lo = HALO - SUBLANES * a
                        acc = acc + wv[j:j + 1, :] * sh[lo:lo + BLOCK, :]
            zc_ref[pl.ds(r0, BLOCK), :] = acc
            return carry

        lax.fori_loop(0, nt, tile, 0)

    col = lambda off: pl.BlockSpec((t, LANES), lambda c: (0, off + c))
    return pl.pallas_call(
        body, name=name, grid=(ncb,),
        in_specs=[col(0), col(ncb), pl.BlockSpec((CONV_TAPS_PADDED, LANES), lambda c: (0, c)),
                  pl.BlockSpec((1, LANES), lambda c: (0, c))],
        out_specs=pl.BlockSpec((t, LANES), lambda c: (0, c)),
        out_shape=jax.ShapeDtypeStruct((t, cdim), F32),
        scratch_shapes=[pltpu.VMEM((t + HALO, LANES), F32)], compiler_params=_cparams(("arbitrary",)),
    )(u, u, w, b.reshape(1, cdim))


def _conv_bwd(dzc, u, w, cdim, name):
    t = u.shape[0]
    ncb = cdim // LANES
    nt = t // BLOCK
    win_rows = BLOCK + HALO

    def body(dzc_ref, a_ref, g_ref, w_ref, da_ref, dg_ref, dw_ref, db_ref, zpad, dpad):
        av = a_ref[...].astype(F32)
        sg = _sigmoid(g_ref[...].astype(F32))
        zpad[0:HALO, :] = jnp.zeros((HALO, LANES), F32)
        zpad[HALO:HALO + t, :] = av * sg
        dpad[0:t, :] = dzc_ref[...]
        dpad[t:t + HALO, :] = jnp.zeros((HALO, LANES), F32)
        dw_ref[...] = jnp.zeros_like(dw_ref)
        db_ref[...] = jnp.sum(dzc_ref[...], axis=0, keepdims=True)
        wv = w_ref[...]

        def tile(i, carry):
            r0 = pl.multiple_of(i * BLOCK, BLOCK)
            zwin = zpad[pl.ds(r0, win_rows), :]
            dwin = dpad[pl.ds(r0, win_rows), :]
            dcur = dwin[0:BLOCK, :]
            dz = jnp.zeros((BLOCK, LANES), F32)
            for b in range(SUBLANES):
                zs = zwin if b == 0 else pltpu.roll(zwin, b, 0)
                ds = dwin if b == 0 else pltpu.roll(dwin, win_rows - b, 0)
                for a in range(HALO // SUBLANES):
                    j = CONV_WIDTH - 1 - (SUBLANES * a + b)
                    if j >= 0:
                        lo = HALO - SUBLANES * a
                        dw_ref[j:j + 1, :] += jnp.sum(dcur * zs[lo:lo + BLOCK, :], axis=0, keepdims=True)
                        dz = dz + wv[j:j + 1, :] * ds[SUBLANES * a:SUBLANES * a + BLOCK, :]
            ac = a_ref[pl.ds(r0, BLOCK), :].astype(F32)
            sc = _sigmoid(g_ref[pl.ds(r0, BLOCK), :].astype(F32))
            da_ref[pl.ds(r0, BLOCK), :] = (dz * sc).astype(BF16)
            dg_ref[pl.ds(r0, BLOCK), :] = (dz * ac * sc * (1.0 - sc)).astype(BF16)
            return carry

        lax.fori_loop(0, nt, tile, 0)

    col = lambda off: pl.BlockSpec((t, LANES), lambda c: (0, off + c))
    wspec = pl.BlockSpec((CONV_TAPS_PADDED, LANES), lambda c: (0, c))
    o = jax.ShapeDtypeStruct((t, cdim), BF16)
    return pl.pallas_call(
        body, name=name, grid=(ncb,), in_specs=[col(0), col(0), col(ncb), wspec],
        out_specs=(col(0), col(0), wspec, pl.BlockSpec((1, LANES), lambda c: (0, c))),
        out_shape=(o, o, jax.ShapeDtypeStruct((CONV_TAPS_PADDED, cdim), F32), jax.ShapeDtypeStruct((1, cdim), F32)),
        scratch_shapes=[pltpu.VMEM((t + HALO, LANES), F32), pltpu.VMEM((t + HALO, LANES), F32)],
        compiler_params=_cparams(("arbitrary",)),
    )(dzc, u, u, w)


def _ln_swish_fwd(zc, g, b, name):
    t, c = zc.shape
    ROWS = _rows(t, 6 * c)

    def body(z_ref, g_ref, b_ref, o_ref):
        z = z_ref[...]
        mu = jnp.mean(z, axis=-1, keepdims=True)
        zc_ = z - mu
        zn = zc_ * lax.rsqrt(jnp.mean(zc_ * zc_, axis=-1, keepdims=True) + EPS)
        y = zn * g_ref[...] + b_ref[...]
        o_ref[...] = (y * _sigmoid(y)).astype(BF16)

    row = pl.BlockSpec((ROWS, c), lambda i: (i, 0))
    vec = pl.BlockSpec((1, c), lambda i: (0, 0))
    return pl.pallas_call(
        body, name=name, grid=(t // ROWS,), in_specs=[row, vec, vec], out_specs=row,
        out_shape=jax.ShapeDtypeStruct((t, c), BF16), compiler_params=_cparams(("arbitrary",)),
    )(zc, g.reshape(1, c), b.reshape(1, c))


def _ln_swish_bwd(dzs, zc, g, b, name):
    t, c = zc.shape
    ROWS = _rows(t, 12 * c)

    def body(d_ref, z_ref, g_ref, b_ref, dz_ref, dg_ref, db_ref):
        @pl.when(pl.program_id(0) == 0)
        def _():
            dg_ref[...] = jnp.zeros_like(dg_ref)
            db_ref[...] = jnp.zeros_like(db_ref)

        z = z_ref[...]
        mu = jnp.mean(z, axis=-1, keepdims=True)
        zc_ = z - mu
        rstd = lax.rsqrt(jnp.mean(zc_ * zc_, axis=-1, keepdims=True) + EPS)
        zn = zc_ * rstd
        y = zn * g_ref[...] + b_ref[...]
        sg = _sigmoid(y)
        dy = d_ref[...] * (sg * (1.0 + y * (1.0 - sg)))
        dg_ref[...] += jnp.sum(dy * zn, axis=0, keepdims=True)
        db_ref[...] += jnp.sum(dy, axis=0, keepdims=True)
        dzn = dy * g_ref[...]
        dz_ref[...] = rstd * (dzn - jnp.mean(dzn, axis=-1, keepdims=True)
                              - zn * jnp.mean(dzn * zn, axis=-1, keepdims=True))

    row = pl.BlockSpec((ROWS, c), lambda i: (i, 0))
    vec = pl.BlockSpec((1, c), lambda i: (0, 0))
    v = jax.ShapeDtypeStruct((1, c), F32)
    return pl.pallas_call(
        body, name=name, grid=(t // ROWS,), in_specs=[row, row, vec, vec], out_specs=(row, vec, vec),
        out_shape=(jax.ShapeDtypeStruct((t, c), F32), v, v), compiler_params=_cparams(("arbitrary",)),
    )(dzs, zc, g.reshape(1, c), b.reshape(1, c))


def _bucket_table():
    qi = np.arange(BLOCK)[:, None]
    kj = np.arange(2 * BLOCK)[None, :]
    off = qi + BLOCK - kj
    band = (off >= 0) & (off <= SUB_WINDOW)
    max_exact = NUM_BUCKETS // 2
    out = []
    for d in DILATIONS:
        dist = (np.clip(off, 0, SUB_WINDOW) * d).astype(np.int32)
        nf = np.maximum(dist, 1).astype(np.float32)
        large = max_exact + (np.log(nf / np.float32(max_exact)) / np.float32(math.log(MAX_REL_DISTANCE / max_exact))
                             * np.float32(NUM_BUCKETS - max_exact)).astype(np.int32)
        large = np.minimum(large, NUM_BUCKETS - 1)
        bucket = np.where(dist < max_exact, dist, large)
        out.append(np.where(band, bucket, -1))
    return np.stack(out).astype(np.int32)


def _bias_expand(rel_bias, buckets, hpg, after, name):
    nh = N_GROUPS * hpg

    def body(rb_ref, bk_ref, _, o_ref):
        h = pl.program_id(0)
        bk = bk_ref[0]
        acc = jnp.full((BLOCK, 2 * BLOCK), NEG_INF, F32)
        for bb in range(NUM_BUCKETS):
            acc = jnp.where(bk == bb, rb_ref[bb, h], acc)
        o_ref[0] = acc

    return pl.pallas_call(
        body, name=name, grid=(nh,),
        in_specs=[pl.BlockSpec(memory_space=pltpu.SMEM),
                  pl.BlockSpec((1, BLOCK, 2 * BLOCK), lambda h: (h // hpg, 0, 0)), pl.BlockSpec(memory_space=pl.ANY)],
        out_specs=pl.BlockSpec((1, BLOCK, 2 * BLOCK), lambda h: (h, 0, 0)),
        out_shape=jax.ShapeDtypeStruct((nh, BLOCK, 2 * BLOCK), F32), compiler_params=_cparams(("arbitrary",)),
    )(rel_bias, buckets, after)


def _bias_reduce(ds_sum, buckets, hpg, name):
    nh = N_GROUPS * hpg

    def body(ds_ref, bk_ref, o_ref):
        bk = bk_ref[0]
        dsv = ds_ref[0]
        lane = lax.broadcasted_iota(jnp.int32, (1, LANES), 1)
        row = jnp.zeros((1, LANES), F32)
        for bb in range(NUM_BUCKETS):
            tot = jnp.sum(jnp.sum(jnp.where(bk == bb, dsv, 0.0), axis=-1, keepdims=True), axis=0, keepdims=True)
            row = jnp.where(lane == bb, tot, row)
        o_ref[0] = row

    return pl.pallas_call(
        body, name=name, grid=(nh,),
        in_specs=[pl.BlockSpec((1, BLOCK, 2 * BLOCK), lambda h: (h, 0, 0)),
                  pl.BlockSpec((1, BLOCK, 2 * BLOCK), lambda h: (h // hpg, 0, 0))],
        out_specs=pl.BlockSpec((1, 1, LANES), lambda h: (h, 0, 0)),
        out_shape=jax.ShapeDtypeStruct((nh, 1, LANES), F32), compiler_params=_cparams(("arbitrary",)),
    )(ds_sum, buckets)


def _chunk_rows(c, d, nb):
    r, n = c // nb, c % nb
    if d == 1:
        return pl.ds(c * BLOCK, BLOCK)
    return pl.ds(r + n * BLOCK * d, BLOCK, stride=d)


def _segment_ones():
    i = lax.broadcasted_iota(jnp.int32, (LANES, LANES), 0) // HEAD_DIM
    j = lax.broadcasted_iota(jnp.int32, (LANES, LANES), 1) // HEAD_DIM
    return (i == j).astype(BF16)


def _segment_sum(v, seg):
    hi = v.astype(BF16)
    lo = (v - hi.astype(F32)).astype(BF16)
    return jnp.dot(hi, seg, preferred_element_type=F32) + jnp.dot(lo, seg, preferred_element_type=F32)


def _head_mean(v, seg):
    return _segment_sum(v, seg) * (1.0 / HEAD_DIM)


def _attn_fwd(u, qg2, kg2, bias, gi, cols, hpg, name):
    t = u.shape[0]
    d = DILATIONS[gi]
    nchunk = t // BLOCK
    nb = (t // d) // BLOCK
    hp = hpg // 2
    qc0, kc0, vc0 = [(c + gi * hpg * HEAD_DIM) // LANES for c in cols]
    contract_lanes = (((1,), (1,)), ((), ()))

    def body(q_ref, k_ref, v_ref, qg_ref, kg_ref, bias_ref, o_ref, lse_ref, qd, kd, vd, od, ld, sbuf, nat):
        seg = _segment_ones()
        lane = lax.broadcasted_iota(jnp.int32, (1, LANES), 1)
        qg = qg_ref[...] * (HEAD_DIM ** -0.5)
        kg = kg_ref[...]
        kd[0:BLOCK, :] = jnp.zeros((BLOCK, LANES), BF16)
        vd[0:BLOCK, :] = jnp.zeros((BLOCK, LANES), BF16)
        od[...] = q_ref[...].astype(F32)
        ld[...] = k_ref[...].astype(F32)
        for c in range(nchunk):
            rows = _chunk_rows(c, d, nb)
            qv = od[rows, :]
            kv = ld[rows, :]
            qd[c * BLOCK:(c + 1) * BLOCK, :] = (qv * lax.rsqrt(_head_mean(qv * qv, seg) + EPS) * qg).astype(BF16)
            kd[(c + 1) * BLOCK:(c + 2) * BLOCK, :] = (kv * lax.rsqrt(_head_mean(kv * kv, seg) + EPS) * kg).astype(BF16)
        od[...] = v_ref[...].astype(F32)
        for c in range(nchunk):
            vd[(c + 1) * BLOCK:(c + 2) * BLOCK, :] = od[_chunk_rows(c, d, nb), :].astype(BF16)

        col = lax.broadcasted_iota(jnp.int32, (BLOCK, 2 * BLOCK), 1)
        for j in range(2):
            mj = jnp.logical_and(lane >= j * HEAD_DIM, lane < (j + 1) * HEAD_DIM)
            for c in range(nchunk):
                kw = kd[c * BLOCK:(c + 2) * BLOCK, :]
                kj = jnp.where(mj, kw, jnp.zeros_like(kw))
                s = lax.dot_general(qd[c * BLOCK:(c + 1) * BLOCK, :], kj, contract_lanes,
                                    preferred_element_type=F32) + bias_ref[j]
                if c % nb == 0:
                    s = jnp.where(col < BLOCK, NEG_INF, s)
                sbuf[c] = s
            for c in range(nchunk):
                rows = slice(c * BLOCK, (c + 1) * BLOCK)
                s = sbuf[c]
                mx = jnp.max(s, axis=-1, keepdims=True)
                p = jnp.exp(s - mx).astype(BF16)
                vw = vd[c * BLOCK:(c + 2) * BLOCK, :]
                oj = jnp.dot(p, jnp.where(mj, vw, jnp.ones_like(vw)), preferred_element_type=F32)
                l = pltpu.roll(oj, HEAD_DIM, 1)
                on = oj / l
                ls = mx + jnp.log(l)
                if j == 0:
                    od[rows, :] = on
                    ld[rows, :] = ls
                else:
                    od[rows, :] = jnp.where(mj, on, od[rows, :])
                    ld[rows, :] = jnp.where(mj, ls, ld[rows, :])

        for c in range(nchunk):
            rows = _chunk_rows(c, d, nb)
            nat[rows, :] = od[c * BLOCK:(c + 1) * BLOCK, :]
            lse_ref[rows, :] = ld[c * BLOCK:(c + 1) * BLOCK, :]
        o_ref[...] = nat[...].astype(BF16)

    ucol = lambda c0: pl.BlockSpec((t, LANES), lambda h: (0, c0 + h))
    vec = pl.BlockSpec((1, LANES), lambda h: (0, 0))
    oblk = pl.BlockSpec((t, LANES), lambda h: (0, h))
    osh = jax.ShapeDtypeStruct((t, hpg * HEAD_DIM), F32)
    osh16 = jax.ShapeDtypeStruct((t, hpg * HEAD_DIM), BF16)
    return pl.pallas_call(
        body, name=name, grid=(hp,),
        in_specs=[ucol(qc0), ucol(kc0), ucol(vc0), vec, vec,
                  pl.BlockSpec((2, BLOCK, 2 * BLOCK), lambda h: (gi * hp + h, 0, 0))],
        out_specs=(oblk, oblk), out_shape=(osh16, osh),
        scratch_shapes=[pltpu.VMEM((t, LANES), BF16), pltpu.VMEM((t + BLOCK, LANES), BF16),
                        pltpu.VMEM((t + BLOCK, LANES), BF16), pltpu.VMEM((t, LANES), F32), pltpu.VMEM((t, LANES), F32),
                        pltpu.VMEM((nchunk, BLOCK, 2 * BLOCK), F32), pltpu.VMEM((t, LANES), F32)],
        compiler_params=_cparams(("arbitrary",)),
    )(u, u, u, qg2, kg2, bias)


def _attn_bwd(u, do_g, dd_g, lse_g, qg2, kg2, bias, ds_in, du_in, gi, cols, hpg, name):
    t = u.shape[0]
    d = DILATIONS[gi]
    nchunk = t // BLOCK
    nb = (t // d) // BLOCK
    hp = hpg // 2
    qc0, kc0, vc0 = [(c + gi * hpg * HEAD_DIM) // LANES for c in cols]
    contract_lanes = (((1,), (1,)), ((), ()))
    contract_rows = (((0,), (0,)), ((), ()))
    qscale = HEAD_DIM ** -0.5

    def body(q_ref, k_ref, v_ref, do_ref, dd_ref, lse_ref, qg_ref, kg_ref, bias_ref, dsin_ref, _du_in,
             dgq_ref, dgk_ref, dsout_ref, du_ref,
             qd, kd, vd, dod, ddd, ld, dqd, dkd, dvd, dsacc, pbuf, dsbuf, qs, ks, qst, kst, vst, out_sems):
        h = pl.program_id(0)

        def flush(step):
            return [pltpu.make_async_copy(
                st, du_ref.at[:, pl.ds(pl.multiple_of((c0 + step) * LANES, LANES), LANES)], out_sems.at[i])
                for i, (st, c0) in enumerate(((qst, qc0), (kst, kc0), (vst, vc0)))]

        seg = _segment_ones()
        lane = lax.broadcasted_iota(jnp.int32, (1, LANES), 1)
        qg = qg_ref[...] * qscale
        kg = kg_ref[...]
        kd[0:BLOCK, :] = jnp.zeros((BLOCK, LANES), BF16)
        vd[0:BLOCK, :] = jnp.zeros((BLOCK, LANES), BF16)
        dsacc[...] = jnp.zeros_like(dsacc)
        qs[...] = q_ref[...].astype(F32)
        ks[...] = k_ref[...].astype(F32)
        dqd[...] = v_ref[...].astype(F32)
        dkd[...] = do_ref[...].astype(F32)
        for c in range(nchunk):
            rows = _chunk_rows(c, d, nb)
            qv = qs[rows, :]
            kv = ks[rows, :]
            qd[c * BLOCK:(c + 1) * BLOCK, :] = (qv * lax.rsqrt(_head_mean(qv * qv, seg) + EPS) * qg).astype(BF16)
            kd[(c + 1) * BLOCK:(c + 2) * BLOCK, :] = (kv * lax.rsqrt(_head_mean(kv * kv, seg) + EPS) * kg).astype(BF16)
            vd[(c + 1) * BLOCK:(c + 2) * BLOCK, :] = dqd[rows, :].astype(BF16)
            dod[c * BLOCK:(c + 1) * BLOCK, :] = dkd[rows, :].astype(BF16)
            ddd[c * BLOCK:(c + 1) * BLOCK, :] = dd_ref[rows, :]
            ld[c * BLOCK:(c + 1) * BLOCK, :] = lse_ref[rows, :]

        col = lax.broadcasted_iota(jnp.int32, (BLOCK, 2 * BLOCK), 1)
        for j in range(2):
            mj = jnp.logical_and(lane >= j * HEAD_DIM, lane < (j + 1) * HEAD_DIM)
            first = lane == j * HEAD_DIM
            for c in range(nchunk):
                rows = slice(c * BLOCK, (c + 1) * BLOCK)
                kw = kd[c * BLOCK:(c + 2) * BLOCK, :]
                vw = vd[c * BLOCK:(c + 2) * BLOCK, :]
                kj = jnp.where(mj, kw, jnp.zeros_like(kw))
                vj = jnp.where(mj, vw, jnp.zeros_like(vw))
                s = lax.dot_general(qd[rows, :], kj, contract_lanes, preferred_element_type=F32) + bias_ref[j]
                if c % nb == 0:
                    s = jnp.where(col < BLOCK, NEG_INF, s)
                dp = lax.dot_general(dod[rows, :], vj, contract_lanes, preferred_element_type=F32)
                lse_j = jnp.sum(jnp.where(first, ld[rows, :], 0.0), axis=-1, keepdims=True)
                dd_j = jnp.sum(jnp.where(first, ddd[rows, :], 0.0), axis=-1, keepdims=True)
                p = jnp.exp(s - lse_j)
                ds = p * (dp + dd_j)
                dsacc[j] += ds
                pbuf[j, c] = p.astype(BF16)
                dsbuf[j, c] = ds.astype(BF16)
        for c in range(nchunk):
            rows = slice(c * BLOCK, (c + 1) * BLOCK)
            has_next = c + 1 < nchunk and (c + 1) % nb != 0
            kw = kd[c * BLOCK:(c + 2) * BLOCK, :]
            dq = jnp.zeros((BLOCK, LANES), F32)
            dk = jnp.zeros((BLOCK, LANES), F32)
            dv = jnp.zeros((BLOCK, LANES), F32)
            both = slice(c * BLOCK, (c + 2) * BLOCK) if has_next else rows
            for j in range(2):
                mj = jnp.logical_and(lane >= j * HEAD_DIM, lane < (j + 1) * HEAD_DIM)
                dq = dq + jnp.dot(dsbuf[j, c], jnp.where(mj, kw, jnp.zeros_like(kw)), preferred_element_type=F32)
                dsk = dsbuf[j, c, :, BLOCK:]
                pk = pbuf[j, c, :, BLOCK:]
                if has_next:
                    dsk = jnp.concatenate([dsk, dsbuf[j, c + 1, :, :BLOCK]], axis=0)
                    pk = jnp.concatenate([pk, pbuf[j, c + 1, :, :BLOCK]], axis=0)
                qq = qd[both, :]
                dd = dod[both, :]
                dk = dk + lax.dot_general(dsk, jnp.where(mj, qq, jnp.zeros_like(qq)), contract_rows,
                                          preferred_element_type=F32)
                dv = dv + lax.dot_general(pk, jnp.where(mj, dd, jnp.zeros_like(dd)), contract_rows,
                                          preferred_element_type=F32)
            dqd[rows, :] = dq
            dkd[rows, :] = dk
            dvd[rows, :] = dv

        dsout_ref[...] = dsin_ref[...] + dsacc[...]

        dgq = jnp.zeros((1, LANES), F32)
        dgk = jnp.zeros((1, LANES), F32)
        for c in range(nchunk):
            rows = _chunk_rows(c, d, nb)
            qv = qs[rows, :]
            rq = lax.rsqrt(_head_mean(qv * qv, seg) + EPS)
            qh = qv * rq
            dy = dqd[c * BLOCK:(c + 1) * BLOCK, :]
            dgq = dgq + jnp.sum(dy * qh, axis=0, keepdims=True) * qscale
            dxh = dy * qg
            ddd[rows, :] = rq * (dxh - qh * _head_mean(dxh * qh, seg))
            kv = ks[rows, :]
            rk = lax.rsqrt(_head_mean(kv * kv, seg) + EPS)
            kh = kv * rk
            dy = dkd[c * BLOCK:(c + 1) * BLOCK, :]
            dgk = dgk + jnp.sum(dy * kh, axis=0, keepdims=True)
            dxh = dy * kg
            ld[rows, :] = rk * (dxh - kh * _head_mean(dxh * kh, seg))
        @pl.when(h > 0)
        def _():
            for cp in flush(h - 1):
                cp.wait()

        qst[...] = ddd[...].astype(BF16)
        kst[...] = ld[...].astype(BF16)
        for c in range(nchunk):
            ddd[_chunk_rows(c, d, nb), :] = dvd[c * BLOCK:(c + 1) * BLOCK, :]
        vst[...] = ddd[...].astype(BF16)
        for cp in flush(h):
            cp.start()

        @pl.when(h == hp - 1)
        def _():
            for cp in flush(h):
                cp.wait()

        dgq_ref[0] = dgq
        dgk_ref[0] = dgk

    ucol = lambda c0: pl.BlockSpec((t, LANES), lambda h: (0, c0 + h))
    vec = pl.BlockSpec((1, LANES), lambda h: (0, 0))
    oblk = pl.BlockSpec((t, LANES), lambda h: (0, h))
    bblk = pl.BlockSpec((2, BLOCK, 2 * BLOCK), lambda h: (gi * hp + h, 0, 0))
    gblk = pl.BlockSpec((1, 1, LANES), lambda h: (h, 0, 0))
    gsh = jax.ShapeDtypeStruct((hp, 1, LANES), F32)
    hbm = pl.BlockSpec(memory_space=pl.ANY)
    return pl.pallas_call(
        body, name=name, grid=(hp,),
        in_specs=[ucol(qc0), ucol(kc0), ucol(vc0), oblk, oblk, oblk, vec, vec, bblk, bblk, hbm],
        out_specs=(gblk, gblk, bblk, hbm),
        out_shape=(gsh, gsh, jax.ShapeDtypeStruct(ds_in.shape, F32), jax.ShapeDtypeStruct(du_in.shape, BF16)),
        input_output_aliases={9: 2, 10: 3},
        scratch_shapes=[pltpu.VMEM((t, LANES), BF16), pltpu.VMEM((t + BLOCK, LANES), BF16),
                        pltpu.VMEM((t + BLOCK, LANES), BF16), pltpu.VMEM((t, LANES), BF16),
                        pltpu.VMEM((t, LANES), F32), pltpu.VMEM((t, LANES), F32), pltpu.VMEM((t, LANES), F32),
                        pltpu.VMEM((t, LANES), F32), pltpu.VMEM((t, LANES), F32),
                        pltpu.VMEM((2, BLOCK, 2 * BLOCK), F32), pltpu.VMEM((2, nchunk, BLOCK, 2 * BLOCK), BF16),
                        pltpu.VMEM((2, nchunk, BLOCK, 2 * BLOCK), BF16), pltpu.VMEM((t, LANES), F32),
                        pltpu.VMEM((t, LANES), F32), pltpu.VMEM((t, LANES), BF16), pltpu.VMEM((t, LANES), BF16),
                        pltpu.VMEM((t, LANES), BF16), pltpu.SemaphoreType.DMA((3,))],
        compiler_params=_cparams(("arbitrary",)),
    )(u, u, u, do_g, dd_g, lse_g, qg2, kg2, bias, ds_in, du_in)


def _group_weights(l0, l1, l2):
    mx = jnp.maximum(jnp.maximum(l0, l1), l2)
    e0, e1, e2 = jnp.exp(l0 - mx), jnp.exp(l1 - mx), jnp.exp(l2 - mx)
    inv = 1.0 / (e0 + e1 + e2)
    return e0 * inv, e1 * inv, e2 * inv


def _combine_fwd(os_, lses, name):
    t, ao = os_[0].shape
    ROWS = _rows(t, 20 * ao)

    def body(o0, o1, o2, l0, l1, l2, o_ref):
        w0, w1, w2 = _group_weights(l0[...], l1[...], l2[...])
        o_ref[...] = (w0 * o0[...].astype(F32) + w1 * o1[...].astype(F32) + w2 * o2[...].astype(F32)).astype(BF16)

    row = pl.BlockSpec((ROWS, ao), lambda i: (i, 0))
    return pl.pallas_call(
        body, name=name, grid=(t // ROWS,), in_specs=[row] * 6, out_specs=row,
        out_shape=jax.ShapeDtypeStruct((t, ao), BF16), compiler_params=_cparams(("arbitrary",)),
    )(*os_, *lses)


def _combine_bwd(do, os_, lses, name):
    t, ao = do.shape
    idx = np.arange(ao) // HEAD_DIM
    seg = jnp.asarray((idx[:, None] == idx[None, :]).astype(np.float32), dtype=BF16)
    ROWS = _rows(t, 40 * ao)

    def body(do_ref, o0, o1, o2, l0, l1, l2, seg_ref, g0, g1, g2, d0, d1, d2):
        w0, w1, w2 = _group_weights(l0[...], l1[...], l2[...])
        dov = do_ref[...]
        o = w0 * o0[...].astype(F32) + w1 * o1[...].astype(F32) + w2 * o2[...].astype(F32)
        sd = _segment_sum(dov * o, seg_ref[...])
        for w, gref, dref in ((w0, g0, d0), (w1, g1, d1), (w2, g2, d2)):
            gref[...] = (w * dov).astype(BF16)
            dref[...] = -(w * sd)

    row = pl.BlockSpec((ROWS, ao), lambda i: (i, 0))
    sh = jax.ShapeDtypeStruct((t, ao), F32)
    sh16 = jax.ShapeDtypeStruct((t, ao), BF16)
    outs = pl.pallas_call(
        body, name=name, grid=(t // ROWS,), in_specs=[row] * 7 + [pl.BlockSpec((ao, ao), lambda i: (0, 0))],
        out_specs=(row,) * 6, out_shape=(sh16,) * 3 + (sh,) * 3, compiler_params=_cparams(("arbitrary",)),
    )(do, *os_, *lses, seg)
    return outs[:3], outs[3:]


def _adamw(w, g, m, v, name):
    shape = w.shape
    cols = shape[-1]
    rows = int(np.prod(shape[:-1]))
    tr = rows if rows <= 512 else _tile_rows(rows)
    c1 = 1.0 - ADAM_B1 ** ADAM_STEP
    c2 = 1.0 - ADAM_B2 ** ADAM_STEP

    def body(w_ref, g_ref, m_ref, v_ref, d_ref, nm_ref, nv_ref):
        gv = g_ref[...]
        mn = ADAM_B1 * m_ref[...] + (1.0 - ADAM_B1) * gv
        vn = ADAM_B2 * v_ref[...] + (1.0 - ADAM_B2) * (gv * gv)
        nm_ref[...] = mn
        nv_ref[...] = vn
        d_ref[...] = -ADAM_LR * ((mn / c1) / (jnp.sqrt(vn / c2) + ADAM_EPS) + ADAM_WD * w_ref[...])

    blk = pl.BlockSpec((tr, cols), lambda i: (i, 0))
    sh = jax.ShapeDtypeStruct((rows, cols), F32)
    outs = pl.pallas_call(
        body, name=name, grid=(rows // tr,), in_specs=[blk] * 4, out_specs=(blk,) * 3, out_shape=(sh,) * 3,
        compiler_params=_cparams(("arbitrary",)),
    )(*[a.reshape(rows, cols) for a in (w, g, m, v)])
    return tuple(o.reshape(shape) for o in outs)


def _tile_rows(rows):
    for t in (512, 256, 128, 64, 32, 16, 8):
        if rows % t == 0:
            return t
    return rows


def _sum_slots(recv, parts, me, layers, l, name):
    _, rows, cols = recv.shape
    tr = rows if rows <= 512 else _tile_rows(rows)

    def body(me_ref, r_ref, own_ref, _, o_ref):
        acc = jnp.zeros(o_ref.shape, F32)
        for s in range(N_DEV):
            acc = acc + jnp.where(me_ref[0] == s, own_ref[...], r_ref[s]).astype(F32)
        o_ref[...] = acc

    return pl.pallas_call(
        body, name=name,
        grid_spec=pltpu.PrefetchScalarGridSpec(
            num_scalar_prefetch=1, grid=(rows // tr,),
            in_specs=[pl.BlockSpec((N_DEV, tr, cols), lambda i, me: (0, i, 0)),
                      pl.BlockSpec((None, tr, cols), lambda i, me: (me[0], i, 0)),
                      pl.BlockSpec(memory_space=pl.ANY)],
            out_specs=pl.BlockSpec((None, tr, cols), lambda i, me: (l, i, 0))),
        out_shape=jax.ShapeDtypeStruct(layers.shape, F32), input_output_aliases={3: 0},
        compiler_params=_cparams(("arbitrary",)),
    )(me.reshape(1), recv, parts, layers)


def _peer(k):
    x, y, c = lax.axis_index("x"), lax.axis_index("y"), lax.axis_index("c")
    return (1 - x if k & 4 else x, 1 - y if k & 2 else y, 1 - c if k & 1 else c)


def _dev_index(p):
    return 4 * p[0] + 2 * p[1] + p[2]


HBM_SPEC = pl.BlockSpec(memory_space=pltpu.HBM)
SEM_SPEC = pl.BlockSpec(memory_space=pltpu.SEMAPHORE)
ANY_SPEC = pl.BlockSpec(memory_space=pl.ANY)
CHIPS = (4, 2, 6)


def _remote(src, dst, send_sem, recv_sem, to):
    return pltpu.make_async_remote_copy(src_ref=src, dst_ref=dst, send_sem=send_sem, recv_sem=recv_sem,
                                        device_id=to, device_id_type=MESH)


def _hbm(a):
    return pltpu.with_memory_space_constraint(a, pltpu.HBM)


def _split_call(body, name, bufs, sems_in, sem_out_sizes, after):
    nb, ns, no = len(bufs), len(sems_in), len(sem_out_sizes)
    extra = [] if after is None else list(after) if isinstance(after, (tuple, list)) else [after]

    def kern(*refs):
        pos = nb + ns + len(extra)
        body(refs[:nb], refs[nb:nb + ns], refs[pos:pos + no])
        token_ref = refs[pos + no + nb]
        token_ref[...] = jnp.zeros_like(token_ref)

    out_shape = (tuple(pltpu.SemaphoreType.DMA((s,)) for s in sem_out_sizes)
                 + tuple(pltpu.HBM(b.shape, b.dtype) for b in bufs) + (jax.ShapeDtypeStruct((8, LANES), F32),))
    res = pl.pallas_call(
        kern, name=name, out_shape=out_shape,
        in_specs=[HBM_SPEC] * nb + [SEM_SPEC] * ns + [ANY_SPEC] * len(extra),
        out_specs=(SEM_SPEC,) * no + (HBM_SPEC,) * nb + (pl.BlockSpec(memory_space=pltpu.VMEM),),
        input_output_aliases={i: no + i for i in range(nb)},
        compiler_params=pltpu.CompilerParams(has_side_effects=pltpu.SideEffectType.DATAFLOW_SIDE_EFFECTING),
    )(*bufs, *sems_in, *extra)
    return res[:no], res[no:no + nb], res[no + nb]


def _gather_start(shards, lands, after, name):
    n = len(shards)

    def body(bufs, _, sems):
        ins, lnd = bufs[:n], bufs[n:]
        d2d_s, d2d_r, ici_s, ici_r = sems
        me = _dev_index(_peer(0))
        for j, k in enumerate(CHIPS):
            for i in range(n):
                _remote(ins[i], lnd[i].at[me], ici_s.at[j], ici_r.at[j], _peer(k)).start()
        for i in range(n):
            _remote(ins[i], lnd[i].at[me], d2d_s.at[0], d2d_r.at[0], _peer(1)).start()

    return _split_call(body, name, [_hbm(a) for a in (*shards, *lands)], [], (1, 1, 3, 3), after)


def _gather_forward(n, bufs, ici_r, after, name):
    def body(refs, sems_in, sems):
        ins, lnd = refs[:n], refs[n:]
        (arrived,) = sems_in
        fwd_s, fwd_r = sems
        for j, k in enumerate(CHIPS):
            blk = _dev_index(_peer(k))
            for i in range(n):
                _remote(ins[i], lnd[i].at[blk], fwd_s.at[j], arrived.at[j], _peer(k)).wait_recv()
            for i in range(n):
                _remote(lnd[i].at[blk], lnd[i].at[blk], fwd_s.at[j], fwd_r.at[j], _peer(1)).start()

    return _split_call(body, name, bufs, [ici_r], (3, 3), after)


def _gather_finish(n, bufs, d2d_s, d2d_r, ici_s, fwd_s, fwd_r, after, name):
    def body(refs, sems_in, _):
        ins, lnd = refs[:n], refs[n:]
        d2d_send, d2d_recv, ici_send, fwd_send, fwd_recv = sems_in
        sib = _peer(1)
        for i in range(n):
            cp = _remote(ins[i], lnd[i].at[_dev_index(sib)], d2d_send.at[0], d2d_recv.at[0], sib)
            cp.wait_send()
            cp.wait_recv()
        for j, k in enumerate(CHIPS):
            passed = _dev_index(_peer(k))
            landed = _dev_index(_peer(k | 1))
            for i in range(n):
                _remote(ins[i], lnd[i].at[passed], ici_send.at[j], fwd_recv.at[j], _peer(k)).wait_send()
                cp = _remote(lnd[i].at[passed], lnd[i].at[landed], fwd_send.at[j], fwd_recv.at[j], sib)
                cp.wait_send()
                cp.wait_recv()

    _, out, token = _split_call(body, name, bufs, [d2d_s, d2d_r, ici_s, fwd_s, fwd_r], (), after)
    return out[n:], token


def _exchange_start(parts, lands, after, name):
    n = len(parts)

    def body(bufs, _, sems):
        src, lnd = bufs[:n], bufs[n:]
        send, recv = sems
        me = _dev_index(_peer(0))
        for k in (4, 5, 2, 3, 6, 7, 1):
            to = _peer(k)
            for i in range(n):
                _remote(src[i].at[_dev_index(to)], lnd[i].at[me], send.at[k - 1], recv.at[k - 1], to).start()

    return _split_call(body, name, [_hbm(a) for a in (*parts, *lands)], [], (7, 7), after)


def _exchange_finish(n, bufs, send, recv, after, name):
    def body(refs, sems_in, _):
        src, lnd = refs[:n], refs[n:]
        send_, recv_ = sems_in
        me = _dev_index(_peer(0))
        for k in range(1, N_DEV):
            frm = _peer(k)
            for i in range(n):
                cp = _remote(src[i].at[me], lnd[i].at[_dev_index(frm)], send_.at[k - 1], recv_.at[k - 1], frm)
                cp.wait_send()
                cp.wait_recv()

    _, out, token = _split_call(body, name, bufs, [send, recv], (), after)
    return out[:n], out[n:], token


def _all_reduce_small(v, after, name):
    rows = v.shape[0]

    def body(v_ref, _, o_ref, buf, send_sems, recv_sems):
        me = _dev_index(_peer(0))
        buf[me] = v_ref[...]
        copies = []
        for k in range(1, N_DEV):
            copies.append(pltpu.make_async_remote_copy(
                src_ref=v_ref, dst_ref=buf.at[me], send_sem=send_sems.at[k - 1], recv_sem=recv_sems.at[k - 1],
                device_id=_peer(k), device_id_type=MESH))
        for cp in copies:
            cp.start()
        for k in range(1, N_DEV):
            pltpu.make_async_remote_copy(
                src_ref=v_ref, dst_ref=buf.at[_dev_index(_peer(k))], send_sem=send_sems.at[k - 1],
                recv_sem=recv_sems.at[k - 1], device_id=_peer(k), device_id_type=MESH).wait_recv()
        for cp in copies:
            cp.wait_send()
        acc = buf[0]
        for s in range(1, N_DEV):
            acc = acc + buf[s]
        o_ref[...] = acc

    vm = pl.BlockSpec(memory_space=pltpu.VMEM)
    return pl.pallas_call(
        body, name=name, in_specs=[vm, pl.BlockSpec(memory_space=pl.ANY)], out_specs=vm,
        out_shape=jax.ShapeDtypeStruct(v.shape, F32),
        scratch_shapes=[pltpu.VMEM((N_DEV, rows, LANES), F32), pltpu.SemaphoreType.DMA((7,)),
                        pltpu.SemaphoreType.DMA((7,))],
    )(v, after)


def _columns(cdim, ao):
    q_col = 2 * cdim
    attn_dim = N_GROUPS * ao
    return (q_col, q_col + attn_dim, q_col + 2 * attn_dim), q_col + 3 * attn_dim


def _layer_fwd(x, h1, sm, bg, get_rest, bias, hpg, next_gain, at_ff1=None):
    cdim = sm["conv_ln_g"].shape[0]
    ao = hpg * HEAD_DIM
    cols, gate_col = _columns(cdim, ao)
    qg2 = jnp.tile(sm["q_norm_g"], 2).reshape(1, LANES)
    kg2 = jnp.tile(sm["k_norm_g"], 2).reshape(1, LANES)
    u = _mm_in_pairs(h1, bg["w_in"], "mm_in")
    zc = _conv_fwd(u, bg["conv_dw_w"], sm["conv_dw_b"], cdim, "conv_fwd")
    zs = _ln_swish_fwd(zc, sm["conv_ln_g"], sm["conv_ln_b"], "ln_swish_fwd")
    os_, lses = [], []
    for gi in range(N_GROUPS):
        o_g, lse_g = _attn_fwd(u, qg2, kg2, bias, gi, cols, hpg, "attn_fwd_g%d" % gi)
        os_.append(o_g)
        lses.append(lse_g)
    o = _combine_fwd(os_, lses, "combine_fwd")
    bg = {**bg, **get_rest(o)}
    mg, yc, ya = _gated_out(zs, o, bg["w_conv_out"], bg["w_attn_out"], u, gate_col, "gated_out")
    x1, h2 = _mm(mg, bg["w_out"], epi="res_rms", extra=x, gain=sm["norm2_g"], name="mm_out")
    f = _mm(h2, bg["w_ff1"], out_dtype=BF16, name="mm_ff1")
    after = at_ff1(f) if at_ff1 is not None else None
    if next_gain is None:
        x2, h_next = _mm(f, bg["w_ff2"], a_relu2=True, epi="res", extra=x1, after=after, name="mm_ff2"), None
    else:
        x2, h_next = _mm(f, bg["w_ff2"], a_relu2=True, epi="res_rms", extra=x1, gain=next_gain, after=after,
                         name="mm_ff2")
    saved = dict(x=x, h1=h1, u=u, zc=zc, zs=zs, yc=yc, os=os_, lses=lses, o=o, ya=ya, mg=mg, x1=x1, h2=h2, f=f,
                 qg2=qg2, kg2=kg2)
    return x2, h_next, saved, bg


GRAD_GROUPS = (("w_ff2", "w_ff1"), ("w_out", "w_conv_out", "w_attn_out", "conv_dw_w"), ("w_in",))


def _layer_bwd(dx, s, sm, bg, bias, ds_sum, after, emit):
    cdim = sm["conv_ln_g"].shape[0]
    ao = bg["w_attn_out"].shape[0]
    hpg = ao // HEAD_DIM
    cols, gate_col = _columns(cdim, ao)
    g = {}
    df = _mm(dx, bg["w_ff2"], tb=True, epi="drelu2", extra=s["f"], out_dtype=BF16, after=after, name="mm_dff2")
    g["w_ff2"] = _mm(s["f"], dx, ta=True, a_relu2=True, out_dtype=BF16, name="mm_gw_ff2")
    g["w_ff1"] = _mm(s["h2"], df, ta=True, out_dtype=BF16, out_slots=True, name="mm_gw_ff1")
    after = emit(GRAD_GROUPS[0], g)
    dx1, dg2 = _mm(df, bg["w_ff1"], tb=True, epi="rms_bwd", extra=(s["x1"], dx), gain=sm["norm2_g"], after=after,
                   name="mm_dff1")
    g["norm2_g"] = dg2[0]
    dmg = _mm(dx1, bg["w_out"], tb=True, name="mm_dout")
    g["w_out"] = _mm(s["mg"], dx1, ta=True, out_dtype=BF16, name="mm_gw_out")
    dyc, dya, dugc, duga = _gate_bwd(dmg, s["u"], s["yc"], s["ya"], gate_col, "gate_bwd")
    dzs = _mm(dyc, bg["w_conv_out"], tb=True, name="mm_dconv_out")
    g["w_conv_out"] = _mm(s["zs"], dyc, ta=True, out_dtype=BF16, name="mm_gw_conv_out")
    do = _mm(dya, bg["w_attn_out"], tb=True, name="mm_dattn_out")
    g["w_attn_out"] = _mm(s["o"], dya, ta=True, out_dtype=BF16, name="mm_gw_attn_out")
    dzc, dlg, dlb = _ln_swish_bwd(dzs, s["zc"], sm["conv_ln_g"], sm["conv_ln_b"], "ln_swish_bwd")
    g["conv_ln_g"] = dlg[0]
    g["conv_ln_b"] = dlb[0]
    da, dgt, dcw, dcb = _conv_bwd(dzc, s["u"], bg["conv_dw_w"], cdim, "conv_bwd")
    g["conv_dw_w"] = dcw[:CONV_WIDTH].astype(BF16)
    g["conv_dw_b"] = dcb[0]
    after = emit(GRAD_GROUPS[1], g)
    do_gs, dd_gs = _combine_bwd(do, s["os"], s["lses"], "combine_bwd")
    du = lax.empty(s["u"].shape, BF16)
    for col, piece in ((0, da), (cdim, dgt), (gate_col, dugc), (gate_col + dugc.shape[1], duga)):
        du = lax.dynamic_update_slice(du, piece, (0, col))
    gqs, gks = [], []
    for gi in range(N_GROUPS):
        gq, gk, ds_sum, du = _attn_bwd(s["u"], do_gs[gi], dd_gs[gi], s["lses"][gi], s["qg2"], s["kg2"], bias,
                                       ds_sum, du, gi, cols, hpg, "attn_bwd_g%d" % gi)
        gqs.append(gq)
        gks.append(gk)
    g["q_norm_g"] = jnp.concatenate(gqs)
    g["k_norm_g"] = jnp.concatenate(gks)
    g["w_in"] = _mm_gw_in_pairs(s["h1"], du, after, "mm_gw_in")
    after = emit(GRAD_GROUPS[2], g)
    dx0, dg1 = _mm_din_pairs(du, bg["w_in"], s["x"], sm["norm1_g"], dx1, after, "mm_din")
    g["norm1_g"] = dg1[0]
    return dx0, g, ds_sum


BIG = ("w_in", "conv_dw_w", "w_conv_out", "w_attn_out", "w_out", "w_ff1", "w_ff2")
COL_SHARDED = ("w_in", "conv_dw_w", "w_conv_out", "w_attn_out", "w_ff1")
SMALL = ("rel_bias", "norm1_g", "q_norm_g", "k_norm_g", "conv_dw_b", "conv_ln_g", "conv_ln_b", "norm2_g")
WEIGHTS = ("rel_bias", "norm1_g", "w_in", "q_norm_g", "k_norm_g", "conv_dw_w", "conv_dw_b", "conv_ln_g", "conv_ln_b",
           "w_conv_out", "w_attn_out", "w_out", "norm2_g", "w_ff1", "w_ff2")


def _to_whole(name, gathered):
    n, a, b = gathered.shape
    if name in COL_SHARDED:
        return gathered.transpose(1, 0, 2).reshape(a, n * b)
    return gathered.reshape(n * a, b)


def _to_slots(name, whole):
    a, b = whole.shape
    if name in COL_SHARDED:
        return whole.reshape(a, N_DEV, b // N_DEV).transpose(1, 0, 2)
    return whole.reshape(N_DEV, a // N_DEV, b)


def _own_slot(block, me):
    land = lax.empty((N_DEV,) + block.shape, block.dtype)
    return lax.dynamic_update_slice(land, block[None], (me,) + (0,) * block.ndim)


def kernel(x, rel_bias, norm1_g, w_in, q_norm_g, k_norm_g, conv_dw_w, conv_dw_b, conv_ln_g, conv_ln_b, w_conv_out, w_attn_out, w_out, norm2_g, w_ff1, w_ff2, loss_target, m_rel_bias, m_norm1_g, m_w_in, m_q_norm_g, m_k_norm_g, m_conv_dw_w, m_conv_dw_b, m_conv_ln_g, m_conv_ln_b, m_w_conv_out, m_w_attn_out, m_w_out, m_norm2_g, m_w_ff1, m_w_ff2, v_rel_bias, v_norm1_g, v_w_in, v_q_norm_g, v_k_norm_g, v_conv_dw_w, v_conv_dw_b, v_conv_ln_g, v_conv_ln_b, v_w_conv_out, v_w_attn_out, v_w_out, v_norm2_g, v_w_ff1, v_w_ff2):
    w = dict(rel_bias=rel_bias, norm1_g=norm1_g, w_in=w_in, q_norm_g=q_norm_g, k_norm_g=k_norm_g, conv_dw_w=conv_dw_w,
             conv_dw_b=conv_dw_b, conv_ln_g=conv_ln_g, conv_ln_b=conv_ln_b, w_conv_out=w_conv_out,
             w_attn_out=w_attn_out, w_out=w_out, norm2_g=norm2_g, w_ff1=w_ff1, w_ff2=w_ff2)
    mom = dict(rel_bias=m_rel_bias, norm1_g=m_norm1_g, w_in=m_w_in, q_norm_g=m_q_norm_g, k_norm_g=m_k_norm_g,
               conv_dw_w=m_conv_dw_w, conv_dw_b=m_conv_dw_b, conv_ln_g=m_conv_ln_g, conv_ln_b=m_conv_ln_b,
               w_conv_out=m_w_conv_out, w_attn_out=m_w_attn_out, w_out=m_w_out, norm2_g=m_norm2_g, w_ff1=m_w_ff1,
               w_ff2=m_w_ff2)
    var = dict(rel_bias=v_rel_bias, norm1_g=v_norm1_g, w_in=v_w_in, q_norm_g=v_q_norm_g, k_norm_g=v_k_norm_g,
               conv_dw_w=v_conv_dw_w, conv_dw_b=v_conv_dw_b, conv_ln_g=v_conv_ln_g, conv_ln_b=v_conv_ln_b,
               w_conv_out=v_w_conv_out, w_attn_out=v_w_attn_out, w_out=v_w_out, norm2_g=v_norm2_g, w_ff1=v_w_ff1,
               w_ff2=v_w_ff2)

    depth = norm1_g.shape[0]
    me = 4 * lax.axis_index("x") + 2 * lax.axis_index("y") + lax.axis_index("c")
    odd_core = lax.axis_index("c") == 1
    hpg = w_attn_out.shape[1] // HEAD_DIM
    buckets = jnp.asarray(_bucket_table())

    first_names = ("w_in", "conv_dw_w")
    rest_names = tuple(k for k in BIG if k not in first_names)

    def chain_start(l, names, after):
        shards = [w[k][l] if k == "conv_dw_w" else w[k][l].astype(BF16) for k in names]
        if "w_in" in names:
            i = names.index("w_in")
            shards[i] = jnp.where(odd_core, jnp.pad(shards[i], ((0, 0), (SHIFT, 0))),
                                  jnp.pad(shards[i], ((0, 0), (0, SHIFT))))
        sems, bufs, token = _gather_start(shards, [_own_slot(s, me) for s in shards], after,
                                          "gather_start_%s_l%d" % (names[0], l))
        return dict(l=l, names=names, sems=sems, bufs=bufs, token=token)

    def chain_forward(ch, after):
        fwd, bufs, token = _gather_forward(len(ch["names"]), ch["bufs"], ch["sems"][3], after,
                                           "gather_forward_%s_l%d" % (ch["names"][0], ch["l"]))
        ch.update(fwd=fwd, bufs=bufs)
        return token

    def chain_finish(ch, after):
        d2d_s, d2d_r, ici_s, _ = ch["sems"]
        gathered, _ = _gather_finish(len(ch["names"]), ch["bufs"], d2d_s, d2d_r, ici_s, ch["fwd"][0], ch["fwd"][1],
                                     after, "gather_finish_%s_l%d" % (ch["names"][0], ch["l"]))
        out = {k: a if k == "w_in" else _to_whole(k, a) for k, a in zip(ch["names"], gathered)}
        if "conv_dw_w" in out:
            out["conv_dw_w"] = jnp.pad(out["conv_dw_w"], ((0, CONV_TAPS_PADDED - CONV_WIDTH), (0, 0)))
        return out

    xs = x[0]
    first = chain_start(0, first_names, None)
    h1 = _rms_fwd(xs, norm1_g[0], first["token"], "rms1_fwd")
    bias = _bias_expand(rel_bias, buckets, hpg, h1, "bias_expand")
    saved, bigs, smalls = [], [], []
    chains = {}
    for l in range(depth):
        sm = {k: w[k][l] for k in SMALL if k != "rel_bias"}
        if l == 0:
            token = chain_forward(first, bias)
            rest = chain_start(0, rest_names, token)
            bg = chain_finish(first, rest["token"])

            def get_rest(o, rest=rest):
                token = chain_forward(rest, o)
                if depth > 1:
                    chains[1] = (chain_start(1, first_names, token),)
                    chains[1] += (chain_start(1, rest_names, chains[1][0]["token"]),)
                    token = chains[1][1]["token"]
                return chain_finish(rest, token)
        elif l == 1:
            first, rest = chains[1]
            token = chain_forward(first, xs)
            if depth > 2:
                chains[2] = chain_start(2, BIG, token)
                token = chains[2]["token"]
            bg = chain_finish(first, token)

            def get_rest(o, rest=rest):
                return chain_finish(rest, chain_forward(rest, o))
        else:
            token = xs if "fwd" in chains[l] else chain_forward(chains[l], xs)
            if l + 1 < depth:
                chains[l + 1] = chain_start(l + 1, BIG, token)
                token = chains[l + 1]["token"]
            whole = chain_finish(chains[l], token)
            bg = {k: whole[k] for k in first_names}

            def get_rest(o, whole=whole):
                return {k: whole[k] for k in rest_names}
        at_ff1 = (lambda f, nxt=l + 1: chain_forward(chains[nxt], f)) if 2 <= l < depth - 1 else None
        xs, h1, sv, bg = _layer_fwd(xs, h1, sm, bg, get_rest, bias, hpg, norm1_g[l + 1] if l + 1 < depth else None,
                                    at_ff1)
        saved.append(sv)
        bigs.append(bg)
        smalls.append(sm)

    loss, dx = _loss_and_grad(xs, loss_target[0], "loss")

    ds_sum = jnp.zeros((N_GROUPS * hpg, BLOCK, 2 * BLOCK), F32)
    g = {k: [None] * depth for k in SMALL if k != "rel_bias"}
    sums = {k: lax.empty((depth, int(np.prod(w[k].shape[1:-1])), w[k].shape[-1] + (SHIFT if k == "w_in" else 0)), F32)
            for k in BIG}
    pending = []

    def finish_oldest(after):
        names, l, (send, recv), bufs = pending.pop(0)
        parts, recvd, token = _exchange_finish(len(names), bufs, send, recv, after,
                                               "exchange_finish_%s_l%d" % (names[0], l))
        for k, r, p in zip(names, recvd, parts):
            three = (N_DEV, -1, r.shape[-1])
            sums[k] = _sum_slots(r.reshape(three), p.reshape(three), me, sums[k], l, "sum_" + k)
        return token

    def make_emit(l):
        def emit(names, gl):
            parts = [gl[k] if k in ("w_ff1", "w_in") else _to_slots(k, gl[k]) for k in names]
            token = finish_oldest(parts[0]) if len(pending) >= len(GRAD_GROUPS) else None
            lands = [lax.empty(p.shape, p.dtype) for p in parts]
            sems, bufs, token = _exchange_start(parts, lands, token, "exchange_start_%s_l%d" % (names[0], l))
            pending.append((names, l, sems, bufs))
            return token
        return emit

    token = None
    for l in reversed(range(depth)):
        dx, gl, ds_sum = _layer_bwd(dx, saved[l], smalls[l], bigs[l], bias, ds_sum, token, make_emit(l))
        for k in g:
            g[k][l] = gl[k]
        token = None
    grad_x = dx

    g = {k: jnp.stack(v) for k, v in g.items()}
    for k in ("q_norm_g", "k_norm_g"):
        g[k] = jnp.sum(g[k].reshape(depth, -1, HEAD_DIM), axis=1)
    db = _bias_reduce(ds_sum, buckets, hpg, "bias_reduce")
    g["rel_bias"] = db[:, 0, :NUM_BUCKETS].T

    flat = jnp.concatenate([g[k].reshape(-1) for k in SMALL])
    nflat = flat.shape[0]
    rows = -(-nflat // (8 * LANES)) * 8
    packed = jnp.pad(flat, (0, rows * LANES - nflat)).reshape(rows, LANES)
    grad, outs = {}, {}
    token = dx
    while pending:
        names = pending[0][0]
        finish_oldest(token)
        for k in names:
            total = sums[k]
            if k == "w_in":
                total = jnp.where(odd_core, total[..., SHIFT:], total[..., :w_in.shape[-1]])
            grad[k] = total.reshape(w[k].shape)
            outs[k] = _adamw(w[k], grad[k], mom[k], var[k], "adamw_" + k)
        token = tuple(outs[k][0] for k in names)
    total = _all_reduce_small(packed, token[-1], "reduce_small").reshape(-1)
    off = 0
    for k in SMALL:
        size = int(np.prod(w[k].shape))
        grad[k] = total[off:off + size].reshape(w[k].shape)
        outs[k] = _adamw(w[k], grad[k], mom[k], var[k], "adamw_" + k)
        off += size
    loss = lax.psum(loss[0, 0], ("x", "y", "c"))
    return (loss, grad_x[None], *[grad[k] for k in WEIGHTS], *[outs[k][0] for k in WEIGHTS],
            *[outs[k][1] for k in WEIGHTS], *[outs[k][2] for k in WEIGHTS])
```

```python
import functools
import math

import numpy as np
import jax
import jax.numpy as jnp
from jax import lax
from jax.experimental import pallas as pl
from jax.experimental.pallas import tpu as pltpu

F32 = jnp.float32
BF16 = jnp.bfloat16

HEAD_DIM = 64
N_GROUPS = 3
DILATIONS = (1, 4, 16)
SUB_WINDOW = 128
BLOCK = 128
CONV_WIDTH = 31
CONV_TAPS_PADDED = 32
NUM_BUCKETS = 32
MAX_REL_DISTANCE = 2048
EPS = 1e-6
NEG_INF = -1e30
LANES = 128
SUBLANES = 8

ADAM_LR = 0.001
ADAM_B1 = 0.9
ADAM_B2 = 0.999
ADAM_EPS = 1e-08
ADAM_WD = 0.01
ADAM_STEP = 10

N_DEV = 8
VMEM_LIMIT = 56 * 1024 * 1024
MESH = pl.DeviceIdType.MESH


def _cparams(sem=None):
    return pltpu.CompilerParams(dimension_semantics=sem, vmem_limit_bytes=VMEM_LIMIT)


def _tile(n, target):
    if n <= target:
        return n
    t = (target // LANES) * LANES
    while t >= LANES:
        if n % t == 0:
            return t
        t -= LANES
    return n


def _sigmoid(v):
    return 1.0 / (1.0 + jnp.exp(-v))


MM_VMEM_BUDGET = 40 * 1024 * 1024


def _rms_apply(x, g):
    return x * lax.rsqrt(jnp.mean(x * x, axis=-1, keepdims=True) + EPS) * g


def _rms_grad(dh, x, g):
    r = lax.rsqrt(jnp.mean(x * x, axis=-1, keepdims=True) + EPS)
    xh = x * r
    dxh = dh * g
    dx = r * (dxh - xh * jnp.mean(dxh * xh, axis=-1, keepdims=True))
    return dx, jnp.sum(dh * xh, axis=0, keepdims=True)


def _mm_tiles(m, n, kdim, a_bytes, b_bytes, io_bytes, whole_rows=False, temps=2):
    def need(tm, tn, tk):
        blocks = 2 * (tm * tk * a_bytes + tk * tn * b_bytes + tm * tn * io_bytes)
        casts = (tm * tk * 2 if a_bytes == 4 else 0) + (tk * tn * 2 if b_bytes == 4 else 0)
        return blocks + casts + temps * tm * tn * 4

    tn = n if whole_rows else _tile(n, 1024)
    while True:
        fits = [(tm * tk, tm, tk) for tm in {_tile(m, c) for c in (1024, 512, 256, 128)}
                for tk in {_tile(kdim, c) for c in (2048, 1024, 512, 256)} if need(tm, tn, tk) <= MM_VMEM_BUDGET]
        if fits:
            _, tm, tk = max(fits)
            return tm, tn, tk
        assert not whole_rows and tn % 256 == 0, "no block size fits the VMEM budget"
        tn //= 2


def _mm(a, b, *, ta=False, tb=False, out_dtype=F32, epi=None, extra=(), gain=None, after=None, out_slots=False,
        a_relu2=False, name):
    extra = tuple(extra) if isinstance(extra, (tuple, list)) else (extra,)
    m = a.shape[1] if ta else a.shape[0]
    kdim = a.shape[0] if ta else a.shape[1]
    n = b.shape[0] if tb else b.shape[1]
    norm = epi in ("res_rms", "rms_bwd")
    io_bytes = (jnp.dtype(out_dtype).itemsize + sum(e.dtype.itemsize for e in extra) + (2 if norm else 0))
    tm, tn, tk = _mm_tiles(m, n // N_DEV if out_slots else n, kdim, a.dtype.itemsize, b.dtype.itemsize, io_bytes,
                           whole_rows=norm, temps=6 if norm else 2)
    if out_slots:
        assert epi is None and tn == n // N_DEV
    nk = kdim // tk
    a_spec = pl.BlockSpec((tk, tm), lambda i, j, k: (k, i)) if ta else pl.BlockSpec((tm, tk), lambda i, j, k: (i, k))
    b_spec = pl.BlockSpec((tn, tk), lambda i, j, k: (j, k)) if tb else pl.BlockSpec((tk, tn), lambda i, j, k: (k, j))
    o_spec = (pl.BlockSpec((None, tm, tn), lambda i, j, k: (j, i, 0)) if out_slots
              else pl.BlockSpec((tm, tn), lambda i, j, k: (i, j)))
    v_spec = pl.BlockSpec((1, tn), lambda i, j, k: (0, j))
    dims = (((0 if ta else 1,), (1 if tb else 0,)), ((), ()))
    n_extra = len(extra)
    n_in = 2 + n_extra + (gain is not None) + (after is not None)
    n_out = 3 if epi == "rms_bwd" else 2 if norm else 1

    def body(*refs):
        a_ref, b_ref = refs[0], refs[1]
        e_refs = refs[2:2 + n_extra]
        g_ref = refs[2 + n_extra] if gain is not None else None
        outs = refs[n_in:n_in + n_out]

        def product():
            av = a_ref[...]
            if a_relu2:
                r = jnp.maximum(av.astype(F32), 0.0)
                av = r * r
            return lax.dot_general(av.astype(BF16), b_ref[...].astype(BF16), dims, preferred_element_type=F32)

        def finish(acc):
            if epi is None:
                outs[0][...] = acc.astype(outs[0].dtype)
            elif epi == "res":
                outs[0][...] = (e_refs[0][...] + acc).astype(outs[0].dtype)
            elif epi == "drelu2":
                outs[0][...] = (acc * (2.0 * jnp.maximum(e_refs[0][...].astype(F32), 0.0))).astype(outs[0].dtype)
            elif epi == "res_rms":
                x1 = e_refs[0][...] + acc
                outs[0][...] = x1
                outs[1][...] = _rms_apply(x1, g_ref[...]).astype(BF16)
            elif epi == "rms_bwd":
                dx, dg = _rms_grad(acc, e_refs[0][...], g_ref[...])
                dx = e_refs[1][...] + dx
                outs[0][...] = dx
                outs[2][...] = dx.astype(BF16)
                i = pl.program_id(0)

                @pl.when(i == 0)
                def _():
                    outs[1][...] = dg

                @pl.when(i > 0)
                def _():
                    outs[1][...] += dg

        if nk == 1:
            finish(product())
            return
        acc_ref = refs[-1]
        k = pl.program_id(2)

        @pl.when(k == 0)
        def _():
            acc_ref[...] = product()

        @pl.when(jnp.logical_and(k > 0, k < nk - 1))
        def _():
            acc_ref[...] += product()

        @pl.when(k == nk - 1)
        def _():
            finish(acc_ref[...] + product())

    in_specs = ([a_spec, b_spec] + [o_spec] * n_extra + ([v_spec] if gain is not None else [])
                + ([pl.BlockSpec(memory_space=pl.ANY)] if after is not None else []))
    if epi == "res_rms":
        out_shape = (jax.ShapeDtypeStruct((m, n), F32), jax.ShapeDtypeStruct((m, n), BF16))
        out_specs = (o_spec, o_spec)
    elif epi == "rms_bwd":
        out_shape = (jax.ShapeDtypeStruct((m, n), F32), jax.ShapeDtypeStruct((1, n), F32),
                     jax.ShapeDtypeStruct((m, n), BF16))
        out_specs = (o_spec, v_spec, o_spec)
    else:
        out_shape = jax.ShapeDtypeStruct((N_DEV, m, tn) if out_slots else (m, n), out_dtype)
        out_specs = o_spec
    args = (a, b) + extra + ((gain.reshape(1, n),) if gain is not None else ()) + ((after,) if after is not None else ())
    return pl.pallas_call(
        body, name=name, grid=(m // tm, n // tn, nk), in_specs=in_specs, out_specs=out_specs, out_shape=out_shape,
        scratch_shapes=[pltpu.VMEM((tm, tn), F32)] if nk > 1 else [],
        compiler_params=_cparams(("arbitrary", "arbitrary", "arbitrary")),
    )(*args)


SHIFT = HEAD_DIM


def _pair_blocks(e, o):
    wp = e.shape[-1]
    return e[:, :wp - LANES], e[:, wp - LANES:] + o[:, :LANES], o[:, LANES:]


def _mm_in_pairs(a, wg, name):
    t, kdim = a.shape
    wp = wg.shape[-1]
    ws = wp - SHIFT
    tm = _tile(t, 1024)

    def body(a_ref, e_ref, o_ref, u_ref):
        av = a_ref[...]
        lo, mid, hi = _pair_blocks(e_ref[...], o_ref[...])
        u_ref[:, :wp - LANES] = jnp.dot(av, lo, preferred_element_type=F32).astype(BF16)
        u_ref[:, wp - LANES:wp] = jnp.dot(av, mid, preferred_element_type=F32).astype(BF16)
        u_ref[:, wp:] = jnp.dot(av, hi, preferred_element_type=F32).astype(BF16)

    return pl.pallas_call(
        body, name=name, grid=(N_DEV // 2, t // tm),
        in_specs=[pl.BlockSpec((tm, kdim), lambda p, i: (i, 0)),
                  pl.BlockSpec((None, kdim, wp), lambda p, i: (2 * p, 0, 0)),
                  pl.BlockSpec((None, kdim, wp), lambda p, i: (2 * p + 1, 0, 0))],
        out_specs=pl.BlockSpec((tm, 2 * ws), lambda p, i: (i, p)),
        out_shape=jax.ShapeDtypeStruct((t, N_DEV * ws), BF16), compiler_params=_cparams(("arbitrary", "arbitrary")),
    )(a, wg, wg)


def _mm_din_pairs(du, wg, x, gain, dres, after, name):
    t = du.shape[0]
    _, kdim, wp = wg.shape
    ws = wp - SHIFT
    tm = _tile(t, 512)
    npair = N_DEV // 2
    lanes = (((1,), (1,)), ((), ()))
    extra = [] if after is None else [after]

    def body(d_ref, e_ref, o_ref, x_ref, g_ref, r_ref, *rest):
        dx_ref, dg_ref, acc_ref = rest[-3], rest[-2], rest[-1]
        i, p = pl.program_id(0), pl.program_id(1)
        lo, mid, hi = _pair_blocks(e_ref[...], o_ref[...])
        part = (lax.dot_general(d_ref[:, :wp - LANES], lo, lanes, preferred_element_type=F32)
                + lax.dot_general(d_ref[:, wp - LANES:wp], mid, lanes, preferred_element_type=F32)
                + lax.dot_general(d_ref[:, wp:], hi, lanes, preferred_element_type=F32))

        @pl.when(p == 0)
        def _():
            acc_ref[...] = part

        @pl.when(jnp.logical_and(p > 0, p < npair - 1))
        def _():
            acc_ref[...] += part

        @pl.when(p == npair - 1)
        def _():
            dx, dg = _rms_grad(acc_ref[...] + part, x_ref[...], g_ref[...])
            dx_ref[...] = r_ref[...] + dx

            @pl.when(i == 0)
            def _():
                dg_ref[...] = dg

            @pl.when(i > 0)
            def _():
                dg_ref[...] += dg

    row = pl.BlockSpec((tm, kdim), lambda i, p: (i, 0))
    vec = pl.BlockSpec((1, kdim), lambda i, p: (0, 0))
    return pl.pallas_call(
        body, name=name, grid=(t // tm, npair),
        in_specs=[pl.BlockSpec((tm, 2 * ws), lambda i, p: (i, p)),
                  pl.BlockSpec((None, kdim, wp), lambda i, p: (2 * p, 0, 0)),
                  pl.BlockSpec((None, kdim, wp), lambda i, p: (2 * p + 1, 0, 0)), row, vec, row]
        + [pl.BlockSpec(memory_space=pl.ANY)] * len(extra),
        out_specs=(row, vec),
        out_shape=(jax.ShapeDtypeStruct((t, kdim), F32), jax.ShapeDtypeStruct((1, kdim), F32)),
        scratch_shapes=[pltpu.VMEM((tm, kdim), F32)], compiler_params=_cparams(("arbitrary", "arbitrary")),
    )(du, wg, wg, x, gain.reshape(1, kdim), dres, *extra)


def _mm_gw_in_pairs(h, du, after, name):
    t, kdim = h.shape
    ws = du.shape[1] // N_DEV
    wp = ws + SHIFT
    tm = _tile(kdim, 512)
    rows = (((0,), (0,)), ((), ()))
    extra = [] if after is None else [after]

    def body(h_ref, d_ref, *rest):
        g_ref = rest[-1]
        g = lax.dot_general(h_ref[...], d_ref[...], rows, preferred_element_type=F32)
        g_ref[0] = g[:, :wp].astype(BF16)
        g_ref[1] = g[:, wp - LANES:].astype(BF16)

    return pl.pallas_call(
        body, name=name, grid=(N_DEV // 2, kdim // tm),
        in_specs=[pl.BlockSpec((t, tm), lambda p, i: (0, i)), pl.BlockSpec((t, 2 * ws), lambda p, i: (0, p))]
        + [pl.BlockSpec(memory_space=pl.ANY)] * len(extra),
        out_specs=pl.BlockSpec((2, tm, wp), lambda p, i: (p, i, 0)),
        out_shape=jax.ShapeDtypeStruct((N_DEV, kdim, wp), BF16), compiler_params=_cparams(("arbitrary", "arbitrary")),
    )(h, du, *extra)


ROW_BLOCK_BUDGET = 24 * 1024 * 1024


def _rows(t, row_bytes):
    rows = t
    while rows > 8 and (2 * rows * row_bytes > ROW_BLOCK_BUDGET or t % rows):
        rows //= 2
    return rows


def _rms_fwd(x, g, after, name):
    t, d = x.shape
    ROWS = _rows(t, 6 * d)

    def body(x_ref, g_ref, _, h_ref):
        h_ref[...] = _rms_apply(x_ref[...], g_ref[...]).astype(BF16)

    return pl.pallas_call(
        body, name=name, grid=(t // ROWS,),
        in_specs=[pl.BlockSpec((ROWS, d), lambda i: (i, 0)), pl.BlockSpec((1, d), lambda i: (0, 0)),
                  pl.BlockSpec(memory_space=pl.ANY)],
        out_specs=pl.BlockSpec((ROWS, d), lambda i: (i, 0)),
        out_shape=jax.ShapeDtypeStruct((t, d), BF16), compiler_params=_cparams(("arbitrary",)),
    )(x, g.reshape(1, d), after)


def _gated_out(zs, o, wc, wa, u, gate_col, name):
    t = zs.shape[0]
    d = wc.shape[1]
    td = math.gcd(_tile(d, 512), gate_col)
    nd = d // td
    c0 = gate_col // td
    tm = _tile(t, 1024)

    def body(zs_ref, o_ref, wc_ref, wa_ref, gc_ref, ga_ref, m_ref, yc_ref, ya_ref):
        yc = jnp.dot(zs_ref[...], wc_ref[...], preferred_element_type=F32)
        ya = jnp.dot(o_ref[...], wa_ref[...], preferred_element_type=F32)
        gc = _sigmoid(gc_ref[...].astype(F32))
        ga = _sigmoid(ga_ref[...].astype(F32))
        m_ref[...] = (gc * yc + ga * ya).astype(BF16)
        yc_ref[...] = yc.astype(BF16)
        ya_ref[...] = ya.astype(BF16)

    blk = pl.BlockSpec((tm, td), lambda i, j: (i, j))
    sh = jax.ShapeDtypeStruct((t, d), BF16)
    return pl.pallas_call(
        body, name=name, grid=(t // tm, nd),
        in_specs=[pl.BlockSpec((tm, zs.shape[1]), lambda i, j: (i, 0)), pl.BlockSpec((tm, o.shape[1]), lambda i, j: (i, 0)),
                  pl.BlockSpec((wc.shape[0], td), lambda i, j: (0, j)), pl.BlockSpec((wa.shape[0], td), lambda i, j: (0, j)),
                  pl.BlockSpec((tm, td), lambda i, j: (i, c0 + j)), pl.BlockSpec((tm, td), lambda i, j: (i, c0 + nd + j))],
        out_specs=(blk, blk, blk), out_shape=(sh, sh, sh), compiler_params=_cparams(("arbitrary", "arbitrary")),
    )(zs, o, wc, wa, u, u)


def _gate_bwd(dm, u, yc, ya, gate_col, name):
    t, d = yc.shape
    td = math.gcd(_tile(d, 512), gate_col)
    nd = d // td
    c0 = gate_col // td
    ROWS = _rows(t, 20 * td)

    def body(dm_ref, gc_ref, ga_ref, yc_ref, ya_ref, dyc_ref, dya_ref, dugc_ref, duga_ref):
        dmv = dm_ref[...]
        gc = _sigmoid(gc_ref[...].astype(F32))
        ga = _sigmoid(ga_ref[...].astype(F32))
        dyc_ref[...] = (dmv * gc).astype(BF16)
        dya_ref[...] = (dmv * ga).astype(BF16)
        dugc_ref[...] = (dmv * yc_ref[...].astype(F32) * gc * (1.0 - gc)).astype(BF16)
        duga_ref[...] = (dmv * ya_ref[...].astype(F32) * ga * (1.0 - ga)).astype(BF16)

    blk = pl.BlockSpec((ROWS, td), lambda i, j: (i, j))
    o = jax.ShapeDtypeStruct((t, d), BF16)
    return pl.pallas_call(
        body, name=name, grid=(t // ROWS, nd),
        in_specs=[blk, pl.BlockSpec((ROWS, td), lambda i, j: (i, c0 + j)),
                  pl.BlockSpec((ROWS, td), lambda i, j: (i, c0 + nd + j)), blk, blk],
        out_specs=(blk, blk, blk, blk), out_shape=(o, o, o, o),
        compiler_params=_cparams(("arbitrary", "arbitrary")),
    )(dm, u, u, yc, ya)


def _loss_and_grad(y, target, name):
    t, d = y.shape
    ROWS = _rows(t, 12 * d)
    n = t // ROWS

    def body(y_ref, t_ref, loss_ref, dy_ref, acc_ref):
        i = pl.program_id(0)

        @pl.when(i == 0)
        def _():
            acc_ref[...] = jnp.zeros_like(acc_ref)

        diff = y_ref[...] - t_ref[...]
        dy_ref[...] = diff * (1.0 / d)
        acc_ref[...] += jnp.sum(diff * diff, axis=0, keepdims=True)

        @pl.when(i == n - 1)
        def _():
            loss_ref[...] = jnp.sum(acc_ref[...], axis=-1, keepdims=True) * (0.5 / d)

    row = pl.BlockSpec((ROWS, d), lambda i: (i, 0))
    return pl.pallas_call(
        body, name=name, grid=(n,), in_specs=[row, row],
        out_specs=(pl.BlockSpec((1, 1), lambda i: (0, 0)), row),
        out_shape=(jax.ShapeDtypeStruct((1, 1), F32), jax.ShapeDtypeStruct((t, d), F32)),
        scratch_shapes=[pltpu.VMEM((1, d), F32)], compiler_params=_cparams(("arbitrary",)),
    )(y, target)


HALO = 32


def _conv_fwd(u, w, b, cdim, name):
    t = u.shape[0]
    ncb = cdim // LANES
    nt = t // BLOCK

    def body(a_ref, g_ref, w_ref, b_ref, zc_ref, zpad):
        zpad[0:HALO, :] = jnp.zeros((HALO, LANES), F32)
        zpad[HALO:HALO + t, :] = a_ref[...].astype(F32) * _sigmoid(g_ref[...].astype(F32))
        wv = w_ref[...]
        bv = b_ref[...]

        def tile(i, carry):
            r0 = pl.multiple_of(i * BLOCK, BLOCK)
            win = zpad[pl.ds(r0, BLOCK + HALO), :]
            acc = jnp.zeros((BLOCK, LANES), F32) + bv
            for b in range(SUBLANES):
                sh = win if b == 0 else pltpu.roll(win, b, 0)
                for a in range(HALO // SUBLANES):
                    j = CONV_WIDTH - 1 - (SUBLANES * a + b)
                    if j >= 0:
                        lo = HALO - SUBLANES * a
                        acc = acc + wv[j:j + 1, :] * sh[lo:lo + BLOCK, :]
            zc_ref[pl.ds(r0, BLOCK), :] = acc
            return carry

        lax.fori_loop(0, nt, tile, 0)

    col = lambda off: pl.BlockSpec((t, LANES), lambda c: (0, off + c))
    return pl.pallas_call(
        body, name=name, grid=(ncb,),
        in_specs=[col(0), col(ncb), pl.BlockSpec((CONV_TAPS_PADDED, LANES), lambda c: (0, c)),
                  pl.BlockSpec((1, LANES), lambda c: (0, c))],
        out_specs=pl.BlockSpec((t, LANES), lambda c: (0, c)),
        out_shape=jax.ShapeDtypeStruct((t, cdim), F32),
        scratch_shapes=[pltpu.VMEM((t + HALO, LANES), F32)], compiler_params=_cparams(("arbitrary",)),
    )(u, u, w, b.reshape(1, cdim))


def _conv_bwd(dzc, u, w, cdim, name):
    t = u.shape[0]
    ncb = cdim // LANES
    nt = t // BLOCK
    win_rows = BLOCK + HALO

    def body(dzc_ref, a_ref, g_ref, w_ref, da_ref, dg_ref, dw_ref, db_ref, zpad, dpad):
        av = a_ref[...].astype(F32)
        sg = _sigmoid(g_ref[...].astype(F32))
        zpad[0:HALO, :] = jnp.zeros((HALO, LANES), F32)
        zpad[HALO:HALO + t, :] = av * sg
        dpad[0:t, :] = dzc_ref[...]
        dpad[t:t + HALO, :] = jnp.zeros((HALO, LANES), F32)
        dw_ref[...] = jnp.zeros_like(dw_ref)
        db_ref[...] = jnp.sum(dzc_ref[...], axis=0, keepdims=True)
        wv = w_ref[...]

        def tile(i, carry):
            r0 = pl.multiple_of(i * BLOCK, BLOCK)
            zwin = zpad[pl.ds(r0, win_rows), :]
            dwin = dpad[pl.ds(r0, win_rows), :]
            dcur = dwin[0:BLOCK, :]
            dz = jnp.zeros((BLOCK, LANES), F32)
            for b in range(SUBLANES):
                zs = zwin if b == 0 else pltpu.roll(zwin, b, 0)
                ds = dwin if b == 0 else pltpu.roll(dwin, win_rows - b, 0)
                for a in range(HALO // SUBLANES):
                    j = CONV_WIDTH - 1 - (SUBLANES * a + b)
                    if j >= 0:
                        lo = HALO - SUBLANES * a
                        dw_ref[j:j + 1, :] += jnp.sum(dcur * zs[lo:lo + BLOCK, :], axis=0, keepdims=True)
                        dz = dz + wv[j:j + 1, :] * ds[SUBLANES * a:SUBLANES * a + BLOCK, :]
            ac = a_ref[pl.ds(r0, BLOCK), :].astype(F32)
            sc = _sigmoid(g_ref[pl.ds(r0, BLOCK), :].astype(F32))
            da_ref[pl.ds(r0, BLOCK), :] = (dz * sc).astype(BF16)
            dg_ref[pl.ds(r0, BLOCK), :] = (dz * ac * sc * (1.0 - sc)).astype(BF16)
            return carry

        lax.fori_loop(0, nt, tile, 0)

    col = lambda off: pl.BlockSpec((t, LANES), lambda c: (0, off + c))
    wspec = pl.BlockSpec((CONV_TAPS_PADDED, LANES), lambda c: (0, c))
    o = jax.ShapeDtypeStruct((t, cdim), BF16)
    return pl.pallas_call(
        body, name=name, grid=(ncb,), in_specs=[col(0), col(0), col(ncb), wspec],
        out_specs=(col(0), col(0), wspec, pl.BlockSpec((1, LANES), lambda c: (0, c))),
        out_shape=(o, o, jax.ShapeDtypeStruct((CONV_TAPS_PADDED, cdim), F32), jax.ShapeDtypeStruct((1, cdim), F32)),
        scratch_shapes=[pltpu.VMEM((t + HALO, LANES), F32), pltpu.VMEM((t + HALO, LANES), F32)],
        compiler_params=_cparams(("arbitrary",)),
    )(dzc, u, u, w)


def _ln_swish_fwd(zc, g, b, name):
    t, c = zc.shape
    ROWS = _rows(t, 6 * c)

    def body(z_ref, g_ref, b_ref, o_ref):
        z = z_ref[...]
        mu = jnp.mean(z, axis=-1, keepdims=True)
        zc_ = z - mu
        zn = zc_ * lax.rsqrt(jnp.mean(zc_ * zc_, axis=-1, keepdims=True) + EPS)
        y = zn * g_ref[...] + b_ref[...]
        o_ref[...] = (y * _sigmoid(y)).astype(BF16)

    row = pl.BlockSpec((ROWS, c), lambda i: (i, 0))
    vec = pl.BlockSpec((1, c), lambda i: (0, 0))
    return pl.pallas_call(
        body, name=name, grid=(t // ROWS,), in_specs=[row, vec, vec], out_specs=row,
        out_shape=jax.ShapeDtypeStruct((t, c), BF16), compiler_params=_cparams(("arbitrary",)),
    )(zc, g.reshape(1, c), b.reshape(1, c))


def _ln_swish_bwd(dzs, zc, g, b, name):
    t, c = zc.shape
    ROWS = _rows(t, 12 * c)

    def body(d_ref, z_ref, g_ref, b_ref, dz_ref, dg_ref, db_ref):
        @pl.when(pl.program_id(0) == 0)
        def _():
            dg_ref[...] = jnp.zeros_like(dg_ref)
            db_ref[...] = jnp.zeros_like(db_ref)

        z = z_ref[...]
        mu = jnp.mean(z, axis=-1, keepdims=True)
        zc_ = z - mu
        rstd = lax.rsqrt(jnp.mean(zc_ * zc_, axis=-1, keepdims=True) + EPS)
        zn = zc_ * rstd
        y = zn * g_ref[...] + b_ref[...]
        sg = _sigmoid(y)
        dy = d_ref[...] * (sg * (1.0 + y * (1.0 - sg)))
        dg_ref[...] += jnp.sum(dy * zn, axis=0, keepdims=True)
        db_ref[...] += jnp.sum(dy, axis=0, keepdims=True)
        dzn = dy * g_ref[...]
        dz_ref[...] = rstd * (dzn - jnp.mean(dzn, axis=-1, keepdims=True)
                              - zn * jnp.mean(dzn * zn, axis=-1, keepdims=True))

    row = pl.BlockSpec((ROWS, c), lambda i: (i, 0))
    vec = pl.BlockSpec((1, c), lambda i: (0, 0))
    v = jax.ShapeDtypeStruct((1, c), F32)
    return pl.pallas_call(
        body, name=name, grid=(t // ROWS,), in_specs=[row, row, vec, vec], out_specs=(row, vec, vec),
        out_shape=(jax.ShapeDtypeStruct((t, c), F32), v, v), compiler_params=_cparams(("arbitrary",)),
    )(dzs, zc, g.reshape(1, c), b.reshape(1, c))


def _bucket_table():
    qi = np.arange(BLOCK)[:, None]
    kj = np.arange(2 * BLOCK)[None, :]
    off = qi + BLOCK - kj
    band = (off >= 0) & (off <= SUB_WINDOW)
    max_exact = NUM_BUCKETS // 2
    out = []
    for d in DILATIONS:
        dist = (np.clip(off, 0, SUB_WINDOW) * d).astype(np.int32)
        nf = np.maximum(dist, 1).astype(np.float32)
        large = max_exact + (np.log(nf / np.float32(max_exact)) / np.float32(math.log(MAX_REL_DISTANCE / max_exact))
                             * np.float32(NUM_BUCKETS - max_exact)).astype(np.int32)
        large = np.minimum(large, NUM_BUCKETS - 1)
        bucket = np.where(dist < max_exact, dist, large)
        out.append(np.where(band, bucket, -1))
    return np.stack(out).astype(np.int32)


def _bias_expand(rel_bias, buckets, hpg, after, name):
    nh = N_GROUPS * hpg

    def body(rb_ref, bk_ref, _, o_ref):
        h = pl.program_id(0)
        bk = bk_ref[0]
        acc = jnp.full((BLOCK, 2 * BLOCK), NEG_INF, F32)
        for bb in range(NUM_BUCKETS):
            acc = jnp.where(bk == bb, rb_ref[bb, h], acc)
        o_ref[0] = acc

    return pl.pallas_call(
        body, name=name, grid=(nh,),
        in_specs=[pl.BlockSpec(memory_space=pltpu.SMEM),
                  pl.BlockSpec((1, BLOCK, 2 * BLOCK), lambda h: (h // hpg, 0, 0)), pl.BlockSpec(memory_space=pl.ANY)],
        out_specs=pl.BlockSpec((1, BLOCK, 2 * BLOCK), lambda h: (h, 0, 0)),
        out_shape=jax.ShapeDtypeStruct((nh, BLOCK, 2 * BLOCK), F32), compiler_params=_cparams(("arbitrary",)),
    )(rel_bias, buckets, after)


def _bias_reduce(ds_sum, buckets, hpg, name):
    nh = N_GROUPS * hpg

    def body(ds_ref, bk_ref, o_ref):
        bk = bk_ref[0]
        dsv = ds_ref[0]
        lane = lax.broadcasted_iota(jnp.int32, (1, LANES), 1)
        row = jnp.zeros((1, LANES), F32)
        for bb in range(NUM_BUCKETS):
            tot = jnp.sum(jnp.sum(jnp.where(bk == bb, dsv, 0.0), axis=-1, keepdims=True), axis=0, keepdims=True)
            row = jnp.where(lane == bb, tot, row)
        o_ref[0] = row

    return pl.pallas_call(
        body, name=name, grid=(nh,),
        in_specs=[pl.BlockSpec((1, BLOCK, 2 * BLOCK), lambda h: (h, 0, 0)),
                  pl.BlockSpec((1, BLOCK, 2 * BLOCK), lambda h: (h // hpg, 0, 0))],
        out_specs=pl.BlockSpec((1, 1, LANES), lambda h: (h, 0, 0)),
        out_shape=jax.ShapeDtypeStruct((nh, 1, LANES), F32), compiler_params=_cparams(("arbitrary",)),
    )(ds_sum, buckets)


def _chunk_rows(c, d, nb):
    r, n = c // nb, c % nb
    if d == 1:
        return pl.ds(c * BLOCK, BLOCK)
    return pl.ds(r + n * BLOCK * d, BLOCK, stride=d)


def _segment_ones():
    i = lax.broadcasted_iota(jnp.int32, (LANES, LANES), 0) // HEAD_DIM
    j = lax.broadcasted_iota(jnp.int32, (LANES, LANES), 1) // HEAD_DIM
    return (i == j).astype(BF16)


def _segment_sum(v, seg):
    hi = v.astype(BF16)
    lo = (v - hi.astype(F32)).astype(BF16)
    return jnp.dot(hi, seg, preferred_element_type=F32) + jnp.dot(lo, seg, preferred_element_type=F32)


def _head_mean(v, seg):
    return _segment_sum(v, seg) * (1.0 / HEAD_DIM)


def _attn_fwd(u, qg2, kg2, bias, gi, cols, hpg, name):
    t = u.shape[0]
    d = DILATIONS[gi]
    nchunk = t // BLOCK
    nb = (t // d) // BLOCK
    hp = hpg // 2
    qc0, kc0, vc0 = [(c + gi * hpg * HEAD_DIM) // LANES for c in cols]
    contract_lanes = (((1,), (1,)), ((), ()))

    def body(q_ref, k_ref, v_ref, qg_ref, kg_ref, bias_ref, o_ref, lse_ref, qd, kd, vd, od, ld, sbuf, nat):
        seg = _segment_ones()
        lane = lax.broadcasted_iota(jnp.int32, (1, LANES), 1)
        qg = qg_ref[...] * (HEAD_DIM ** -0.5)
        kg = kg_ref[...]
        kd[0:BLOCK, :] = jnp.zeros((BLOCK, LANES), BF16)
        vd[0:BLOCK, :] = jnp.zeros((BLOCK, LANES), BF16)
        od[...] = q_ref[...].astype(F32)
        ld[...] = k_ref[...].astype(F32)
        for c in range(nchunk):
            rows = _chunk_rows(c, d, nb)
            qv = od[rows, :]
            kv = ld[rows, :]
            qd[c * BLOCK:(c + 1) * BLOCK, :] = (qv * lax.rsqrt(_head_mean(qv * qv, seg) + EPS) * qg).astype(BF16)
            kd[(c + 1) * BLOCK:(c + 2) * BLOCK, :] = (kv * lax.rsqrt(_head_mean(kv * kv, seg) + EPS) * kg).astype(BF16)
        od[...] = v_ref[...].astype(F32)
        for c in range(nchunk):
            vd[(c + 1) * BLOCK:(c + 2) * BLOCK, :] = od[_chunk_rows(c, d, nb), :].astype(BF16)

        col = lax.broadcasted_iota(jnp.int32, (BLOCK, 2 * BLOCK), 1)
        for j in range(2):
            mj = jnp.logical_and(lane >= j * HEAD_DIM, lane < (j + 1) * HEAD_DIM)
            for c in range(nchunk):
                kw = kd[c * BLOCK:(c + 2) * BLOCK, :]
                kj = jnp.where(mj, kw, jnp.zeros_like(kw))
                s = lax.dot_general(qd[c * BLOCK:(c + 1) * BLOCK, :], kj, contract_lanes,
                                    preferred_element_type=F32) + bias_ref[j]
                if c % nb == 0:
                    s = jnp.where(col < BLOCK, NEG_INF, s)
                sbuf[c] = s
            for c in range(nchunk):
                rows = slice(c * BLOCK, (c + 1) * BLOCK)
                s = sbuf[c]
                mx = jnp.max(s, axis=-1, keepdims=True)
                p = jnp.exp(s - mx).astype(BF16)
                vw = vd[c * BLOCK:(c + 2) * BLOCK, :]
                oj = jnp.dot(p, jnp.where(mj, vw, jnp.ones_like(vw)), preferred_element_type=F32)
                l = pltpu.roll(oj, HEAD_DIM, 1)
                on = oj / l
                ls = mx + jnp.log(l)
                if j == 0:
                    od[rows, :] = on
                    ld[rows, :] = ls
                else:
                    od[rows, :] = jnp.where(mj, on, od[rows, :])
                    ld[rows, :] = jnp.where(mj, ls, ld[rows, :])

        for c in range(nchunk):
            rows = _chunk_rows(c, d, nb)
            nat[rows, :] = od[c * BLOCK:(c + 1) * BLOCK, :]
            lse_ref[rows, :] = ld[c * BLOCK:(c + 1) * BLOCK, :]
        o_ref[...] = nat[...].astype(BF16)

    ucol = lambda c0: pl.BlockSpec((t, LANES), lambda h: (0, c0 + h))
    vec = pl.BlockSpec((1, LANES), lambda h: (0, 0))
    oblk = pl.BlockSpec((t, LANES), lambda h: (0, h))
    osh = jax.ShapeDtypeStruct((t, hpg * HEAD_DIM), F32)
    osh16 = jax.ShapeDtypeStruct((t, hpg * HEAD_DIM), BF16)
    return pl.pallas_call(
        body, name=name, grid=(hp,),
        in_specs=[ucol(qc0), ucol(kc0), ucol(vc0), vec, vec,
                  pl.BlockSpec((2, BLOCK, 2 * BLOCK), lambda h: (gi * hp + h, 0, 0))],
        out_specs=(oblk, oblk), out_shape=(osh16, osh),
        scratch_shapes=[pltpu.VMEM((t, LANES), BF16), pltpu.VMEM((t + BLOCK, LANES), BF16),
                        pltpu.VMEM((t + BLOCK, LANES), BF16), pltpu.VMEM((t, LANES), F32), pltpu.VMEM((t, LANES), F32),
                        pltpu.VMEM((nchunk, BLOCK, 2 * BLOCK), F32), pltpu.VMEM((t, LANES), F32)],
        compiler_params=_cparams(("arbitrary",)),
    )(u, u, u, qg2, kg2, bias)


def _attn_bwd(u, do_g, dd_g, lse_g, qg2, kg2, bias, ds_in, du_in, gi, cols, hpg, name):
    t = u.shape[0]
    d = DILATIONS[gi]
    nchunk = t // BLOCK
    nb = (t // d) // BLOCK
    hp = hpg // 2
    qc0, kc0, vc0 = [(c + gi * hpg * HEAD_DIM) // LANES for c in cols]
    contract_lanes = (((1,), (1,)), ((), ()))
    contract_rows = (((0,), (0,)), ((), ()))
    qscale = HEAD_DIM ** -0.5

    def body(q_ref, k_ref, v_ref, do_ref, dd_ref, lse_ref, qg_ref, kg_ref, bias_ref, dsin_ref, _du_in,
             dgq_ref, dgk_ref, dsout_ref, du_ref,
             qd, kd, vd, dod, ddd, ld, dqd, dkd, dvd, dsacc, pbuf, dsbuf, qs, ks, qst, kst, vst, out_sems):
        h = pl.program_id(0)

        def flush(step):
            return [pltpu.make_async_copy(
                st, du_ref.at[:, pl.ds(pl.multiple_of((c0 + step) * LANES, LANES), LANES)], out_sems.at[i])
                for i, (st, c0) in enumerate(((qst, qc0), (kst, kc0), (vst, vc0)))]

        seg = _segment_ones()
        lane = lax.broadcasted_iota(jnp.int32, (1, LANES), 1)
        qg = qg_ref[...] * qscale
        kg = kg_ref[...]
        kd[0:BLOCK, :] = jnp.zeros((BLOCK, LANES), BF16)
        vd[0:BLOCK, :] = jnp.zeros((BLOCK, LANES), BF16)
        dsacc[...] = jnp.zeros_like(dsacc)
        qs[...] = q_ref[...].astype(F32)
        ks[...] = k_ref[...].astype(F32)
        dqd[...] = v_ref[...].astype(F32)
        dkd[...] = do_ref[...].astype(F32)
        for c in range(nchunk):
            rows = _chunk_rows(c, d, nb)
            qv = qs[rows, :]
            kv = ks[rows, :]
            qd[c * BLOCK:(c + 1) * BLOCK, :] = (qv * lax.rsqrt(_head_mean(qv * qv, seg) + EPS) * qg).astype(BF16)
            kd[(c + 1) * BLOCK:(c + 2) * BLOCK, :] = (kv * lax.rsqrt(_head_mean(kv * kv, seg) + EPS) * kg).astype(BF16)
            vd[(c + 1) * BLOCK:(c + 2) * BLOCK, :] = dqd[rows, :].astype(BF16)
            dod[c * BLOCK:(c + 1) * BLOCK, :] = dkd[rows, :].astype(BF16)
            ddd[c * BLOCK:(c + 1) * BLOCK, :] = dd_ref[rows, :]
            ld[c * BLOCK:(c + 1) * BLOCK, :] = lse_ref[rows, :]

        col = lax.broadcasted_iota(jnp.int32, (BLOCK, 2 * BLOCK), 1)
        for j in range(2):
            mj = jnp.logical_and(lane >= j * HEAD_DIM, lane < (j + 1) * HEAD_DIM)
            first = lane == j * HEAD_DIM
            for c in range(nchunk):
                rows = slice(c * BLOCK, (c + 1) * BLOCK)
                kw = kd[c * BLOCK:(c + 2) * BLOCK, :]
                vw = vd[c * BLOCK:(c + 2) * BLOCK, :]
                kj = jnp.where(mj, kw, jnp.zeros_like(kw))
                vj = jnp.where(mj, vw, jnp.zeros_like(vw))
                s = lax.dot_general(qd[rows, :], kj, contract_lanes, preferred_element_type=F32) + bias_ref[j]
                if c % nb == 0:
                    s = jnp.where(col < BLOCK, NEG_INF, s)
                dp = lax.dot_general(dod[rows, :], vj, contract_lanes, preferred_element_type=F32)
                lse_j = jnp.sum(jnp.where(first, ld[rows, :], 0.0), axis=-1, keepdims=True)
                dd_j = jnp.sum(jnp.where(first, ddd[rows, :], 0.0), axis=-1, keepdims=True)
                p = jnp.exp(s - lse_j)
                ds = p * (dp + dd_j)
                dsacc[j] += ds
                pbuf[j, c] = p.astype(BF16)
                dsbuf[j, c] = ds.astype(BF16)
        for c in range(nchunk):
            rows = slice(c * BLOCK, (c + 1) * BLOCK)
            has_next = c + 1 < nchunk and (c + 1) % nb != 0
            kw = kd[c * BLOCK:(c + 2) * BLOCK, :]
            dq = jnp.zeros((BLOCK, LANES), F32)
            dk = jnp.zeros((BLOCK, LANES), F32)
            dv = jnp.zeros((BLOCK, LANES), F32)
            both = slice(c * BLOCK, (c + 2) * BLOCK) if has_next else rows
            for j in range(2):
                mj = jnp.logical_and(lane >= j * HEAD_DIM, lane < (j + 1) * HEAD_DIM)
                dq = dq + jnp.dot(dsbuf[j, c], jnp.where(mj, kw, jnp.zeros_like(kw)), preferred_element_type=F32)
                dsk = dsbuf[j, c, :, BLOCK:]
                pk = pbuf[j, c, :, BLOCK:]
                if has_next:
                    dsk = jnp.concatenate([dsk, dsbuf[j, c + 1, :, :BLOCK]], axis=0)
                    pk = jnp.concatenate([pk, pbuf[j, c + 1, :, :BLOCK]], axis=0)
                qq = qd[both, :]
                dd = dod[both, :]
                dk = dk + lax.dot_general(dsk, jnp.where(mj, qq, jnp.zeros_like(qq)), contract_rows,
                                          preferred_element_type=F32)
                dv = dv + lax.dot_general(pk, jnp.where(mj, dd, jnp.zeros_like(dd)), contract_rows,
                                          preferred_element_type=F32)
            dqd[rows, :] = dq
            dkd[rows, :] = dk
            dvd[rows, :] = dv

        dsout_ref[...] = dsin_ref[...] + dsacc[...]

        dgq = jnp.zeros((1, LANES), F32)
        dgk = jnp.zeros((1, LANES), F32)
        for c in range(nchunk):
            rows = _chunk_rows(c, d, nb)
            qv = qs[rows, :]
            rq = lax.rsqrt(_head_mean(qv * qv, seg) + EPS)
            qh = qv * rq
            dy = dqd[c * BLOCK:(c + 1) * BLOCK, :]
            dgq = dgq + jnp.sum(dy * qh, axis=0, keepdims=True) * qscale
            dxh = dy * qg
            ddd[rows, :] = rq * (dxh - qh * _head_mean(dxh * qh, seg))
            kv = ks[rows, :]
            rk = lax.rsqrt(_head_mean(kv * kv, seg) + EPS)
            kh = kv * rk
            dy = dkd[c * BLOCK:(c + 1) * BLOCK, :]
            dgk = dgk + jnp.sum(dy * kh, axis=0, keepdims=True)
            dxh = dy * kg
            ld[rows, :] = rk * (dxh - kh * _head_mean(dxh * kh, seg))
        @pl.when(h > 0)
        def _():
            for cp in flush(h - 1):
                cp.wait()

        qst[...] = ddd[...].astype(BF16)
        kst[...] = ld[...].astype(BF16)
        for c in range(nchunk):
            ddd[_chunk_rows(c, d, nb), :] = dvd[c * BLOCK:(c + 1) * BLOCK, :]
        vst[...] = ddd[...].astype(BF16)
        for cp in flush(h):
            cp.start()

        @pl.when(h == hp - 1)
        def _():
            for cp in flush(h):
                cp.wait()

        dgq_ref[0] = dgq
        dgk_ref[0] = dgk

    ucol = lambda c0: pl.BlockSpec((t, LANES), lambda h: (0, c0 + h))
    vec = pl.BlockSpec((1, LANES), lambda h: (0, 0))
    oblk = pl.BlockSpec((t, LANES), lambda h: (0, h))
    bblk = pl.BlockSpec((2, BLOCK, 2 * BLOCK), lambda h: (gi * hp + h, 0, 0))
    gblk = pl.BlockSpec((1, 1, LANES), lambda h: (h, 0, 0))
    gsh = jax.ShapeDtypeStruct((hp, 1, LANES), F32)
    hbm = pl.BlockSpec(memory_space=pl.ANY)
    return pl.pallas_call(
        body, name=name, grid=(hp,),
        in_specs=[ucol(qc0), ucol(kc0), ucol(vc0), oblk, oblk, oblk, vec, vec, bblk, bblk, hbm],
        out_specs=(gblk, gblk, bblk, hbm),
        out_shape=(gsh, gsh, jax.ShapeDtypeStruct(ds_in.shape, F32), jax.ShapeDtypeStruct(du_in.shape, BF16)),
        input_output_aliases={9: 2, 10: 3},
        scratch_shapes=[pltpu.VMEM((t, LANES), BF16), pltpu.VMEM((t + BLOCK, LANES), BF16),
                        pltpu.VMEM((t + BLOCK, LANES), BF16), pltpu.VMEM((t, LANES), BF16),
                        pltpu.VMEM((t, LANES), F32), pltpu.VMEM((t, LANES), F32), pltpu.VMEM((t, LANES), F32),
                        pltpu.VMEM((t, LANES), F32), pltpu.VMEM((t, LANES), F32),
                        pltpu.VMEM((2, BLOCK, 2 * BLOCK), F32), pltpu.VMEM((2, nchunk, BLOCK, 2 * BLOCK), BF16),
                        pltpu.VMEM((2, nchunk, BLOCK, 2 * BLOCK), BF16), pltpu.VMEM((t, LANES), F32),
                        pltpu.VMEM((t, LANES), F32), pltpu.VMEM((t, LANES), BF16), pltpu.VMEM((t, LANES), BF16),
                        pltpu.VMEM((t, LANES), BF16), pltpu.SemaphoreType.DMA((3,))],
        compiler_params=_cparams(("arbitrary",)),
    )(u, u, u, do_g, dd_g, lse_g, qg2, kg2, bias, ds_in, du_in)


def _group_weights(l0, l1, l2):
    mx = jnp.maximum(jnp.maximum(l0, l1), l2)
    e0, e1, e2 = jnp.exp(l0 - mx), jnp.exp(l1 - mx), jnp.exp(l2 - mx)
    inv = 1.0 / (e0 + e1 + e2)
    return e0 * inv, e1 * inv, e2 * inv


def _combine_fwd(os_, lses, name):
    t, ao = os_[0].shape
    ROWS = _rows(t, 20 * ao)

    def body(o0, o1, o2, l0, l1, l2, o_ref):
        w0, w1, w2 = _group_weights(l0[...], l1[...], l2[...])
        o_ref[...] = (w0 * o0[...].astype(F32) + w1 * o1[...].astype(F32) + w2 * o2[...].astype(F32)).astype(BF16)

    row = pl.BlockSpec((ROWS, ao), lambda i: (i, 0))
    return pl.pallas_call(
        body, name=name, grid=(t // ROWS,), in_specs=[row] * 6, out_specs=row,
        out_shape=jax.ShapeDtypeStruct((t, ao), BF16), compiler_params=_cparams(("arbitrary",)),
    )(*os_, *lses)


def _combine_bwd(do, os_, lses, name):
    t, ao = do.shape
    idx = np.arange(ao) // HEAD_DIM
    seg = jnp.asarray((idx[:, None] == idx[None, :]).astype(np.float32), dtype=BF16)
    ROWS = _rows(t, 40 * ao)

    def body(do_ref, o0, o1, o2, l0, l1, l2, seg_ref, g0, g1, g2, d0, d1, d2):
        w0, w1, w2 = _group_weights(l0[...], l1[...], l2[...])
        dov = do_ref[...]
        o = w0 * o0[...].astype(F32) + w1 * o1[...].astype(F32) + w2 * o2[...].astype(F32)
        sd = _segment_sum(dov * o, seg_ref[...])
        for w, gref, dref in ((w0, g0, d0), (w1, g1, d1), (w2, g2, d2)):
            gref[...] = (w * dov).astype(BF16)
            dref[...] = -(w * sd)

    row = pl.BlockSpec((ROWS, ao), lambda i: (i, 0))
    sh = jax.ShapeDtypeStruct((t, ao), F32)
    sh16 = jax.ShapeDtypeStruct((t, ao), BF16)
    outs = pl.pallas_call(
        body, name=name, grid=(t // ROWS,), in_specs=[row] * 7 + [pl.BlockSpec((ao, ao), lambda i: (0, 0))],
        out_specs=(row,) * 6, out_shape=(sh16,) * 3 + (sh,) * 3, compiler_params=_cparams(("arbitrary",)),
    )(do, *os_, *lses, seg)
    return outs[:3], outs[3:]


def _adamw(w, g, m, v, name):
    shape = w.shape
    cols = shape[-1]
    rows = int(np.prod(shape[:-1]))
    tr = rows if rows <= 512 else _tile_rows(rows)
    c1 = 1.0 - ADAM_B1 ** ADAM_STEP
    c2 = 1.0 - ADAM_B2 ** ADAM_STEP

    def body(w_ref, g_ref, m_ref, v_ref, d_ref, nm_ref, nv_ref):
        gv = g_ref[...]
        mn = ADAM_B1 * m_ref[...] + (1.0 - ADAM_B1) * gv
        vn = ADAM_B2 * v_ref[...] + (1.0 - ADAM_B2) * (gv * gv)
        nm_ref[...] = mn
        nv_ref[...] = vn
        d_ref[...] = -ADAM_LR * ((mn / c1) / (jnp.sqrt(vn / c2) + ADAM_EPS) + ADAM_WD * w_ref[...])

    blk = pl.BlockSpec((tr, cols), lambda i: (i, 0))
    sh = jax.ShapeDtypeStruct((rows, cols), F32)
    outs = pl.pallas_call(
        body, name=name, grid=(rows // tr,), in_specs=[blk] * 4, out_specs=(blk,) * 3, out_shape=(sh,) * 3,
        compiler_params=_cparams(("arbitrary",)),
    )(*[a.reshape(rows, cols) for a in (w, g, m, v)])
    return tuple(o.reshape(shape) for o in outs)


def _tile_rows(rows):
    for t in (512, 256, 128, 64, 32, 16, 8):
        if rows % t == 0:
            return t
    return rows


def _sum_slots(recv, parts, me, layers, l, name):
    _, rows, cols = recv.shape
    tr = rows if rows <= 512 else _tile_rows(rows)

    def body(me_ref, r_ref, own_ref, _, o_ref):
        acc = jnp.zeros(o_ref.shape, F32)
        for s in range(N_DEV):
            acc = acc + jnp.where(me_ref[0] == s, own_ref[...], r_ref[s]).astype(F32)
        o_ref[...] = acc

    return pl.pallas_call(
        body, name=name,
        grid_spec=pltpu.PrefetchScalarGridSpec(
            num_scalar_prefetch=1, grid=(rows // tr,),
            in_specs=[pl.BlockSpec((N_DEV, tr, cols), lambda i, me: (0, i, 0)),
                      pl.BlockSpec((None, tr, cols), lambda i, me: (me[0], i, 0)),
                      pl.BlockSpec(memory_space=pl.ANY)],
            out_specs=pl.BlockSpec((None, tr, cols), lambda i, me: (l, i, 0))),
        out_shape=jax.ShapeDtypeStruct(layers.shape, F32), input_output_aliases={3: 0},
        compiler_params=_cparams(("arbitrary",)),
    )(me.reshape(1), recv, parts, layers)


def _peer(k):
    x, y, c = lax.axis_index("x"), lax.axis_index("y"), lax.axis_index("c")
    return (1 - x if k & 4 else x, 1 - y if k & 2 else y, 1 - c if k & 1 else c)


def _dev_index(p):
    return 4 * p[0] + 2 * p[1] + p[2]


HBM_SPEC = pl.BlockSpec(memory_space=pltpu.HBM)
SEM_SPEC = pl.BlockSpec(memory_space=pltpu.SEMAPHORE)
ANY_SPEC = pl.BlockSpec(memory_space=pl.ANY)
CHIPS = (4, 2, 6)


def _remote(src, dst, send_sem, recv_sem, to):
    return pltpu.make_async_remote_copy(src_ref=src, dst_ref=dst, send_sem=send_sem, recv_sem=recv_sem,
                                        device_id=to, device_id_type=MESH)


def _hbm(a):
    return pltpu.with_memory_space_constraint(a, pltpu.HBM)


def _split_call(body, name, bufs, sems_in, sem_out_sizes, after):
    nb, ns, no = len(bufs), len(sems_in), len(sem_out_sizes)
    extra = [] if after is None else list(after) if isinstance(after, (tuple, list)) else [after]

    def kern(*refs):
        pos = nb + ns + len(extra)
        body(refs[:nb], refs[nb:nb + ns], refs[pos:pos + no])
        token_ref = refs[pos + no + nb]
        token_ref[...] = jnp.zeros_like(token_ref)

    out_shape = (tuple(pltpu.SemaphoreType.DMA((s,)) for s in sem_out_sizes)
                 + tuple(pltpu.HBM(b.shape, b.dtype) for b in bufs) + (jax.ShapeDtypeStruct((8, LANES), F32),))
    res = pl.pallas_call(
        kern, name=name, out_shape=out_shape,
        in_specs=[HBM_SPEC] * nb + [SEM_SPEC] * ns + [ANY_SPEC] * len(extra),
        out_specs=(SEM_SPEC,) * no + (HBM_SPEC,) * nb + (pl.BlockSpec(memory_space=pltpu.VMEM),),
        input_output_aliases={i: no + i for i in range(nb)},
        compiler_params=pltpu.CompilerParams(has_side_effects=pltpu.SideEffectType.DATAFLOW_SIDE_EFFECTING),
    )(*bufs, *sems_in, *extra)
    return res[:no], res[no:no + nb], res[no + nb]


def _gather_start(shards, lands, after, name):
    n = len(shards)

    def body(bufs, _, sems):
        ins, lnd = bufs[:n], bufs[n:]
        d2d_s, d2d_r, ici_s, ici_r = sems
        me = _dev_index(_peer(0))
        for j, k in enumerate(CHIPS):
            for i in range(n):
                _remote(ins[i], lnd[i].at[me], ici_s.at[j], ici_r.at[j], _peer(k)).start()
        for i in range(n):
            _remote(ins[i], lnd[i].at[me], d2d_s.at[0], d2d_r.at[0], _peer(1)).start()

    return _split_call(body, name, [_hbm(a) for a in (*shards, *lands)], [], (1, 1, 3, 3), after)


def _gather_forward(n, bufs, ici_r, after, name):
    def body(refs, sems_in, sems):
        ins, lnd = refs[:n], refs[n:]
        (arrived,) = sems_in
        fwd_s, fwd_r = sems
        for j, k in enumerate(CHIPS):
            blk = _dev_index(_peer(k))
            for i in range(n):
                _remote(ins[i], lnd[i].at[blk], fwd_s.at[j], arrived.at[j], _peer(k)).wait_recv()
            for i in range(n):
                _remote(lnd[i].at[blk], lnd[i].at[blk], fwd_s.at[j], fwd_r.at[j], _peer(1)).start()

    return _split_call(body, name, bufs, [ici_r], (3, 3), after)


def _gather_finish(n, bufs, d2d_s, d2d_r, ici_s, fwd_s, fwd_r, after, name):
    def body(refs, sems_in, _):
        ins, lnd = refs[:n], refs[n:]
        d2d_send, d2d_recv, ici_send, fwd_send, fwd_recv = sems_in
        sib = _peer(1)
        for i in range(n):
            cp = _remote(ins[i], lnd[i].at[_dev_index(sib)], d2d_send.at[0], d2d_recv.at[0], sib)
            cp.wait_send()
            cp.wait_recv()
        for j, k in enumerate(CHIPS):
            passed = _dev_index(_peer(k))
            landed = _dev_index(_peer(k | 1))
            for i in range(n):
                _remote(ins[i], lnd[i].at[passed], ici_send.at[j], fwd_recv.at[j], _peer(k)).wait_send()
                cp = _remote(lnd[i].at[passed], lnd[i].at[landed], fwd_send.at[j], fwd_recv.at[j], sib)
                cp.wait_send()
                cp.wait_recv()

    _, out, token = _split_call(body, name, bufs, [d2d_s, d2d_r, ici_s, fwd_s, fwd_r], (), after)
    return out[n:], token


def _exchange_start(parts, lands, after, name):
    n = len(parts)

    def body(bufs, _, sems):
        src, lnd = bufs[:n], bufs[n:]
        send, recv = sems
        me = _dev_index(_peer(0))
        for k in (4, 5, 2, 3, 6, 7, 1):
            to = _peer(k)
            for i in range(n):
                _remote(src[i].at[_dev_index(to)], lnd[i].at[me], send.at[k - 1], recv.at[k - 1], to).start()

    return _split_call(body, name, [_hbm(a) for a in (*parts, *lands)], [], (7, 7), after)


def _exchange_finish(n, bufs, send, recv, after, name):
    def body(refs, sems_in, _):
        src, lnd = refs[:n], refs[n:]
        send_, recv_ = sems_in
        me = _dev_index(_peer(0))
        for k in range(1, N_DEV):
            frm = _peer(k)
            for i in range(n):
                cp = _remote(src[i].at[me], lnd[i].at[_dev_index(frm)], send_.at[k - 1], recv_.at[k - 1], frm)
                cp.wait_send()
                cp.wait_recv()

    _, out, token = _split_call(body, name, bufs, [send, recv], (), after)
    return out[:n], out[n:], token


def _all_reduce_small(v, after, name):
    rows = v.shape[0]

    def body(v_ref, _, o_ref, buf, send_sems, recv_sems):
        me = _dev_index(_peer(0))
        buf[me] = v_ref[...]
        copies = []
        for k in range(1, N_DEV):
            copies.append(pltpu.make_async_remote_copy(
                src_ref=v_ref, dst_ref=buf.at[me], send_sem=send_sems.at[k - 1], recv_sem=recv_sems.at[k - 1],
                device_id=_peer(k), device_id_type=MESH))
        for cp in copies:
            cp.start()
        for k in range(1, N_DEV):
            pltpu.make_async_remote_copy(
                src_ref=v_ref, dst_ref=buf.at[_dev_index(_peer(k))], send_sem=send_sems.at[k - 1],
                recv_sem=recv_sems.at[k - 1], device_id=_peer(k), device_id_type=MESH).wait_recv()
        for cp in copies:
            cp.wait_send()
        acc = buf[0]
        for s in range(1, N_DEV):
            acc = acc + buf[s]
        o_ref[...] = acc

    vm = pl.BlockSpec(memory_space=pltpu.VMEM)
    return pl.pallas_call(
        body, name=name, in_specs=[vm, pl.BlockSpec(memory_space=pl.ANY)], out_specs=vm,
        out_shape=jax.ShapeDtypeStruct(v.shape, F32),
        scratch_shapes=[pltpu.VMEM((N_DEV, rows, LANES), F32), pltpu.SemaphoreType.DMA((7,)),
                        pltpu.SemaphoreType.DMA((7,))],
    )(v, after)


def _columns(cdim, ao):
    q_col = 2 * cdim
    attn_dim = N_GROUPS * ao
    return (q_col, q_col + attn_dim, q_col + 2 * attn_dim), q_col + 3 * attn_dim


def _layer_fwd(x, h1, sm, bg, get_rest, bias, hpg, next_gain, at_ff1=None):
    cdim = sm["conv_ln_g"].shape[0]
    ao = hpg * HEAD_DIM
    cols, gate_col = _columns(cdim, ao)
    qg2 = jnp.tile(sm["q_norm_g"], 2).reshape(1, LANES)
    kg2 = jnp.tile(sm["k_norm_g"], 2).reshape(1, LANES)
    u = _mm_in_pairs(h1, bg["w_in"], "mm_in")
    zc = _conv_fwd(u, bg["conv_dw_w"], sm["conv_dw_b"], cdim, "conv_fwd")
    zs = _ln_swish_fwd(zc, sm["conv_ln_g"], sm["conv_ln_b"], "ln_swish_fwd")
    os_, lses = [], []
    for gi in range(N_GROUPS):
        o_g, lse_g = _attn_fwd(u, qg2, kg2, bias, gi, cols, hpg, "attn_fwd_g%d" % gi)
        os_.append(o_g)
        lses.append(lse_g)
    o = _combine_fwd(os_, lses, "combine_fwd")
    bg = {**bg, **get_rest(o)}
    mg, yc, ya = _gated_out(zs, o, bg["w_conv_out"], bg["w_attn_out"], u, gate_col, "gated_out")
    x1, h2 = _mm(mg, bg["w_out"], epi="res_rms", extra=x, gain=sm["norm2_g"], name="mm_out")
    f = _mm(h2, bg["w_ff1"], out_dtype=BF16, name="mm_ff1")
    after = at_ff1(f) if at_ff1 is not None else None
    if next_gain is None:
        x2, h_next = _mm(f, bg["w_ff2"], a_relu2=True, epi="res", extra=x1, after=after, name="mm_ff2"), None
    else:
        x2, h_next = _mm(f, bg["w_ff2"], a_relu2=True, epi="res_rms", extra=x1, gain=next_gain, after=after,
                         name="mm_ff2")
    saved = dict(x=x, h1=h1, u=u, zc=zc, zs=zs, yc=yc, os=os_, lses=lses, o=o, ya=ya, mg=mg, x1=x1, h2=h2, f=f,
                 qg2=qg2, kg2=kg2)
    return x2, h_next, saved, bg


GRAD_GROUPS = (("w_ff2", "w_ff1"), ("w_out", "w_conv_out", "w_attn_out", "conv_dw_w"), ("w_in",))


def _layer_bwd(dx, s, sm, bg, bias, ds_sum, after, emit):
    cdim = sm["conv_ln_g"].shape[0]
    ao = bg["w_attn_out"].shape[0]
    hpg = ao // HEAD_DIM
    cols, gate_col = _columns(cdim, ao)
    g = {}
    df = _mm(dx, bg["w_ff2"], tb=True, epi="drelu2", extra=s["f"], out_dtype=BF16, after=after, name="mm_dff2")
    g["w_ff2"] = _mm(s["f"], dx, ta=True, a_relu2=True, out_dtype=BF16, name="mm_gw_ff2")
    g["w_ff1"] = _mm(s["h2"], df, ta=True, out_dtype=BF16, out_slots=True, name="mm_gw_ff1")
    after = emit(GRAD_GROUPS[0], g)
    dx1, dg2, dx1h = _mm(df, bg["w_ff1"], tb=True, epi="rms_bwd", extra=(s["x1"], dx), gain=sm["norm2_g"], after=after,
                   name="mm_dff1")
    g["norm2_g"] = dg2[0]
    dmg = _mm(dx1h, bg["w_out"], tb=True, name="mm_dout")
    g["w_out"] = _mm(s["mg"], dx1h, ta=True, out_dtype=BF16, name="mm_gw_out")
    dyc, dya, dugc, duga = _gate_bwd(dmg, s["u"], s["yc"], s["ya"], gate_col, "gate_bwd")
    dzs = _mm(dyc, bg["w_conv_out"], tb=True, name="mm_dconv_out")
    g["w_conv_out"] = _mm(s["zs"], dyc, ta=True, out_dtype=BF16, name="mm_gw_conv_out")
    do = _mm(dya, bg["w_attn_out"], tb=True, name="mm_dattn_out")
    g["w_attn_out"] = _mm(s["o"], dya, ta=True, out_dtype=BF16, name="mm_gw_attn_out")
    dzc, dlg, dlb = _ln_swish_bwd(dzs, s["zc"], sm["conv_ln_g"], sm["conv_ln_b"], "ln_swish_bwd")
    g["conv_ln_g"] = dlg[0]
    g["conv_ln_b"] = dlb[0]
    da, dgt, dcw, dcb = _conv_bwd(dzc, s["u"], bg["conv_dw_w"], cdim, "conv_bwd")
    g["conv_dw_w"] = dcw[:CONV_WIDTH].astype(BF16)
    g["conv_dw_b"] = dcb[0]
    after = emit(GRAD_GROUPS[1], g)
    do_gs, dd_gs = _combine_bwd(do, s["os"], s["lses"], "combine_bwd")
    du = lax.empty(s["u"].shape, BF16)
    for col, piece in ((0, da), (cdim, dgt), (gate_col, dugc), (gate_col + dugc.shape[1], duga)):
        du = lax.dynamic_update_slice(du, piece, (0, col))
    gqs, gks = [], []
    for gi in range(N_GROUPS):
        gq, gk, ds_sum, du = _attn_bwd(s["u"], do_gs[gi], dd_gs[gi], s["lses"][gi], s["qg2"], s["kg2"], bias,
                                       ds_sum, du, gi, cols, hpg, "attn_bwd_g%d" % gi)
        gqs.append(gq)
        gks.append(gk)
    g["q_norm_g"] = jnp.concatenate(gqs)
    g["k_norm_g"] = jnp.concatenate(gks)
    g["w_in"] = _mm_gw_in_pairs(s["h1"], du, after, "mm_gw_in")
    after = emit(GRAD_GROUPS[2], g)
    dx0, dg1 = _mm_din_pairs(du, bg["w_in"], s["x"], sm["norm1_g"], dx1, after, "mm_din")
    g["norm1_g"] = dg1[0]
    return dx0, g, ds_sum


BIG = ("w_in", "conv_dw_w", "w_conv_out", "w_attn_out", "w_out", "w_ff1", "w_ff2")
COL_SHARDED = ("w_in", "conv_dw_w", "w_conv_out", "w_attn_out", "w_ff1")
SMALL = ("rel_bias", "norm1_g", "q_norm_g", "k_norm_g", "conv_dw_b", "conv_ln_g", "conv_ln_b", "norm2_g")
WEIGHTS = ("rel_bias", "norm1_g", "w_in", "q_norm_g", "k_norm_g", "conv_dw_w", "conv_dw_b", "conv_ln_g", "conv_ln_b",
           "w_conv_out", "w_attn_out", "w_out", "norm2_g", "w_ff1", "w_ff2")


def _to_whole(name, gathered):
    n, a, b = gathered.shape
    if name in COL_SHARDED:
        return gathered.transpose(1, 0, 2).reshape(a, n * b)
    return gathered.reshape(n * a, b)


def _to_slots(name, whole):
    a, b = whole.shape
    if name in COL_SHARDED:
        return whole.reshape(a, N_DEV, b // N_DEV).transpose(1, 0, 2)
    return whole.reshape(N_DEV, a // N_DEV, b)


def _own_slot(block, me):
    land = lax.empty((N_DEV,) + block.shape, block.dtype)
    return lax.dynamic_update_slice(land, block[None], (me,) + (0,) * block.ndim)


def kernel(x, rel_bias, norm1_g, w_in, q_norm_g, k_norm_g, conv_dw_w, conv_dw_b, conv_ln_g, conv_ln_b, w_conv_out, w_attn_out, w_out, norm2_g, w_ff1, w_ff2, loss_target, m_rel_bias, m_norm1_g, m_w_in, m_q_norm_g, m_k_norm_g, m_conv_dw_w, m_conv_dw_b, m_conv_ln_g, m_conv_ln_b, m_w_conv_out, m_w_attn_out, m_w_out, m_norm2_g, m_w_ff1, m_w_ff2, v_rel_bias, v_norm1_g, v_w_in, v_q_norm_g, v_k_norm_g, v_conv_dw_w, v_conv_dw_b, v_conv_ln_g, v_conv_ln_b, v_w_conv_out, v_w_attn_out, v_w_out, v_norm2_g, v_w_ff1, v_w_ff2):
    w = dict(rel_bias=rel_bias, norm1_g=norm1_g, w_in=w_in, q_norm_g=q_norm_g, k_norm_g=k_norm_g, conv_dw_w=conv_dw_w,
             conv_dw_b=conv_dw_b, conv_ln_g=conv_ln_g, conv_ln_b=conv_ln_b, w_conv_out=w_conv_out,
             w_attn_out=w_attn_out, w_out=w_out, norm2_g=norm2_g, w_ff1=w_ff1, w_ff2=w_ff2)
    mom = dict(rel_bias=m_rel_bias, norm1_g=m_norm1_g, w_in=m_w_in, q_norm_g=m_q_norm_g, k_norm_g=m_k_norm_g,
               conv_dw_w=m_conv_dw_w, conv_dw_b=m_conv_dw_b, conv_ln_g=m_conv_ln_g, conv_ln_b=m_conv_ln_b,
               w_conv_out=m_w_conv_out, w_attn_out=m_w_attn_out, w_out=m_w_out, norm2_g=m_norm2_g, w_ff1=m_w_ff1,
               w_ff2=m_w_ff2)
    var = dict(rel_bias=v_rel_bias, norm1_g=v_norm1_g, w_in=v_w_in, q_norm_g=v_q_norm_g, k_norm_g=v_k_norm_g,
               conv_dw_w=v_conv_dw_w, conv_dw_b=v_conv_dw_b, conv_ln_g=v_conv_ln_g, conv_ln_b=v_conv_ln_b,
               w_conv_out=v_w_conv_out, w_attn_out=v_w_attn_out, w_out=v_w_out, norm2_g=v_norm2_g, w_ff1=v_w_ff1,
               w_ff2=v_w_ff2)

    depth = norm1_g.shape[0]
    me = 4 * lax.axis_index("x") + 2 * lax.axis_index("y") + lax.axis_index("c")
    odd_core = lax.axis_index("c") == 1
    hpg = w_attn_out.shape[1] // HEAD_DIM
    buckets = jnp.asarray(_bucket_table())

    first_names = ("w_in", "conv_dw_w")
    rest_names = tuple(k for k in BIG if k not in first_names)

    def chain_start(l, names, after):
        shards = [w[k][l] if k == "conv_dw_w" else w[k][l].astype(BF16) for k in names]
        if "w_in" in names:
            i = names.index("w_in")
            shards[i] = jnp.where(odd_core, jnp.pad(shards[i], ((0, 0), (SHIFT, 0))),
                                  jnp.pad(shards[i], ((0, 0), (0, SHIFT))))
        sems, bufs, token = _gather_start(shards, [_own_slot(s, me) for s in shards], after,
                                          "gather_start_%s_l%d" % (names[0], l))
        return dict(l=l, names=names, sems=sems, bufs=bufs, token=token)

    def chain_forward(ch, after):
        fwd, bufs, token = _gather_forward(len(ch["names"]), ch["bufs"], ch["sems"][3], after,
                                           "gather_forward_%s_l%d" % (ch["names"][0], ch["l"]))
        ch.update(fwd=fwd, bufs=bufs)
        return token

    def chain_finish(ch, after):
        d2d_s, d2d_r, ici_s, _ = ch["sems"]
        gathered, _ = _gather_finish(len(ch["names"]), ch["bufs"], d2d_s, d2d_r, ici_s, ch["fwd"][0], ch["fwd"][1],
                                     after, "gather_finish_%s_l%d" % (ch["names"][0], ch["l"]))
        out = {k: a if k == "w_in" else _to_whole(k, a) for k, a in zip(ch["names"], gathered)}
        if "conv_dw_w" in out:
            out["conv_dw_w"] = jnp.pad(out["conv_dw_w"], ((0, CONV_TAPS_PADDED - CONV_WIDTH), (0, 0)))
        return out

    xs = x[0]
    first = chain_start(0, first_names, None)
    h1 = _rms_fwd(xs, norm1_g[0], first["token"], "rms1_fwd")
    bias = _bias_expand(rel_bias, buckets, hpg, h1, "bias_expand")
    saved, bigs, smalls = [], [], []
    chains = {}
    for l in range(depth):
        sm = {k: w[k][l] for k in SMALL if k != "rel_bias"}
        if l == 0:
            token = chain_forward(first, bias)
            rest = chain_start(0, rest_names, token)
            bg = chain_finish(first, rest["token"])

            def get_rest(o, rest=rest):
                token = chain_forward(rest, o)
                if depth > 1:
                    chains[1] = (chain_start(1, first_names, token),)
                    chains[1] += (chain_start(1, rest_names, chains[1][0]["token"]),)
                    token = chains[1][1]["token"]
                return chain_finish(rest, token)
        elif l == 1:
            first, rest = chains[1]
            token = chain_forward(first, xs)
            if depth > 2:
                chains[2] = chain_start(2, BIG, token)
                token = chains[2]["token"]
            bg = chain_finish(first, token)

            def get_rest(o, rest=rest):
                return chain_finish(rest, chain_forward(rest, o))
        else:
            token = xs if "fwd" in chains[l] else chain_forward(chains[l], xs)
            if l + 1 < depth:
                chains[l + 1] = chain_start(l + 1, BIG, token)
                token = chains[l + 1]["token"]
            whole = chain_finish(chains[l], token)
            bg = {k: whole[k] for k in first_names}

            def get_rest(o, whole=whole):
                return {k: whole[k] for k in rest_names}
        at_ff1 = (lambda f, nxt=l + 1: chain_forward(chains[nxt], f)) if 2 <= l < depth - 1 else None
        xs, h1, sv, bg = _layer_fwd(xs, h1, sm, bg, get_rest, bias, hpg, norm1_g[l + 1] if l + 1 < depth else None,
                                    at_ff1)
        saved.append(sv)
        bigs.append(bg)
        smalls.append(sm)

    loss, dx = _loss_and_grad(xs, loss_target[0], "loss")

    ds_sum = jnp.zeros((N_GROUPS * hpg, BLOCK, 2 * BLOCK), F32)
    g = {k: [None] * depth for k in SMALL if k != "rel_bias"}
    sums = {k: lax.empty((depth, int(np.prod(w[k].shape[1:-1])), w[k].shape[-1] + (SHIFT if k == "w_in" else 0)), F32)
            for k in BIG}
    pending = []

    def finish_oldest(after):
        names, l, (send, recv), bufs = pending.pop(0)
        parts, recvd, token = _exchange_finish(len(names), bufs, send, recv, after,
                                               "exchange_finish_%s_l%d" % (names[0], l))
        for k, r, p in zip(names, recvd, parts):
            three = (N_DEV, -1, r.shape[-1])
            sums[k] = _sum_slots(r.reshape(three), p.reshape(three), me, sums[k], l, "sum_" + k)
        return token

    def make_emit(l):
        def emit(names, gl):
            parts = [gl[k] if k in ("w_ff1", "w_in") else _to_slots(k, gl[k]) for k in names]
            token = finish_oldest(parts[0]) if len(pending) >= len(GRAD_GROUPS) else None
            lands = [lax.empty(p.shape, p.dtype) for p in parts]
            sems, bufs, token = _exchange_start(parts, lands, token, "exchange_start_%s_l%d" % (names[0], l))
            pending.append((names, l, sems, bufs))
            return token
        return emit

    token = None
    for l in reversed(range(depth)):
        dx, gl, ds_sum = _layer_bwd(dx, saved[l], smalls[l], bigs[l], bias, ds_sum, token, make_emit(l))
        for k in g:
            g[k][l] = gl[k]
        token = None
    grad_x = dx

    g = {k: jnp.stack(v) for k, v in g.items()}
    for k in ("q_norm_g", "k_norm_g"):
        g[k] = jnp.sum(g[k].reshape(depth, -1, HEAD_DIM), axis=1)
    db = _bias_reduce(ds_sum, buckets, hpg, "bias_reduce")
    g["rel_bias"] = db[:, 0, :NUM_BUCKETS].T

    flat = jnp.concatenate([g[k].reshape(-1) for k in SMALL])
    nflat = flat.shape[0]
    rows = -(-nflat // (8 * LANES)) * 8
    packed = jnp.pad(flat, (0, rows * LANES - nflat)).reshape(rows, LANES)
    grad, outs = {}, {}
    token = dx
    while pending:
        names = pending[0][0]
        finish_oldest(token)
        for k in names:
            total = sums[k]
            if k == "w_in":
                total = jnp.where(odd_core, total[..., SHIFT:], total[..., :w_in.shape[-1]])
            grad[k] = total.reshape(w[k].shape)
            outs[k] = _adamw(w[k], grad[k], mom[k], var[k], "adamw_" + k)
        token = tuple(outs[k][0] for k in names)
    total = _all_reduce_small(packed, token[-1], "reduce_small").reshape(-1)
    off = 0
    for k in SMALL:
        size = int(np.prod(w[k].shape))
        grad[k] = total[off:off + size].reshape(w[k].shape)
        outs[k] = _adamw(w[k], grad[k], mom[k], var[k], "adamw_" + k)
        off += size
    loss = lax.psum(loss[0, 0], ("x", "y", "c"))
    return (loss, grad_x[None], *[grad[k] for k in WEIGHTS], *[outs[k][0] for k in WEIGHTS],
            *[outs[k][1] for k in WEIGHTS], *[outs[k][2] for k in WEIGHTS])
```
